```python
import math
import jax, jax.numpy as jnp
from jax import lax
import numpy as np

D_MODEL = 1024
BATCH = 8
SEQ = 8192
DEPTH = 1

MEM_LEN = 256
EPS = 1e-6
CONV_WIDTH = 4
SSD_EXPAND = 2
SSD_WIDTH = SSD_EXPAND * D_MODEL
SSD_HEAD_DIM = 64
SSD_HEADS = SSD_WIDTH // SSD_HEAD_DIM
SSD_GROUPS = 4
SSD_STATE = 128
SSD_CHUNK = 128
SSD_CONV_CH = SSD_WIDTH + 2 * SSD_GROUPS * SSD_STATE
LRU_WIDTH = 3 * D_MODEL // 2
LRU_BLOCKS = 16
LRU_BLOCK = LRU_WIDTH // LRU_BLOCKS
LRU_C = 8.0
MEM_HEADS = 4
MEM_HEAD_DIM = D_MODEL // MEM_HEADS
N_BRANCH = 3
SPLIT_POINTS = (
    SSD_WIDTH,
    SSD_WIDTH + SSD_CONV_CH,
    SSD_WIDTH + SSD_CONV_CH + SSD_HEADS,
    SSD_WIDTH + SSD_CONV_CH + SSD_HEADS + LRU_WIDTH,
    SSD_WIDTH + SSD_CONV_CH + SSD_HEADS + 2 * LRU_WIDTH,
    SSD_WIDTH + SSD_CONV_CH + SSD_HEADS + 2 * LRU_WIDTH + D_MODEL,
)
IN_WIDTH = SSD_WIDTH + SSD_CONV_CH + SSD_HEADS + 2 * LRU_WIDTH + D_MODEL + N_BRANCH * D_MODEL

kernel_name = "hybrid_ssd_rglru_memxattn_gated_block"


def rms_norm(x, g):
    xf = x.astype(jnp.float32)
    y = xf * lax.rsqrt(jnp.mean(xf * xf, axis=-1, keepdims=True) + EPS)
    return (y * g.astype(jnp.float32)).astype(x.dtype)


def causal_dwconv(x, w, b):
    k = w.shape[0]
    y = lax.conv_general_dilated(
        x, w[:, None, :].astype(x.dtype), window_strides=(1,), padding=[(k - 1, 0)],
        dimension_numbers=('NWC', 'WIO', 'NWC'), feature_group_count=x.shape[-1])
    return y + b


def ssd_scan(x, dt, a_neg, b_in, c_in):
    f32 = jnp.float32
    bsz, s, h, p = x.shape
    g, n = b_in.shape[2], b_in.shape[3]
    k = h // g
    l = SSD_CHUNK
    nc = s // l
    x = x.astype(f32).reshape(bsz, nc, l, g, k, p)
    dt = dt.astype(f32).reshape(bsz, nc, l, g, k)
    bm = b_in.astype(f32).reshape(bsz, nc, l, g, n)
    cm = c_in.astype(f32).reshape(bsz, nc, l, g, n)
    xdt = x * dt[..., None]
    a_cs = jnp.cumsum(dt * a_neg.astype(f32).reshape(g, k), axis=2)
    causal = jnp.tril(jnp.ones((l, l), dtype=bool))
    seg = a_cs[:, :, :, None] - a_cs[:, :, None, :]
    decay = jnp.exp(jnp.where(causal[:, :, None, None], seg, -jnp.inf))
    cb = jnp.einsum('bclgn,bcsgn->bclsg', cm, bm)
    y_diag = jnp.einsum('bclsgk,bcsgkp->bclgkp', decay * cb[..., None], xdt)
    decay_to_end = jnp.exp(a_cs[:, :, -1:] - a_cs)
    states = jnp.einsum('bclgn,bclgkp->bcgkpn', bm, xdt * decay_to_end[..., None])
    chunk_decay = jnp.exp(a_cs[:, :, -1])

    def step(carry, inp):
        st, dec = inp
        return carry * dec[..., None, None] + st, carry

    init = jnp.zeros((bsz, g, k, p, n), f32)
    _, prev = lax.scan(step, init, (jnp.moveaxis(states, 1, 0), jnp.moveaxis(chunk_decay, 1, 0)))
    prev = jnp.moveaxis(prev, 0, 1)
    y_off = jnp.einsum('bclgn,bcgkpn->bclgkp', cm, prev) * jnp.exp(a_cs)[..., None]
    return (y_diag + y_off).reshape(bsz, s, h, p)


def rg_lru(x, w_a, b_a, w_x, b_x, lam):
    f32 = jnp.float32
    bsz, s, w = x.shape
    xb = x.reshape(bsz, s, LRU_BLOCKS, LRU_BLOCK)
    r = jax.nn.sigmoid(jnp.einsum('bsni,nij->bsnj', xb, w_a) + b_a).reshape(bsz, s, w)
    i = jax.nn.sigmoid(jnp.einsum('bsni,nij->bsnj', xb, w_x) + b_x).reshape(bsz, s, w)
    log_a = (-LRU_C * r.astype(f32)) * jax.nn.softplus(-lam.astype(f32))
    a = jnp.exp(log_a)
    mult = jnp.sqrt(-jnp.expm1(2.0 * log_a))
    u = mult * (i * x).astype(f32)

    def combine(left, right):
        a1, b1 = left
        a2, b2 = right
        return a1 * a2, a2 * b1 + b2

    _, hs = lax.associative_scan(combine, (a, u), axis=1)
    return hs.astype(x.dtype)


def memory_attention(q, mem_n, w_kv):
    bsz, s, _ = q.shape
    m = mem_n.shape[1]
    kv = mem_n @ w_kv
    k, v = jnp.split(kv, 2, axis=-1)
    q = q.reshape(bsz, s, MEM_HEADS, MEM_HEAD_DIM)
    k = k.reshape(bsz, m, MEM_HEADS, MEM_HEAD_DIM)
    v = v.reshape(bsz, m, MEM_HEADS, MEM_HEAD_DIM)
    scores = jnp.einsum('bshd,bmhd->bhsm', q, k).astype(jnp.float32) * (MEM_HEAD_DIM ** -0.5)
    probs = jax.nn.softmax(scores, axis=-1).astype(v.dtype)
    return jnp.einsum('bhsm,bmhd->bshd', probs, v).reshape(bsz, s, D_MODEL)


def _fwd_setup_inputs(seed: int = 0) -> dict:
    key = jax.random.key(seed)
    ks = jax.random.split(key, 24)
    f32 = jnp.float32
    nrm = lambda k, shape, scale: jax.random.normal(k, shape, f32) * scale
    x = jax.random.normal(ks[0], (BATCH, SEQ, D_MODEL), f32)
    mem = jax.random.normal(ks[1], (BATCH, MEM_LEN, D_MODEL), f32)
    norm_g = 1.0 + nrm(ks[2], (DEPTH, D_MODEL), 0.02)
    w_in = nrm(ks[3], (DEPTH, D_MODEL, IN_WIDTH), D_MODEL ** -0.5)
    ssd_conv_w = nrm(ks[4], (DEPTH, CONV_WIDTH, SSD_CONV_CH), CONV_WIDTH ** -0.5)
    ssd_conv_b = nrm(ks[5], (DEPTH, SSD_CONV_CH), 0.02)
    dt0 = jnp.exp(jax.random.uniform(ks[6], (DEPTH, SSD_HEADS), f32, math.log(1e-3), math.log(1e-1)))
    ssd_dt_bias = dt0 + jnp.log(-jnp.expm1(-dt0))
    ssd_a_log = jnp.log(jax.random.uniform(ks[7], (DEPTH, SSD_HEADS), f32, 1.0, 16.0))
    ssd_d = 1.0 + nrm(ks[8], (DEPTH, SSD_HEADS), 0.02)
    ssd_norm_g = 1.0 + nrm(ks[9], (DEPTH, SSD_GROUPS, SSD_WIDTH // SSD_GROUPS), 0.02)
    lru_conv_w = nrm(ks[10], (DEPTH, CONV_WIDTH, LRU_WIDTH), CONV_WIDTH ** -0.5)
    lru_conv_b = nrm(ks[11], (DEPTH, LRU_WIDTH), 0.02)
    lru_w_a = nrm(ks[12], (DEPTH, LRU_BLOCKS, LRU_BLOCK, LRU_BLOCK), LRU_BLOCK ** -0.5)
    lru_b_a = nrm(ks[13], (DEPTH, LRU_BLOCKS, LRU_BLOCK), 0.02)
    lru_w_x = nrm(ks[14], (DEPTH, LRU_BLOCKS, LRU_BLOCK, LRU_BLOCK), LRU_BLOCK ** -0.5)
    lru_b_x = nrm(ks[15], (DEPTH, LRU_BLOCKS, LRU_BLOCK), 0.02)
    a8 = jax.random.uniform(ks[16], (DEPTH, LRU_WIDTH), f32, 0.9, 0.999)
    sig = a8 ** (1.0 / LRU_C)
    lru_lambda = jnp.log(sig) - jnp.log1p(-sig)
    mem_norm_g = 1.0 + nrm(ks[17], (DEPTH, D_MODEL), 0.02)
    w_kv = nrm(ks[18], (DEPTH, D_MODEL, 2 * D_MODEL), D_MODEL ** -0.5)
    w_br_ssd = nrm(ks[19], (DEPTH, SSD_WIDTH, D_MODEL), SSD_WIDTH ** -0.5)
    w_br_lru = nrm(ks[20], (DEPTH, LRU_WIDTH, D_MODEL), LRU_WIDTH ** -0.5)
    w_br_mem = nrm(ks[21], (DEPTH, D_MODEL, D_MODEL), D_MODEL ** -0.5)
    w_out = nrm(ks[22], (DEPTH, D_MODEL, D_MODEL), D_MODEL ** -0.5)
    final_g = 1.0 + nrm(ks[23], (D_MODEL,), 0.02)
    return {"x": x, "mem": mem, "norm_g": norm_g, "w_in": w_in,
            "ssd_conv_w": ssd_conv_w, "ssd_conv_b": ssd_conv_b, "ssd_dt_bias": ssd_dt_bias,
            "ssd_a_log": ssd_a_log, "ssd_d": ssd_d, "ssd_norm_g": ssd_norm_g,
            "lru_conv_w": lru_conv_w, "lru_conv_b": lru_conv_b, "lru_w_a": lru_w_a,
            "lru_b_a": lru_b_a, "lru_w_x": lru_w_x, "lru_b_x": lru_b_x, "lru_lambda": lru_lambda,
            "mem_norm_g": mem_norm_g, "w_kv": w_kv, "w_br_ssd": w_br_ssd, "w_br_lru": w_br_lru,
            "w_br_mem": w_br_mem, "w_out": w_out, "final_g": final_g}


def _fwd_reference(x, mem, norm_g, w_in, ssd_conv_w, ssd_conv_b, ssd_dt_bias, ssd_a_log, ssd_d,
              ssd_norm_g, lru_conv_w, lru_conv_b, lru_w_a, lru_b_a, lru_w_x, lru_b_x, lru_lambda,
              mem_norm_g, w_kv, w_br_ssd, w_br_lru, w_br_mem, w_out, final_g):
    bsz, s, _ = x.shape
    for l in range(DEPTH):
        h = rms_norm(x, norm_g[l])
        proj = h @ w_in[l]
        z, xbc, dt_raw, lru_gate, lru_x, q, gate_logits = jnp.split(proj, SPLIT_POINTS, axis=-1)

        xbc = jax.nn.silu(causal_dwconv(xbc, ssd_conv_w[l], ssd_conv_b[l]))
        xs, bs, cs = jnp.split(xbc, [SSD_WIDTH, SSD_WIDTH + SSD_GROUPS * SSD_STATE], axis=-1)
        xs = xs.reshape(bsz, s, SSD_HEADS, SSD_HEAD_DIM)
        dt = jax.nn.softplus((dt_raw + ssd_dt_bias[l]).astype(jnp.float32))
        y = ssd_scan(xs, dt, -jnp.exp(ssd_a_log[l].astype(jnp.float32)),
                     bs.reshape(bsz, s, SSD_GROUPS, SSD_STATE), cs.reshape(bsz, s, SSD_GROUPS, SSD_STATE))
        y = (y + xs.astype(jnp.float32) * ssd_d[l][:, None].astype(jnp.float32)).astype(x.dtype)
        y = y.reshape(bsz, s, SSD_WIDTH) * jax.nn.silu(z)
        y_ssd = rms_norm(y.reshape(bsz, s, SSD_GROUPS, SSD_WIDTH // SSD_GROUPS),
                         ssd_norm_g[l]).reshape(bsz, s, SSD_WIDTH)

        xl = causal_dwconv(lru_x, lru_conv_w[l], lru_conv_b[l])
        y_lru = rg_lru(xl, lru_w_a[l], lru_b_a[l], lru_w_x[l], lru_b_x[l], lru_lambda[l]) * jax.nn.silu(lru_gate)

        mem_n = rms_norm(mem, mem_norm_g[l])
        y_mem = memory_attention(q, mem_n, w_kv[l])

        gates = jax.nn.sigmoid(gate_logits).reshape(bsz, s, N_BRANCH, D_MODEL)
        merged = (gates[:, :, 0] * (y_ssd @ w_br_ssd[l])
                  + gates[:, :, 1] * (y_lru @ w_br_lru[l])
                  + gates[:, :, 2] * (y_mem @ w_br_mem[l]))
        x = x + merged @ w_out[l]
    return rms_norm(x, final_g)


import jax as _jax
import jax.numpy as _jnp

TWIN_FORMAT = 'train_step'
FWD_PARAMS = ['x', 'mem', 'norm_g', 'w_in', 'ssd_conv_w', 'ssd_conv_b', 'ssd_dt_bias', 'ssd_a_log', 'ssd_d', 'ssd_norm_g', 'lru_conv_w', 'lru_conv_b', 'lru_w_a', 'lru_b_a', 'lru_w_x', 'lru_b_x', 'lru_lambda', 'mem_norm_g', 'w_kv', 'w_br_ssd', 'w_br_lru', 'w_br_mem', 'w_out', 'final_g']
TWIN_WEIGHTS = ['norm_g', 'w_in', 'ssd_conv_w', 'ssd_conv_b', 'ssd_dt_bias', 'ssd_a_log', 'ssd_d', 'ssd_norm_g', 'lru_conv_w', 'lru_conv_b', 'lru_w_a', 'lru_b_a', 'lru_w_x', 'lru_b_x', 'lru_lambda', 'mem_norm_g', 'w_kv', 'w_br_ssd', 'w_br_lru', 'w_br_mem', 'w_out', 'final_g']
TWIN_DIFF_INPUT = 'x'
TWIN_INPUTS = ['x', 'mem', 'norm_g', 'w_in', 'ssd_conv_w', 'ssd_conv_b', 'ssd_dt_bias', 'ssd_a_log', 'ssd_d', 'ssd_norm_g', 'lru_conv_w', 'lru_conv_b', 'lru_w_a', 'lru_b_a', 'lru_w_x', 'lru_b_x', 'lru_lambda', 'mem_norm_g', 'w_kv', 'w_br_ssd', 'w_br_lru', 'w_br_mem', 'w_out', 'final_g', 'loss_target', 'm_norm_g', 'm_w_in', 'm_ssd_conv_w', 'm_ssd_conv_b', 'm_ssd_dt_bias', 'm_ssd_a_log', 'm_ssd_d', 'm_ssd_norm_g', 'm_lru_conv_w', 'm_lru_conv_b', 'm_lru_w_a', 'm_lru_b_a', 'm_lru_w_x', 'm_lru_b_x', 'm_lru_lambda', 'm_mem_norm_g', 'm_w_kv', 'm_w_br_ssd', 'm_w_br_lru', 'm_w_br_mem', 'm_w_out', 'm_final_g', 'v_norm_g', 'v_w_in', 'v_ssd_conv_w', 'v_ssd_conv_b', 'v_ssd_dt_bias', 'v_ssd_a_log', 'v_ssd_d', 'v_ssd_norm_g', 'v_lru_conv_w', 'v_lru_conv_b', 'v_lru_w_a', 'v_lru_b_a', 'v_lru_w_x', 'v_lru_b_x', 'v_lru_lambda', 'v_mem_norm_g', 'v_w_kv', 'v_w_br_ssd', 'v_w_br_lru', 'v_w_br_mem', 'v_w_out', 'v_final_g']
TWIN_OUTPUTS = ['loss', 'grad_x', 'grad_norm_g', 'grad_w_in', 'grad_ssd_conv_w', 'grad_ssd_conv_b', 'grad_ssd_dt_bias', 'grad_ssd_a_log', 'grad_ssd_d', 'grad_ssd_norm_g', 'grad_lru_conv_w', 'grad_lru_conv_b', 'grad_lru_w_a', 'grad_lru_b_a', 'grad_lru_w_x', 'grad_lru_b_x', 'grad_lru_lambda', 'grad_mem_norm_g', 'grad_w_kv', 'grad_w_br_ssd', 'grad_w_br_lru', 'grad_w_br_mem', 'grad_w_out', 'grad_final_g', 'delta_norm_g', 'delta_w_in', 'delta_ssd_conv_w', 'delta_ssd_conv_b', 'delta_ssd_dt_bias', 'delta_ssd_a_log', 'delta_ssd_d', 'delta_ssd_norm_g', 'delta_lru_conv_w', 'delta_lru_conv_b', 'delta_lru_w_a', 'delta_lru_b_a', 'delta_lru_w_x', 'delta_lru_b_x', 'delta_lru_lambda', 'delta_mem_norm_g', 'delta_w_kv', 'delta_w_br_ssd', 'delta_w_br_lru', 'delta_w_br_mem', 'delta_w_out', 'delta_final_g', 'new_m_norm_g', 'new_m_w_in', 'new_m_ssd_conv_w', 'new_m_ssd_conv_b', 'new_m_ssd_dt_bias', 'new_m_ssd_a_log', 'new_m_ssd_d', 'new_m_ssd_norm_g', 'new_m_lru_conv_w', 'new_m_lru_conv_b', 'new_m_lru_w_a', 'new_m_lru_b_a', 'new_m_lru_w_x', 'new_m_lru_b_x', 'new_m_lru_lambda', 'new_m_mem_norm_g', 'new_m_w_kv', 'new_m_w_br_ssd', 'new_m_w_br_lru', 'new_m_w_br_mem', 'new_m_w_out', 'new_m_final_g', 'new_v_norm_g', 'new_v_w_in', 'new_v_ssd_conv_w', 'new_v_ssd_conv_b', 'new_v_ssd_dt_bias', 'new_v_ssd_a_log', 'new_v_ssd_d', 'new_v_ssd_norm_g', 'new_v_lru_conv_w', 'new_v_lru_conv_b', 'new_v_lru_w_a', 'new_v_lru_b_a', 'new_v_lru_w_x', 'new_v_lru_b_x', 'new_v_lru_lambda', 'new_v_mem_norm_g', 'new_v_w_kv', 'new_v_w_br_ssd', 'new_v_w_br_lru', 'new_v_w_br_mem', 'new_v_w_out', 'new_v_final_g']
TWIN_LEAF_KINDS = {'loss': 'loss', 'grad_x': 'grad_x', 'grad_norm_g': 'grad_w', 'grad_w_in': 'grad_w', 'grad_ssd_conv_w': 'grad_w', 'grad_ssd_conv_b': 'grad_w', 'grad_ssd_dt_bias': 'grad_w', 'grad_ssd_a_log': 'grad_w', 'grad_ssd_d': 'grad_w', 'grad_ssd_norm_g': 'grad_w', 'grad_lru_conv_w': 'grad_w', 'grad_lru_conv_b': 'grad_w', 'grad_lru_w_a': 'grad_w', 'grad_lru_b_a': 'grad_w', 'grad_lru_w_x': 'grad_w', 'grad_lru_b_x': 'grad_w', 'grad_lru_lambda': 'grad_w', 'grad_mem_norm_g': 'grad_w', 'grad_w_kv': 'grad_w', 'grad_w_br_ssd': 'grad_w', 'grad_w_br_lru': 'grad_w', 'grad_w_br_mem': 'grad_w', 'grad_w_out': 'grad_w', 'grad_final_g': 'grad_w', 'delta_norm_g': 'delta_w', 'delta_w_in': 'delta_w', 'delta_ssd_conv_w': 'delta_w', 'delta_ssd_conv_b': 'delta_w', 'delta_ssd_dt_bias': 'delta_w', 'delta_ssd_a_log': 'delta_w', 'delta_ssd_d': 'delta_w', 'delta_ssd_norm_g': 'delta_w', 'delta_lru_conv_w': 'delta_w', 'delta_lru_conv_b': 'delta_w', 'delta_lru_w_a': 'delta_w', 'delta_lru_b_a': 'delta_w', 'delta_lru_w_x': 'delta_w', 'delta_lru_b_x': 'delta_w', 'delta_lru_lambda': 'delta_w', 'delta_mem_norm_g': 'delta_w', 'delta_w_kv': 'delta_w', 'delta_w_br_ssd': 'delta_w', 'delta_w_br_lru': 'delta_w', 'delta_w_br_mem': 'delta_w', 'delta_w_out': 'delta_w', 'delta_final_g': 'delta_w', 'new_m_norm_g': 'new_m', 'new_m_w_in': 'new_m', 'new_m_ssd_conv_w': 'new_m', 'new_m_ssd_conv_b': 'new_m', 'new_m_ssd_dt_bias': 'new_m', 'new_m_ssd_a_log': 'new_m', 'new_m_ssd_d': 'new_m', 'new_m_ssd_norm_g': 'new_m', 'new_m_lru_conv_w': 'new_m', 'new_m_lru_conv_b': 'new_m', 'new_m_lru_w_a': 'new_m', 'new_m_lru_b_a': 'new_m', 'new_m_lru_w_x': 'new_m', 'new_m_lru_b_x': 'new_m', 'new_m_lru_lambda': 'new_m', 'new_m_mem_norm_g': 'new_m', 'new_m_w_kv': 'new_m', 'new_m_w_br_ssd': 'new_m', 'new_m_w_br_lru': 'new_m', 'new_m_w_br_mem': 'new_m', 'new_m_w_out': 'new_m', 'new_m_final_g': 'new_m', 'new_v_norm_g': 'new_v', 'new_v_w_in': 'new_v', 'new_v_ssd_conv_w': 'new_v', 'new_v_ssd_conv_b': 'new_v', 'new_v_ssd_dt_bias': 'new_v', 'new_v_ssd_a_log': 'new_v', 'new_v_ssd_d': 'new_v', 'new_v_ssd_norm_g': 'new_v', 'new_v_lru_conv_w': 'new_v', 'new_v_lru_conv_b': 'new_v', 'new_v_lru_w_a': 'new_v', 'new_v_lru_b_a': 'new_v', 'new_v_lru_w_x': 'new_v', 'new_v_lru_b_x': 'new_v', 'new_v_lru_lambda': 'new_v', 'new_v_mem_norm_g': 'new_v', 'new_v_w_kv': 'new_v', 'new_v_w_br_ssd': 'new_v', 'new_v_w_br_lru': 'new_v', 'new_v_w_br_mem': 'new_v', 'new_v_w_out': 'new_v', 'new_v_final_g': 'new_v'}


def _forward(args):
    return _fwd_reference(*[args[k] for k in FWD_PARAMS])


def _output_shape():
    def fwd():
        inp = _fwd_setup_inputs(0)
        return _fwd_reference(*[inp[k] for k in FWD_PARAMS])
    out = _jax.eval_shape(fwd)
    return out.shape, out.dtype

N_MICROBATCH = 1
ADAM_LR = 0.001
ADAM_B1 = 0.9
ADAM_B2 = 0.999
ADAM_EPS = 1e-08
ADAM_WD = 0.01
ADAM_STEP = 10
PER_EXAMPLE_BATCH_AXIS = {'x': 0, 'mem': 0, 'loss_target': 0}
SHARED_INPUTS = []
_WEIGHT_DTYPES = {'norm_g': _jnp.float32, 'w_in': _jnp.float32, 'ssd_conv_w': _jnp.float32, 'ssd_conv_b': _jnp.float32, 'ssd_dt_bias': _jnp.float32, 'ssd_a_log': _jnp.float32, 'ssd_d': _jnp.float32, 'ssd_norm_g': _jnp.float32, 'lru_conv_w': _jnp.float32, 'lru_conv_b': _jnp.float32, 'lru_w_a': _jnp.float32, 'lru_b_a': _jnp.float32, 'lru_w_x': _jnp.float32, 'lru_b_x': _jnp.float32, 'lru_lambda': _jnp.float32, 'mem_norm_g': _jnp.float32, 'w_kv': _jnp.float32, 'w_br_ssd': _jnp.float32, 'w_br_lru': _jnp.float32, 'w_br_mem': _jnp.float32, 'w_out': _jnp.float32, 'final_g': _jnp.float32}
MOMENT_SCALE = {'norm_g': 1.981857e-01, 'w_in': 5.546567e-02, 'ssd_conv_w': 7.170866e-02, 'ssd_conv_b': 9.815045e-02, 'ssd_dt_bias': 1.952226e-01, 'ssd_a_log': 1.811804e-01, 'ssd_d': 4.803757e-01, 'ssd_norm_g': 8.543680e-02, 'lru_conv_w': 3.685614e-02, 'lru_conv_b': 4.328370e-01, 'lru_w_a': 1.406413e-02, 'lru_b_a': 1.175835e-02, 'lru_w_x': 2.530043e-02, 'lru_b_x': 1.237438e-02, 'lru_lambda': 2.133207e-02, 'mem_norm_g': 1.918850e-02, 'w_kv': 1.242564e-02, 'w_br_ssd': 1.175430e-01, 'w_br_lru': 5.054199e-02, 'w_br_mem': 1.260346e-02, 'w_out': 1.283351e-01, 'final_g': 6.397911e+01}


def _to_microbatches(a, axis):
    t = _jnp.moveaxis(a, axis, 0)
    t = t.reshape((N_MICROBATCH, t.shape[0] // N_MICROBATCH) + t.shape[1:])
    return _jnp.moveaxis(t, 1, axis + 1)


def setup_inputs(seed: int = 0) -> dict:
    inp = _fwd_setup_inputs(seed)
    key = _jax.random.fold_in(_jax.random.key(seed), 7919)
    shape, _ = _output_shape()
    out = dict(inp)
    out["loss_target"] = _jax.random.normal(_jax.random.fold_in(key, 0), shape, _jnp.float32)
    for i, name in enumerate(TWIN_WEIGHTS):
        w = inp[name].astype(_jnp.float32)
        if MOMENT_SCALE is None:
            s = _jnp.sqrt(_jnp.mean(_jnp.square(w)) + 1e-30)
        else:
            s = MOMENT_SCALE[name]
        km, kv = _jax.random.split(_jax.random.fold_in(key, i + 1))
        out[name] = w
        out["m_" + name] = s * _jax.random.normal(km, w.shape, _jnp.float32)
        out["v_" + name] = (s * s) * _jax.random.uniform(kv, w.shape, _jnp.float32, 0.5, 1.5)
    if N_MICROBATCH > 1:
        for name, axis in PER_EXAMPLE_BATCH_AXIS.items():
            out[name] = _to_microbatches(out[name], axis)
    return {'x': out['x'], 'mem': out['mem'], 'norm_g': out['norm_g'], 'w_in': out['w_in'], 'ssd_conv_w': out['ssd_conv_w'], 'ssd_conv_b': out['ssd_conv_b'], 'ssd_dt_bias': out['ssd_dt_bias'], 'ssd_a_log': out['ssd_a_log'], 'ssd_d': out['ssd_d'], 'ssd_norm_g': out['ssd_norm_g'], 'lru_conv_w': out['lru_conv_w'], 'lru_conv_b': out['lru_conv_b'], 'lru_w_a': out['lru_w_a'], 'lru_b_a': out['lru_b_a'], 'lru_w_x': out['lru_w_x'], 'lru_b_x': out['lru_b_x'], 'lru_lambda': out['lru_lambda'], 'mem_norm_g': out['mem_norm_g'], 'w_kv': out['w_kv'], 'w_br_ssd': out['w_br_ssd'], 'w_br_lru': out['w_br_lru'], 'w_br_mem': out['w_br_mem'], 'w_out': out['w_out'], 'final_g': out['final_g'], 'loss_target': out['loss_target'], 'm_norm_g': out['m_norm_g'], 'm_w_in': out['m_w_in'], 'm_ssd_conv_w': out['m_ssd_conv_w'], 'm_ssd_conv_b': out['m_ssd_conv_b'], 'm_ssd_dt_bias': out['m_ssd_dt_bias'], 'm_ssd_a_log': out['m_ssd_a_log'], 'm_ssd_d': out['m_ssd_d'], 'm_ssd_norm_g': out['m_ssd_norm_g'], 'm_lru_conv_w': out['m_lru_conv_w'], 'm_lru_conv_b': out['m_lru_conv_b'], 'm_lru_w_a': out['m_lru_w_a'], 'm_lru_b_a': out['m_lru_b_a'], 'm_lru_w_x': out['m_lru_w_x'], 'm_lru_b_x': out['m_lru_b_x'], 'm_lru_lambda': out['m_lru_lambda'], 'm_mem_norm_g': out['m_mem_norm_g'], 'm_w_kv': out['m_w_kv'], 'm_w_br_ssd': out['m_w_br_ssd'], 'm_w_br_lru': out['m_w_br_lru'], 'm_w_br_mem': out['m_w_br_mem'], 'm_w_out': out['m_w_out'], 'm_final_g': out['m_final_g'], 'v_norm_g': out['v_norm_g'], 'v_w_in': out['v_w_in'], 'v_ssd_conv_w': out['v_ssd_conv_w'], 'v_ssd_conv_b': out['v_ssd_conv_b'], 'v_ssd_dt_bias': out['v_ssd_dt_bias'], 'v_ssd_a_log': out['v_ssd_a_log'], 'v_ssd_d': out['v_ssd_d'], 'v_ssd_norm_g': out['v_ssd_norm_g'], 'v_lru_conv_w': out['v_lru_conv_w'], 'v_lru_conv_b': out['v_lru_conv_b'], 'v_lru_w_a': out['v_lru_w_a'], 'v_lru_b_a': out['v_lru_b_a'], 'v_lru_w_x': out['v_lru_w_x'], 'v_lru_b_x': out['v_lru_b_x'], 'v_lru_lambda': out['v_lru_lambda'], 'v_mem_norm_g': out['v_mem_norm_g'], 'v_w_kv': out['v_w_kv'], 'v_w_br_ssd': out['v_w_br_ssd'], 'v_w_br_lru': out['v_w_br_lru'], 'v_w_br_mem': out['v_w_br_mem'], 'v_w_out': out['v_w_out'], 'v_final_g': out['v_final_g']}


def _loss(weights, diff, rest, loss_target):
    with _jax.named_scope("forward"):
        args = {**rest, TWIN_DIFF_INPUT: diff, **{k: w.astype(_WEIGHT_DTYPES[k]) for k, w in weights.items()}}
        y = _forward(args)
    with _jax.named_scope("loss_head"):
        err = _jnp.square(y.astype(_jnp.float32) - loss_target)
        return 0.5 * _jnp.sum(_jnp.mean(err, axis=-1)) if err.ndim else 0.5 * err


def _adamw(w, g, m, v):
    m = ADAM_B1 * m + (1.0 - ADAM_B1) * g
    v = ADAM_B2 * v + (1.0 - ADAM_B2) * _jnp.square(g)
    m_hat = m / (1.0 - ADAM_B1 ** ADAM_STEP)
    v_hat = v / (1.0 - ADAM_B2 ** ADAM_STEP)
    delta = -ADAM_LR * (m_hat / (_jnp.sqrt(v_hat) + ADAM_EPS) + ADAM_WD * w)
    return delta, m, v


def reference(x, mem, norm_g, w_in, ssd_conv_w, ssd_conv_b, ssd_dt_bias, ssd_a_log, ssd_d, ssd_norm_g, lru_conv_w, lru_conv_b, lru_w_a, lru_b_a, lru_w_x, lru_b_x, lru_lambda, mem_norm_g, w_kv, w_br_ssd, w_br_lru, w_br_mem, w_out, final_g, loss_target, m_norm_g, m_w_in, m_ssd_conv_w, m_ssd_conv_b, m_ssd_dt_bias, m_ssd_a_log, m_ssd_d, m_ssd_norm_g, m_lru_conv_w, m_lru_conv_b, m_lru_w_a, m_lru_b_a, m_lru_w_x, m_lru_b_x, m_lru_lambda, m_mem_norm_g, m_w_kv, m_w_br_ssd, m_w_br_lru, m_w_br_mem, m_w_out, m_final_g, v_norm_g, v_w_in, v_ssd_conv_w, v_ssd_conv_b, v_ssd_dt_bias, v_ssd_a_log, v_ssd_d, v_ssd_norm_g, v_lru_conv_w, v_lru_conv_b, v_lru_w_a, v_lru_b_a, v_lru_w_x, v_lru_b_x, v_lru_lambda, v_mem_norm_g, v_w_kv, v_w_br_ssd, v_w_br_lru, v_w_br_mem, v_w_out, v_final_g):
    given = dict(x=x, mem=mem, norm_g=norm_g, w_in=w_in, ssd_conv_w=ssd_conv_w, ssd_conv_b=ssd_conv_b, ssd_dt_bias=ssd_dt_bias, ssd_a_log=ssd_a_log, ssd_d=ssd_d, ssd_norm_g=ssd_norm_g, lru_conv_w=lru_conv_w, lru_conv_b=lru_conv_b, lru_w_a=lru_w_a, lru_b_a=lru_b_a, lru_w_x=lru_w_x, lru_b_x=lru_b_x, lru_lambda=lru_lambda, mem_norm_g=mem_norm_g, w_kv=w_kv, w_br_ssd=w_br_ssd, w_br_lru=w_br_lru, w_br_mem=w_br_mem, w_out=w_out, final_g=final_g, loss_target=loss_target, m_norm_g=m_norm_g, m_w_in=m_w_in, m_ssd_conv_w=m_ssd_conv_w, m_ssd_conv_b=m_ssd_conv_b, m_ssd_dt_bias=m_ssd_dt_bias, m_ssd_a_log=m_ssd_a_log, m_ssd_d=m_ssd_d, m_ssd_norm_g=m_ssd_norm_g, m_lru_conv_w=m_lru_conv_w, m_lru_conv_b=m_lru_conv_b, m_lru_w_a=m_lru_w_a, m_lru_b_a=m_lru_b_a, m_lru_w_x=m_lru_w_x, m_lru_b_x=m_lru_b_x, m_lru_lambda=m_lru_lambda, m_mem_norm_g=m_mem_norm_g, m_w_kv=m_w_kv, m_w_br_ssd=m_w_br_ssd, m_w_br_lru=m_w_br_lru, m_w_br_mem=m_w_br_mem, m_w_out=m_w_out, m_final_g=m_final_g, v_norm_g=v_norm_g, v_w_in=v_w_in, v_ssd_conv_w=v_ssd_conv_w, v_ssd_conv_b=v_ssd_conv_b, v_ssd_dt_bias=v_ssd_dt_bias, v_ssd_a_log=v_ssd_a_log, v_ssd_d=v_ssd_d, v_ssd_norm_g=v_ssd_norm_g, v_lru_conv_w=v_lru_conv_w, v_lru_conv_b=v_lru_conv_b, v_lru_w_a=v_lru_w_a, v_lru_b_a=v_lru_b_a, v_lru_w_x=v_lru_w_x, v_lru_b_x=v_lru_b_x, v_lru_lambda=v_lru_lambda, v_mem_norm_g=v_mem_norm_g, v_w_kv=v_w_kv, v_w_br_ssd=v_w_br_ssd, v_w_br_lru=v_w_br_lru, v_w_br_mem=v_w_br_mem, v_w_out=v_w_out, v_final_g=v_final_g)
    weights = {n: given[n] for n in TWIN_WEIGHTS}
    shared = {n: given[n] for n in SHARED_INPUTS}
    per_example = {n: given[n] for n in ['x', 'mem']}
    grad_fn = _jax.value_and_grad(_loss, argnums=(0, 1))

    def one_microbatch(ex, loss_target):
        ex = dict(ex)
        diff = ex.pop(TWIN_DIFF_INPUT)
        return grad_fn(weights, diff, {**shared, **ex}, loss_target)

    if N_MICROBATCH == 1:
        loss, (grad_w, grad_x) = one_microbatch(per_example, given["loss_target"])
    else:
        def body(carry, xs):
            loss_sum, grad_sum = carry
            l_k, (gw_k, gx_k) = one_microbatch(xs[0], xs[1])
            with _jax.named_scope("update"):
                return (loss_sum + l_k, _jax.tree.map(_jnp.add, grad_sum, gw_k)), gx_k

        init = (_jnp.zeros((), _jnp.float32), _jax.tree.map(_jnp.zeros_like, weights))
        (loss, grad_w), grad_x = _jax.lax.scan(body, init, (per_example, given["loss_target"]))
    with _jax.named_scope("update"):
        delta_w, new_m, new_v = {}, {}, {}
        for n in TWIN_WEIGHTS:
            delta_w[n], new_m[n], new_v[n] = _adamw(weights[n], grad_w[n], given["m_" + n], given["v_" + n])
    return (loss, grad_x, *[grad_w[n] for n in TWIN_WEIGHTS], *[delta_w[n] for n in TWIN_WEIGHTS],
            *[new_m[n] for n in TWIN_WEIGHTS], *[new_v[n] for n in TWIN_WEIGHTS])
```

```python
import functools
import math

import jax
import jax.numpy as jnp
from jax import lax
from jax.experimental import pallas as pl
from jax.experimental.pallas import tpu as pltpu

F32 = jnp.float32
MXU_DTYPE = jnp.bfloat16

D_MODEL = 1024
EPS = 1e-6
CONV_WIDTH = 4
SSD_WIDTH = 2048
SSD_HEAD_DIM = 64
SSD_HEADS = 32
SSD_GROUPS = 4
SSD_STATE = 128
SSD_CHUNK = 128
SSD_CONV_CH = SSD_WIDTH + 2 * SSD_GROUPS * SSD_STATE
SSD_PAIRS = SSD_HEADS // 2
PAIRS_PER_GROUP = SSD_PAIRS // SSD_GROUPS
LRU_WIDTH = 1536
LRU_BLOCKS = 16
LRU_BLOCK = 96
LRU_GROUP = 4 * LRU_BLOCK
LRU_NGROUPS = LRU_WIDTH // LRU_GROUP
LRU_C = 8.0
LRU_ROWS = 256
MEM_HEADS = 4
MEM_HEAD_DIM = 256
IN_WIDTH = 12320
N_DEV = 8
LANES = 128
SSD_SEG = SSD_WIDTH + SSD_CONV_CH
DT_PAD = LANES
SEG_BOUNDS = (0, 5120, 5152, 8224, 9248, 12320)

ADAM_LR = 0.001
ADAM_B1 = 0.9
ADAM_B2 = 0.999
ADAM_EPS = 1e-08
ADAM_WD = 0.01
ADAM_STEP = 10

VMEM_LIMIT = 56 * 1024 * 1024

NN = (((1,), (0,)), ((), ()))
NT = (((1,), (1,)), ((), ()))
TN = (((0,), (0,)), ((), ()))


def _dot(a, b, dims):
    return lax.dot_general(a.astype(MXU_DTYPE), b.astype(MXU_DTYPE), dims, preferred_element_type=F32)


def _sigmoid(x):
    return 1.0 / (1.0 + jnp.exp(-x))


def _log1p(e):
    u = 1.0 + e
    return jnp.where(u == 1.0, e, jnp.log(u) * (e / jnp.where(u == 1.0, 1.0, u - 1.0)))


def _softplus(x):
    return jnp.maximum(x, 0.0) + _log1p(jnp.exp(-jnp.abs(x)))


def _expm1(x):
    u = jnp.exp(x)
    um1 = u - 1.0
    lg = jnp.log(u)
    safe = jnp.where(um1 == 0.0, 1.0, lg)
    return jnp.where(um1 == 0.0, x, jnp.where(um1 == -1.0, -1.0, um1 * (x / safe)))


def _params(semantics):
    return pltpu.CompilerParams(dimension_semantics=semantics, vmem_limit_bytes=VMEM_LIMIT)


def _shift_down(cur, halo8, k):
    rolled = pltpu.roll(cur, k, 0)
    row8 = lax.broadcasted_iota(jnp.int32, halo8.shape, 0)
    top = jnp.where(row8 >= k, rolled[0:8], pltpu.roll(halo8, k, 0))
    return jnp.concatenate([top, rolled[8:]], axis=0)


def _shift_up(cur, next8, k):
    rows = cur.shape[0]
    rolled = pltpu.roll(cur, rows - k, 0)
    row8 = lax.broadcasted_iota(jnp.int32, next8.shape, 0)
    bot = jnp.where(row8 < 8 - k, rolled[rows - 8:rows], pltpu.roll(next8, 8 - k, 0))
    return jnp.concatenate([rolled[:rows - 8], bot], axis=0)


def _causal_conv(raw, halo8, w, b):
    acc = raw * w[3:4, :] + b
    for k in range(1, CONV_WIDTH):
        acc = acc + _shift_down(raw, halo8, k) * w[3 - k:4 - k, :]
    return acc


def _conv_backward(dco, next8, raw, halo8, w):
    d_raw = dco * w[3:4, :]
    for k in range(1, CONV_WIDTH):
        d_raw = d_raw + _shift_up(dco, next8, k) * w[3 - k:4 - k, :]
    gw = []
    for k in range(CONV_WIDTH):
        shifted = raw if k == 3 else _shift_down(raw, halo8, 3 - k)
        gw.append(jnp.sum(dco * shifted, axis=0, keepdims=True))
    gb = jnp.sum(dco, axis=0, keepdims=True)
    return d_raw, gw, gb


def _cumsum_rows(v):
    rows = v.shape[0]
    row = lax.broadcasted_iota(jnp.int32, v.shape, 0)
    s = 1
    while s < rows:
        v = v + jnp.where(row >= s, pltpu.roll(v, s, 0), 0.0)
        s *= 2
    return v


def _rev_cumsum_rows(v):
    rows = v.shape[0]
    row = lax.broadcasted_iota(jnp.int32, v.shape, 0)
    s = 1
    while s < rows:
        v = v + jnp.where(row < rows - s, pltpu.roll(v, rows - s, 0), 0.0)
        s *= 2
    return v


def _matmul(a, b, mode, name, out_dtype=F32, tm=1024, tn=1024, tk=512, acc_in=None):
    if mode == "nn":
        (m, kk), n = a.shape, b.shape[1]
    elif mode == "nt":
        (m, kk), n = a.shape, b.shape[0]
    else:
        (kk, m), n = a.shape, b.shape[1]
    tm, tn, tk = min(tm, m), min(tn, n), min(tk, kk)
    assert m % tm == 0 and n % tn == 0 and kk % tk == 0, (name, a.shape, b.shape)
    nk = kk // tk
    dims = {"nn": NN, "nt": NT, "tn": TN}[mode]
    a_spec = pl.BlockSpec((tk, tm), lambda i, j, k: (k, i)) if mode == "tn" else pl.BlockSpec((tm, tk), lambda i, j, k: (i, k))
    b_spec = pl.BlockSpec((tn, tk), lambda i, j, k: (j, k)) if mode == "nt" else pl.BlockSpec((tk, tn), lambda i, j, k: (k, j))
    o_spec = pl.BlockSpec((tm, tn), lambda i, j, k: (i, j))
    has_acc = acc_in is not None

    def body(*refs):
        if has_acc:
            a_ref, b_ref, c_ref, o_ref, acc_ref = refs
        else:
            a_ref, b_ref, o_ref, acc_ref = refs
        k = pl.program_id(2)

        @pl.when(k == 0)
        def _():
            acc_ref[...] = c_ref[...].astype(F32) if has_acc else jnp.zeros_like(acc_ref)

        acc_ref[...] += _dot(a_ref[...], b_ref[...], dims)

        @pl.when(k == nk - 1)
        def _():
            o_ref[...] = acc_ref[...].astype(o_ref.dtype)

    args = (a, b) + ((acc_in,) if has_acc else ())
    in_specs = [a_spec, b_spec] + ([o_spec] if has_acc else [])
    return pl.pallas_call(
        body, name=name, grid=(m // tm, n // tn, nk), in_specs=in_specs, out_specs=o_spec,
        out_shape=jax.ShapeDtypeStruct((m, n), out_dtype),
        scratch_shapes=[pltpu.VMEM((tm, tn), F32)],
        compiler_params=_params(("parallel", "parallel", "arbitrary")),
    )(*args)


def _rms_fwd(x, g, name, rows=512):
    t, d = x.shape
    rows = min(rows, t)

    def body(x_ref, g_ref, h_ref):
        xv = x_ref[...]
        r = lax.rsqrt(jnp.mean(xv * xv, axis=-1, keepdims=True) + EPS)
        h_ref[...] = ((xv * r) * g_ref[...]).astype(h_ref.dtype)

    return pl.pallas_call(
        body, name=name, grid=(t // rows,),
        in_specs=[pl.BlockSpec((rows, d), lambda i: (i, 0)), pl.BlockSpec((1, d), lambda i: (0, 0))],
        out_specs=pl.BlockSpec((rows, d), lambda i: (i, 0)),
        out_shape=jax.ShapeDtypeStruct((t, d), MXU_DTYPE),
        compiler_params=_params(("parallel",)),
    )(x, g)


def _rms_bwd(x, dh, dres, g, name, rows=512):
    t, d = x.shape
    rows = min(rows, t)
    has_res = dres is not None

    def body(*refs):
        if has_res:
            x_ref, dh_ref, dr_ref, g_ref, dx_ref, gg_ref = refs
        else:
            x_ref, dh_ref, g_ref, dx_ref, gg_ref = refs

        @pl.when(pl.program_id(0) == 0)
        def _():
            gg_ref[...] = jnp.zeros_like(gg_ref)

        xv = x_ref[...]
        dhv = dh_ref[...]
        r = lax.rsqrt(jnp.mean(xv * xv, axis=-1, keepdims=True) + EPS)
        n = xv * r
        dn = dhv * g_ref[...]
        dx = r * (dn - n * jnp.mean(dn * n, axis=-1, keepdims=True))
        if has_res:
            dx = dx + dr_ref[...]
        dx_ref[...] = dx
        gg_ref[...] += jnp.sum(dhv * n, axis=0, keepdims=True)

    row_spec = pl.BlockSpec((rows, d), lambda i: (i, 0))
    vec_spec = pl.BlockSpec((1, d), lambda i: (0, 0))
    args = (x, dh) + ((dres,) if has_res else ()) + (g,)
    return pl.pallas_call(
        body, name=name, grid=(t // rows,),
        in_specs=[row_spec, row_spec] + ([row_spec] if has_res else []) + [vec_spec],
        out_specs=[row_spec, vec_spec],
        out_shape=[jax.ShapeDtypeStruct((t, d), F32), jax.ShapeDtypeStruct((1, d), F32)],
        compiler_params=_params(("arbitrary",)),
    )(*args)


def _pair_select(lo, m, h0):
    return jnp.where(lo, m[:, h0:h0 + 1], m[:, h0 + 1:h0 + 2])


def _halves(lo, v):
    return (jnp.sum(jnp.where(lo, v, 0.0), axis=1, keepdims=True),
            jnp.sum(jnp.where(lo, 0.0, v), axis=1, keepdims=True))


def _ssd_common(dt_raw, dtb, alog):
    dt = _softplus(dt_raw + dtb)
    aneg = -jnp.exp(alog)
    a_cs = _cumsum_rows(dt * aneg)
    return dt, aneg, a_cs, a_cs.T


def _ssd_specs(nc, rev):
    cidx = (lambda c: nc - 1 - c) if rev else (lambda c: c)
    L = SSD_CHUNK
    return dict(
        z=pl.BlockSpec((L, SSD_WIDTH), lambda c: (cidx(c), 0)),
        xr=pl.BlockSpec((L, SSD_WIDTH), lambda c: (cidx(c), 1)),
        br=pl.BlockSpec((L, 512), lambda c: (cidx(c), 8)),
        cr=pl.BlockSpec((L, 512), lambda c: (cidx(c), 9)),
        dt=pl.BlockSpec((L, DT_PAD), lambda c: (cidx(c), 0)),
        cwx=pl.BlockSpec((CONV_WIDTH, SSD_WIDTH), lambda c: (0, 0)),
        cwb=pl.BlockSpec((CONV_WIDTH, 512), lambda c: (0, 4)),
        cwc=pl.BlockSpec((CONV_WIDTH, 512), lambda c: (0, 5)),
        cbx=pl.BlockSpec((1, SSD_WIDTH), lambda c: (0, 0)),
        cbb=pl.BlockSpec((1, 512), lambda c: (0, 4)),
        cbc=pl.BlockSpec((1, 512), lambda c: (0, 5)),
        vec128=pl.BlockSpec((1, LANES), lambda c: (0, 0)),
        vecw=pl.BlockSpec((1, SSD_WIDTH), lambda c: (0, 0)),
        wide=pl.BlockSpec((L, SSD_WIDTH), lambda c: (cidx(c), 0)),
        states=pl.BlockSpec((1, SSD_PAIRS, 128, SSD_STATE), lambda c: (cidx(c), 0, 0, 0)),
    )


def _ssd_fwd(proj_ssd, dt_p, conv_w, conv_b, dtb, alog, d_row, ng_row):
    t = proj_ssd.shape[0]
    nc = t // SSD_CHUNK
    L = SSD_CHUNK
    sp = _ssd_specs(nc, False)

    def body(z_ref, xr_ref, br_ref, cr_ref, dt_ref, cwx_ref, cwb_ref, cwc_ref, cbx_ref, cbb_ref, cbc_ref,
             dtb_ref, alog_ref, d_ref, ng_ref, yssd_ref, y_ref, st_ref,
             hx_ref, hb_ref, hc_ref, state_ref, yacc_ref):
        @pl.when(pl.program_id(0) == 0)
        def _():
            hx_ref[...] = jnp.zeros_like(hx_ref)
            hb_ref[...] = jnp.zeros_like(hb_ref)
            hc_ref[...] = jnp.zeros_like(hc_ref)
            state_ref[...] = jnp.zeros_like(state_ref)

        xr, br, cr = xr_ref[...], br_ref[...], cr_ref[...]
        px = _causal_conv(xr, hx_ref[...], cwx_ref[...], cbx_ref[...])
        pb = _causal_conv(br, hb_ref[...], cwb_ref[...], cbb_ref[...])
        pc = _causal_conv(cr, hc_ref[...], cwc_ref[...], cbc_ref[...])
        hx_ref[...] = xr[L - 8:L, :]
        hb_ref[...] = br[L - 8:L, :]
        hc_ref[...] = cr[L - 8:L, :]
        xs = px * _sigmoid(px)
        bm = pb * _sigmoid(pb)
        cm = pc * _sigmoid(pc)

        dt, _, a_cs, a_t = _ssd_common(dt_ref[...], dtb_ref[...], alog_ref[...])
        exp_a = jnp.exp(a_cs)
        a_last = a_cs[L - 1:L, :]
        dte = jnp.exp(a_last - a_cs)
        dec = jnp.exp(a_last)

        lane = lax.broadcasted_iota(jnp.int32, (L, LANES), 1)
        sub = lax.broadcasted_iota(jnp.int32, (L, LANES), 0)
        lo = lane < SSD_HEAD_DIM
        causal = sub >= lane
        top = sub < SSD_HEAD_DIM

        for g in range(SSD_GROUPS):
            b_g = bm[:, g * SSD_STATE:(g + 1) * SSD_STATE]
            c_g = cm[:, g * SSD_STATE:(g + 1) * SSD_STATE]
            cb = _dot(c_g, b_g, NT)
            for jj in range(PAIRS_PER_GROUP):
                j = g * PAIRS_PER_GROUP + jj
                h0 = 2 * j
                cols = slice(j * LANES, (j + 1) * LANES)
                xs_p = xs[:, cols]
                xdt = xs_p * _pair_select(lo, dt, h0)
                g0 = jnp.where(causal, jnp.exp(a_cs[:, h0:h0 + 1] - a_t[h0:h0 + 1, :]), 0.0) * cb
                g1 = jnp.where(causal, jnp.exp(a_cs[:, h0 + 1:h0 + 2] - a_t[h0 + 1:h0 + 2, :]), 0.0) * cb
                lhs = jnp.concatenate([g0, g1], axis=1)
                rhs = jnp.concatenate([jnp.where(lo, xdt, 0.0), jnp.where(lo, 0.0, xdt)], axis=0)
                y_diag = _dot(lhs, rhs, NN)
                h_p = state_ref[j]
                st_ref[0, j] = h_p
                y_off = _dot(c_g, h_p, NT) * _pair_select(lo, exp_a, h0)
                s_new = _dot(xdt * _pair_select(lo, dte, h0), b_g, TN)
                dec_rows = jnp.where(top, dec[:, h0:h0 + 1], dec[:, h0 + 1:h0 + 2])
                state_ref[j] = h_p * dec_rows + s_new
                yacc_ref[:, cols] = (y_diag + y_off) + xs_p * d_ref[:, cols]

        y = yacc_ref[...]
        y_ref[...] = y
        zz = z_ref[...]
        y2 = y * (zz * _sigmoid(zz))
        gw = SSD_WIDTH // SSD_GROUPS
        for g in range(SSD_GROUPS):
            seg = y2[:, g * gw:(g + 1) * gw]
            r = lax.rsqrt(jnp.mean(seg * seg, axis=-1, keepdims=True) + EPS)
            yssd_ref[:, g * gw:(g + 1) * gw] = ((seg * r) * ng_ref[:, g * gw:(g + 1) * gw]).astype(yssd_ref.dtype)

    return pl.pallas_call(
        body, name="ssd_fwd", grid=(nc,),
        in_specs=[sp["z"], sp["xr"], sp["br"], sp["cr"], sp["dt"], sp["cwx"], sp["cwb"], sp["cwc"],
                  sp["cbx"], sp["cbb"], sp["cbc"], sp["vec128"], sp["vec128"], sp["vecw"], sp["vecw"]],
        out_specs=[sp["wide"], sp["wide"], sp["states"]],
        out_shape=[jax.ShapeDtypeStruct((t, SSD_WIDTH), MXU_DTYPE), jax.ShapeDtypeStruct((t, SSD_WIDTH), F32),
                   jax.ShapeDtypeStruct((nc, SSD_PAIRS, 128, SSD_STATE), F32)],
        scratch_shapes=[pltpu.VMEM((8, SSD_WIDTH), F32), pltpu.VMEM((8, 512), F32), pltpu.VMEM((8, 512), F32),
                        pltpu.VMEM((SSD_PAIRS, 128, SSD_STATE), F32), pltpu.VMEM((L, SSD_WIDTH), F32)],
        compiler_params=_params(("arbitrary",)),
    )(proj_ssd, proj_ssd, proj_ssd, proj_ssd, dt_p, conv_w, conv_w, conv_w, conv_b, conv_b, conv_b,
      dtb, alog, d_row, ng_row)


def _ssd_bwd(proj_ssd, dt_p, y, states, dyssd, conv_w, conv_b, dtb, alog, d_row, ng_row):
    t = proj_ssd.shape[0]
    nc = t // SSD_CHUNK
    L = SSD_CHUNK
    sp = _ssd_specs(nc, True)
    groups8 = L // 8

    def halo_spec(width, col):
        return pl.BlockSpec((8, width), lambda c: (jnp.maximum((nc - 1 - c) * groups8 - 1, 0), col))

    def body(z_ref, xr_ref, br_ref, cr_ref, hx_ref, hb_ref, hc_ref, dt_ref, y_ref, st_ref, dy_ref,
             cwx_ref, cwb_ref, cwc_ref, cbx_ref, cbb_ref, cbc_ref, dtb_ref, alog_ref, d_ref, ng_ref,
             dssd_ref, ddt_ref, gcw_ref, gcb_ref, gdtb_ref, galog_ref, gd_ref, gng_ref,
             gn_ref, nx_ref, nb_ref, ncc_ref, dxs_ref):
        step = pl.program_id(0)

        @pl.when(step == 0)
        def _():
            gn_ref[...] = jnp.zeros_like(gn_ref)
            nx_ref[...] = jnp.zeros_like(nx_ref)
            nb_ref[...] = jnp.zeros_like(nb_ref)
            ncc_ref[...] = jnp.zeros_like(ncc_ref)
            for ref in (gcw_ref, gcb_ref, gdtb_ref, galog_ref, gd_ref, gng_ref):
                ref[...] = jnp.zeros_like(ref)

        first_chunk = step == nc - 1
        keep = jnp.where(first_chunk, 0.0, 1.0)
        xr, br, cr = xr_ref[...], br_ref[...], cr_ref[...]
        hx, hb, hc = hx_ref[...] * keep, hb_ref[...] * keep, hc_ref[...] * keep
        cwx, cwb, cwc = cwx_ref[...], cwb_ref[...], cwc_ref[...]
        px = _causal_conv(xr, hx, cwx, cbx_ref[...])
        pb = _causal_conv(br, hb, cwb, cbb_ref[...])
        pc = _causal_conv(cr, hc, cwc, cbc_ref[...])
        sx, sb, sc = _sigmoid(px), _sigmoid(pb), _sigmoid(pc)
        xs, bm, cm = px * sx, pb * sb, pc * sc

        dt_in = dt_ref[...] + dtb_ref[...]
        dt, aneg, a_cs, a_t = _ssd_common(dt_ref[...], dtb_ref[...], alog_ref[...])
        exp_a = jnp.exp(a_cs)
        a_last = a_cs[L - 1:L, :]
        dte = jnp.exp(a_last - a_cs)
        dec = jnp.exp(a_last)

        lane = lax.broadcasted_iota(jnp.int32, (L, LANES), 1)
        sub = lax.broadcasted_iota(jnp.int32, (L, LANES), 0)
        lo = lane < SSD_HEAD_DIM
        causal = sub >= lane
        top = sub < SSD_HEAD_DIM
        last_row = sub == L - 1

        yv = y_ref[...]
        zz = z_ref[...]
        sz = _sigmoid(zz)
        silz = zz * sz
        y2 = yv * silz
        dyv = dy_ref[...]
        gw = SSD_WIDTH // SSD_GROUPS
        d_y2_parts = []
        gng_parts = []
        for g in range(SSD_GROUPS):
            seg = y2[:, g * gw:(g + 1) * gw]
            dseg = dyv[:, g * gw:(g + 1) * gw]
            r = lax.rsqrt(jnp.mean(seg * seg, axis=-1, keepdims=True) + EPS)
            n = seg * r
            dn = dseg * ng_ref[:, g * gw:(g + 1) * gw]
            gng_parts.append(jnp.sum(dseg * n, axis=0, keepdims=True))
            d_y2_parts.append(r * (dn - n * jnp.mean(dn * n, axis=-1, keepdims=True)))
        d_y2 = jnp.concatenate(d_y2_parts, axis=1)
        gng_ref[...] += jnp.concatenate(gng_parts, axis=1)
        d_y = d_y2 * silz
        dssd_ref[:, 0:SSD_WIDTH] = (d_y2 * yv * (sz * (1.0 + zz * (1.0 - sz)))).astype(dssd_ref.dtype)
        gd_ref[...] += jnp.sum(d_y * xs, axis=0, keepdims=True)
        dxs_ref[...] = d_y * d_ref[...]

        d_a = jnp.zeros((L, LANES), F32)
        d_at = jnp.zeros((LANES, L), F32)
        ddt = jnp.zeros((L, LANES), F32)
        d_b_parts, d_c_parts = [], []
        for g in range(SSD_GROUPS):
            b_g = bm[:, g * SSD_STATE:(g + 1) * SSD_STATE]
            c_g = cm[:, g * SSD_STATE:(g + 1) * SSD_STATE]
            cb = _dot(c_g, b_g, NT)
            d_cb = jnp.zeros((L, L), F32)
            d_bg = jnp.zeros((L, SSD_STATE), F32)
            d_cg = jnp.zeros((L, SSD_STATE), F32)
            for jj in range(PAIRS_PER_GROUP):
                j = g * PAIRS_PER_GROUP + jj
                h0 = 2 * j
                cols = slice(j * LANES, (j + 1) * LANES)
                dy_p = d_y[:, cols]
                xs_p = xs[:, cols]
                dt_pp = _pair_select(lo, dt, h0)
                expa_p = _pair_select(lo, exp_a, h0)
                dte_p = _pair_select(lo, dte, h0)
                xdt = xs_p * dt_pp
                l0 = jnp.where(causal, jnp.exp(a_cs[:, h0:h0 + 1] - a_t[h0:h0 + 1, :]), 0.0)
                l1 = jnp.where(causal, jnp.exp(a_cs[:, h0 + 1:h0 + 2] - a_t[h0 + 1:h0 + 2, :]), 0.0)
                g0, g1 = l0 * cb, l1 * cb
                h_p = st_ref[0, j]
                gn_p = gn_ref[j]
                dys = dy_p * expa_p
                d_cg = d_cg + _dot(dys, h_p, NN)
                d_h = _dot(dys, c_g, TN)
                t1 = dy_p * _dot(c_g, h_p, NT) * expa_p
                dw = _dot(b_g, gn_p, NT)
                d_bg = d_bg + _dot(xdt * dte_p, gn_p, NN)
                d_xdt = dw * dte_p
                t2 = d_xdt * xdt
                dyl, dyh = jnp.where(lo, dy_p, 0.0), jnp.where(lo, 0.0, dy_p)
                d_xdt = d_xdt + _dot(jnp.concatenate([g0, g1], axis=0), jnp.concatenate([dyl, dyh], axis=0), TN)
                dm0 = _dot(dyl, xdt, NT)
                dm1 = _dot(dyh, xdt, NT)
                d_cb = d_cb + (l0 * dm0 + l1 * dm1)
                e0, e1 = dm0 * g0, dm1 * g1
                a0, a1 = _halves(lo, t1 - t2)
                a0 = a0 + jnp.sum(e0, axis=1, keepdims=True)
                a1 = a1 + jnp.sum(e1, axis=1, keepdims=True)
                s0, s1 = _halves(lo, t2)
                gh = jnp.sum(gn_p * h_p, axis=1, keepdims=True)
                dd0 = jnp.sum(jnp.where(top[:, 0:1], gh, 0.0), axis=0, keepdims=True)
                dd1 = jnp.sum(jnp.where(top[:, 0:1], 0.0, gh), axis=0, keepdims=True)
                end0 = jnp.sum(s0, axis=0, keepdims=True) + dd0 * dec[:, h0:h0 + 1]
                end1 = jnp.sum(s1, axis=0, keepdims=True) + dd1 * dec[:, h0 + 1:h0 + 2]
                d_a = d_a + jnp.where(lane == h0, a0 + jnp.where(last_row, end0, 0.0), 0.0)
                d_a = d_a + jnp.where(lane == h0 + 1, a1 + jnp.where(last_row, end1, 0.0), 0.0)
                d_at = d_at - jnp.where(sub == h0, jnp.sum(e0, axis=0, keepdims=True), 0.0)
                d_at = d_at - jnp.where(sub == h0 + 1, jnp.sum(e1, axis=0, keepdims=True), 0.0)
                dec_rows = jnp.where(top, dec[:, h0:h0 + 1], dec[:, h0 + 1:h0 + 2])
                gn_ref[j] = d_h + dec_rows * gn_p
                q0, q1 = _halves(lo, d_xdt * xs_p)
                ddt = ddt + jnp.where(lane == h0, q0, 0.0) + jnp.where(lane == h0 + 1, q1, 0.0)
                dxs_ref[:, cols] += d_xdt * dt_pp
            d_cg = d_cg + _dot(d_cb, b_g, NN)
            d_bg = d_bg + _dot(d_cb, c_g, TN)
            d_b_parts.append(d_bg)
            d_c_parts.append(d_cg)

        rc = _rev_cumsum_rows(d_a + d_at.T)
        d_dt = rc * aneg + ddt
        galog_ref[...] += jnp.sum(rc * dt, axis=0, keepdims=True) * aneg
        d_dtraw = d_dt * _sigmoid(dt_in)
        gdtb_ref[...] += jnp.sum(d_dtraw, axis=0, keepdims=True)
        ddt_ref[...] = d_dtraw.astype(ddt_ref.dtype)

        def dsilu(p, s):
            return s * (1.0 + p * (1.0 - s))

        dcx = dxs_ref[...] * dsilu(px, sx)
        dcb = jnp.concatenate(d_b_parts, axis=1) * dsilu(pb, sb)
        dcc = jnp.concatenate(d_c_parts, axis=1) * dsilu(pc, sc)
        drx, gwx, gbx = _conv_backward(dcx, nx_ref[...], xr, hx, cwx)
        drb, gwb, gbb = _conv_backward(dcb, nb_ref[...], br, hb, cwb)
        drc, gwc, gbc = _conv_backward(dcc, ncc_ref[...], cr, hc, cwc)
        nx_ref[...] = dcx[0:8, :]
        nb_ref[...] = dcb[0:8, :]
        ncc_ref[...] = dcc[0:8, :]
        dssd_ref[:, SSD_WIDTH:2 * SSD_WIDTH] = drx.astype(dssd_ref.dtype)
        dssd_ref[:, 2 * SSD_WIDTH:2 * SSD_WIDTH + 512] = drb.astype(dssd_ref.dtype)
        dssd_ref[:, 2 * SSD_WIDTH + 512:SSD_SEG] = drc.astype(dssd_ref.dtype)
        for k in range(CONV_WIDTH):
            gcw_ref[k:k + 1, :] += jnp.concatenate([gwx[k], gwb[k], gwc[k]], axis=1)
        gcb_ref[...] += jnp.concatenate([gbx, gbb, gbc], axis=1)

    const = lambda shape: pl.BlockSpec(shape, lambda c: (0,) * len(shape))
    return pl.pallas_call(
        body, name="ssd_bwd", grid=(nc,),
        in_specs=[sp["z"], sp["xr"], sp["br"], sp["cr"], halo_spec(SSD_WIDTH, 1), halo_spec(512, 8), halo_spec(512, 9),
                  sp["dt"], sp["wide"], sp["states"], sp["wide"],
                  sp["cwx"], sp["cwb"], sp["cwc"], sp["cbx"], sp["cbb"], sp["cbc"],
                  sp["vec128"], sp["vec128"], sp["vecw"], sp["vecw"]],
        out_specs=[pl.BlockSpec((L, SSD_SEG), lambda c: (nc - 1 - c, 0)), sp["dt"],
                   const((CONV_WIDTH, SSD_CONV_CH)), const((1, SSD_CONV_CH)), const((1, LANES)), const((1, LANES)),
                   const((1, SSD_WIDTH)), const((1, SSD_WIDTH))],
        out_shape=[jax.ShapeDtypeStruct((t, SSD_SEG), MXU_DTYPE), jax.ShapeDtypeStruct((t, DT_PAD), MXU_DTYPE),
                   jax.ShapeDtypeStruct((CONV_WIDTH, SSD_CONV_CH), F32), jax.ShapeDtypeStruct((1, SSD_CONV_CH), F32),
                   jax.ShapeDtypeStruct((1, LANES), F32), jax.ShapeDtypeStruct((1, LANES), F32),
                   jax.ShapeDtypeStruct((1, SSD_WIDTH), F32), jax.ShapeDtypeStruct((1, SSD_WIDTH), F32)],
        scratch_shapes=[pltpu.VMEM((SSD_PAIRS, 128, SSD_STATE), F32), pltpu.VMEM((8, SSD_WIDTH), F32),
                        pltpu.VMEM((8, 512), F32), pltpu.VMEM((8, 512), F32), pltpu.VMEM((L, SSD_WIDTH), F32)],
        compiler_params=_params(("arbitrary",)),
    )(proj_ssd, proj_ssd, proj_ssd, proj_ssd, proj_ssd, proj_ssd, proj_ssd, dt_p, y, states, dyssd,
      conv_w, conv_w, conv_w, conv_b, conv_b, conv_b, dtb, alog, d_row, ng_row)


def _lru_gates(xl, wa_ref, wx_ref, ba, bx, lam):
    pre_a, pre_x = [], []
    for g in range(LRU_NGROUPS):
        xg = xl[:, g * LRU_GROUP:(g + 1) * LRU_GROUP]
        pre_a.append(_dot(xg, wa_ref[g], NN))
        pre_x.append(_dot(xg, wx_ref[g], NN))
    r = _sigmoid(jnp.concatenate(pre_a, axis=1) + ba)
    i = _sigmoid(jnp.concatenate(pre_x, axis=1) + bx)
    log_a = (-LRU_C * r) * _softplus(-lam)
    a = jnp.exp(log_a)
    mult = jnp.sqrt(-_expm1(2.0 * log_a))
    return r, i, log_a, a, mult


def _lru_fwd(proj_lru, conv_w, conv_b, wa, wx, ba, bx, lam):
    t = proj_lru.shape[0]
    rows = min(LRU_ROWS, t)
    nb = t // rows
    W = LRU_WIDTH

    def body(lg_ref, lx_ref, cw_ref, cb_ref, wa_ref, wx_ref, ba_ref, bx_ref, lam_ref, ylru_ref, h_ref,
             halo_ref, carry_ref):
        @pl.when(pl.program_id(0) == 0)
        def _():
            halo_ref[...] = jnp.zeros_like(halo_ref)
            carry_ref[...] = jnp.zeros_like(carry_ref)

        lx = lx_ref[...]
        xl = _causal_conv(lx, halo_ref[...], cw_ref[...], cb_ref[...])
        halo_ref[...] = lx[rows - 8:rows, :]
        _, i, _, a, mult = _lru_gates(xl, wa_ref, wx_ref, ba_ref[...], bx_ref[...], lam_ref[...])
        u = mult * (i * xl)
        row = lax.broadcasted_iota(jnp.int32, (rows, W), 0)
        p = a
        s = 1
        while s < rows:
            ok = row >= s
            u = p * jnp.where(ok, pltpu.roll(u, s, 0), 0.0) + u
            p = p * jnp.where(ok, pltpu.roll(p, s, 0), 1.0)
            s *= 2
        h = p * carry_ref[...] + u
        carry_ref[...] = h[rows - 1:rows, :]
        h_ref[...] = h
        lg = lg_ref[...]
        ylru_ref[...] = (h * (lg * _sigmoid(lg))).astype(ylru_ref.dtype)

    const = lambda shape: pl.BlockSpec(shape, lambda b: (0,) * len(shape))
    return pl.pallas_call(
        body, name="lru_fwd", grid=(nb,),
        in_specs=[pl.BlockSpec((rows, W), lambda b: (b, 0)), pl.BlockSpec((rows, W), lambda b: (b, 1)),
                  const((CONV_WIDTH, W)), const((1, W)), const((LRU_NGROUPS, LRU_GROUP, LRU_GROUP)),
                  const((LRU_NGROUPS, LRU_GROUP, LRU_GROUP)), const((1, W)), const((1, W)), const((1, W))],
        out_specs=[pl.BlockSpec((rows, W), lambda b: (b, 0)), pl.BlockSpec((rows, W), lambda b: (b, 0))],
        out_shape=[jax.ShapeDtypeStruct((t, W), MXU_DTYPE), jax.ShapeDtypeStruct((t, W), F32)],
        scratch_shapes=[pltpu.VMEM((8, W), F32), pltpu.VMEM((1, W), F32)],
        compiler_params=_params(("arbitrary",)),
    )(proj_lru, proj_lru, conv_w, conv_b, wa, wx, ba, bx, lam)


def _lru_bwd(proj_lru, h, dylru, conv_w, conv_b, wa, wx, ba, bx, lam):
    t = proj_lru.shape[0]
    rows = min(LRU_ROWS, t)
    nb = t // rows
    W = LRU_WIDTH
    groups8 = rows // 8

    def rev(b):
        return nb - 1 - b

    def halo_spec(col):
        return pl.BlockSpec((8, W), lambda b: (jnp.maximum(rev(b) * groups8 - 1, 0), col))

    def body(lg_ref, lx_ref, hlx_ref, h_ref, hh_ref, dy_ref, cw_ref, cb_ref, wa_ref, wx_ref, ba_ref, bx_ref, lam_ref,
             dlru_ref, gcw_ref, gcb_ref, gba_ref, gbx_ref, glam_ref, gwa_ref, gwx_ref,
             gcarry_ref, afirst_ref, nxt_ref):
        step = pl.program_id(0)

        @pl.when(step == 0)
        def _():
            gcarry_ref[...] = jnp.zeros_like(gcarry_ref)
            afirst_ref[...] = jnp.zeros_like(afirst_ref)
            nxt_ref[...] = jnp.zeros_like(nxt_ref)
            for ref in (gcw_ref, gcb_ref, gba_ref, gbx_ref, glam_ref, gwa_ref, gwx_ref):
                ref[...] = jnp.zeros_like(ref)

        keep = jnp.where(step == nb - 1, 0.0, 1.0)
        lx = lx_ref[...]
        hlx = hlx_ref[...] * keep
        cw = cw_ref[...]
        xl = _causal_conv(lx, hlx, cw, cb_ref[...])
        lam = lam_ref[...]
        r, i, log_a, a, mult = _lru_gates(xl, wa_ref, wx_ref, ba_ref[...], bx_ref[...], lam)
        hv = h_ref[...]
        h_prev = _shift_down(hv, hh_ref[...] * keep, 1)
        lg = lg_ref[...]
        sg = _sigmoid(lg)
        dyv = dy_ref[...]
        d_h = dyv * (lg * sg)
        dlru_ref[:, 0:W] = (dyv * hv * (sg * (1.0 + lg * (1.0 - sg)))).astype(dlru_ref.dtype)

        row = lax.broadcasted_iota(jnp.int32, (rows, W), 0)
        p = jnp.where(row < rows - 1, pltpu.roll(a, rows - 1, 0), afirst_ref[...])
        u = d_h
        s = 1
        while s < rows:
            ok = row < rows - s
            u = p * jnp.where(ok, pltpu.roll(u, rows - s, 0), 0.0) + u
            p = p * jnp.where(ok, pltpu.roll(p, rows - s, 0), 1.0)
            s *= 2
        gsc = p * gcarry_ref[...] + u
        gcarry_ref[...] = gsc[0:1, :]
        afirst_ref[...] = a[0:1, :]

        d_a = gsc * h_prev
        v = i * xl
        d_mult = gsc * v
        d_v = gsc * mult
        d_i = d_v * xl
        d_xl = d_v * i
        d_la = d_a * a - d_mult * (a * a) / mult
        sp_neg = _softplus(-lam)
        d_r = d_la * (-LRU_C * sp_neg)
        glam_ref[...] += jnp.sum(d_la * r, axis=0, keepdims=True) * (LRU_C * _sigmoid(-lam))
        d_pa = d_r * r * (1.0 - r)
        d_px = d_i * i * (1.0 - i)
        gba_ref[...] += jnp.sum(d_pa, axis=0, keepdims=True)
        gbx_ref[...] += jnp.sum(d_px, axis=0, keepdims=True)
        parts = []
        for g in range(LRU_NGROUPS):
            cols = slice(g * LRU_GROUP, (g + 1) * LRU_GROUP)
            xg, dpa_g, dpx_g = xl[:, cols], d_pa[:, cols], d_px[:, cols]
            parts.append(_dot(dpa_g, wa_ref[g], NT) + _dot(dpx_g, wx_ref[g], NT))
            gwa_ref[g] += _dot(xg, dpa_g, TN)
            gwx_ref[g] += _dot(xg, dpx_g, TN)
        d_xl = d_xl + jnp.concatenate(parts, axis=1)
        d_lx, gw, gb = _conv_backward(d_xl, nxt_ref[...], lx, hlx, cw)
        nxt_ref[...] = d_xl[0:8, :]
        dlru_ref[:, W:2 * W] = d_lx.astype(dlru_ref.dtype)
        for k in range(CONV_WIDTH):
            gcw_ref[k:k + 1, :] += gw[k]
        gcb_ref[...] += gb

    const = lambda shape: pl.BlockSpec(shape, lambda b: (0,) * len(shape))
    wspec = const((LRU_NGROUPS, LRU_GROUP, LRU_GROUP))
    blk = lambda col: pl.BlockSpec((rows, W), lambda b: (rev(b), col))
    return pl.pallas_call(
        body, name="lru_bwd", grid=(nb,),
        in_specs=[blk(0), blk(1), halo_spec(1), blk(0), halo_spec(0), blk(0),
                  const((CONV_WIDTH, W)), const((1, W)), wspec, wspec, const((1, W)), const((1, W)), const((1, W))],
        out_specs=[pl.BlockSpec((rows, 2 * W), lambda b: (rev(b), 0)), const((CONV_WIDTH, W)), const((1, W)),
                   const((1, W)), const((1, W)), const((1, W)), wspec, wspec],
        out_shape=[jax.ShapeDtypeStruct((t, 2 * W), MXU_DTYPE), jax.ShapeDtypeStruct((CONV_WIDTH, W), F32),
                   jax.ShapeDtypeStruct((1, W), F32), jax.ShapeDtypeStruct((1, W), F32), jax.ShapeDtypeStruct((1, W), F32),
                   jax.ShapeDtypeStruct((1, W), F32), jax.ShapeDtypeStruct((LRU_NGROUPS, LRU_GROUP, LRU_GROUP), F32),
                   jax.ShapeDtypeStruct((LRU_NGROUPS, LRU_GROUP, LRU_GROUP), F32)],
        scratch_shapes=[pltpu.VMEM((1, W), F32), pltpu.VMEM((1, W), F32), pltpu.VMEM((8, W), F32)],
        compiler_params=_params(("arbitrary",)),
    )(proj_lru, proj_lru, proj_lru, h, h, dylru, conv_w, conv_b, wa, wx, ba, bx, lam)


def _mem_scores(q_h, k_h):
    s = _dot(q_h, k_h, NT) * (MEM_HEAD_DIM ** -0.5)
    s = s - jnp.max(s, axis=-1, keepdims=True)
    e = jnp.exp(s)
    return e / jnp.sum(e, axis=-1, keepdims=True)


def _mem_fwd(q, kv, rows=512):
    t = q.shape[0]
    rows = min(rows, t)
    m = kv.shape[0]

    def body(q_ref, kv_ref, y_ref):
        for hd in range(MEM_HEADS):
            cols = slice(hd * MEM_HEAD_DIM, (hd + 1) * MEM_HEAD_DIM)
            vcols = slice(D_MODEL + hd * MEM_HEAD_DIM, D_MODEL + (hd + 1) * MEM_HEAD_DIM)
            p = _mem_scores(q_ref[:, cols], kv_ref[:, cols])
            y_ref[:, cols] = _dot(p, kv_ref[:, vcols], NN).astype(y_ref.dtype)

    return pl.pallas_call(
        body, name="mem_fwd", grid=(t // rows,),
        in_specs=[pl.BlockSpec((rows, D_MODEL), lambda i: (i, 0)), pl.BlockSpec((m, 2 * D_MODEL), lambda i: (0, 0))],
        out_specs=pl.BlockSpec((rows, D_MODEL), lambda i: (i, 0)),
        out_shape=jax.ShapeDtypeStruct((t, D_MODEL), MXU_DTYPE),
        compiler_params=_params(("parallel",)),
    )(q, kv)


def _mem_bwd(q, kv, dy, rows=512):
    t = q.shape[0]
    rows = min(rows, t)
    m = kv.shape[0]

    def body(q_ref, kv_ref, dy_ref, dq_ref, dkv_ref):
        @pl.when(pl.program_id(0) == 0)
        def _():
            dkv_ref[...] = jnp.zeros_like(dkv_ref)

        for hd in range(MEM_HEADS):
            cols = slice(hd * MEM_HEAD_DIM, (hd + 1) * MEM_HEAD_DIM)
            vcols = slice(D_MODEL + hd * MEM_HEAD_DIM, D_MODEL + (hd + 1) * MEM_HEAD_DIM)
            q_h, k_h, dy_h = q_ref[:, cols], kv_ref[:, cols], dy_ref[:, cols]
            p = _mem_scores(q_h, k_h)
            dp = _dot(dy_h, kv_ref[:, vcols], NT)
            dkv_ref[:, vcols] += _dot(p, dy_h, TN)
            ds = p * (dp - jnp.sum(dp * p, axis=-1, keepdims=True)) * (MEM_HEAD_DIM ** -0.5)
            dq_ref[:, cols] = _dot(ds, k_h, NN).astype(dq_ref.dtype)
            dkv_ref[:, cols] += _dot(ds, q_h, TN)

    return pl.pallas_call(
        body, name="mem_bwd", grid=(t // rows,),
        in_specs=[pl.BlockSpec((rows, D_MODEL), lambda i: (i, 0)), pl.BlockSpec((m, 2 * D_MODEL), lambda i: (0, 0)),
                  pl.BlockSpec((rows, D_MODEL), lambda i: (i, 0))],
        out_specs=[pl.BlockSpec((rows, D_MODEL), lambda i: (i, 0)), pl.BlockSpec((m, 2 * D_MODEL), lambda i: (0, 0))],
        out_shape=[jax.ShapeDtypeStruct((t, D_MODEL), MXU_DTYPE), jax.ShapeDtypeStruct((m, 2 * D_MODEL), F32)],
        compiler_params=_params(("arbitrary",)),
    )(q, kv, dy)


def _merge_fwd(x, yssd, ylru, ymem, gl, w_bs, w_bl, w_bm, w_out, fg, tgt, rows=256):
    t = x.shape[0]
    rows = min(rows, t)
    D = D_MODEL

    def body(x_ref, ys_ref, yl_ref, ym_ref, gl_ref, wbs_ref, wbl_ref, wbm_ref, wo_ref, fg_ref, tgt_ref,
             ps_ref, pl_ref, pm_ref, mg_ref, dx2_ref, loss_ref, gfg_ref):
        @pl.when(pl.program_id(0) == 0)
        def _():
            loss_ref[...] = jnp.zeros_like(loss_ref)
            gfg_ref[...] = jnp.zeros_like(gfg_ref)

        ps = _dot(ys_ref[...], wbs_ref[...], NN)
        pl_ = _dot(yl_ref[...], wbl_ref[...], NN)
        pm = _dot(ym_ref[...], wbm_ref[...], NN)
        ps_ref[...] = ps
        pl_ref[...] = pl_
        pm_ref[...] = pm
        merged = (_sigmoid(gl_ref[:, 0:D]) * ps + _sigmoid(gl_ref[:, D:2 * D]) * pl_) + _sigmoid(gl_ref[:, 2 * D:3 * D]) * pm
        mg_ref[...] = merged.astype(mg_ref.dtype)
        x2 = x_ref[...] + _dot(merged, wo_ref[...], NN)
        r2 = lax.rsqrt(jnp.mean(x2 * x2, axis=-1, keepdims=True) + EPS)
        xn = x2 * r2
        fg = fg_ref[...]
        diff = xn * fg - tgt_ref[...]
        tile_loss = 0.5 * jnp.sum(jnp.mean(diff * diff, axis=-1, keepdims=True), axis=0, keepdims=True)
        loss_ref[...] += jnp.broadcast_to(tile_loss, loss_ref.shape)
        d_out = diff * (1.0 / D)
        gfg_ref[...] += jnp.sum(d_out * xn, axis=0, keepdims=True)
        dxn = d_out * fg
        dx2_ref[...] = r2 * (dxn - xn * jnp.mean(dxn * xn, axis=-1, keepdims=True))

    row = lambda w: pl.BlockSpec((rows, w), lambda i: (i, 0))
    const = lambda shape: pl.BlockSpec(shape, lambda i: (0,) * len(shape))
    return pl.pallas_call(
        body, name="merge_fwd", grid=(t // rows,),
        in_specs=[row(D), row(SSD_WIDTH), row(LRU_WIDTH), row(D), row(3 * D), const((SSD_WIDTH, D)), const((LRU_WIDTH, D)),
                  const((D, D)), const((D, D)), const((1, D)), row(D)],
        out_specs=[row(D), row(D), row(D), row(D), row(D), const((1, LANES)), const((1, D))],
        out_shape=[jax.ShapeDtypeStruct((t, D), F32), jax.ShapeDtypeStruct((t, D), F32), jax.ShapeDtypeStruct((t, D), F32),
                   jax.ShapeDtypeStruct((t, D), MXU_DTYPE), jax.ShapeDtypeStruct((t, D), F32),
                   jax.ShapeDtypeStruct((1, LANES), F32), jax.ShapeDtypeStruct((1, D), F32)],
        compiler_params=_params(("arbitrary",)),
    )(x, yssd, ylru, ymem, gl, w_bs, w_bl, w_bm, w_out, fg, tgt)


def _merge_bwd(dx2, gl, ps, pl_in, pm, w_bs, w_bl, w_bm, w_out, rows=256):
    t = dx2.shape[0]
    rows = min(rows, t)
    D = D_MODEL

    def body(dx2_ref, gl_ref, ps_ref, pl_ref, pm_ref, wbs_ref, wbl_ref, wbm_ref, wo_ref,
             dg_ref, dps_ref, dpl_ref, dpm_ref, dys_ref, dyl_ref, dym_ref):
        dm = _dot(dx2_ref[...], wo_ref[...], NT)
        for idx, (p_ref, dp_ref, w_ref, dy_ref) in enumerate(
                ((ps_ref, dps_ref, wbs_ref, dys_ref), (pl_ref, dpl_ref, wbl_ref, dyl_ref), (pm_ref, dpm_ref, wbm_ref, dym_ref))):
            gate = _sigmoid(gl_ref[:, idx * D:(idx + 1) * D])
            dg_ref[:, idx * D:(idx + 1) * D] = ((dm * p_ref[...]) * gate * (1.0 - gate)).astype(dg_ref.dtype)
            dp = dm * gate
            dp_ref[...] = dp.astype(dp_ref.dtype)
            dy_ref[...] = _dot(dp, w_ref[...], NT)

    row = lambda w: pl.BlockSpec((rows, w), lambda i: (i, 0))
    const = lambda shape: pl.BlockSpec(shape, lambda i: (0,) * len(shape))
    return pl.pallas_call(
        body, name="merge_bwd", grid=(t // rows,),
        in_specs=[row(D), row(3 * D), row(D), row(D), row(D), const((SSD_WIDTH, D)), const((LRU_WIDTH, D)),
                  const((D, D)), const((D, D))],
        out_specs=[row(3 * D), row(D), row(D), row(D), row(SSD_WIDTH), row(LRU_WIDTH), row(D)],
        out_shape=[jax.ShapeDtypeStruct((t, 3 * D), MXU_DTYPE), jax.ShapeDtypeStruct((t, D), MXU_DTYPE),
                   jax.ShapeDtypeStruct((t, D), MXU_DTYPE), jax.ShapeDtypeStruct((t, D), MXU_DTYPE),
                   jax.ShapeDtypeStruct((t, SSD_WIDTH), F32), jax.ShapeDtypeStruct((t, LRU_WIDTH), F32),
                   jax.ShapeDtypeStruct((t, D), F32)],
        compiler_params=_params(("parallel",)),
    )(dx2, gl, ps, pl_in, pm, w_bs, w_bl, w_bm, w_out)


FLIPS = tuple((dx, dy, dc) for dx in (0, 1) for dy in (0, 1) for dc in (0, 1))[1:]


def _mesh_place():
    x, y, c = lax.axis_index("x"), lax.axis_index("y"), lax.axis_index("c")
    return x, y, c, 4 * x + 2 * y + c


def _flip(x, y, c, f):
    px = 1 - x if f[0] else x
    py = 1 - y if f[1] else y
    pc = 1 - c if f[2] else c
    return (px, py, pc), 4 * px + 2 * py + pc


def _all_gather(arrs):
    n = len(arrs)

    def body(*refs):
        ins, outs = refs[:n], refs[n:2 * n]
        send_sems, recv_sems, local_sems = refs[2 * n:]
        x, y, c, me = _mesh_place()
        local = [pltpu.make_async_copy(ins[a], outs[a].at[me], local_sems.at[a]) for a in range(n)]
        for cp in local:
            cp.start()
        sends = []
        for k, f in enumerate(FLIPS):
            peer, _ = _flip(x, y, c, f)
            for a in range(n):
                cp = pltpu.make_async_remote_copy(src_ref=ins[a], dst_ref=outs[a].at[me], send_sem=send_sems.at[a, k],
                                                  recv_sem=recv_sems.at[a, k], device_id=peer, device_id_type=pl.DeviceIdType.MESH)
                cp.start()
                sends.append(cp)
        for k, f in enumerate(FLIPS):
            peer, peer_slot = _flip(x, y, c, f)
            for a in range(n):
                pltpu.make_async_remote_copy(src_ref=ins[a], dst_ref=outs[a].at[peer_slot], send_sem=send_sems.at[a, k],
                                             recv_sem=recv_sems.at[a, k], device_id=peer,
                                             device_id_type=pl.DeviceIdType.MESH).wait_recv()
        for cp in sends:
            cp.wait_send()
        for cp in local:
            cp.wait()

    any_spec = pl.BlockSpec(memory_space=pl.ANY)
    return pl.pallas_call(
        body, name="weights_all_gather", in_specs=[any_spec] * n, out_specs=[any_spec] * n,
        out_shape=[jax.ShapeDtypeStruct((N_DEV,) + a.shape, a.dtype) for a in arrs],
        scratch_shapes=[pltpu.SemaphoreType.DMA((n, len(FLIPS))), pltpu.SemaphoreType.DMA((n, len(FLIPS))),
                        pltpu.SemaphoreType.DMA((n,))],
    )(*arrs)


def _grad_exchange(big, small):
    def body(big_ref, small_ref, obig_ref, osmall_ref, send_sems, recv_sems, local_sems):
        x, y, c, me = _mesh_place()
        local = [pltpu.make_async_copy(big_ref.at[me], obig_ref.at[me], local_sems.at[0]),
                 pltpu.make_async_copy(small_ref, osmall_ref.at[me], local_sems.at[1])]
        for cp in local:
            cp.start()
        sends = []
        for k, f in enumerate(FLIPS):
            peer, peer_slot = _flip(x, y, c, f)
            cps = [pltpu.make_async_remote_copy(src_ref=big_ref.at[peer_slot], dst_ref=obig_ref.at[me], send_sem=send_sems.at[0, k],
                                                recv_sem=recv_sems.at[0, k], device_id=peer, device_id_type=pl.DeviceIdType.MESH),
                   pltpu.make_async_remote_copy(src_ref=small_ref, dst_ref=osmall_ref.at[me], send_sem=send_sems.at[1, k],
                                                recv_sem=recv_sems.at[1, k], device_id=peer, device_id_type=pl.DeviceIdType.MESH)]
            for cp in cps:
                cp.start()
            sends += cps
        for k, f in enumerate(FLIPS):
            peer, peer_slot = _flip(x, y, c, f)
            pltpu.make_async_remote_copy(src_ref=big_ref.at[me], dst_ref=obig_ref.at[peer_slot], send_sem=send_sems.at[0, k],
                                         recv_sem=recv_sems.at[0, k], device_id=peer, device_id_type=pl.DeviceIdType.MESH).wait_recv()
            pltpu.make_async_remote_copy(src_ref=small_ref, dst_ref=osmall_ref.at[peer_slot], send_sem=send_sems.at[1, k],
                                         recv_sem=recv_sems.at[1, k], device_id=peer, device_id_type=pl.DeviceIdType.MESH).wait_recv()
        for cp in sends:
            cp.wait_send()
        for cp in local:
            cp.wait()

    any_spec = pl.BlockSpec(memory_space=pl.ANY)
    return pl.pallas_call(
        body, name="grad_exchange", in_specs=[any_spec, any_spec], out_specs=[any_spec, any_spec],
        out_shape=[jax.ShapeDtypeStruct(big.shape, big.dtype), jax.ShapeDtypeStruct((N_DEV,) + small.shape, small.dtype)],
        scratch_shapes=[pltpu.SemaphoreType.DMA((2, len(FLIPS))), pltpu.SemaphoreType.DMA((2, len(FLIPS))),
                        pltpu.SemaphoreType.DMA((2,))],
    )(big, small)


def _sum_slots(parts, name, rows=512):
    s, r, _ = parts.shape
    rows = math.gcd(rows, r)

    def body(p_ref, o_ref):
        acc = p_ref[0]
        for k in range(1, s):
            acc = acc + p_ref[k]
        o_ref[...] = acc

    return pl.pallas_call(
        body, name=name, grid=(r // rows,),
        in_specs=[pl.BlockSpec((s, rows, LANES), lambda i: (0, i, 0))],
        out_specs=pl.BlockSpec((rows, LANES), lambda i: (i, 0)),
        out_shape=jax.ShapeDtypeStruct((r, LANES), F32),
        compiler_params=_params(("parallel",)),
    )(parts)


def _adamw(w, g, m, v, name, rows=512):
    r = w.shape[0]
    rows = math.gcd(rows, r)

    def body(w_ref, g_ref, m_ref, v_ref, d_ref, nm_ref, nv_ref):
        gv = g_ref[...]
        nm = ADAM_B1 * m_ref[...] + (1.0 - ADAM_B1) * gv
        nv = ADAM_B2 * v_ref[...] + (1.0 - ADAM_B2) * (gv * gv)
        m_hat = nm / (1.0 - ADAM_B1 ** ADAM_STEP)
        v_hat = nv / (1.0 - ADAM_B2 ** ADAM_STEP)
        d_ref[...] = -ADAM_LR * (m_hat / (jnp.sqrt(v_hat) + ADAM_EPS) + ADAM_WD * w_ref[...])
        nm_ref[...] = nm
        nv_ref[...] = nv

    spec = pl.BlockSpec((rows, LANES), lambda i: (i, 0))
    shape = jax.ShapeDtypeStruct((r, LANES), F32)
    return pl.pallas_call(
        body, name=name, grid=(r // rows,), in_specs=[spec] * 4, out_specs=[spec] * 3, out_shape=[shape] * 3,
        compiler_params=_params(("parallel",)),
    )(w, g, m, v)


def _pack(arrs, dtype, row_multiple):
    flat = jnp.concatenate([a.reshape(-1).astype(dtype) for a in arrs])
    unit = LANES * row_multiple
    padded = -(-flat.shape[0] // unit) * unit
    return jnp.pad(flat, (0, padded - flat.shape[0])).reshape(-1, LANES)


def _unpack(packed, shapes, lead=()):
    flat = packed.reshape(lead + (-1,))
    out, off = [], 0
    for shp in shapes:
        n = math.prod(shp)
        out.append(flat[..., off:off + n].reshape(lead + tuple(shp)))
        off += n
    return out


def _block_diag_groups(w):
    w4 = w.reshape(LRU_NGROUPS, 4, LRU_BLOCK, LRU_BLOCK)
    eye = jnp.eye(4, dtype=w.dtype)
    return jnp.einsum("gaij,ab->gaibj", w4, eye).reshape(LRU_NGROUPS, LRU_GROUP, LRU_GROUP)


def _block_diag_extract(wg):
    w5 = wg.reshape(LRU_NGROUPS, 4, LRU_BLOCK, 4, LRU_BLOCK)
    idx = jnp.arange(4)
    return w5[:, idx, :, idx, :].transpose(1, 0, 2, 3).reshape(LRU_BLOCKS, LRU_BLOCK, LRU_BLOCK)


BIG = ("w_in", "w_kv", "w_br_ssd", "w_br_lru", "w_br_mem", "w_out")
SMALL_SHARDED = ("ssd_conv_w", "ssd_norm_g", "lru_conv_w")
REPLICATED = ("norm_g", "ssd_conv_b", "ssd_dt_bias", "ssd_a_log", "ssd_d", "lru_conv_b", "lru_w_a", "lru_b_a",
              "lru_w_x", "lru_b_x", "lru_lambda", "mem_norm_g", "final_g")
WEIGHTS = ("norm_g", "w_in", "ssd_conv_w", "ssd_conv_b", "ssd_dt_bias", "ssd_a_log", "ssd_d", "ssd_norm_g", "lru_conv_w",
           "lru_conv_b", "lru_w_a", "lru_b_a", "lru_w_x", "lru_b_x", "lru_lambda", "mem_norm_g", "w_kv", "w_br_ssd",
           "w_br_lru", "w_br_mem", "w_out", "final_g")


def kernel(x, mem, norm_g, w_in, ssd_conv_w, ssd_conv_b, ssd_dt_bias, ssd_a_log, ssd_d, ssd_norm_g, lru_conv_w, lru_conv_b, lru_w_a, lru_b_a, lru_w_x, lru_b_x, lru_lambda, mem_norm_g, w_kv, w_br_ssd, w_br_lru, w_br_mem, w_out, final_g, loss_target, m_norm_g, m_w_in, m_ssd_conv_w, m_ssd_conv_b, m_ssd_dt_bias, m_ssd_a_log, m_ssd_d, m_ssd_norm_g, m_lru_conv_w, m_lru_conv_b, m_lru_w_a, m_lru_b_a, m_lru_w_x, m_lru_b_x, m_lru_lambda, m_mem_norm_g, m_w_kv, m_w_br_ssd, m_w_br_lru, m_w_br_mem, m_w_out, m_final_g, v_norm_g, v_w_in, v_ssd_conv_w, v_ssd_conv_b, v_ssd_dt_bias, v_ssd_a_log, v_ssd_d, v_ssd_norm_g, v_lru_conv_w, v_lru_conv_b, v_lru_w_a, v_lru_b_a, v_lru_w_x, v_lru_b_x, v_lru_lambda, v_mem_norm_g, v_w_kv, v_w_br_ssd, v_w_br_lru, v_w_br_mem, v_w_out, v_final_g):
    env = dict(locals())
    W = {n: env[n] for n in WEIGHTS}
    M = {n: env["m_" + n] for n in WEIGHTS}
    V = {n: env["v_" + n] for n in WEIGHTS}
    me = 4 * lax.axis_index("x") + 2 * lax.axis_index("y") + lax.axis_index("c")
    t = x.shape[1]
    xt = x[0]
    memt = mem[0]
    tgt = loss_target[0]

    big_shapes = [W[n].shape for n in BIG]
    small_shapes = [W[n].shape for n in SMALL_SHARDED]
    gb, gs = _all_gather([_pack([W[n] for n in BIG], MXU_DTYPE, 16), _pack([W[n] for n in SMALL_SHARDED], F32, 8)])
    g_in, g_kv, g_bs, g_bl, g_bm, g_out = _unpack(gb, big_shapes, (N_DEV,))
    g_cw, g_ng, g_lcw = _unpack(gs, small_shapes, (N_DEV,))
    cols = lambda a: jnp.moveaxis(a[:, 0], 0, -2).reshape(a.shape[2:-1] + (-1,))
    rows_ = lambda a: a[:, 0].reshape((-1,) + a.shape[3:])
    w_in_f, w_kv_f = cols(g_in), cols(g_kv)
    w_bs_f, w_bl_f, w_bm_f, w_out_f = rows_(g_bs), rows_(g_bl), rows_(g_bm), rows_(g_out)
    conv_w_f, ssd_ng_f, lru_cw_f = cols(g_cw), cols(g_ng), cols(g_lcw)
    b = SEG_BOUNDS
    w_ssd, w_lru, w_q, w_g = (w_in_f[:, b[0]:b[1]], w_in_f[:, b[2]:b[3]], w_in_f[:, b[3]:b[4]], w_in_f[:, b[4]:b[5]])
    w_dt = jnp.pad(w_in_f[:, b[1]:b[2]], ((0, 0), (0, DT_PAD - SSD_HEADS)))

    pad_heads = lambda a: jnp.pad(a, ((0, 0), (0, LANES - SSD_HEADS)))
    dtb, alog = pad_heads(ssd_dt_bias), pad_heads(ssd_a_log)
    d_row = jnp.repeat(ssd_d, SSD_HEAD_DIM, axis=1)
    ng_row = ssd_ng_f.reshape(1, SSD_WIDTH)
    wa_g, wx_g = _block_diag_groups(lru_w_a[0]), _block_diag_groups(lru_w_x[0])
    ba, bx = lru_b_a.reshape(1, LRU_WIDTH), lru_b_x.reshape(1, LRU_WIDTH)
    fg = final_g.reshape(1, D_MODEL)

    h = _rms_fwd(xt, norm_g, "norm_fwd")
    proj_ssd = _matmul(h, w_ssd, "nn", "proj_ssd", tk=D_MODEL)
    proj_lru = _matmul(h, w_lru, "nn", "proj_lru", tk=D_MODEL)
    proj_q = _matmul(h, w_q, "nn", "proj_q", tk=D_MODEL)
    proj_g = _matmul(h, w_g, "nn", "proj_g", tk=D_MODEL)
    proj_dt = _matmul(h, w_dt, "nn", "proj_dt", tk=D_MODEL)
    mem_n = _rms_fwd(memt, mem_norm_g, "mem_norm_fwd")
    kv = _matmul(mem_n, w_kv_f, "nn", "mem_kv", tk=D_MODEL)
    yssd, y_scan, states = _ssd_fwd(proj_ssd, proj_dt, conv_w_f, ssd_conv_b, dtb, alog, d_row, ng_row)
    ylru, h_lru = _lru_fwd(proj_lru, lru_cw_f, lru_conv_b, wa_g, wx_g, ba, bx, lru_lambda)
    ymem = _mem_fwd(proj_q, kv)
    ps, pl_, pm, merged, dx2, loss_vec, g_fg = _merge_fwd(xt, yssd, ylru, ymem, proj_g, w_bs_f, w_bl_f, w_bm_f, w_out_f, fg, tgt)

    d_g, dps, dpl, dpm, dyssd, dylru, dymem = _merge_bwd(dx2, proj_g, ps, pl_, pm, w_bs_f, w_bl_f, w_bm_f, w_out_f)
    gw_out = _matmul(merged, dx2, "tn", "grad_w_out", tk=1024)
    gw_bs = _matmul(yssd, dps, "tn", "grad_w_br_ssd", tk=1024)
    gw_bl = _matmul(ylru, dpl, "tn", "grad_w_br_lru", tm=LRU_WIDTH, tk=1024)
    gw_bm = _matmul(ymem, dpm, "tn", "grad_w_br_mem", tk=1024)
    d_q, d_kv = _mem_bwd(proj_q, kv, dymem)
    gw_kv = _matmul(mem_n, d_kv, "tn", "grad_w_kv", tk=memt.shape[0])
    d_memn = _matmul(d_kv, w_kv_f, "nt", "d_mem_n", tk=1024)
    _, g_memng = _rms_bwd(memt, d_memn, None, mem_norm_g, "mem_norm_bwd")
    d_lru, gl_cw, gl_cb, g_ba, g_bx, g_lam, gwa_g, gwx_g = _lru_bwd(proj_lru, h_lru, dylru, lru_cw_f, lru_conv_b, wa_g, wx_g, ba, bx, lru_lambda)
    d_ssd, d_dt, gs_cw, gs_cb, g_dtb, g_alog, g_dch, g_ngrow = _ssd_bwd(proj_ssd, proj_dt, y_scan, states, dyssd, conv_w_f, ssd_conv_b, dtb, alog, d_row, ng_row)
    dh = _matmul(d_ssd, w_ssd, "nt", "dh_ssd", tk=1024)
    dh = _matmul(d_lru, w_lru, "nt", "dh_lru", tk=1024, acc_in=dh)
    dh = _matmul(d_q, w_q, "nt", "dh_q", tk=1024, acc_in=dh)
    dh = _matmul(d_g, w_g, "nt", "dh_g", tk=1024, acc_in=dh)
    dh = _matmul(d_dt, w_dt, "nt", "dh_dt", tk=DT_PAD, acc_in=dh)
    gw_ssd = _matmul(h, d_ssd, "tn", "grad_w_in_ssd", tk=1024)
    gw_lru = _matmul(h, d_lru, "tn", "grad_w_in_lru", tk=1024)
    gw_q = _matmul(h, d_q, "tn", "grad_w_in_q", tk=1024)
    gw_g = _matmul(h, d_g, "tn", "grad_w_in_g", tk=1024)
    gw_dt = _matmul(h, d_dt, "tn", "grad_w_in_dt", tk=1024)
    grad_x, g_normg = _rms_bwd(xt, dh, dx2, norm_g, "norm_bwd")

    gw_in = jnp.concatenate([gw_ssd, gw_dt[:, :SSD_HEADS], gw_lru, gw_q, gw_g], axis=1)
    split_cols = lambda a: jnp.moveaxis(a.reshape(a.shape[:-1] + (N_DEV, -1)), -2, 0)
    split_rows = lambda a: a.reshape((N_DEV, -1) + a.shape[1:])
    big_parts = [split_cols(gw_in), split_cols(gw_kv), split_rows(gw_bs), split_rows(gw_bl), split_rows(gw_bm), split_rows(gw_out)]
    big_flat = jnp.concatenate([p.reshape(N_DEV, -1) for p in big_parts], axis=1)
    big_rows = -(-big_flat.shape[1] // (16 * LANES)) * 16
    big_send = jnp.pad(big_flat, ((0, 0), (0, big_rows * LANES - big_flat.shape[1]))).reshape(N_DEV, big_rows, LANES)

    small_grads = {
        "norm_g": g_normg, "ssd_conv_w": gs_cw, "ssd_conv_b": gs_cb, "ssd_dt_bias": g_dtb[:, :SSD_HEADS],
        "ssd_a_log": g_alog[:, :SSD_HEADS], "ssd_d": jnp.sum(g_dch.reshape(SSD_HEADS, SSD_HEAD_DIM), axis=1).reshape(1, SSD_HEADS),
        "ssd_norm_g": g_ngrow.reshape(SSD_GROUPS, -1), "lru_conv_w": gl_cw, "lru_conv_b": gl_cb,
        "lru_w_a": _block_diag_extract(gwa_g), "lru_b_a": g_ba, "lru_w_x": _block_diag_extract(gwx_g), "lru_b_x": g_bx,
        "lru_lambda": g_lam, "mem_norm_g": g_memng, "final_g": g_fg,
    }
    small_names = tuple(small_grads)
    small_send = _pack([small_grads[n] for n in small_names], F32, 8)
    big_recv, small_recv = _grad_exchange(big_send, small_send)

    w_big = _pack([W[n] for n in BIG], F32, 16)
    m_big = _pack([M[n] for n in BIG], F32, 16)
    v_big = _pack([V[n] for n in BIG], F32, 16)
    g_big = _sum_slots(big_recv, "grad_sum_big")
    d_big, nm_big, nv_big = _adamw(w_big, g_big, m_big, v_big, "adamw_big")

    totals = dict(zip(small_names, _unpack(_sum_slots(small_recv, "grad_sum_small"), [small_grads[n].shape for n in small_names])))
    grads = {}
    for n in REPLICATED:
        grads[n] = totals[n].reshape(W[n].shape)
    for n in SMALL_SHARDED:
        width = W[n].shape[-1]
        grads[n] = lax.dynamic_slice_in_dim(totals[n], me * width, width, axis=-1).reshape(W[n].shape)
    small_all = REPLICATED + SMALL_SHARDED
    w_s = _pack([W[n] for n in small_all], F32, 8)
    g_s = _pack([grads[n] for n in small_all], F32, 8)
    m_s = _pack([M[n] for n in small_all], F32, 8)
    v_s = _pack([V[n] for n in small_all], F32, 8)
    d_s, nm_s, nv_s = _adamw(w_s, g_s, m_s, v_s, "adamw_small")

    delta, new_m, new_v = {}, {}, {}
    for packed, names, dst in ((g_big, BIG, grads), (d_big, BIG, delta), (nm_big, BIG, new_m), (nv_big, BIG, new_v),
                               (d_s, small_all, delta), (nm_s, small_all, new_m), (nv_s, small_all, new_v)):
        for n, a in zip(names, _unpack(packed, [W[k].shape for k in names])):
            dst[n] = a

    loss = lax.psum(loss_vec[0, 0], ("x", "y", "c"))
    return (loss, grad_x[None], *[grads[n] for n in WEIGHTS], *[delta[n] for n in WEIGHTS],
            *[new_m[n] for n in WEIGHTS], *[new_v[n] for n in WEIGHTS])
```

```python
import functools
import math

import jax
import jax.numpy as jnp
from jax import lax
from jax.experimental import pallas as pl
from jax.experimental.pallas import tpu as pltpu

F32 = jnp.float32
MXU_DTYPE = jnp.bfloat16

D_MODEL = 1024
EPS = 1e-6
CONV_WIDTH = 4
SSD_WIDTH = 2048
SSD_HEAD_DIM = 64
SSD_HEADS = 32
SSD_GROUPS = 4
SSD_STATE = 128
SSD_CHUNK = 128
SSD_CONV_CH = SSD_WIDTH + 2 * SSD_GROUPS * SSD_STATE
SSD_PAIRS = SSD_HEADS // 2
PAIRS_PER_GROUP = SSD_PAIRS // SSD_GROUPS
LRU_WIDTH = 1536
LRU_BLOCKS = 16
LRU_BLOCK = 96
LRU_GROUP = 4 * LRU_BLOCK
LRU_NGROUPS = LRU_WIDTH // LRU_GROUP
LRU_C = 8.0
LRU_ROWS = 256
MEM_HEADS = 4
MEM_HEAD_DIM = 256
IN_WIDTH = 12320
N_DEV = 8
LANES = 128
SSD_SEG = SSD_WIDTH + SSD_CONV_CH
DT_PAD = LANES
SEG_BOUNDS = (0, 5120, 5152, 8224, 9248, 12320)

ADAM_LR = 0.001
ADAM_B1 = 0.9
ADAM_B2 = 0.999
ADAM_EPS = 1e-08
ADAM_WD = 0.01
ADAM_STEP = 10

VMEM_LIMIT = 56 * 1024 * 1024

NN = (((1,), (0,)), ((), ()))
NT = (((1,), (1,)), ((), ()))
TN = (((0,), (0,)), ((), ()))


def _dot(a, b, dims):
    return lax.dot_general(a.astype(MXU_DTYPE), b.astype(MXU_DTYPE), dims, preferred_element_type=F32)


def _sigmoid(x):
    return 1.0 / (1.0 + jnp.exp(-x))


def _log1p(e):
    u = 1.0 + e
    return jnp.where(u == 1.0, e, jnp.log(u) * (e / jnp.where(u == 1.0, 1.0, u - 1.0)))


def _softplus(x):
    return jnp.maximum(x, 0.0) + _log1p(jnp.exp(-jnp.abs(x)))


def _expm1(x):
    u = jnp.exp(x)
    um1 = u - 1.0
    lg = jnp.log(u)
    safe = jnp.where(um1 == 0.0, 1.0, lg)
    return jnp.where(um1 == 0.0, x, jnp.where(um1 == -1.0, -1.0, um1 * (x / safe)))


def _params(semantics):
    return pltpu.CompilerParams(dimension_semantics=semantics, vmem_limit_bytes=VMEM_LIMIT)


def _shift_down(cur, halo8, k):
    rolled = pltpu.roll(cur, k, 0)
    row8 = lax.broadcasted_iota(jnp.int32, halo8.shape, 0)
    top = jnp.where(row8 >= k, rolled[0:8], pltpu.roll(halo8, k, 0))
    return jnp.concatenate([top, rolled[8:]], axis=0)


def _shift_up(cur, next8, k):
    rows = cur.shape[0]
    rolled = pltpu.roll(cur, rows - k, 0)
    row8 = lax.broadcasted_iota(jnp.int32, next8.shape, 0)
    bot = jnp.where(row8 < 8 - k, rolled[rows - 8:rows], pltpu.roll(next8, 8 - k, 0))
    return jnp.concatenate([rolled[:rows - 8], bot], axis=0)


def _causal_conv(raw, halo8, w, b):
    acc = raw * w[3:4, :] + b
    for k in range(1, CONV_WIDTH):
        acc = acc + _shift_down(raw, halo8, k) * w[3 - k:4 - k, :]
    return acc


def _conv_backward(dco, next8, raw, halo8, w):
    d_raw = dco * w[3:4, :]
    for k in range(1, CONV_WIDTH):
        d_raw = d_raw + _shift_up(dco, next8, k) * w[3 - k:4 - k, :]
    gw = []
    for k in range(CONV_WIDTH):
        shifted = raw if k == 3 else _shift_down(raw, halo8, 3 - k)
        gw.append(jnp.sum(dco * shifted, axis=0, keepdims=True))
    gb = jnp.sum(dco, axis=0, keepdims=True)
    return d_raw, gw, gb


def _cumsum_rows(v):
    rows = v.shape[0]
    row = lax.broadcasted_iota(jnp.int32, v.shape, 0)
    s = 1
    while s < rows:
        v = v + jnp.where(row >= s, pltpu.roll(v, s, 0), 0.0)
        s *= 2
    return v


def _rev_cumsum_rows(v):
    rows = v.shape[0]
    row = lax.broadcasted_iota(jnp.int32, v.shape, 0)
    s = 1
    while s < rows:
        v = v + jnp.where(row < rows - s, pltpu.roll(v, rows - s, 0), 0.0)
        s *= 2
    return v


def _matmul(a, b, mode, name, out_dtype=F32, tm=1024, tn=1024, tk=512, acc_in=None):
    if mode == "nn":
        (m, kk), n = a.shape, b.shape[1]
    elif mode == "nt":
        (m, kk), n = a.shape, b.shape[0]
    else:
        (kk, m), n = a.shape, b.shape[1]
    tm, tn, tk = min(tm, m), min(tn, n), min(tk, kk)
    assert m % tm == 0 and n % tn == 0 and kk % tk == 0, (name, a.shape, b.shape)
    nk = kk // tk
    dims = {"nn": NN, "nt": NT, "tn": TN}[mode]
    a_spec = pl.BlockSpec((tk, tm), lambda i, j, k: (k, i)) if mode == "tn" else pl.BlockSpec((tm, tk), lambda i, j, k: (i, k))
    b_spec = pl.BlockSpec((tn, tk), lambda i, j, k: (j, k)) if mode == "nt" else pl.BlockSpec((tk, tn), lambda i, j, k: (k, j))
    o_spec = pl.BlockSpec((tm, tn), lambda i, j, k: (i, j))
    has_acc = acc_in is not None

    def body(*refs):
        if has_acc:
            a_ref, b_ref, c_ref, o_ref, acc_ref = refs
        else:
            a_ref, b_ref, o_ref, acc_ref = refs
        k = pl.program_id(2)

        @pl.when(k == 0)
        def _():
            acc_ref[...] = c_ref[...].astype(F32) if has_acc else jnp.zeros_like(acc_ref)

        acc_ref[...] += _dot(a_ref[...], b_ref[...], dims)

        @pl.when(k == nk - 1)
        def _():
            o_ref[...] = acc_ref[...].astype(o_ref.dtype)

    args = (a, b) + ((acc_in,) if has_acc else ())
    in_specs = [a_spec, b_spec] + ([o_spec] if has_acc else [])
    return pl.pallas_call(
        body, name=name, grid=(m // tm, n // tn, nk), in_specs=in_specs, out_specs=o_spec,
        out_shape=jax.ShapeDtypeStruct((m, n), out_dtype),
        scratch_shapes=[pltpu.VMEM((tm, tn), F32)],
        compiler_params=_params(("parallel", "parallel", "arbitrary")),
    )(*args)


def _rms_fwd(x, g, name, rows=512):
    t, d = x.shape
    rows = min(rows, t)

    def body(x_ref, g_ref, h_ref):
        xv = x_ref[...]
        r = lax.rsqrt(jnp.mean(xv * xv, axis=-1, keepdims=True) + EPS)
        h_ref[...] = ((xv * r) * g_ref[...]).astype(h_ref.dtype)

    return pl.pallas_call(
        body, name=name, grid=(t // rows,),
        in_specs=[pl.BlockSpec((rows, d), lambda i: (i, 0)), pl.BlockSpec((1, d), lambda i: (0, 0))],
        out_specs=pl.BlockSpec((rows, d), lambda i: (i, 0)),
        out_shape=jax.ShapeDtypeStruct((t, d), MXU_DTYPE),
        compiler_params=_params(("parallel",)),
    )(x, g)


def _rms_bwd(x, dh, dres, g, name, rows=512):
    t, d = x.shape
    rows = min(rows, t)
    has_res = dres is not None

    def body(*refs):
        if has_res:
            x_ref, dh_ref, dr_ref, g_ref, dx_ref, gg_ref = refs
        else:
            x_ref, dh_ref, g_ref, dx_ref, gg_ref = refs

        @pl.when(pl.program_id(0) == 0)
        def _():
            gg_ref[...] = jnp.zeros_like(gg_ref)

        xv = x_ref[...]
        dhv = dh_ref[...]
        r = lax.rsqrt(jnp.mean(xv * xv, axis=-1, keepdims=True) + EPS)
        n = xv * r
        dn = dhv * g_ref[...]
        dx = r * (dn - n * jnp.mean(dn * n, axis=-1, keepdims=True))
        if has_res:
            dx = dx + dr_ref[...]
        dx_ref[...] = dx
        gg_ref[...] += jnp.sum(dhv * n, axis=0, keepdims=True)

    row_spec = pl.BlockSpec((rows, d), lambda i: (i, 0))
    vec_spec = pl.BlockSpec((1, d), lambda i: (0, 0))
    args = (x, dh) + ((dres,) if has_res else ()) + (g,)
    return pl.pallas_call(
        body, name=name, grid=(t // rows,),
        in_specs=[row_spec, row_spec] + ([row_spec] if has_res else []) + [vec_spec],
        out_specs=[row_spec, vec_spec],
        out_shape=[jax.ShapeDtypeStruct((t, d), F32), jax.ShapeDtypeStruct((1, d), F32)],
        compiler_params=_params(("arbitrary",)),
    )(*args)


def _pair_select(lo, m, h0):
    return jnp.where(lo, m[:, h0:h0 + 1], m[:, h0 + 1:h0 + 2])


def _halves(lo, v):
    return (jnp.sum(jnp.where(lo, v, 0.0), axis=1, keepdims=True),
            jnp.sum(jnp.where(lo, 0.0, v), axis=1, keepdims=True))


def _ssd_common(dt_raw, dtb, alog):
    dt = _softplus(dt_raw + dtb)
    aneg = -jnp.exp(alog)
    a_cs = _cumsum_rows(dt * aneg)
    return dt, aneg, a_cs, a_cs.T


def _ssd_specs(nc, rev):
    cidx = (lambda c: nc - 1 - c) if rev else (lambda c: c)
    L = SSD_CHUNK
    return dict(
        z=pl.BlockSpec((L, SSD_WIDTH), lambda c: (cidx(c), 0)),
        xr=pl.BlockSpec((L, SSD_WIDTH), lambda c: (cidx(c), 1)),
        br=pl.BlockSpec((L, 512), lambda c: (cidx(c), 8)),
        cr=pl.BlockSpec((L, 512), lambda c: (cidx(c), 9)),
        dt=pl.BlockSpec((L, DT_PAD), lambda c: (cidx(c), 0)),
        cwx=pl.BlockSpec((CONV_WIDTH, SSD_WIDTH), lambda c: (0, 0)),
        cwb=pl.BlockSpec((CONV_WIDTH, 512), lambda c: (0, 4)),
        cwc=pl.BlockSpec((CONV_WIDTH, 512), lambda c: (0, 5)),
        cbx=pl.BlockSpec((1, SSD_WIDTH), lambda c: (0, 0)),
        cbb=pl.BlockSpec((1, 512), lambda c: (0, 4)),
        cbc=pl.BlockSpec((1, 512), lambda c: (0, 5)),
        vec128=pl.BlockSpec((1, LANES), lambda c: (0, 0)),
        vecw=pl.BlockSpec((1, SSD_WIDTH), lambda c: (0, 0)),
        wide=pl.BlockSpec((L, SSD_WIDTH), lambda c: (cidx(c), 0)),
        states=pl.BlockSpec((1, SSD_PAIRS, 128, SSD_STATE), lambda c: (cidx(c), 0, 0, 0)),
    )


def _ssd_fwd(proj_ssd, dt_p, conv_w, conv_b, dtb, alog, d_row, ng_row):
    t = proj_ssd.shape[0]
    nc = t // SSD_CHUNK
    L = SSD_CHUNK
    sp = _ssd_specs(nc, False)

    def body(z_ref, xr_ref, br_ref, cr_ref, dt_ref, cwx_ref, cwb_ref, cwc_ref, cbx_ref, cbb_ref, cbc_ref,
             dtb_ref, alog_ref, d_ref, ng_ref, yssd_ref, y_ref, st_ref,
             hx_ref, hb_ref, hc_ref, state_ref, yacc_ref):
        @pl.when(pl.program_id(0) == 0)
        def _():
            hx_ref[...] = jnp.zeros_like(hx_ref)
            hb_ref[...] = jnp.zeros_like(hb_ref)
            hc_ref[...] = jnp.zeros_like(hc_ref)
            state_ref[...] = jnp.zeros_like(state_ref)

        xr, br, cr = xr_ref[...], br_ref[...], cr_ref[...]
        px = _causal_conv(xr, hx_ref[...], cwx_ref[...], cbx_ref[...])
        pb = _causal_conv(br, hb_ref[...], cwb_ref[...], cbb_ref[...])
        pc = _causal_conv(cr, hc_ref[...], cwc_ref[...], cbc_ref[...])
        hx_ref[...] = xr[L - 8:L, :]
        hb_ref[...] = br[L - 8:L, :]
        hc_ref[...] = cr[L - 8:L, :]
        xs = px * _sigmoid(px)
        bm = pb * _sigmoid(pb)
        cm = pc * _sigmoid(pc)

        dt, _, a_cs, a_t = _ssd_common(dt_ref[...], dtb_ref[...], alog_ref[...])
        exp_a = jnp.exp(a_cs)
        a_last = a_cs[L - 1:L, :]
        dte = jnp.exp(a_last - a_cs)
        dec = jnp.exp(a_last)

        lane = lax.broadcasted_iota(jnp.int32, (L, LANES), 1)
        sub = lax.broadcasted_iota(jnp.int32, (L, LANES), 0)
        lo = lane < SSD_HEAD_DIM
        causal = sub >= lane
        top = sub < SSD_HEAD_DIM

        for g in range(SSD_GROUPS):
            b_g = bm[:, g * SSD_STATE:(g + 1) * SSD_STATE]
            c_g = cm[:, g * SSD_STATE:(g + 1) * SSD_STATE]
            cb = _dot(c_g, b_g, NT)
            for jj in range(PAIRS_PER_GROUP):
                j = g * PAIRS_PER_GROUP + jj
                h0 = 2 * j
                cols = slice(j * LANES, (j + 1) * LANES)
                xs_p = xs[:, cols]
                xdt = xs_p * _pair_select(lo, dt, h0)
                g0 = jnp.where(causal, jnp.exp(a_cs[:, h0:h0 + 1] - a_t[h0:h0 + 1, :]), 0.0) * cb
                g1 = jnp.where(causal, jnp.exp(a_cs[:, h0 + 1:h0 + 2] - a_t[h0 + 1:h0 + 2, :]), 0.0) * cb
                lhs = jnp.concatenate([g0, g1], axis=1)
                rhs = jnp.concatenate([jnp.where(lo, xdt, 0.0), jnp.where(lo, 0.0, xdt)], axis=0)
                y_diag = _dot(lhs, rhs, NN)
                h_p = state_ref[j]
                st_ref[0, j] = h_p
                y_off = _dot(c_g, h_p, NT) * _pair_select(lo, exp_a, h0)
                s_new = _dot(xdt * _pair_select(lo, dte, h0), b_g, TN)
                dec_rows = jnp.where(top, dec[:, h0:h0 + 1], dec[:, h0 + 1:h0 + 2])
                state_ref[j] = h_p * dec_rows + s_new
                yacc_ref[:, cols] = (y_diag + y_off) + xs_p * d_ref[:, cols]

        y = yacc_ref[...]
        y_ref[...] = y
        zz = z_ref[...]
        y2 = y * (zz * _sigmoid(zz))
        gw = SSD_WIDTH // SSD_GROUPS
        for g in range(SSD_GROUPS):
            seg = y2[:, g * gw:(g + 1) * gw]
            r = lax.rsqrt(jnp.mean(seg * seg, axis=-1, keepdims=True) + EPS)
            yssd_ref[:, g * gw:(g + 1) * gw] = ((seg * r) * ng_ref[:, g * gw:(g + 1) * gw]).astype(yssd_ref.dtype)

    return pl.pallas_call(
        body, name="ssd_fwd", grid=(nc,),
        in_specs=[sp["z"], sp["xr"], sp["br"], sp["cr"], sp["dt"], sp["cwx"], sp["cwb"], sp["cwc"],
                  sp["cbx"], sp["cbb"], sp["cbc"], sp["vec128"], sp["vec128"], sp["vecw"], sp["vecw"]],
        out_specs=[sp["wide"], sp["wide"], sp["states"]],
        out_shape=[jax.ShapeDtypeStruct((t, SSD_WIDTH), MXU_DTYPE), jax.ShapeDtypeStruct((t, SSD_WIDTH), F32),
                   jax.ShapeDtypeStruct((nc, SSD_PAIRS, 128, SSD_STATE), F32)],
        scratch_shapes=[pltpu.VMEM((8, SSD_WIDTH), F32), pltpu.VMEM((8, 512), F32), pltpu.VMEM((8, 512), F32),
                        pltpu.VMEM((SSD_PAIRS, 128, SSD_STATE), F32), pltpu.VMEM((L, SSD_WIDTH), F32)],
        compiler_params=_params(("arbitrary",)),
    )(proj_ssd, proj_ssd, proj_ssd, proj_ssd, dt_p, conv_w, conv_w, conv_w, conv_b, conv_b, conv_b,
      dtb, alog, d_row, ng_row)


def _ssd_bwd(proj_ssd, dt_p, y, states, dyssd, conv_w, conv_b, dtb, alog, d_row, ng_row):
    t = proj_ssd.shape[0]
    nc = t // SSD_CHUNK
    L = SSD_CHUNK
    sp = _ssd_specs(nc, True)
    groups8 = L // 8

    def halo_spec(width, col):
        return pl.BlockSpec((8, width), lambda c: (jnp.maximum((nc - 1 - c) * groups8 - 1, 0), col))

    def body(z_ref, xr_ref, br_ref, cr_ref, hx_ref, hb_ref, hc_ref, dt_ref, y_ref, st_ref, dy_ref,
             cwx_ref, cwb_ref, cwc_ref, cbx_ref, cbb_ref, cbc_ref, dtb_ref, alog_ref, d_ref, ng_ref,
             dssd_ref, ddt_ref, gcw_ref, gcb_ref, gdtb_ref, galog_ref, gd_ref, gng_ref,
             gn_ref, nx_ref, nb_ref, ncc_ref, dxs_ref):
        step = pl.program_id(0)

        @pl.when(step == 0)
        def _():
            gn_ref[...] = jnp.zeros_like(gn_ref)
            nx_ref[...] = jnp.zeros_like(nx_ref)
            nb_ref[...] = jnp.zeros_like(nb_ref)
            ncc_ref[...] = jnp.zeros_like(ncc_ref)
            for ref in (gcw_ref, gcb_ref, gdtb_ref, galog_ref, gd_ref, gng_ref):
                ref[...] = jnp.zeros_like(ref)

        first_chunk = step == nc - 1
        keep = jnp.where(first_chunk, 0.0, 1.0)
        xr, br, cr = xr_ref[...], br_ref[...], cr_ref[...]
        hx, hb, hc = hx_ref[...] * keep, hb_ref[...] * keep, hc_ref[...] * keep
        cwx, cwb, cwc = cwx_ref[...], cwb_ref[...], cwc_ref[...]
        px = _causal_conv(xr, hx, cwx, cbx_ref[...])
        pb = _causal_conv(br, hb, cwb, cbb_ref[...])
        pc = _causal_conv(cr, hc, cwc, cbc_ref[...])
        sx, sb, sc = _sigmoid(px), _sigmoid(pb), _sigmoid(pc)
        xs, bm, cm = px * sx, pb * sb, pc * sc

        dt_in = dt_ref[...] + dtb_ref[...]
        dt, aneg, a_cs, a_t = _ssd_common(dt_ref[...], dtb_ref[...], alog_ref[...])
        exp_a = jnp.exp(a_cs)
        a_last = a_cs[L - 1:L, :]
        dte = jnp.exp(a_last - a_cs)
        dec = jnp.exp(a_last)

        lane = lax.broadcasted_iota(jnp.int32, (L, LANES), 1)
        sub = lax.broadcasted_iota(jnp.int32, (L, LANES), 0)
        lo = lane < SSD_HEAD_DIM
        causal = sub >= lane
        top = sub < SSD_HEAD_DIM
        last_row = sub == L - 1

        yv = y_ref[...]
        zz = z_ref[...]
        sz = _sigmoid(zz)
        silz = zz * sz
        y2 = yv * silz
        dyv = dy_ref[...]
        gw = SSD_WIDTH // SSD_GROUPS
        d_y2_parts = []
        gng_parts = []
        for g in range(SSD_GROUPS):
            seg = y2[:, g * gw:(g + 1) * gw]
            dseg = dyv[:, g * gw:(g + 1) * gw]
            r = lax.rsqrt(jnp.mean(seg * seg, axis=-1, keepdims=True) + EPS)
            n = seg * r
            dn = dseg * ng_ref[:, g * gw:(g + 1) * gw]
            gng_parts.append(jnp.sum(dseg * n, axis=0, keepdims=True))
            d_y2_parts.append(r * (dn - n * jnp.mean(dn * n, axis=-1, keepdims=True)))
        d_y2 = jnp.concatenate(d_y2_parts, axis=1)
        gng_ref[...] += jnp.concatenate(gng_parts, axis=1)
        d_y = d_y2 * silz
        dssd_ref[:, 0:SSD_WIDTH] = (d_y2 * yv * (sz * (1.0 + zz * (1.0 - sz)))).astype(dssd_ref.dtype)
        gd_ref[...] += jnp.sum(d_y * xs, axis=0, keepdims=True)
        dxs_ref[...] = d_y * d_ref[...]

        d_a = jnp.zeros((L, LANES), F32)
        d_at = jnp.zeros((LANES, L), F32)
        ddt = jnp.zeros((L, LANES), F32)
        d_b_parts, d_c_parts = [], []
        for g in range(SSD_GROUPS):
            b_g = bm[:, g * SSD_STATE:(g + 1) * SSD_STATE]
            c_g = cm[:, g * SSD_STATE:(g + 1) * SSD_STATE]
            cb = _dot(c_g, b_g, NT)
            d_cb = jnp.zeros((L, L), F32)
            d_bg = jnp.zeros((L, SSD_STATE), F32)
            d_cg = jnp.zeros((L, SSD_STATE), F32)
            for jj in range(PAIRS_PER_GROUP):
                j = g * PAIRS_PER_GROUP + jj
                h0 = 2 * j
                cols = slice(j * LANES, (j + 1) * LANES)
                dy_p = d_y[:, cols]
                xs_p = xs[:, cols]
                dt_pp = _pair_select(lo, dt, h0)
                expa_p = _pair_select(lo, exp_a, h0)
                dte_p = _pair_select(lo, dte, h0)
                xdt = xs_p * dt_pp
                l0 = jnp.where(causal, jnp.exp(a_cs[:, h0:h0 + 1] - a_t[h0:h0 + 1, :]), 0.0)
                l1 = jnp.where(causal, jnp.exp(a_cs[:, h0 + 1:h0 + 2] - a_t[h0 + 1:h0 + 2, :]), 0.0)
                g0, g1 = l0 * cb, l1 * cb
                h_p = st_ref[0, j]
                gn_p = gn_ref[j]
                dys = dy_p * expa_p
                d_cg = d_cg + _dot(dys, h_p, NN)
                d_h = _dot(dys, c_g, TN)
                t1 = dy_p * _dot(c_g, h_p, NT) * expa_p
                dw = _dot(b_g, gn_p, NT)
                d_bg = d_bg + _dot(xdt * dte_p, gn_p, NN)
                d_xdt = dw * dte_p
                t2 = d_xdt * xdt
                dyl, dyh = jnp.where(lo, dy_p, 0.0), jnp.where(lo, 0.0, dy_p)
                d_xdt = d_xdt + _dot(jnp.concatenate([g0, g1], axis=0), jnp.concatenate([dyl, dyh], axis=0), TN)
                dm0 = _dot(dyl, xdt, NT)
                dm1 = _dot(dyh, xdt, NT)
                d_cb = d_cb + (l0 * dm0 + l1 * dm1)
                e0, e1 = dm0 * g0, dm1 * g1
                a0, a1 = _halves(lo, t1 - t2)
                a0 = a0 + jnp.sum(e0, axis=1, keepdims=True)
                a1 = a1 + jnp.sum(e1, axis=1, keepdims=True)
                s0, s1 = _halves(lo, t2)
                gh = jnp.sum(gn_p * h_p, axis=1, keepdims=True)
                dd0 = jnp.sum(jnp.where(top[:, 0:1], gh, 0.0), axis=0, keepdims=True)
                dd1 = jnp.sum(jnp.where(top[:, 0:1], 0.0, gh), axis=0, keepdims=True)
                end0 = jnp.sum(s0, axis=0, keepdims=True) + dd0 * dec[:, h0:h0 + 1]
                end1 = jnp.sum(s1, axis=0, keepdims=True) + dd1 * dec[:, h0 + 1:h0 + 2]
                d_a = d_a + jnp.where(lane == h0, a0 + jnp.where(last_row, end0, 0.0), 0.0)
                d_a = d_a + jnp.where(lane == h0 + 1, a1 + jnp.where(last_row, end1, 0.0), 0.0)
                d_at = d_at - jnp.where(sub == h0, jnp.sum(e0, axis=0, keepdims=True), 0.0)
                d_at = d_at - jnp.where(sub == h0 + 1, jnp.sum(e1, axis=0, keepdims=True), 0.0)
                dec_rows = jnp.where(top, dec[:, h0:h0 + 1], dec[:, h0 + 1:h0 + 2])
                gn_ref[j] = d_h + dec_rows * gn_p
                q0, q1 = _halves(lo, d_xdt * xs_p)
                ddt = ddt + jnp.where(lane == h0, q0, 0.0) + jnp.where(lane == h0 + 1, q1, 0.0)
                dxs_ref[:, cols] += d_xdt * dt_pp
            d_cg = d_cg + _dot(d_cb, b_g, NN)
            d_bg = d_bg + _dot(d_cb, c_g, TN)
            d_b_parts.append(d_bg)
            d_c_parts.append(d_cg)

        rc = _rev_cumsum_rows(d_a + d_at.T)
        d_dt = rc * aneg + ddt
        galog_ref[...] += jnp.sum(rc * dt, axis=0, keepdims=True) * aneg
        d_dtraw = d_dt * _sigmoid(dt_in)
        gdtb_ref[...] += jnp.sum(d_dtraw, axis=0, keepdims=True)
        ddt_ref[...] = d_dtraw.astype(ddt_ref.dtype)

        def dsilu(p, s):
            return s * (1.0 + p * (1.0 - s))

        dcx = dxs_ref[...] * dsilu(px, sx)
        dcb = jnp.concatenate(d_b_parts, axis=1) * dsilu(pb, sb)
        dcc = jnp.concatenate(d_c_parts, axis=1) * dsilu(pc, sc)
        drx, gwx, gbx = _conv_backward(dcx, nx_ref[...], xr, hx, cwx)
        drb, gwb, gbb = _conv_backward(dcb, nb_ref[...], br, hb, cwb)
        drc, gwc, gbc = _conv_backward(dcc, ncc_ref[...], cr, hc, cwc)
        nx_ref[...] = dcx[0:8, :]
        nb_ref[...] = dcb[0:8, :]
        ncc_ref[...] = dcc[0:8, :]
        dssd_ref[:, SSD_WIDTH:2 * SSD_WIDTH] = drx.astype(dssd_ref.dtype)
        dssd_ref[:, 2 * SSD_WIDTH:2 * SSD_WIDTH + 512] = drb.astype(dssd_ref.dtype)
        dssd_ref[:, 2 * SSD_WIDTH + 512:SSD_SEG] = drc.astype(dssd_ref.dtype)
        for k in range(CONV_WIDTH):
            gcw_ref[k:k + 1, :] += jnp.concatenate([gwx[k], gwb[k], gwc[k]], axis=1)
        gcb_ref[...] += jnp.concatenate([gbx, gbb, gbc], axis=1)

    const = lambda shape: pl.BlockSpec(shape, lambda c: (0,) * len(shape))
    return pl.pallas_call(
        body, name="ssd_bwd", grid=(nc,),
        in_specs=[sp["z"], sp["xr"], sp["br"], sp["cr"], halo_spec(SSD_WIDTH, 1), halo_spec(512, 8), halo_spec(512, 9),
                  sp["dt"], sp["wide"], sp["states"], sp["wide"],
                  sp["cwx"], sp["cwb"], sp["cwc"], sp["cbx"], sp["cbb"], sp["cbc"],
                  sp["vec128"], sp["vec128"], sp["vecw"], sp["vecw"]],
        out_specs=[pl.BlockSpec((L, SSD_SEG), lambda c: (nc - 1 - c, 0)), sp["dt"],
                   const((CONV_WIDTH, SSD_CONV_CH)), const((1, SSD_CONV_CH)), const((1, LANES)), const((1, LANES)),
                   const((1, SSD_WIDTH)), const((1, SSD_WIDTH))],
        out_shape=[jax.ShapeDtypeStruct((t, SSD_SEG), MXU_DTYPE), jax.ShapeDtypeStruct((t, DT_PAD), MXU_DTYPE),
                   jax.ShapeDtypeStruct((CONV_WIDTH, SSD_CONV_CH), F32), jax.ShapeDtypeStruct((1, SSD_CONV_CH), F32),
                   jax.ShapeDtypeStruct((1, LANES), F32), jax.ShapeDtypeStruct((1, LANES), F32),
                   jax.ShapeDtypeStruct((1, SSD_WIDTH), F32), jax.ShapeDtypeStruct((1, SSD_WIDTH), F32)],
        scratch_shapes=[pltpu.VMEM((SSD_PAIRS, 128, SSD_STATE), F32), pltpu.VMEM((8, SSD_WIDTH), F32),
                        pltpu.VMEM((8, 512), F32), pltpu.VMEM((8, 512), F32), pltpu.VMEM((L, SSD_WIDTH), F32)],
        compiler_params=_params(("arbitrary",)),
    )(proj_ssd, proj_ssd, proj_ssd, proj_ssd, proj_ssd, proj_ssd, proj_ssd, dt_p, y, states, dyssd,
      conv_w, conv_w, conv_w, conv_b, conv_b, conv_b, dtb, alog, d_row, ng_row)


def _lru_gates(xl, wa_ref, wx_ref, ba, bx, lam):
    pre_a, pre_x = [], []
    for g in range(LRU_NGROUPS):
        xg = xl[:, g * LRU_GROUP:(g + 1) * LRU_GROUP]
        pre_a.append(_dot(xg, wa_ref[g], NN))
        pre_x.append(_dot(xg, wx_ref[g], NN))
    r = _sigmoid(jnp.concatenate(pre_a, axis=1) + ba)
    i = _sigmoid(jnp.concatenate(pre_x, axis=1) + bx)
    log_a = (-LRU_C * r) * _softplus(-lam)
    a = jnp.exp(log_a)
    mult = jnp.sqrt(-_expm1(2.0 * log_a))
    return r, i, log_a, a, mult


def _lru_fwd(proj_lru, conv_w, conv_b, wa, wx, ba, bx, lam):
    t = proj_lru.shape[0]
    rows = min(LRU_ROWS, t)
    nb = t // rows
    W = LRU_WIDTH

    def body(lg_ref, lx_ref, cw_ref, cb_ref, wa_ref, wx_ref, ba_ref, bx_ref, lam_ref, ylru_ref, h_ref,
             halo_ref, carry_ref):
        @pl.when(pl.program_id(0) == 0)
        def _():
            halo_ref[...] = jnp.zeros_like(halo_ref)
            carry_ref[...] = jnp.zeros_like(carry_ref)

        lx = lx_ref[...]
        xl = _causal_conv(lx, halo_ref[...], cw_ref[...], cb_ref[...])
        halo_ref[...] = lx[rows - 8:rows, :]
        _, i, _, a, mult = _lru_gates(xl, wa_ref, wx_ref, ba_ref[...], bx_ref[...], lam_ref[...])
        u = mult * (i * xl)
        row = lax.broadcasted_iota(jnp.int32, (rows, W), 0)
        p = a
        s = 1
        while s < rows:
            ok = row >= s
            u = p * jnp.where(ok, pltpu.roll(u, s, 0), 0.0) + u
            p = p * jnp.where(ok, pltpu.roll(p, s, 0), 1.0)
            s *= 2
        h = p * carry_ref[...] + u
        carry_ref[...] = h[rows - 1:rows, :]
        h_ref[...] = h
        lg = lg_ref[...]
        ylru_ref[...] = (h * (lg * _sigmoid(lg))).astype(ylru_ref.dtype)

    const = lambda shape: pl.BlockSpec(shape, lambda b: (0,) * len(shape))
    return pl.pallas_call(
        body, name="lru_fwd", grid=(nb,),
        in_specs=[pl.BlockSpec((rows, W), lambda b: (b, 0)), pl.BlockSpec((rows, W), lambda b: (b, 1)),
                  const((CONV_WIDTH, W)), const((1, W)), const((LRU_NGROUPS, LRU_GROUP, LRU_GROUP)),
                  const((LRU_NGROUPS, LRU_GROUP, LRU_GROUP)), const((1, W)), const((1, W)), const((1, W))],
        out_specs=[pl.BlockSpec((rows, W), lambda b: (b, 0)), pl.BlockSpec((rows, W), lambda b: (b, 0))],
        out_shape=[jax.ShapeDtypeStruct((t, W), MXU_DTYPE), jax.ShapeDtypeStruct((t, W), F32)],
        scratch_shapes=[pltpu.VMEM((8, W), F32), pltpu.VMEM((1, W), F32)],
        compiler_params=_params(("arbitrary",)),
    )(proj_lru, proj_lru, conv_w, conv_b, wa, wx, ba, bx, lam)


def _lru_bwd(proj_lru, h, dylru, conv_w, conv_b, wa, wx, ba, bx, lam):
    t = proj_lru.shape[0]
    rows = min(LRU_ROWS, t)
    nb = t // rows
    W = LRU_WIDTH
    groups8 = rows // 8

    def rev(b):
        return nb - 1 - b

    def halo_spec(col):
        return pl.BlockSpec((8, W), lambda b: (jnp.maximum(rev(b) * groups8 - 1, 0), col))

    def body(lg_ref, lx_ref, hlx_ref, h_ref, hh_ref, dy_ref, cw_ref, cb_ref, wa_ref, wx_ref, ba_ref, bx_ref, lam_ref,
             dlru_ref, gcw_ref, gcb_ref, gba_ref, gbx_ref, glam_ref, gwa_ref, gwx_ref,
             gcarry_ref, afirst_ref, nxt_ref):
        step = pl.program_id(0)

        @pl.when(step == 0)
        def _():
            gcarry_ref[...] = jnp.zeros_like(gcarry_ref)
            afirst_ref[...] = jnp.zeros_like(afirst_ref)
            nxt_ref[...] = jnp.zeros_like(nxt_ref)
            for ref in (gcw_ref, gcb_ref, gba_ref, gbx_ref, glam_ref, gwa_ref, gwx_ref):
                ref[...] = jnp.zeros_like(ref)

        keep = jnp.where(step == nb - 1, 0.0, 1.0)
        lx = lx_ref[...]
        hlx = hlx_ref[...] * keep
        cw = cw_ref[...]
        xl = _causal_conv(lx, hlx, cw, cb_ref[...])
        lam = lam_ref[...]
        r, i, log_a, a, mult = _lru_gates(xl, wa_ref, wx_ref, ba_ref[...], bx_ref[...], lam)
        hv = h_ref[...]
        h_prev = _shift_down(hv, hh_ref[...] * keep, 1)
        lg = lg_ref[...]
        sg = _sigmoid(lg)
        dyv = dy_ref[...]
        d_h = dyv * (lg * sg)
        dlru_ref[:, 0:W] = (dyv * hv * (sg * (1.0 + lg * (1.0 - sg)))).astype(dlru_ref.dtype)

        row = lax.broadcasted_iota(jnp.int32, (rows, W), 0)
        p = jnp.where(row < rows - 1, pltpu.roll(a, rows - 1, 0), afirst_ref[...])
        u = d_h
        s = 1
        while s < rows:
            ok = row < rows - s
            u = p * jnp.where(ok, pltpu.roll(u, rows - s, 0), 0.0) + u
            p = p * jnp.where(ok, pltpu.roll(p, rows - s, 0), 1.0)
            s *= 2
        gsc = p * gcarry_ref[...] + u
        gcarry_ref[...] = gsc[0:1, :]
        afirst_ref[...] = a[0:1, :]

        d_a = gsc * h_prev
        v = i * xl
        d_mult = gsc * v
        d_v = gsc * mult
        d_i = d_v * xl
        d_xl = d_v * i
        d_la = d_a * a - d_mult * (a * a) / mult
        sp_neg = _softplus(-lam)
        d_r = d_la * (-LRU_C * sp_neg)
        glam_ref[...] += jnp.sum(d_la * r, axis=0, keepdims=True) * (LRU_C * _sigmoid(-lam))
        d_pa = d_r * r * (1.0 - r)
        d_px = d_i * i * (1.0 - i)
        gba_ref[...] += jnp.sum(d_pa, axis=0, keepdims=True)
        gbx_ref[...] += jnp.sum(d_px, axis=0, keepdims=True)
        parts = []
        for g in range(LRU_NGROUPS):
            cols = slice(g * LRU_GROUP, (g + 1) * LRU_GROUP)
            xg, dpa_g, dpx_g = xl[:, cols], d_pa[:, cols], d_px[:, cols]
            parts.append(_dot(dpa_g, wa_ref[g], NT) + _dot(dpx_g, wx_ref[g], NT))
            gwa_ref[g] += _dot(xg, dpa_g, TN)
            gwx_ref[g] += _dot(xg, dpx_g, TN)
        d_xl = d_xl + jnp.concatenate(parts, axis=1)
        d_lx, gw, gb = _conv_backward(d_xl, nxt_ref[...], lx, hlx, cw)
        nxt_ref[...] = d_xl[0:8, :]
        dlru_ref[:, W:2 * W] = d_lx.astype(dlru_ref.dtype)
        for k in range(CONV_WIDTH):
            gcw_ref[k:k + 1, :] += gw[k]
        gcb_ref[...] += gb

    const = lambda shape: pl.BlockSpec(shape, lambda b: (0,) * len(shape))
    wspec = const((LRU_NGROUPS, LRU_GROUP, LRU_GROUP))
    blk = lambda col: pl.BlockSpec((rows, W), lambda b: (rev(b), col))
    return pl.pallas_call(
        body, name="lru_bwd", grid=(nb,),
        in_specs=[blk(0), blk(1), halo_spec(1), blk(0), halo_spec(0), blk(0),
                  const((CONV_WIDTH, W)), const((1, W)), wspec, wspec, const((1, W)), const((1, W)), const((1, W))],
        out_specs=[pl.BlockSpec((rows, 2 * W), lambda b: (rev(b), 0)), const((CONV_WIDTH, W)), const((1, W)),
                   const((1, W)), const((1, W)), const((1, W)), wspec, wspec],
        out_shape=[jax.ShapeDtypeStruct((t, 2 * W), MXU_DTYPE), jax.ShapeDtypeStruct((CONV_WIDTH, W), F32),
                   jax.ShapeDtypeStruct((1, W), F32), jax.ShapeDtypeStruct((1, W), F32), jax.ShapeDtypeStruct((1, W), F32),
                   jax.ShapeDtypeStruct((1, W), F32), jax.ShapeDtypeStruct((LRU_NGROUPS, LRU_GROUP, LRU_GROUP), F32),
                   jax.ShapeDtypeStruct((LRU_NGROUPS, LRU_GROUP, LRU_GROUP), F32)],
        scratch_shapes=[pltpu.VMEM((1, W), F32), pltpu.VMEM((1, W), F32), pltpu.VMEM((8, W), F32)],
        compiler_params=_params(("arbitrary",)),
    )(proj_lru, proj_lru, proj_lru, h, h, dylru, conv_w, conv_b, wa, wx, ba, bx, lam)


def _mem_scores(q_h, k_h):
    s = _dot(q_h, k_h, NT) * (MEM_HEAD_DIM ** -0.5)
    s = s - jnp.max(s, axis=-1, keepdims=True)
    e = jnp.exp(s)
    return e / jnp.sum(e, axis=-1, keepdims=True)


def _mem_fwd(q, kv, rows=512):
    t = q.shape[0]
    rows = min(rows, t)
    m = kv.shape[0]

    def body(q_ref, kv_ref, y_ref):
        for hd in range(MEM_HEADS):
            cols = slice(hd * MEM_HEAD_DIM, (hd + 1) * MEM_HEAD_DIM)
            vcols = slice(D_MODEL + hd * MEM_HEAD_DIM, D_MODEL + (hd + 1) * MEM_HEAD_DIM)
            p = _mem_scores(q_ref[:, cols], kv_ref[:, cols])
            y_ref[:, cols] = _dot(p, kv_ref[:, vcols], NN).astype(y_ref.dtype)

    return pl.pallas_call(
        body, name="mem_fwd", grid=(t // rows,),
        in_specs=[pl.BlockSpec((rows, D_MODEL), lambda i: (i, 0)), pl.BlockSpec((m, 2 * D_MODEL), lambda i: (0, 0))],
        out_specs=pl.BlockSpec((rows, D_MODEL), lambda i: (i, 0)),
        out_shape=jax.ShapeDtypeStruct((t, D_MODEL), MXU_DTYPE),
        compiler_params=_params(("parallel",)),
    )(q, kv)


def _mem_bwd(q, kv, dy, rows=512):
    t = q.shape[0]
    rows = min(rows, t)
    m = kv.shape[0]

    def body(q_ref, kv_ref, dy_ref, dq_ref, dkv_ref):
        @pl.when(pl.program_id(0) == 0)
        def _():
            dkv_ref[...] = jnp.zeros_like(dkv_ref)

        for hd in range(MEM_HEADS):
            cols = slice(hd * MEM_HEAD_DIM, (hd + 1) * MEM_HEAD_DIM)
            vcols = slice(D_MODEL + hd * MEM_HEAD_DIM, D_MODEL + (hd + 1) * MEM_HEAD_DIM)
            q_h, k_h, dy_h = q_ref[:, cols], kv_ref[:, cols], dy_ref[:, cols]
            p = _mem_scores(q_h, k_h)
            dp = _dot(dy_h, kv_ref[:, vcols], NT)
            dkv_ref[:, vcols] += _dot(p, dy_h, TN)
            ds = p * (dp - jnp.sum(dp * p, axis=-1, keepdims=True)) * (MEM_HEAD_DIM ** -0.5)
            dq_ref[:, cols] = _dot(ds, k_h, NN).astype(dq_ref.dtype)
            dkv_ref[:, cols] += _dot(ds, q_h, TN)

    return pl.pallas_call(
        body, name="mem_bwd", grid=(t // rows,),
        in_specs=[pl.BlockSpec((rows, D_MODEL), lambda i: (i, 0)), pl.BlockSpec((m, 2 * D_MODEL), lambda i: (0, 0)),
                  pl.BlockSpec((rows, D_MODEL), lambda i: (i, 0))],
        out_specs=[pl.BlockSpec((rows, D_MODEL), lambda i: (i, 0)), pl.BlockSpec((m, 2 * D_MODEL), lambda i: (0, 0))],
        out_shape=[jax.ShapeDtypeStruct((t, D_MODEL), MXU_DTYPE), jax.ShapeDtypeStruct((m, 2 * D_MODEL), F32)],
        compiler_params=_params(("arbitrary",)),
    )(q, kv, dy)


def _merge_fwd(x, yssd, ylru, ymem, gl, w_bs, w_bl, w_bm, w_out, fg, tgt, rows=256):
    t = x.shape[0]
    rows = min(rows, t)
    D = D_MODEL

    def body(x_ref, ys_ref, yl_ref, ym_ref, gl_ref, wbs_ref, wbl_ref, wbm_ref, wo_ref, fg_ref, tgt_ref,
             ps_ref, pl_ref, pm_ref, mg_ref, dx2_ref, loss_ref, gfg_ref):
        @pl.when(pl.program_id(0) == 0)
        def _():
            loss_ref[...] = jnp.zeros_like(loss_ref)
            gfg_ref[...] = jnp.zeros_like(gfg_ref)

        ps = _dot(ys_ref[...], wbs_ref[...], NN)
        pl_ = _dot(yl_ref[...], wbl_ref[...], NN)
        pm = _dot(ym_ref[...], wbm_ref[...], NN)
        ps_ref[...] = ps
        pl_ref[...] = pl_
        pm_ref[...] = pm
        merged = (_sigmoid(gl_ref[:, 0:D]) * ps + _sigmoid(gl_ref[:, D:2 * D]) * pl_) + _sigmoid(gl_ref[:, 2 * D:3 * D]) * pm
        mg_ref[...] = merged.astype(mg_ref.dtype)
        x2 = x_ref[...] + _dot(merged, wo_ref[...], NN)
        r2 = lax.rsqrt(jnp.mean(x2 * x2, axis=-1, keepdims=True) + EPS)
        xn = x2 * r2
        fg = fg_ref[...]
        diff = xn * fg - tgt_ref[...]
        tile_loss = 0.5 * jnp.sum(jnp.mean(diff * diff, axis=-1, keepdims=True), axis=0, keepdims=True)
        loss_ref[...] += jnp.broadcast_to(tile_loss, loss_ref.shape)
        d_out = diff * (1.0 / D)
        gfg_ref[...] += jnp.sum(d_out * xn, axis=0, keepdims=True)
        dxn = d_out * fg
        dx2_ref[...] = r2 * (dxn - xn * jnp.mean(dxn * xn, axis=-1, keepdims=True))

    row = lambda w: pl.BlockSpec((rows, w), lambda i: (i, 0))
    const = lambda shape: pl.BlockSpec(shape, lambda i: (0,) * len(shape))
    return pl.pallas_call(
        body, name="merge_fwd", grid=(t // rows,),
        in_specs=[row(D), row(SSD_WIDTH), row(LRU_WIDTH), row(D), row(3 * D), const((SSD_WIDTH, D)), const((LRU_WIDTH, D)),
                  const((D, D)), const((D, D)), const((1, D)), row(D)],
        out_specs=[row(D), row(D), row(D), row(D), row(D), const((1, LANES)), const((1, D))],
        out_shape=[jax.ShapeDtypeStruct((t, D), F32), jax.ShapeDtypeStruct((t, D), F32), jax.ShapeDtypeStruct((t, D), F32),
                   jax.ShapeDtypeStruct((t, D), MXU_DTYPE), jax.ShapeDtypeStruct((t, D), F32),
                   jax.ShapeDtypeStruct((1, LANES), F32), jax.ShapeDtypeStruct((1, D), F32)],
        compiler_params=_params(("arbitrary",)),
    )(x, yssd, ylru, ymem, gl, w_bs, w_bl, w_bm, w_out, fg, tgt)


def _merge_bwd(dx2, gl, ps, pl_in, pm, w_bs, w_bl, w_bm, w_out, rows=256):
    t = dx2.shape[0]
    rows = min(rows, t)
    D = D_MODEL

    def body(dx2_ref, gl_ref, ps_ref, pl_ref, pm_ref, wbs_ref, wbl_ref, wbm_ref, wo_ref,
             dg_ref, dps_ref, dpl_ref, dpm_ref, dys_ref, dyl_ref, dym_ref):
        dm = _dot(dx2_ref[...], wo_ref[...], NT)
        for idx, (p_ref, dp_ref, w_ref, dy_ref) in enumerate(
                ((ps_ref, dps_ref, wbs_ref, dys_ref), (pl_ref, dpl_ref, wbl_ref, dyl_ref), (pm_ref, dpm_ref, wbm_ref, dym_ref))):
            gate = _sigmoid(gl_ref[:, idx * D:(idx + 1) * D])
            dg_ref[:, idx * D:(idx + 1) * D] = ((dm * p_ref[...]) * gate * (1.0 - gate)).astype(dg_ref.dtype)
            dp = dm * gate
            dp_ref[...] = dp.astype(dp_ref.dtype)
            dy_ref[...] = _dot(dp, w_ref[...], NT)

    row = lambda w: pl.BlockSpec((rows, w), lambda i: (i, 0))
    const = lambda shape: pl.BlockSpec(shape, lambda i: (0,) * len(shape))
    return pl.pallas_call(
        body, name="merge_bwd", grid=(t // rows,),
        in_specs=[row(D), row(3 * D), row(D), row(D), row(D), const((SSD_WIDTH, D)), const((LRU_WIDTH, D)),
                  const((D, D)), const((D, D))],
        out_specs=[row(3 * D), row(D), row(D), row(D), row(SSD_WIDTH), row(LRU_WIDTH), row(D)],
        out_shape=[jax.ShapeDtypeStruct((t, 3 * D), MXU_DTYPE), jax.ShapeDtypeStruct((t, D), MXU_DTYPE),
                   jax.ShapeDtypeStruct((t, D), MXU_DTYPE), jax.ShapeDtypeStruct((t, D), MXU_DTYPE),
                   jax.ShapeDtypeStruct((t, SSD_WIDTH), F32), jax.ShapeDtypeStruct((t, LRU_WIDTH), F32),
                   jax.ShapeDtypeStruct((t, D), F32)],
        compiler_params=_params(("parallel",)),
    )(dx2, gl, ps, pl_in, pm, w_bs, w_bl, w_bm, w_out)


FLIPS = tuple((dx, dy, dc) for dx in (0, 1) for dy in (0, 1) for dc in (0, 1))[1:]


def _mesh_place():
    x, y, c = lax.axis_index("x"), lax.axis_index("y"), lax.axis_index("c")
    return x, y, c, 4 * x + 2 * y + c


def _flip(x, y, c, f):
    px = 1 - x if f[0] else x
    py = 1 - y if f[1] else y
    pc = 1 - c if f[2] else c
    return (px, py, pc), 4 * px + 2 * py + pc


def _all_gather(arrs):
    n = len(arrs)

    def body(*refs):
        ins, outs = refs[:n], refs[n:2 * n]
        send_sems, recv_sems, local_sems = refs[2 * n:]
        x, y, c, me = _mesh_place()
        local = [pltpu.make_async_copy(ins[a], outs[a].at[me], local_sems.at[a]) for a in range(n)]
        for cp in local:
            cp.start()
        sends = []
        for k, f in enumerate(FLIPS):
            peer, _ = _flip(x, y, c, f)
            for a in range(n):
                cp = pltpu.make_async_remote_copy(src_ref=ins[a], dst_ref=outs[a].at[me], send_sem=send_sems.at[a, k],
                                                  recv_sem=recv_sems.at[a, k], device_id=peer, device_id_type=pl.DeviceIdType.MESH)
                cp.start()
                sends.append(cp)
        for k, f in enumerate(FLIPS):
            peer, peer_slot = _flip(x, y, c, f)
            for a in range(n):
                pltpu.make_async_remote_copy(src_ref=ins[a], dst_ref=outs[a].at[peer_slot], send_sem=send_sems.at[a, k],
                                             recv_sem=recv_sems.at[a, k], device_id=peer,
                                             device_id_type=pl.DeviceIdType.MESH).wait_recv()
        for cp in sends:
            cp.wait_send()
        for cp in local:
            cp.wait()

    any_spec = pl.BlockSpec(memory_space=pl.ANY)
    return pl.pallas_call(
        body, name="weights_all_gather", in_specs=[any_spec] * n, out_specs=[any_spec] * n,
        out_shape=[jax.ShapeDtypeStruct((N_DEV,) + a.shape, a.dtype) for a in arrs],
        scratch_shapes=[pltpu.SemaphoreType.DMA((n, len(FLIPS))), pltpu.SemaphoreType.DMA((n, len(FLIPS))),
                        pltpu.SemaphoreType.DMA((n,))],
    )(*arrs)


def _grad_exchange(slotted, shared):
    ns, n = len(slotted), len(slotted) + len(shared)

    def body(*refs):
        ins, outs = refs[:n], refs[n:2 * n]
        send_sems, recv_sems, local_sems = refs[2 * n:]
        x, y, c, me = _mesh_place()

        def src(a, slot):
            return ins[a].at[slot] if a < ns else ins[a]

        local = [pltpu.make_async_copy(src(a, me), outs[a].at[me], local_sems.at[a]) for a in range(n)]
        for cp in local:
            cp.start()
        sends = []
        for k, f in enumerate(FLIPS):
            peer, peer_slot = _flip(x, y, c, f)
            for a in range(n):
                cp = pltpu.make_async_remote_copy(src_ref=src(a, peer_slot), dst_ref=outs[a].at[me], send_sem=send_sems.at[a, k],
                                                  recv_sem=recv_sems.at[a, k], device_id=peer, device_id_type=pl.DeviceIdType.MESH)
                cp.start()
                sends.append(cp)
        for k, f in enumerate(FLIPS):
            peer, peer_slot = _flip(x, y, c, f)
            for a in range(n):
                pltpu.make_async_remote_copy(src_ref=src(a, me), dst_ref=outs[a].at[peer_slot], send_sem=send_sems.at[a, k],
                                             recv_sem=recv_sems.at[a, k], device_id=peer,
                                             device_id_type=pl.DeviceIdType.MESH).wait_recv()
        for cp in sends:
            cp.wait_send()
        for cp in local:
            cp.wait()

    any_spec = pl.BlockSpec(memory_space=pl.ANY)
    return pl.pallas_call(
        body, name="grad_exchange", in_specs=[any_spec] * n, out_specs=[any_spec] * n,
        out_shape=[jax.ShapeDtypeStruct(a.shape, a.dtype) for a in slotted]
        + [jax.ShapeDtypeStruct((N_DEV,) + a.shape, a.dtype) for a in shared],
        scratch_shapes=[pltpu.SemaphoreType.DMA((n, len(FLIPS))), pltpu.SemaphoreType.DMA((n, len(FLIPS))),
                        pltpu.SemaphoreType.DMA((n,))],
    )(*slotted, *shared)


def _row_tile(r, limit):
    if r <= limit:
        return r
    best = 8
    for cand in range(8, limit + 1, 8):
        if r % cand == 0:
            best = cand
    assert r % best == 0, r
    return best


def _adam_update(w, g, m, v):
    nm = ADAM_B1 * m + (1.0 - ADAM_B1) * g
    nv = ADAM_B2 * v + (1.0 - ADAM_B2) * (g * g)
    m_hat = nm / (1.0 - ADAM_B1 ** ADAM_STEP)
    v_hat = nv / (1.0 - ADAM_B2 ** ADAM_STEP)
    return -ADAM_LR * (m_hat / (jnp.sqrt(v_hat) + ADAM_EPS) + ADAM_WD * w), nm, nv


def _sum_adamw(parts, w, m, v, name):
    s, r, c = parts.shape
    rows = _row_tile(r, max(8, (4 << 20) // (s * c * 4) // 8 * 8))

    def body(p_ref, w_ref, m_ref, v_ref, g_ref, d_ref, nm_ref, nv_ref):
        g = p_ref[0]
        for k in range(1, s):
            g = g + p_ref[k]
        g_ref[...] = g
        d_ref[...], nm_ref[...], nv_ref[...] = _adam_update(w_ref[...], g, m_ref[...], v_ref[...])

    spec = pl.BlockSpec((rows, c), lambda i: (i, 0))
    shape = jax.ShapeDtypeStruct((r, c), F32)
    return pl.pallas_call(
        body, name=name, grid=(r // rows,),
        in_specs=[pl.BlockSpec((s, rows, c), lambda i: (0, i, 0)), spec, spec, spec],
        out_specs=[spec] * 4, out_shape=[shape] * 4,
        compiler_params=_params(("parallel",)),
    )(parts, w, m, v)


def _sum_slots(parts, name, rows=512):
    s, r, _ = parts.shape
    rows = _row_tile(r, rows)

    def body(p_ref, o_ref):
        acc = p_ref[0]
        for k in range(1, s):
            acc = acc + p_ref[k]
        o_ref[...] = acc

    return pl.pallas_call(
        body, name=name, grid=(r // rows,),
        in_specs=[pl.BlockSpec((s, rows, LANES), lambda i: (0, i, 0))],
        out_specs=pl.BlockSpec((rows, LANES), lambda i: (i, 0)),
        out_shape=jax.ShapeDtypeStruct((r, LANES), F32),
        compiler_params=_params(("parallel",)),
    )(parts)


def _adamw(w, g, m, v, name, rows=512):
    r = w.shape[0]
    rows = _row_tile(r, rows)

    def body(w_ref, g_ref, m_ref, v_ref, d_ref, nm_ref, nv_ref):
        d_ref[...], nm_ref[...], nv_ref[...] = _adam_update(w_ref[...], g_ref[...], m_ref[...], v_ref[...])

    spec = pl.BlockSpec((rows, LANES), lambda i: (i, 0))
    shape = jax.ShapeDtypeStruct((r, LANES), F32)
    return pl.pallas_call(
        body, name=name, grid=(r // rows,), in_specs=[spec] * 4, out_specs=[spec] * 3, out_shape=[shape] * 3,
        compiler_params=_params(("parallel",)),
    )(w, g, m, v)


def _pack(arrs, dtype, row_multiple):
    flat = jnp.concatenate([a.reshape(-1).astype(dtype) for a in arrs])
    unit = LANES * row_multiple
    padded = -(-flat.shape[0] // unit) * unit
    return jnp.pad(flat, (0, padded - flat.shape[0])).reshape(-1, LANES)


def _unpack(packed, shapes, lead=()):
    flat = packed.reshape(lead + (-1,))
    out, off = [], 0
    for shp in shapes:
        n = math.prod(shp)
        out.append(flat[..., off:off + n].reshape(lead + tuple(shp)))
        off += n
    return out


def _gather_cols(g, lo, hi):
    width = g.shape[2]
    pieces = []
    for s in range(N_DEV):
        a, e = max(lo, s * width), min(hi, (s + 1) * width)
        if a < e:
            pieces.append(g[s, :, a - s * width:e - s * width])
    return pieces[0] if len(pieces) == 1 else jnp.concatenate(pieces, axis=1)


def _scatter_cols(segs, width):
    slots = []
    for k in range(N_DEV):
        lo, hi = k * width, (k + 1) * width
        pieces = []
        for arr, s_lo, s_hi in segs:
            a, e = max(lo, s_lo), min(hi, s_hi)
            if a < e:
                pieces.append(arr[:, a - s_lo:e - s_lo])
        slots.append(pieces[0] if len(pieces) == 1 else jnp.concatenate(pieces, axis=1))
    return jnp.stack(slots)


def _block_diag_groups(w):
    w4 = w.reshape(LRU_NGROUPS, 4, LRU_BLOCK, LRU_BLOCK)
    eye = jnp.eye(4, dtype=w.dtype)
    return jnp.einsum("gaij,ab->gaibj", w4, eye).reshape(LRU_NGROUPS, LRU_GROUP, LRU_GROUP)


def _block_diag_extract(wg):
    w5 = wg.reshape(LRU_NGROUPS, 4, LRU_BLOCK, 4, LRU_BLOCK)
    idx = jnp.arange(4)
    return w5[:, idx, :, idx, :].transpose(1, 0, 2, 3).reshape(LRU_BLOCKS, LRU_BLOCK, LRU_BLOCK)


BIG = ("w_in", "w_kv", "w_br_ssd", "w_br_lru", "w_br_mem", "w_out")
SMALL_SHARDED = ("ssd_conv_w", "ssd_norm_g", "lru_conv_w")
REPLICATED = ("norm_g", "ssd_conv_b", "ssd_dt_bias", "ssd_a_log", "ssd_d", "lru_conv_b", "lru_w_a", "lru_b_a",
              "lru_w_x", "lru_b_x", "lru_lambda", "mem_norm_g", "final_g")
WEIGHTS = ("norm_g", "w_in", "ssd_conv_w", "ssd_conv_b", "ssd_dt_bias", "ssd_a_log", "ssd_d", "ssd_norm_g", "lru_conv_w",
           "lru_conv_b", "lru_w_a", "lru_b_a", "lru_w_x", "lru_b_x", "lru_lambda", "mem_norm_g", "w_kv", "w_br_ssd",
           "w_br_lru", "w_br_mem", "w_out", "final_g")


def kernel(x, mem, norm_g, w_in, ssd_conv_w, ssd_conv_b, ssd_dt_bias, ssd_a_log, ssd_d, ssd_norm_g, lru_conv_w, lru_conv_b, lru_w_a, lru_b_a, lru_w_x, lru_b_x, lru_lambda, mem_norm_g, w_kv, w_br_ssd, w_br_lru, w_br_mem, w_out, final_g, loss_target, m_norm_g, m_w_in, m_ssd_conv_w, m_ssd_conv_b, m_ssd_dt_bias, m_ssd_a_log, m_ssd_d, m_ssd_norm_g, m_lru_conv_w, m_lru_conv_b, m_lru_w_a, m_lru_b_a, m_lru_w_x, m_lru_b_x, m_lru_lambda, m_mem_norm_g, m_w_kv, m_w_br_ssd, m_w_br_lru, m_w_br_mem, m_w_out, m_final_g, v_norm_g, v_w_in, v_ssd_conv_w, v_ssd_conv_b, v_ssd_dt_bias, v_ssd_a_log, v_ssd_d, v_ssd_norm_g, v_lru_conv_w, v_lru_conv_b, v_lru_w_a, v_lru_b_a, v_lru_w_x, v_lru_b_x, v_lru_lambda, v_mem_norm_g, v_w_kv, v_w_br_ssd, v_w_br_lru, v_w_br_mem, v_w_out, v_final_g):
    env = dict(locals())
    W = {n: env[n] for n in WEIGHTS}
    M = {n: env["m_" + n] for n in WEIGHTS}
    V = {n: env["v_" + n] for n in WEIGHTS}
    me = 4 * lax.axis_index("x") + 2 * lax.axis_index("y") + lax.axis_index("c")
    t = x.shape[1]
    xt = x[0]
    memt = mem[0]
    tgt = loss_target[0]

    small_shapes = [W[n].shape for n in SMALL_SHARDED]
    gathered = _all_gather([W[n][0].astype(MXU_DTYPE) for n in BIG] + [_pack([W[n] for n in SMALL_SHARDED], F32, 8)])
    g_in, g_kv, g_bs, g_bl, g_bm, g_out, gs = gathered
    g_cw, g_ng, g_lcw = _unpack(gs, small_shapes, (N_DEV,))
    cols = lambda a: jnp.moveaxis(a[:, 0], 0, -2).reshape(a.shape[2:-1] + (-1,))
    rows_ = lambda a: a.reshape((-1,) + a.shape[2:])
    w_bs_f, w_bl_f, w_bm_f, w_out_f = rows_(g_bs), rows_(g_bl), rows_(g_bm), rows_(g_out)
    conv_w_f, ssd_ng_f, lru_cw_f = cols(g_cw), cols(g_ng), cols(g_lcw)
    b = SEG_BOUNDS
    w_kv_f = _gather_cols(g_kv, 0, 2 * D_MODEL)
    w_ssd, w_lru, w_q, w_g = (_gather_cols(g_in, b[0], b[1]), _gather_cols(g_in, b[2], b[3]),
                              _gather_cols(g_in, b[3], b[4]), _gather_cols(g_in, b[4], b[5]))
    w_dt = jnp.pad(_gather_cols(g_in, b[1], b[2]), ((0, 0), (0, DT_PAD - SSD_HEADS)))

    pad_heads = lambda a: jnp.pad(a, ((0, 0), (0, LANES - SSD_HEADS)))
    dtb, alog = pad_heads(ssd_dt_bias), pad_heads(ssd_a_log)
    d_row = jnp.repeat(ssd_d, SSD_HEAD_DIM, axis=1)
    ng_row = ssd_ng_f.reshape(1, SSD_WIDTH)
    wa_g, wx_g = _block_diag_groups(lru_w_a[0]), _block_diag_groups(lru_w_x[0])
    ba, bx = lru_b_a.reshape(1, LRU_WIDTH), lru_b_x.reshape(1, LRU_WIDTH)
    fg = final_g.reshape(1, D_MODEL)

    h = _rms_fwd(xt, norm_g, "norm_fwd")
    proj_ssd = _matmul(h, w_ssd, "nn", "proj_ssd", tk=D_MODEL)
    proj_lru = _matmul(h, w_lru, "nn", "proj_lru", tk=D_MODEL)
    proj_q = _matmul(h, w_q, "nn", "proj_q", tk=D_MODEL)
    proj_g = _matmul(h, w_g, "nn", "proj_g", tk=D_MODEL)
    proj_dt = _matmul(h, w_dt, "nn", "proj_dt", tk=D_MODEL)
    mem_n = _rms_fwd(memt, mem_norm_g, "mem_norm_fwd")
    kv = _matmul(mem_n, w_kv_f, "nn", "mem_kv", tk=D_MODEL)
    yssd, y_scan, states = _ssd_fwd(proj_ssd, proj_dt, conv_w_f, ssd_conv_b, dtb, alog, d_row, ng_row)
    ylru, h_lru = _lru_fwd(proj_lru, lru_cw_f, lru_conv_b, wa_g, wx_g, ba, bx, lru_lambda)
    ymem = _mem_fwd(proj_q, kv)
    ps, pl_, pm, merged, dx2, loss_vec, g_fg = _merge_fwd(xt, yssd, ylru, ymem, proj_g, w_bs_f, w_bl_f, w_bm_f, w_out_f, fg, tgt)

    d_g, dps, dpl, dpm, dyssd, dylru, dymem = _merge_bwd(dx2, proj_g, ps, pl_, pm, w_bs_f, w_bl_f, w_bm_f, w_out_f)
    gw_out = _matmul(merged, dx2, "tn", "grad_w_out", tk=1024)
    gw_bs = _matmul(yssd, dps, "tn", "grad_w_br_ssd", tk=1024)
    gw_bl = _matmul(ylru, dpl, "tn", "grad_w_br_lru", tm=LRU_WIDTH, tk=1024)
    gw_bm = _matmul(ymem, dpm, "tn", "grad_w_br_mem", tk=1024)
    d_q, d_kv = _mem_bwd(proj_q, kv, dymem)
    gw_kv = _matmul(mem_n, d_kv, "tn", "grad_w_kv", tk=memt.shape[0])
    d_memn = _matmul(d_kv, w_kv_f, "nt", "d_mem_n", tk=1024)
    _, g_memng = _rms_bwd(memt, d_memn, None, mem_norm_g, "mem_norm_bwd")
    d_lru, gl_cw, gl_cb, g_ba, g_bx, g_lam, gwa_g, gwx_g = _lru_bwd(proj_lru, h_lru, dylru, lru_cw_f, lru_conv_b, wa_g, wx_g, ba, bx, lru_lambda)
    d_ssd, d_dt, gs_cw, gs_cb, g_dtb, g_alog, g_dch, g_ngrow = _ssd_bwd(proj_ssd, proj_dt, y_scan, states, dyssd, conv_w_f, ssd_conv_b, dtb, alog, d_row, ng_row)
    dh = _matmul(d_ssd, w_ssd, "nt", "dh_ssd", tk=1024)
    dh = _matmul(d_lru, w_lru, "nt", "dh_lru", tk=1024, acc_in=dh)
    dh = _matmul(d_q, w_q, "nt", "dh_q", tk=1024, acc_in=dh)
    dh = _matmul(d_g, w_g, "nt", "dh_g", tk=1024, acc_in=dh)
    dh = _matmul(d_dt, w_dt, "nt", "dh_dt", tk=DT_PAD, acc_in=dh)
    gw_ssd = _matmul(h, d_ssd, "tn", "grad_w_in_ssd", tk=1024)
    gw_lru = _matmul(h, d_lru, "tn", "grad_w_in_lru", tk=1024)
    gw_q = _matmul(h, d_q, "tn", "grad_w_in_q", tk=1024)
    gw_g = _matmul(h, d_g, "tn", "grad_w_in_g", tk=1024)
    gw_dt = _matmul(h, d_dt, "tn", "grad_w_in_dt", tk=1024)
    grad_x, g_normg = _rms_bwd(xt, dh, dx2, norm_g, "norm_bwd")

    split_rows = lambda a: a.reshape((N_DEV, -1) + a.shape[1:])
    in_segs = [(gw_ssd, b[0], b[1]), (gw_dt, b[1], b[2]), (gw_lru, b[2], b[3]), (gw_q, b[3], b[4]), (gw_g, b[4], b[5])]
    big_send = [_scatter_cols(in_segs, IN_WIDTH // N_DEV), _scatter_cols([(gw_kv, 0, 2 * D_MODEL)], 2 * D_MODEL // N_DEV),
                split_rows(gw_bs), split_rows(gw_bl), split_rows(gw_bm), split_rows(gw_out)]

    small_grads = {
        "norm_g": g_normg, "ssd_conv_w": gs_cw, "ssd_conv_b": gs_cb, "ssd_dt_bias": g_dtb[:, :SSD_HEADS],
        "ssd_a_log": g_alog[:, :SSD_HEADS], "ssd_d": jnp.sum(g_dch.reshape(SSD_HEADS, SSD_HEAD_DIM), axis=1).reshape(1, SSD_HEADS),
        "ssd_norm_g": g_ngrow.reshape(SSD_GROUPS, -1), "lru_conv_w": gl_cw, "lru_conv_b": gl_cb,
        "lru_w_a": _block_diag_extract(gwa_g), "lru_b_a": g_ba, "lru_w_x": _block_diag_extract(gwx_g), "lru_b_x": g_bx,
        "lru_lambda": g_lam, "mem_norm_g": g_memng, "final_g": g_fg,
    }
    small_names = tuple(small_grads)
    small_send = _pack([small_grads[n] for n in small_names], F32, 512)
    *big_recv, small_recv = _grad_exchange(big_send, [small_send])

    grads, delta, new_m, new_v = {}, {}, {}, {}
    for n, parts in zip(BIG, big_recv):
        res = _sum_adamw(parts, W[n][0], M[n][0], V[n][0], "adamw_" + n)
        for dst, a in zip((grads, delta, new_m, new_v), res):
            dst[n] = a[None]

    totals = dict(zip(small_names, _unpack(_sum_slots(small_recv, "grad_sum_small"), [small_grads[n].shape for n in small_names])))
    for n in REPLICATED:
        grads[n] = totals[n].reshape(W[n].shape)
    for n in SMALL_SHARDED:
        width = W[n].shape[-1]
        grads[n] = lax.dynamic_slice_in_dim(totals[n], me * width, width, axis=-1).reshape(W[n].shape)
    small_all = REPLICATED + SMALL_SHARDED
    w_s = _pack([W[n] for n in small_all], F32, 512)
    g_s = _pack([grads[n] for n in small_all], F32, 512)
    m_s = _pack([M[n] for n in small_all], F32, 512)
    v_s = _pack([V[n] for n in small_all], F32, 512)
    d_s, nm_s, nv_s = _adamw(w_s, g_s, m_s, v_s, "adamw_small")

    for packed, names, dst in ((d_s, small_all, delta), (nm_s, small_all, new_m), (nv_s, small_all, new_v)):
        for n, a in zip(names, _unpack(packed, [W[k].shape for k in names])):
            dst[n] = a

    loss = lax.psum(loss_vec[0, 0], ("x", "y", "c"))
    return (loss, grad_x[None], *[grads[n] for n in WEIGHTS], *[delta[n] for n in WEIGHTS],
            *[new_m[n] for n in WEIGHTS], *[new_v[n] for n in WEIGHTS])
```

```python
import functools
import math

import jax
import jax.numpy as jnp
from jax import lax
from jax.experimental import pallas as pl
from jax.experimental.pallas import tpu as pltpu

F32 = jnp.float32
MXU_DTYPE = jnp.bfloat16
GRAD_WIRE_DTYPE = jnp.bfloat16

D_MODEL = 1024
EPS = 1e-6
CONV_WIDTH = 4
SSD_WIDTH = 2048
SSD_HEAD_DIM = 64
SSD_HEADS = 32
SSD_GROUPS = 4
SSD_STATE = 128
SSD_CHUNK = 128
SSD_CONV_CH = SSD_WIDTH + 2 * SSD_GROUPS * SSD_STATE
SSD_PAIRS = SSD_HEADS // 2
PAIRS_PER_GROUP = SSD_PAIRS // SSD_GROUPS
LRU_WIDTH = 1536
LRU_BLOCKS = 16
LRU_BLOCK = 96
LRU_GROUP = 4 * LRU_BLOCK
LRU_NGROUPS = LRU_WIDTH // LRU_GROUP
LRU_C = 8.0
LRU_ROWS = 256
MEM_HEADS = 4
MEM_HEAD_DIM = 256
IN_WIDTH = 12320
N_DEV = 8
LANES = 128
SSD_SEG = SSD_WIDTH + SSD_CONV_CH
DT_PAD = LANES
SEG_BOUNDS = (0, 5120, 5152, 8224, 9248, 12320)

ADAM_LR = 0.001
ADAM_B1 = 0.9
ADAM_B2 = 0.999
ADAM_EPS = 1e-08
ADAM_WD = 0.01
ADAM_STEP = 10

VMEM_LIMIT = 56 * 1024 * 1024

NN = (((1,), (0,)), ((), ()))
NT = (((1,), (1,)), ((), ()))
TN = (((0,), (0,)), ((), ()))


def _dot(a, b, dims):
    return lax.dot_general(a.astype(MXU_DTYPE), b.astype(MXU_DTYPE), dims, preferred_element_type=F32)


def _sigmoid(x):
    return 1.0 / (1.0 + jnp.exp(-x))


def _log1p(e):
    u = 1.0 + e
    return jnp.where(u == 1.0, e, jnp.log(u) * (e / jnp.where(u == 1.0, 1.0, u - 1.0)))


def _softplus(x):
    return jnp.maximum(x, 0.0) + _log1p(jnp.exp(-jnp.abs(x)))


def _expm1(x):
    u = jnp.exp(x)
    um1 = u - 1.0
    lg = jnp.log(u)
    safe = jnp.where(um1 == 0.0, 1.0, lg)
    return jnp.where(um1 == 0.0, x, jnp.where(um1 == -1.0, -1.0, um1 * (x / safe)))


def _params(semantics):
    return pltpu.CompilerParams(dimension_semantics=semantics, vmem_limit_bytes=VMEM_LIMIT)


def _shift_down(cur, halo8, k):
    rolled = pltpu.roll(cur, k, 0)
    row8 = lax.broadcasted_iota(jnp.int32, halo8.shape, 0)
    top = jnp.where(row8 >= k, rolled[0:8], pltpu.roll(halo8, k, 0))
    return jnp.concatenate([top, rolled[8:]], axis=0)


def _shift_up(cur, next8, k):
    rows = cur.shape[0]
    rolled = pltpu.roll(cur, rows - k, 0)
    row8 = lax.broadcasted_iota(jnp.int32, next8.shape, 0)
    bot = jnp.where(row8 < 8 - k, rolled[rows - 8:rows], pltpu.roll(next8, 8 - k, 0))
    return jnp.concatenate([rolled[:rows - 8], bot], axis=0)


def _causal_conv(raw, halo8, w, b):
    acc = raw * w[3:4, :] + b
    for k in range(1, CONV_WIDTH):
        acc = acc + _shift_down(raw, halo8, k) * w[3 - k:4 - k, :]
    return acc


def _conv_backward(dco, next8, raw, halo8, w):
    d_raw = dco * w[3:4, :]
    for k in range(1, CONV_WIDTH):
        d_raw = d_raw + _shift_up(dco, next8, k) * w[3 - k:4 - k, :]
    gw = []
    for k in range(CONV_WIDTH):
        shifted = raw if k == 3 else _shift_down(raw, halo8, 3 - k)
        gw.append(jnp.sum(dco * shifted, axis=0, keepdims=True))
    gb = jnp.sum(dco, axis=0, keepdims=True)
    return d_raw, gw, gb


def _cumsum_rows(v):
    rows = v.shape[0]
    row = lax.broadcasted_iota(jnp.int32, v.shape, 0)
    s = 1
    while s < rows:
        v = v + jnp.where(row >= s, pltpu.roll(v, s, 0), 0.0)
        s *= 2
    return v


def _rev_cumsum_rows(v):
    rows = v.shape[0]
    row = lax.broadcasted_iota(jnp.int32, v.shape, 0)
    s = 1
    while s < rows:
        v = v + jnp.where(row < rows - s, pltpu.roll(v, rows - s, 0), 0.0)
        s *= 2
    return v


def _matmul(a, b, mode, name, out_dtype=F32, tm=1024, tn=1024, tk=512, acc_in=None):
    if mode == "nn":
        (m, kk), n = a.shape, b.shape[1]
    elif mode == "nt":
        (m, kk), n = a.shape, b.shape[0]
    else:
        (kk, m), n = a.shape, b.shape[1]
    tm, tn, tk = min(tm, m), min(tn, n), min(tk, kk)
    assert m % tm == 0 and n % tn == 0 and kk % tk == 0, (name, a.shape, b.shape)
    nk = kk // tk
    dims = {"nn": NN, "nt": NT, "tn": TN}[mode]
    a_spec = pl.BlockSpec((tk, tm), lambda i, j, k: (k, i)) if mode == "tn" else pl.BlockSpec((tm, tk), lambda i, j, k: (i, k))
    b_spec = pl.BlockSpec((tn, tk), lambda i, j, k: (j, k)) if mode == "nt" else pl.BlockSpec((tk, tn), lambda i, j, k: (k, j))
    o_spec = pl.BlockSpec((tm, tn), lambda i, j, k: (i, j))
    has_acc = acc_in is not None

    def body(*refs):
        if has_acc:
            a_ref, b_ref, c_ref, o_ref, acc_ref = refs
        else:
            a_ref, b_ref, o_ref, acc_ref = refs
        k = pl.program_id(2)

        @pl.when(k == 0)
        def _():
            acc_ref[...] = c_ref[...].astype(F32) if has_acc else jnp.zeros_like(acc_ref)

        acc_ref[...] += _dot(a_ref[...], b_ref[...], dims)

        @pl.when(k == nk - 1)
        def _():
            o_ref[...] = acc_ref[...].astype(o_ref.dtype)

    args = (a, b) + ((acc_in,) if has_acc else ())
    in_specs = [a_spec, b_spec] + ([o_spec] if has_acc else [])
    return pl.pallas_call(
        body, name=name, grid=(m // tm, n // tn, nk), in_specs=in_specs, out_specs=o_spec,
        out_shape=jax.ShapeDtypeStruct((m, n), out_dtype),
        scratch_shapes=[pltpu.VMEM((tm, tn), F32)],
        compiler_params=_params(("parallel", "parallel", "arbitrary")),
    )(*args)


def _rms_fwd(x, g, name, rows=512):
    t, d = x.shape
    rows = min(rows, t)

    def body(x_ref, g_ref, h_ref):
        xv = x_ref[...]
        r = lax.rsqrt(jnp.mean(xv * xv, axis=-1, keepdims=True) + EPS)
        h_ref[...] = ((xv * r) * g_ref[...]).astype(h_ref.dtype)

    return pl.pallas_call(
        body, name=name, grid=(t // rows,),
        in_specs=[pl.BlockSpec((rows, d), lambda i: (i, 0)), pl.BlockSpec((1, d), lambda i: (0, 0))],
        out_specs=pl.BlockSpec((rows, d), lambda i: (i, 0)),
        out_shape=jax.ShapeDtypeStruct((t, d), MXU_DTYPE),
        compiler_params=_params(("parallel",)),
    )(x, g)


def _rms_bwd(x, dh, dres, g, name, rows=512):
    t, d = x.shape
    rows = min(rows, t)
    has_res = dres is not None

    def body(*refs):
        if has_res:
            x_ref, dh_ref, dr_ref, g_ref, dx_ref, gg_ref = refs
        else:
            x_ref, dh_ref, g_ref, dx_ref, gg_ref = refs

        @pl.when(pl.program_id(0) == 0)
        def _():
            gg_ref[...] = jnp.zeros_like(gg_ref)

        xv = x_ref[...]
        dhv = dh_ref[...]
        r = lax.rsqrt(jnp.mean(xv * xv, axis=-1, keepdims=True) + EPS)
        n = xv * r
        dn = dhv * g_ref[...]
        dx = r * (dn - n * jnp.mean(dn * n, axis=-1, keepdims=True))
        if has_res:
            dx = dx + dr_ref[...]
        dx_ref[...] = dx
        gg_ref[...] += jnp.sum(dhv * n, axis=0, keepdims=True)

    row_spec = pl.BlockSpec((rows, d), lambda i: (i, 0))
    vec_spec = pl.BlockSpec((1, d), lambda i: (0, 0))
    args = (x, dh) + ((dres,) if has_res else ()) + (g,)
    return pl.pallas_call(
        body, name=name, grid=(t // rows,),
        in_specs=[row_spec, row_spec] + ([row_spec] if has_res else []) + [vec_spec],
        out_specs=[row_spec, vec_spec],
        out_shape=[jax.ShapeDtypeStruct((t, d), F32), jax.ShapeDtypeStruct((1, d), F32)],
        compiler_params=_params(("arbitrary",)),
    )(*args)


def _pair_select(lo, m, h0):
    return jnp.where(lo, m[:, h0:h0 + 1], m[:, h0 + 1:h0 + 2])


def _halves(lo, v):
    return (jnp.sum(jnp.where(lo, v, 0.0), axis=1, keepdims=True),
            jnp.sum(jnp.where(lo, 0.0, v), axis=1, keepdims=True))


def _ssd_common(dt_raw, dtb, alog):
    dt = _softplus(dt_raw + dtb)
    aneg = -jnp.exp(alog)
    a_cs = _cumsum_rows(dt * aneg)
    return dt, aneg, a_cs, a_cs.T


def _ssd_specs(nc, rev):
    cidx = (lambda c: nc - 1 - c) if rev else (lambda c: c)
    L = SSD_CHUNK
    return dict(
        z=pl.BlockSpec((L, SSD_WIDTH), lambda c: (cidx(c), 0)),
        xr=pl.BlockSpec((L, SSD_WIDTH), lambda c: (cidx(c), 1)),
        br=pl.BlockSpec((L, 512), lambda c: (cidx(c), 8)),
        cr=pl.BlockSpec((L, 512), lambda c: (cidx(c), 9)),
        dt=pl.BlockSpec((L, DT_PAD), lambda c: (cidx(c), 0)),
        cwx=pl.BlockSpec((CONV_WIDTH, SSD_WIDTH), lambda c: (0, 0)),
        cwb=pl.BlockSpec((CONV_WIDTH, 512), lambda c: (0, 4)),
        cwc=pl.BlockSpec((CONV_WIDTH, 512), lambda c: (0, 5)),
        cbx=pl.BlockSpec((1, SSD_WIDTH), lambda c: (0, 0)),
        cbb=pl.BlockSpec((1, 512), lambda c: (0, 4)),
        cbc=pl.BlockSpec((1, 512), lambda c: (0, 5)),
        vec128=pl.BlockSpec((1, LANES), lambda c: (0, 0)),
        vecw=pl.BlockSpec((1, SSD_WIDTH), lambda c: (0, 0)),
        wide=pl.BlockSpec((L, SSD_WIDTH), lambda c: (cidx(c), 0)),
        states=pl.BlockSpec((1, SSD_PAIRS, 128, SSD_STATE), lambda c: (cidx(c), 0, 0, 0)),
    )


def _ssd_fwd(proj_ssd, dt_p, conv_w, conv_b, dtb, alog, d_row, ng_row):
    t = proj_ssd.shape[0]
    nc = t // SSD_CHUNK
    L = SSD_CHUNK
    sp = _ssd_specs(nc, False)

    def body(z_ref, xr_ref, br_ref, cr_ref, dt_ref, cwx_ref, cwb_ref, cwc_ref, cbx_ref, cbb_ref, cbc_ref,
             dtb_ref, alog_ref, d_ref, ng_ref, yssd_ref, y_ref, st_ref,
             hx_ref, hb_ref, hc_ref, state_ref, yacc_ref):
        @pl.when(pl.program_id(0) == 0)
        def _():
            hx_ref[...] = jnp.zeros_like(hx_ref)
            hb_ref[...] = jnp.zeros_like(hb_ref)
            hc_ref[...] = jnp.zeros_like(hc_ref)
            state_ref[...] = jnp.zeros_like(state_ref)

        xr, br, cr = xr_ref[...], br_ref[...], cr_ref[...]
        px = _causal_conv(xr, hx_ref[...], cwx_ref[...], cbx_ref[...])
        pb = _causal_conv(br, hb_ref[...], cwb_ref[...], cbb_ref[...])
        pc = _causal_conv(cr, hc_ref[...], cwc_ref[...], cbc_ref[...])
        hx_ref[...] = xr[L - 8:L, :]
        hb_ref[...] = br[L - 8:L, :]
        hc_ref[...] = cr[L - 8:L, :]
        xs = px * _sigmoid(px)
        bm = pb * _sigmoid(pb)
        cm = pc * _sigmoid(pc)

        dt, _, a_cs, a_t = _ssd_common(dt_ref[...], dtb_ref[...], alog_ref[...])
        exp_a = jnp.exp(a_cs)
        a_last = a_cs[L - 1:L, :]
        dte = jnp.exp(a_last - a_cs)
        dec = jnp.exp(a_last)

        lane = lax.broadcasted_iota(jnp.int32, (L, LANES), 1)
        sub = lax.broadcasted_iota(jnp.int32, (L, LANES), 0)
        lo = lane < SSD_HEAD_DIM
        causal = sub >= lane
        top = sub < SSD_HEAD_DIM

        for g in range(SSD_GROUPS):
            b_g = bm[:, g * SSD_STATE:(g + 1) * SSD_STATE]
            c_g = cm[:, g * SSD_STATE:(g + 1) * SSD_STATE]
            cb = _dot(c_g, b_g, NT)
            for jj in range(PAIRS_PER_GROUP):
                j = g * PAIRS_PER_GROUP + jj
                h0 = 2 * j
                cols = slice(j * LANES, (j + 1) * LANES)
                xs_p = xs[:, cols]
                xdt = xs_p * _pair_select(lo, dt, h0)
                g0 = jnp.where(causal, jnp.exp(a_cs[:, h0:h0 + 1] - a_t[h0:h0 + 1, :]), 0.0) * cb
                g1 = jnp.where(causal, jnp.exp(a_cs[:, h0 + 1:h0 + 2] - a_t[h0 + 1:h0 + 2, :]), 0.0) * cb
                lhs = jnp.concatenate([g0, g1], axis=1)
                rhs = jnp.concatenate([jnp.where(lo, xdt, 0.0), jnp.where(lo, 0.0, xdt)], axis=0)
                y_diag = _dot(lhs, rhs, NN)
                h_p = state_ref[j]
                st_ref[0, j] = h_p
                y_off = _dot(c_g, h_p, NT) * _pair_select(lo, exp_a, h0)
                s_new = _dot(xdt * _pair_select(lo, dte, h0), b_g, TN)
                dec_rows = jnp.where(top, dec[:, h0:h0 + 1], dec[:, h0 + 1:h0 + 2])
                state_ref[j] = h_p * dec_rows + s_new
                yacc_ref[:, cols] = (y_diag + y_off) + xs_p * d_ref[:, cols]

        y = yacc_ref[...]
        y_ref[...] = y
        zz = z_ref[...]
        y2 = y * (zz * _sigmoid(zz))
        gw = SSD_WIDTH // SSD_GROUPS
        for g in range(SSD_GROUPS):
            seg = y2[:, g * gw:(g + 1) * gw]
            r = lax.rsqrt(jnp.mean(seg * seg, axis=-1, keepdims=True) + EPS)
            yssd_ref[:, g * gw:(g + 1) * gw] = ((seg * r) * ng_ref[:, g * gw:(g + 1) * gw]).astype(yssd_ref.dtype)

    return pl.pallas_call(
        body, name="ssd_fwd", grid=(nc,),
        in_specs=[sp["z"], sp["xr"], sp["br"], sp["cr"], sp["dt"], sp["cwx"], sp["cwb"], sp["cwc"],
                  sp["cbx"], sp["cbb"], sp["cbc"], sp["vec128"], sp["vec128"], sp["vecw"], sp["vecw"]],
        out_specs=[sp["wide"], sp["wide"], sp["states"]],
        out_shape=[jax.ShapeDtypeStruct((t, SSD_WIDTH), MXU_DTYPE), jax.ShapeDtypeStruct((t, SSD_WIDTH), F32),
                   jax.ShapeDtypeStruct((nc, SSD_PAIRS, 128, SSD_STATE), F32)],
        scratch_shapes=[pltpu.VMEM((8, SSD_WIDTH), F32), pltpu.VMEM((8, 512), F32), pltpu.VMEM((8, 512), F32),
                        pltpu.VMEM((SSD_PAIRS, 128, SSD_STATE), F32), pltpu.VMEM((L, SSD_WIDTH), F32)],
        compiler_params=_params(("arbitrary",)),
    )(proj_ssd, proj_ssd, proj_ssd, proj_ssd, dt_p, conv_w, conv_w, conv_w, conv_b, conv_b, conv_b,
      dtb, alog, d_row, ng_row)


def _ssd_bwd(proj_ssd, dt_p, y, states, dyssd, conv_w, conv_b, dtb, alog, d_row, ng_row):
    t = proj_ssd.shape[0]
    nc = t // SSD_CHUNK
    L = SSD_CHUNK
    sp = _ssd_specs(nc, True)
    groups8 = L // 8

    def halo_spec(width, col):
        return pl.BlockSpec((8, width), lambda c: (jnp.maximum((nc - 1 - c) * groups8 - 1, 0), col))

    def body(z_ref, xr_ref, br_ref, cr_ref, hx_ref, hb_ref, hc_ref, dt_ref, y_ref, st_ref, dy_ref,
             cwx_ref, cwb_ref, cwc_ref, cbx_ref, cbb_ref, cbc_ref, dtb_ref, alog_ref, d_ref, ng_ref,
             dssd_ref, ddt_ref, gcw_ref, gcb_ref, gdtb_ref, galog_ref, gd_ref, gng_ref,
             gn_ref, nx_ref, nb_ref, ncc_ref, dxs_ref):
        step = pl.program_id(0)

        @pl.when(step == 0)
        def _():
            gn_ref[...] = jnp.zeros_like(gn_ref)
            nx_ref[...] = jnp.zeros_like(nx_ref)
            nb_ref[...] = jnp.zeros_like(nb_ref)
            ncc_ref[...] = jnp.zeros_like(ncc_ref)
            for ref in (gcw_ref, gcb_ref, gdtb_ref, galog_ref, gd_ref, gng_ref):
                ref[...] = jnp.zeros_like(ref)

        first_chunk = step == nc - 1
        keep = jnp.where(first_chunk, 0.0, 1.0)
        xr, br, cr = xr_ref[...], br_ref[...], cr_ref[...]
        hx, hb, hc = hx_ref[...] * keep, hb_ref[...] * keep, hc_ref[...] * keep
        cwx, cwb, cwc = cwx_ref[...], cwb_ref[...], cwc_ref[...]
        px = _causal_conv(xr, hx, cwx, cbx_ref[...])
        pb = _causal_conv(br, hb, cwb, cbb_ref[...])
        pc = _causal_conv(cr, hc, cwc, cbc_ref[...])
        sx, sb, sc = _sigmoid(px), _sigmoid(pb), _sigmoid(pc)
        xs, bm, cm = px * sx, pb * sb, pc * sc

        dt_in = dt_ref[...] + dtb_ref[...]
        dt, aneg, a_cs, a_t = _ssd_common(dt_ref[...], dtb_ref[...], alog_ref[...])
        exp_a = jnp.exp(a_cs)
        a_last = a_cs[L - 1:L, :]
        dte = jnp.exp(a_last - a_cs)
        dec = jnp.exp(a_last)

        lane = lax.broadcasted_iota(jnp.int32, (L, LANES), 1)
        sub = lax.broadcasted_iota(jnp.int32, (L, LANES), 0)
        lo = lane < SSD_HEAD_DIM
        causal = sub >= lane
        top = sub < SSD_HEAD_DIM
        last_row = sub == L - 1

        yv = y_ref[...]
        zz = z_ref[...]
        sz = _sigmoid(zz)
        silz = zz * sz
        y2 = yv * silz
        dyv = dy_ref[...]
        gw = SSD_WIDTH // SSD_GROUPS
        d_y2_parts = []
        gng_parts = []
        for g in range(SSD_GROUPS):
            seg = y2[:, g * gw:(g + 1) * gw]
            dseg = dyv[:, g * gw:(g + 1) * gw]
            r = lax.rsqrt(jnp.mean(seg * seg, axis=-1, keepdims=True) + EPS)
            n = seg * r
            dn = dseg * ng_ref[:, g * gw:(g + 1) * gw]
            gng_parts.append(jnp.sum(dseg * n, axis=0, keepdims=True))
            d_y2_parts.append(r * (dn - n * jnp.mean(dn * n, axis=-1, keepdims=True)))
        d_y2 = jnp.concatenate(d_y2_parts, axis=1)
        gng_ref[...] += jnp.concatenate(gng_parts, axis=1)
        d_y = d_y2 * silz
        dssd_ref[:, 0:SSD_WIDTH] = (d_y2 * yv * (sz * (1.0 + zz * (1.0 - sz)))).astype(dssd_ref.dtype)
        gd_ref[...] += jnp.sum(d_y * xs, axis=0, keepdims=True)
        dxs_ref[...] = d_y * d_ref[...]

        d_a = jnp.zeros((L, LANES), F32)
        d_at = jnp.zeros((LANES, L), F32)
        ddt = jnp.zeros((L, LANES), F32)
        d_b_parts, d_c_parts = [], []
        for g in range(SSD_GROUPS):
            b_g = bm[:, g * SSD_STATE:(g + 1) * SSD_STATE]
            c_g = cm[:, g * SSD_STATE:(g + 1) * SSD_STATE]
            cb = _dot(c_g, b_g, NT)
            d_cb = jnp.zeros((L, L), F32)
            d_bg = jnp.zeros((L, SSD_STATE), F32)
            d_cg = jnp.zeros((L, SSD_STATE), F32)
            for jj in range(PAIRS_PER_GROUP):
                j = g * PAIRS_PER_GROUP + jj
                h0 = 2 * j
                cols = slice(j * LANES, (j + 1) * LANES)
                dy_p = d_y[:, cols]
                xs_p = xs[:, cols]
                dt_pp = _pair_select(lo, dt, h0)
                expa_p = _pair_select(lo, exp_a, h0)
                dte_p = _pair_select(lo, dte, h0)
                xdt = xs_p * dt_pp
                l0 = jnp.where(causal, jnp.exp(a_cs[:, h0:h0 + 1] - a_t[h0:h0 + 1, :]), 0.0)
                l1 = jnp.where(causal, jnp.exp(a_cs[:, h0 + 1:h0 + 2] - a_t[h0 + 1:h0 + 2, :]), 0.0)
                g0, g1 = l0 * cb, l1 * cb
                h_p = st_ref[0, j]
                gn_p = gn_ref[j]
                dys = dy_p * expa_p
                d_cg = d_cg + _dot(dys, h_p, NN)
                d_h = _dot(dys, c_g, TN)
                t1 = dy_p * _dot(c_g, h_p, NT) * expa_p
                dw = _dot(b_g, gn_p, NT)
                d_bg = d_bg + _dot(xdt * dte_p, gn_p, NN)
                d_xdt = dw * dte_p
                t2 = d_xdt * xdt
                dyl, dyh = jnp.where(lo, dy_p, 0.0), jnp.where(lo, 0.0, dy_p)
                d_xdt = d_xdt + _dot(jnp.concatenate([g0, g1], axis=0), jnp.concatenate([dyl, dyh], axis=0), TN)
                dm0 = _dot(dyl, xdt, NT)
                dm1 = _dot(dyh, xdt, NT)
                d_cb = d_cb + (l0 * dm0 + l1 * dm1)
                e0, e1 = dm0 * g0, dm1 * g1
                a0, a1 = _halves(lo, t1 - t2)
                a0 = a0 + jnp.sum(e0, axis=1, keepdims=True)
                a1 = a1 + jnp.sum(e1, axis=1, keepdims=True)
                s0, s1 = _halves(lo, t2)
                gh = jnp.sum(gn_p * h_p, axis=1, keepdims=True)
                dd0 = jnp.sum(jnp.where(top[:, 0:1], gh, 0.0), axis=0, keepdims=True)
                dd1 = jnp.sum(jnp.where(top[:, 0:1], 0.0, gh), axis=0, keepdims=True)
                end0 = jnp.sum(s0, axis=0, keepdims=True) + dd0 * dec[:, h0:h0 + 1]
                end1 = jnp.sum(s1, axis=0, keepdims=True) + dd1 * dec[:, h0 + 1:h0 + 2]
                d_a = d_a + jnp.where(lane == h0, a0 + jnp.where(last_row, end0, 0.0), 0.0)
                d_a = d_a + jnp.where(lane == h0 + 1, a1 + jnp.where(last_row, end1, 0.0), 0.0)
                d_at = d_at - jnp.where(sub == h0, jnp.sum(e0, axis=0, keepdims=True), 0.0)
                d_at = d_at - jnp.where(sub == h0 + 1, jnp.sum(e1, axis=0, keepdims=True), 0.0)
                dec_rows = jnp.where(top, dec[:, h0:h0 + 1], dec[:, h0 + 1:h0 + 2])
                gn_ref[j] = d_h + dec_rows * gn_p
                q0, q1 = _halves(lo, d_xdt * xs_p)
                ddt = ddt + jnp.where(lane == h0, q0, 0.0) + jnp.where(lane == h0 + 1, q1, 0.0)
                dxs_ref[:, cols] += d_xdt * dt_pp
            d_cg = d_cg + _dot(d_cb, b_g, NN)
            d_bg = d_bg + _dot(d_cb, c_g, TN)
            d_b_parts.append(d_bg)
            d_c_parts.append(d_cg)

        rc = _rev_cumsum_rows(d_a + d_at.T)
        d_dt = rc * aneg + ddt
        galog_ref[...] += jnp.sum(rc * dt, axis=0, keepdims=True) * aneg
        d_dtraw = d_dt * _sigmoid(dt_in)
        gdtb_ref[...] += jnp.sum(d_dtraw, axis=0, keepdims=True)
        ddt_ref[...] = d_dtraw.astype(ddt_ref.dtype)

        def dsilu(p, s):
            return s * (1.0 + p * (1.0 - s))

        dcx = dxs_ref[...] * dsilu(px, sx)
        dcb = jnp.concatenate(d_b_parts, axis=1) * dsilu(pb, sb)
        dcc = jnp.concatenate(d_c_parts, axis=1) * dsilu(pc, sc)
        drx, gwx, gbx = _conv_backward(dcx, nx_ref[...], xr, hx, cwx)
        drb, gwb, gbb = _conv_backward(dcb, nb_ref[...], br, hb, cwb)
        drc, gwc, gbc = _conv_backward(dcc, ncc_ref[...], cr, hc, cwc)
        nx_ref[...] = dcx[0:8, :]
        nb_ref[...] = dcb[0:8, :]
        ncc_ref[...] = dcc[0:8, :]
        dssd_ref[:, SSD_WIDTH:2 * SSD_WIDTH] = drx.astype(dssd_ref.dtype)
        dssd_ref[:, 2 * SSD_WIDTH:2 * SSD_WIDTH + 512] = drb.astype(dssd_ref.dtype)
        dssd_ref[:, 2 * SSD_WIDTH + 512:SSD_SEG] = drc.astype(dssd_ref.dtype)
        for k in range(CONV_WIDTH):
            gcw_ref[k:k + 1, :] += jnp.concatenate([gwx[k], gwb[k], gwc[k]], axis=1)
        gcb_ref[...] += jnp.concatenate([gbx, gbb, gbc], axis=1)

    const = lambda shape: pl.BlockSpec(shape, lambda c: (0,) * len(shape))
    return pl.pallas_call(
        body, name="ssd_bwd", grid=(nc,),
        in_specs=[sp["z"], sp["xr"], sp["br"], sp["cr"], halo_spec(SSD_WIDTH, 1), halo_spec(512, 8), halo_spec(512, 9),
                  sp["dt"], sp["wide"], sp["states"], sp["wide"],
                  sp["cwx"], sp["cwb"], sp["cwc"], sp["cbx"], sp["cbb"], sp["cbc"],
                  sp["vec128"], sp["vec128"], sp["vecw"], sp["vecw"]],
        out_specs=[pl.BlockSpec((L, SSD_SEG), lambda c: (nc - 1 - c, 0)), sp["dt"],
                   const((CONV_WIDTH, SSD_CONV_CH)), const((1, SSD_CONV_CH)), const((1, LANES)), const((1, LANES)),
                   const((1, SSD_WIDTH)), const((1, SSD_WIDTH))],
        out_shape=[jax.ShapeDtypeStruct((t, SSD_SEG), MXU_DTYPE), jax.ShapeDtypeStruct((t, DT_PAD), MXU_DTYPE),
                   jax.ShapeDtypeStruct((CONV_WIDTH, SSD_CONV_CH), F32), jax.ShapeDtypeStruct((1, SSD_CONV_CH), F32),
                   jax.ShapeDtypeStruct((1, LANES), F32), jax.ShapeDtypeStruct((1, LANES), F32),
                   jax.ShapeDtypeStruct((1, SSD_WIDTH), F32), jax.ShapeDtypeStruct((1, SSD_WIDTH), F32)],
        scratch_shapes=[pltpu.VMEM((SSD_PAIRS, 128, SSD_STATE), F32), pltpu.VMEM((8, SSD_WIDTH), F32),
                        pltpu.VMEM((8, 512), F32), pltpu.VMEM((8, 512), F32), pltpu.VMEM((L, SSD_WIDTH), F32)],
        compiler_params=_params(("arbitrary",)),
    )(proj_ssd, proj_ssd, proj_ssd, proj_ssd, proj_ssd, proj_ssd, proj_ssd, dt_p, y, states, dyssd,
      conv_w, conv_w, conv_w, conv_b, conv_b, conv_b, dtb, alog, d_row, ng_row)


def _lru_gates(xl, wa_ref, wx_ref, ba, bx, lam):
    pre_a, pre_x = [], []
    for g in range(LRU_NGROUPS):
        xg = xl[:, g * LRU_GROUP:(g + 1) * LRU_GROUP]
        pre_a.append(_dot(xg, wa_ref[g], NN))
        pre_x.append(_dot(xg, wx_ref[g], NN))
    r = _sigmoid(jnp.concatenate(pre_a, axis=1) + ba)
    i = _sigmoid(jnp.concatenate(pre_x, axis=1) + bx)
    log_a = (-LRU_C * r) * _softplus(-lam)
    a = jnp.exp(log_a)
    mult = jnp.sqrt(-_expm1(2.0 * log_a))
    return r, i, log_a, a, mult


def _lru_fwd(proj_lru, conv_w, conv_b, wa, wx, ba, bx, lam):
    t = proj_lru.shape[0]
    rows = min(LRU_ROWS, t)
    nb = t // rows
    W = LRU_WIDTH

    def body(lg_ref, lx_ref, cw_ref, cb_ref, wa_ref, wx_ref, ba_ref, bx_ref, lam_ref, ylru_ref, h_ref,
             halo_ref, carry_ref):
        @pl.when(pl.program_id(0) == 0)
        def _():
            halo_ref[...] = jnp.zeros_like(halo_ref)
            carry_ref[...] = jnp.zeros_like(carry_ref)

        lx = lx_ref[...]
        xl = _causal_conv(lx, halo_ref[...], cw_ref[...], cb_ref[...])
        halo_ref[...] = lx[rows - 8:rows, :]
        _, i, _, a, mult = _lru_gates(xl, wa_ref, wx_ref, ba_ref[...], bx_ref[...], lam_ref[...])
        u = mult * (i * xl)
        row = lax.broadcasted_iota(jnp.int32, (rows, W), 0)
        p = a
        s = 1
        while s < rows:
            ok = row >= s
            u = p * jnp.where(ok, pltpu.roll(u, s, 0), 0.0) + u
            p = p * jnp.where(ok, pltpu.roll(p, s, 0), 1.0)
            s *= 2
        h = p * carry_ref[...] + u
        carry_ref[...] = h[rows - 1:rows, :]
        h_ref[...] = h
        lg = lg_ref[...]
        ylru_ref[...] = (h * (lg * _sigmoid(lg))).astype(ylru_ref.dtype)

    const = lambda shape: pl.BlockSpec(shape, lambda b: (0,) * len(shape))
    return pl.pallas_call(
        body, name="lru_fwd", grid=(nb,),
        in_specs=[pl.BlockSpec((rows, W), lambda b: (b, 0)), pl.BlockSpec((rows, W), lambda b: (b, 1)),
                  const((CONV_WIDTH, W)), const((1, W)), const((LRU_NGROUPS, LRU_GROUP, LRU_GROUP)),
                  const((LRU_NGROUPS, LRU_GROUP, LRU_GROUP)), const((1, W)), const((1, W)), const((1, W))],
        out_specs=[pl.BlockSpec((rows, W), lambda b: (b, 0)), pl.BlockSpec((rows, W), lambda b: (b, 0))],
        out_shape=[jax.ShapeDtypeStruct((t, W), MXU_DTYPE), jax.ShapeDtypeStruct((t, W), F32)],
        scratch_shapes=[pltpu.VMEM((8, W), F32), pltpu.VMEM((1, W), F32)],
        compiler_params=_params(("arbitrary",)),
    )(proj_lru, proj_lru, conv_w, conv_b, wa, wx, ba, bx, lam)


def _lru_bwd(proj_lru, h, dylru, conv_w, conv_b, wa, wx, ba, bx, lam):
    t = proj_lru.shape[0]
    rows = min(LRU_ROWS, t)
    nb = t // rows
    W = LRU_WIDTH
    groups8 = rows // 8

    def rev(b):
        return nb - 1 - b

    def halo_spec(col):
        return pl.BlockSpec((8, W), lambda b: (jnp.maximum(rev(b) * groups8 - 1, 0), col))

    def body(lg_ref, lx_ref, hlx_ref, h_ref, hh_ref, dy_ref, cw_ref, cb_ref, wa_ref, wx_ref, ba_ref, bx_ref, lam_ref,
             dlru_ref, gcw_ref, gcb_ref, gba_ref, gbx_ref, glam_ref, gwa_ref, gwx_ref,
             gcarry_ref, afirst_ref, nxt_ref):
        step = pl.program_id(0)

        @pl.when(step == 0)
        def _():
            gcarry_ref[...] = jnp.zeros_like(gcarry_ref)
            afirst_ref[...] = jnp.zeros_like(afirst_ref)
            nxt_ref[...] = jnp.zeros_like(nxt_ref)
            for ref in (gcw_ref, gcb_ref, gba_ref, gbx_ref, glam_ref, gwa_ref, gwx_ref):
                ref[...] = jnp.zeros_like(ref)

        keep = jnp.where(step == nb - 1, 0.0, 1.0)
        lx = lx_ref[...]
        hlx = hlx_ref[...] * keep
        cw = cw_ref[...]
        xl = _causal_conv(lx, hlx, cw, cb_ref[...])
        lam = lam_ref[...]
        r, i, log_a, a, mult = _lru_gates(xl, wa_ref, wx_ref, ba_ref[...], bx_ref[...], lam)
        hv = h_ref[...]
        h_prev = _shift_down(hv, hh_ref[...] * keep, 1)
        lg = lg_ref[...]
        sg = _sigmoid(lg)
        dyv = dy_ref[...]
        d_h = dyv * (lg * sg)
        dlru_ref[:, 0:W] = (dyv * hv * (sg * (1.0 + lg * (1.0 - sg)))).astype(dlru_ref.dtype)

        row = lax.broadcasted_iota(jnp.int32, (rows, W), 0)
        p = jnp.where(row < rows - 1, pltpu.roll(a, rows - 1, 0), afirst_ref[...])
        u = d_h
        s = 1
        while s < rows:
            ok = row < rows - s
            u = p * jnp.where(ok, pltpu.roll(u, rows - s, 0), 0.0) + u
            p = p * jnp.where(ok, pltpu.roll(p, rows - s, 0), 1.0)
            s *= 2
        gsc = p * gcarry_ref[...] + u
        gcarry_ref[...] = gsc[0:1, :]
        afirst_ref[...] = a[0:1, :]

        d_a = gsc * h_prev
        v = i * xl
        d_mult = gsc * v
        d_v = gsc * mult
        d_i = d_v * xl
        d_xl = d_v * i
        d_la = d_a * a - d_mult * (a * a) / mult
        sp_neg = _softplus(-lam)
        d_r = d_la * (-LRU_C * sp_neg)
        glam_ref[...] += jnp.sum(d_la * r, axis=0, keepdims=True) * (LRU_C * _sigmoid(-lam))
        d_pa = d_r * r * (1.0 - r)
        d_px = d_i * i * (1.0 - i)
        gba_ref[...] += jnp.sum(d_pa, axis=0, keepdims=True)
        gbx_ref[...] += jnp.sum(d_px, axis=0, keepdims=True)
        parts = []
        for g in range(LRU_NGROUPS):
            cols = slice(g * LRU_GROUP, (g + 1) * LRU_GROUP)
            xg, dpa_g, dpx_g = xl[:, cols], d_pa[:, cols], d_px[:, cols]
            parts.append(_dot(dpa_g, wa_ref[g], NT) + _dot(dpx_g, wx_ref[g], NT))
            gwa_ref[g] += _dot(xg, dpa_g, TN)
            gwx_ref[g] += _dot(xg, dpx_g, TN)
        d_xl = d_xl + jnp.concatenate(parts, axis=1)
        d_lx, gw, gb = _conv_backward(d_xl, nxt_ref[...], lx, hlx, cw)
        nxt_ref[...] = d_xl[0:8, :]
        dlru_ref[:, W:2 * W] = d_lx.astype(dlru_ref.dtype)
        for k in range(CONV_WIDTH):
            gcw_ref[k:k + 1, :] += gw[k]
        gcb_ref[...] += gb

    const = lambda shape: pl.BlockSpec(shape, lambda b: (0,) * len(shape))
    wspec = const((LRU_NGROUPS, LRU_GROUP, LRU_GROUP))
    blk = lambda col: pl.BlockSpec((rows, W), lambda b: (rev(b), col))
    return pl.pallas_call(
        body, name="lru_bwd", grid=(nb,),
        in_specs=[blk(0), blk(1), halo_spec(1), blk(0), halo_spec(0), blk(0),
                  const((CONV_WIDTH, W)), const((1, W)), wspec, wspec, const((1, W)), const((1, W)), const((1, W))],
        out_specs=[pl.BlockSpec((rows, 2 * W), lambda b: (rev(b), 0)), const((CONV_WIDTH, W)), const((1, W)),
                   const((1, W)), const((1, W)), const((1, W)), wspec, wspec],
        out_shape=[jax.ShapeDtypeStruct((t, 2 * W), MXU_DTYPE), jax.ShapeDtypeStruct((CONV_WIDTH, W), F32),
                   jax.ShapeDtypeStruct((1, W), F32), jax.ShapeDtypeStruct((1, W), F32), jax.ShapeDtypeStruct((1, W), F32),
                   jax.ShapeDtypeStruct((1, W), F32), jax.ShapeDtypeStruct((LRU_NGROUPS, LRU_GROUP, LRU_GROUP), F32),
                   jax.ShapeDtypeStruct((LRU_NGROUPS, LRU_GROUP, LRU_GROUP), F32)],
        scratch_shapes=[pltpu.VMEM((1, W), F32), pltpu.VMEM((1, W), F32), pltpu.VMEM((8, W), F32)],
        compiler_params=_params(("arbitrary",)),
    )(proj_lru, proj_lru, proj_lru, h, h, dylru, conv_w, conv_b, wa, wx, ba, bx, lam)


def _mem_scores(q_h, k_h):
    s = _dot(q_h, k_h, NT) * (MEM_HEAD_DIM ** -0.5)
    s = s - jnp.max(s, axis=-1, keepdims=True)
    e = jnp.exp(s)
    return e / jnp.sum(e, axis=-1, keepdims=True)


def _mem_fwd(q, kv, rows=512):
    t = q.shape[0]
    rows = min(rows, t)
    m = kv.shape[0]

    def body(q_ref, kv_ref, y_ref):
        for hd in range(MEM_HEADS):
            cols = slice(hd * MEM_HEAD_DIM, (hd + 1) * MEM_HEAD_DIM)
            vcols = slice(D_MODEL + hd * MEM_HEAD_DIM, D_MODEL + (hd + 1) * MEM_HEAD_DIM)
            p = _mem_scores(q_ref[:, cols], kv_ref[:, cols])
            y_ref[:, cols] = _dot(p, kv_ref[:, vcols], NN).astype(y_ref.dtype)

    return pl.pallas_call(
        body, name="mem_fwd", grid=(t // rows,),
        in_specs=[pl.BlockSpec((rows, D_MODEL), lambda i: (i, 0)), pl.BlockSpec((m, 2 * D_MODEL), lambda i: (0, 0))],
        out_specs=pl.BlockSpec((rows, D_MODEL), lambda i: (i, 0)),
        out_shape=jax.ShapeDtypeStruct((t, D_MODEL), MXU_DTYPE),
        compiler_params=_params(("parallel",)),
    )(q, kv)


def _mem_bwd(q, kv, dy, rows=512):
    t = q.shape[0]
    rows = min(rows, t)
    m = kv.shape[0]

    def body(q_ref, kv_ref, dy_ref, dq_ref, dkv_ref):
        @pl.when(pl.program_id(0) == 0)
        def _():
            dkv_ref[...] = jnp.zeros_like(dkv_ref)

        for hd in range(MEM_HEADS):
            cols = slice(hd * MEM_HEAD_DIM, (hd + 1) * MEM_HEAD_DIM)
            vcols = slice(D_MODEL + hd * MEM_HEAD_DIM, D_MODEL + (hd + 1) * MEM_HEAD_DIM)
            q_h, k_h, dy_h = q_ref[:, cols], kv_ref[:, cols], dy_ref[:, cols]
            p = _mem_scores(q_h, k_h)
            dp = _dot(dy_h, kv_ref[:, vcols], NT)
            dkv_ref[:, vcols] += _dot(p, dy_h, TN)
            ds = p * (dp - jnp.sum(dp * p, axis=-1, keepdims=True)) * (MEM_HEAD_DIM ** -0.5)
            dq_ref[:, cols] = _dot(ds, k_h, NN).astype(dq_ref.dtype)
            dkv_ref[:, cols] += _dot(ds, q_h, TN)

    return pl.pallas_call(
        body, name="mem_bwd", grid=(t // rows,),
        in_specs=[pl.BlockSpec((rows, D_MODEL), lambda i: (i, 0)), pl.BlockSpec((m, 2 * D_MODEL), lambda i: (0, 0)),
                  pl.BlockSpec((rows, D_MODEL), lambda i: (i, 0))],
        out_specs=[pl.BlockSpec((rows, D_MODEL), lambda i: (i, 0)), pl.BlockSpec((m, 2 * D_MODEL), lambda i: (0, 0))],
        out_shape=[jax.ShapeDtypeStruct((t, D_MODEL), MXU_DTYPE), jax.ShapeDtypeStruct((m, 2 * D_MODEL), F32)],
        compiler_params=_params(("arbitrary",)),
    )(q, kv, dy)


def _merge_fwd(x, yssd, ylru, ymem, gl, w_bs, w_bl, w_bm, w_out, fg, tgt, rows=256):
    t = x.shape[0]
    rows = min(rows, t)
    D = D_MODEL

    def body(x_ref, ys_ref, yl_ref, ym_ref, gl_ref, wbs_ref, wbl_ref, wbm_ref, wo_ref, fg_ref, tgt_ref,
             ps_ref, pl_ref, pm_ref, mg_ref, dx2_ref, loss_ref, gfg_ref):
        @pl.when(pl.program_id(0) == 0)
        def _():
            loss_ref[...] = jnp.zeros_like(loss_ref)
            gfg_ref[...] = jnp.zeros_like(gfg_ref)

        ps = _dot(ys_ref[...], wbs_ref[...], NN)
        pl_ = _dot(yl_ref[...], wbl_ref[...], NN)
        pm = _dot(ym_ref[...], wbm_ref[...], NN)
        ps_ref[...] = ps
        pl_ref[...] = pl_
        pm_ref[...] = pm
        merged = (_sigmoid(gl_ref[:, 0:D]) * ps + _sigmoid(gl_ref[:, D:2 * D]) * pl_) + _sigmoid(gl_ref[:, 2 * D:3 * D]) * pm
        mg_ref[...] = merged.astype(mg_ref.dtype)
        x2 = x_ref[...] + _dot(merged, wo_ref[...], NN)
        r2 = lax.rsqrt(jnp.mean(x2 * x2, axis=-1, keepdims=True) + EPS)
        xn = x2 * r2
        fg = fg_ref[...]
        diff = xn * fg - tgt_ref[...]
        tile_loss = 0.5 * jnp.sum(jnp.mean(diff * diff, axis=-1, keepdims=True), axis=0, keepdims=True)
        loss_ref[...] += jnp.broadcast_to(tile_loss, loss_ref.shape)
        d_out = diff * (1.0 / D)
        gfg_ref[...] += jnp.sum(d_out * xn, axis=0, keepdims=True)
        dxn = d_out * fg
        dx2_ref[...] = r2 * (dxn - xn * jnp.mean(dxn * xn, axis=-1, keepdims=True))

    row = lambda w: pl.BlockSpec((rows, w), lambda i: (i, 0))
    const = lambda shape: pl.BlockSpec(shape, lambda i: (0,) * len(shape))
    return pl.pallas_call(
        body, name="merge_fwd", grid=(t // rows,),
        in_specs=[row(D), row(SSD_WIDTH), row(LRU_WIDTH), row(D), row(3 * D), const((SSD_WIDTH, D)), const((LRU_WIDTH, D)),
                  const((D, D)), const((D, D)), const((1, D)), row(D)],
        out_specs=[row(D), row(D), row(D), row(D), row(D), const((1, LANES)), const((1, D))],
        out_shape=[jax.ShapeDtypeStruct((t, D), F32), jax.ShapeDtypeStruct((t, D), F32), jax.ShapeDtypeStruct((t, D), F32),
                   jax.ShapeDtypeStruct((t, D), MXU_DTYPE), jax.ShapeDtypeStruct((t, D), F32),
                   jax.ShapeDtypeStruct((1, LANES), F32), jax.ShapeDtypeStruct((1, D), F32)],
        compiler_params=_params(("arbitrary",)),
    )(x, yssd, ylru, ymem, gl, w_bs, w_bl, w_bm, w_out, fg, tgt)


def _merge_bwd(dx2, gl, ps, pl_in, pm, w_bs, w_bl, w_bm, w_out, rows=256):
    t = dx2.shape[0]
    rows = min(rows, t)
    D = D_MODEL

    def body(dx2_ref, gl_ref, ps_ref, pl_ref, pm_ref, wbs_ref, wbl_ref, wbm_ref, wo_ref,
             dg_ref, dps_ref, dpl_ref, dpm_ref, dys_ref, dyl_ref, dym_ref):
        dm = _dot(dx2_ref[...], wo_ref[...], NT)
        for idx, (p_ref, dp_ref, w_ref, dy_ref) in enumerate(
                ((ps_ref, dps_ref, wbs_ref, dys_ref), (pl_ref, dpl_ref, wbl_ref, dyl_ref), (pm_ref, dpm_ref, wbm_ref, dym_ref))):
            gate = _sigmoid(gl_ref[:, idx * D:(idx + 1) * D])
            dg_ref[:, idx * D:(idx + 1) * D] = ((dm * p_ref[...]) * gate * (1.0 - gate)).astype(dg_ref.dtype)
            dp = dm * gate
            dp_ref[...] = dp.astype(dp_ref.dtype)
            dy_ref[...] = _dot(dp, w_ref[...], NT)

    row = lambda w: pl.BlockSpec((rows, w), lambda i: (i, 0))
    const = lambda shape: pl.BlockSpec(shape, lambda i: (0,) * len(shape))
    return pl.pallas_call(
        body, name="merge_bwd", grid=(t // rows,),
        in_specs=[row(D), row(3 * D), row(D), row(D), row(D), const((SSD_WIDTH, D)), const((LRU_WIDTH, D)),
                  const((D, D)), const((D, D))],
        out_specs=[row(3 * D), row(D), row(D), row(D), row(SSD_WIDTH), row(LRU_WIDTH), row(D)],
        out_shape=[jax.ShapeDtypeStruct((t, 3 * D), MXU_DTYPE), jax.ShapeDtypeStruct((t, D), MXU_DTYPE),
                   jax.ShapeDtypeStruct((t, D), MXU_DTYPE), jax.ShapeDtypeStruct((t, D), MXU_DTYPE),
                   jax.ShapeDtypeStruct((t, SSD_WIDTH), F32), jax.ShapeDtypeStruct((t, LRU_WIDTH), F32),
                   jax.ShapeDtypeStruct((t, D), F32)],
        compiler_params=_params(("parallel",)),
    )(dx2, gl, ps, pl_in, pm, w_bs, w_bl, w_bm, w_out)


def _mesh_place():
    x, y, c = lax.axis_index("x"), lax.axis_index("y"), lax.axis_index("c")
    return x, y, c, 4 * x + 2 * y + c


def _other_chips(x, y):
    return [(1 - x, y), (x, 1 - y), (1 - x, 1 - y)]


def _all_gather(arrs, name):
    n = len(arrs)

    def body(*refs):
        ins, outs = refs[:n], refs[n:2 * n]
        send_sems, recv_sems, local_sems = refs[2 * n:]
        x, y, c, me = _mesh_place()
        sibling = (x, y, 1 - c)
        chips = _other_chips(x, y)

        def slot(px, py, pc):
            return 4 * px + 2 * py + pc

        def copy(a, k, block, to, src=None):
            return pltpu.make_async_remote_copy(
                src_ref=outs[a].at[block] if src is None else src, dst_ref=outs[a].at[block],
                send_sem=send_sems.at[a, k], recv_sem=recv_sems.at[a, k], device_id=to, device_id_type=pl.DeviceIdType.MESH)

        local = [pltpu.make_async_copy(ins[a], outs[a].at[me], local_sems.at[a]) for a in range(n)]
        for cp in local:
            cp.start()
        sends = []
        for a in range(n):
            sends.append(copy(a, 0, me, sibling, src=ins[a]))
            for j, chip in enumerate(chips):
                sends.append(copy(a, 1 + j, me, (*chip, c), src=ins[a]))
        for cp in sends:
            cp.start()
        for j, chip in enumerate(chips):
            for a in range(n):
                copy(a, 1 + j, slot(*chip, c), sibling).wait_recv()
                passed = copy(a, 4 + j, slot(*chip, c), sibling)
                passed.start()
                sends.append(passed)
        for a in range(n):
            copy(a, 0, slot(x, y, 1 - c), sibling).wait_recv()
        for j, chip in enumerate(chips):
            for a in range(n):
                copy(a, 4 + j, slot(*chip, 1 - c), sibling).wait_recv()
        for cp in sends:
            cp.wait_send()
        for cp in local:
            cp.wait()

    any_spec = pl.BlockSpec(memory_space=pl.ANY)
    return pl.pallas_call(
        body, name=name, in_specs=[any_spec] * n, out_specs=[any_spec] * n,
        out_shape=[jax.ShapeDtypeStruct((N_DEV,) + a.shape, a.dtype) for a in arrs],
        scratch_shapes=[pltpu.SemaphoreType.DMA((n, 7)), pltpu.SemaphoreType.DMA((n, 7)), pltpu.SemaphoreType.DMA((n,))],
    )(*arrs)


N_CHIPS = 4


def _pair_exchange(parts):
    n = len(parts)

    def body(*refs):
        ins, outs = refs[:n], refs[n:2 * n]
        send_sems, recv_sems = refs[2 * n:]
        x, y, c, _ = _mesh_place()
        sibling = (x, y, 1 - c)
        sends = []
        for a in range(n):
            for q in range(N_CHIPS):
                cp = pltpu.make_async_remote_copy(src_ref=ins[a].at[q, 1 - c], dst_ref=outs[a].at[q], send_sem=send_sems.at[a, q],
                                                  recv_sem=recv_sems.at[a, q], device_id=sibling, device_id_type=pl.DeviceIdType.MESH)
                cp.start()
                sends.append(cp)
        for cp in sends:
            cp.wait_recv()
        for cp in sends:
            cp.wait_send()

    any_spec = pl.BlockSpec(memory_space=pl.ANY)
    return pl.pallas_call(
        body, name="grad_pair_exchange", in_specs=[any_spec] * n, out_specs=[any_spec] * n,
        out_shape=[jax.ShapeDtypeStruct((N_CHIPS,) + a.shape[2:], a.dtype) for a in parts],
        scratch_shapes=[pltpu.SemaphoreType.DMA((n, N_CHIPS)), pltpu.SemaphoreType.DMA((n, N_CHIPS))],
    )(*parts)


def _chip_sum(part, recv, core, name):
    _, _, r, c = part.shape
    rows = _row_tile(r, max(8, (2 << 20) // (c * 4) // 8 * 8))

    def body(core_ref, p_ref, r_ref, s_ref, t_ref):
        s = p_ref[...] + r_ref[...]
        s_ref[...] = s
        t_ref[...] = s.astype(t_ref.dtype)

    blk = pl.BlockSpec((None, rows, c), lambda q, i, core_ref: (q, i, 0))
    return pl.pallas_call(
        body, name=name,
        grid_spec=pltpu.PrefetchScalarGridSpec(
            num_scalar_prefetch=1, grid=(N_CHIPS, r // rows),
            in_specs=[pl.BlockSpec((None, None, rows, c), lambda q, i, core_ref: (q, core_ref[0], i, 0)), blk],
            out_specs=[blk, blk]),
        out_shape=[jax.ShapeDtypeStruct((N_CHIPS, r, c), F32), jax.ShapeDtypeStruct((N_CHIPS, r, c), GRAD_WIRE_DTYPE)],
        compiler_params=_params(("parallel", "parallel")),
    )(core, part, recv)


def _chip_exchange(sums):
    n = len(sums)

    def body(*refs):
        ins, outs = refs[:n], refs[n:2 * n]
        send_sems, recv_sems = refs[2 * n:]
        x, y, c, _ = _mesh_place()
        my_chip = 2 * x + y
        sends = []
        for a in range(n):
            for j, (px, py) in enumerate(_other_chips(x, y)):
                cp = pltpu.make_async_remote_copy(src_ref=ins[a].at[2 * px + py], dst_ref=outs[a].at[my_chip], send_sem=send_sems.at[a, j],
                                                  recv_sem=recv_sems.at[a, j], device_id=(px, py, c), device_id_type=pl.DeviceIdType.MESH)
                cp.start()
                sends.append(cp)
        for a in range(n):
            for j, (px, py) in enumerate(_other_chips(x, y)):
                pltpu.make_async_remote_copy(src_ref=ins[a].at[my_chip], dst_ref=outs[a].at[2 * px + py], send_sem=send_sems.at[a, j],
                                             recv_sem=recv_sems.at[a, j], device_id=(px, py, c),
                                             device_id_type=pl.DeviceIdType.MESH).wait_recv()
        for cp in sends:
            cp.wait_send()

    any_spec = pl.BlockSpec(memory_space=pl.ANY)
    return pl.pallas_call(
        body, name="grad_chip_exchange", in_specs=[any_spec] * n, out_specs=[any_spec] * n,
        out_shape=[jax.ShapeDtypeStruct(a.shape, a.dtype) for a in sums],
        scratch_shapes=[pltpu.SemaphoreType.DMA((n, 3)), pltpu.SemaphoreType.DMA((n, 3))],
    )(*sums)


def _row_tile(r, limit):
    if r <= limit:
        return r
    best = 8
    for cand in range(8, limit + 1, 8):
        if r % cand == 0:
            best = cand
    assert r % best == 0, r
    return best


def _adam_update(w, g, m, v):
    nm = ADAM_B1 * m + (1.0 - ADAM_B1) * g
    nv = ADAM_B2 * v + (1.0 - ADAM_B2) * (g * g)
    m_hat = nm / (1.0 - ADAM_B1 ** ADAM_STEP)
    v_hat = nv / (1.0 - ADAM_B2 ** ADAM_STEP)
    return -ADAM_LR * (m_hat / (jnp.sqrt(v_hat) + ADAM_EPS) + ADAM_WD * w), nm, nv


def _sum_adamw(own, recv, chip, w, m, v, name):
    _, r, c = own.shape
    rows = _row_tile(r, max(8, (1 << 20) // (c * 4) // 8 * 8))

    def body(chip_ref, o_ref, r1_ref, r2_ref, r3_ref, w_ref, m_ref, v_ref, g_ref, d_ref, nm_ref, nv_ref):
        g = ((o_ref[...] + r1_ref[...].astype(F32)) + r2_ref[...].astype(F32)) + r3_ref[...].astype(F32)
        g_ref[...] = g
        d_ref[...], nm_ref[...], nv_ref[...] = _adam_update(w_ref[...], g, m_ref[...], v_ref[...])

    def slot(k):
        return pl.BlockSpec((None, rows, c), lambda i, chip_ref: ((chip_ref[0] + k) % N_CHIPS, i, 0))

    spec = pl.BlockSpec((rows, c), lambda i, chip_ref: (i, 0))
    shape = jax.ShapeDtypeStruct((r, c), F32)
    return pl.pallas_call(
        body, name=name,
        grid_spec=pltpu.PrefetchScalarGridSpec(
            num_scalar_prefetch=1, grid=(r // rows,),
            in_specs=[slot(0), slot(1), slot(2), slot(3), spec, spec, spec], out_specs=[spec] * 4),
        out_shape=[shape] * 4,
        compiler_params=_params(("parallel",)),
    )(chip, own, recv, recv, recv, w, m, v)


def _sum_slots(parts, name, rows=512):
    s, r, _ = parts.shape
    rows = _row_tile(r, rows)

    def body(p_ref, o_ref):
        acc = p_ref[0]
        for k in range(1, s):
            acc = acc + p_ref[k]
        o_ref[...] = acc

    return pl.pallas_call(
        body, name=name, grid=(r // rows,),
        in_specs=[pl.BlockSpec((s, rows, LANES), lambda i: (0, i, 0))],
        out_specs=pl.BlockSpec((rows, LANES), lambda i: (i, 0)),
        out_shape=jax.ShapeDtypeStruct((r, LANES), F32),
        compiler_params=_params(("parallel",)),
    )(parts)


def _adamw(w, g, m, v, name, rows=512):
    r = w.shape[0]
    rows = _row_tile(r, rows)

    def body(w_ref, g_ref, m_ref, v_ref, d_ref, nm_ref, nv_ref):
        d_ref[...], nm_ref[...], nv_ref[...] = _adam_update(w_ref[...], g_ref[...], m_ref[...], v_ref[...])

    spec = pl.BlockSpec((rows, LANES), lambda i: (i, 0))
    shape = jax.ShapeDtypeStruct((r, LANES), F32)
    return pl.pallas_call(
        body, name=name, grid=(r // rows,), in_specs=[spec] * 4, out_specs=[spec] * 3, out_shape=[shape] * 3,
        compiler_params=_params(("parallel",)),
    )(w, g, m, v)


def _pack(arrs, dtype, row_multiple):
    flat = jnp.concatenate([a.reshape(-1).astype(dtype) for a in arrs])
    unit = LANES * row_multiple
    padded = -(-flat.shape[0] // unit) * unit
    return jnp.pad(flat, (0, padded - flat.shape[0])).reshape(-1, LANES)


def _unpack(packed, shapes, lead=()):
    flat = packed.reshape(lead + (-1,))
    out, off = [], 0
    for shp in shapes:
        n = math.prod(shp)
        out.append(flat[..., off:off + n].reshape(lead + tuple(shp)))
        off += n
    return out


def _gather_cols(g, lo, hi):
    width = g.shape[2]
    pieces = []
    for s in range(N_DEV):
        a, e = max(lo, s * width), min(hi, (s + 1) * width)
        if a < e:
            pieces.append(g[s, :, a - s * width:e - s * width])
    return pieces[0] if len(pieces) == 1 else jnp.concatenate(pieces, axis=1)


def _scatter_cols(segs, width):
    slots = []
    for k in range(N_DEV):
        lo, hi = k * width, (k + 1) * width
        pieces = []
        for arr, s_lo, s_hi in segs:
            a, e = max(lo, s_lo), min(hi, s_hi)
            if a < e:
                pieces.append(arr[:, a - s_lo:e - s_lo])
        slots.append(pieces[0] if len(pieces) == 1 else jnp.concatenate(pieces, axis=1))
    return jnp.stack(slots)


def _block_diag_groups(w):
    w4 = w.reshape(LRU_NGROUPS, 4, LRU_BLOCK, LRU_BLOCK)
    eye = jnp.eye(4, dtype=w.dtype)
    return jnp.einsum("gaij,ab->gaibj", w4, eye).reshape(LRU_NGROUPS, LRU_GROUP, LRU_GROUP)


def _block_diag_extract(wg):
    w5 = wg.reshape(LRU_NGROUPS, 4, LRU_BLOCK, 4, LRU_BLOCK)
    idx = jnp.arange(4)
    return w5[:, idx, :, idx, :].transpose(1, 0, 2, 3).reshape(LRU_BLOCKS, LRU_BLOCK, LRU_BLOCK)


BIG = ("w_in", "w_kv", "w_br_ssd", "w_br_lru", "w_br_mem", "w_out")
SMALL_SHARDED = ("ssd_conv_w", "ssd_norm_g", "lru_conv_w")
REPLICATED = ("norm_g", "ssd_conv_b", "ssd_dt_bias", "ssd_a_log", "ssd_d", "lru_conv_b", "lru_w_a", "lru_b_a",
              "lru_w_x", "lru_b_x", "lru_lambda", "mem_norm_g", "final_g")
WEIGHTS = ("norm_g", "w_in", "ssd_conv_w", "ssd_conv_b", "ssd_dt_bias", "ssd_a_log", "ssd_d", "ssd_norm_g", "lru_conv_w",
           "lru_conv_b", "lru_w_a", "lru_b_a", "lru_w_x", "lru_b_x", "lru_lambda", "mem_norm_g", "w_kv", "w_br_ssd",
           "w_br_lru", "w_br_mem", "w_out", "final_g")


def kernel(x, mem, norm_g, w_in, ssd_conv_w, ssd_conv_b, ssd_dt_bias, ssd_a_log, ssd_d, ssd_norm_g, lru_conv_w, lru_conv_b, lru_w_a, lru_b_a, lru_w_x, lru_b_x, lru_lambda, mem_norm_g, w_kv, w_br_ssd, w_br_lru, w_br_mem, w_out, final_g, loss_target, m_norm_g, m_w_in, m_ssd_conv_w, m_ssd_conv_b, m_ssd_dt_bias, m_ssd_a_log, m_ssd_d, m_ssd_norm_g, m_lru_conv_w, m_lru_conv_b, m_lru_w_a, m_lru_b_a, m_lru_w_x, m_lru_b_x, m_lru_lambda, m_mem_norm_g, m_w_kv, m_w_br_ssd, m_w_br_lru, m_w_br_mem, m_w_out, m_final_g, v_norm_g, v_w_in, v_ssd_conv_w, v_ssd_conv_b, v_ssd_dt_bias, v_ssd_a_log, v_ssd_d, v_ssd_norm_g, v_lru_conv_w, v_lru_conv_b, v_lru_w_a, v_lru_b_a, v_lru_w_x, v_lru_b_x, v_lru_lambda, v_mem_norm_g, v_w_kv, v_w_br_ssd, v_w_br_lru, v_w_br_mem, v_w_out, v_final_g):
    env = dict(locals())
    W = {n: env[n] for n in WEIGHTS}
    M = {n: env["m_" + n] for n in WEIGHTS}
    V = {n: env["v_" + n] for n in WEIGHTS}
    me = 4 * lax.axis_index("x") + 2 * lax.axis_index("y") + lax.axis_index("c")
    t = x.shape[1]
    xt = x[0]
    memt = mem[0]
    tgt = loss_target[0]

    small_shapes = [W[n].shape for n in SMALL_SHARDED]
    gathered = _all_gather([W[n][0].astype(MXU_DTYPE) for n in BIG] + [_pack([W[n] for n in SMALL_SHARDED], F32, 8)],
                           "weights_all_gather")
    g_in, g_kv, g_bs, g_bl, g_bm, g_out, gs = gathered
    g_cw, g_ng, g_lcw = _unpack(gs, small_shapes, (N_DEV,))
    cols = lambda a: jnp.moveaxis(a[:, 0], 0, -2).reshape(a.shape[2:-1] + (-1,))
    rows_ = lambda a: a.reshape((-1,) + a.shape[2:])
    w_bs_f, w_bl_f, w_bm_f, w_out_f = rows_(g_bs), rows_(g_bl), rows_(g_bm), rows_(g_out)
    conv_w_f, ssd_ng_f, lru_cw_f = cols(g_cw), cols(g_ng), cols(g_lcw)
    b = SEG_BOUNDS
    w_kv_f = _gather_cols(g_kv, 0, 2 * D_MODEL)
    w_ssd, w_lru, w_q, w_g = (_gather_cols(g_in, b[0], b[1]), _gather_cols(g_in, b[2], b[3]),
                              _gather_cols(g_in, b[3], b[4]), _gather_cols(g_in, b[4], b[5]))
    w_dt = jnp.pad(_gather_cols(g_in, b[1], b[2]), ((0, 0), (0, DT_PAD - SSD_HEADS)))

    pad_heads = lambda a: jnp.pad(a, ((0, 0), (0, LANES - SSD_HEADS)))
    dtb, alog = pad_heads(ssd_dt_bias), pad_heads(ssd_a_log)
    d_row = jnp.repeat(ssd_d, SSD_HEAD_DIM, axis=1)
    ng_row = ssd_ng_f.reshape(1, SSD_WIDTH)
    wa_g, wx_g = _block_diag_groups(lru_w_a[0]), _block_diag_groups(lru_w_x[0])
    ba, bx = lru_b_a.reshape(1, LRU_WIDTH), lru_b_x.reshape(1, LRU_WIDTH)
    fg = final_g.reshape(1, D_MODEL)

    h = _rms_fwd(xt, norm_g, "norm_fwd")
    proj_ssd = _matmul(h, w_ssd, "nn", "proj_ssd", tk=D_MODEL)
    proj_lru = _matmul(h, w_lru, "nn", "proj_lru", tk=D_MODEL)
    proj_q = _matmul(h, w_q, "nn", "proj_q", tk=D_MODEL)
    proj_g = _matmul(h, w_g, "nn", "proj_g", tk=D_MODEL)
    proj_dt = _matmul(h, w_dt, "nn", "proj_dt", tk=D_MODEL)
    mem_n = _rms_fwd(memt, mem_norm_g, "mem_norm_fwd")
    kv = _matmul(mem_n, w_kv_f, "nn", "mem_kv", tk=D_MODEL)
    yssd, y_scan, states = _ssd_fwd(proj_ssd, proj_dt, conv_w_f, ssd_conv_b, dtb, alog, d_row, ng_row)
    ylru, h_lru = _lru_fwd(proj_lru, lru_cw_f, lru_conv_b, wa_g, wx_g, ba, bx, lru_lambda)
    ymem = _mem_fwd(proj_q, kv)
    ps, pl_, pm, merged, dx2, loss_vec, g_fg = _merge_fwd(xt, yssd, ylru, ymem, proj_g, w_bs_f, w_bl_f, w_bm_f, w_out_f, fg, tgt)

    d_g, dps, dpl, dpm, dyssd, dylru, dymem = _merge_bwd(dx2, proj_g, ps, pl_, pm, w_bs_f, w_bl_f, w_bm_f, w_out_f)
    gw_out = _matmul(merged, dx2, "tn", "grad_w_out", tk=1024)
    gw_bs = _matmul(yssd, dps, "tn", "grad_w_br_ssd", tk=1024)
    gw_bl = _matmul(ylru, dpl, "tn", "grad_w_br_lru", tm=LRU_WIDTH, tk=1024)
    gw_bm = _matmul(ymem, dpm, "tn", "grad_w_br_mem", tk=1024)
    d_q, d_kv = _mem_bwd(proj_q, kv, dymem)
    gw_kv = _matmul(mem_n, d_kv, "tn", "grad_w_kv", tk=memt.shape[0])
    d_memn = _matmul(d_kv, w_kv_f, "nt", "d_mem_n", tk=1024)
    _, g_memng = _rms_bwd(memt, d_memn, None, mem_norm_g, "mem_norm_bwd")
    d_lru, gl_cw, gl_cb, g_ba, g_bx, g_lam, gwa_g, gwx_g = _lru_bwd(proj_lru, h_lru, dylru, lru_cw_f, lru_conv_b, wa_g, wx_g, ba, bx, lru_lambda)
    d_ssd, d_dt, gs_cw, gs_cb, g_dtb, g_alog, g_dch, g_ngrow = _ssd_bwd(proj_ssd, proj_dt, y_scan, states, dyssd, conv_w_f, ssd_conv_b, dtb, alog, d_row, ng_row)
    dh = _matmul(d_ssd, w_ssd, "nt", "dh_ssd", tk=1024)
    dh = _matmul(d_lru, w_lru, "nt", "dh_lru", tk=1024, acc_in=dh)
    dh = _matmul(d_q, w_q, "nt", "dh_q", tk=1024, acc_in=dh)
    dh = _matmul(d_g, w_g, "nt", "dh_g", tk=1024, acc_in=dh)
    dh = _matmul(d_dt, w_dt, "nt", "dh_dt", tk=DT_PAD, acc_in=dh)
    gw_ssd = _matmul(h, d_ssd, "tn", "grad_w_in_ssd", tk=1024)
    gw_lru = _matmul(h, d_lru, "tn", "grad_w_in_lru", tk=1024)
    gw_q = _matmul(h, d_q, "tn", "grad_w_in_q", tk=1024)
    gw_g = _matmul(h, d_g, "tn", "grad_w_in_g", tk=1024)
    gw_dt = _matmul(h, d_dt, "tn", "grad_w_in_dt", tk=1024)
    grad_x, g_normg = _rms_bwd(xt, dh, dx2, norm_g, "norm_bwd")

    split_rows = lambda a: a.reshape((N_DEV, -1) + a.shape[1:])
    in_segs = [(gw_ssd, b[0], b[1]), (gw_dt, b[1], b[2]), (gw_lru, b[2], b[3]), (gw_q, b[3], b[4]), (gw_g, b[4], b[5])]
    big_send = [_scatter_cols(in_segs, IN_WIDTH // N_DEV), _scatter_cols([(gw_kv, 0, 2 * D_MODEL)], 2 * D_MODEL // N_DEV),
                split_rows(gw_bs), split_rows(gw_bl), split_rows(gw_bm), split_rows(gw_out)]

    small_grads = {
        "norm_g": g_normg, "ssd_conv_w": gs_cw, "ssd_conv_b": gs_cb, "ssd_dt_bias": g_dtb[:, :SSD_HEADS],
        "ssd_a_log": g_alog[:, :SSD_HEADS], "ssd_d": jnp.sum(g_dch.reshape(SSD_HEADS, SSD_HEAD_DIM), axis=1).reshape(1, SSD_HEADS),
        "ssd_norm_g": g_ngrow.reshape(SSD_GROUPS, -1), "lru_conv_w": gl_cw, "lru_conv_b": gl_cb,
        "lru_w_a": _block_diag_extract(gwa_g), "lru_b_a": g_ba, "lru_w_x": _block_diag_extract(gwx_g), "lru_b_x": g_bx,
        "lru_lambda": g_lam, "mem_norm_g": g_memng, "final_g": g_fg,
    }
    small_names = tuple(small_grads)
    small_send = _pack([small_grads[n] for n in small_names], F32, 512)
    (small_recv,) = _all_gather([small_send], "small_grads_all_gather")

    core = lax.axis_index("c").astype(jnp.int32).reshape(1)
    chip = (2 * lax.axis_index("x") + lax.axis_index("y")).astype(jnp.int32).reshape(1)
    by_chip = [a.reshape((N_CHIPS, 2) + a.shape[1:]) for a in big_send]
    from_sibling = _pair_exchange(by_chip)
    chip_sums = [_chip_sum(p, r, core, "chip_sum_" + n) for n, p, r in zip(BIG, by_chip, from_sibling)]
    from_chips = _chip_exchange([s16 for _, s16 in chip_sums])

    grads, delta, new_m, new_v = {}, {}, {}, {}
    for n, (s32, _), recv in zip(BIG, chip_sums, from_chips):
        res = _sum_adamw(s32, recv, chip, W[n][0], M[n][0], V[n][0], "adamw_" + n)
        for dst, a in zip((grads, delta, new_m, new_v), res):
            dst[n] = a[None]

    totals = dict(zip(small_names, _unpack(_sum_slots(small_recv, "grad_sum_small"), [small_grads[n].shape for n in small_names])))
    for n in REPLICATED:
        grads[n] = totals[n].reshape(W[n].shape)
    for n in SMALL_SHARDED:
        width = W[n].shape[-1]
        grads[n] = lax.dynamic_slice_in_dim(totals[n], me * width, width, axis=-1).reshape(W[n].shape)
    small_all = REPLICATED + SMALL_SHARDED
    w_s = _pack([W[n] for n in small_all], F32, 512)
    g_s = _pack([grads[n] for n in small_all], F32, 512)
    m_s = _pack([M[n] for n in small_all], F32, 512)
    v_s = _pack([V[n] for n in small_all], F32, 512)
    d_s, nm_s, nv_s = _adamw(w_s, g_s, m_s, v_s, "adamw_small")

    for packed, names, dst in ((d_s, small_all, delta), (nm_s, small_all, new_m), (nv_s, small_all, new_v)):
        for n, a in zip(names, _unpack(packed, [W[k].shape for k in names])):
            dst[n] = a

    loss = lax.psum(loss_vec[0, 0], ("x", "y", "c"))
    return (loss, grad_x[None], *[grads[n] for n in WEIGHTS], *[delta[n] for n in WEIGHTS],
            *[new_m[n] for n in WEIGHTS], *[new_v[n] for n in WEIGHTS])
```

```python
import functools
import math

import jax
import jax.numpy as jnp
from jax import lax
from jax.experimental import pallas as pl
from jax.experimental.pallas import tpu as pltpu

F32 = jnp.float32
MXU_DTYPE = jnp.bfloat16
GRAD_WIRE_DTYPE = jnp.bfloat16

D_MODEL = 1024
EPS = 1e-6
CONV_WIDTH = 4
SSD_WIDTH = 2048
SSD_HEAD_DIM = 64
SSD_HEADS = 32
SSD_GROUPS = 4
SSD_STATE = 128
SSD_CHUNK = 128
SSD_CONV_CH = SSD_WIDTH + 2 * SSD_GROUPS * SSD_STATE
SSD_PAIRS = SSD_HEADS // 2
PAIRS_PER_GROUP = SSD_PAIRS // SSD_GROUPS
LRU_WIDTH = 1536
LRU_BLOCKS = 16
LRU_BLOCK = 96
LRU_GROUP = 4 * LRU_BLOCK
LRU_NGROUPS = LRU_WIDTH // LRU_GROUP
LRU_C = 8.0
LRU_ROWS = 256
MEM_HEADS = 4
MEM_HEAD_DIM = 256
IN_WIDTH = 12320
N_DEV = 8
LANES = 128
SSD_SEG = SSD_WIDTH + SSD_CONV_CH
DT_PAD = LANES
SEG_BOUNDS = (0, 5120, 5152, 8224, 9248, 12320)

ADAM_LR = 0.001
ADAM_B1 = 0.9
ADAM_B2 = 0.999
ADAM_EPS = 1e-08
ADAM_WD = 0.01
ADAM_STEP = 10

VMEM_LIMIT = 56 * 1024 * 1024

NN = (((1,), (0,)), ((), ()))
NT = (((1,), (1,)), ((), ()))
TN = (((0,), (0,)), ((), ()))


def _dot(a, b, dims):
    return lax.dot_general(a.astype(MXU_DTYPE), b.astype(MXU_DTYPE), dims, preferred_element_type=F32)


def _sigmoid(x):
    return 0.5 * jnp.tanh(0.5 * x) + 0.5


def _log1p(e):
    u = 1.0 + e
    return jnp.where(u == 1.0, e, jnp.log(u) * (e / jnp.where(u == 1.0, 1.0, u - 1.0)))


def _softplus(x):
    return jnp.maximum(x, 0.0) + _log1p(jnp.exp(-jnp.abs(x)))


def _params(semantics):
    return pltpu.CompilerParams(dimension_semantics=semantics, vmem_limit_bytes=VMEM_LIMIT)


def _shift_down(cur, halo8, k):
    rolled = pltpu.roll(cur, k, 0)
    row8 = lax.broadcasted_iota(jnp.int32, halo8.shape, 0)
    top = jnp.where(row8 >= k, rolled[0:8], pltpu.roll(halo8, k, 0))
    return jnp.concatenate([top, rolled[8:]], axis=0)


def _shift_up(cur, next8, k):
    rows = cur.shape[0]
    rolled = pltpu.roll(cur, rows - k, 0)
    row8 = lax.broadcasted_iota(jnp.int32, next8.shape, 0)
    bot = jnp.where(row8 < 8 - k, rolled[rows - 8:rows], pltpu.roll(next8, 8 - k, 0))
    return jnp.concatenate([rolled[:rows - 8], bot], axis=0)


def _causal_conv(raw, halo8, w, b):
    acc = raw * w[3:4, :] + b
    for k in range(1, CONV_WIDTH):
        acc = acc + _shift_down(raw, halo8, k) * w[3 - k:4 - k, :]
    return acc


def _conv_backward(dco, next8, raw, w):
    d_raw = dco * w[3:4, :]
    gw = [None] * CONV_WIDTH
    gw[3] = jnp.sum(dco * raw, axis=0, keepdims=True)
    for j in range(1, CONV_WIDTH):
        up = _shift_up(dco, next8, j)
        d_raw = d_raw + up * w[3 - j:4 - j, :]
        gw[3 - j] = jnp.sum(up * raw, axis=0, keepdims=True)
    gb = jnp.sum(dco, axis=0, keepdims=True)
    return d_raw, gw, gb


def _cumsum_rows(v):
    rows = v.shape[0]
    row = lax.broadcasted_iota(jnp.int32, v.shape, 0)
    s = 1
    while s < rows:
        v = v + jnp.where(row >= s, pltpu.roll(v, s, 0), 0.0)
        s *= 2
    return v


def _rev_cumsum_rows(v):
    rows = v.shape[0]
    row = lax.broadcasted_iota(jnp.int32, v.shape, 0)
    s = 1
    while s < rows:
        v = v + jnp.where(row < rows - s, pltpu.roll(v, rows - s, 0), 0.0)
        s *= 2
    return v


def _matmul(a, b, mode, name, out_dtype=F32, tm=1024, tn=1024, tk=512, acc_in=None):
    if mode == "nn":
        (m, kk), n = a.shape, b.shape[1]
    elif mode == "nt":
        (m, kk), n = a.shape, b.shape[0]
    else:
        (kk, m), n = a.shape, b.shape[1]
    tm, tn, tk = min(tm, m), min(tn, n), min(tk, kk)
    assert m % tm == 0 and n % tn == 0 and kk % tk == 0, (name, a.shape, b.shape)
    nk = kk // tk
    dims = {"nn": NN, "nt": NT, "tn": TN}[mode]
    a_spec = pl.BlockSpec((tk, tm), lambda i, j, k: (k, i)) if mode == "tn" else pl.BlockSpec((tm, tk), lambda i, j, k: (i, k))
    b_spec = pl.BlockSpec((tn, tk), lambda i, j, k: (j, k)) if mode == "nt" else pl.BlockSpec((tk, tn), lambda i, j, k: (k, j))
    o_spec = pl.BlockSpec((tm, tn), lambda i, j, k: (i, j))
    has_acc = acc_in is not None

    def body_single(*refs):
        if has_acc:
            a_ref, b_ref, c_ref, o_ref = refs
            o_ref[...] = (c_ref[...].astype(F32) + _dot(a_ref[...], b_ref[...], dims)).astype(o_ref.dtype)
        else:
            a_ref, b_ref, o_ref = refs
            o_ref[...] = _dot(a_ref[...], b_ref[...], dims).astype(o_ref.dtype)

    def body(*refs):
        if has_acc:
            a_ref, b_ref, c_ref, o_ref, acc_ref = refs
        else:
            a_ref, b_ref, o_ref, acc_ref = refs
        k = pl.program_id(2)

        @pl.when(k == 0)
        def _():
            acc_ref[...] = c_ref[...].astype(F32) if has_acc else jnp.zeros_like(acc_ref)

        acc_ref[...] += _dot(a_ref[...], b_ref[...], dims)

        @pl.when(k == nk - 1)
        def _():
            o_ref[...] = acc_ref[...].astype(o_ref.dtype)

    args = (a, b) + ((acc_in,) if has_acc else ())
    in_specs = [a_spec, b_spec] + ([o_spec] if has_acc else [])
    return pl.pallas_call(
        body_single if nk == 1 else body, name=name, grid=(m // tm, n // tn, nk), in_specs=in_specs, out_specs=o_spec,
        out_shape=jax.ShapeDtypeStruct((m, n), out_dtype),
        scratch_shapes=[] if nk == 1 else [pltpu.VMEM((tm, tn), F32)],
        compiler_params=_params(("parallel", "parallel", "arbitrary")),
    )(*args)


def _rms_fwd(x, g, name, rows=512):
    t, d = x.shape
    rows = min(rows, t)

    def body(x_ref, g_ref, h_ref):
        xv = x_ref[...]
        r = lax.rsqrt(jnp.mean(xv * xv, axis=-1, keepdims=True) + EPS)
        h_ref[...] = ((xv * r) * g_ref[...]).astype(h_ref.dtype)

    return pl.pallas_call(
        body, name=name, grid=(t // rows,),
        in_specs=[pl.BlockSpec((rows, d), lambda i: (i, 0)), pl.BlockSpec((1, d), lambda i: (0, 0))],
        out_specs=pl.BlockSpec((rows, d), lambda i: (i, 0)),
        out_shape=jax.ShapeDtypeStruct((t, d), MXU_DTYPE),
        compiler_params=_params(("parallel",)),
    )(x, g)


def _rms_bwd(x, dh, dres, g, name, rows=512):
    t, d = x.shape
    rows = min(rows, t)
    has_res = dres is not None

    def body(*refs):
        if has_res:
            x_ref, dh_ref, dr_ref, g_ref, dx_ref, gg_ref = refs
        else:
            x_ref, dh_ref, g_ref, dx_ref, gg_ref = refs

        @pl.when(pl.program_id(0) == 0)
        def _():
            gg_ref[...] = jnp.zeros_like(gg_ref)

        xv = x_ref[...]
        dhv = dh_ref[...]
        r = lax.rsqrt(jnp.mean(xv * xv, axis=-1, keepdims=True) + EPS)
        n = xv * r
        dn = dhv * g_ref[...]
        dx = r * (dn - n * jnp.mean(dn * n, axis=-1, keepdims=True))
        if has_res:
            dx = dx + dr_ref[...]
        dx_ref[...] = dx
        gg_ref[...] += jnp.sum(dhv * n, axis=0, keepdims=True)

    row_spec = pl.BlockSpec((rows, d), lambda i: (i, 0))
    vec_spec = pl.BlockSpec((1, d), lambda i: (0, 0))
    args = (x, dh) + ((dres,) if has_res else ()) + (g,)
    return pl.pallas_call(
        body, name=name, grid=(t // rows,),
        in_specs=[row_spec, row_spec] + ([row_spec] if has_res else []) + [vec_spec],
        out_specs=[row_spec, vec_spec],
        out_shape=[jax.ShapeDtypeStruct((t, d), F32), jax.ShapeDtypeStruct((1, d), F32)],
        compiler_params=_params(("arbitrary",)),
    )(*args)


def _pair_select(lo, m, h0):
    return jnp.where(lo, m[:, h0:h0 + 1], m[:, h0 + 1:h0 + 2])


def _halves(lo, v):
    return (jnp.sum(jnp.where(lo, v, 0.0), axis=1, keepdims=True),
            jnp.sum(jnp.where(lo, 0.0, v), axis=1, keepdims=True))


def _ssd_common(dt_raw, dtb, alog):
    dt = _softplus(dt_raw + dtb)
    aneg = -jnp.exp(alog)
    a_cs = _cumsum_rows(dt * aneg)
    return dt, aneg, a_cs, a_cs.T


def _ssd_specs(nc, rev):
    cidx = (lambda c: nc - 1 - c) if rev else (lambda c: c)
    L = SSD_CHUNK
    return dict(
        z=pl.BlockSpec((L, SSD_WIDTH), lambda c: (cidx(c), 0)),
        xr=pl.BlockSpec((L, SSD_WIDTH), lambda c: (cidx(c), 1)),
        br=pl.BlockSpec((L, 512), lambda c: (cidx(c), 8)),
        cr=pl.BlockSpec((L, 512), lambda c: (cidx(c), 9)),
        dt=pl.BlockSpec((L, DT_PAD), lambda c: (cidx(c), 0)),
        cwx=pl.BlockSpec((CONV_WIDTH, SSD_WIDTH), lambda c: (0, 0)),
        cwb=pl.BlockSpec((CONV_WIDTH, 512), lambda c: (0, 4)),
        cwc=pl.BlockSpec((CONV_WIDTH, 512), lambda c: (0, 5)),
        cbx=pl.BlockSpec((1, SSD_WIDTH), lambda c: (0, 0)),
        cbb=pl.BlockSpec((1, 512), lambda c: (0, 4)),
        cbc=pl.BlockSpec((1, 512), lambda c: (0, 5)),
        vec128=pl.BlockSpec((1, LANES), lambda c: (0, 0)),
        vecw=pl.BlockSpec((1, SSD_WIDTH), lambda c: (0, 0)),
        wide=pl.BlockSpec((L, SSD_WIDTH), lambda c: (cidx(c), 0)),
        states=pl.BlockSpec((1, SSD_PAIRS, 128, SSD_STATE), lambda c: (cidx(c), 0, 0, 0)),
    )


def _ssd_fwd(proj_ssd, dt_p, conv_w, conv_b, dtb, alog, d_row, ng_row):
    t = proj_ssd.shape[0]
    nc = t // SSD_CHUNK
    L = SSD_CHUNK
    sp = _ssd_specs(nc, False)

    def body(z_ref, xr_ref, br_ref, cr_ref, dt_ref, cwx_ref, cwb_ref, cwc_ref, cbx_ref, cbb_ref, cbc_ref,
             dtb_ref, alog_ref, d_ref, ng_ref, yssd_ref, y_ref, st_ref,
             hx_ref, hb_ref, hc_ref, state_ref, yacc_ref):
        @pl.when(pl.program_id(0) == 0)
        def _():
            hx_ref[...] = jnp.zeros_like(hx_ref)
            hb_ref[...] = jnp.zeros_like(hb_ref)
            hc_ref[...] = jnp.zeros_like(hc_ref)
            state_ref[...] = jnp.zeros_like(state_ref)

        xr, br, cr = xr_ref[...], br_ref[...], cr_ref[...]
        px = _causal_conv(xr, hx_ref[...], cwx_ref[...], cbx_ref[...])
        pb = _causal_conv(br, hb_ref[...], cwb_ref[...], cbb_ref[...])
        pc = _causal_conv(cr, hc_ref[...], cwc_ref[...], cbc_ref[...])
        hx_ref[...] = xr[L - 8:L, :]
        hb_ref[...] = br[L - 8:L, :]
        hc_ref[...] = cr[L - 8:L, :]
        xs = px * _sigmoid(px)
        bm = pb * _sigmoid(pb)
        cm = pc * _sigmoid(pc)

        dt, _, a_cs, a_t = _ssd_common(dt_ref[...], dtb_ref[...], alog_ref[...])
        exp_a = jnp.exp(a_cs)
        a_last = a_cs[L - 1:L, :]
        dte = jnp.exp(a_last - a_cs)
        dec = jnp.exp(a_last)

        lane = lax.broadcasted_iota(jnp.int32, (L, LANES), 1)
        sub = lax.broadcasted_iota(jnp.int32, (L, LANES), 0)
        lo = lane < SSD_HEAD_DIM
        causal = sub >= lane
        top = sub < SSD_HEAD_DIM

        for g in range(SSD_GROUPS):
            b_g = bm[:, g * SSD_STATE:(g + 1) * SSD_STATE]
            c_g = cm[:, g * SSD_STATE:(g + 1) * SSD_STATE]
            cb = _dot(c_g, b_g, NT)
            for jj in range(PAIRS_PER_GROUP):
                j = g * PAIRS_PER_GROUP + jj
                h0 = 2 * j
                cols = slice(j * LANES, (j + 1) * LANES)
                xs_p = xs[:, cols]
                xdt = xs_p * _pair_select(lo, dt, h0)
                g0 = jnp.where(causal, jnp.exp(a_cs[:, h0:h0 + 1] - a_t[h0:h0 + 1, :]), 0.0) * cb
                g1 = jnp.where(causal, jnp.exp(a_cs[:, h0 + 1:h0 + 2] - a_t[h0 + 1:h0 + 2, :]), 0.0) * cb
                lhs = jnp.concatenate([g0, g1], axis=1)
                rhs = jnp.concatenate([jnp.where(lo, xdt, 0.0), jnp.where(lo, 0.0, xdt)], axis=0)
                y_diag = _dot(lhs, rhs, NN)
                h_p = state_ref[j]
                st_ref[0, j] = h_p
                y_off = _dot(c_g, h_p, NT) * _pair_select(lo, exp_a, h0)
                s_new = _dot(xdt * _pair_select(lo, dte, h0), b_g, TN)
                dec_rows = jnp.where(top, dec[:, h0:h0 + 1], dec[:, h0 + 1:h0 + 2])
                state_ref[j] = h_p * dec_rows + s_new
                yacc_ref[:, cols] = (y_diag + y_off) + xs_p * d_ref[:, cols]

        y = yacc_ref[...]
        y_ref[...] = y
        zz = z_ref[...]
        y2 = y * (zz * _sigmoid(zz))
        gw = SSD_WIDTH // SSD_GROUPS
        for g in range(SSD_GROUPS):
            seg = y2[:, g * gw:(g + 1) * gw]
            r = lax.rsqrt(jnp.mean(seg * seg, axis=-1, keepdims=True) + EPS)
            yssd_ref[:, g * gw:(g + 1) * gw] = ((seg * r) * ng_ref[:, g * gw:(g + 1) * gw]).astype(yssd_ref.dtype)

    return pl.pallas_call(
        body, name="ssd_fwd", grid=(nc,),
        in_specs=[sp["z"], sp["xr"], sp["br"], sp["cr"], sp["dt"], sp["cwx"], sp["cwb"], sp["cwc"],
                  sp["cbx"], sp["cbb"], sp["cbc"], sp["vec128"], sp["vec128"], sp["vecw"], sp["vecw"]],
        out_specs=[sp["wide"], sp["wide"], sp["states"]],
        out_shape=[jax.ShapeDtypeStruct((t, SSD_WIDTH), MXU_DTYPE), jax.ShapeDtypeStruct((t, SSD_WIDTH), F32),
                   jax.ShapeDtypeStruct((nc, SSD_PAIRS, 128, SSD_STATE), F32)],
        scratch_shapes=[pltpu.VMEM((8, SSD_WIDTH), F32), pltpu.VMEM((8, 512), F32), pltpu.VMEM((8, 512), F32),
                        pltpu.VMEM((SSD_PAIRS, 128, SSD_STATE), F32), pltpu.VMEM((L, SSD_WIDTH), F32)],
        compiler_params=_params(("arbitrary",)),
    )(proj_ssd, proj_ssd, proj_ssd, proj_ssd, dt_p, conv_w, conv_w, conv_w, conv_b, conv_b, conv_b,
      dtb, alog, d_row, ng_row)


def _ssd_bwd(proj_ssd, dt_p, y, states, dyssd, conv_w, conv_b, dtb, alog, d_row, ng_row):
    t = proj_ssd.shape[0]
    nc = t // SSD_CHUNK
    L = SSD_CHUNK
    sp = _ssd_specs(nc, True)
    groups8 = L // 8

    def halo_spec(width, col):
        return pl.BlockSpec((8, width), lambda c: (jnp.maximum((nc - 1 - c) * groups8 - 1, 0), col))

    def body(z_ref, xr_ref, br_ref, cr_ref, hx_ref, hb_ref, hc_ref, dt_ref, y_ref, st_ref, dy_ref,
             cwx_ref, cwb_ref, cwc_ref, cbx_ref, cbb_ref, cbc_ref, dtb_ref, alog_ref, d_ref, ng_ref,
             dssd_ref, ddt_ref, gcw_ref, gcb_ref, gdtb_ref, galog_ref, gd_ref, gng_ref,
             gn_ref, nx_ref, nb_ref, ncc_ref, dxs_ref):
        step = pl.program_id(0)

        @pl.when(step == 0)
        def _():
            gn_ref[...] = jnp.zeros_like(gn_ref)
            nx_ref[...] = jnp.zeros_like(nx_ref)
            nb_ref[...] = jnp.zeros_like(nb_ref)
            ncc_ref[...] = jnp.zeros_like(ncc_ref)
            for ref in (gcw_ref, gcb_ref, gdtb_ref, galog_ref, gd_ref, gng_ref):
                ref[...] = jnp.zeros_like(ref)

        first_chunk = step == nc - 1
        keep = jnp.where(first_chunk, 0.0, 1.0)
        xr, br, cr = xr_ref[...], br_ref[...], cr_ref[...]
        hx, hb, hc = hx_ref[...] * keep, hb_ref[...] * keep, hc_ref[...] * keep
        cwx, cwb, cwc = cwx_ref[...], cwb_ref[...], cwc_ref[...]
        px = _causal_conv(xr, hx, cwx, cbx_ref[...])
        pb = _causal_conv(br, hb, cwb, cbb_ref[...])
        pc = _causal_conv(cr, hc, cwc, cbc_ref[...])
        sx, sb, sc = _sigmoid(px), _sigmoid(pb), _sigmoid(pc)
        xs, bm, cm = px * sx, pb * sb, pc * sc

        dt_in = dt_ref[...] + dtb_ref[...]
        dt, aneg, a_cs, a_t = _ssd_common(dt_ref[...], dtb_ref[...], alog_ref[...])
        exp_a = jnp.exp(a_cs)
        a_last = a_cs[L - 1:L, :]
        dte = jnp.exp(a_last - a_cs)
        dec = jnp.exp(a_last)

        lane = lax.broadcasted_iota(jnp.int32, (L, LANES), 1)
        sub = lax.broadcasted_iota(jnp.int32, (L, LANES), 0)
        lo = lane < SSD_HEAD_DIM
        causal = sub >= lane
        top = sub < SSD_HEAD_DIM
        last_row = sub == L - 1

        yv = y_ref[...]
        zz = z_ref[...]
        sz = _sigmoid(zz)
        silz = zz * sz
        y2 = yv * silz
        dyv = dy_ref[...]
        gw = SSD_WIDTH // SSD_GROUPS
        d_y2_parts = []
        gng_parts = []
        for g in range(SSD_GROUPS):
            seg = y2[:, g * gw:(g + 1) * gw]
            dseg = dyv[:, g * gw:(g + 1) * gw]
            r = lax.rsqrt(jnp.mean(seg * seg, axis=-1, keepdims=True) + EPS)
            n = seg * r
            dn = dseg * ng_ref[:, g * gw:(g + 1) * gw]
            gng_parts.append(jnp.sum(dseg * n, axis=0, keepdims=True))
            d_y2_parts.append(r * (dn - n * jnp.mean(dn * n, axis=-1, keepdims=True)))
        d_y2 = jnp.concatenate(d_y2_parts, axis=1)
        gng_ref[...] += jnp.concatenate(gng_parts, axis=1)
        d_y = d_y2 * silz
        dssd_ref[:, 0:SSD_WIDTH] = (d_y2 * yv * (sz * (1.0 + zz * (1.0 - sz)))).astype(dssd_ref.dtype)
        gd_ref[...] += jnp.sum(d_y * xs, axis=0, keepdims=True)
        dxs_ref[...] = d_y * d_ref[...]

        d_a = jnp.zeros((L, LANES), F32)
        d_at = jnp.zeros((LANES, L), F32)
        ddt = jnp.zeros((L, LANES), F32)
        d_b_parts, d_c_parts = [], []
        for g in range(SSD_GROUPS):
            b_g = bm[:, g * SSD_STATE:(g + 1) * SSD_STATE]
            c_g = cm[:, g * SSD_STATE:(g + 1) * SSD_STATE]
            cb = _dot(c_g, b_g, NT)
            d_cb = jnp.zeros((L, L), F32)
            d_bg = jnp.zeros((L, SSD_STATE), F32)
            d_cg = jnp.zeros((L, SSD_STATE), F32)
            for jj in range(PAIRS_PER_GROUP):
                j = g * PAIRS_PER_GROUP + jj
                h0 = 2 * j
                cols = slice(j * LANES, (j + 1) * LANES)
                dy_p = d_y[:, cols]
                xs_p = xs[:, cols]
                dt_pp = _pair_select(lo, dt, h0)
                expa_p = _pair_select(lo, exp_a, h0)
                dte_p = _pair_select(lo, dte, h0)
                xdt = xs_p * dt_pp
                l0 = jnp.where(causal, jnp.exp(a_cs[:, h0:h0 + 1] - a_t[h0:h0 + 1, :]), 0.0)
                l1 = jnp.where(causal, jnp.exp(a_cs[:, h0 + 1:h0 + 2] - a_t[h0 + 1:h0 + 2, :]), 0.0)
                g0, g1 = l0 * cb, l1 * cb
                h_p = st_ref[0, j]
                gn_p = gn_ref[j]
                dys = dy_p * expa_p
                d_cg = d_cg + _dot(dys, h_p, NN)
                d_h = _dot(dys, c_g, TN)
                t1 = dy_p * _dot(c_g, h_p, NT) * expa_p
                dw = _dot(b_g, gn_p, NT)
                d_bg = d_bg + _dot(xdt * dte_p, gn_p, NN)
                d_xdt = dw * dte_p
                t2 = d_xdt * xdt
                dyl, dyh = jnp.where(lo, dy_p, 0.0), jnp.where(lo, 0.0, dy_p)
                d_xdt = d_xdt + _dot(jnp.concatenate([g0, g1], axis=0), jnp.concatenate([dyl, dyh], axis=0), TN)
                dm0 = _dot(dyl, xdt, NT)
                dm1 = _dot(dyh, xdt, NT)
                d_cb = d_cb + (l0 * dm0 + l1 * dm1)
                e0, e1 = dm0 * g0, dm1 * g1
                a0, a1 = _halves(lo, t1 - t2)
                a0 = a0 + jnp.sum(e0, axis=1, keepdims=True)
                a1 = a1 + jnp.sum(e1, axis=1, keepdims=True)
                s0, s1 = _halves(lo, t2)
                gh = jnp.sum(gn_p * h_p, axis=1, keepdims=True)
                dd0 = jnp.sum(jnp.where(top[:, 0:1], gh, 0.0), axis=0, keepdims=True)
                dd1 = jnp.sum(jnp.where(top[:, 0:1], 0.0, gh), axis=0, keepdims=True)
                end0 = jnp.sum(s0, axis=0, keepdims=True) + dd0 * dec[:, h0:h0 + 1]
                end1 = jnp.sum(s1, axis=0, keepdims=True) + dd1 * dec[:, h0 + 1:h0 + 2]
                d_a = d_a + jnp.where(lane == h0, a0 + jnp.where(last_row, end0, 0.0), 0.0)
                d_a = d_a + jnp.where(lane == h0 + 1, a1 + jnp.where(last_row, end1, 0.0), 0.0)
                d_at = d_at - jnp.where(sub == h0, jnp.sum(e0, axis=0, keepdims=True), 0.0)
                d_at = d_at - jnp.where(sub == h0 + 1, jnp.sum(e1, axis=0, keepdims=True), 0.0)
                dec_rows = jnp.where(top, dec[:, h0:h0 + 1], dec[:, h0 + 1:h0 + 2])
                gn_ref[j] = d_h + dec_rows * gn_p
                q0, q1 = _halves(lo, d_xdt * xs_p)
                ddt = ddt + jnp.where(lane == h0, q0, 0.0) + jnp.where(lane == h0 + 1, q1, 0.0)
                dxs_ref[:, cols] += d_xdt * dt_pp
            d_cg = d_cg + _dot(d_cb, b_g, NN)
            d_bg = d_bg + _dot(d_cb, c_g, TN)
            d_b_parts.append(d_bg)
            d_c_parts.append(d_cg)

        rc = _rev_cumsum_rows(d_a + d_at.T)
        d_dt = rc * aneg + ddt
        galog_ref[...] += jnp.sum(rc * dt, axis=0, keepdims=True) * aneg
        d_dtraw = d_dt * _sigmoid(dt_in)
        gdtb_ref[...] += jnp.sum(d_dtraw, axis=0, keepdims=True)
        ddt_ref[...] = d_dtraw.astype(ddt_ref.dtype)

        def dsilu(p, s):
            return s * (1.0 + p * (1.0 - s))

        dcx = dxs_ref[...] * dsilu(px, sx)
        dcb = jnp.concatenate(d_b_parts, axis=1) * dsilu(pb, sb)
        dcc = jnp.concatenate(d_c_parts, axis=1) * dsilu(pc, sc)
        drx, gwx, gbx = _conv_backward(dcx, nx_ref[...], xr, cwx)
        drb, gwb, gbb = _conv_backward(dcb, nb_ref[...], br, cwb)
        drc, gwc, gbc = _conv_backward(dcc, ncc_ref[...], cr, cwc)
        nx_ref[...] = dcx[0:8, :]
        nb_ref[...] = dcb[0:8, :]
        ncc_ref[...] = dcc[0:8, :]
        dssd_ref[:, SSD_WIDTH:2 * SSD_WIDTH] = drx.astype(dssd_ref.dtype)
        dssd_ref[:, 2 * SSD_WIDTH:2 * SSD_WIDTH + 512] = drb.astype(dssd_ref.dtype)
        dssd_ref[:, 2 * SSD_WIDTH + 512:SSD_SEG] = drc.astype(dssd_ref.dtype)
        for k in range(CONV_WIDTH):
            gcw_ref[k:k + 1, :] += jnp.concatenate([gwx[k], gwb[k], gwc[k]], axis=1)
        gcb_ref[...] += jnp.concatenate([gbx, gbb, gbc], axis=1)

    const = lambda shape: pl.BlockSpec(shape, lambda c: (0,) * len(shape))
    return pl.pallas_call(
        body, name="ssd_bwd", grid=(nc,),
        in_specs=[sp["z"], sp["xr"], sp["br"], sp["cr"], halo_spec(SSD_WIDTH, 1), halo_spec(512, 8), halo_spec(512, 9),
                  sp["dt"], sp["wide"], sp["states"], sp["wide"],
                  sp["cwx"], sp["cwb"], sp["cwc"], sp["cbx"], sp["cbb"], sp["cbc"],
                  sp["vec128"], sp["vec128"], sp["vecw"], sp["vecw"]],
        out_specs=[pl.BlockSpec((L, SSD_SEG), lambda c: (nc - 1 - c, 0)), sp["dt"],
                   const((CONV_WIDTH, SSD_CONV_CH)), const((1, SSD_CONV_CH)), const((1, LANES)), const((1, LANES)),
                   const((1, SSD_WIDTH)), const((1, SSD_WIDTH))],
        out_shape=[jax.ShapeDtypeStruct((t, SSD_SEG), MXU_DTYPE), jax.ShapeDtypeStruct((t, DT_PAD), MXU_DTYPE),
                   jax.ShapeDtypeStruct((CONV_WIDTH, SSD_CONV_CH), F32), jax.ShapeDtypeStruct((1, SSD_CONV_CH), F32),
                   jax.ShapeDtypeStruct((1, LANES), F32), jax.ShapeDtypeStruct((1, LANES), F32),
                   jax.ShapeDtypeStruct((1, SSD_WIDTH), F32), jax.ShapeDtypeStruct((1, SSD_WIDTH), F32)],
        scratch_shapes=[pltpu.VMEM((SSD_PAIRS, 128, SSD_STATE), F32), pltpu.VMEM((8, SSD_WIDTH), F32),
                        pltpu.VMEM((8, 512), F32), pltpu.VMEM((8, 512), F32), pltpu.VMEM((L, SSD_WIDTH), F32)],
        compiler_params=_params(("arbitrary",)),
    )(proj_ssd, proj_ssd, proj_ssd, proj_ssd, proj_ssd, proj_ssd, proj_ssd, dt_p, y, states, dyssd,
      conv_w, conv_w, conv_w, conv_b, conv_b, conv_b, dtb, alog, d_row, ng_row)


def _lru_gates(xl, wa_ref, wx_ref, ba, bx, lam):
    pre_a, pre_x = [], []
    for g in range(LRU_NGROUPS):
        xg = xl[:, g * LRU_GROUP:(g + 1) * LRU_GROUP]
        pre_a.append(_dot(xg, wa_ref[g], NN))
        pre_x.append(_dot(xg, wx_ref[g], NN))
    r = _sigmoid(jnp.concatenate(pre_a, axis=1) + ba)
    i = _sigmoid(jnp.concatenate(pre_x, axis=1) + bx)
    log_a = (-LRU_C * r) * _softplus(-lam)
    a = jnp.exp(log_a)
    mult = jnp.sqrt(-jnp.tanh(log_a) * (a * a + 1.0))
    return r, i, log_a, a, mult


def _scan_rows(p, u, carry, reverse):
    rows, w = p.shape
    groups = rows // 8
    p3, u3 = p.reshape(groups, 8, w), u.reshape(groups, 8, w)
    row = lax.broadcasted_iota(jnp.int32, (groups, 8, w), 1)
    for s in (1, 2, 4):
        ok = row < 8 - s if reverse else row >= s
        shift = 8 - s if reverse else s
        u3 = p3 * jnp.where(ok, pltpu.roll(u3, shift, 1), 0.0) + u3
        p3 = p3 * jnp.where(ok, pltpu.roll(p3, shift, 1), 1.0)
    out = [None] * groups
    for k in (range(groups - 1, -1, -1) if reverse else range(groups)):
        out[k] = p3[k] * carry + u3[k]
        carry = out[k][0:1, :] if reverse else out[k][7:8, :]
    return jnp.concatenate(out, axis=0), carry


def _lru_fwd(proj_lru, conv_w, conv_b, wa, wx, ba, bx, lam):
    t = proj_lru.shape[0]
    rows = min(LRU_ROWS, t)
    nb = t // rows
    W = LRU_WIDTH

    def body(lg_ref, lx_ref, cw_ref, cb_ref, wa_ref, wx_ref, ba_ref, bx_ref, lam_ref, ylru_ref, h_ref,
             halo_ref, carry_ref):
        @pl.when(pl.program_id(0) == 0)
        def _():
            halo_ref[...] = jnp.zeros_like(halo_ref)
            carry_ref[...] = jnp.zeros_like(carry_ref)

        lx = lx_ref[...]
        xl = _causal_conv(lx, halo_ref[...], cw_ref[...], cb_ref[...])
        halo_ref[...] = lx[rows - 8:rows, :]
        _, i, _, a, mult = _lru_gates(xl, wa_ref, wx_ref, ba_ref[...], bx_ref[...], lam_ref[...])
        u = mult * (i * xl)
        h, carry_ref[...] = _scan_rows(a, u, carry_ref[...], False)
        h_ref[...] = h
        lg = lg_ref[...]
        ylru_ref[...] = (h * (lg * _sigmoid(lg))).astype(ylru_ref.dtype)

    const = lambda shape: pl.BlockSpec(shape, lambda b: (0,) * len(shape))
    return pl.pallas_call(
        body, name="lru_fwd", grid=(nb,),
        in_specs=[pl.BlockSpec((rows, W), lambda b: (b, 0)), pl.BlockSpec((rows, W), lambda b: (b, 1)),
                  const((CONV_WIDTH, W)), const((1, W)), const((LRU_NGROUPS, LRU_GROUP, LRU_GROUP)),
                  const((LRU_NGROUPS, LRU_GROUP, LRU_GROUP)), const((1, W)), const((1, W)), const((1, W))],
        out_specs=[pl.BlockSpec((rows, W), lambda b: (b, 0)), pl.BlockSpec((rows, W), lambda b: (b, 0))],
        out_shape=[jax.ShapeDtypeStruct((t, W), MXU_DTYPE), jax.ShapeDtypeStruct((t, W), F32)],
        scratch_shapes=[pltpu.VMEM((8, W), F32), pltpu.VMEM((1, W), F32)],
        compiler_params=_params(("arbitrary",)),
    )(proj_lru, proj_lru, conv_w, conv_b, wa, wx, ba, bx, lam)


def _lru_bwd(proj_lru, h, dylru, conv_w, conv_b, wa, wx, ba, bx, lam):
    t = proj_lru.shape[0]
    rows = min(LRU_ROWS, t)
    nb = t // rows
    W = LRU_WIDTH
    groups8 = rows // 8

    def rev(b):
        return nb - 1 - b

    def halo_spec(col):
        return pl.BlockSpec((8, W), lambda b: (jnp.maximum(rev(b) * groups8 - 1, 0), col))

    def body(lg_ref, lx_ref, hlx_ref, h_ref, hh_ref, dy_ref, cw_ref, cb_ref, wa_ref, wx_ref, ba_ref, bx_ref, lam_ref,
             dlru_ref, gcw_ref, gcb_ref, gba_ref, gbx_ref, glam_ref, gwa_ref, gwx_ref,
             gcarry_ref, afirst_ref, nxt_ref):
        step = pl.program_id(0)

        @pl.when(step == 0)
        def _():
            gcarry_ref[...] = jnp.zeros_like(gcarry_ref)
            afirst_ref[...] = jnp.zeros_like(afirst_ref)
            nxt_ref[...] = jnp.zeros_like(nxt_ref)
            for ref in (gcw_ref, gcb_ref, gba_ref, gbx_ref, glam_ref, gwa_ref, gwx_ref):
                ref[...] = jnp.zeros_like(ref)

        keep = jnp.where(step == nb - 1, 0.0, 1.0)
        lx = lx_ref[...]
        hlx = hlx_ref[...] * keep
        cw = cw_ref[...]
        xl = _causal_conv(lx, hlx, cw, cb_ref[...])
        lam = lam_ref[...]
        r, i, log_a, a, mult = _lru_gates(xl, wa_ref, wx_ref, ba_ref[...], bx_ref[...], lam)
        hv = h_ref[...]
        h_prev = _shift_down(hv, hh_ref[...] * keep, 1)
        lg = lg_ref[...]
        sg = _sigmoid(lg)
        dyv = dy_ref[...]
        d_h = dyv * (lg * sg)
        dlru_ref[:, 0:W] = (dyv * hv * (sg * (1.0 + lg * (1.0 - sg)))).astype(dlru_ref.dtype)

        row = lax.broadcasted_iota(jnp.int32, (rows, W), 0)
        p = jnp.where(row < rows - 1, pltpu.roll(a, rows - 1, 0), afirst_ref[...])
        gsc, gcarry_ref[...] = _scan_rows(p, d_h, gcarry_ref[...], True)
        afirst_ref[...] = a[0:1, :]

        d_a = gsc * h_prev
        v = i * xl
        d_mult = gsc * v
        d_v = gsc * mult
        d_i = d_v * xl
        d_xl = d_v * i
        d_la = d_a * a - d_mult * (a * a) / mult
        sp_neg = _softplus(-lam)
        d_r = d_la * (-LRU_C * sp_neg)
        glam_ref[...] += jnp.sum(d_la * r, axis=0, keepdims=True) * (LRU_C * _sigmoid(-lam))
        d_pa = d_r * r * (1.0 - r)
        d_px = d_i * i * (1.0 - i)
        gba_ref[...] += jnp.sum(d_pa, axis=0, keepdims=True)
        gbx_ref[...] += jnp.sum(d_px, axis=0, keepdims=True)
        parts = []
        for g in range(LRU_NGROUPS):
            cols = slice(g * LRU_GROUP, (g + 1) * LRU_GROUP)
            xg, dpa_g, dpx_g = xl[:, cols], d_pa[:, cols], d_px[:, cols]
            parts.append(_dot(dpa_g, wa_ref[g], NT) + _dot(dpx_g, wx_ref[g], NT))
            gwa_ref[g] += _dot(xg, dpa_g, TN)
            gwx_ref[g] += _dot(xg, dpx_g, TN)
        d_xl = d_xl + jnp.concatenate(parts, axis=1)
        d_lx, gw, gb = _conv_backward(d_xl, nxt_ref[...], lx, cw)
        nxt_ref[...] = d_xl[0:8, :]
        dlru_ref[:, W:2 * W] = d_lx.astype(dlru_ref.dtype)
        for k in range(CONV_WIDTH):
            gcw_ref[k:k + 1, :] += gw[k]
        gcb_ref[...] += gb

    const = lambda shape: pl.BlockSpec(shape, lambda b: (0,) * len(shape))
    wspec = const((LRU_NGROUPS, LRU_GROUP, LRU_GROUP))
    blk = lambda col: pl.BlockSpec((rows, W), lambda b: (rev(b), col))
    return pl.pallas_call(
        body, name="lru_bwd", grid=(nb,),
        in_specs=[blk(0), blk(1), halo_spec(1), blk(0), halo_spec(0), blk(0),
                  const((CONV_WIDTH, W)), const((1, W)), wspec, wspec, const((1, W)), const((1, W)), const((1, W))],
        out_specs=[pl.BlockSpec((rows, 2 * W), lambda b: (rev(b), 0)), const((CONV_WIDTH, W)), const((1, W)),
                   const((1, W)), const((1, W)), const((1, W)), wspec, wspec],
        out_shape=[jax.ShapeDtypeStruct((t, 2 * W), MXU_DTYPE), jax.ShapeDtypeStruct((CONV_WIDTH, W), F32),
                   jax.ShapeDtypeStruct((1, W), F32), jax.ShapeDtypeStruct((1, W), F32), jax.ShapeDtypeStruct((1, W), F32),
                   jax.ShapeDtypeStruct((1, W), F32), jax.ShapeDtypeStruct((LRU_NGROUPS, LRU_GROUP, LRU_GROUP), F32),
                   jax.ShapeDtypeStruct((LRU_NGROUPS, LRU_GROUP, LRU_GROUP), F32)],
        scratch_shapes=[pltpu.VMEM((1, W), F32), pltpu.VMEM((1, W), F32), pltpu.VMEM((8, W), F32)],
        compiler_params=_params(("arbitrary",)),
    )(proj_lru, proj_lru, proj_lru, h, h, dylru, conv_w, conv_b, wa, wx, ba, bx, lam)


def _mem_scores(q_h, k_h):
    s = _dot(q_h, k_h, NT) * (MEM_HEAD_DIM ** -0.5)
    s = s - jnp.max(s, axis=-1, keepdims=True)
    e = jnp.exp(s)
    return e / jnp.sum(e, axis=-1, keepdims=True)


def _mem_fwd(q, kv, rows=512):
    t = q.shape[0]
    rows = min(rows, t)
    m = kv.shape[0]

    def body(q_ref, kv_ref, y_ref):
        for hd in range(MEM_HEADS):
            cols = slice(hd * MEM_HEAD_DIM, (hd + 1) * MEM_HEAD_DIM)
            vcols = slice(D_MODEL + hd * MEM_HEAD_DIM, D_MODEL + (hd + 1) * MEM_HEAD_DIM)
            p = _mem_scores(q_ref[:, cols], kv_ref[:, cols])
            y_ref[:, cols] = _dot(p, kv_ref[:, vcols], NN).astype(y_ref.dtype)

    return pl.pallas_call(
        body, name="mem_fwd", grid=(t // rows,),
        in_specs=[pl.BlockSpec((rows, D_MODEL), lambda i: (i, 0)), pl.BlockSpec((m, 2 * D_MODEL), lambda i: (0, 0))],
        out_specs=pl.BlockSpec((rows, D_MODEL), lambda i: (i, 0)),
        out_shape=jax.ShapeDtypeStruct((t, D_MODEL), MXU_DTYPE),
        compiler_params=_params(("parallel",)),
    )(q, kv)


def _mem_bwd(q, kv, dy, rows=512):
    t = q.shape[0]
    rows = min(rows, t)
    m = kv.shape[0]

    def body(q_ref, kv_ref, dy_ref, dq_ref, dkv_ref):
        @pl.when(pl.program_id(0) == 0)
        def _():
            dkv_ref[...] = jnp.zeros_like(dkv_ref)

        for hd in range(MEM_HEADS):
            cols = slice(hd * MEM_HEAD_DIM, (hd + 1) * MEM_HEAD_DIM)
            vcols = slice(D_MODEL + hd * MEM_HEAD_DIM, D_MODEL + (hd + 1) * MEM_HEAD_DIM)
            q_h, k_h, dy_h = q_ref[:, cols], kv_ref[:, cols], dy_ref[:, cols]
            p = _mem_scores(q_h, k_h)
            dp = _dot(dy_h, kv_ref[:, vcols], NT)
            dkv_ref[:, vcols] += _dot(p, dy_h, TN)
            ds = p * (dp - jnp.sum(dp * p, axis=-1, keepdims=True)) * (MEM_HEAD_DIM ** -0.5)
            dq_ref[:, cols] = _dot(ds, k_h, NN).astype(dq_ref.dtype)
            dkv_ref[:, cols] += _dot(ds, q_h, TN)

    return pl.pallas_call(
        body, name="mem_bwd", grid=(t // rows,),
        in_specs=[pl.BlockSpec((rows, D_MODEL), lambda i: (i, 0)), pl.BlockSpec((m, 2 * D_MODEL), lambda i: (0, 0)),
                  pl.BlockSpec((rows, D_MODEL), lambda i: (i, 0))],
        out_specs=[pl.BlockSpec((rows, D_MODEL), lambda i: (i, 0)), pl.BlockSpec((m, 2 * D_MODEL), lambda i: (0, 0))],
        out_shape=[jax.ShapeDtypeStruct((t, D_MODEL), MXU_DTYPE), jax.ShapeDtypeStruct((m, 2 * D_MODEL), F32)],
        compiler_params=_params(("arbitrary",)),
    )(q, kv, dy)


def _merge_fwd(x, yssd, ylru, ymem, gl, w_bs, w_bl, w_bm, w_out, fg, tgt, rows=256):
    t = x.shape[0]
    rows = min(rows, t)
    D = D_MODEL

    def body(x_ref, ys_ref, yl_ref, ym_ref, gl_ref, wbs_ref, wbl_ref, wbm_ref, wo_ref, fg_ref, tgt_ref,
             ps_ref, pl_ref, pm_ref, mg_ref, dx2_ref, loss_ref, gfg_ref):
        @pl.when(pl.program_id(0) == 0)
        def _():
            loss_ref[...] = jnp.zeros_like(loss_ref)
            gfg_ref[...] = jnp.zeros_like(gfg_ref)

        ps = _dot(ys_ref[...], wbs_ref[...], NN)
        pl_ = _dot(yl_ref[...], wbl_ref[...], NN)
        pm = _dot(ym_ref[...], wbm_ref[...], NN)
        ps_ref[...] = ps
        pl_ref[...] = pl_
        pm_ref[...] = pm
        merged = (_sigmoid(gl_ref[:, 0:D]) * ps + _sigmoid(gl_ref[:, D:2 * D]) * pl_) + _sigmoid(gl_ref[:, 2 * D:3 * D]) * pm
        mg_ref[...] = merged.astype(mg_ref.dtype)
        x2 = x_ref[...] + _dot(merged, wo_ref[...], NN)
        r2 = lax.rsqrt(jnp.mean(x2 * x2, axis=-1, keepdims=True) + EPS)
        xn = x2 * r2
        fg = fg_ref[...]
        diff = xn * fg - tgt_ref[...]
        tile_loss = 0.5 * jnp.sum(jnp.mean(diff * diff, axis=-1, keepdims=True), axis=0, keepdims=True)
        loss_ref[...] += jnp.broadcast_to(tile_loss, loss_ref.shape)
        d_out = diff * (1.0 / D)
        gfg_ref[...] += jnp.sum(d_out * xn, axis=0, keepdims=True)
        dxn = d_out * fg
        dx2_ref[...] = r2 * (dxn - xn * jnp.mean(dxn * xn, axis=-1, keepdims=True))

    row = lambda w: pl.BlockSpec((rows, w), lambda i: (i, 0))
    const = lambda shape: pl.BlockSpec(shape, lambda i: (0,) * len(shape))
    return pl.pallas_call(
        body, name="merge_fwd", grid=(t // rows,),
        in_specs=[row(D), row(SSD_WIDTH), row(LRU_WIDTH), row(D), row(3 * D), const((SSD_WIDTH, D)), const((LRU_WIDTH, D)),
                  const((D, D)), const((D, D)), const((1, D)), row(D)],
        out_specs=[row(D), row(D), row(D), row(D), row(D), const((1, LANES)), const((1, D))],
        out_shape=[jax.ShapeDtypeStruct((t, D), F32), jax.ShapeDtypeStruct((t, D), F32), jax.ShapeDtypeStruct((t, D), F32),
                   jax.ShapeDtypeStruct((t, D), MXU_DTYPE), jax.ShapeDtypeStruct((t, D), F32),
                   jax.ShapeDtypeStruct((1, LANES), F32), jax.ShapeDtypeStruct((1, D), F32)],
        compiler_params=_params(("arbitrary",)),
    )(x, yssd, ylru, ymem, gl, w_bs, w_bl, w_bm, w_out, fg, tgt)


def _merge_bwd(dx2, gl, ps, pl_in, pm, w_bs, w_bl, w_bm, w_out, rows=256):
    t = dx2.shape[0]
    rows = min(rows, t)
    D = D_MODEL

    def body(dx2_ref, gl_ref, ps_ref, pl_ref, pm_ref, wbs_ref, wbl_ref, wbm_ref, wo_ref,
             dg_ref, dps_ref, dpl_ref, dpm_ref, dys_ref, dyl_ref, dym_ref):
        dm = _dot(dx2_ref[...], wo_ref[...], NT)
        for idx, (p_ref, dp_ref, w_ref, dy_ref) in enumerate(
                ((ps_ref, dps_ref, wbs_ref, dys_ref), (pl_ref, dpl_ref, wbl_ref, dyl_ref), (pm_ref, dpm_ref, wbm_ref, dym_ref))):
            gate = _sigmoid(gl_ref[:, idx * D:(idx + 1) * D])
            dg_ref[:, idx * D:(idx + 1) * D] = ((dm * p_ref[...]) * gate * (1.0 - gate)).astype(dg_ref.dtype)
            dp = dm * gate
            dp_ref[...] = dp.astype(dp_ref.dtype)
            dy_ref[...] = _dot(dp, w_ref[...], NT)

    row = lambda w: pl.BlockSpec((rows, w), lambda i: (i, 0))
    const = lambda shape: pl.BlockSpec(shape, lambda i: (0,) * len(shape))
    return pl.pallas_call(
        body, name="merge_bwd", grid=(t // rows,),
        in_specs=[row(D), row(3 * D), row(D), row(D), row(D), const((SSD_WIDTH, D)), const((LRU_WIDTH, D)),
                  const((D, D)), const((D, D))],
        out_specs=[row(3 * D), row(D), row(D), row(D), row(SSD_WIDTH), row(LRU_WIDTH), row(D)],
        out_shape=[jax.ShapeDtypeStruct((t, 3 * D), MXU_DTYPE), jax.ShapeDtypeStruct((t, D), MXU_DTYPE),
                   jax.ShapeDtypeStruct((t, D), MXU_DTYPE), jax.ShapeDtypeStruct((t, D), MXU_DTYPE),
                   jax.ShapeDtypeStruct((t, SSD_WIDTH), F32), jax.ShapeDtypeStruct((t, LRU_WIDTH), F32),
                   jax.ShapeDtypeStruct((t, D), F32)],
        compiler_params=_params(("parallel",)),
    )(dx2, gl, ps, pl_in, pm, w_bs, w_bl, w_bm, w_out)


def _mesh_place():
    x, y, c = lax.axis_index("x"), lax.axis_index("y"), lax.axis_index("c")
    return x, y, c, 4 * x + 2 * y + c


def _other_chips(x, y):
    return [(1 - x, y), (x, 1 - y), (1 - x, 1 - y)]


def _all_gather(arrs, name):
    n = len(arrs)

    def body(*refs):
        ins, outs = refs[:n], refs[n:2 * n]
        send_sems, recv_sems, local_sems = refs[2 * n:]
        x, y, c, me = _mesh_place()
        sibling = (x, y, 1 - c)
        chips = _other_chips(x, y)

        def slot(px, py, pc):
            return 4 * px + 2 * py + pc

        def copy(a, k, block, to, src=None):
            return pltpu.make_async_remote_copy(
                src_ref=outs[a].at[block] if src is None else src, dst_ref=outs[a].at[block],
                send_sem=send_sems.at[a, k], recv_sem=recv_sems.at[a, k], device_id=to, device_id_type=pl.DeviceIdType.MESH)

        local = [pltpu.make_async_copy(ins[a], outs[a].at[me], local_sems.at[a]) for a in range(n)]
        for cp in local:
            cp.start()
        sends = []
        for a in range(n):
            sends.append(copy(a, 0, me, sibling, src=ins[a]))
            for j, chip in enumerate(chips):
                sends.append(copy(a, 1 + j, me, (*chip, c), src=ins[a]))
        for cp in sends:
            cp.start()
        for j, chip in enumerate(chips):
            for a in range(n):
                copy(a, 1 + j, slot(*chip, c), sibling).wait_recv()
                passed = copy(a, 4 + j, slot(*chip, c), sibling)
                passed.start()
                sends.append(passed)
        for a in range(n):
            copy(a, 0, slot(x, y, 1 - c), sibling).wait_recv()
        for j, chip in enumerate(chips):
            for a in range(n):
                copy(a, 4 + j, slot(*chip, 1 - c), sibling).wait_recv()
        for cp in sends:
            cp.wait_send()
        for cp in local:
            cp.wait()

    any_spec = pl.BlockSpec(memory_space=pl.ANY)
    return pl.pallas_call(
        body, name=name, in_specs=[any_spec] * n, out_specs=[any_spec] * n,
        out_shape=[jax.ShapeDtypeStruct((N_DEV,) + a.shape, a.dtype) for a in arrs],
        scratch_shapes=[pltpu.SemaphoreType.DMA((n, 7)), pltpu.SemaphoreType.DMA((n, 7)), pltpu.SemaphoreType.DMA((n,))],
    )(*arrs)


N_CHIPS = 4


def _pair_exchange(parts):
    n = len(parts)

    def body(*refs):
        ins, outs = refs[:n], refs[n:2 * n]
        send_sems, recv_sems = refs[2 * n:]
        x, y, c, _ = _mesh_place()
        sibling = (x, y, 1 - c)
        sends = []
        for a in range(n):
            for q in range(N_CHIPS):
                cp = pltpu.make_async_remote_copy(src_ref=ins[a].at[q, 1 - c], dst_ref=outs[a].at[q], send_sem=send_sems.at[a, q],
                                                  recv_sem=recv_sems.at[a, q], device_id=sibling, device_id_type=pl.DeviceIdType.MESH)
                cp.start()
                sends.append(cp)
        for cp in sends:
            cp.wait_recv()
        for cp in sends:
            cp.wait_send()

    any_spec = pl.BlockSpec(memory_space=pl.ANY)
    return pl.pallas_call(
        body, name="grad_pair_exchange", in_specs=[any_spec] * n, out_specs=[any_spec] * n,
        out_shape=[jax.ShapeDtypeStruct((N_CHIPS,) + a.shape[2:], a.dtype) for a in parts],
        scratch_shapes=[pltpu.SemaphoreType.DMA((n, N_CHIPS)), pltpu.SemaphoreType.DMA((n, N_CHIPS))],
    )(*parts)


def _chip_sum(part, recv, core, name):
    _, _, r, c = part.shape
    rows = _row_tile(r, max(8, (2 << 20) // (c * 4) // 8 * 8))

    def body(core_ref, p_ref, r_ref, s_ref, t_ref):
        s = p_ref[...] + r_ref[...]
        s_ref[...] = s
        t_ref[...] = s.astype(t_ref.dtype)

    blk = pl.BlockSpec((None, rows, c), lambda q, i, core_ref: (q, i, 0))
    return pl.pallas_call(
        body, name=name,
        grid_spec=pltpu.PrefetchScalarGridSpec(
            num_scalar_prefetch=1, grid=(N_CHIPS, r // rows),
            in_specs=[pl.BlockSpec((None, None, rows, c), lambda q, i, core_ref: (q, core_ref[0], i, 0)), blk],
            out_specs=[blk, blk]),
        out_shape=[jax.ShapeDtypeStruct((N_CHIPS, r, c), F32), jax.ShapeDtypeStruct((N_CHIPS, r, c), GRAD_WIRE_DTYPE)],
        compiler_params=_params(("parallel", "parallel")),
    )(core, part, recv)


def _chip_exchange(sums):
    n = len(sums)

    def body(*refs):
        ins, outs = refs[:n], refs[n:2 * n]
        send_sems, recv_sems = refs[2 * n:]
        x, y, c, _ = _mesh_place()
        my_chip = 2 * x + y
        sends = []
        for a in range(n):
            for j, (px, py) in enumerate(_other_chips(x, y)):
                cp = pltpu.make_async_remote_copy(src_ref=ins[a].at[2 * px + py], dst_ref=outs[a].at[my_chip], send_sem=send_sems.at[a, j],
                                                  recv_sem=recv_sems.at[a, j], device_id=(px, py, c), device_id_type=pl.DeviceIdType.MESH)
                cp.start()
                sends.append(cp)
        for a in range(n):
            for j, (px, py) in enumerate(_other_chips(x, y)):
                pltpu.make_async_remote_copy(src_ref=ins[a].at[my_chip], dst_ref=outs[a].at[2 * px + py], send_sem=send_sems.at[a, j],
                                             recv_sem=recv_sems.at[a, j], device_id=(px, py, c),
                                             device_id_type=pl.DeviceIdType.MESH).wait_recv()
        for cp in sends:
            cp.wait_send()

    any_spec = pl.BlockSpec(memory_space=pl.ANY)
    return pl.pallas_call(
        body, name="grad_chip_exchange", in_specs=[any_spec] * n, out_specs=[any_spec] * n,
        out_shape=[jax.ShapeDtypeStruct(a.shape, a.dtype) for a in sums],
        scratch_shapes=[pltpu.SemaphoreType.DMA((n, 3)), pltpu.SemaphoreType.DMA((n, 3))],
    )(*sums)


def _row_tile(r, limit):
    if r <= limit:
        return r
    best = 8
    for cand in range(8, limit + 1, 8):
        if r % cand == 0:
            best = cand
    assert r % best == 0, r
    return best


def _adam_update(w, g, m, v):
    nm = ADAM_B1 * m + (1.0 - ADAM_B1) * g
    nv = ADAM_B2 * v + (1.0 - ADAM_B2) * (g * g)
    m_hat = nm / (1.0 - ADAM_B1 ** ADAM_STEP)
    v_hat = nv / (1.0 - ADAM_B2 ** ADAM_STEP)
    return -ADAM_LR * (m_hat / (jnp.sqrt(v_hat) + ADAM_EPS) + ADAM_WD * w), nm, nv


def _sum_adamw(own, recv, chip, w, m, v, name):
    _, r, c = own.shape
    rows = _row_tile(r, max(8, (1 << 20) // (c * 4) // 8 * 8))

    def body(chip_ref, o_ref, r1_ref, r2_ref, r3_ref, w_ref, m_ref, v_ref, g_ref, d_ref, nm_ref, nv_ref):
        g = ((o_ref[...] + r1_ref[...].astype(F32)) + r2_ref[...].astype(F32)) + r3_ref[...].astype(F32)
        g_ref[...] = g
        d_ref[...], nm_ref[...], nv_ref[...] = _adam_update(w_ref[...], g, m_ref[...], v_ref[...])

    def slot(k):
        return pl.BlockSpec((None, rows, c), lambda i, chip_ref: ((chip_ref[0] + k) % N_CHIPS, i, 0))

    spec = pl.BlockSpec((rows, c), lambda i, chip_ref: (i, 0))
    shape = jax.ShapeDtypeStruct((r, c), F32)
    return pl.pallas_call(
        body, name=name,
        grid_spec=pltpu.PrefetchScalarGridSpec(
            num_scalar_prefetch=1, grid=(r // rows,),
            in_specs=[slot(0), slot(1), slot(2), slot(3), spec, spec, spec], out_specs=[spec] * 4),
        out_shape=[shape] * 4,
        compiler_params=_params(("parallel",)),
    )(chip, own, recv, recv, recv, w, m, v)


def _sum_slots(parts, name, rows=512):
    s, r, _ = parts.shape
    rows = _row_tile(r, rows)

    def body(p_ref, o_ref):
        acc = p_ref[0]
        for k in range(1, s):
            acc = acc + p_ref[k]
        o_ref[...] = acc

    return pl.pallas_call(
        body, name=name, grid=(r // rows,),
        in_specs=[pl.BlockSpec((s, rows, LANES), lambda i: (0, i, 0))],
        out_specs=pl.BlockSpec((rows, LANES), lambda i: (i, 0)),
        out_shape=jax.ShapeDtypeStruct((r, LANES), F32),
        compiler_params=_params(("parallel",)),
    )(parts)


def _adamw(w, g, m, v, name, rows=512):
    r = w.shape[0]
    rows = _row_tile(r, rows)

    def body(w_ref, g_ref, m_ref, v_ref, d_ref, nm_ref, nv_ref):
        d_ref[...], nm_ref[...], nv_ref[...] = _adam_update(w_ref[...], g_ref[...], m_ref[...], v_ref[...])

    spec = pl.BlockSpec((rows, LANES), lambda i: (i, 0))
    shape = jax.ShapeDtypeStruct((r, LANES), F32)
    return pl.pallas_call(
        body, name=name, grid=(r // rows,), in_specs=[spec] * 4, out_specs=[spec] * 3, out_shape=[shape] * 3,
        compiler_params=_params(("parallel",)),
    )(w, g, m, v)


def _pack(arrs, dtype, row_multiple):
    flat = jnp.concatenate([a.reshape(-1).astype(dtype) for a in arrs])
    unit = LANES * row_multiple
    padded = -(-flat.shape[0] // unit) * unit
    return jnp.pad(flat, (0, padded - flat.shape[0])).reshape(-1, LANES)


def _unpack(packed, shapes, lead=()):
    flat = packed.reshape(lead + (-1,))
    out, off = [], 0
    for shp in shapes:
        n = math.prod(shp)
        out.append(flat[..., off:off + n].reshape(lead + tuple(shp)))
        off += n
    return out


def _gather_cols(g, lo, hi):
    width = g.shape[2]
    pieces = []
    for s in range(N_DEV):
        a, e = max(lo, s * width), min(hi, (s + 1) * width)
        if a < e:
            pieces.append(g[s, :, a - s * width:e - s * width])
    return pieces[0] if len(pieces) == 1 else jnp.concatenate(pieces, axis=1)


def _scatter_cols(segs, width):
    slots = []
    for k in range(N_DEV):
        lo, hi = k * width, (k + 1) * width
        pieces = []
        for arr, s_lo, s_hi in segs:
            a, e = max(lo, s_lo), min(hi, s_hi)
            if a < e:
                pieces.append(arr[:, a - s_lo:e - s_lo])
        slots.append(pieces[0] if len(pieces) == 1 else jnp.concatenate(pieces, axis=1))
    return jnp.stack(slots)


def _block_diag_groups(w):
    w4 = w.reshape(LRU_NGROUPS, 4, LRU_BLOCK, LRU_BLOCK)
    eye = jnp.eye(4, dtype=w.dtype)
    return jnp.einsum("gaij,ab->gaibj", w4, eye).reshape(LRU_NGROUPS, LRU_GROUP, LRU_GROUP)


def _block_diag_extract(wg):
    w5 = wg.reshape(LRU_NGROUPS, 4, LRU_BLOCK, 4, LRU_BLOCK)
    idx = jnp.arange(4)
    return w5[:, idx, :, idx, :].transpose(1, 0, 2, 3).reshape(LRU_BLOCKS, LRU_BLOCK, LRU_BLOCK)


BIG = ("w_in", "w_kv", "w_br_ssd", "w_br_lru", "w_br_mem", "w_out")
SMALL_SHARDED = ("ssd_conv_w", "ssd_norm_g", "lru_conv_w")
REPLICATED = ("norm_g", "ssd_conv_b", "ssd_dt_bias", "ssd_a_log", "ssd_d", "lru_conv_b", "lru_w_a", "lru_b_a",
              "lru_w_x", "lru_b_x", "lru_lambda", "mem_norm_g", "final_g")
WEIGHTS = ("norm_g", "w_in", "ssd_conv_w", "ssd_conv_b", "ssd_dt_bias", "ssd_a_log", "ssd_d", "ssd_norm_g", "lru_conv_w",
           "lru_conv_b", "lru_w_a", "lru_b_a", "lru_w_x", "lru_b_x", "lru_lambda", "mem_norm_g", "w_kv", "w_br_ssd",
           "w_br_lru", "w_br_mem", "w_out", "final_g")


def kernel(x, mem, norm_g, w_in, ssd_conv_w, ssd_conv_b, ssd_dt_bias, ssd_a_log, ssd_d, ssd_norm_g, lru_conv_w, lru_conv_b, lru_w_a, lru_b_a, lru_w_x, lru_b_x, lru_lambda, mem_norm_g, w_kv, w_br_ssd, w_br_lru, w_br_mem, w_out, final_g, loss_target, m_norm_g, m_w_in, m_ssd_conv_w, m_ssd_conv_b, m_ssd_dt_bias, m_ssd_a_log, m_ssd_d, m_ssd_norm_g, m_lru_conv_w, m_lru_conv_b, m_lru_w_a, m_lru_b_a, m_lru_w_x, m_lru_b_x, m_lru_lambda, m_mem_norm_g, m_w_kv, m_w_br_ssd, m_w_br_lru, m_w_br_mem, m_w_out, m_final_g, v_norm_g, v_w_in, v_ssd_conv_w, v_ssd_conv_b, v_ssd_dt_bias, v_ssd_a_log, v_ssd_d, v_ssd_norm_g, v_lru_conv_w, v_lru_conv_b, v_lru_w_a, v_lru_b_a, v_lru_w_x, v_lru_b_x, v_lru_lambda, v_mem_norm_g, v_w_kv, v_w_br_ssd, v_w_br_lru, v_w_br_mem, v_w_out, v_final_g):
    env = dict(locals())
    W = {n: env[n] for n in WEIGHTS}
    M = {n: env["m_" + n] for n in WEIGHTS}
    V = {n: env["v_" + n] for n in WEIGHTS}
    me = 4 * lax.axis_index("x") + 2 * lax.axis_index("y") + lax.axis_index("c")
    t = x.shape[1]
    xt = x[0]
    memt = mem[0]
    tgt = loss_target[0]

    small_shapes = [W[n].shape for n in SMALL_SHARDED]
    gathered = _all_gather([W[n][0].astype(MXU_DTYPE) for n in BIG] + [_pack([W[n] for n in SMALL_SHARDED], F32, 8)],
                           "weights_all_gather")
    g_in, g_kv, g_bs, g_bl, g_bm, g_out, gs = gathered
    g_cw, g_ng, g_lcw = _unpack(gs, small_shapes, (N_DEV,))
    cols = lambda a: jnp.moveaxis(a[:, 0], 0, -2).reshape(a.shape[2:-1] + (-1,))
    rows_ = lambda a: a.reshape((-1,) + a.shape[2:])
    w_bs_f, w_bl_f, w_bm_f, w_out_f = rows_(g_bs), rows_(g_bl), rows_(g_bm), rows_(g_out)
    conv_w_f, ssd_ng_f, lru_cw_f = cols(g_cw), cols(g_ng), cols(g_lcw)
    b = SEG_BOUNDS
    w_kv_f = _gather_cols(g_kv, 0, 2 * D_MODEL)
    w_ssd, w_lru, w_q, w_g = (_gather_cols(g_in, b[0], b[1]), _gather_cols(g_in, b[2], b[3]),
                              _gather_cols(g_in, b[3], b[4]), _gather_cols(g_in, b[4], b[5]))
    w_dt = jnp.pad(_gather_cols(g_in, b[1], b[2]), ((0, 0), (0, DT_PAD - SSD_HEADS)))

    pad_heads = lambda a: jnp.pad(a, ((0, 0), (0, LANES - SSD_HEADS)))
    dtb, alog = pad_heads(ssd_dt_bias), pad_heads(ssd_a_log)
    d_row = jnp.repeat(ssd_d, SSD_HEAD_DIM, axis=1)
    ng_row = ssd_ng_f.reshape(1, SSD_WIDTH)
    wa_g, wx_g = _block_diag_groups(lru_w_a[0]), _block_diag_groups(lru_w_x[0])
    ba, bx = lru_b_a.reshape(1, LRU_WIDTH), lru_b_x.reshape(1, LRU_WIDTH)
    fg = final_g.reshape(1, D_MODEL)

    h = _rms_fwd(xt, norm_g, "norm_fwd")
    proj_ssd = _matmul(h, w_ssd, "nn", "proj_ssd", tk=D_MODEL)
    proj_lru = _matmul(h, w_lru, "nn", "proj_lru", tk=D_MODEL)
    proj_q = _matmul(h, w_q, "nn", "proj_q", tk=D_MODEL)
    proj_g = _matmul(h, w_g, "nn", "proj_g", tk=D_MODEL)
    proj_dt = _matmul(h, w_dt, "nn", "proj_dt", tk=D_MODEL)
    mem_n = _rms_fwd(memt, mem_norm_g, "mem_norm_fwd")
    kv = _matmul(mem_n, w_kv_f, "nn", "mem_kv", tk=D_MODEL)
    yssd, y_scan, states = _ssd_fwd(proj_ssd, proj_dt, conv_w_f, ssd_conv_b, dtb, alog, d_row, ng_row)
    ylru, h_lru = _lru_fwd(proj_lru, lru_cw_f, lru_conv_b, wa_g, wx_g, ba, bx, lru_lambda)
    ymem = _mem_fwd(proj_q, kv)
    ps, pl_, pm, merged, dx2, loss_vec, g_fg = _merge_fwd(xt, yssd, ylru, ymem, proj_g, w_bs_f, w_bl_f, w_bm_f, w_out_f, fg, tgt)

    d_g, dps, dpl, dpm, dyssd, dylru, dymem = _merge_bwd(dx2, proj_g, ps, pl_, pm, w_bs_f, w_bl_f, w_bm_f, w_out_f)
    gw_out = _matmul(merged, dx2, "tn", "grad_w_out", tk=1024)
    gw_bs = _matmul(yssd, dps, "tn", "grad_w_br_ssd", tk=1024)
    gw_bl = _matmul(ylru, dpl, "tn", "grad_w_br_lru", tm=LRU_WIDTH, tk=1024)
    gw_bm = _matmul(ymem, dpm, "tn", "grad_w_br_mem", tk=1024)
    d_q, d_kv = _mem_bwd(proj_q, kv, dymem)
    gw_kv = _matmul(mem_n, d_kv, "tn", "grad_w_kv", tk=memt.shape[0])
    d_memn = _matmul(d_kv, w_kv_f, "nt", "d_mem_n", tk=1024)
    _, g_memng = _rms_bwd(memt, d_memn, None, mem_norm_g, "mem_norm_bwd")
    d_lru, gl_cw, gl_cb, g_ba, g_bx, g_lam, gwa_g, gwx_g = _lru_bwd(proj_lru, h_lru, dylru, lru_cw_f, lru_conv_b, wa_g, wx_g, ba, bx, lru_lambda)
    d_ssd, d_dt, gs_cw, gs_cb, g_dtb, g_alog, g_dch, g_ngrow = _ssd_bwd(proj_ssd, proj_dt, y_scan, states, dyssd, conv_w_f, ssd_conv_b, dtb, alog, d_row, ng_row)
    dh = _matmul(d_ssd, w_ssd, "nt", "dh_ssd", tk=1024)
    dh = _matmul(d_lru, w_lru, "nt", "dh_lru", tk=1024, acc_in=dh)
    dh = _matmul(d_q, w_q, "nt", "dh_q", tk=1024, acc_in=dh)
    dh = _matmul(d_g, w_g, "nt", "dh_g", tk=1024, acc_in=dh)
    dh = _matmul(d_dt, w_dt, "nt", "dh_dt", tk=DT_PAD, acc_in=dh)
    gw_ssd = _matmul(h, d_ssd, "tn", "grad_w_in_ssd", tk=1024)
    gw_lru = _matmul(h, d_lru, "tn", "grad_w_in_lru", tk=1024)
    gw_q = _matmul(h, d_q, "tn", "grad_w_in_q", tk=1024)
    gw_g = _matmul(h, d_g, "tn", "grad_w_in_g", tk=1024)
    gw_dt = _matmul(h, d_dt, "tn", "grad_w_in_dt", tk=1024)
    grad_x, g_normg = _rms_bwd(xt, dh, dx2, norm_g, "norm_bwd")

    split_rows = lambda a: a.reshape((N_DEV, -1) + a.shape[1:])
    in_segs = [(gw_ssd, b[0], b[1]), (gw_dt, b[1], b[2]), (gw_lru, b[2], b[3]), (gw_q, b[3], b[4]), (gw_g, b[4], b[5])]
    big_send = [_scatter_cols(in_segs, IN_WIDTH // N_DEV), _scatter_cols([(gw_kv, 0, 2 * D_MODEL)], 2 * D_MODEL // N_DEV),
                split_rows(gw_bs), split_rows(gw_bl), split_rows(gw_bm), split_rows(gw_out)]

    small_grads = {
        "norm_g": g_normg, "ssd_conv_w": gs_cw, "ssd_conv_b": gs_cb, "ssd_dt_bias": g_dtb[:, :SSD_HEADS],
        "ssd_a_log": g_alog[:, :SSD_HEADS], "ssd_d": jnp.sum(g_dch.reshape(SSD_HEADS, SSD_HEAD_DIM), axis=1).reshape(1, SSD_HEADS),
        "ssd_norm_g": g_ngrow.reshape(SSD_GROUPS, -1), "lru_conv_w": gl_cw, "lru_conv_b": gl_cb,
        "lru_w_a": _block_diag_extract(gwa_g), "lru_b_a": g_ba, "lru_w_x": _block_diag_extract(gwx_g), "lru_b_x": g_bx,
        "lru_lambda": g_lam, "mem_norm_g": g_memng, "final_g": g_fg,
    }
    small_names = tuple(small_grads)
    small_send = _pack([small_grads[n] for n in small_names], F32, 512)
    (small_recv,) = _all_gather([small_send], "small_grads_all_gather")

    core = lax.axis_index("c").astype(jnp.int32).reshape(1)
    chip = (2 * lax.axis_index("x") + lax.axis_index("y")).astype(jnp.int32).reshape(1)
    by_chip = [a.reshape((N_CHIPS, 2) + a.shape[1:]) for a in big_send]
    from_sibling = _pair_exchange(by_chip)
    chip_sums = [_chip_sum(p, r, core, "chip_sum_" + n) for n, p, r in zip(BIG, by_chip, from_sibling)]
    from_chips = _chip_exchange([s16 for _, s16 in chip_sums])

    grads, delta, new_m, new_v = {}, {}, {}, {}
    for n, (s32, _), recv in zip(BIG, chip_sums, from_chips):
        res = _sum_adamw(s32, recv, chip, W[n][0], M[n][0], V[n][0], "adamw_" + n)
        for dst, a in zip((grads, delta, new_m, new_v), res):
            dst[n] = a[None]

    totals = dict(zip(small_names, _unpack(_sum_slots(small_recv, "grad_sum_small"), [small_grads[n].shape for n in small_names])))
    for n in REPLICATED:
        grads[n] = totals[n].reshape(W[n].shape)
    for n in SMALL_SHARDED:
        width = W[n].shape[-1]
        grads[n] = lax.dynamic_slice_in_dim(totals[n], me * width, width, axis=-1).reshape(W[n].shape)
    small_all = REPLICATED + SMALL_SHARDED
    w_s = _pack([W[n] for n in small_all], F32, 512)
    g_s = _pack([grads[n] for n in small_all], F32, 512)
    m_s = _pack([M[n] for n in small_all], F32, 512)
    v_s = _pack([V[n] for n in small_all], F32, 512)
    d_s, nm_s, nv_s = _adamw(w_s, g_s, m_s, v_s, "adamw_small")

    for packed, names, dst in ((d_s, small_all, delta), (nm_s, small_all, new_m), (nv_s, small_all, new_v)):
        for n, a in zip(names, _unpack(packed, [W[k].shape for k in names])):
            dst[n] = a

    loss = lax.psum(loss_vec[0, 0], ("x", "y", "c"))
    return (loss, grad_x[None], *[grads[n] for n in WEIGHTS], *[delta[n] for n in WEIGHTS],
            *[new_m[n] for n in WEIGHTS], *[new_v[n] for n in WEIGHTS])
```

```python
import functools
import math

import jax
import jax.numpy as jnp
from jax import lax
from jax.experimental import pallas as pl
from jax.experimental.pallas import tpu as pltpu

F32 = jnp.float32
MXU_DTYPE = jnp.bfloat16
GRAD_WIRE_DTYPE = jnp.bfloat16

D_MODEL = 1024
EPS = 1e-6
CONV_WIDTH = 4
SSD_WIDTH = 2048
SSD_HEAD_DIM = 64
SSD_HEADS = 32
SSD_GROUPS = 4
SSD_STATE = 128
SSD_CHUNK = 128
SSD_CONV_CH = SSD_WIDTH + 2 * SSD_GROUPS * SSD_STATE
SSD_PAIRS = SSD_HEADS // 2
PAIRS_PER_GROUP = SSD_PAIRS // SSD_GROUPS
LRU_WIDTH = 1536
LRU_BLOCKS = 16
LRU_BLOCK = 96
LRU_GROUP = 4 * LRU_BLOCK
LRU_NGROUPS = LRU_WIDTH // LRU_GROUP
LRU_C = 8.0
LRU_ROWS = 256
MEM_HEADS = 4
MEM_HEAD_DIM = 256
IN_WIDTH = 12320
N_DEV = 8
LANES = 128
SSD_SEG = SSD_WIDTH + SSD_CONV_CH
DT_PAD = LANES
SEG_BOUNDS = (0, 5120, 5152, 8224, 9248, 12320)

ADAM_LR = 0.001
ADAM_B1 = 0.9
ADAM_B2 = 0.999
ADAM_EPS = 1e-08
ADAM_WD = 0.01
ADAM_STEP = 10

VMEM_LIMIT = 56 * 1024 * 1024

NN = (((1,), (0,)), ((), ()))
NT = (((1,), (1,)), ((), ()))
TN = (((0,), (0,)), ((), ()))


def _dot(a, b, dims):
    return lax.dot_general(a.astype(MXU_DTYPE), b.astype(MXU_DTYPE), dims, preferred_element_type=F32)


def _sigmoid(x):
    return 0.5 * jnp.tanh(0.5 * x) + 0.5


def _log1p(e):
    u = 1.0 + e
    return jnp.where(u == 1.0, e, jnp.log(u) * (e / jnp.where(u == 1.0, 1.0, u - 1.0)))


def _softplus(x):
    return jnp.maximum(x, 0.0) + _log1p(jnp.exp(-jnp.abs(x)))


def _params(semantics):
    return pltpu.CompilerParams(dimension_semantics=semantics, vmem_limit_bytes=VMEM_LIMIT)


def _shift_down(cur, halo8, k):
    rolled = pltpu.roll(cur, k, 0)
    row8 = lax.broadcasted_iota(jnp.int32, halo8.shape, 0)
    top = jnp.where(row8 >= k, rolled[0:8], pltpu.roll(halo8, k, 0))
    return jnp.concatenate([top, rolled[8:]], axis=0)


def _shift_up(cur, next8, k):
    rows = cur.shape[0]
    rolled = pltpu.roll(cur, rows - k, 0)
    row8 = lax.broadcasted_iota(jnp.int32, next8.shape, 0)
    bot = jnp.where(row8 < 8 - k, rolled[rows - 8:rows], pltpu.roll(next8, 8 - k, 0))
    return jnp.concatenate([rolled[:rows - 8], bot], axis=0)


def _causal_conv(raw, halo8, w, b):
    acc = raw * w[3:4, :] + b
    for k in range(1, CONV_WIDTH):
        acc = acc + _shift_down(raw, halo8, k) * w[3 - k:4 - k, :]
    return acc


def _conv_backward(dco, next8, raw, w):
    d_raw = dco * w[3:4, :]
    gw = [None] * CONV_WIDTH
    gw[3] = jnp.sum(dco * raw, axis=0, keepdims=True)
    for j in range(1, CONV_WIDTH):
        up = _shift_up(dco, next8, j)
        d_raw = d_raw + up * w[3 - j:4 - j, :]
        gw[3 - j] = jnp.sum(up * raw, axis=0, keepdims=True)
    gb = jnp.sum(dco, axis=0, keepdims=True)
    return d_raw, gw, gb


def _cumsum_rows(v):
    rows = v.shape[0]
    row = lax.broadcasted_iota(jnp.int32, v.shape, 0)
    s = 1
    while s < rows:
        v = v + jnp.where(row >= s, pltpu.roll(v, s, 0), 0.0)
        s *= 2
    return v


def _rev_cumsum_rows(v):
    rows = v.shape[0]
    row = lax.broadcasted_iota(jnp.int32, v.shape, 0)
    s = 1
    while s < rows:
        v = v + jnp.where(row < rows - s, pltpu.roll(v, rows - s, 0), 0.0)
        s *= 2
    return v


def _matmul(a, b, mode, name, out_dtype=F32, tm=1024, tn=1024, tk=512, acc_in=None):
    if mode == "nn":
        (m, kk), n = a.shape, b.shape[1]
    elif mode == "nt":
        (m, kk), n = a.shape, b.shape[0]
    else:
        (kk, m), n = a.shape, b.shape[1]
    tm, tn, tk = min(tm, m), min(tn, n), min(tk, kk)
    assert m % tm == 0 and n % tn == 0 and kk % tk == 0, (name, a.shape, b.shape)
    nk = kk // tk
    dims = {"nn": NN, "nt": NT, "tn": TN}[mode]
    a_spec = pl.BlockSpec((tk, tm), lambda i, j, k: (k, i)) if mode == "tn" else pl.BlockSpec((tm, tk), lambda i, j, k: (i, k))
    b_spec = pl.BlockSpec((tn, tk), lambda i, j, k: (j, k)) if mode == "nt" else pl.BlockSpec((tk, tn), lambda i, j, k: (k, j))
    o_spec = pl.BlockSpec((tm, tn), lambda i, j, k: (i, j))
    has_acc = acc_in is not None

    def body_single(*refs):
        if has_acc:
            a_ref, b_ref, c_ref, o_ref = refs
            o_ref[...] = (c_ref[...].astype(F32) + _dot(a_ref[...], b_ref[...], dims)).astype(o_ref.dtype)
        else:
            a_ref, b_ref, o_ref = refs
            o_ref[...] = _dot(a_ref[...], b_ref[...], dims).astype(o_ref.dtype)

    def body(*refs):
        if has_acc:
            a_ref, b_ref, c_ref, o_ref, acc_ref = refs
        else:
            a_ref, b_ref, o_ref, acc_ref = refs
        k = pl.program_id(2)

        @pl.when(k == 0)
        def _():
            acc_ref[...] = c_ref[...].astype(F32) if has_acc else jnp.zeros_like(acc_ref)

        acc_ref[...] += _dot(a_ref[...], b_ref[...], dims)

        @pl.when(k == nk - 1)
        def _():
            o_ref[...] = acc_ref[...].astype(o_ref.dtype)

    args = (a, b) + ((acc_in,) if has_acc else ())
    in_specs = [a_spec, b_spec] + ([o_spec] if has_acc else [])
    return pl.pallas_call(
        body_single if nk == 1 else body, name=name, grid=(m // tm, n // tn, nk), in_specs=in_specs, out_specs=o_spec,
        out_shape=jax.ShapeDtypeStruct((m, n), out_dtype),
        scratch_shapes=[] if nk == 1 else [pltpu.VMEM((tm, tn), F32)],
        compiler_params=_params(("parallel", "parallel", "arbitrary")),
    )(*args)


def _rms_fwd(x, g, name, rows=512):
    t, d = x.shape
    rows = min(rows, t)

    def body(x_ref, g_ref, h_ref):
        xv = x_ref[...]
        r = lax.rsqrt(jnp.mean(xv * xv, axis=-1, keepdims=True) + EPS)
        h_ref[...] = ((xv * r) * g_ref[...]).astype(h_ref.dtype)

    return pl.pallas_call(
        body, name=name, grid=(t // rows,),
        in_specs=[pl.BlockSpec((rows, d), lambda i: (i, 0)), pl.BlockSpec((1, d), lambda i: (0, 0))],
        out_specs=pl.BlockSpec((rows, d), lambda i: (i, 0)),
        out_shape=jax.ShapeDtypeStruct((t, d), MXU_DTYPE),
        compiler_params=_params(("parallel",)),
    )(x, g)


def _rms_bwd(x, dh, dres, g, name, rows=512):
    t, d = x.shape
    rows = min(rows, t)
    has_res = dres is not None

    def body(*refs):
        if has_res:
            x_ref, dh_ref, dr_ref, g_ref, dx_ref, gg_ref = refs
        else:
            x_ref, dh_ref, g_ref, dx_ref, gg_ref = refs

        @pl.when(pl.program_id(0) == 0)
        def _():
            gg_ref[...] = jnp.zeros_like(gg_ref)

        xv = x_ref[...]
        dhv = dh_ref[...]
        r = lax.rsqrt(jnp.mean(xv * xv, axis=-1, keepdims=True) + EPS)
        n = xv * r
        dn = dhv * g_ref[...]
        dx = r * (dn - n * jnp.mean(dn * n, axis=-1, keepdims=True))
        if has_res:
            dx = dx + dr_ref[...]
        dx_ref[...] = dx
        gg_ref[...] += jnp.sum(dhv * n, axis=0, keepdims=True)

    row_spec = pl.BlockSpec((rows, d), lambda i: (i, 0))
    vec_spec = pl.BlockSpec((1, d), lambda i: (0, 0))
    args = (x, dh) + ((dres,) if has_res else ()) + (g,)
    return pl.pallas_call(
        body, name=name, grid=(t // rows,),
        in_specs=[row_spec, row_spec] + ([row_spec] if has_res else []) + [vec_spec],
        out_specs=[row_spec, vec_spec],
        out_shape=[jax.ShapeDtypeStruct((t, d), F32), jax.ShapeDtypeStruct((1, d), F32)],
        compiler_params=_params(("arbitrary",)),
    )(*args)


def _pair_select(lo, m, h0):
    return jnp.where(lo, m[:, h0:h0 + 1], m[:, h0 + 1:h0 + 2])


def _halves(lo, v):
    return (jnp.sum(jnp.where(lo, v, 0.0), axis=1, keepdims=True),
            jnp.sum(jnp.where(lo, 0.0, v), axis=1, keepdims=True))


def _ssd_common(dt_raw, dtb, alog):
    dt = _softplus(dt_raw + dtb)
    aneg = -jnp.exp(alog)
    a_cs = _cumsum_rows(dt * aneg)
    return dt, aneg, a_cs, a_cs.T


def _ssd_specs(nc, rev):
    cidx = (lambda c: nc - 1 - c) if rev else (lambda c: c)
    L = SSD_CHUNK
    return dict(
        z=pl.BlockSpec((L, SSD_WIDTH), lambda c: (cidx(c), 0)),
        xr=pl.BlockSpec((L, SSD_WIDTH), lambda c: (cidx(c), 1)),
        br=pl.BlockSpec((L, 512), lambda c: (cidx(c), 8)),
        cr=pl.BlockSpec((L, 512), lambda c: (cidx(c), 9)),
        dt=pl.BlockSpec((L, DT_PAD), lambda c: (cidx(c), 0)),
        cwx=pl.BlockSpec((CONV_WIDTH, SSD_WIDTH), lambda c: (0, 0)),
        cwb=pl.BlockSpec((CONV_WIDTH, 512), lambda c: (0, 4)),
        cwc=pl.BlockSpec((CONV_WIDTH, 512), lambda c: (0, 5)),
        cbx=pl.BlockSpec((1, SSD_WIDTH), lambda c: (0, 0)),
        cbb=pl.BlockSpec((1, 512), lambda c: (0, 4)),
        cbc=pl.BlockSpec((1, 512), lambda c: (0, 5)),
        vec128=pl.BlockSpec((1, LANES), lambda c: (0, 0)),
        vecw=pl.BlockSpec((1, SSD_WIDTH), lambda c: (0, 0)),
        wide=pl.BlockSpec((L, SSD_WIDTH), lambda c: (cidx(c), 0)),
        states=pl.BlockSpec((1, SSD_PAIRS, 128, SSD_STATE), lambda c: (cidx(c), 0, 0, 0)),
    )


def _ssd_fwd(proj_ssd, dt_p, conv_w, conv_b, dtb, alog, d_row, ng_row):
    t = proj_ssd.shape[0]
    nc = t // SSD_CHUNK
    L = SSD_CHUNK
    sp = _ssd_specs(nc, False)

    def body(z_ref, xr_ref, br_ref, cr_ref, dt_ref, cwx_ref, cwb_ref, cwc_ref, cbx_ref, cbb_ref, cbc_ref,
             dtb_ref, alog_ref, d_ref, ng_ref, yssd_ref, y_ref, st_ref,
             hx_ref, hb_ref, hc_ref, state_ref, yacc_ref):
        @pl.when(pl.program_id(0) == 0)
        def _():
            hx_ref[...] = jnp.zeros_like(hx_ref)
            hb_ref[...] = jnp.zeros_like(hb_ref)
            hc_ref[...] = jnp.zeros_like(hc_ref)
            state_ref[...] = jnp.zeros_like(state_ref)

        xr, br, cr = xr_ref[...], br_ref[...], cr_ref[...]
        px = _causal_conv(xr, hx_ref[...], cwx_ref[...], cbx_ref[...])
        pb = _causal_conv(br, hb_ref[...], cwb_ref[...], cbb_ref[...])
        pc = _causal_conv(cr, hc_ref[...], cwc_ref[...], cbc_ref[...])
        hx_ref[...] = xr[L - 8:L, :]
        hb_ref[...] = br[L - 8:L, :]
        hc_ref[...] = cr[L - 8:L, :]
        xs = px * _sigmoid(px)
        bm = pb * _sigmoid(pb)
        cm = pc * _sigmoid(pc)

        dt, _, a_cs, a_t = _ssd_common(dt_ref[...], dtb_ref[...], alog_ref[...])
        exp_a = jnp.exp(a_cs)
        a_last = a_cs[L - 1:L, :]
        dte = jnp.exp(a_last - a_cs)
        dec = jnp.exp(a_last)

        lane = lax.broadcasted_iota(jnp.int32, (L, LANES), 1)
        sub = lax.broadcasted_iota(jnp.int32, (L, LANES), 0)
        lo = lane < SSD_HEAD_DIM
        causal = sub >= lane
        top = sub < SSD_HEAD_DIM

        for g in range(SSD_GROUPS):
            b_g = bm[:, g * SSD_STATE:(g + 1) * SSD_STATE]
            c_g = cm[:, g * SSD_STATE:(g + 1) * SSD_STATE]
            cb = _dot(c_g, b_g, NT)
            for jj in range(PAIRS_PER_GROUP):
                j = g * PAIRS_PER_GROUP + jj
                h0 = 2 * j
                cols = slice(j * LANES, (j + 1) * LANES)
                xs_p = xs[:, cols]
                xdt = xs_p * _pair_select(lo, dt, h0)
                g0 = jnp.where(causal, jnp.exp(a_cs[:, h0:h0 + 1] - a_t[h0:h0 + 1, :]), 0.0) * cb
                g1 = jnp.where(causal, jnp.exp(a_cs[:, h0 + 1:h0 + 2] - a_t[h0 + 1:h0 + 2, :]), 0.0) * cb
                lhs = jnp.concatenate([g0, g1], axis=1)
                rhs = jnp.concatenate([jnp.where(lo, xdt, 0.0), jnp.where(lo, 0.0, xdt)], axis=0)
                y_diag = _dot(lhs, rhs, NN)
                h_p = state_ref[j]
                st_ref[0, j] = h_p
                y_off = _dot(c_g, h_p, NT) * _pair_select(lo, exp_a, h0)
                s_new = _dot(xdt * _pair_select(lo, dte, h0), b_g, TN)
                dec_rows = jnp.where(top, dec[:, h0:h0 + 1], dec[:, h0 + 1:h0 + 2])
                state_ref[j] = h_p * dec_rows + s_new
                yacc_ref[:, cols] = (y_diag + y_off) + xs_p * d_ref[:, cols]

        y = yacc_ref[...]
        y_ref[...] = y
        zz = z_ref[...]
        y2 = y * (zz * _sigmoid(zz))
        gw = SSD_WIDTH // SSD_GROUPS
        for g in range(SSD_GROUPS):
            seg = y2[:, g * gw:(g + 1) * gw]
            r = lax.rsqrt(jnp.mean(seg * seg, axis=-1, keepdims=True) + EPS)
            yssd_ref[:, g * gw:(g + 1) * gw] = ((seg * r) * ng_ref[:, g * gw:(g + 1) * gw]).astype(yssd_ref.dtype)

    return pl.pallas_call(
        body, name="ssd_fwd", grid=(nc,),
        in_specs=[sp["z"], sp["xr"], sp["br"], sp["cr"], sp["dt"], sp["cwx"], sp["cwb"], sp["cwc"],
                  sp["cbx"], sp["cbb"], sp["cbc"], sp["vec128"], sp["vec128"], sp["vecw"], sp["vecw"]],
        out_specs=[sp["wide"], sp["wide"], sp["states"]],
        out_shape=[jax.ShapeDtypeStruct((t, SSD_WIDTH), MXU_DTYPE), jax.ShapeDtypeStruct((t, SSD_WIDTH), F32),
                   jax.ShapeDtypeStruct((nc, SSD_PAIRS, 128, SSD_STATE), F32)],
        scratch_shapes=[pltpu.VMEM((8, SSD_WIDTH), F32), pltpu.VMEM((8, 512), F32), pltpu.VMEM((8, 512), F32),
                        pltpu.VMEM((SSD_PAIRS, 128, SSD_STATE), F32), pltpu.VMEM((L, SSD_WIDTH), F32)],
        compiler_params=_params(("arbitrary",)),
    )(proj_ssd, proj_ssd, proj_ssd, proj_ssd, dt_p, conv_w, conv_w, conv_w, conv_b, conv_b, conv_b,
      dtb, alog, d_row, ng_row)


def _ssd_bwd(proj_ssd, dt_p, y, states, dyssd, conv_w, conv_b, dtb, alog, d_row, ng_row):
    t = proj_ssd.shape[0]
    nc = t // SSD_CHUNK
    L = SSD_CHUNK
    sp = _ssd_specs(nc, True)
    groups8 = L // 8

    def halo_spec(width, col):
        return pl.BlockSpec((8, width), lambda c: (jnp.maximum((nc - 1 - c) * groups8 - 1, 0), col))

    def body(z_ref, xr_ref, br_ref, cr_ref, hx_ref, hb_ref, hc_ref, dt_ref, y_ref, st_ref, dy_ref,
             cwx_ref, cwb_ref, cwc_ref, cbx_ref, cbb_ref, cbc_ref, dtb_ref, alog_ref, d_ref, ng_ref,
             dssd_ref, ddt_ref, gcw_ref, gcb_ref, gdtb_ref, galog_ref, gd_ref, gng_ref,
             gn_ref, nx_ref, nb_ref, ncc_ref, dxs_ref):
        step = pl.program_id(0)

        @pl.when(step == 0)
        def _():
            gn_ref[...] = jnp.zeros_like(gn_ref)
            nx_ref[...] = jnp.zeros_like(nx_ref)
            nb_ref[...] = jnp.zeros_like(nb_ref)
            ncc_ref[...] = jnp.zeros_like(ncc_ref)
            for ref in (gcw_ref, gcb_ref, gdtb_ref, galog_ref, gd_ref, gng_ref):
                ref[...] = jnp.zeros_like(ref)

        first_chunk = step == nc - 1
        keep = jnp.where(first_chunk, 0.0, 1.0)
        xr, br, cr = xr_ref[...], br_ref[...], cr_ref[...]
        hx, hb, hc = hx_ref[...] * keep, hb_ref[...] * keep, hc_ref[...] * keep
        cwx, cwb, cwc = cwx_ref[...], cwb_ref[...], cwc_ref[...]
        px = _causal_conv(xr, hx, cwx, cbx_ref[...])
        pb = _causal_conv(br, hb, cwb, cbb_ref[...])
        pc = _causal_conv(cr, hc, cwc, cbc_ref[...])
        sx, sb, sc = _sigmoid(px), _sigmoid(pb), _sigmoid(pc)
        xs, bm, cm = px * sx, pb * sb, pc * sc

        dt_in = dt_ref[...] + dtb_ref[...]
        dt, aneg, a_cs, a_t = _ssd_common(dt_ref[...], dtb_ref[...], alog_ref[...])
        exp_a = jnp.exp(a_cs)
        a_last = a_cs[L - 1:L, :]
        dte = jnp.exp(a_last - a_cs)
        dec = jnp.exp(a_last)

        lane = lax.broadcasted_iota(jnp.int32, (L, LANES), 1)
        sub = lax.broadcasted_iota(jnp.int32, (L, LANES), 0)
        lo = lane < SSD_HEAD_DIM
        causal = sub >= lane
        top = sub < SSD_HEAD_DIM
        last_row = sub == L - 1

        yv = y_ref[...]
        zz = z_ref[...]
        sz = _sigmoid(zz)
        silz = zz * sz
        y2 = yv * silz
        dyv = dy_ref[...]
        gw = SSD_WIDTH // SSD_GROUPS
        d_y2_parts = []
        gng_parts = []
        for g in range(SSD_GROUPS):
            seg = y2[:, g * gw:(g + 1) * gw]
            dseg = dyv[:, g * gw:(g + 1) * gw]
            r = lax.rsqrt(jnp.mean(seg * seg, axis=-1, keepdims=True) + EPS)
            n = seg * r
            dn = dseg * ng_ref[:, g * gw:(g + 1) * gw]
            gng_parts.append(jnp.sum(dseg * n, axis=0, keepdims=True))
            d_y2_parts.append(r * (dn - n * jnp.mean(dn * n, axis=-1, keepdims=True)))
        d_y2 = jnp.concatenate(d_y2_parts, axis=1)
        gng_ref[...] += jnp.concatenate(gng_parts, axis=1)
        d_y = d_y2 * silz
        dssd_ref[:, 0:SSD_WIDTH] = (d_y2 * yv * (sz * (1.0 + zz * (1.0 - sz)))).astype(dssd_ref.dtype)
        gd_ref[...] += jnp.sum(d_y * xs, axis=0, keepdims=True)
        dxs_ref[...] = d_y * d_ref[...]

        d_a = jnp.zeros((L, LANES), F32)
        d_at = jnp.zeros((LANES, L), F32)
        ddt = jnp.zeros((L, LANES), F32)
        d_b_parts, d_c_parts = [], []
        for g in range(SSD_GROUPS):
            b_g = bm[:, g * SSD_STATE:(g + 1) * SSD_STATE]
            c_g = cm[:, g * SSD_STATE:(g + 1) * SSD_STATE]
            cb = _dot(c_g, b_g, NT)
            d_cb = jnp.zeros((L, L), F32)
            d_bg = jnp.zeros((L, SSD_STATE), F32)
            d_cg = jnp.zeros((L, SSD_STATE), F32)
            for jj in range(PAIRS_PER_GROUP):
                j = g * PAIRS_PER_GROUP + jj
                h0 = 2 * j
                cols = slice(j * LANES, (j + 1) * LANES)
                dy_p = d_y[:, cols]
                xs_p = xs[:, cols]
                dt_pp = _pair_select(lo, dt, h0)
                expa_p = _pair_select(lo, exp_a, h0)
                dte_p = _pair_select(lo, dte, h0)
                xdt = xs_p * dt_pp
                l0 = jnp.where(causal, jnp.exp(a_cs[:, h0:h0 + 1] - a_t[h0:h0 + 1, :]), 0.0)
                l1 = jnp.where(causal, jnp.exp(a_cs[:, h0 + 1:h0 + 2] - a_t[h0 + 1:h0 + 2, :]), 0.0)
                g0, g1 = l0 * cb, l1 * cb
                h_p = st_ref[0, j]
                gn_p = gn_ref[j]
                dys = dy_p * expa_p
                d_cg = d_cg + _dot(dys, h_p, NN)
                d_h = _dot(dys, c_g, TN)
                t1 = dy_p * _dot(c_g, h_p, NT) * expa_p
                dw = _dot(b_g, gn_p, NT)
                d_bg = d_bg + _dot(xdt * dte_p, gn_p, NN)
                d_xdt = dw * dte_p
                t2 = d_xdt * xdt
                dyl, dyh = jnp.where(lo, dy_p, 0.0), jnp.where(lo, 0.0, dy_p)
                d_xdt = d_xdt + _dot(jnp.concatenate([g0, g1], axis=0), jnp.concatenate([dyl, dyh], axis=0), TN)
                dm0 = _dot(dyl, xdt, NT)
                dm1 = _dot(dyh, xdt, NT)
                d_cb = d_cb + (l0 * dm0 + l1 * dm1)
                e0, e1 = dm0 * g0, dm1 * g1
                a0, a1 = _halves(lo, t1 - t2)
                a0 = a0 + jnp.sum(e0, axis=1, keepdims=True)
                a1 = a1 + jnp.sum(e1, axis=1, keepdims=True)
                s0, s1 = _halves(lo, t2)
                gh = jnp.sum(gn_p * h_p, axis=1, keepdims=True)
                dd0 = jnp.sum(jnp.where(top[:, 0:1], gh, 0.0), axis=0, keepdims=True)
                dd1 = jnp.sum(jnp.where(top[:, 0:1], 0.0, gh), axis=0, keepdims=True)
                end0 = jnp.sum(s0, axis=0, keepdims=True) + dd0 * dec[:, h0:h0 + 1]
                end1 = jnp.sum(s1, axis=0, keepdims=True) + dd1 * dec[:, h0 + 1:h0 + 2]
                d_a = d_a + jnp.where(lane == h0, a0 + jnp.where(last_row, end0, 0.0), 0.0)
                d_a = d_a + jnp.where(lane == h0 + 1, a1 + jnp.where(last_row, end1, 0.0), 0.0)
                d_at = d_at - jnp.where(sub == h0, jnp.sum(e0, axis=0, keepdims=True), 0.0)
                d_at = d_at - jnp.where(sub == h0 + 1, jnp.sum(e1, axis=0, keepdims=True), 0.0)
                dec_rows = jnp.where(top, dec[:, h0:h0 + 1], dec[:, h0 + 1:h0 + 2])
                gn_ref[j] = d_h + dec_rows * gn_p
                q0, q1 = _halves(lo, d_xdt * xs_p)
                ddt = ddt + jnp.where(lane == h0, q0, 0.0) + jnp.where(lane == h0 + 1, q1, 0.0)
                dxs_ref[:, cols] += d_xdt * dt_pp
            d_cg = d_cg + _dot(d_cb, b_g, NN)
            d_bg = d_bg + _dot(d_cb, c_g, TN)
            d_b_parts.append(d_bg)
            d_c_parts.append(d_cg)

        rc = _rev_cumsum_rows(d_a + d_at.T)
        d_dt = rc * aneg + ddt
        galog_ref[...] += jnp.sum(rc * dt, axis=0, keepdims=True) * aneg
        d_dtraw = d_dt * _sigmoid(dt_in)
        gdtb_ref[...] += jnp.sum(d_dtraw, axis=0, keepdims=True)
        ddt_ref[...] = d_dtraw.astype(ddt_ref.dtype)

        def dsilu(p, s):
            return s * (1.0 + p * (1.0 - s))

        dcx = dxs_ref[...] * dsilu(px, sx)
        dcb = jnp.concatenate(d_b_parts, axis=1) * dsilu(pb, sb)
        dcc = jnp.concatenate(d_c_parts, axis=1) * dsilu(pc, sc)
        drx, gwx, gbx = _conv_backward(dcx, nx_ref[...], xr, cwx)
        drb, gwb, gbb = _conv_backward(dcb, nb_ref[...], br, cwb)
        drc, gwc, gbc = _conv_backward(dcc, ncc_ref[...], cr, cwc)
        nx_ref[...] = dcx[0:8, :]
        nb_ref[...] = dcb[0:8, :]
        ncc_ref[...] = dcc[0:8, :]
        dssd_ref[:, SSD_WIDTH:2 * SSD_WIDTH] = drx.astype(dssd_ref.dtype)
        dssd_ref[:, 2 * SSD_WIDTH:2 * SSD_WIDTH + 512] = drb.astype(dssd_ref.dtype)
        dssd_ref[:, 2 * SSD_WIDTH + 512:SSD_SEG] = drc.astype(dssd_ref.dtype)
        for k in range(CONV_WIDTH):
            gcw_ref[k:k + 1, :] += jnp.concatenate([gwx[k], gwb[k], gwc[k]], axis=1)
        gcb_ref[...] += jnp.concatenate([gbx, gbb, gbc], axis=1)

    const = lambda shape: pl.BlockSpec(shape, lambda c: (0,) * len(shape))
    return pl.pallas_call(
        body, name="ssd_bwd", grid=(nc,),
        in_specs=[sp["z"], sp["xr"], sp["br"], sp["cr"], halo_spec(SSD_WIDTH, 1), halo_spec(512, 8), halo_spec(512, 9),
                  sp["dt"], sp["wide"], sp["states"], sp["wide"],
                  sp["cwx"], sp["cwb"], sp["cwc"], sp["cbx"], sp["cbb"], sp["cbc"],
                  sp["vec128"], sp["vec128"], sp["vecw"], sp["vecw"]],
        out_specs=[pl.BlockSpec((L, SSD_SEG), lambda c: (nc - 1 - c, 0)), sp["dt"],
                   const((CONV_WIDTH, SSD_CONV_CH)), const((1, SSD_CONV_CH)), const((1, LANES)), const((1, LANES)),
                   const((1, SSD_WIDTH)), const((1, SSD_WIDTH))],
        out_shape=[jax.ShapeDtypeStruct((t, SSD_SEG), MXU_DTYPE), jax.ShapeDtypeStruct((t, DT_PAD), MXU_DTYPE),
                   jax.ShapeDtypeStruct((CONV_WIDTH, SSD_CONV_CH), F32), jax.ShapeDtypeStruct((1, SSD_CONV_CH), F32),
                   jax.ShapeDtypeStruct((1, LANES), F32), jax.ShapeDtypeStruct((1, LANES), F32),
                   jax.ShapeDtypeStruct((1, SSD_WIDTH), F32), jax.ShapeDtypeStruct((1, SSD_WIDTH), F32)],
        scratch_shapes=[pltpu.VMEM((SSD_PAIRS, 128, SSD_STATE), F32), pltpu.VMEM((8, SSD_WIDTH), F32),
                        pltpu.VMEM((8, 512), F32), pltpu.VMEM((8, 512), F32), pltpu.VMEM((L, SSD_WIDTH), F32)],
        compiler_params=_params(("arbitrary",)),
    )(proj_ssd, proj_ssd, proj_ssd, proj_ssd, proj_ssd, proj_ssd, proj_ssd, dt_p, y, states, dyssd,
      conv_w, conv_w, conv_w, conv_b, conv_b, conv_b, dtb, alog, d_row, ng_row)


def _lru_gates(xl, wa_ref, wx_ref, ba, bx, lam):
    pre_a, pre_x = [], []
    for g in range(LRU_NGROUPS):
        xg = xl[:, g * LRU_GROUP:(g + 1) * LRU_GROUP]
        pre_a.append(_dot(xg, wa_ref[g], NN))
        pre_x.append(_dot(xg, wx_ref[g], NN))
    r = _sigmoid(jnp.concatenate(pre_a, axis=1) + ba)
    i = _sigmoid(jnp.concatenate(pre_x, axis=1) + bx)
    log_a = (-LRU_C * r) * _softplus(-lam)
    a = jnp.exp(log_a)
    mult = jnp.sqrt(-jnp.tanh(log_a) * (a * a + 1.0))
    return r, i, log_a, a, mult


def _scan_rows(p, u, carry, reverse):
    rows, w = p.shape
    groups = rows // 8
    p3, u3 = p.reshape(groups, 8, w), u.reshape(groups, 8, w)
    row = lax.broadcasted_iota(jnp.int32, (groups, 8, w), 1)
    for s in (1, 2, 4):
        ok = row < 8 - s if reverse else row >= s
        shift = 8 - s if reverse else s
        u3 = p3 * jnp.where(ok, pltpu.roll(u3, shift, 1), 0.0) + u3
        p3 = p3 * jnp.where(ok, pltpu.roll(p3, shift, 1), 1.0)
    out = [None] * groups
    for k in (range(groups - 1, -1, -1) if reverse else range(groups)):
        out[k] = p3[k] * carry + u3[k]
        carry = out[k][0:1, :] if reverse else out[k][7:8, :]
    return jnp.concatenate(out, axis=0), carry


def _lru_fwd(proj_lru, conv_w, conv_b, wa, wx, ba, bx, lam):
    t = proj_lru.shape[0]
    rows = min(LRU_ROWS, t)
    nb = t // rows
    W = LRU_WIDTH

    def body(lg_ref, lx_ref, cw_ref, cb_ref, wa_ref, wx_ref, ba_ref, bx_ref, lam_ref, ylru_ref, h_ref,
             halo_ref, carry_ref):
        @pl.when(pl.program_id(0) == 0)
        def _():
            halo_ref[...] = jnp.zeros_like(halo_ref)
            carry_ref[...] = jnp.zeros_like(carry_ref)

        lx = lx_ref[...]
        xl = _causal_conv(lx, halo_ref[...], cw_ref[...], cb_ref[...])
        halo_ref[...] = lx[rows - 8:rows, :]
        _, i, _, a, mult = _lru_gates(xl, wa_ref, wx_ref, ba_ref[...], bx_ref[...], lam_ref[...])
        u = mult * (i * xl)
        h, carry_ref[...] = _scan_rows(a, u, carry_ref[...], False)
        h_ref[...] = h
        lg = lg_ref[...]
        ylru_ref[...] = (h * (lg * _sigmoid(lg))).astype(ylru_ref.dtype)

    const = lambda shape: pl.BlockSpec(shape, lambda b: (0,) * len(shape))
    return pl.pallas_call(
        body, name="lru_fwd", grid=(nb,),
        in_specs=[pl.BlockSpec((rows, W), lambda b: (b, 0)), pl.BlockSpec((rows, W), lambda b: (b, 1)),
                  const((CONV_WIDTH, W)), const((1, W)), const((LRU_NGROUPS, LRU_GROUP, LRU_GROUP)),
                  const((LRU_NGROUPS, LRU_GROUP, LRU_GROUP)), const((1, W)), const((1, W)), const((1, W))],
        out_specs=[pl.BlockSpec((rows, W), lambda b: (b, 0)), pl.BlockSpec((rows, W), lambda b: (b, 0))],
        out_shape=[jax.ShapeDtypeStruct((t, W), MXU_DTYPE), jax.ShapeDtypeStruct((t, W), F32)],
        scratch_shapes=[pltpu.VMEM((8, W), F32), pltpu.VMEM((1, W), F32)],
        compiler_params=_params(("arbitrary",)),
    )(proj_lru, proj_lru, conv_w, conv_b, wa, wx, ba, bx, lam)


def _lru_bwd(proj_lru, h, dylru, conv_w, conv_b, wa, wx, ba, bx, lam):
    t = proj_lru.shape[0]
    rows = min(LRU_ROWS, t)
    nb = t // rows
    W = LRU_WIDTH
    groups8 = rows // 8

    def rev(b):
        return nb - 1 - b

    def halo_spec(col):
        return pl.BlockSpec((8, W), lambda b: (jnp.maximum(rev(b) * groups8 - 1, 0), col))

    def body(lg_ref, lx_ref, hlx_ref, h_ref, hh_ref, dy_ref, cw_ref, cb_ref, wa_ref, wx_ref, ba_ref, bx_ref, lam_ref,
             dlru_ref, gcw_ref, gcb_ref, gba_ref, gbx_ref, glam_ref, gwa_ref, gwx_ref,
             gcarry_ref, afirst_ref, nxt_ref):
        step = pl.program_id(0)

        @pl.when(step == 0)
        def _():
            gcarry_ref[...] = jnp.zeros_like(gcarry_ref)
            afirst_ref[...] = jnp.zeros_like(afirst_ref)
            nxt_ref[...] = jnp.zeros_like(nxt_ref)
            for ref in (gcw_ref, gcb_ref, gba_ref, gbx_ref, glam_ref, gwa_ref, gwx_ref):
                ref[...] = jnp.zeros_like(ref)

        keep = jnp.where(step == nb - 1, 0.0, 1.0)
        lx = lx_ref[...]
        hlx = hlx_ref[...] * keep
        cw = cw_ref[...]
        xl = _causal_conv(lx, hlx, cw, cb_ref[...])
        lam = lam_ref[...]
        r, i, log_a, a, mult = _lru_gates(xl, wa_ref, wx_ref, ba_ref[...], bx_ref[...], lam)
        hv = h_ref[...]
        h_prev = _shift_down(hv, hh_ref[...] * keep, 1)
        lg = lg_ref[...]
        sg = _sigmoid(lg)
        dyv = dy_ref[...]
        d_h = dyv * (lg * sg)
        dlru_ref[:, 0:W] = (dyv * hv * (sg * (1.0 + lg * (1.0 - sg)))).astype(dlru_ref.dtype)

        row = lax.broadcasted_iota(jnp.int32, (rows, W), 0)
        p = jnp.where(row < rows - 1, pltpu.roll(a, rows - 1, 0), afirst_ref[...])
        gsc, gcarry_ref[...] = _scan_rows(p, d_h, gcarry_ref[...], True)
        afirst_ref[...] = a[0:1, :]

        d_a = gsc * h_prev
        v = i * xl
        d_mult = gsc * v
        d_v = gsc * mult
        d_i = d_v * xl
        d_xl = d_v * i
        d_la = d_a * a - d_mult * (a * a) / mult
        sp_neg = _softplus(-lam)
        d_r = d_la * (-LRU_C * sp_neg)
        glam_ref[...] += jnp.sum(d_la * r, axis=0, keepdims=True) * (LRU_C * _sigmoid(-lam))
        d_pa = d_r * r * (1.0 - r)
        d_px = d_i * i * (1.0 - i)
        gba_ref[...] += jnp.sum(d_pa, axis=0, keepdims=True)
        gbx_ref[...] += jnp.sum(d_px, axis=0, keepdims=True)
        parts = []
        for g in range(LRU_NGROUPS):
            cols = slice(g * LRU_GROUP, (g + 1) * LRU_GROUP)
            xg, dpa_g, dpx_g = xl[:, cols], d_pa[:, cols], d_px[:, cols]
            parts.append(_dot(dpa_g, wa_ref[g], NT) + _dot(dpx_g, wx_ref[g], NT))
            gwa_ref[g] += _dot(xg, dpa_g, TN)
            gwx_ref[g] += _dot(xg, dpx_g, TN)
        d_xl = d_xl + jnp.concatenate(parts, axis=1)
        d_lx, gw, gb = _conv_backward(d_xl, nxt_ref[...], lx, cw)
        nxt_ref[...] = d_xl[0:8, :]
        dlru_ref[:, W:2 * W] = d_lx.astype(dlru_ref.dtype)
        for k in range(CONV_WIDTH):
            gcw_ref[k:k + 1, :] += gw[k]
        gcb_ref[...] += gb

    const = lambda shape: pl.BlockSpec(shape, lambda b: (0,) * len(shape))
    wspec = const((LRU_NGROUPS, LRU_GROUP, LRU_GROUP))
    blk = lambda col: pl.BlockSpec((rows, W), lambda b: (rev(b), col))
    return pl.pallas_call(
        body, name="lru_bwd", grid=(nb,),
        in_specs=[blk(0), blk(1), halo_spec(1), blk(0), halo_spec(0), blk(0),
                  const((CONV_WIDTH, W)), const((1, W)), wspec, wspec, const((1, W)), const((1, W)), const((1, W))],
        out_specs=[pl.BlockSpec((rows, 2 * W), lambda b: (rev(b), 0)), const((CONV_WIDTH, W)), const((1, W)),
                   const((1, W)), const((1, W)), const((1, W)), wspec, wspec],
        out_shape=[jax.ShapeDtypeStruct((t, 2 * W), MXU_DTYPE), jax.ShapeDtypeStruct((CONV_WIDTH, W), F32),
                   jax.ShapeDtypeStruct((1, W), F32), jax.ShapeDtypeStruct((1, W), F32), jax.ShapeDtypeStruct((1, W), F32),
                   jax.ShapeDtypeStruct((1, W), F32), jax.ShapeDtypeStruct((LRU_NGROUPS, LRU_GROUP, LRU_GROUP), F32),
                   jax.ShapeDtypeStruct((LRU_NGROUPS, LRU_GROUP, LRU_GROUP), F32)],
        scratch_shapes=[pltpu.VMEM((1, W), F32), pltpu.VMEM((1, W), F32), pltpu.VMEM((8, W), F32)],
        compiler_params=_params(("arbitrary",)),
    )(proj_lru, proj_lru, proj_lru, h, h, dylru, conv_w, conv_b, wa, wx, ba, bx, lam)


def _mem_scores(q_h, k_h):
    s = _dot(q_h, k_h, NT) * (MEM_HEAD_DIM ** -0.5)
    s = s - jnp.max(s, axis=-1, keepdims=True)
    e = jnp.exp(s)
    return e / jnp.sum(e, axis=-1, keepdims=True)


def _mem_fwd(q, kv, rows=512):
    t = q.shape[0]
    rows = min(rows, t)
    m = kv.shape[0]

    def body(q_ref, kv_ref, y_ref):
        for hd in range(MEM_HEADS):
            cols = slice(hd * MEM_HEAD_DIM, (hd + 1) * MEM_HEAD_DIM)
            vcols = slice(D_MODEL + hd * MEM_HEAD_DIM, D_MODEL + (hd + 1) * MEM_HEAD_DIM)
            p = _mem_scores(q_ref[:, cols], kv_ref[:, cols])
            y_ref[:, cols] = _dot(p, kv_ref[:, vcols], NN).astype(y_ref.dtype)

    return pl.pallas_call(
        body, name="mem_fwd", grid=(t // rows,),
        in_specs=[pl.BlockSpec((rows, D_MODEL), lambda i: (i, 0)), pl.BlockSpec((m, 2 * D_MODEL), lambda i: (0, 0))],
        out_specs=pl.BlockSpec((rows, D_MODEL), lambda i: (i, 0)),
        out_shape=jax.ShapeDtypeStruct((t, D_MODEL), MXU_DTYPE),
        compiler_params=_params(("parallel",)),
    )(q, kv)


def _mem_bwd(q, kv, dy, rows=512):
    t = q.shape[0]
    rows = min(rows, t)
    m = kv.shape[0]

    def body(q_ref, kv_ref, dy_ref, dq_ref, dkv_ref):
        @pl.when(pl.program_id(0) == 0)
        def _():
            dkv_ref[...] = jnp.zeros_like(dkv_ref)

        for hd in range(MEM_HEADS):
            cols = slice(hd * MEM_HEAD_DIM, (hd + 1) * MEM_HEAD_DIM)
            vcols = slice(D_MODEL + hd * MEM_HEAD_DIM, D_MODEL + (hd + 1) * MEM_HEAD_DIM)
            q_h, k_h, dy_h = q_ref[:, cols], kv_ref[:, cols], dy_ref[:, cols]
            p = _mem_scores(q_h, k_h)
            dp = _dot(dy_h, kv_ref[:, vcols], NT)
            dkv_ref[:, vcols] += _dot(p, dy_h, TN)
            ds = p * (dp - jnp.sum(dp * p, axis=-1, keepdims=True)) * (MEM_HEAD_DIM ** -0.5)
            dq_ref[:, cols] = _dot(ds, k_h, NN).astype(dq_ref.dtype)
            dkv_ref[:, cols] += _dot(ds, q_h, TN)

    return pl.pallas_call(
        body, name="mem_bwd", grid=(t // rows,),
        in_specs=[pl.BlockSpec((rows, D_MODEL), lambda i: (i, 0)), pl.BlockSpec((m, 2 * D_MODEL), lambda i: (0, 0)),
                  pl.BlockSpec((rows, D_MODEL), lambda i: (i, 0))],
        out_specs=[pl.BlockSpec((rows, D_MODEL), lambda i: (i, 0)), pl.BlockSpec((m, 2 * D_MODEL), lambda i: (0, 0))],
        out_shape=[jax.ShapeDtypeStruct((t, D_MODEL), MXU_DTYPE), jax.ShapeDtypeStruct((m, 2 * D_MODEL), F32)],
        compiler_params=_params(("arbitrary",)),
    )(q, kv, dy)


def _merge_fwd(x, yssd, ylru, ymem, gl, w_bs, w_bl, w_bm, w_out, fg, tgt, rows=256):
    t = x.shape[0]
    rows = min(rows, t)
    D = D_MODEL

    def body(x_ref, ys_ref, yl_ref, ym_ref, gl_ref, wbs_ref, wbl_ref, wbm_ref, wo_ref, fg_ref, tgt_ref,
             ps_ref, pl_ref, pm_ref, mg_ref, dx2_ref, loss_ref, gfg_ref):
        @pl.when(pl.program_id(0) == 0)
        def _():
            loss_ref[...] = jnp.zeros_like(loss_ref)
            gfg_ref[...] = jnp.zeros_like(gfg_ref)

        ps = _dot(ys_ref[...], wbs_ref[...], NN)
        pl_ = _dot(yl_ref[...], wbl_ref[...], NN)
        pm = _dot(ym_ref[...], wbm_ref[...], NN)
        ps_ref[...] = ps
        pl_ref[...] = pl_
        pm_ref[...] = pm
        merged = (_sigmoid(gl_ref[:, 0:D]) * ps + _sigmoid(gl_ref[:, D:2 * D]) * pl_) + _sigmoid(gl_ref[:, 2 * D:3 * D]) * pm
        mg_ref[...] = merged.astype(mg_ref.dtype)
        x2 = x_ref[...] + _dot(merged, wo_ref[...], NN)
        r2 = lax.rsqrt(jnp.mean(x2 * x2, axis=-1, keepdims=True) + EPS)
        xn = x2 * r2
        fg = fg_ref[...]
        diff = xn * fg - tgt_ref[...]
        tile_loss = 0.5 * jnp.sum(jnp.mean(diff * diff, axis=-1, keepdims=True), axis=0, keepdims=True)
        loss_ref[...] += jnp.broadcast_to(tile_loss, loss_ref.shape)
        d_out = diff * (1.0 / D)
        gfg_ref[...] += jnp.sum(d_out * xn, axis=0, keepdims=True)
        dxn = d_out * fg
        dx2_ref[...] = r2 * (dxn - xn * jnp.mean(dxn * xn, axis=-1, keepdims=True))

    row = lambda w: pl.BlockSpec((rows, w), lambda i: (i, 0))
    const = lambda shape: pl.BlockSpec(shape, lambda i: (0,) * len(shape))
    return pl.pallas_call(
        body, name="merge_fwd", grid=(t // rows,),
        in_specs=[row(D), row(SSD_WIDTH), row(LRU_WIDTH), row(D), row(3 * D), const((SSD_WIDTH, D)), const((LRU_WIDTH, D)),
                  const((D, D)), const((D, D)), const((1, D)), row(D)],
        out_specs=[row(D), row(D), row(D), row(D), row(D), const((1, LANES)), const((1, D))],
        out_shape=[jax.ShapeDtypeStruct((t, D), F32), jax.ShapeDtypeStruct((t, D), F32), jax.ShapeDtypeStruct((t, D), F32),
                   jax.ShapeDtypeStruct((t, D), MXU_DTYPE), jax.ShapeDtypeStruct((t, D), F32),
                   jax.ShapeDtypeStruct((1, LANES), F32), jax.ShapeDtypeStruct((1, D), F32)],
        compiler_params=_params(("arbitrary",)),
    )(x, yssd, ylru, ymem, gl, w_bs, w_bl, w_bm, w_out, fg, tgt)


def _merge_bwd(dx2, gl, ps, pl_in, pm, w_bs, w_bl, w_bm, w_out, rows=256):
    t = dx2.shape[0]
    rows = min(rows, t)
    D = D_MODEL

    def body(dx2_ref, gl_ref, ps_ref, pl_ref, pm_ref, wbs_ref, wbl_ref, wbm_ref, wo_ref,
             dg_ref, dps_ref, dpl_ref, dpm_ref, dys_ref, dyl_ref, dym_ref):
        dm = _dot(dx2_ref[...], wo_ref[...], NT)
        for idx, (p_ref, dp_ref, w_ref, dy_ref) in enumerate(
                ((ps_ref, dps_ref, wbs_ref, dys_ref), (pl_ref, dpl_ref, wbl_ref, dyl_ref), (pm_ref, dpm_ref, wbm_ref, dym_ref))):
            gate = _sigmoid(gl_ref[:, idx * D:(idx + 1) * D])
            dg_ref[:, idx * D:(idx + 1) * D] = ((dm * p_ref[...]) * gate * (1.0 - gate)).astype(dg_ref.dtype)
            dp = dm * gate
            dp_ref[...] = dp.astype(dp_ref.dtype)
            dy_ref[...] = _dot(dp, w_ref[...], NT)

    row = lambda w: pl.BlockSpec((rows, w), lambda i: (i, 0))
    const = lambda shape: pl.BlockSpec(shape, lambda i: (0,) * len(shape))
    return pl.pallas_call(
        body, name="merge_bwd", grid=(t // rows,),
        in_specs=[row(D), row(3 * D), row(D), row(D), row(D), const((SSD_WIDTH, D)), const((LRU_WIDTH, D)),
                  const((D, D)), const((D, D))],
        out_specs=[row(3 * D), row(D), row(D), row(D), row(SSD_WIDTH), row(LRU_WIDTH), row(D)],
        out_shape=[jax.ShapeDtypeStruct((t, 3 * D), MXU_DTYPE), jax.ShapeDtypeStruct((t, D), MXU_DTYPE),
                   jax.ShapeDtypeStruct((t, D), MXU_DTYPE), jax.ShapeDtypeStruct((t, D), MXU_DTYPE),
                   jax.ShapeDtypeStruct((t, SSD_WIDTH), F32), jax.ShapeDtypeStruct((t, LRU_WIDTH), F32),
                   jax.ShapeDtypeStruct((t, D), F32)],
        compiler_params=_params(("parallel",)),
    )(dx2, gl, ps, pl_in, pm, w_bs, w_bl, w_bm, w_out)


def _mesh_place():
    x, y, c = lax.axis_index("x"), lax.axis_index("y"), lax.axis_index("c")
    return x, y, c, 4 * x + 2 * y + c


def _other_chips(x, y):
    return [(1 - x, y), (x, 1 - y), (1 - x, 1 - y)]


def _all_gather(arrs, name):
    n = len(arrs)

    def body(*refs):
        ins, outs = refs[:n], refs[n:2 * n]
        send_sems, recv_sems, local_sems = refs[2 * n:]
        x, y, c, me = _mesh_place()
        sibling = (x, y, 1 - c)
        chips = _other_chips(x, y)

        def slot(px, py, pc):
            return 4 * px + 2 * py + pc

        def copy(a, k, block, to, src=None):
            return pltpu.make_async_remote_copy(
                src_ref=outs[a].at[block] if src is None else src, dst_ref=outs[a].at[block],
                send_sem=send_sems.at[a, k], recv_sem=recv_sems.at[a, k], device_id=to, device_id_type=pl.DeviceIdType.MESH)

        local = [pltpu.make_async_copy(ins[a], outs[a].at[me], local_sems.at[a]) for a in range(n)]
        for cp in local:
            cp.start()
        sends = []
        for a in range(n):
            sends.append(copy(a, 0, me, sibling, src=ins[a]))
            for j, chip in enumerate(chips):
                sends.append(copy(a, 1 + j, me, (*chip, c), src=ins[a]))
        for cp in sends:
            cp.start()
        for j, chip in enumerate(chips):
            for a in range(n):
                copy(a, 1 + j, slot(*chip, c), sibling).wait_recv()
                passed = copy(a, 4 + j, slot(*chip, c), sibling)
                passed.start()
                sends.append(passed)
        for a in range(n):
            copy(a, 0, slot(x, y, 1 - c), sibling).wait_recv()
        for j, chip in enumerate(chips):
            for a in range(n):
                copy(a, 4 + j, slot(*chip, 1 - c), sibling).wait_recv()
        for cp in sends:
            cp.wait_send()
        for cp in local:
            cp.wait()

    any_spec = pl.BlockSpec(memory_space=pl.ANY)
    return pl.pallas_call(
        body, name=name, in_specs=[any_spec] * n, out_specs=[any_spec] * n,
        out_shape=[jax.ShapeDtypeStruct((N_DEV,) + a.shape, a.dtype) for a in arrs],
        scratch_shapes=[pltpu.SemaphoreType.DMA((n, 7)), pltpu.SemaphoreType.DMA((n, 7)), pltpu.SemaphoreType.DMA((n,))],
    )(*arrs)


N_CHIPS = 4


def _pair_exchange(parts):
    n = len(parts)

    def body(*refs):
        ins, outs = refs[:n], refs[n:2 * n]
        send_sems, recv_sems = refs[2 * n:]
        x, y, c, _ = _mesh_place()
        sibling = (x, y, 1 - c)
        sends = []
        for a in range(n):
            for q in range(N_CHIPS):
                cp = pltpu.make_async_remote_copy(src_ref=ins[a].at[q, 1 - c], dst_ref=outs[a].at[q], send_sem=send_sems.at[a, q],
                                                  recv_sem=recv_sems.at[a, q], device_id=sibling, device_id_type=pl.DeviceIdType.MESH)
                cp.start()
                sends.append(cp)
        for cp in sends:
            cp.wait_recv()
        for cp in sends:
            cp.wait_send()

    any_spec = pl.BlockSpec(memory_space=pl.ANY)
    return pl.pallas_call(
        body, name="grad_pair_exchange", in_specs=[any_spec] * n, out_specs=[any_spec] * n,
        out_shape=[jax.ShapeDtypeStruct((N_CHIPS,) + a.shape[2:], a.dtype) for a in parts],
        scratch_shapes=[pltpu.SemaphoreType.DMA((n, N_CHIPS)), pltpu.SemaphoreType.DMA((n, N_CHIPS))],
    )(*parts)


def _chip_sum(part, recv, core, name):
    _, _, r, c = part.shape
    rows = _row_tile(r, max(8, (2 << 20) // (c * 4) // 8 * 8))

    def body(core_ref, p_ref, r_ref, s_ref, t_ref):
        s = p_ref[...] + r_ref[...]
        s_ref[...] = s
        t_ref[...] = s.astype(t_ref.dtype)

    blk = pl.BlockSpec((None, rows, c), lambda q, i, core_ref: (q, i, 0))
    return pl.pallas_call(
        body, name=name,
        grid_spec=pltpu.PrefetchScalarGridSpec(
            num_scalar_prefetch=1, grid=(N_CHIPS, r // rows),
            in_specs=[pl.BlockSpec((None, None, rows, c), lambda q, i, core_ref: (q, core_ref[0], i, 0)), blk],
            out_specs=[blk, blk]),
        out_shape=[jax.ShapeDtypeStruct((N_CHIPS, r, c), F32), jax.ShapeDtypeStruct((N_CHIPS, r, c), GRAD_WIRE_DTYPE)],
        compiler_params=_params(("parallel", "parallel")),
    )(core, part, recv)


def _chip_exchange(sums):
    n = len(sums)

    def body(*refs):
        ins, outs = refs[:n], refs[n:2 * n]
        send_sems, recv_sems = refs[2 * n:]
        x, y, c, _ = _mesh_place()
        my_chip = 2 * x + y
        sends = []
        for a in range(n):
            for j, (px, py) in enumerate(_other_chips(x, y)):
                cp = pltpu.make_async_remote_copy(src_ref=ins[a].at[2 * px + py], dst_ref=outs[a].at[my_chip], send_sem=send_sems.at[a, j],
                                                  recv_sem=recv_sems.at[a, j], device_id=(px, py, c), device_id_type=pl.DeviceIdType.MESH)
                cp.start()
                sends.append(cp)
        for a in range(n):
            for j, (px, py) in enumerate(_other_chips(x, y)):
                pltpu.make_async_remote_copy(src_ref=ins[a].at[my_chip], dst_ref=outs[a].at[2 * px + py], send_sem=send_sems.at[a, j],
                                             recv_sem=recv_sems.at[a, j], device_id=(px, py, c),
                                             device_id_type=pl.DeviceIdType.MESH).wait_recv()
        for cp in sends:
            cp.wait_send()

    any_spec = pl.BlockSpec(memory_space=pl.ANY)
    return pl.pallas_call(
        body, name="grad_chip_exchange", in_specs=[any_spec] * n, out_specs=[any_spec] * n,
        out_shape=[jax.ShapeDtypeStruct(a.shape, a.dtype) for a in sums],
        scratch_shapes=[pltpu.SemaphoreType.DMA((n, 3)), pltpu.SemaphoreType.DMA((n, 3))],
    )(*sums)


def _row_tile(r, limit):
    if r <= limit:
        return r
    best = 8
    for cand in range(8, limit + 1, 8):
        if r % cand == 0:
            best = cand
    assert r % best == 0, r
    return best


def _adam_update(w, g, m, v):
    nm = ADAM_B1 * m + (1.0 - ADAM_B1) * g
    nv = ADAM_B2 * v + (1.0 - ADAM_B2) * (g * g)
    m_hat = nm / (1.0 - ADAM_B1 ** ADAM_STEP)
    v_hat = nv / (1.0 - ADAM_B2 ** ADAM_STEP)
    return -ADAM_LR * (m_hat / (jnp.sqrt(v_hat) + ADAM_EPS) + ADAM_WD * w), nm, nv


def _sum_adamw(own, recv, chip, w, m, v, name):
    _, r, c = own.shape
    rows = _row_tile(r, max(8, (1 << 20) // (c * 4) // 8 * 8))

    def body(chip_ref, o_ref, r1_ref, r2_ref, r3_ref, w_ref, m_ref, v_ref, g_ref, d_ref, nm_ref, nv_ref):
        g = ((o_ref[...] + r1_ref[...].astype(F32)) + r2_ref[...].astype(F32)) + r3_ref[...].astype(F32)
        g_ref[...] = g
        d_ref[...], nm_ref[...], nv_ref[...] = _adam_update(w_ref[...], g, m_ref[...], v_ref[...])

    def slot(k):
        return pl.BlockSpec((None, rows, c), lambda i, chip_ref: ((chip_ref[0] + k) % N_CHIPS, i, 0))

    spec = pl.BlockSpec((None, rows, c), lambda i, chip_ref: (0, i, 0))
    shape = jax.ShapeDtypeStruct((1, r, c), F32)
    return pl.pallas_call(
        body, name=name,
        grid_spec=pltpu.PrefetchScalarGridSpec(
            num_scalar_prefetch=1, grid=(r // rows,),
            in_specs=[slot(0), slot(1), slot(2), slot(3), spec, spec, spec], out_specs=[spec] * 4),
        out_shape=[shape] * 4,
        compiler_params=_params(("parallel",)),
    )(chip, own, recv, recv, recv, w, m, v)


def _small_adamw(parts, ws, ms, vs):
    n = len(parts)

    def body(*refs):
        p_refs, w_refs, m_refs, v_refs = refs[:n], refs[n:2 * n], refs[2 * n:3 * n], refs[3 * n:4 * n]
        outs = refs[4 * n:]
        for i in range(n):
            g = p_refs[i][0]
            for k in range(1, N_DEV):
                g = g + p_refs[i][k]
            outs[i][...] = g
            outs[n + i][...], outs[2 * n + i][...], outs[3 * n + i][...] = _adam_update(
                w_refs[i][...], g, m_refs[i][...], v_refs[i][...])

    vmem = pl.BlockSpec(memory_space=pltpu.VMEM)
    shapes = [jax.ShapeDtypeStruct(w.shape, F32) for w in ws]
    res = pl.pallas_call(
        body, name="adamw_small", in_specs=[vmem] * (4 * n), out_specs=[vmem] * (4 * n), out_shape=shapes * 4,
        compiler_params=pltpu.CompilerParams(vmem_limit_bytes=VMEM_LIMIT),
    )(*parts, *ws, *ms, *vs)
    return res[:n], res[n:2 * n], res[2 * n:3 * n], res[3 * n:]


def _pack(arrs, dtype, row_multiple):
    flat = jnp.concatenate([a.reshape(-1).astype(dtype) for a in arrs])
    unit = LANES * row_multiple
    padded = -(-flat.shape[0] // unit) * unit
    return jnp.pad(flat, (0, padded - flat.shape[0])).reshape(-1, LANES)


def _unpack(packed, shapes, lead=()):
    flat = packed.reshape(lead + (-1,))
    out, off = [], 0
    for shp in shapes:
        n = math.prod(shp)
        out.append(flat[..., off:off + n].reshape(lead + tuple(shp)))
        off += n
    return out


def _gather_cols(g, lo, hi):
    width = g.shape[2]
    pieces = []
    for s in range(N_DEV):
        a, e = max(lo, s * width), min(hi, (s + 1) * width)
        if a < e:
            pieces.append(g[s, :, a - s * width:e - s * width])
    return pieces[0] if len(pieces) == 1 else jnp.concatenate(pieces, axis=1)


def _scatter_cols(segs, width):
    slots = []
    for k in range(N_DEV):
        lo, hi = k * width, (k + 1) * width
        pieces = []
        for arr, s_lo, s_hi in segs:
            a, e = max(lo, s_lo), min(hi, s_hi)
            if a < e:
                pieces.append(arr[:, a - s_lo:e - s_lo])
        slots.append(pieces[0] if len(pieces) == 1 else jnp.concatenate(pieces, axis=1))
    return jnp.stack(slots)


def _block_diag_groups(w):
    w4 = w.reshape(LRU_NGROUPS, 4, LRU_BLOCK, LRU_BLOCK)
    eye = jnp.eye(4, dtype=w.dtype)
    return jnp.einsum("gaij,ab->gaibj", w4, eye).reshape(LRU_NGROUPS, LRU_GROUP, LRU_GROUP)


def _block_diag_extract(wg):
    w5 = wg.reshape(LRU_NGROUPS, 4, LRU_BLOCK, 4, LRU_BLOCK)
    idx = jnp.arange(4)
    return w5[:, idx, :, idx, :].transpose(1, 0, 2, 3).reshape(LRU_BLOCKS, LRU_BLOCK, LRU_BLOCK)


BIG = ("w_in", "w_kv", "w_br_ssd", "w_br_lru", "w_br_mem", "w_out")
SMALL_SHARDED = ("ssd_conv_w", "ssd_norm_g", "lru_conv_w")
REPLICATED = ("norm_g", "ssd_conv_b", "ssd_dt_bias", "ssd_a_log", "ssd_d", "lru_conv_b", "lru_w_a", "lru_b_a",
              "lru_w_x", "lru_b_x", "lru_lambda", "mem_norm_g", "final_g")
WEIGHTS = ("norm_g", "w_in", "ssd_conv_w", "ssd_conv_b", "ssd_dt_bias", "ssd_a_log", "ssd_d", "ssd_norm_g", "lru_conv_w",
           "lru_conv_b", "lru_w_a", "lru_b_a", "lru_w_x", "lru_b_x", "lru_lambda", "mem_norm_g", "w_kv", "w_br_ssd",
           "w_br_lru", "w_br_mem", "w_out", "final_g")


def kernel(x, mem, norm_g, w_in, ssd_conv_w, ssd_conv_b, ssd_dt_bias, ssd_a_log, ssd_d, ssd_norm_g, lru_conv_w, lru_conv_b, lru_w_a, lru_b_a, lru_w_x, lru_b_x, lru_lambda, mem_norm_g, w_kv, w_br_ssd, w_br_lru, w_br_mem, w_out, final_g, loss_target, m_norm_g, m_w_in, m_ssd_conv_w, m_ssd_conv_b, m_ssd_dt_bias, m_ssd_a_log, m_ssd_d, m_ssd_norm_g, m_lru_conv_w, m_lru_conv_b, m_lru_w_a, m_lru_b_a, m_lru_w_x, m_lru_b_x, m_lru_lambda, m_mem_norm_g, m_w_kv, m_w_br_ssd, m_w_br_lru, m_w_br_mem, m_w_out, m_final_g, v_norm_g, v_w_in, v_ssd_conv_w, v_ssd_conv_b, v_ssd_dt_bias, v_ssd_a_log, v_ssd_d, v_ssd_norm_g, v_lru_conv_w, v_lru_conv_b, v_lru_w_a, v_lru_b_a, v_lru_w_x, v_lru_b_x, v_lru_lambda, v_mem_norm_g, v_w_kv, v_w_br_ssd, v_w_br_lru, v_w_br_mem, v_w_out, v_final_g):
    env = dict(locals())
    W = {n: env[n] for n in WEIGHTS}
    M = {n: env["m_" + n] for n in WEIGHTS}
    V = {n: env["v_" + n] for n in WEIGHTS}
    me = 4 * lax.axis_index("x") + 2 * lax.axis_index("y") + lax.axis_index("c")
    t = x.shape[1]
    xt = x[0]
    memt = mem[0]
    tgt = loss_target[0]

    small_shapes = [W[n].shape for n in SMALL_SHARDED]
    gathered = _all_gather([W[n][0].astype(MXU_DTYPE) for n in BIG] + [_pack([W[n] for n in SMALL_SHARDED], F32, 8)],
                           "weights_all_gather")
    g_in, g_kv, g_bs, g_bl, g_bm, g_out, gs = gathered
    g_cw, g_ng, g_lcw = _unpack(gs, small_shapes, (N_DEV,))
    cols = lambda a: jnp.moveaxis(a[:, 0], 0, -2).reshape(a.shape[2:-1] + (-1,))
    rows_ = lambda a: a.reshape((-1,) + a.shape[2:])
    w_bs_f, w_bl_f, w_bm_f, w_out_f = rows_(g_bs), rows_(g_bl), rows_(g_bm), rows_(g_out)
    conv_w_f, ssd_ng_f, lru_cw_f = cols(g_cw), cols(g_ng), cols(g_lcw)
    b = SEG_BOUNDS
    w_kv_f = _gather_cols(g_kv, 0, 2 * D_MODEL)
    w_ssd, w_lru, w_q, w_g = (_gather_cols(g_in, b[0], b[1]), _gather_cols(g_in, b[2], b[3]),
                              _gather_cols(g_in, b[3], b[4]), _gather_cols(g_in, b[4], b[5]))
    w_dt = jnp.pad(_gather_cols(g_in, b[1], b[2]), ((0, 0), (0, DT_PAD - SSD_HEADS)))

    pad_heads = lambda a: jnp.pad(a, ((0, 0), (0, LANES - SSD_HEADS)))
    dtb, alog = pad_heads(ssd_dt_bias), pad_heads(ssd_a_log)
    d_row = jnp.repeat(ssd_d, SSD_HEAD_DIM, axis=1)
    ng_row = ssd_ng_f.reshape(1, SSD_WIDTH)
    wa_g, wx_g = _block_diag_groups(lru_w_a[0]), _block_diag_groups(lru_w_x[0])
    ba, bx = lru_b_a.reshape(1, LRU_WIDTH), lru_b_x.reshape(1, LRU_WIDTH)
    fg = final_g.reshape(1, D_MODEL)

    h = _rms_fwd(xt, norm_g, "norm_fwd")
    proj_ssd = _matmul(h, w_ssd, "nn", "proj_ssd", tk=D_MODEL)
    proj_lru = _matmul(h, w_lru, "nn", "proj_lru", tk=D_MODEL)
    proj_q = _matmul(h, w_q, "nn", "proj_q", tk=D_MODEL)
    proj_g = _matmul(h, w_g, "nn", "proj_g", tk=D_MODEL)
    proj_dt = _matmul(h, w_dt, "nn", "proj_dt", tk=D_MODEL)
    mem_n = _rms_fwd(memt, mem_norm_g, "mem_norm_fwd")
    kv = _matmul(mem_n, w_kv_f, "nn", "mem_kv", tk=D_MODEL)
    yssd, y_scan, states = _ssd_fwd(proj_ssd, proj_dt, conv_w_f, ssd_conv_b, dtb, alog, d_row, ng_row)
    ylru, h_lru = _lru_fwd(proj_lru, lru_cw_f, lru_conv_b, wa_g, wx_g, ba, bx, lru_lambda)
    ymem = _mem_fwd(proj_q, kv)
    ps, pl_, pm, merged, dx2, loss_vec, g_fg = _merge_fwd(xt, yssd, ylru, ymem, proj_g, w_bs_f, w_bl_f, w_bm_f, w_out_f, fg, tgt)

    d_g, dps, dpl, dpm, dyssd, dylru, dymem = _merge_bwd(dx2, proj_g, ps, pl_, pm, w_bs_f, w_bl_f, w_bm_f, w_out_f)
    gw_out = _matmul(merged, dx2, "tn", "grad_w_out", tk=1024)
    gw_bs = _matmul(yssd, dps, "tn", "grad_w_br_ssd", tk=1024)
    gw_bl = _matmul(ylru, dpl, "tn", "grad_w_br_lru", tm=LRU_WIDTH, tk=1024)
    gw_bm = _matmul(ymem, dpm, "tn", "grad_w_br_mem", tk=1024)
    d_q, d_kv = _mem_bwd(proj_q, kv, dymem)
    gw_kv = _matmul(mem_n, d_kv, "tn", "grad_w_kv", tk=memt.shape[0])
    d_memn = _matmul(d_kv, w_kv_f, "nt", "d_mem_n", tk=1024)
    _, g_memng = _rms_bwd(memt, d_memn, None, mem_norm_g, "mem_norm_bwd")
    d_lru, gl_cw, gl_cb, g_ba, g_bx, g_lam, gwa_g, gwx_g = _lru_bwd(proj_lru, h_lru, dylru, lru_cw_f, lru_conv_b, wa_g, wx_g, ba, bx, lru_lambda)
    d_ssd, d_dt, gs_cw, gs_cb, g_dtb, g_alog, g_dch, g_ngrow = _ssd_bwd(proj_ssd, proj_dt, y_scan, states, dyssd, conv_w_f, ssd_conv_b, dtb, alog, d_row, ng_row)
    dh = _matmul(d_ssd, w_ssd, "nt", "dh_ssd", tk=1024)
    dh = _matmul(d_lru, w_lru, "nt", "dh_lru", tk=1024, acc_in=dh)
    dh = _matmul(d_q, w_q, "nt", "dh_q", tk=1024, acc_in=dh)
    dh = _matmul(d_g, w_g, "nt", "dh_g", tk=1024, acc_in=dh)
    dh = _matmul(d_dt, w_dt, "nt", "dh_dt", tk=DT_PAD, acc_in=dh)
    gw_ssd = _matmul(h, d_ssd, "tn", "grad_w_in_ssd", tk=1024)
    gw_lru = _matmul(h, d_lru, "tn", "grad_w_in_lru", tk=1024)
    gw_q = _matmul(h, d_q, "tn", "grad_w_in_q", tk=1024)
    gw_g = _matmul(h, d_g, "tn", "grad_w_in_g", tk=1024)
    gw_dt = _matmul(h, d_dt, "tn", "grad_w_in_dt", tk=1024)
    grad_x, g_normg = _rms_bwd(xt, dh, dx2, norm_g, "norm_bwd")

    split_rows = lambda a: a.reshape((N_DEV, -1) + a.shape[1:])
    in_segs = [(gw_ssd, b[0], b[1]), (gw_dt, b[1], b[2]), (gw_lru, b[2], b[3]), (gw_q, b[3], b[4]), (gw_g, b[4], b[5])]
    big_send = [_scatter_cols(in_segs, IN_WIDTH // N_DEV), _scatter_cols([(gw_kv, 0, 2 * D_MODEL)], 2 * D_MODEL // N_DEV),
                split_rows(gw_bs), split_rows(gw_bl), split_rows(gw_bm), split_rows(gw_out)]

    small_grads = {
        "norm_g": g_normg, "ssd_conv_w": gs_cw, "ssd_conv_b": gs_cb, "ssd_dt_bias": g_dtb[:, :SSD_HEADS],
        "ssd_a_log": g_alog[:, :SSD_HEADS], "ssd_d": jnp.sum(g_dch.reshape(SSD_HEADS, SSD_HEAD_DIM), axis=1).reshape(1, SSD_HEADS),
        "ssd_norm_g": g_ngrow.reshape(SSD_GROUPS, -1), "lru_conv_w": gl_cw, "lru_conv_b": gl_cb,
        "lru_w_a": _block_diag_extract(gwa_g), "lru_b_a": g_ba, "lru_w_x": _block_diag_extract(gwx_g), "lru_b_x": g_bx,
        "lru_lambda": g_lam, "mem_norm_g": g_memng, "final_g": g_fg,
    }
    small_all = REPLICATED + SMALL_SHARDED

    def small_shape(n, shards):
        shp = W[n].shape[1:] if W[n].ndim > 2 else (1, W[n].shape[-1])
        return shp[:-1] + (shp[-1] * shards,)

    small_recv = _all_gather([small_grads[n].reshape(small_shape(n, N_DEV if n in SMALL_SHARDED else 1)) for n in small_all],
                             "small_grads_all_gather")

    core = lax.axis_index("c").astype(jnp.int32).reshape(1)
    chip = (2 * lax.axis_index("x") + lax.axis_index("y")).astype(jnp.int32).reshape(1)
    by_chip = [a.reshape((N_CHIPS, 2) + a.shape[1:]) for a in big_send]
    from_sibling = _pair_exchange(by_chip)
    chip_sums = [_chip_sum(p, r, core, "chip_sum_" + n) for n, p, r in zip(BIG, by_chip, from_sibling)]
    from_chips = _chip_exchange([s16 for _, s16 in chip_sums])

    grads, delta, new_m, new_v = {}, {}, {}, {}
    for n, (s32, _), recv in zip(BIG, chip_sums, from_chips):
        res = _sum_adamw(s32, recv, chip, W[n], M[n], V[n], "adamw_" + n)
        for dst, a in zip((grads, delta, new_m, new_v), res):
            dst[n] = a

    parts = []
    for n, a in zip(small_all, small_recv):
        if n in SMALL_SHARDED:
            width = W[n].shape[-1]
            a = lax.dynamic_slice_in_dim(a, me * width, width, axis=a.ndim - 1)
        parts.append(a)
    canon = lambda d: [d[n].reshape(small_shape(n, 1)) for n in small_all]
    for dst, res in zip((grads, delta, new_m, new_v), _small_adamw(parts, canon(W), canon(M), canon(V))):
        for n, a in zip(small_all, res):
            dst[n] = a.reshape(W[n].shape)

    loss = lax.psum(loss_vec[0, 0], ("x", "y", "c"))
    return (loss, grad_x[None], *[grads[n] for n in WEIGHTS], *[delta[n] for n in WEIGHTS],
            *[new_m[n] for n in WEIGHTS], *[new_v[n] for n in WEIGHTS])
```

```python
import functools
import math

import jax
import jax.numpy as jnp
from jax import lax
from jax.experimental import pallas as pl
from jax.experimental.pallas import tpu as pltpu

F32 = jnp.float32
MXU_DTYPE = jnp.bfloat16
GRAD_WIRE_DTYPE = jnp.bfloat16

D_MODEL = 1024
EPS = 1e-6
CONV_WIDTH = 4
SSD_WIDTH = 2048
SSD_HEAD_DIM = 64
SSD_HEADS = 32
SSD_GROUPS = 4
SSD_STATE = 128
SSD_CHUNK = 128
SSD_CONV_CH = SSD_WIDTH + 2 * SSD_GROUPS * SSD_STATE
SSD_PAIRS = SSD_HEADS // 2
PAIRS_PER_GROUP = SSD_PAIRS // SSD_GROUPS
LRU_WIDTH = 1536
LRU_BLOCKS = 16
LRU_BLOCK = 96
LRU_GROUP = 4 * LRU_BLOCK
LRU_NGROUPS = LRU_WIDTH // LRU_GROUP
LRU_C = 8.0
LRU_ROWS = 256
MEM_HEADS = 4
MEM_HEAD_DIM = 256
IN_WIDTH = 12320
N_DEV = 8
LANES = 128
SSD_SEG = SSD_WIDTH + SSD_CONV_CH
DT_PAD = LANES
SEG_BOUNDS = (0, 5120, 5152, 8224, 9248, 12320)

ADAM_LR = 0.001
ADAM_B1 = 0.9
ADAM_B2 = 0.999
ADAM_EPS = 1e-08
ADAM_WD = 0.01
ADAM_STEP = 10

VMEM_LIMIT = 56 * 1024 * 1024

NN = (((1,), (0,)), ((), ()))
NT = (((1,), (1,)), ((), ()))
TN = (((0,), (0,)), ((), ()))


def _dot(a, b, dims):
    return lax.dot_general(a.astype(MXU_DTYPE), b.astype(MXU_DTYPE), dims, preferred_element_type=F32)


def _sigmoid(x):
    return 0.5 * jnp.tanh(0.5 * x) + 0.5


def _log1p(e):
    u = 1.0 + e
    return jnp.where(u == 1.0, e, jnp.log(u) * (e / jnp.where(u == 1.0, 1.0, u - 1.0)))


def _softplus(x):
    return jnp.maximum(x, 0.0) + _log1p(jnp.exp(-jnp.abs(x)))


def _params(semantics):
    return pltpu.CompilerParams(dimension_semantics=semantics, vmem_limit_bytes=VMEM_LIMIT)


def _shift_down(cur, halo8, k):
    rolled = pltpu.roll(cur, k, 0)
    row8 = lax.broadcasted_iota(jnp.int32, halo8.shape, 0)
    top = jnp.where(row8 >= k, rolled[0:8], pltpu.roll(halo8, k, 0))
    return jnp.concatenate([top, rolled[8:]], axis=0)


def _shift_up(cur, next8, k):
    rows = cur.shape[0]
    rolled = pltpu.roll(cur, rows - k, 0)
    row8 = lax.broadcasted_iota(jnp.int32, next8.shape, 0)
    bot = jnp.where(row8 < 8 - k, rolled[rows - 8:rows], pltpu.roll(next8, 8 - k, 0))
    return jnp.concatenate([rolled[:rows - 8], bot], axis=0)


def _causal_conv(raw, halo8, w, b):
    acc = raw * w[3:4, :] + b
    for k in range(1, CONV_WIDTH):
        acc = acc + _shift_down(raw, halo8, k) * w[3 - k:4 - k, :]
    return acc


def _conv_backward(dco, next8, raw, w):
    d_raw = dco * w[3:4, :]
    gw = [None] * CONV_WIDTH
    gw[3] = jnp.sum(dco * raw, axis=0, keepdims=True)
    for j in range(1, CONV_WIDTH):
        up = _shift_up(dco, next8, j)
        d_raw = d_raw + up * w[3 - j:4 - j, :]
        gw[3 - j] = jnp.sum(up * raw, axis=0, keepdims=True)
    gb = jnp.sum(dco, axis=0, keepdims=True)
    return d_raw, gw, gb


def _cumsum_rows(v):
    rows = v.shape[0]
    row = lax.broadcasted_iota(jnp.int32, v.shape, 0)
    s = 1
    while s < rows:
        v = v + jnp.where(row >= s, pltpu.roll(v, s, 0), 0.0)
        s *= 2
    return v


def _rev_cumsum_rows(v):
    rows = v.shape[0]
    row = lax.broadcasted_iota(jnp.int32, v.shape, 0)
    s = 1
    while s < rows:
        v = v + jnp.where(row < rows - s, pltpu.roll(v, rows - s, 0), 0.0)
        s *= 2
    return v


def _matmul(a, b, mode, name, out_dtype=F32, tm=1024, tn=1024, tk=512, acc_in=None):
    if mode == "nn":
        (m, kk), n = a.shape, b.shape[1]
    elif mode == "nt":
        (m, kk), n = a.shape, b.shape[0]
    else:
        (kk, m), n = a.shape, b.shape[1]
    tm, tn, tk = min(tm, m), min(tn, n), min(tk, kk)
    assert m % tm == 0 and n % tn == 0 and kk % tk == 0, (name, a.shape, b.shape)
    nk = kk // tk
    dims = {"nn": NN, "nt": NT, "tn": TN}[mode]
    a_spec = pl.BlockSpec((tk, tm), lambda i, j, k: (k, i)) if mode == "tn" else pl.BlockSpec((tm, tk), lambda i, j, k: (i, k))
    b_spec = pl.BlockSpec((tn, tk), lambda i, j, k: (j, k)) if mode == "nt" else pl.BlockSpec((tk, tn), lambda i, j, k: (k, j))
    o_spec = pl.BlockSpec((tm, tn), lambda i, j, k: (i, j))
    has_acc = acc_in is not None

    def body_single(*refs):
        if has_acc:
            a_ref, b_ref, c_ref, o_ref = refs
            o_ref[...] = (c_ref[...].astype(F32) + _dot(a_ref[...], b_ref[...], dims)).astype(o_ref.dtype)
        else:
            a_ref, b_ref, o_ref = refs
            o_ref[...] = _dot(a_ref[...], b_ref[...], dims).astype(o_ref.dtype)

    def body(*refs):
        if has_acc:
            a_ref, b_ref, c_ref, o_ref, acc_ref = refs
        else:
            a_ref, b_ref, o_ref, acc_ref = refs
        k = pl.program_id(2)

        @pl.when(k == 0)
        def _():
            acc_ref[...] = c_ref[...].astype(F32) if has_acc else jnp.zeros_like(acc_ref)

        acc_ref[...] += _dot(a_ref[...], b_ref[...], dims)

        @pl.when(k == nk - 1)
        def _():
            o_ref[...] = acc_ref[...].astype(o_ref.dtype)

    args = (a, b) + ((acc_in,) if has_acc else ())
    in_specs = [a_spec, b_spec] + ([o_spec] if has_acc else [])
    return pl.pallas_call(
        body_single if nk == 1 else body, name=name, grid=(m // tm, n // tn, nk), in_specs=in_specs, out_specs=o_spec,
        out_shape=jax.ShapeDtypeStruct((m, n), out_dtype),
        scratch_shapes=[] if nk == 1 else [pltpu.VMEM((tm, tn), F32)],
        compiler_params=_params(("parallel", "parallel", "arbitrary")),
    )(*args)


def _rms_fwd(x, g, name, rows=512):
    t, d = x.shape
    rows = min(rows, t)

    def body(x_ref, g_ref, h_ref):
        xv = x_ref[...]
        r = lax.rsqrt(jnp.mean(xv * xv, axis=-1, keepdims=True) + EPS)
        h_ref[...] = ((xv * r) * g_ref[...]).astype(h_ref.dtype)

    return pl.pallas_call(
        body, name=name, grid=(t // rows,),
        in_specs=[pl.BlockSpec((rows, d), lambda i: (i, 0)), pl.BlockSpec((1, d), lambda i: (0, 0))],
        out_specs=pl.BlockSpec((rows, d), lambda i: (i, 0)),
        out_shape=jax.ShapeDtypeStruct((t, d), MXU_DTYPE),
        compiler_params=_params(("parallel",)),
    )(x, g)


def _rms_bwd(x, dh, dres, g, name, rows=512):
    t, d = x.shape
    rows = min(rows, t)
    has_res = dres is not None

    def body(*refs):
        if has_res:
            x_ref, dh_ref, dr_ref, g_ref, dx_ref, gg_ref = refs
        else:
            x_ref, dh_ref, g_ref, dx_ref, gg_ref = refs

        @pl.when(pl.program_id(0) == 0)
        def _():
            gg_ref[...] = jnp.zeros_like(gg_ref)

        xv = x_ref[...]
        dhv = dh_ref[...]
        r = lax.rsqrt(jnp.mean(xv * xv, axis=-1, keepdims=True) + EPS)
        n = xv * r
        dn = dhv * g_ref[...]
        dx = r * (dn - n * jnp.mean(dn * n, axis=-1, keepdims=True))
        if has_res:
            dx = dx + dr_ref[...]
        dx_ref[...] = dx
        gg_ref[...] += jnp.sum(dhv * n, axis=0, keepdims=True)

    row_spec = pl.BlockSpec((rows, d), lambda i: (i, 0))
    vec_spec = pl.BlockSpec((1, d), lambda i: (0, 0))
    args = (x, dh) + ((dres,) if has_res else ()) + (g,)
    return pl.pallas_call(
        body, name=name, grid=(t // rows,),
        in_specs=[row_spec, row_spec] + ([row_spec] if has_res else []) + [vec_spec],
        out_specs=[row_spec, vec_spec],
        out_shape=[jax.ShapeDtypeStruct((t, d), F32), jax.ShapeDtypeStruct((1, d), F32)],
        compiler_params=_params(("arbitrary",)),
    )(*args)


def _pair_select(lo, m, h0):
    return jnp.where(lo, m[:, h0:h0 + 1], m[:, h0 + 1:h0 + 2])


def _halves(lo, v):
    return (jnp.sum(jnp.where(lo, v, 0.0), axis=1, keepdims=True),
            jnp.sum(jnp.where(lo, 0.0, v), axis=1, keepdims=True))


def _ssd_common(dt_raw, dtb, alog):
    dt = _softplus(dt_raw + dtb)
    aneg = -jnp.exp(alog)
    a_cs = _cumsum_rows(dt * aneg)
    return dt, aneg, a_cs, a_cs.T


def _ssd_specs(nc, rev):
    cidx = (lambda c: nc - 1 - c) if rev else (lambda c: c)
    L = SSD_CHUNK
    return dict(
        z=pl.BlockSpec((L, SSD_WIDTH), lambda c: (cidx(c), 0)),
        xr=pl.BlockSpec((L, SSD_WIDTH), lambda c: (cidx(c), 1)),
        br=pl.BlockSpec((L, 512), lambda c: (cidx(c), 8)),
        cr=pl.BlockSpec((L, 512), lambda c: (cidx(c), 9)),
        dt=pl.BlockSpec((L, DT_PAD), lambda c: (cidx(c), 0)),
        cwx=pl.BlockSpec((CONV_WIDTH, SSD_WIDTH), lambda c: (0, 0)),
        cwb=pl.BlockSpec((CONV_WIDTH, 512), lambda c: (0, 4)),
        cwc=pl.BlockSpec((CONV_WIDTH, 512), lambda c: (0, 5)),
        cbx=pl.BlockSpec((1, SSD_WIDTH), lambda c: (0, 0)),
        cbb=pl.BlockSpec((1, 512), lambda c: (0, 4)),
        cbc=pl.BlockSpec((1, 512), lambda c: (0, 5)),
        vec128=pl.BlockSpec((1, LANES), lambda c: (0, 0)),
        vecw=pl.BlockSpec((1, SSD_WIDTH), lambda c: (0, 0)),
        wide=pl.BlockSpec((L, SSD_WIDTH), lambda c: (cidx(c), 0)),
        states=pl.BlockSpec((1, SSD_PAIRS, 128, SSD_STATE), lambda c: (cidx(c), 0, 0, 0)),
    )


def _ssd_fwd(proj_ssd, dt_p, conv_w, conv_b, dtb, alog, d_row, ng_row):
    t = proj_ssd.shape[0]
    nc = t // SSD_CHUNK
    L = SSD_CHUNK
    sp = _ssd_specs(nc, False)

    def body(z_ref, xr_ref, br_ref, cr_ref, dt_ref, cwx_ref, cwb_ref, cwc_ref, cbx_ref, cbb_ref, cbc_ref,
             dtb_ref, alog_ref, d_ref, ng_ref, yssd_ref, y_ref, st_ref,
             hx_ref, hb_ref, hc_ref, state_ref, yacc_ref):
        @pl.when(pl.program_id(0) == 0)
        def _():
            hx_ref[...] = jnp.zeros_like(hx_ref)
            hb_ref[...] = jnp.zeros_like(hb_ref)
            hc_ref[...] = jnp.zeros_like(hc_ref)
            state_ref[...] = jnp.zeros_like(state_ref)

        xr, br, cr = xr_ref[...], br_ref[...], cr_ref[...]
        px = _causal_conv(xr, hx_ref[...], cwx_ref[...], cbx_ref[...])
        pb = _causal_conv(br, hb_ref[...], cwb_ref[...], cbb_ref[...])
        pc = _causal_conv(cr, hc_ref[...], cwc_ref[...], cbc_ref[...])
        hx_ref[...] = xr[L - 8:L, :]
        hb_ref[...] = br[L - 8:L, :]
        hc_ref[...] = cr[L - 8:L, :]
        xs = px * _sigmoid(px)
        bm = pb * _sigmoid(pb)
        cm = pc * _sigmoid(pc)

        dt, _, a_cs, a_t = _ssd_common(dt_ref[...], dtb_ref[...], alog_ref[...])
        exp_a = jnp.exp(a_cs)
        a_last = a_cs[L - 1:L, :]
        dte = jnp.exp(a_last - a_cs)
        dec = jnp.exp(a_last)

        lane = lax.broadcasted_iota(jnp.int32, (L, LANES), 1)
        sub = lax.broadcasted_iota(jnp.int32, (L, LANES), 0)
        lo = lane < SSD_HEAD_DIM
        causal = sub >= lane
        top = sub < SSD_HEAD_DIM

        for g in range(SSD_GROUPS):
            b_g = bm[:, g * SSD_STATE:(g + 1) * SSD_STATE]
            c_g = cm[:, g * SSD_STATE:(g + 1) * SSD_STATE]
            cb = _dot(c_g, b_g, NT)
            for jj in range(PAIRS_PER_GROUP):
                j = g * PAIRS_PER_GROUP + jj
                h0 = 2 * j
                cols = slice(j * LANES, (j + 1) * LANES)
                xs_p = xs[:, cols]
                xdt = xs_p * _pair_select(lo, dt, h0)
                g0 = jnp.where(causal, jnp.exp(a_cs[:, h0:h0 + 1] - a_t[h0:h0 + 1, :]), 0.0) * cb
                g1 = jnp.where(causal, jnp.exp(a_cs[:, h0 + 1:h0 + 2] - a_t[h0 + 1:h0 + 2, :]), 0.0) * cb
                lhs = jnp.concatenate([g0, g1], axis=1)
                rhs = jnp.concatenate([jnp.where(lo, xdt, 0.0), jnp.where(lo, 0.0, xdt)], axis=0)
                y_diag = _dot(lhs, rhs, NN)
                h_p = state_ref[j]
                st_ref[0, j] = h_p
                y_off = _dot(c_g, h_p, NT) * _pair_select(lo, exp_a, h0)
                s_new = _dot(xdt * _pair_select(lo, dte, h0), b_g, TN)
                dec_rows = jnp.where(top, dec[:, h0:h0 + 1], dec[:, h0 + 1:h0 + 2])
                state_ref[j] = h_p * dec_rows + s_new
                yacc_ref[:, cols] = (y_diag + y_off) + xs_p * d_ref[:, cols]

        y = yacc_ref[...]
        y_ref[...] = y
        zz = z_ref[...]
        y2 = y * (zz * _sigmoid(zz))
        gw = SSD_WIDTH // SSD_GROUPS
        for g in range(SSD_GROUPS):
            seg = y2[:, g * gw:(g + 1) * gw]
            r = lax.rsqrt(jnp.mean(seg * seg, axis=-1, keepdims=True) + EPS)
            yssd_ref[:, g * gw:(g + 1) * gw] = ((seg * r) * ng_ref[:, g * gw:(g + 1) * gw]).astype(yssd_ref.dtype)

    return pl.pallas_call(
        body, name="ssd_fwd", grid=(nc,),
        in_specs=[sp["z"], sp["xr"], sp["br"], sp["cr"], sp["dt"], sp["cwx"], sp["cwb"], sp["cwc"],
                  sp["cbx"], sp["cbb"], sp["cbc"], sp["vec128"], sp["vec128"], sp["vecw"], sp["vecw"]],
        out_specs=[sp["wide"], sp["wide"], sp["states"]],
        out_shape=[jax.ShapeDtypeStruct((t, SSD_WIDTH), MXU_DTYPE), jax.ShapeDtypeStruct((t, SSD_WIDTH), F32),
                   jax.ShapeDtypeStruct((nc, SSD_PAIRS, 128, SSD_STATE), F32)],
        scratch_shapes=[pltpu.VMEM((8, SSD_WIDTH), F32), pltpu.VMEM((8, 512), F32), pltpu.VMEM((8, 512), F32),
                        pltpu.VMEM((SSD_PAIRS, 128, SSD_STATE), F32), pltpu.VMEM((L, SSD_WIDTH), F32)],
        compiler_params=_params(("arbitrary",)),
    )(proj_ssd, proj_ssd, proj_ssd, proj_ssd, dt_p, conv_w, conv_w, conv_w, conv_b, conv_b, conv_b,
      dtb, alog, d_row, ng_row)


def _ssd_bwd(proj_ssd, dt_p, y, states, dyssd, conv_w, conv_b, dtb, alog, d_row, ng_row):
    t = proj_ssd.shape[0]
    nc = t // SSD_CHUNK
    L = SSD_CHUNK
    sp = _ssd_specs(nc, True)
    groups8 = L // 8

    def halo_spec(width, col):
        return pl.BlockSpec((8, width), lambda c: (jnp.maximum((nc - 1 - c) * groups8 - 1, 0), col))

    def body(z_ref, xr_ref, br_ref, cr_ref, hx_ref, hb_ref, hc_ref, dt_ref, y_ref, st_ref, dy_ref,
             cwx_ref, cwb_ref, cwc_ref, cbx_ref, cbb_ref, cbc_ref, dtb_ref, alog_ref, d_ref, ng_ref,
             dssd_ref, ddt_ref, gcw_ref, gcb_ref, gdtb_ref, galog_ref, gd_ref, gng_ref,
             gn_ref, nx_ref, nb_ref, ncc_ref, dxs_ref):
        step = pl.program_id(0)

        @pl.when(step == 0)
        def _():
            gn_ref[...] = jnp.zeros_like(gn_ref)
            nx_ref[...] = jnp.zeros_like(nx_ref)
            nb_ref[...] = jnp.zeros_like(nb_ref)
            ncc_ref[...] = jnp.zeros_like(ncc_ref)
            for ref in (gcw_ref, gcb_ref, gdtb_ref, galog_ref, gd_ref, gng_ref):
                ref[...] = jnp.zeros_like(ref)

        first_chunk = step == nc - 1
        keep = jnp.where(first_chunk, 0.0, 1.0)
        xr, br, cr = xr_ref[...], br_ref[...], cr_ref[...]
        hx, hb, hc = hx_ref[...] * keep, hb_ref[...] * keep, hc_ref[...] * keep
        cwx, cwb, cwc = cwx_ref[...], cwb_ref[...], cwc_ref[...]
        px = _causal_conv(xr, hx, cwx, cbx_ref[...])
        pb = _causal_conv(br, hb, cwb, cbb_ref[...])
        pc = _causal_conv(cr, hc, cwc, cbc_ref[...])
        sx, sb, sc = _sigmoid(px), _sigmoid(pb), _sigmoid(pc)
        xs, bm, cm = px * sx, pb * sb, pc * sc

        dt_in = dt_ref[...] + dtb_ref[...]
        dt, aneg, a_cs, a_t = _ssd_common(dt_ref[...], dtb_ref[...], alog_ref[...])
        exp_a = jnp.exp(a_cs)
        a_last = a_cs[L - 1:L, :]
        dte = jnp.exp(a_last - a_cs)
        dec = jnp.exp(a_last)

        lane = lax.broadcasted_iota(jnp.int32, (L, LANES), 1)
        sub = lax.broadcasted_iota(jnp.int32, (L, LANES), 0)
        lo = lane < SSD_HEAD_DIM
        causal = sub >= lane
        top = sub < SSD_HEAD_DIM
        last_row = sub == L - 1

        yv = y_ref[...]
        zz = z_ref[...]
        sz = _sigmoid(zz)
        silz = zz * sz
        y2 = yv * silz
        dyv = dy_ref[...]
        gw = SSD_WIDTH // SSD_GROUPS
        d_y2_parts = []
        gng_parts = []
        for g in range(SSD_GROUPS):
            seg = y2[:, g * gw:(g + 1) * gw]
            dseg = dyv[:, g * gw:(g + 1) * gw]
            r = lax.rsqrt(jnp.mean(seg * seg, axis=-1, keepdims=True) + EPS)
            n = seg * r
            dn = dseg * ng_ref[:, g * gw:(g + 1) * gw]
            gng_parts.append(jnp.sum(dseg * n, axis=0, keepdims=True))
            d_y2_parts.append(r * (dn - n * jnp.mean(dn * n, axis=-1, keepdims=True)))
        d_y2 = jnp.concatenate(d_y2_parts, axis=1)
        gng_ref[...] += jnp.concatenate(gng_parts, axis=1)
        d_y = d_y2 * silz
        dssd_ref[:, 0:SSD_WIDTH] = (d_y2 * yv * (sz * (1.0 + zz * (1.0 - sz)))).astype(dssd_ref.dtype)
        gd_ref[...] += jnp.sum(d_y * xs, axis=0, keepdims=True)
        dxs_ref[...] = d_y * d_ref[...]

        d_a = jnp.zeros((L, LANES), F32)
        d_at = jnp.zeros((LANES, L), F32)
        ddt = jnp.zeros((L, LANES), F32)
        d_b_parts, d_c_parts = [], []
        for g in range(SSD_GROUPS):
            b_g = bm[:, g * SSD_STATE:(g + 1) * SSD_STATE]
            c_g = cm[:, g * SSD_STATE:(g + 1) * SSD_STATE]
            cb = _dot(c_g, b_g, NT)
            d_cb = jnp.zeros((L, L), F32)
            d_bg = jnp.zeros((L, SSD_STATE), F32)
            d_cg = jnp.zeros((L, SSD_STATE), F32)
            for jj in range(PAIRS_PER_GROUP):
                j = g * PAIRS_PER_GROUP + jj
                h0 = 2 * j
                cols = slice(j * LANES, (j + 1) * LANES)
                dy_p = d_y[:, cols]
                xs_p = xs[:, cols]
                dt_pp = _pair_select(lo, dt, h0)
                expa_p = _pair_select(lo, exp_a, h0)
                dte_p = _pair_select(lo, dte, h0)
                xdt = xs_p * dt_pp
                l0 = jnp.where(causal, jnp.exp(a_cs[:, h0:h0 + 1] - a_t[h0:h0 + 1, :]), 0.0)
                l1 = jnp.where(causal, jnp.exp(a_cs[:, h0 + 1:h0 + 2] - a_t[h0 + 1:h0 + 2, :]), 0.0)
                g0, g1 = l0 * cb, l1 * cb
                h_p = st_ref[0, j]
                gn_p = gn_ref[j]
                dys = dy_p * expa_p
                d_cg = d_cg + _dot(dys, h_p, NN)
                d_h = _dot(dys, c_g, TN)
                t1 = dy_p * _dot(c_g, h_p, NT) * expa_p
                dw = _dot(b_g, gn_p, NT)
                d_bg = d_bg + _dot(xdt * dte_p, gn_p, NN)
                d_xdt = dw * dte_p
                t2 = d_xdt * xdt
                dyl, dyh = jnp.where(lo, dy_p, 0.0), jnp.where(lo, 0.0, dy_p)
                d_xdt = d_xdt + _dot(jnp.concatenate([g0, g1], axis=0), jnp.concatenate([dyl, dyh], axis=0), TN)
                dm0 = _dot(dyl, xdt, NT)
                dm1 = _dot(dyh, xdt, NT)
                d_cb = d_cb + (l0 * dm0 + l1 * dm1)
                e0, e1 = dm0 * g0, dm1 * g1
                a0, a1 = _halves(lo, t1 - t2)
                a0 = a0 + jnp.sum(e0, axis=1, keepdims=True)
                a1 = a1 + jnp.sum(e1, axis=1, keepdims=True)
                s0, s1 = _halves(lo, t2)
                gh = jnp.sum(gn_p * h_p, axis=1, keepdims=True)
                dd0 = jnp.sum(jnp.where(top[:, 0:1], gh, 0.0), axis=0, keepdims=True)
                dd1 = jnp.sum(jnp.where(top[:, 0:1], 0.0, gh), axis=0, keepdims=True)
                end0 = jnp.sum(s0, axis=0, keepdims=True) + dd0 * dec[:, h0:h0 + 1]
                end1 = jnp.sum(s1, axis=0, keepdims=True) + dd1 * dec[:, h0 + 1:h0 + 2]
                d_a = d_a + jnp.where(lane == h0, a0 + jnp.where(last_row, end0, 0.0), 0.0)
                d_a = d_a + jnp.where(lane == h0 + 1, a1 + jnp.where(last_row, end1, 0.0), 0.0)
                d_at = d_at - jnp.where(sub == h0, jnp.sum(e0, axis=0, keepdims=True), 0.0)
                d_at = d_at - jnp.where(sub == h0 + 1, jnp.sum(e1, axis=0, keepdims=True), 0.0)
                dec_rows = jnp.where(top, dec[:, h0:h0 + 1], dec[:, h0 + 1:h0 + 2])
                gn_ref[j] = d_h + dec_rows * gn_p
                q0, q1 = _halves(lo, d_xdt * xs_p)
                ddt = ddt + jnp.where(lane == h0, q0, 0.0) + jnp.where(lane == h0 + 1, q1, 0.0)
                dxs_ref[:, cols] += d_xdt * dt_pp
            d_cg = d_cg + _dot(d_cb, b_g, NN)
            d_bg = d_bg + _dot(d_cb, c_g, TN)
            d_b_parts.append(d_bg)
            d_c_parts.append(d_cg)

        rc = _rev_cumsum_rows(d_a + d_at.T)
        d_dt = rc * aneg + ddt
        galog_ref[...] += jnp.sum(rc * dt, axis=0, keepdims=True) * aneg
        d_dtraw = d_dt * _sigmoid(dt_in)
        gdtb_ref[...] += jnp.sum(d_dtraw, axis=0, keepdims=True)
        ddt_ref[...] = d_dtraw.astype(ddt_ref.dtype)

        def dsilu(p, s):
            return s * (1.0 + p * (1.0 - s))

        dcx = dxs_ref[...] * dsilu(px, sx)
        dcb = jnp.concatenate(d_b_parts, axis=1) * dsilu(pb, sb)
        dcc = jnp.concatenate(d_c_parts, axis=1) * dsilu(pc, sc)
        drx, gwx, gbx = _conv_backward(dcx, nx_ref[...], xr, cwx)
        drb, gwb, gbb = _conv_backward(dcb, nb_ref[...], br, cwb)
        drc, gwc, gbc = _conv_backward(dcc, ncc_ref[...], cr, cwc)
        nx_ref[...] = dcx[0:8, :]
        nb_ref[...] = dcb[0:8, :]
        ncc_ref[...] = dcc[0:8, :]
        dssd_ref[:, SSD_WIDTH:2 * SSD_WIDTH] = drx.astype(dssd_ref.dtype)
        dssd_ref[:, 2 * SSD_WIDTH:2 * SSD_WIDTH + 512] = drb.astype(dssd_ref.dtype)
        dssd_ref[:, 2 * SSD_WIDTH + 512:SSD_SEG] = drc.astype(dssd_ref.dtype)
        for k in range(CONV_WIDTH):
            gcw_ref[k:k + 1, :] += jnp.concatenate([gwx[k], gwb[k], gwc[k]], axis=1)
        gcb_ref[...] += jnp.concatenate([gbx, gbb, gbc], axis=1)

    const = lambda shape: pl.BlockSpec(shape, lambda c: (0,) * len(shape))
    return pl.pallas_call(
        body, name="ssd_bwd", grid=(nc,),
        in_specs=[sp["z"], sp["xr"], sp["br"], sp["cr"], halo_spec(SSD_WIDTH, 1), halo_spec(512, 8), halo_spec(512, 9),
                  sp["dt"], sp["wide"], sp["states"], sp["wide"],
                  sp["cwx"], sp["cwb"], sp["cwc"], sp["cbx"], sp["cbb"], sp["cbc"],
                  sp["vec128"], sp["vec128"], sp["vecw"], sp["vecw"]],
        out_specs=[pl.BlockSpec((L, SSD_SEG), lambda c: (nc - 1 - c, 0)), sp["dt"],
                   const((CONV_WIDTH, SSD_CONV_CH)), const((1, SSD_CONV_CH)), const((1, LANES)), const((1, LANES)),
                   const((1, SSD_WIDTH)), const((1, SSD_WIDTH))],
        out_shape=[jax.ShapeDtypeStruct((t, SSD_SEG), MXU_DTYPE), jax.ShapeDtypeStruct((t, DT_PAD), MXU_DTYPE),
                   jax.ShapeDtypeStruct((CONV_WIDTH, SSD_CONV_CH), F32), jax.ShapeDtypeStruct((1, SSD_CONV_CH), F32),
                   jax.ShapeDtypeStruct((1, LANES), F32), jax.ShapeDtypeStruct((1, LANES), F32),
                   jax.ShapeDtypeStruct((1, SSD_WIDTH), F32), jax.ShapeDtypeStruct((1, SSD_WIDTH), F32)],
        scratch_shapes=[pltpu.VMEM((SSD_PAIRS, 128, SSD_STATE), F32), pltpu.VMEM((8, SSD_WIDTH), F32),
                        pltpu.VMEM((8, 512), F32), pltpu.VMEM((8, 512), F32), pltpu.VMEM((L, SSD_WIDTH), F32)],
        compiler_params=_params(("arbitrary",)),
    )(proj_ssd, proj_ssd, proj_ssd, proj_ssd, proj_ssd, proj_ssd, proj_ssd, dt_p, y, states, dyssd,
      conv_w, conv_w, conv_w, conv_b, conv_b, conv_b, dtb, alog, d_row, ng_row)


def _lru_gates(xl, wa_ref, wx_ref, ba, bx, lam):
    pre_a, pre_x = [], []
    for g in range(LRU_NGROUPS):
        xg = xl[:, g * LRU_GROUP:(g + 1) * LRU_GROUP]
        pre_a.append(_dot(xg, wa_ref[g], NN))
        pre_x.append(_dot(xg, wx_ref[g], NN))
    r = _sigmoid(jnp.concatenate(pre_a, axis=1) + ba)
    i = _sigmoid(jnp.concatenate(pre_x, axis=1) + bx)
    log_a = (-LRU_C * r) * _softplus(-lam)
    a = jnp.exp(log_a)
    mult = jnp.sqrt(-jnp.tanh(log_a) * (a * a + 1.0))
    return r, i, log_a, a, mult


def _scan_rows(p, u, carry, reverse):
    rows, w = p.shape
    groups = rows // 8
    p3, u3 = p.reshape(groups, 8, w), u.reshape(groups, 8, w)
    row = lax.broadcasted_iota(jnp.int32, (groups, 8, w), 1)
    for s in (1, 2, 4):
        ok = row < 8 - s if reverse else row >= s
        shift = 8 - s if reverse else s
        u3 = p3 * jnp.where(ok, pltpu.roll(u3, shift, 1), 0.0) + u3
        p3 = p3 * jnp.where(ok, pltpu.roll(p3, shift, 1), 1.0)
    out = [None] * groups
    for k in (range(groups - 1, -1, -1) if reverse else range(groups)):
        out[k] = p3[k] * carry + u3[k]
        carry = out[k][0:1, :] if reverse else out[k][7:8, :]
    return jnp.concatenate(out, axis=0), carry


def _lru_fwd(proj_lru, conv_w, conv_b, wa, wx, ba, bx, lam):
    t = proj_lru.shape[0]
    rows = min(LRU_ROWS, t)
    nb = t // rows
    W = LRU_WIDTH

    def body(lg_ref, lx_ref, cw_ref, cb_ref, wa_ref, wx_ref, ba_ref, bx_ref, lam_ref, ylru_ref, h_ref,
             halo_ref, carry_ref):
        @pl.when(pl.program_id(0) == 0)
        def _():
            halo_ref[...] = jnp.zeros_like(halo_ref)
            carry_ref[...] = jnp.zeros_like(carry_ref)

        lx = lx_ref[...]
        xl = _causal_conv(lx, halo_ref[...], cw_ref[...], cb_ref[...])
        halo_ref[...] = lx[rows - 8:rows, :]
        _, i, _, a, mult = _lru_gates(xl, wa_ref, wx_ref, ba_ref[...], bx_ref[...], lam_ref[...])
        u = mult * (i * xl)
        h, carry_ref[...] = _scan_rows(a, u, carry_ref[...], False)
        h_ref[...] = h
        lg = lg_ref[...]
        ylru_ref[...] = (h * (lg * _sigmoid(lg))).astype(ylru_ref.dtype)

    const = lambda shape: pl.BlockSpec(shape, lambda b: (0,) * len(shape))
    return pl.pallas_call(
        body, name="lru_fwd", grid=(nb,),
        in_specs=[pl.BlockSpec((rows, W), lambda b: (b, 0)), pl.BlockSpec((rows, W), lambda b: (b, 1)),
                  const((CONV_WIDTH, W)), const((1, W)), const((LRU_NGROUPS, LRU_GROUP, LRU_GROUP)),
                  const((LRU_NGROUPS, LRU_GROUP, LRU_GROUP)), const((1, W)), const((1, W)), const((1, W))],
        out_specs=[pl.BlockSpec((rows, W), lambda b: (b, 0)), pl.BlockSpec((rows, W), lambda b: (b, 0))],
        out_shape=[jax.ShapeDtypeStruct((t, W), MXU_DTYPE), jax.ShapeDtypeStruct((t, W), F32)],
        scratch_shapes=[pltpu.VMEM((8, W), F32), pltpu.VMEM((1, W), F32)],
        compiler_params=_params(("arbitrary",)),
    )(proj_lru, proj_lru, conv_w, conv_b, wa, wx, ba, bx, lam)


def _lru_bwd(proj_lru, h, dylru, conv_w, conv_b, wa, wx, ba, bx, lam):
    t = proj_lru.shape[0]
    rows = min(LRU_ROWS, t)
    nb = t // rows
    W = LRU_WIDTH
    groups8 = rows // 8

    def rev(b):
        return nb - 1 - b

    def halo_spec(col):
        return pl.BlockSpec((8, W), lambda b: (jnp.maximum(rev(b) * groups8 - 1, 0), col))

    def body(lg_ref, lx_ref, hlx_ref, h_ref, hh_ref, dy_ref, cw_ref, cb_ref, wa_ref, wx_ref, ba_ref, bx_ref, lam_ref,
             dlru_ref, gcw_ref, gcb_ref, gba_ref, gbx_ref, glam_ref, gwa_ref, gwx_ref,
             gcarry_ref, afirst_ref, nxt_ref):
        step = pl.program_id(0)

        @pl.when(step == 0)
        def _():
            gcarry_ref[...] = jnp.zeros_like(gcarry_ref)
            afirst_ref[...] = jnp.zeros_like(afirst_ref)
            nxt_ref[...] = jnp.zeros_like(nxt_ref)
            for ref in (gcw_ref, gcb_ref, gba_ref, gbx_ref, glam_ref, gwa_ref, gwx_ref):
                ref[...] = jnp.zeros_like(ref)

        keep = jnp.where(step == nb - 1, 0.0, 1.0)
        lx = lx_ref[...]
        hlx = hlx_ref[...] * keep
        cw = cw_ref[...]
        xl = _causal_conv(lx, hlx, cw, cb_ref[...])
        lam = lam_ref[...]
        r, i, log_a, a, mult = _lru_gates(xl, wa_ref, wx_ref, ba_ref[...], bx_ref[...], lam)
        hv = h_ref[...]
        h_prev = _shift_down(hv, hh_ref[...] * keep, 1)
        lg = lg_ref[...]
        sg = _sigmoid(lg)
        dyv = dy_ref[...]
        d_h = dyv * (lg * sg)
        dlru_ref[:, 0:W] = (dyv * hv * (sg * (1.0 + lg * (1.0 - sg)))).astype(dlru_ref.dtype)

        row = lax.broadcasted_iota(jnp.int32, (rows, W), 0)
        p = jnp.where(row < rows - 1, pltpu.roll(a, rows - 1, 0), afirst_ref[...])
        gsc, gcarry_ref[...] = _scan_rows(p, d_h, gcarry_ref[...], True)
        afirst_ref[...] = a[0:1, :]

        d_a = gsc * h_prev
        v = i * xl
        d_mult = gsc * v
        d_v = gsc * mult
        d_i = d_v * xl
        d_xl = d_v * i
        d_la = d_a * a - d_mult * (a * a) / mult
        sp_neg = _softplus(-lam)
        d_r = d_la * (-LRU_C * sp_neg)
        glam_ref[...] += jnp.sum(d_la * r, axis=0, keepdims=True) * (LRU_C * _sigmoid(-lam))
        d_pa = d_r * r * (1.0 - r)
        d_px = d_i * i * (1.0 - i)
        gba_ref[...] += jnp.sum(d_pa, axis=0, keepdims=True)
        gbx_ref[...] += jnp.sum(d_px, axis=0, keepdims=True)
        parts = []
        for g in range(LRU_NGROUPS):
            cols = slice(g * LRU_GROUP, (g + 1) * LRU_GROUP)
            xg, dpa_g, dpx_g = xl[:, cols], d_pa[:, cols], d_px[:, cols]
            parts.append(_dot(dpa_g, wa_ref[g], NT) + _dot(dpx_g, wx_ref[g], NT))
            gwa_ref[g] += _dot(xg, dpa_g, TN)
            gwx_ref[g] += _dot(xg, dpx_g, TN)
        d_xl = d_xl + jnp.concatenate(parts, axis=1)
        d_lx, gw, gb = _conv_backward(d_xl, nxt_ref[...], lx, cw)
        nxt_ref[...] = d_xl[0:8, :]
        dlru_ref[:, W:2 * W] = d_lx.astype(dlru_ref.dtype)
        for k in range(CONV_WIDTH):
            gcw_ref[k:k + 1, :] += gw[k]
        gcb_ref[...] += gb

    const = lambda shape: pl.BlockSpec(shape, lambda b: (0,) * len(shape))
    wspec = const((LRU_NGROUPS, LRU_GROUP, LRU_GROUP))
    blk = lambda col: pl.BlockSpec((rows, W), lambda b: (rev(b), col))
    return pl.pallas_call(
        body, name="lru_bwd", grid=(nb,),
        in_specs=[blk(0), blk(1), halo_spec(1), blk(0), halo_spec(0), blk(0),
                  const((CONV_WIDTH, W)), const((1, W)), wspec, wspec, const((1, W)), const((1, W)), const((1, W))],
        out_specs=[pl.BlockSpec((rows, 2 * W), lambda b: (rev(b), 0)), const((CONV_WIDTH, W)), const((1, W)),
                   const((1, W)), const((1, W)), const((1, W)), wspec, wspec],
        out_shape=[jax.ShapeDtypeStruct((t, 2 * W), MXU_DTYPE), jax.ShapeDtypeStruct((CONV_WIDTH, W), F32),
                   jax.ShapeDtypeStruct((1, W), F32), jax.ShapeDtypeStruct((1, W), F32), jax.ShapeDtypeStruct((1, W), F32),
                   jax.ShapeDtypeStruct((1, W), F32), jax.ShapeDtypeStruct((LRU_NGROUPS, LRU_GROUP, LRU_GROUP), F32),
                   jax.ShapeDtypeStruct((LRU_NGROUPS, LRU_GROUP, LRU_GROUP), F32)],
        scratch_shapes=[pltpu.VMEM((1, W), F32), pltpu.VMEM((1, W), F32), pltpu.VMEM((8, W), F32)],
        compiler_params=_params(("arbitrary",)),
    )(proj_lru, proj_lru, proj_lru, h, h, dylru, conv_w, conv_b, wa, wx, ba, bx, lam)


def _mem_scores(q_h, k_h):
    s = _dot(q_h, k_h, NT) * (MEM_HEAD_DIM ** -0.5)
    s = s - jnp.max(s, axis=-1, keepdims=True)
    e = jnp.exp(s)
    return e / jnp.sum(e, axis=-1, keepdims=True)


def _mem_fwd(q, kv, rows=512):
    t = q.shape[0]
    rows = min(rows, t)
    m = kv.shape[0]

    def body(q_ref, kv_ref, y_ref):
        for hd in range(MEM_HEADS):
            cols = slice(hd * MEM_HEAD_DIM, (hd + 1) * MEM_HEAD_DIM)
            vcols = slice(D_MODEL + hd * MEM_HEAD_DIM, D_MODEL + (hd + 1) * MEM_HEAD_DIM)
            p = _mem_scores(q_ref[:, cols], kv_ref[:, cols])
            y_ref[:, cols] = _dot(p, kv_ref[:, vcols], NN).astype(y_ref.dtype)

    return pl.pallas_call(
        body, name="mem_fwd", grid=(t // rows,),
        in_specs=[pl.BlockSpec((rows, D_MODEL), lambda i: (i, 0)), pl.BlockSpec((m, 2 * D_MODEL), lambda i: (0, 0))],
        out_specs=pl.BlockSpec((rows, D_MODEL), lambda i: (i, 0)),
        out_shape=jax.ShapeDtypeStruct((t, D_MODEL), MXU_DTYPE),
        compiler_params=_params(("parallel",)),
    )(q, kv)


def _mem_bwd(q, kv, dy, rows=512):
    t = q.shape[0]
    rows = min(rows, t)
    m = kv.shape[0]

    def body(q_ref, kv_ref, dy_ref, dq_ref, dkv_ref):
        @pl.when(pl.program_id(0) == 0)
        def _():
            dkv_ref[...] = jnp.zeros_like(dkv_ref)

        for hd in range(MEM_HEADS):
            cols = slice(hd * MEM_HEAD_DIM, (hd + 1) * MEM_HEAD_DIM)
            vcols = slice(D_MODEL + hd * MEM_HEAD_DIM, D_MODEL + (hd + 1) * MEM_HEAD_DIM)
            q_h, k_h, dy_h = q_ref[:, cols], kv_ref[:, cols], dy_ref[:, cols]
            p = _mem_scores(q_h, k_h)
            dp = _dot(dy_h, kv_ref[:, vcols], NT)
            dkv_ref[:, vcols] += _dot(p, dy_h, TN)
            ds = p * (dp - jnp.sum(dp * p, axis=-1, keepdims=True)) * (MEM_HEAD_DIM ** -0.5)
            dq_ref[:, cols] = _dot(ds, k_h, NN).astype(dq_ref.dtype)
            dkv_ref[:, cols] += _dot(ds, q_h, TN)

    return pl.pallas_call(
        body, name="mem_bwd", grid=(t // rows,),
        in_specs=[pl.BlockSpec((rows, D_MODEL), lambda i: (i, 0)), pl.BlockSpec((m, 2 * D_MODEL), lambda i: (0, 0)),
                  pl.BlockSpec((rows, D_MODEL), lambda i: (i, 0))],
        out_specs=[pl.BlockSpec((rows, D_MODEL), lambda i: (i, 0)), pl.BlockSpec((m, 2 * D_MODEL), lambda i: (0, 0))],
        out_shape=[jax.ShapeDtypeStruct((t, D_MODEL), MXU_DTYPE), jax.ShapeDtypeStruct((m, 2 * D_MODEL), F32)],
        compiler_params=_params(("arbitrary",)),
    )(q, kv, dy)


def _merge_fwd(x, yssd, ylru, ymem, gl, w_bs, w_bl, w_bm, w_out, fg, tgt, rows=256):
    t = x.shape[0]
    rows = min(rows, t)
    D = D_MODEL

    def body(x_ref, ys_ref, yl_ref, ym_ref, gl_ref, wbs_ref, wbl_ref, wbm_ref, wo_ref, fg_ref, tgt_ref,
             ps_ref, pl_ref, pm_ref, mg_ref, dx2_ref, loss_ref, gfg_ref):
        @pl.when(pl.program_id(0) == 0)
        def _():
            loss_ref[...] = jnp.zeros_like(loss_ref)
            gfg_ref[...] = jnp.zeros_like(gfg_ref)

        ps = _dot(ys_ref[...], wbs_ref[...], NN)
        pl_ = _dot(yl_ref[...], wbl_ref[...], NN)
        pm = _dot(ym_ref[...], wbm_ref[...], NN)
        ps_ref[...] = ps
        pl_ref[...] = pl_
        pm_ref[...] = pm
        merged = (_sigmoid(gl_ref[:, 0:D]) * ps + _sigmoid(gl_ref[:, D:2 * D]) * pl_) + _sigmoid(gl_ref[:, 2 * D:3 * D]) * pm
        mg_ref[...] = merged.astype(mg_ref.dtype)
        x2 = x_ref[...] + _dot(merged, wo_ref[...], NN)
        r2 = lax.rsqrt(jnp.mean(x2 * x2, axis=-1, keepdims=True) + EPS)
        xn = x2 * r2
        fg = fg_ref[...]
        diff = xn * fg - tgt_ref[...]
        tile_loss = 0.5 * jnp.sum(jnp.mean(diff * diff, axis=-1, keepdims=True), axis=0, keepdims=True)
        loss_ref[...] += jnp.broadcast_to(tile_loss, loss_ref.shape)
        d_out = diff * (1.0 / D)
        gfg_ref[...] += jnp.sum(d_out * xn, axis=0, keepdims=True)
        dxn = d_out * fg
        dx2_ref[...] = r2 * (dxn - xn * jnp.mean(dxn * xn, axis=-1, keepdims=True))

    row = lambda w: pl.BlockSpec((rows, w), lambda i: (i, 0))
    const = lambda shape: pl.BlockSpec(shape, lambda i: (0,) * len(shape))
    return pl.pallas_call(
        body, name="merge_fwd", grid=(t // rows,),
        in_specs=[row(D), row(SSD_WIDTH), row(LRU_WIDTH), row(D), row(3 * D), const((SSD_WIDTH, D)), const((LRU_WIDTH, D)),
                  const((D, D)), const((D, D)), const((1, D)), row(D)],
        out_specs=[row(D), row(D), row(D), row(D), row(D), const((1, LANES)), const((1, D))],
        out_shape=[jax.ShapeDtypeStruct((t, D), F32), jax.ShapeDtypeStruct((t, D), F32), jax.ShapeDtypeStruct((t, D), F32),
                   jax.ShapeDtypeStruct((t, D), MXU_DTYPE), jax.ShapeDtypeStruct((t, D), F32),
                   jax.ShapeDtypeStruct((1, LANES), F32), jax.ShapeDtypeStruct((1, D), F32)],
        compiler_params=_params(("arbitrary",)),
    )(x, yssd, ylru, ymem, gl, w_bs, w_bl, w_bm, w_out, fg, tgt)


def _merge_bwd(dx2, gl, ps, pl_in, pm, w_bs, w_bl, w_bm, w_out, rows=256):
    t = dx2.shape[0]
    rows = min(rows, t)
    D = D_MODEL

    def body(dx2_ref, gl_ref, ps_ref, pl_ref, pm_ref, wbs_ref, wbl_ref, wbm_ref, wo_ref,
             dg_ref, dps_ref, dpl_ref, dpm_ref, dys_ref, dyl_ref, dym_ref):
        dm = _dot(dx2_ref[...], wo_ref[...], NT)
        for idx, (p_ref, dp_ref, w_ref, dy_ref) in enumerate(
                ((ps_ref, dps_ref, wbs_ref, dys_ref), (pl_ref, dpl_ref, wbl_ref, dyl_ref), (pm_ref, dpm_ref, wbm_ref, dym_ref))):
            gate = _sigmoid(gl_ref[:, idx * D:(idx + 1) * D])
            dg_ref[:, idx * D:(idx + 1) * D] = ((dm * p_ref[...]) * gate * (1.0 - gate)).astype(dg_ref.dtype)
            dp = dm * gate
            dp_ref[...] = dp.astype(dp_ref.dtype)
            dy_ref[...] = _dot(dp, w_ref[...], NT)

    row = lambda w: pl.BlockSpec((rows, w), lambda i: (i, 0))
    const = lambda shape: pl.BlockSpec(shape, lambda i: (0,) * len(shape))
    return pl.pallas_call(
        body, name="merge_bwd", grid=(t // rows,),
        in_specs=[row(D), row(3 * D), row(D), row(D), row(D), const((SSD_WIDTH, D)), const((LRU_WIDTH, D)),
                  const((D, D)), const((D, D))],
        out_specs=[row(3 * D), row(D), row(D), row(D), row(SSD_WIDTH), row(LRU_WIDTH), row(D)],
        out_shape=[jax.ShapeDtypeStruct((t, 3 * D), MXU_DTYPE), jax.ShapeDtypeStruct((t, D), MXU_DTYPE),
                   jax.ShapeDtypeStruct((t, D), MXU_DTYPE), jax.ShapeDtypeStruct((t, D), MXU_DTYPE),
                   jax.ShapeDtypeStruct((t, SSD_WIDTH), F32), jax.ShapeDtypeStruct((t, LRU_WIDTH), F32),
                   jax.ShapeDtypeStruct((t, D), F32)],
        compiler_params=_params(("parallel",)),
    )(dx2, gl, ps, pl_in, pm, w_bs, w_bl, w_bm, w_out)


def _mesh_place():
    x, y, c = lax.axis_index("x"), lax.axis_index("y"), lax.axis_index("c")
    return x, y, c, 4 * x + 2 * y + c


def _other_chips(x, y):
    return [(1 - x, y), (x, 1 - y), (1 - x, 1 - y)]


def _all_gather(arrs, name):
    n = len(arrs)

    def body(*refs):
        ins, outs = refs[:n], refs[n:2 * n]
        send_sems, recv_sems, local_sems = refs[2 * n:]
        x, y, c, me = _mesh_place()
        sibling = (x, y, 1 - c)
        chips = _other_chips(x, y)

        def slot(px, py, pc):
            return 4 * px + 2 * py + pc

        def copy(a, k, block, to, src=None):
            return pltpu.make_async_remote_copy(
                src_ref=outs[a].at[block] if src is None else src, dst_ref=outs[a].at[block],
                send_sem=send_sems.at[a, k], recv_sem=recv_sems.at[a, k], device_id=to, device_id_type=pl.DeviceIdType.MESH)

        local = [pltpu.make_async_copy(ins[a], outs[a].at[me], local_sems.at[a]) for a in range(n)]
        for cp in local:
            cp.start()
        sends = []
        for a in range(n):
            sends.append(copy(a, 0, me, sibling, src=ins[a]))
            for j, chip in enumerate(chips):
                sends.append(copy(a, 1 + j, me, (*chip, c), src=ins[a]))
        for cp in sends:
            cp.start()
        for j, chip in enumerate(chips):
            for a in range(n):
                copy(a, 1 + j, slot(*chip, c), sibling).wait_recv()
                passed = copy(a, 4 + j, slot(*chip, c), sibling)
                passed.start()
                sends.append(passed)
        for a in range(n):
            copy(a, 0, slot(x, y, 1 - c), sibling).wait_recv()
        for j, chip in enumerate(chips):
            for a in range(n):
                copy(a, 4 + j, slot(*chip, 1 - c), sibling).wait_recv()
        for cp in sends:
            cp.wait_send()
        for cp in local:
            cp.wait()

    any_spec = pl.BlockSpec(memory_space=pl.ANY)
    return pl.pallas_call(
        body, name=name, in_specs=[any_spec] * n, out_specs=[any_spec] * n,
        out_shape=[jax.ShapeDtypeStruct((N_DEV,) + a.shape, a.dtype) for a in arrs],
        scratch_shapes=[pltpu.SemaphoreType.DMA((n, 7)), pltpu.SemaphoreType.DMA((n, 7)), pltpu.SemaphoreType.DMA((n,))],
    )(*arrs)


N_CHIPS = 4


def _pair_exchange(parts):
    n = len(parts)

    def body(*refs):
        ins, outs = refs[:n], refs[n:2 * n]
        send_sems, recv_sems = refs[2 * n:]
        x, y, c, _ = _mesh_place()
        sibling = (x, y, 1 - c)
        sends = []
        for a in range(n):
            for q in range(N_CHIPS):
                cp = pltpu.make_async_remote_copy(src_ref=ins[a].at[q, 1 - c], dst_ref=outs[a].at[q], send_sem=send_sems.at[a, q],
                                                  recv_sem=recv_sems.at[a, q], device_id=sibling, device_id_type=pl.DeviceIdType.MESH)
                cp.start()
                sends.append(cp)
        for cp in sends:
            cp.wait_recv()
        for cp in sends:
            cp.wait_send()

    any_spec = pl.BlockSpec(memory_space=pl.ANY)
    return pl.pallas_call(
        body, name="grad_pair_exchange", in_specs=[any_spec] * n, out_specs=[any_spec] * n,
        out_shape=[jax.ShapeDtypeStruct((N_CHIPS,) + a.shape[2:], a.dtype) for a in parts],
        scratch_shapes=[pltpu.SemaphoreType.DMA((n, N_CHIPS)), pltpu.SemaphoreType.DMA((n, N_CHIPS))],
    )(*parts)


def _col_tile(r, c, limit_bytes):
    assert c % LANES == 0, c
    best = LANES
    for cand in range(LANES, c + 1, LANES):
        if c % cand == 0 and r * cand * 4 <= limit_bytes:
            best = cand
    return best


def _chip_sum(part, recv, core, name):
    _, _, r, c = part.shape
    ct = _col_tile(r, c, 2 << 20)

    def body(core_ref, p_ref, r_ref, s_ref, t_ref):
        s = p_ref[...] + r_ref[...]
        s_ref[...] = s
        t_ref[...] = s.astype(t_ref.dtype)

    blk = pl.BlockSpec((None, r, ct), lambda q, i, core_ref: (q, 0, i))
    return pl.pallas_call(
        body, name=name,
        grid_spec=pltpu.PrefetchScalarGridSpec(
            num_scalar_prefetch=1, grid=(N_CHIPS, c // ct),
            in_specs=[pl.BlockSpec((None, None, r, ct), lambda q, i, core_ref: (q, core_ref[0], 0, i)), blk],
            out_specs=[blk, blk]),
        out_shape=[jax.ShapeDtypeStruct((N_CHIPS, r, c), F32), jax.ShapeDtypeStruct((N_CHIPS, r, c), GRAD_WIRE_DTYPE)],
        compiler_params=_params(("parallel", "parallel")),
    )(core, part, recv)


def _chip_exchange(sums):
    n = len(sums)

    def body(*refs):
        ins, outs = refs[:n], refs[n:2 * n]
        send_sems, recv_sems = refs[2 * n:]
        x, y, c, _ = _mesh_place()
        my_chip = 2 * x + y
        sends = []
        for a in range(n):
            for j, (px, py) in enumerate(_other_chips(x, y)):
                cp = pltpu.make_async_remote_copy(src_ref=ins[a].at[2 * px + py], dst_ref=outs[a].at[my_chip], send_sem=send_sems.at[a, j],
                                                  recv_sem=recv_sems.at[a, j], device_id=(px, py, c), device_id_type=pl.DeviceIdType.MESH)
                cp.start()
                sends.append(cp)
        for a in range(n):
            for j, (px, py) in enumerate(_other_chips(x, y)):
                pltpu.make_async_remote_copy(src_ref=ins[a].at[my_chip], dst_ref=outs[a].at[2 * px + py], send_sem=send_sems.at[a, j],
                                             recv_sem=recv_sems.at[a, j], device_id=(px, py, c),
                                             device_id_type=pl.DeviceIdType.MESH).wait_recv()
        for cp in sends:
            cp.wait_send()

    any_spec = pl.BlockSpec(memory_space=pl.ANY)
    return pl.pallas_call(
        body, name="grad_chip_exchange", in_specs=[any_spec] * n, out_specs=[any_spec] * n,
        out_shape=[jax.ShapeDtypeStruct(a.shape, a.dtype) for a in sums],
        scratch_shapes=[pltpu.SemaphoreType.DMA((n, 3)), pltpu.SemaphoreType.DMA((n, 3))],
    )(*sums)


def _row_tile(r, limit):
    if r <= limit:
        return r
    best = 8
    for cand in range(8, limit + 1, 8):
        if r % cand == 0:
            best = cand
    assert r % best == 0, r
    return best


def _adam_update(w, g, m, v):
    nm = ADAM_B1 * m + (1.0 - ADAM_B1) * g
    nv = ADAM_B2 * v + (1.0 - ADAM_B2) * (g * g)
    m_hat = nm / (1.0 - ADAM_B1 ** ADAM_STEP)
    v_hat = nv / (1.0 - ADAM_B2 ** ADAM_STEP)
    return -ADAM_LR * (m_hat / (jnp.sqrt(v_hat) + ADAM_EPS) + ADAM_WD * w), nm, nv


def _sum_adamw(own, recv, chip, w, m, v, name):
    _, r, c = own.shape
    ct = _col_tile(r, c, 1 << 20)

    def body(chip_ref, o_ref, r1_ref, r2_ref, r3_ref, w_ref, m_ref, v_ref, g_ref, d_ref, nm_ref, nv_ref):
        g = ((o_ref[...] + r1_ref[...].astype(F32)) + r2_ref[...].astype(F32)) + r3_ref[...].astype(F32)
        g_ref[...] = g
        d_ref[...], nm_ref[...], nv_ref[...] = _adam_update(w_ref[...], g, m_ref[...], v_ref[...])

    def slot(k):
        return pl.BlockSpec((None, r, ct), lambda i, chip_ref: ((chip_ref[0] + k) % N_CHIPS, 0, i))

    spec = pl.BlockSpec((r, ct), lambda i, chip_ref: (0, i))
    shape = jax.ShapeDtypeStruct((r, c), F32)
    return pl.pallas_call(
        body, name=name,
        grid_spec=pltpu.PrefetchScalarGridSpec(
            num_scalar_prefetch=1, grid=(c // ct,),
            in_specs=[slot(0), slot(1), slot(2), slot(3), spec, spec, spec], out_specs=[spec] * 4),
        out_shape=[shape] * 4,
        compiler_params=_params(("parallel",)),
    )(chip, own, recv, recv, recv, w, m, v)


def _small_adamw(parts, ws, ms, vs):
    n = len(parts)

    def body(*refs):
        p_refs, w_refs, m_refs, v_refs = refs[:n], refs[n:2 * n], refs[2 * n:3 * n], refs[3 * n:4 * n]
        outs = refs[4 * n:]
        for i in range(n):
            g = p_refs[i][0]
            for k in range(1, N_DEV):
                g = g + p_refs[i][k]
            outs[i][...] = g
            outs[n + i][...], outs[2 * n + i][...], outs[3 * n + i][...] = _adam_update(
                w_refs[i][...], g, m_refs[i][...], v_refs[i][...])

    vmem = pl.BlockSpec(memory_space=pltpu.VMEM)
    shapes = [jax.ShapeDtypeStruct(w.shape, F32) for w in ws]
    res = pl.pallas_call(
        body, name="adamw_small", in_specs=[vmem] * (4 * n), out_specs=[vmem] * (4 * n), out_shape=shapes * 4,
        compiler_params=pltpu.CompilerParams(vmem_limit_bytes=VMEM_LIMIT),
    )(*parts, *ws, *ms, *vs)
    return res[:n], res[n:2 * n], res[2 * n:3 * n], res[3 * n:]


def _pack(arrs, dtype, row_multiple):
    flat = jnp.concatenate([a.reshape(-1).astype(dtype) for a in arrs])
    unit = LANES * row_multiple
    padded = -(-flat.shape[0] // unit) * unit
    return jnp.pad(flat, (0, padded - flat.shape[0])).reshape(-1, LANES)


def _unpack(packed, shapes, lead=()):
    flat = packed.reshape(lead + (-1,))
    out, off = [], 0
    for shp in shapes:
        n = math.prod(shp)
        out.append(flat[..., off:off + n].reshape(lead + tuple(shp)))
        off += n
    return out


def _gather_cols(g, lo, hi):
    width = g.shape[2]
    pieces = []
    for s in range(N_DEV):
        a, e = max(lo, s * width), min(hi, (s + 1) * width)
        if a < e:
            pieces.append(g[s, :, a - s * width:e - s * width])
    return pieces[0] if len(pieces) == 1 else jnp.concatenate(pieces, axis=1)


def _scatter_cols(segs, width):
    slots = []
    for k in range(N_DEV):
        lo, hi = k * width, (k + 1) * width
        pieces = []
        for arr, s_lo, s_hi in segs:
            a, e = max(lo, s_lo), min(hi, s_hi)
            if a < e:
                pieces.append(arr[:, a - s_lo:e - s_lo])
        slots.append(pieces[0] if len(pieces) == 1 else jnp.concatenate(pieces, axis=1))
    return jnp.stack(slots)


def _block_diag_groups(w):
    w4 = w.reshape(LRU_NGROUPS, 4, LRU_BLOCK, LRU_BLOCK)
    eye = jnp.eye(4, dtype=w.dtype)
    return jnp.einsum("gaij,ab->gaibj", w4, eye).reshape(LRU_NGROUPS, LRU_GROUP, LRU_GROUP)


def _block_diag_extract(wg):
    w5 = wg.reshape(LRU_NGROUPS, 4, LRU_BLOCK, 4, LRU_BLOCK)
    idx = jnp.arange(4)
    return w5[:, idx, :, idx, :].transpose(1, 0, 2, 3).reshape(LRU_BLOCKS, LRU_BLOCK, LRU_BLOCK)


BIG = ("w_in", "w_kv", "w_br_ssd", "w_br_lru", "w_br_mem", "w_out")
SMALL_SHARDED = ("ssd_conv_w", "ssd_norm_g", "lru_conv_w")
REPLICATED = ("norm_g", "ssd_conv_b", "ssd_dt_bias", "ssd_a_log", "ssd_d", "lru_conv_b", "lru_w_a", "lru_b_a",
              "lru_w_x", "lru_b_x", "lru_lambda", "mem_norm_g", "final_g")
WEIGHTS = ("norm_g", "w_in", "ssd_conv_w", "ssd_conv_b", "ssd_dt_bias", "ssd_a_log", "ssd_d", "ssd_norm_g", "lru_conv_w",
           "lru_conv_b", "lru_w_a", "lru_b_a", "lru_w_x", "lru_b_x", "lru_lambda", "mem_norm_g", "w_kv", "w_br_ssd",
           "w_br_lru", "w_br_mem", "w_out", "final_g")


def kernel(x, mem, norm_g, w_in, ssd_conv_w, ssd_conv_b, ssd_dt_bias, ssd_a_log, ssd_d, ssd_norm_g, lru_conv_w, lru_conv_b, lru_w_a, lru_b_a, lru_w_x, lru_b_x, lru_lambda, mem_norm_g, w_kv, w_br_ssd, w_br_lru, w_br_mem, w_out, final_g, loss_target, m_norm_g, m_w_in, m_ssd_conv_w, m_ssd_conv_b, m_ssd_dt_bias, m_ssd_a_log, m_ssd_d, m_ssd_norm_g, m_lru_conv_w, m_lru_conv_b, m_lru_w_a, m_lru_b_a, m_lru_w_x, m_lru_b_x, m_lru_lambda, m_mem_norm_g, m_w_kv, m_w_br_ssd, m_w_br_lru, m_w_br_mem, m_w_out, m_final_g, v_norm_g, v_w_in, v_ssd_conv_w, v_ssd_conv_b, v_ssd_dt_bias, v_ssd_a_log, v_ssd_d, v_ssd_norm_g, v_lru_conv_w, v_lru_conv_b, v_lru_w_a, v_lru_b_a, v_lru_w_x, v_lru_b_x, v_lru_lambda, v_mem_norm_g, v_w_kv, v_w_br_ssd, v_w_br_lru, v_w_br_mem, v_w_out, v_final_g):
    env = dict(locals())
    W = {n: env[n] for n in WEIGHTS}
    M = {n: env["m_" + n] for n in WEIGHTS}
    V = {n: env["v_" + n] for n in WEIGHTS}
    me = 4 * lax.axis_index("x") + 2 * lax.axis_index("y") + lax.axis_index("c")
    t = x.shape[1]
    xt = x[0]
    memt = mem[0]
    tgt = loss_target[0]

    small_shapes = [W[n].shape for n in SMALL_SHARDED]
    as2d = lambda d, n: jnp.transpose(d[n][0]) if n == "w_in" else d[n][0]
    gathered = _all_gather([as2d(W, n).astype(MXU_DTYPE) for n in BIG] + [_pack([W[n] for n in SMALL_SHARDED], F32, 8)],
                           "weights_all_gather")
    g_in, g_kv, g_bs, g_bl, g_bm, g_out, gs = gathered
    g_cw, g_ng, g_lcw = _unpack(gs, small_shapes, (N_DEV,))
    cols = lambda a: jnp.moveaxis(a[:, 0], 0, -2).reshape(a.shape[2:-1] + (-1,))
    rows_ = lambda a: a.reshape((-1,) + a.shape[2:])
    w_bs_f, w_bl_f, w_bm_f, w_out_f = rows_(g_bs), rows_(g_bl), rows_(g_bm), rows_(g_out)
    conv_w_f, ssd_ng_f, lru_cw_f = cols(g_cw), cols(g_ng), cols(g_lcw)
    b = SEG_BOUNDS
    w_kv_f = _gather_cols(g_kv, 0, 2 * D_MODEL)
    w_in_t = g_in.reshape(IN_WIDTH, D_MODEL)
    w_ssd, w_lru, w_q, w_g = w_in_t[b[0]:b[1]], w_in_t[b[2]:b[3]], w_in_t[b[3]:b[4]], w_in_t[b[4]:b[5]]
    w_dt = jnp.pad(w_in_t[b[1]:b[2]], ((0, DT_PAD - SSD_HEADS), (0, 0)))

    pad_heads = lambda a: jnp.pad(a, ((0, 0), (0, LANES - SSD_HEADS)))
    dtb, alog = pad_heads(ssd_dt_bias), pad_heads(ssd_a_log)
    d_row = jnp.repeat(ssd_d, SSD_HEAD_DIM, axis=1)
    ng_row = ssd_ng_f.reshape(1, SSD_WIDTH)
    wa_g, wx_g = _block_diag_groups(lru_w_a[0]), _block_diag_groups(lru_w_x[0])
    ba, bx = lru_b_a.reshape(1, LRU_WIDTH), lru_b_x.reshape(1, LRU_WIDTH)
    fg = final_g.reshape(1, D_MODEL)

    h = _rms_fwd(xt, norm_g, "norm_fwd")
    proj_ssd = _matmul(h, w_ssd, "nt", "proj_ssd", tk=D_MODEL)
    proj_lru = _matmul(h, w_lru, "nt", "proj_lru", tk=D_MODEL)
    proj_q = _matmul(h, w_q, "nt", "proj_q", tk=D_MODEL)
    proj_g = _matmul(h, w_g, "nt", "proj_g", tk=D_MODEL)
    proj_dt = _matmul(h, w_dt, "nt", "proj_dt", tk=D_MODEL)
    mem_n = _rms_fwd(memt, mem_norm_g, "mem_norm_fwd")
    kv = _matmul(mem_n, w_kv_f, "nn", "mem_kv", tk=D_MODEL)
    yssd, y_scan, states = _ssd_fwd(proj_ssd, proj_dt, conv_w_f, ssd_conv_b, dtb, alog, d_row, ng_row)
    ylru, h_lru = _lru_fwd(proj_lru, lru_cw_f, lru_conv_b, wa_g, wx_g, ba, bx, lru_lambda)
    ymem = _mem_fwd(proj_q, kv)
    ps, pl_, pm, merged, dx2, loss_vec, g_fg = _merge_fwd(xt, yssd, ylru, ymem, proj_g, w_bs_f, w_bl_f, w_bm_f, w_out_f, fg, tgt)

    d_g, dps, dpl, dpm, dyssd, dylru, dymem = _merge_bwd(dx2, proj_g, ps, pl_, pm, w_bs_f, w_bl_f, w_bm_f, w_out_f)
    gw_out = _matmul(merged, dx2, "tn", "grad_w_out", tk=1024)
    gw_bs = _matmul(yssd, dps, "tn", "grad_w_br_ssd", tk=1024)
    gw_bl = _matmul(ylru, dpl, "tn", "grad_w_br_lru", tm=LRU_WIDTH, tk=1024)
    gw_bm = _matmul(ymem, dpm, "tn", "grad_w_br_mem", tk=1024)
    d_q, d_kv = _mem_bwd(proj_q, kv, dymem)
    gw_kv = _matmul(mem_n, d_kv, "tn", "grad_w_kv", tk=memt.shape[0])
    d_memn = _matmul(d_kv, w_kv_f, "nt", "d_mem_n", tk=1024)
    _, g_memng = _rms_bwd(memt, d_memn, None, mem_norm_g, "mem_norm_bwd")
    d_lru, gl_cw, gl_cb, g_ba, g_bx, g_lam, gwa_g, gwx_g = _lru_bwd(proj_lru, h_lru, dylru, lru_cw_f, lru_conv_b, wa_g, wx_g, ba, bx, lru_lambda)
    d_ssd, d_dt, gs_cw, gs_cb, g_dtb, g_alog, g_dch, g_ngrow = _ssd_bwd(proj_ssd, proj_dt, y_scan, states, dyssd, conv_w_f, ssd_conv_b, dtb, alog, d_row, ng_row)
    dh = _matmul(d_ssd, w_ssd, "nn", "dh_ssd", tk=1024)
    dh = _matmul(d_lru, w_lru, "nn", "dh_lru", tk=1024, acc_in=dh)
    dh = _matmul(d_q, w_q, "nn", "dh_q", tk=1024, acc_in=dh)
    dh = _matmul(d_g, w_g, "nn", "dh_g", tk=1024, acc_in=dh)
    dh = _matmul(d_dt, w_dt, "nn", "dh_dt", tk=DT_PAD, acc_in=dh)
    gw_ssd = _matmul(d_ssd, h, "tn", "grad_w_in_ssd", tk=1024)
    gw_lru = _matmul(d_lru, h, "tn", "grad_w_in_lru", tk=1024)
    gw_q = _matmul(d_q, h, "tn", "grad_w_in_q", tk=1024)
    gw_g = _matmul(d_g, h, "tn", "grad_w_in_g", tk=1024)
    gw_dt = _matmul(d_dt, h, "tn", "grad_w_in_dt", tk=1024)
    grad_x, g_normg = _rms_bwd(xt, dh, dx2, norm_g, "norm_bwd")

    split_rows = lambda a: a.reshape((N_DEV, -1) + a.shape[1:])
    gw_in_t = jnp.concatenate([gw_ssd, gw_dt[:SSD_HEADS], gw_lru, gw_q, gw_g], axis=0)
    big_send = [split_rows(gw_in_t), _scatter_cols([(gw_kv, 0, 2 * D_MODEL)], 2 * D_MODEL // N_DEV),
                split_rows(gw_bs), split_rows(gw_bl), split_rows(gw_bm), split_rows(gw_out)]

    small_grads = {
        "norm_g": g_normg, "ssd_conv_w": gs_cw, "ssd_conv_b": gs_cb, "ssd_dt_bias": g_dtb[:, :SSD_HEADS],
        "ssd_a_log": g_alog[:, :SSD_HEADS], "ssd_d": jnp.sum(g_dch.reshape(SSD_HEADS, SSD_HEAD_DIM), axis=1).reshape(1, SSD_HEADS),
        "ssd_norm_g": g_ngrow.reshape(SSD_GROUPS, -1), "lru_conv_w": gl_cw, "lru_conv_b": gl_cb,
        "lru_w_a": _block_diag_extract(gwa_g), "lru_b_a": g_ba, "lru_w_x": _block_diag_extract(gwx_g), "lru_b_x": g_bx,
        "lru_lambda": g_lam, "mem_norm_g": g_memng, "final_g": g_fg,
    }
    small_all = REPLICATED + SMALL_SHARDED

    def small_shape(n, shards):
        shp = W[n].shape[1:] if W[n].ndim > 2 else (1, W[n].shape[-1])
        return shp[:-1] + (shp[-1] * shards,)

    small_recv = _all_gather([small_grads[n].reshape(small_shape(n, N_DEV if n in SMALL_SHARDED else 1)) for n in small_all],
                             "small_grads_all_gather")

    core = lax.axis_index("c").astype(jnp.int32).reshape(1)
    chip = (2 * lax.axis_index("x") + lax.axis_index("y")).astype(jnp.int32).reshape(1)
    by_chip = [a.reshape((N_CHIPS, 2) + a.shape[1:]) for a in big_send]
    from_sibling = _pair_exchange(by_chip)
    chip_sums = [_chip_sum(p, r, core, "chip_sum_" + n) for n, p, r in zip(BIG, by_chip, from_sibling)]
    from_chips = _chip_exchange([s16 for _, s16 in chip_sums])

    grads, delta, new_m, new_v = {}, {}, {}, {}
    for n, (s32, _), recv in zip(BIG, chip_sums, from_chips):
        res = _sum_adamw(s32, recv, chip, as2d(W, n), as2d(M, n), as2d(V, n), "adamw_" + n)
        for dst, a in zip((grads, delta, new_m, new_v), res):
            dst[n] = (jnp.transpose(a) if n == "w_in" else a)[None]

    parts = []
    for n, a in zip(small_all, small_recv):
        if n in SMALL_SHARDED:
            width = W[n].shape[-1]
            a = lax.dynamic_slice_in_dim(a, me * width, width, axis=a.ndim - 1)
        parts.append(a)
    canon = lambda d: [d[n].reshape(small_shape(n, 1)) for n in small_all]
    for dst, res in zip((grads, delta, new_m, new_v), _small_adamw(parts, canon(W), canon(M), canon(V))):
        for n, a in zip(small_all, res):
            dst[n] = a.reshape(W[n].shape)

    loss = lax.psum(loss_vec[0, 0], ("x", "y", "c"))
    return (loss, grad_x[None], *[grads[n] for n in WEIGHTS], *[delta[n] for n in WEIGHTS],
            *[new_m[n] for n in WEIGHTS], *[new_v[n] for n in WEIGHTS])
```

```python
import functools
import math

import jax
import jax.numpy as jnp
from jax import lax
from jax.experimental import pallas as pl
from jax.experimental.pallas import tpu as pltpu

F32 = jnp.float32
MXU_DTYPE = jnp.bfloat16
GRAD_WIRE_DTYPE = jnp.bfloat16

D_MODEL = 1024
EPS = 1e-6
CONV_WIDTH = 4
SSD_WIDTH = 2048
SSD_HEAD_DIM = 64
SSD_HEADS = 32
SSD_GROUPS = 4
SSD_STATE = 128
SSD_CHUNK = 128
SSD_CONV_CH = SSD_WIDTH + 2 * SSD_GROUPS * SSD_STATE
SSD_PAIRS = SSD_HEADS // 2
PAIRS_PER_GROUP = SSD_PAIRS // SSD_GROUPS
LRU_WIDTH = 1536
LRU_BLOCKS = 16
LRU_BLOCK = 96
LRU_GROUP = 4 * LRU_BLOCK
LRU_NGROUPS = LRU_WIDTH // LRU_GROUP
LRU_C = 8.0
LRU_ROWS = 256
MEM_HEADS = 4
MEM_HEAD_DIM = 256
IN_WIDTH = 12320
N_DEV = 8
LANES = 128
SSD_SEG = SSD_WIDTH + SSD_CONV_CH
DT_PAD = LANES
SEG_BOUNDS = (0, 5120, 5152, 8224, 9248, 12320)

ADAM_LR = 0.001
ADAM_B1 = 0.9
ADAM_B2 = 0.999
ADAM_EPS = 1e-08
ADAM_WD = 0.01
ADAM_STEP = 10

VMEM_LIMIT = 56 * 1024 * 1024

NN = (((1,), (0,)), ((), ()))
NT = (((1,), (1,)), ((), ()))
TN = (((0,), (0,)), ((), ()))


def _dot(a, b, dims):
    return lax.dot_general(a.astype(MXU_DTYPE), b.astype(MXU_DTYPE), dims, preferred_element_type=F32)


def _sigmoid(x):
    return 0.5 * jnp.tanh(0.5 * x) + 0.5


def _log1p(e):
    u = 1.0 + e
    return jnp.where(u == 1.0, e, jnp.log(u) * (e / jnp.where(u == 1.0, 1.0, u - 1.0)))


def _softplus(x):
    return jnp.maximum(x, 0.0) + _log1p(jnp.exp(-jnp.abs(x)))


def _params(semantics):
    return pltpu.CompilerParams(dimension_semantics=semantics, vmem_limit_bytes=VMEM_LIMIT)


def _shift_down(cur, halo8, k):
    rolled = pltpu.roll(cur, k, 0)
    row8 = lax.broadcasted_iota(jnp.int32, halo8.shape, 0)
    top = jnp.where(row8 >= k, rolled[0:8], pltpu.roll(halo8, k, 0))
    return jnp.concatenate([top, rolled[8:]], axis=0)


def _shift_up(cur, next8, k):
    rows = cur.shape[0]
    rolled = pltpu.roll(cur, rows - k, 0)
    row8 = lax.broadcasted_iota(jnp.int32, next8.shape, 0)
    bot = jnp.where(row8 < 8 - k, rolled[rows - 8:rows], pltpu.roll(next8, 8 - k, 0))
    return jnp.concatenate([rolled[:rows - 8], bot], axis=0)


def _causal_conv(raw, halo8, w, b):
    acc = raw * w[3:4, :] + b
    for k in range(1, CONV_WIDTH):
        acc = acc + _shift_down(raw, halo8, k) * w[3 - k:4 - k, :]
    return acc


def _conv_backward(dco, next8, raw, w):
    d_raw = dco * w[3:4, :]
    gw = [None] * CONV_WIDTH
    gw[3] = jnp.sum(dco * raw, axis=0, keepdims=True)
    for j in range(1, CONV_WIDTH):
        up = _shift_up(dco, next8, j)
        d_raw = d_raw + up * w[3 - j:4 - j, :]
        gw[3 - j] = jnp.sum(up * raw, axis=0, keepdims=True)
    gb = jnp.sum(dco, axis=0, keepdims=True)
    return d_raw, gw, gb


def _cumsum_rows(v):
    rows = v.shape[0]
    row = lax.broadcasted_iota(jnp.int32, v.shape, 0)
    s = 1
    while s < rows:
        v = v + jnp.where(row >= s, pltpu.roll(v, s, 0), 0.0)
        s *= 2
    return v


def _rev_cumsum_rows(v):
    rows = v.shape[0]
    row = lax.broadcasted_iota(jnp.int32, v.shape, 0)
    s = 1
    while s < rows:
        v = v + jnp.where(row < rows - s, pltpu.roll(v, rows - s, 0), 0.0)
        s *= 2
    return v


def _matmul(a, b, mode, name, tm=1024, tn=1024, tk=1024):
    if mode == "nn":
        (m, kk), n = a.shape, b.shape[1]
    elif mode == "nt":
        (m, kk), n = a.shape, b.shape[0]
    else:
        (kk, m), n = a.shape, b.shape[1]
    tm, tn, tk = min(tm, m), min(tn, n), min(tk, kk)
    assert m % tm == 0 and n % tn == 0 and kk % tk == 0, (name, a.shape, b.shape)
    nk = kk // tk
    dims = {"nn": NN, "nt": NT, "tn": TN}[mode]
    a_spec = pl.BlockSpec((tk, tm), lambda i, j, k: (k, i)) if mode == "tn" else pl.BlockSpec((tm, tk), lambda i, j, k: (i, k))
    b_spec = pl.BlockSpec((tn, tk), lambda i, j, k: (j, k)) if mode == "nt" else pl.BlockSpec((tk, tn), lambda i, j, k: (k, j))
    o_spec = pl.BlockSpec((tm, tn), lambda i, j, k: (i, j))

    def body_single(a_ref, b_ref, o_ref):
        o_ref[...] = _dot(a_ref[...], b_ref[...], dims)

    def body(a_ref, b_ref, o_ref, acc_ref):
        k = pl.program_id(2)

        @pl.when(k == 0)
        def _():
            acc_ref[...] = jnp.zeros_like(acc_ref)

        acc_ref[...] += _dot(a_ref[...], b_ref[...], dims)

        @pl.when(k == nk - 1)
        def _():
            o_ref[...] = acc_ref[...]

    return pl.pallas_call(
        body_single if nk == 1 else body, name=name, grid=(m // tm, n // tn, nk), in_specs=[a_spec, b_spec], out_specs=o_spec,
        out_shape=jax.ShapeDtypeStruct((m, n), F32),
        scratch_shapes=[] if nk == 1 else [pltpu.VMEM((tm, tn), F32)],
        compiler_params=_params(("parallel", "parallel", "arbitrary")),
    )(a, b)


def _rms_fwd(x, g, name, rows=512):
    t, d = x.shape
    rows = min(rows, t)

    def body(x_ref, g_ref, h_ref):
        xv = x_ref[...]
        r = lax.rsqrt(jnp.mean(xv * xv, axis=-1, keepdims=True) + EPS)
        h_ref[...] = ((xv * r) * g_ref[...]).astype(h_ref.dtype)

    return pl.pallas_call(
        body, name=name, grid=(t // rows,),
        in_specs=[pl.BlockSpec((rows, d), lambda i: (i, 0)), pl.BlockSpec((1, d), lambda i: (0, 0))],
        out_specs=pl.BlockSpec((rows, d), lambda i: (i, 0)),
        out_shape=jax.ShapeDtypeStruct((t, d), MXU_DTYPE),
        compiler_params=_params(("parallel",)),
    )(x, g)


def _rms_bwd(x, dh, dres, g, name, rows=512):
    t, d = x.shape
    rows = min(rows, t)
    has_res = dres is not None

    def body(*refs):
        if has_res:
            x_ref, dh_ref, dr_ref, g_ref, dx_ref, gg_ref = refs
        else:
            x_ref, dh_ref, g_ref, dx_ref, gg_ref = refs

        @pl.when(pl.program_id(0) == 0)
        def _():
            gg_ref[...] = jnp.zeros_like(gg_ref)

        xv = x_ref[...]
        dhv = dh_ref[...]
        r = lax.rsqrt(jnp.mean(xv * xv, axis=-1, keepdims=True) + EPS)
        n = xv * r
        dn = dhv * g_ref[...]
        dx = r * (dn - n * jnp.mean(dn * n, axis=-1, keepdims=True))
        if has_res:
            dx = dx + dr_ref[...]
        dx_ref[...] = dx
        gg_ref[...] += jnp.sum(dhv * n, axis=0, keepdims=True)

    row_spec = pl.BlockSpec((rows, d), lambda i: (i, 0))
    vec_spec = pl.BlockSpec((1, d), lambda i: (0, 0))
    args = (x, dh) + ((dres,) if has_res else ()) + (g,)
    return pl.pallas_call(
        body, name=name, grid=(t // rows,),
        in_specs=[row_spec, row_spec] + ([row_spec] if has_res else []) + [vec_spec],
        out_specs=[row_spec, vec_spec],
        out_shape=[jax.ShapeDtypeStruct((t, d), F32), jax.ShapeDtypeStruct((1, d), F32)],
        compiler_params=_params(("arbitrary",)),
    )(*args)


def _dh_norm_bwd(segs, x, dres, g, rows=512, tk=1024):
    t, d = x.shape
    rows = min(rows, t)
    plan = []
    step0 = 0
    for a, _ in segs:
        kb = min(tk, a.shape[1])
        assert a.shape[1] % kb == 0, a.shape
        plan.append((step0, a.shape[1] // kb, kb))
        step0 += a.shape[1] // kb
    n_steps = step0
    ns = len(segs)

    def body(*refs):
        a_refs, w_refs = refs[0:2 * ns:2], refs[1:2 * ns:2]
        x_ref, dr_ref, g_ref, dx_ref, gg_ref, acc_ref = refs[2 * ns:]
        i, k = pl.program_id(0), pl.program_id(1)

        @pl.when((i == 0) & (k == 0))
        def _():
            gg_ref[...] = jnp.zeros_like(gg_ref)

        @pl.when(k == 0)
        def _():
            acc_ref[...] = jnp.zeros_like(acc_ref)

        for s, (first, nblk, _) in enumerate(plan):
            @pl.when((k >= first) & (k < first + nblk))
            def _(s=s):
                acc_ref[...] += _dot(a_refs[s][...], w_refs[s][...], NN)

        @pl.when(k == n_steps - 1)
        def _():
            xv = x_ref[...]
            dhv = acc_ref[...]
            r = lax.rsqrt(jnp.mean(xv * xv, axis=-1, keepdims=True) + EPS)
            n = xv * r
            dn = dhv * g_ref[...]
            dx_ref[...] = r * (dn - n * jnp.mean(dn * n, axis=-1, keepdims=True)) + dr_ref[...]
            gg_ref[...] += jnp.sum(dhv * n, axis=0, keepdims=True)

    in_specs, args = [], []
    for (a, w), (first, nblk, kb) in zip(segs, plan):
        blk = lambda k, first=first, nblk=nblk: jnp.clip(k - first, 0, nblk - 1)
        in_specs.append(pl.BlockSpec((rows, kb), lambda i, k, blk=blk: (i, blk(k))))
        in_specs.append(pl.BlockSpec((kb, d), lambda i, k, blk=blk: (blk(k), 0)))
        args += [a, w]
    row_spec = pl.BlockSpec((rows, d), lambda i, k: (i, 0))
    vec_spec = pl.BlockSpec((1, d), lambda i, k: (0, 0))
    return pl.pallas_call(
        body, name="dh_norm_bwd", grid=(t // rows, n_steps),
        in_specs=in_specs + [row_spec, row_spec, vec_spec], out_specs=[row_spec, vec_spec],
        out_shape=[jax.ShapeDtypeStruct((t, d), F32), jax.ShapeDtypeStruct((1, d), F32)],
        scratch_shapes=[pltpu.VMEM((rows, d), F32)],
        compiler_params=_params(("arbitrary", "arbitrary")),
    )(*args, x, dres, g)


def _pair_select(lo, m, h0):
    return jnp.where(lo, m[:, h0:h0 + 1], m[:, h0 + 1:h0 + 2])


def _halves(lo, v):
    return (jnp.sum(jnp.where(lo, v, 0.0), axis=1, keepdims=True),
            jnp.sum(jnp.where(lo, 0.0, v), axis=1, keepdims=True))


def _ssd_common(dt_raw, dtb, alog):
    dt = _softplus(dt_raw + dtb)
    aneg = -jnp.exp(alog)
    a_cs = _cumsum_rows(dt * aneg)
    return dt, aneg, a_cs, a_cs.T


def _ssd_specs(nc, rev):
    cidx = (lambda c: nc - 1 - c) if rev else (lambda c: c)
    L = SSD_CHUNK
    return dict(
        z=pl.BlockSpec((L, SSD_WIDTH), lambda c: (cidx(c), 0)),
        xr=pl.BlockSpec((L, SSD_WIDTH), lambda c: (cidx(c), 1)),
        br=pl.BlockSpec((L, 512), lambda c: (cidx(c), 8)),
        cr=pl.BlockSpec((L, 512), lambda c: (cidx(c), 9)),
        dt=pl.BlockSpec((L, DT_PAD), lambda c: (cidx(c), 0)),
        cwx=pl.BlockSpec((CONV_WIDTH, SSD_WIDTH), lambda c: (0, 0)),
        cwb=pl.BlockSpec((CONV_WIDTH, 512), lambda c: (0, 4)),
        cwc=pl.BlockSpec((CONV_WIDTH, 512), lambda c: (0, 5)),
        cbx=pl.BlockSpec((1, SSD_WIDTH), lambda c: (0, 0)),
        cbb=pl.BlockSpec((1, 512), lambda c: (0, 4)),
        cbc=pl.BlockSpec((1, 512), lambda c: (0, 5)),
        vec128=pl.BlockSpec((1, LANES), lambda c: (0, 0)),
        vecw=pl.BlockSpec((1, SSD_WIDTH), lambda c: (0, 0)),
        wide=pl.BlockSpec((L, SSD_WIDTH), lambda c: (cidx(c), 0)),
        states=pl.BlockSpec((1, SSD_PAIRS, 128, SSD_STATE), lambda c: (cidx(c), 0, 0, 0)),
    )


def _ssd_fwd(proj_ssd, dt_p, conv_w, conv_b, dtb, alog, d_row, ng_row):
    t = proj_ssd.shape[0]
    nc = t // SSD_CHUNK
    L = SSD_CHUNK
    sp = _ssd_specs(nc, False)

    def body(z_ref, xr_ref, br_ref, cr_ref, dt_ref, cwx_ref, cwb_ref, cwc_ref, cbx_ref, cbb_ref, cbc_ref,
             dtb_ref, alog_ref, d_ref, ng_ref, yssd_ref, y_ref, st_ref, pre_ref,
             hx_ref, hb_ref, hc_ref, state_ref, yacc_ref):
        @pl.when(pl.program_id(0) == 0)
        def _():
            hx_ref[...] = jnp.zeros_like(hx_ref)
            hb_ref[...] = jnp.zeros_like(hb_ref)
            hc_ref[...] = jnp.zeros_like(hc_ref)
            state_ref[...] = jnp.zeros_like(state_ref)

        xr, br, cr = xr_ref[...], br_ref[...], cr_ref[...]
        px = _causal_conv(xr, hx_ref[...], cwx_ref[...], cbx_ref[...])
        pb = _causal_conv(br, hb_ref[...], cwb_ref[...], cbb_ref[...])
        pc = _causal_conv(cr, hc_ref[...], cwc_ref[...], cbc_ref[...])
        hx_ref[...] = xr[L - 8:L, :]
        hb_ref[...] = br[L - 8:L, :]
        hc_ref[...] = cr[L - 8:L, :]
        pre_ref[:, 0:SSD_WIDTH] = px
        pre_ref[:, SSD_WIDTH:SSD_WIDTH + 512] = pb
        pre_ref[:, SSD_WIDTH + 512:SSD_CONV_CH] = pc
        xs = px * _sigmoid(px)
        bm = pb * _sigmoid(pb)
        cm = pc * _sigmoid(pc)

        dt, _, a_cs, a_t = _ssd_common(dt_ref[...], dtb_ref[...], alog_ref[...])
        exp_a = jnp.exp(a_cs)
        a_last = a_cs[L - 1:L, :]
        dte = jnp.exp(a_last - a_cs)
        dec = jnp.exp(a_last)

        lane = lax.broadcasted_iota(jnp.int32, (L, LANES), 1)
        sub = lax.broadcasted_iota(jnp.int32, (L, LANES), 0)
        lo = lane < SSD_HEAD_DIM
        causal = sub >= lane
        top = sub < SSD_HEAD_DIM

        for g in range(SSD_GROUPS):
            b_g = bm[:, g * SSD_STATE:(g + 1) * SSD_STATE]
            c_g = cm[:, g * SSD_STATE:(g + 1) * SSD_STATE]
            cb = _dot(c_g, b_g, NT)
            for jj in range(PAIRS_PER_GROUP):
                j = g * PAIRS_PER_GROUP + jj
                h0 = 2 * j
                cols = slice(j * LANES, (j + 1) * LANES)
                xs_p = xs[:, cols]
                xdt = xs_p * _pair_select(lo, dt, h0)
                g0 = jnp.where(causal, jnp.exp(a_cs[:, h0:h0 + 1] - a_t[h0:h0 + 1, :]), 0.0) * cb
                g1 = jnp.where(causal, jnp.exp(a_cs[:, h0 + 1:h0 + 2] - a_t[h0 + 1:h0 + 2, :]), 0.0) * cb
                lhs = jnp.concatenate([g0, g1], axis=1)
                rhs = jnp.concatenate([jnp.where(lo, xdt, 0.0), jnp.where(lo, 0.0, xdt)], axis=0)
                y_diag = _dot(lhs, rhs, NN)
                h_p = state_ref[j]
                st_ref[0, j] = h_p
                y_off = _dot(c_g, h_p, NT) * _pair_select(lo, exp_a, h0)
                s_new = _dot(xdt * _pair_select(lo, dte, h0), b_g, TN)
                dec_rows = jnp.where(top, dec[:, h0:h0 + 1], dec[:, h0 + 1:h0 + 2])
                state_ref[j] = h_p * dec_rows + s_new
                yacc_ref[:, cols] = (y_diag + y_off) + xs_p * d_ref[:, cols]

        y = yacc_ref[...]
        y_ref[...] = y
        zz = z_ref[...]
        y2 = y * (zz * _sigmoid(zz))
        gw = SSD_WIDTH // SSD_GROUPS
        for g in range(SSD_GROUPS):
            seg = y2[:, g * gw:(g + 1) * gw]
            r = lax.rsqrt(jnp.mean(seg * seg, axis=-1, keepdims=True) + EPS)
            yssd_ref[:, g * gw:(g + 1) * gw] = ((seg * r) * ng_ref[:, g * gw:(g + 1) * gw]).astype(yssd_ref.dtype)

    return pl.pallas_call(
        body, name="ssd_fwd", grid=(nc,),
        in_specs=[sp["z"], sp["xr"], sp["br"], sp["cr"], sp["dt"], sp["cwx"], sp["cwb"], sp["cwc"],
                  sp["cbx"], sp["cbb"], sp["cbc"], sp["vec128"], sp["vec128"], sp["vecw"], sp["vecw"]],
        out_specs=[sp["wide"], sp["wide"], sp["states"], pl.BlockSpec((L, SSD_CONV_CH), lambda c: (c, 0))],
        out_shape=[jax.ShapeDtypeStruct((t, SSD_WIDTH), MXU_DTYPE), jax.ShapeDtypeStruct((t, SSD_WIDTH), F32),
                   jax.ShapeDtypeStruct((nc, SSD_PAIRS, 128, SSD_STATE), F32), jax.ShapeDtypeStruct((t, SSD_CONV_CH), F32)],
        scratch_shapes=[pltpu.VMEM((8, SSD_WIDTH), F32), pltpu.VMEM((8, 512), F32), pltpu.VMEM((8, 512), F32),
                        pltpu.VMEM((SSD_PAIRS, 128, SSD_STATE), F32), pltpu.VMEM((L, SSD_WIDTH), F32)],
        compiler_params=_params(("arbitrary",)),
    )(proj_ssd, proj_ssd, proj_ssd, proj_ssd, dt_p, conv_w, conv_w, conv_w, conv_b, conv_b, conv_b,
      dtb, alog, d_row, ng_row)


def _ssd_bwd(proj_ssd, pre, dt_p, y, states, dyssd, conv_w, dtb, alog, d_row, ng_row):
    t = proj_ssd.shape[0]
    nc = t // SSD_CHUNK
    L = SSD_CHUNK
    sp = _ssd_specs(nc, True)

    def pre_spec(width, col):
        return pl.BlockSpec((L, width), lambda c: (nc - 1 - c, col))

    def body(z_ref, xr_ref, br_ref, cr_ref, px_ref, pb_ref, pc_ref, dt_ref, y_ref, st_ref, dy_ref,
             cwx_ref, cwb_ref, cwc_ref, dtb_ref, alog_ref, d_ref, ng_ref,
             dssd_ref, ddt_ref, gcw_ref, gcb_ref, gdtb_ref, galog_ref, gd_ref, gng_ref,
             gn_ref, nx_ref, nb_ref, ncc_ref, dxs_ref):
        step = pl.program_id(0)

        @pl.when(step == 0)
        def _():
            gn_ref[...] = jnp.zeros_like(gn_ref)
            nx_ref[...] = jnp.zeros_like(nx_ref)
            nb_ref[...] = jnp.zeros_like(nb_ref)
            ncc_ref[...] = jnp.zeros_like(ncc_ref)
            for ref in (gcw_ref, gcb_ref, gdtb_ref, galog_ref, gd_ref, gng_ref):
                ref[...] = jnp.zeros_like(ref)

        xr, br, cr = xr_ref[...], br_ref[...], cr_ref[...]
        cwx, cwb, cwc = cwx_ref[...], cwb_ref[...], cwc_ref[...]
        px, pb, pc = px_ref[...], pb_ref[...], pc_ref[...]
        sx, sb, sc = _sigmoid(px), _sigmoid(pb), _sigmoid(pc)
        xs, bm, cm = px * sx, pb * sb, pc * sc

        dt_in = dt_ref[...] + dtb_ref[...]
        dt, aneg, a_cs, a_t = _ssd_common(dt_ref[...], dtb_ref[...], alog_ref[...])
        exp_a = jnp.exp(a_cs)
        a_last = a_cs[L - 1:L, :]
        dte = jnp.exp(a_last - a_cs)
        dec = jnp.exp(a_last)

        lane = lax.broadcasted_iota(jnp.int32, (L, LANES), 1)
        sub = lax.broadcasted_iota(jnp.int32, (L, LANES), 0)
        lo = lane < SSD_HEAD_DIM
        causal = sub >= lane
        top = sub < SSD_HEAD_DIM
        last_row = sub == L - 1

        yv = y_ref[...]
        zz = z_ref[...]
        sz = _sigmoid(zz)
        silz = zz * sz
        y2 = yv * silz
        dyv = dy_ref[...]
        gw = SSD_WIDTH // SSD_GROUPS
        d_y2_parts = []
        gng_parts = []
        for g in range(SSD_GROUPS):
            seg = y2[:, g * gw:(g + 1) * gw]
            dseg = dyv[:, g * gw:(g + 1) * gw]
            r = lax.rsqrt(jnp.mean(seg * seg, axis=-1, keepdims=True) + EPS)
            n = seg * r
            dn = dseg * ng_ref[:, g * gw:(g + 1) * gw]
            gng_parts.append(jnp.sum(dseg * n, axis=0, keepdims=True))
            d_y2_parts.append(r * (dn - n * jnp.mean(dn * n, axis=-1, keepdims=True)))
        d_y2 = jnp.concatenate(d_y2_parts, axis=1)
        gng_ref[...] += jnp.concatenate(gng_parts, axis=1)
        d_y = d_y2 * silz
        dssd_ref[:, 0:SSD_WIDTH] = (d_y2 * yv * (sz * (1.0 + zz * (1.0 - sz)))).astype(dssd_ref.dtype)
        gd_ref[...] += jnp.sum(d_y * xs, axis=0, keepdims=True)
        dxs_ref[...] = d_y * d_ref[...]

        d_a = jnp.zeros((L, LANES), F32)
        d_at = jnp.zeros((LANES, L), F32)
        ddt = jnp.zeros((L, LANES), F32)
        d_b_parts, d_c_parts = [], []
        for g in range(SSD_GROUPS):
            b_g = bm[:, g * SSD_STATE:(g + 1) * SSD_STATE]
            c_g = cm[:, g * SSD_STATE:(g + 1) * SSD_STATE]
            cb = _dot(c_g, b_g, NT)
            d_cb = jnp.zeros((L, L), F32)
            d_bg = jnp.zeros((L, SSD_STATE), F32)
            d_cg = jnp.zeros((L, SSD_STATE), F32)
            for jj in range(PAIRS_PER_GROUP):
                j = g * PAIRS_PER_GROUP + jj
                h0 = 2 * j
                cols = slice(j * LANES, (j + 1) * LANES)
                dy_p = d_y[:, cols]
                xs_p = xs[:, cols]
                dt_pp = _pair_select(lo, dt, h0)
                expa_p = _pair_select(lo, exp_a, h0)
                dte_p = _pair_select(lo, dte, h0)
                xdt = xs_p * dt_pp
                l0 = jnp.where(causal, jnp.exp(a_cs[:, h0:h0 + 1] - a_t[h0:h0 + 1, :]), 0.0)
                l1 = jnp.where(causal, jnp.exp(a_cs[:, h0 + 1:h0 + 2] - a_t[h0 + 1:h0 + 2, :]), 0.0)
                g0, g1 = l0 * cb, l1 * cb
                h_p = st_ref[0, j]
                gn_p = gn_ref[j]
                dys = dy_p * expa_p
                d_cg = d_cg + _dot(dys, h_p, NN)
                d_h = _dot(dys, c_g, TN)
                t1 = dy_p * _dot(c_g, h_p, NT) * expa_p
                dw = _dot(b_g, gn_p, NT)
                d_bg = d_bg + _dot(xdt * dte_p, gn_p, NN)
                d_xdt = dw * dte_p
                t2 = d_xdt * xdt
                dyl, dyh = jnp.where(lo, dy_p, 0.0), jnp.where(lo, 0.0, dy_p)
                d_xdt = d_xdt + _dot(jnp.concatenate([g0, g1], axis=0), jnp.concatenate([dyl, dyh], axis=0), TN)
                dm0 = _dot(dyl, xdt, NT)
                dm1 = _dot(dyh, xdt, NT)
                d_cb = d_cb + (l0 * dm0 + l1 * dm1)
                e0, e1 = dm0 * g0, dm1 * g1
                a0, a1 = _halves(lo, t1 - t2)
                a0 = a0 + jnp.sum(e0, axis=1, keepdims=True)
                a1 = a1 + jnp.sum(e1, axis=1, keepdims=True)
                s0, s1 = _halves(lo, t2)
                gh = jnp.sum(gn_p * h_p, axis=1, keepdims=True)
                dd0 = jnp.sum(jnp.where(top[:, 0:1], gh, 0.0), axis=0, keepdims=True)
                dd1 = jnp.sum(jnp.where(top[:, 0:1], 0.0, gh), axis=0, keepdims=True)
                end0 = jnp.sum(s0, axis=0, keepdims=True) + dd0 * dec[:, h0:h0 + 1]
                end1 = jnp.sum(s1, axis=0, keepdims=True) + dd1 * dec[:, h0 + 1:h0 + 2]
                d_a = d_a + jnp.where(lane == h0, a0 + jnp.where(last_row, end0, 0.0), 0.0)
                d_a = d_a + jnp.where(lane == h0 + 1, a1 + jnp.where(last_row, end1, 0.0), 0.0)
                d_at = d_at - jnp.where(sub == h0, jnp.sum(e0, axis=0, keepdims=True), 0.0)
                d_at = d_at - jnp.where(sub == h0 + 1, jnp.sum(e1, axis=0, keepdims=True), 0.0)
                dec_rows = jnp.where(top, dec[:, h0:h0 + 1], dec[:, h0 + 1:h0 + 2])
                gn_ref[j] = d_h + dec_rows * gn_p
                q0, q1 = _halves(lo, d_xdt * xs_p)
                ddt = ddt + jnp.where(lane == h0, q0, 0.0) + jnp.where(lane == h0 + 1, q1, 0.0)
                dxs_ref[:, cols] += d_xdt * dt_pp
            d_cg = d_cg + _dot(d_cb, b_g, NN)
            d_bg = d_bg + _dot(d_cb, c_g, TN)
            d_b_parts.append(d_bg)
            d_c_parts.append(d_cg)

        rc = _rev_cumsum_rows(d_a + d_at.T)
        d_dt = rc * aneg + ddt
        galog_ref[...] += jnp.sum(rc * dt, axis=0, keepdims=True) * aneg
        d_dtraw = d_dt * _sigmoid(dt_in)
        gdtb_ref[...] += jnp.sum(d_dtraw, axis=0, keepdims=True)
        ddt_ref[...] = d_dtraw.astype(ddt_ref.dtype)

        def dsilu(p, s):
            return s * (1.0 + p * (1.0 - s))

        dcx = dxs_ref[...] * dsilu(px, sx)
        dcb = jnp.concatenate(d_b_parts, axis=1) * dsilu(pb, sb)
        dcc = jnp.concatenate(d_c_parts, axis=1) * dsilu(pc, sc)
        drx, gwx, gbx = _conv_backward(dcx, nx_ref[...], xr, cwx)
        drb, gwb, gbb = _conv_backward(dcb, nb_ref[...], br, cwb)
        drc, gwc, gbc = _conv_backward(dcc, ncc_ref[...], cr, cwc)
        nx_ref[...] = dcx[0:8, :]
        nb_ref[...] = dcb[0:8, :]
        ncc_ref[...] = dcc[0:8, :]
        dssd_ref[:, SSD_WIDTH:2 * SSD_WIDTH] = drx.astype(dssd_ref.dtype)
        dssd_ref[:, 2 * SSD_WIDTH:2 * SSD_WIDTH + 512] = drb.astype(dssd_ref.dtype)
        dssd_ref[:, 2 * SSD_WIDTH + 512:SSD_SEG] = drc.astype(dssd_ref.dtype)
        for k in range(CONV_WIDTH):
            gcw_ref[k:k + 1, :] += jnp.concatenate([gwx[k], gwb[k], gwc[k]], axis=1)
        gcb_ref[...] += jnp.concatenate([gbx, gbb, gbc], axis=1)

    const = lambda shape: pl.BlockSpec(shape, lambda c: (0,) * len(shape))
    return pl.pallas_call(
        body, name="ssd_bwd", grid=(nc,),
        in_specs=[sp["z"], sp["xr"], sp["br"], sp["cr"], pre_spec(SSD_WIDTH, 0), pre_spec(512, 4), pre_spec(512, 5),
                  sp["dt"], sp["wide"], sp["states"], sp["wide"],
                  sp["cwx"], sp["cwb"], sp["cwc"], sp["vec128"], sp["vec128"], sp["vecw"], sp["vecw"]],
        out_specs=[pl.BlockSpec((L, SSD_SEG), lambda c: (nc - 1 - c, 0)), sp["dt"],
                   const((CONV_WIDTH, SSD_CONV_CH)), const((1, SSD_CONV_CH)), const((1, LANES)), const((1, LANES)),
                   const((1, SSD_WIDTH)), const((1, SSD_WIDTH))],
        out_shape=[jax.ShapeDtypeStruct((t, SSD_SEG), MXU_DTYPE), jax.ShapeDtypeStruct((t, DT_PAD), MXU_DTYPE),
                   jax.ShapeDtypeStruct((CONV_WIDTH, SSD_CONV_CH), F32), jax.ShapeDtypeStruct((1, SSD_CONV_CH), F32),
                   jax.ShapeDtypeStruct((1, LANES), F32), jax.ShapeDtypeStruct((1, LANES), F32),
                   jax.ShapeDtypeStruct((1, SSD_WIDTH), F32), jax.ShapeDtypeStruct((1, SSD_WIDTH), F32)],
        scratch_shapes=[pltpu.VMEM((SSD_PAIRS, 128, SSD_STATE), F32), pltpu.VMEM((8, SSD_WIDTH), F32),
                        pltpu.VMEM((8, 512), F32), pltpu.VMEM((8, 512), F32), pltpu.VMEM((L, SSD_WIDTH), F32)],
        compiler_params=_params(("arbitrary",)),
    )(proj_ssd, proj_ssd, proj_ssd, proj_ssd, pre, pre, pre, dt_p, y, states, dyssd,
      conv_w, conv_w, conv_w, dtb, alog, d_row, ng_row)


def _lru_gates(xl, wa_ref, wx_ref, ba, bx, lam):
    pre_a, pre_x = [], []
    for g in range(LRU_NGROUPS):
        xg = xl[:, g * LRU_GROUP:(g + 1) * LRU_GROUP]
        pre_a.append(_dot(xg, wa_ref[g], NN))
        pre_x.append(_dot(xg, wx_ref[g], NN))
    r = _sigmoid(jnp.concatenate(pre_a, axis=1) + ba)
    i = _sigmoid(jnp.concatenate(pre_x, axis=1) + bx)
    log_a = (-LRU_C * r) * _softplus(-lam)
    a = jnp.exp(log_a)
    mult = jnp.sqrt(-jnp.tanh(log_a) * (a * a + 1.0))
    return r, i, log_a, a, mult


def _scan_rows(p, u, carry, reverse):
    rows, w = p.shape
    groups = rows // 8
    p3, u3 = p.reshape(groups, 8, w), u.reshape(groups, 8, w)
    row = lax.broadcasted_iota(jnp.int32, (groups, 8, w), 1)
    for s in (1, 2, 4):
        ok = row < 8 - s if reverse else row >= s
        shift = 8 - s if reverse else s
        u3 = p3 * jnp.where(ok, pltpu.roll(u3, shift, 1), 0.0) + u3
        p3 = p3 * jnp.where(ok, pltpu.roll(p3, shift, 1), 1.0)
    out = [None] * groups
    for k in (range(groups - 1, -1, -1) if reverse else range(groups)):
        out[k] = p3[k] * carry + u3[k]
        carry = out[k][0:1, :] if reverse else out[k][7:8, :]
    return jnp.concatenate(out, axis=0), carry


def _lru_fwd(proj_lru, conv_w, conv_b, wa, wx, ba, bx, lam):
    t = proj_lru.shape[0]
    rows = min(LRU_ROWS, t)
    nb = t // rows
    W = LRU_WIDTH

    def body(lg_ref, lx_ref, cw_ref, cb_ref, wa_ref, wx_ref, ba_ref, bx_ref, lam_ref, ylru_ref, h_ref,
             halo_ref, carry_ref):
        @pl.when(pl.program_id(0) == 0)
        def _():
            halo_ref[...] = jnp.zeros_like(halo_ref)
            carry_ref[...] = jnp.zeros_like(carry_ref)

        lx = lx_ref[...]
        xl = _causal_conv(lx, halo_ref[...], cw_ref[...], cb_ref[...])
        halo_ref[...] = lx[rows - 8:rows, :]
        _, i, _, a, mult = _lru_gates(xl, wa_ref, wx_ref, ba_ref[...], bx_ref[...], lam_ref[...])
        u = mult * (i * xl)
        h, carry_ref[...] = _scan_rows(a, u, carry_ref[...], False)
        h_ref[...] = h
        lg = lg_ref[...]
        ylru_ref[...] = (h * (lg * _sigmoid(lg))).astype(ylru_ref.dtype)

    const = lambda shape: pl.BlockSpec(shape, lambda b: (0,) * len(shape))
    return pl.pallas_call(
        body, name="lru_fwd", grid=(nb,),
        in_specs=[pl.BlockSpec((rows, W), lambda b: (b, 0)), pl.BlockSpec((rows, W), lambda b: (b, 1)),
                  const((CONV_WIDTH, W)), const((1, W)), const((LRU_NGROUPS, LRU_GROUP, LRU_GROUP)),
                  const((LRU_NGROUPS, LRU_GROUP, LRU_GROUP)), const((1, W)), const((1, W)), const((1, W))],
        out_specs=[pl.BlockSpec((rows, W), lambda b: (b, 0)), pl.BlockSpec((rows, W), lambda b: (b, 0))],
        out_shape=[jax.ShapeDtypeStruct((t, W), MXU_DTYPE), jax.ShapeDtypeStruct((t, W), F32)],
        scratch_shapes=[pltpu.VMEM((8, W), F32), pltpu.VMEM((1, W), F32)],
        compiler_params=_params(("arbitrary",)),
    )(proj_lru, proj_lru, conv_w, conv_b, wa, wx, ba, bx, lam)


def _lru_bwd(proj_lru, h, dylru, conv_w, conv_b, wa, wx, ba, bx, lam):
    t = proj_lru.shape[0]
    rows = min(LRU_ROWS, t)
    nb = t // rows
    W = LRU_WIDTH
    groups8 = rows // 8

    def rev(b):
        return nb - 1 - b

    def halo_spec(col):
        return pl.BlockSpec((8, W), lambda b: (jnp.maximum(rev(b) * groups8 - 1, 0), col))

    def body(lg_ref, lx_ref, hlx_ref, h_ref, hh_ref, dy_ref, cw_ref, cb_ref, wa_ref, wx_ref, ba_ref, bx_ref, lam_ref,
             dlru_ref, gcw_ref, gcb_ref, gba_ref, gbx_ref, glam_ref, gwa_ref, gwx_ref,
             gcarry_ref, afirst_ref, nxt_ref):
        step = pl.program_id(0)

        @pl.when(step == 0)
        def _():
            gcarry_ref[...] = jnp.zeros_like(gcarry_ref)
            afirst_ref[...] = jnp.zeros_like(afirst_ref)
            nxt_ref[...] = jnp.zeros_like(nxt_ref)
            for ref in (gcw_ref, gcb_ref, gba_ref, gbx_ref, glam_ref, gwa_ref, gwx_ref):
                ref[...] = jnp.zeros_like(ref)

        keep = jnp.where(step == nb - 1, 0.0, 1.0)
        lx = lx_ref[...]
        hlx = hlx_ref[...] * keep
        cw = cw_ref[...]
        xl = _causal_conv(lx, hlx, cw, cb_ref[...])
        lam = lam_ref[...]
        r, i, log_a, a, mult = _lru_gates(xl, wa_ref, wx_ref, ba_ref[...], bx_ref[...], lam)
        hv = h_ref[...]
        h_prev = _shift_down(hv, hh_ref[...] * keep, 1)
        lg = lg_ref[...]
        sg = _sigmoid(lg)
        dyv = dy_ref[...]
        d_h = dyv * (lg * sg)
        dlru_ref[:, 0:W] = (dyv * hv * (sg * (1.0 + lg * (1.0 - sg)))).astype(dlru_ref.dtype)

        row = lax.broadcasted_iota(jnp.int32, (rows, W), 0)
        p = jnp.where(row < rows - 1, pltpu.roll(a, rows - 1, 0), afirst_ref[...])
        gsc, gcarry_ref[...] = _scan_rows(p, d_h, gcarry_ref[...], True)
        afirst_ref[...] = a[0:1, :]

        d_a = gsc * h_prev
        v = i * xl
        d_mult = gsc * v
        d_v = gsc * mult
        d_i = d_v * xl
        d_xl = d_v * i
        d_la = d_a * a - d_mult * (a * a) / mult
        sp_neg = _softplus(-lam)
        d_r = d_la * (-LRU_C * sp_neg)
        glam_ref[...] += jnp.sum(d_la * r, axis=0, keepdims=True) * (LRU_C * _sigmoid(-lam))
        d_pa = d_r * r * (1.0 - r)
        d_px = d_i * i * (1.0 - i)
        gba_ref[...] += jnp.sum(d_pa, axis=0, keepdims=True)
        gbx_ref[...] += jnp.sum(d_px, axis=0, keepdims=True)
        parts = []
        for g in range(LRU_NGROUPS):
            cols = slice(g * LRU_GROUP, (g + 1) * LRU_GROUP)
            xg, dpa_g, dpx_g = xl[:, cols], d_pa[:, cols], d_px[:, cols]
            parts.append(_dot(dpa_g, wa_ref[g], NT) + _dot(dpx_g, wx_ref[g], NT))
            gwa_ref[g] += _dot(xg, dpa_g, TN)
            gwx_ref[g] += _dot(xg, dpx_g, TN)
        d_xl = d_xl + jnp.concatenate(parts, axis=1)
        d_lx, gw, gb = _conv_backward(d_xl, nxt_ref[...], lx, cw)
        nxt_ref[...] = d_xl[0:8, :]
        dlru_ref[:, W:2 * W] = d_lx.astype(dlru_ref.dtype)
        for k in range(CONV_WIDTH):
            gcw_ref[k:k + 1, :] += gw[k]
        gcb_ref[...] += gb

    const = lambda shape: pl.BlockSpec(shape, lambda b: (0,) * len(shape))
    wspec = const((LRU_NGROUPS, LRU_GROUP, LRU_GROUP))
    blk = lambda col: pl.BlockSpec((rows, W), lambda b: (rev(b), col))
    return pl.pallas_call(
        body, name="lru_bwd", grid=(nb,),
        in_specs=[blk(0), blk(1), halo_spec(1), blk(0), halo_spec(0), blk(0),
                  const((CONV_WIDTH, W)), const((1, W)), wspec, wspec, const((1, W)), const((1, W)), const((1, W))],
        out_specs=[pl.BlockSpec((rows, 2 * W), lambda b: (rev(b), 0)), const((CONV_WIDTH, W)), const((1, W)),
                   const((1, W)), const((1, W)), const((1, W)), wspec, wspec],
        out_shape=[jax.ShapeDtypeStruct((t, 2 * W), MXU_DTYPE), jax.ShapeDtypeStruct((CONV_WIDTH, W), F32),
                   jax.ShapeDtypeStruct((1, W), F32), jax.ShapeDtypeStruct((1, W), F32), jax.ShapeDtypeStruct((1, W), F32),
                   jax.ShapeDtypeStruct((1, W), F32), jax.ShapeDtypeStruct((LRU_NGROUPS, LRU_GROUP, LRU_GROUP), F32),
                   jax.ShapeDtypeStruct((LRU_NGROUPS, LRU_GROUP, LRU_GROUP), F32)],
        scratch_shapes=[pltpu.VMEM((1, W), F32), pltpu.VMEM((1, W), F32), pltpu.VMEM((8, W), F32)],
        compiler_params=_params(("arbitrary",)),
    )(proj_lru, proj_lru, proj_lru, h, h, dylru, conv_w, conv_b, wa, wx, ba, bx, lam)


def _mem_scores(q_h, k_h):
    s = _dot(q_h, k_h, NT) * (MEM_HEAD_DIM ** -0.5)
    s = s - jnp.max(s, axis=-1, keepdims=True)
    e = jnp.exp(s)
    return e / jnp.sum(e, axis=-1, keepdims=True)


def _mem_fwd(q, kv, rows=512):
    t = q.shape[0]
    rows = min(rows, t)
    m = kv.shape[0]

    def body(q_ref, kv_ref, y_ref):
        for hd in range(MEM_HEADS):
            cols = slice(hd * MEM_HEAD_DIM, (hd + 1) * MEM_HEAD_DIM)
            vcols = slice(D_MODEL + hd * MEM_HEAD_DIM, D_MODEL + (hd + 1) * MEM_HEAD_DIM)
            p = _mem_scores(q_ref[:, cols], kv_ref[:, cols])
            y_ref[:, cols] = _dot(p, kv_ref[:, vcols], NN).astype(y_ref.dtype)

    return pl.pallas_call(
        body, name="mem_fwd", grid=(t // rows,),
        in_specs=[pl.BlockSpec((rows, D_MODEL), lambda i: (i, 0)), pl.BlockSpec((m, 2 * D_MODEL), lambda i: (0, 0))],
        out_specs=pl.BlockSpec((rows, D_MODEL), lambda i: (i, 0)),
        out_shape=jax.ShapeDtypeStruct((t, D_MODEL), MXU_DTYPE),
        compiler_params=_params(("parallel",)),
    )(q, kv)


def _mem_bwd(q, kv, dy, rows=512):
    t = q.shape[0]
    rows = min(rows, t)
    m = kv.shape[0]

    def body(q_ref, kv_ref, dy_ref, dq_ref, dkv_ref):
        @pl.when(pl.program_id(0) == 0)
        def _():
            dkv_ref[...] = jnp.zeros_like(dkv_ref)

        for hd in range(MEM_HEADS):
            cols = slice(hd * MEM_HEAD_DIM, (hd + 1) * MEM_HEAD_DIM)
            vcols = slice(D_MODEL + hd * MEM_HEAD_DIM, D_MODEL + (hd + 1) * MEM_HEAD_DIM)
            q_h, k_h, dy_h = q_ref[:, cols], kv_ref[:, cols], dy_ref[:, cols]
            p = _mem_scores(q_h, k_h)
            dp = _dot(dy_h, kv_ref[:, vcols], NT)
            dkv_ref[:, vcols] += _dot(p, dy_h, TN)
            ds = p * (dp - jnp.sum(dp * p, axis=-1, keepdims=True)) * (MEM_HEAD_DIM ** -0.5)
            dq_ref[:, cols] = _dot(ds, k_h, NN).astype(dq_ref.dtype)
            dkv_ref[:, cols] += _dot(ds, q_h, TN)

    return pl.pallas_call(
        body, name="mem_bwd", grid=(t // rows,),
        in_specs=[pl.BlockSpec((rows, D_MODEL), lambda i: (i, 0)), pl.BlockSpec((m, 2 * D_MODEL), lambda i: (0, 0)),
                  pl.BlockSpec((rows, D_MODEL), lambda i: (i, 0))],
        out_specs=[pl.BlockSpec((rows, D_MODEL), lambda i: (i, 0)), pl.BlockSpec((m, 2 * D_MODEL), lambda i: (0, 0))],
        out_shape=[jax.ShapeDtypeStruct((t, D_MODEL), MXU_DTYPE), jax.ShapeDtypeStruct((m, 2 * D_MODEL), F32)],
        compiler_params=_params(("arbitrary",)),
    )(q, kv, dy)


def _merge_fwd(x, yssd, ylru, ymem, gl, w_bs, w_bl, w_bm, w_out, fg, tgt, rows=256):
    t = x.shape[0]
    rows = min(rows, t)
    D = D_MODEL

    def body(x_ref, ys_ref, yl_ref, ym_ref, gl_ref, wbs_ref, wbl_ref, wbm_ref, wo_ref, fg_ref, tgt_ref,
             ps_ref, pl_ref, pm_ref, mg_ref, dx2_ref, loss_ref, gfg_ref):
        @pl.when(pl.program_id(0) == 0)
        def _():
            loss_ref[...] = jnp.zeros_like(loss_ref)
            gfg_ref[...] = jnp.zeros_like(gfg_ref)

        ps = _dot(ys_ref[...], wbs_ref[...], NN)
        pl_ = _dot(yl_ref[...], wbl_ref[...], NN)
        pm = _dot(ym_ref[...], wbm_ref[...], NN)
        ps_ref[...] = ps
        pl_ref[...] = pl_
        pm_ref[...] = pm
        merged = (_sigmoid(gl_ref[:, 0:D]) * ps + _sigmoid(gl_ref[:, D:2 * D]) * pl_) + _sigmoid(gl_ref[:, 2 * D:3 * D]) * pm
        mg_ref[...] = merged.astype(mg_ref.dtype)
        x2 = x_ref[...] + _dot(merged, wo_ref[...], NN)
        r2 = lax.rsqrt(jnp.mean(x2 * x2, axis=-1, keepdims=True) + EPS)
        xn = x2 * r2
        fg = fg_ref[...]
        diff = xn * fg - tgt_ref[...]
        tile_loss = 0.5 * jnp.sum(jnp.mean(diff * diff, axis=-1, keepdims=True), axis=0, keepdims=True)
        loss_ref[...] += jnp.broadcast_to(tile_loss, loss_ref.shape)
        d_out = diff * (1.0 / D)
        gfg_ref[...] += jnp.sum(d_out * xn, axis=0, keepdims=True)
        dxn = d_out * fg
        dx2_ref[...] = r2 * (dxn - xn * jnp.mean(dxn * xn, axis=-1, keepdims=True))

    row = lambda w: pl.BlockSpec((rows, w), lambda i: (i, 0))
    const = lambda shape: pl.BlockSpec(shape, lambda i: (0,) * len(shape))
    return pl.pallas_call(
        body, name="merge_fwd", grid=(t // rows,),
        in_specs=[row(D), row(SSD_WIDTH), row(LRU_WIDTH), row(D), row(3 * D), const((SSD_WIDTH, D)), const((LRU_WIDTH, D)),
                  const((D, D)), const((D, D)), const((1, D)), row(D)],
        out_specs=[row(D), row(D), row(D), row(D), row(D), const((1, LANES)), const((1, D))],
        out_shape=[jax.ShapeDtypeStruct((t, D), F32), jax.ShapeDtypeStruct((t, D), F32), jax.ShapeDtypeStruct((t, D), F32),
                   jax.ShapeDtypeStruct((t, D), MXU_DTYPE), jax.ShapeDtypeStruct((t, D), F32),
                   jax.ShapeDtypeStruct((1, LANES), F32), jax.ShapeDtypeStruct((1, D), F32)],
        compiler_params=_params(("arbitrary",)),
    )(x, yssd, ylru, ymem, gl, w_bs, w_bl, w_bm, w_out, fg, tgt)


def _merge_bwd(dx2, gl, ps, pl_in, pm, w_bs, w_bl, w_bm, w_out, rows=256):
    t = dx2.shape[0]
    rows = min(rows, t)
    D = D_MODEL

    def body(dx2_ref, gl_ref, ps_ref, pl_ref, pm_ref, wbs_ref, wbl_ref, wbm_ref, wo_ref,
             dg_ref, dps_ref, dpl_ref, dpm_ref, dys_ref, dyl_ref, dym_ref):
        dm = _dot(dx2_ref[...], wo_ref[...], NT)
        for idx, (p_ref, dp_ref, w_ref, dy_ref) in enumerate(
                ((ps_ref, dps_ref, wbs_ref, dys_ref), (pl_ref, dpl_ref, wbl_ref, dyl_ref), (pm_ref, dpm_ref, wbm_ref, dym_ref))):
            gate = _sigmoid(gl_ref[:, idx * D:(idx + 1) * D])
            dg_ref[:, idx * D:(idx + 1) * D] = ((dm * p_ref[...]) * gate * (1.0 - gate)).astype(dg_ref.dtype)
            dp = dm * gate
            dp_ref[...] = dp.astype(dp_ref.dtype)
            dy_ref[...] = _dot(dp, w_ref[...], NT)

    row = lambda w: pl.BlockSpec((rows, w), lambda i: (i, 0))
    const = lambda shape: pl.BlockSpec(shape, lambda i: (0,) * len(shape))
    return pl.pallas_call(
        body, name="merge_bwd", grid=(t // rows,),
        in_specs=[row(D), row(3 * D), row(D), row(D), row(D), const((SSD_WIDTH, D)), const((LRU_WIDTH, D)),
                  const((D, D)), const((D, D))],
        out_specs=[row(3 * D), row(D), row(D), row(D), row(SSD_WIDTH), row(LRU_WIDTH), row(D)],
        out_shape=[jax.ShapeDtypeStruct((t, 3 * D), MXU_DTYPE), jax.ShapeDtypeStruct((t, D), MXU_DTYPE),
                   jax.ShapeDtypeStruct((t, D), MXU_DTYPE), jax.ShapeDtypeStruct((t, D), MXU_DTYPE),
                   jax.ShapeDtypeStruct((t, SSD_WIDTH), F32), jax.ShapeDtypeStruct((t, LRU_WIDTH), F32),
                   jax.ShapeDtypeStruct((t, D), F32)],
        compiler_params=_params(("parallel",)),
    )(dx2, gl, ps, pl_in, pm, w_bs, w_bl, w_bm, w_out)


def _mesh_place():
    x, y, c = lax.axis_index("x"), lax.axis_index("y"), lax.axis_index("c")
    return x, y, c, 4 * x + 2 * y + c


def _other_chips(x, y):
    return [(1 - x, y), (x, 1 - y), (1 - x, 1 - y)]


def _all_gather(arrs, name):
    n = len(arrs)

    def body(*refs):
        ins, outs = refs[:n], refs[n:2 * n]
        send_sems, recv_sems, local_sems = refs[2 * n:]
        x, y, c, me = _mesh_place()
        sibling = (x, y, 1 - c)
        chips = _other_chips(x, y)

        def slot(px, py, pc):
            return 4 * px + 2 * py + pc

        def copy(a, k, block, to, src=None):
            return pltpu.make_async_remote_copy(
                src_ref=outs[a].at[block] if src is None else src, dst_ref=outs[a].at[block],
                send_sem=send_sems.at[a, k], recv_sem=recv_sems.at[a, k], device_id=to, device_id_type=pl.DeviceIdType.MESH)

        local = [pltpu.make_async_copy(ins[a], outs[a].at[me], local_sems.at[a]) for a in range(n)]
        for cp in local:
            cp.start()
        sends = []
        for a in range(n):
            sends.append(copy(a, 0, me, sibling, src=ins[a]))
            for j, chip in enumerate(chips):
                sends.append(copy(a, 1 + j, me, (*chip, c), src=ins[a]))
        for cp in sends:
            cp.start()
        for j, chip in enumerate(chips):
            for a in range(n):
                copy(a, 1 + j, slot(*chip, c), sibling).wait_recv()
                passed = copy(a, 4 + j, slot(*chip, c), sibling)
                passed.start()
                sends.append(passed)
        for a in range(n):
            copy(a, 0, slot(x, y, 1 - c), sibling).wait_recv()
        for j, chip in enumerate(chips):
            for a in range(n):
                copy(a, 4 + j, slot(*chip, 1 - c), sibling).wait_recv()
        for cp in sends:
            cp.wait_send()
        for cp in local:
            cp.wait()

    any_spec = pl.BlockSpec(memory_space=pl.ANY)
    return pl.pallas_call(
        body, name=name, in_specs=[any_spec] * n, out_specs=[any_spec] * n,
        out_shape=[jax.ShapeDtypeStruct((N_DEV,) + a.shape, a.dtype) for a in arrs],
        scratch_shapes=[pltpu.SemaphoreType.DMA((n, 7)), pltpu.SemaphoreType.DMA((n, 7)), pltpu.SemaphoreType.DMA((n,))],
    )(*arrs)


N_CHIPS = 4


def _pair_exchange(parts):
    n = len(parts)

    def body(*refs):
        ins, outs = refs[:n], refs[n:2 * n]
        send_sems, recv_sems = refs[2 * n:]
        x, y, c, _ = _mesh_place()
        sibling = (x, y, 1 - c)
        sends = []
        for a in range(n):
            for q in range(N_CHIPS):
                cp = pltpu.make_async_remote_copy(src_ref=ins[a].at[q, 1 - c], dst_ref=outs[a].at[q], send_sem=send_sems.at[a, q],
                                                  recv_sem=recv_sems.at[a, q], device_id=sibling, device_id_type=pl.DeviceIdType.MESH)
                cp.start()
                sends.append(cp)
        for cp in sends:
            cp.wait_recv()
        for cp in sends:
            cp.wait_send()

    any_spec = pl.BlockSpec(memory_space=pl.ANY)
    return pl.pallas_call(
        body, name="grad_pair_exchange", in_specs=[any_spec] * n, out_specs=[any_spec] * n,
        out_shape=[jax.ShapeDtypeStruct((N_CHIPS,) + a.shape[2:], a.dtype) for a in parts],
        scratch_shapes=[pltpu.SemaphoreType.DMA((n, N_CHIPS)), pltpu.SemaphoreType.DMA((n, N_CHIPS))],
    )(*parts)


def _col_tile(r, c, limit_bytes):
    assert c % LANES == 0, c
    best = LANES
    for cand in range(LANES, c + 1, LANES):
        if c % cand == 0 and r * cand * 4 <= limit_bytes:
            best = cand
    return best


def _chip_sum(part, recv, core, name):
    _, _, r, c = part.shape
    ct = _col_tile(r, c, 2 << 20)

    def body(core_ref, p_ref, r_ref, s_ref, t_ref):
        s = p_ref[...] + r_ref[...]
        s_ref[...] = s
        t_ref[...] = s.astype(t_ref.dtype)

    blk = pl.BlockSpec((None, r, ct), lambda q, i, core_ref: (q, 0, i))
    return pl.pallas_call(
        body, name=name,
        grid_spec=pltpu.PrefetchScalarGridSpec(
            num_scalar_prefetch=1, grid=(N_CHIPS, c // ct),
            in_specs=[pl.BlockSpec((None, None, r, ct), lambda q, i, core_ref: (q, core_ref[0], 0, i)), blk],
            out_specs=[blk, blk]),
        out_shape=[jax.ShapeDtypeStruct((N_CHIPS, r, c), F32), jax.ShapeDtypeStruct((N_CHIPS, r, c), GRAD_WIRE_DTYPE)],
        compiler_params=_params(("parallel", "parallel")),
    )(core, part, recv)


def _chip_exchange(sums):
    n = len(sums)

    def body(*refs):
        ins, outs = refs[:n], refs[n:2 * n]
        send_sems, recv_sems = refs[2 * n:]
        x, y, c, _ = _mesh_place()
        my_chip = 2 * x + y
        sends = []
        for a in range(n):
            for j, (px, py) in enumerate(_other_chips(x, y)):
                cp = pltpu.make_async_remote_copy(src_ref=ins[a].at[2 * px + py], dst_ref=outs[a].at[my_chip], send_sem=send_sems.at[a, j],
                                                  recv_sem=recv_sems.at[a, j], device_id=(px, py, c), device_id_type=pl.DeviceIdType.MESH)
                cp.start()
                sends.append(cp)
        for a in range(n):
            for j, (px, py) in enumerate(_other_chips(x, y)):
                pltpu.make_async_remote_copy(src_ref=ins[a].at[my_chip], dst_ref=outs[a].at[2 * px + py], send_sem=send_sems.at[a, j],
                                             recv_sem=recv_sems.at[a, j], device_id=(px, py, c),
                                             device_id_type=pl.DeviceIdType.MESH).wait_recv()
        for cp in sends:
            cp.wait_send()

    any_spec = pl.BlockSpec(memory_space=pl.ANY)
    return pl.pallas_call(
        body, name="grad_chip_exchange", in_specs=[any_spec] * n, out_specs=[any_spec] * n,
        out_shape=[jax.ShapeDtypeStruct(a.shape, a.dtype) for a in sums],
        scratch_shapes=[pltpu.SemaphoreType.DMA((n, 3)), pltpu.SemaphoreType.DMA((n, 3))],
    )(*sums)


def _row_tile(r, limit):
    if r <= limit:
        return r
    best = 8
    for cand in range(8, limit + 1, 8):
        if r % cand == 0:
            best = cand
    assert r % best == 0, r
    return best


def _adam_update(w, g, m, v):
    nm = ADAM_B1 * m + (1.0 - ADAM_B1) * g
    nv = ADAM_B2 * v + (1.0 - ADAM_B2) * (g * g)
    m_hat = nm / (1.0 - ADAM_B1 ** ADAM_STEP)
    v_hat = nv / (1.0 - ADAM_B2 ** ADAM_STEP)
    return -ADAM_LR * (m_hat / (jnp.sqrt(v_hat) + ADAM_EPS) + ADAM_WD * w), nm, nv


def _sum_adamw(own, recv, chip, w, m, v, name):
    _, r, c = own.shape
    ct = _col_tile(r, c, 1 << 20)

    def body(chip_ref, o_ref, r1_ref, r2_ref, r3_ref, w_ref, m_ref, v_ref, g_ref, d_ref, nm_ref, nv_ref):
        g = ((o_ref[...] + r1_ref[...].astype(F32)) + r2_ref[...].astype(F32)) + r3_ref[...].astype(F32)
        g_ref[...] = g
        d_ref[...], nm_ref[...], nv_ref[...] = _adam_update(w_ref[...], g, m_ref[...], v_ref[...])

    def slot(k):
        return pl.BlockSpec((None, r, ct), lambda i, chip_ref: ((chip_ref[0] + k) % N_CHIPS, 0, i))

    spec = pl.BlockSpec((r, ct), lambda i, chip_ref: (0, i))
    shape = jax.ShapeDtypeStruct((r, c), F32)
    return pl.pallas_call(
        body, name=name,
        grid_spec=pltpu.PrefetchScalarGridSpec(
            num_scalar_prefetch=1, grid=(c // ct,),
            in_specs=[slot(0), slot(1), slot(2), slot(3), spec, spec, spec], out_specs=[spec] * 4),
        out_shape=[shape] * 4,
        compiler_params=_params(("parallel",)),
    )(chip, own, recv, recv, recv, w, m, v)


def _small_adamw(parts, ws, ms, vs):
    n = len(parts)

    def body(*refs):
        p_refs, w_refs, m_refs, v_refs = refs[:n], refs[n:2 * n], refs[2 * n:3 * n], refs[3 * n:4 * n]
        outs = refs[4 * n:]
        for i in range(n):
            g = p_refs[i][0]
            for k in range(1, N_DEV):
                g = g + p_refs[i][k]
            outs[i][...] = g
            outs[n + i][...], outs[2 * n + i][...], outs[3 * n + i][...] = _adam_update(
                w_refs[i][...], g, m_refs[i][...], v_refs[i][...])

    vmem = pl.BlockSpec(memory_space=pltpu.VMEM)
    shapes = [jax.ShapeDtypeStruct(w.shape, F32) for w in ws]
    res = pl.pallas_call(
        body, name="adamw_small", in_specs=[vmem] * (4 * n), out_specs=[vmem] * (4 * n), out_shape=shapes * 4,
        compiler_params=pltpu.CompilerParams(vmem_limit_bytes=VMEM_LIMIT),
    )(*parts, *ws, *ms, *vs)
    return res[:n], res[n:2 * n], res[2 * n:3 * n], res[3 * n:]


def _pack(arrs, dtype, row_multiple):
    flat = jnp.concatenate([a.reshape(-1).astype(dtype) for a in arrs])
    unit = LANES * row_multiple
    padded = -(-flat.shape[0] // unit) * unit
    return jnp.pad(flat, (0, padded - flat.shape[0])).reshape(-1, LANES)


def _unpack(packed, shapes, lead=()):
    flat = packed.reshape(lead + (-1,))
    out, off = [], 0
    for shp in shapes:
        n = math.prod(shp)
        out.append(flat[..., off:off + n].reshape(lead + tuple(shp)))
        off += n
    return out


def _gather_cols(g, lo, hi):
    width = g.shape[2]
    pieces = []
    for s in range(N_DEV):
        a, e = max(lo, s * width), min(hi, (s + 1) * width)
        if a < e:
            pieces.append(g[s, :, a - s * width:e - s * width])
    return pieces[0] if len(pieces) == 1 else jnp.concatenate(pieces, axis=1)


def _scatter_cols(segs, width):
    slots = []
    for k in range(N_DEV):
        lo, hi = k * width, (k + 1) * width
        pieces = []
        for arr, s_lo, s_hi in segs:
            a, e = max(lo, s_lo), min(hi, s_hi)
            if a < e:
                pieces.append(arr[:, a - s_lo:e - s_lo])
        slots.append(pieces[0] if len(pieces) == 1 else jnp.concatenate(pieces, axis=1))
    return jnp.stack(slots)


def _block_diag_groups(w):
    w4 = w.reshape(LRU_NGROUPS, 4, LRU_BLOCK, LRU_BLOCK)
    eye = jnp.eye(4, dtype=w.dtype)
    return jnp.einsum("gaij,ab->gaibj", w4, eye).reshape(LRU_NGROUPS, LRU_GROUP, LRU_GROUP)


def _block_diag_extract(wg):
    w5 = wg.reshape(LRU_NGROUPS, 4, LRU_BLOCK, 4, LRU_BLOCK)
    idx = jnp.arange(4)
    return w5[:, idx, :, idx, :].transpose(1, 0, 2, 3).reshape(LRU_BLOCKS, LRU_BLOCK, LRU_BLOCK)


BIG = ("w_in", "w_kv", "w_br_ssd", "w_br_lru", "w_br_mem", "w_out")
SMALL_SHARDED = ("ssd_conv_w", "ssd_norm_g", "lru_conv_w")
REPLICATED = ("norm_g", "ssd_conv_b", "ssd_dt_bias", "ssd_a_log", "ssd_d", "lru_conv_b", "lru_w_a", "lru_b_a",
              "lru_w_x", "lru_b_x", "lru_lambda", "mem_norm_g", "final_g")
WEIGHTS = ("norm_g", "w_in", "ssd_conv_w", "ssd_conv_b", "ssd_dt_bias", "ssd_a_log", "ssd_d", "ssd_norm_g", "lru_conv_w",
           "lru_conv_b", "lru_w_a", "lru_b_a", "lru_w_x", "lru_b_x", "lru_lambda", "mem_norm_g", "w_kv", "w_br_ssd",
           "w_br_lru", "w_br_mem", "w_out", "final_g")


def kernel(x, mem, norm_g, w_in, ssd_conv_w, ssd_conv_b, ssd_dt_bias, ssd_a_log, ssd_d, ssd_norm_g, lru_conv_w, lru_conv_b, lru_w_a, lru_b_a, lru_w_x, lru_b_x, lru_lambda, mem_norm_g, w_kv, w_br_ssd, w_br_lru, w_br_mem, w_out, final_g, loss_target, m_norm_g, m_w_in, m_ssd_conv_w, m_ssd_conv_b, m_ssd_dt_bias, m_ssd_a_log, m_ssd_d, m_ssd_norm_g, m_lru_conv_w, m_lru_conv_b, m_lru_w_a, m_lru_b_a, m_lru_w_x, m_lru_b_x, m_lru_lambda, m_mem_norm_g, m_w_kv, m_w_br_ssd, m_w_br_lru, m_w_br_mem, m_w_out, m_final_g, v_norm_g, v_w_in, v_ssd_conv_w, v_ssd_conv_b, v_ssd_dt_bias, v_ssd_a_log, v_ssd_d, v_ssd_norm_g, v_lru_conv_w, v_lru_conv_b, v_lru_w_a, v_lru_b_a, v_lru_w_x, v_lru_b_x, v_lru_lambda, v_mem_norm_g, v_w_kv, v_w_br_ssd, v_w_br_lru, v_w_br_mem, v_w_out, v_final_g):
    env = dict(locals())
    W = {n: env[n] for n in WEIGHTS}
    M = {n: env["m_" + n] for n in WEIGHTS}
    V = {n: env["v_" + n] for n in WEIGHTS}
    me = 4 * lax.axis_index("x") + 2 * lax.axis_index("y") + lax.axis_index("c")
    t = x.shape[1]
    xt = x[0]
    memt = mem[0]
    tgt = loss_target[0]

    small_shapes = [W[n].shape for n in SMALL_SHARDED]
    as2d = lambda d, n: jnp.transpose(d[n][0]) if n == "w_in" else d[n][0]
    gathered = _all_gather([as2d(W, n).astype(MXU_DTYPE) for n in BIG] + [_pack([W[n] for n in SMALL_SHARDED], F32, 8)],
                           "weights_all_gather")
    g_in, g_kv, g_bs, g_bl, g_bm, g_out, gs = gathered
    g_cw, g_ng, g_lcw = _unpack(gs, small_shapes, (N_DEV,))
    cols = lambda a: jnp.moveaxis(a[:, 0], 0, -2).reshape(a.shape[2:-1] + (-1,))
    rows_ = lambda a: a.reshape((-1,) + a.shape[2:])
    w_bs_f, w_bl_f, w_bm_f, w_out_f = rows_(g_bs), rows_(g_bl), rows_(g_bm), rows_(g_out)
    conv_w_f, ssd_ng_f, lru_cw_f = cols(g_cw), cols(g_ng), cols(g_lcw)
    b = SEG_BOUNDS
    w_kv_f = _gather_cols(g_kv, 0, 2 * D_MODEL)
    w_in_t = g_in.reshape(IN_WIDTH, D_MODEL)
    w_ssd, w_lru, w_q, w_g = w_in_t[b[0]:b[1]], w_in_t[b[2]:b[3]], w_in_t[b[3]:b[4]], w_in_t[b[4]:b[5]]
    w_dt = jnp.pad(w_in_t[b[1]:b[2]], ((0, DT_PAD - SSD_HEADS), (0, 0)))

    pad_heads = lambda a: jnp.pad(a, ((0, 0), (0, LANES - SSD_HEADS)))
    dtb, alog = pad_heads(ssd_dt_bias), pad_heads(ssd_a_log)
    d_row = jnp.repeat(ssd_d, SSD_HEAD_DIM, axis=1)
    ng_row = ssd_ng_f.reshape(1, SSD_WIDTH)
    wa_g, wx_g = _block_diag_groups(lru_w_a[0]), _block_diag_groups(lru_w_x[0])
    ba, bx = lru_b_a.reshape(1, LRU_WIDTH), lru_b_x.reshape(1, LRU_WIDTH)
    fg = final_g.reshape(1, D_MODEL)

    h = _rms_fwd(xt, norm_g, "norm_fwd")
    proj_ssd = _matmul(h, w_ssd, "nt", "proj_ssd", tm=2048)
    proj_lru = _matmul(h, w_lru, "nt", "proj_lru", tm=2048)
    proj_q = _matmul(h, w_q, "nt", "proj_q", tm=2048)
    proj_g = _matmul(h, w_g, "nt", "proj_g", tm=2048)
    proj_dt = _matmul(h, w_dt, "nt", "proj_dt", tm=2048)
    mem_n = _rms_fwd(memt, mem_norm_g, "mem_norm_fwd")
    kv = _matmul(mem_n, w_kv_f, "nn", "mem_kv")
    yssd, y_scan, states, ssd_pre = _ssd_fwd(proj_ssd, proj_dt, conv_w_f, ssd_conv_b, dtb, alog, d_row, ng_row)
    ylru, h_lru = _lru_fwd(proj_lru, lru_cw_f, lru_conv_b, wa_g, wx_g, ba, bx, lru_lambda)
    ymem = _mem_fwd(proj_q, kv)
    ps, pl_, pm, merged, dx2, loss_vec, g_fg = _merge_fwd(xt, yssd, ylru, ymem, proj_g, w_bs_f, w_bl_f, w_bm_f, w_out_f, fg, tgt)

    d_g, dps, dpl, dpm, dyssd, dylru, dymem = _merge_bwd(dx2, proj_g, ps, pl_, pm, w_bs_f, w_bl_f, w_bm_f, w_out_f)
    gw_out = _matmul(merged, dx2, "tn", "grad_w_out", tk=2048)
    gw_bs = _matmul(yssd, dps, "tn", "grad_w_br_ssd", tk=2048)
    gw_bl = _matmul(ylru, dpl, "tn", "grad_w_br_lru", tm=LRU_WIDTH, tk=2048)
    gw_bm = _matmul(ymem, dpm, "tn", "grad_w_br_mem", tk=2048)
    d_q, d_kv = _mem_bwd(proj_q, kv, dymem)
    gw_kv = _matmul(mem_n, d_kv, "tn", "grad_w_kv")
    d_memn = _matmul(d_kv, w_kv_f, "nt", "d_mem_n")
    _, g_memng = _rms_bwd(memt, d_memn, None, mem_norm_g, "mem_norm_bwd")
    d_lru, gl_cw, gl_cb, g_ba, g_bx, g_lam, gwa_g, gwx_g = _lru_bwd(proj_lru, h_lru, dylru, lru_cw_f, lru_conv_b, wa_g, wx_g, ba, bx, lru_lambda)
    d_ssd, d_dt, gs_cw, gs_cb, g_dtb, g_alog, g_dch, g_ngrow = _ssd_bwd(proj_ssd, ssd_pre, proj_dt, y_scan, states, dyssd, conv_w_f, dtb, alog, d_row, ng_row)
    grad_x, g_normg = _dh_norm_bwd([(d_ssd, w_ssd), (d_lru, w_lru), (d_q, w_q), (d_g, w_g), (d_dt, w_dt)], xt, dx2, norm_g)
    gw_ssd = _matmul(d_ssd, h, "tn", "grad_w_in_ssd", tk=2048)
    gw_lru = _matmul(d_lru, h, "tn", "grad_w_in_lru", tk=2048)
    gw_q = _matmul(d_q, h, "tn", "grad_w_in_q", tk=2048)
    gw_g = _matmul(d_g, h, "tn", "grad_w_in_g", tk=2048)
    gw_dt = _matmul(d_dt, h, "tn", "grad_w_in_dt", tk=2048)

    split_rows = lambda a: a.reshape((N_DEV, -1) + a.shape[1:])
    gw_in_t = jnp.concatenate([gw_ssd, gw_dt[:SSD_HEADS], gw_lru, gw_q, gw_g], axis=0)
    big_send = [split_rows(gw_in_t), _scatter_cols([(gw_kv, 0, 2 * D_MODEL)], 2 * D_MODEL // N_DEV),
                split_rows(gw_bs), split_rows(gw_bl), split_rows(gw_bm), split_rows(gw_out)]

    small_grads = {
        "norm_g": g_normg, "ssd_conv_w": gs_cw, "ssd_conv_b": gs_cb, "ssd_dt_bias": g_dtb[:, :SSD_HEADS],
        "ssd_a_log": g_alog[:, :SSD_HEADS], "ssd_d": jnp.sum(g_dch.reshape(SSD_HEADS, SSD_HEAD_DIM), axis=1).reshape(1, SSD_HEADS),
        "ssd_norm_g": g_ngrow.reshape(SSD_GROUPS, -1), "lru_conv_w": gl_cw, "lru_conv_b": gl_cb,
        "lru_w_a": _block_diag_extract(gwa_g), "lru_b_a": g_ba, "lru_w_x": _block_diag_extract(gwx_g), "lru_b_x": g_bx,
        "lru_lambda": g_lam, "mem_norm_g": g_memng, "final_g": g_fg,
    }
    small_all = REPLICATED + SMALL_SHARDED

    def small_shape(n, shards):
        shp = W[n].shape[1:] if W[n].ndim > 2 else (1, W[n].shape[-1])
        return shp[:-1] + (shp[-1] * shards,)

    small_recv = _all_gather([small_grads[n].reshape(small_shape(n, N_DEV if n in SMALL_SHARDED else 1)) for n in small_all],
                             "small_grads_all_gather")

    core = lax.axis_index("c").astype(jnp.int32).reshape(1)
    chip = (2 * lax.axis_index("x") + lax.axis_index("y")).astype(jnp.int32).reshape(1)
    by_chip = [a.reshape((N_CHIPS, 2) + a.shape[1:]) for a in big_send]
    from_sibling = _pair_exchange(by_chip)
    chip_sums = [_chip_sum(p, r, core, "chip_sum_" + n) for n, p, r in zip(BIG, by_chip, from_sibling)]
    from_chips = _chip_exchange([s16 for _, s16 in chip_sums])

    grads, delta, new_m, new_v = {}, {}, {}, {}
    for n, (s32, _), recv in zip(BIG, chip_sums, from_chips):
        res = _sum_adamw(s32, recv, chip, as2d(W, n), as2d(M, n), as2d(V, n), "adamw_" + n)
        for dst, a in zip((grads, delta, new_m, new_v), res):
            dst[n] = (jnp.transpose(a) if n == "w_in" else a)[None]

    parts = []
    for n, a in zip(small_all, small_recv):
        if n in SMALL_SHARDED:
            width = W[n].shape[-1]
            a = lax.dynamic_slice_in_dim(a, me * width, width, axis=a.ndim - 1)
        parts.append(a)
    canon = lambda d: [d[n].reshape(small_shape(n, 1)) for n in small_all]
    for dst, res in zip((grads, delta, new_m, new_v), _small_adamw(parts, canon(W), canon(M), canon(V))):
        for n, a in zip(small_all, res):
            dst[n] = a.reshape(W[n].shape)

    loss = lax.psum(loss_vec[0, 0], ("x", "y", "c"))
    return (loss, grad_x[None], *[grads[n] for n in WEIGHTS], *[delta[n] for n in WEIGHTS],
            *[new_m[n] for n in WEIGHTS], *[new_v[n] for n in WEIGHTS])
```

```python
import functools
import math

import jax
import jax.numpy as jnp
from jax import lax
from jax.experimental import pallas as pl
from jax.experimental.pallas import tpu as pltpu

F32 = jnp.float32
MXU_DTYPE = jnp.bfloat16
GRAD_WIRE_DTYPE = jnp.bfloat16

D_MODEL = 1024
EPS = 1e-6
CONV_WIDTH = 4
SSD_WIDTH = 2048
SSD_HEAD_DIM = 64
SSD_HEADS = 32
SSD_GROUPS = 4
SSD_STATE = 128
SSD_CHUNK = 128
SSD_CONV_CH = SSD_WIDTH + 2 * SSD_GROUPS * SSD_STATE
SSD_PAIRS = SSD_HEADS // 2
PAIRS_PER_GROUP = SSD_PAIRS // SSD_GROUPS
GROUP_COLS = SSD_WIDTH // SSD_GROUPS
LRU_WIDTH = 1536
LRU_BLOCKS = 16
LRU_BLOCK = 96
LRU_GROUP = 4 * LRU_BLOCK
LRU_NGROUPS = LRU_WIDTH // LRU_GROUP
LRU_C = 8.0
LRU_ROWS = 256
MEM_HEADS = 4
MEM_HEAD_DIM = 256
IN_WIDTH = 12320
N_DEV = 8
LANES = 128
SSD_SEG = SSD_WIDTH + SSD_CONV_CH
DT_PAD = LANES
SEG_BOUNDS = (0, 5120, 5152, 8224, 9248, 12320)

ADAM_LR = 0.001
ADAM_B1 = 0.9
ADAM_B2 = 0.999
ADAM_EPS = 1e-08
ADAM_WD = 0.01
ADAM_STEP = 10

VMEM_LIMIT = 56 * 1024 * 1024

NN = (((1,), (0,)), ((), ()))
NT = (((1,), (1,)), ((), ()))
TN = (((0,), (0,)), ((), ()))


def _dot(a, b, dims):
    return lax.dot_general(a.astype(MXU_DTYPE), b.astype(MXU_DTYPE), dims, preferred_element_type=F32)


def _sigmoid(x):
    return 0.5 * jnp.tanh(0.5 * x) + 0.5


def _log1p(e):
    u = 1.0 + e
    return jnp.where(u == 1.0, e, jnp.log(u) * (e / jnp.where(u == 1.0, 1.0, u - 1.0)))


def _softplus(x):
    return jnp.maximum(x, 0.0) + _log1p(jnp.exp(-jnp.abs(x)))


def _params(semantics):
    return pltpu.CompilerParams(dimension_semantics=semantics, vmem_limit_bytes=VMEM_LIMIT)


def _shift_down(cur, halo8, k):
    rolled = pltpu.roll(cur, k, 0)
    row8 = lax.broadcasted_iota(jnp.int32, halo8.shape, 0)
    top = jnp.where(row8 >= k, rolled[0:8], pltpu.roll(halo8, k, 0))
    return jnp.concatenate([top, rolled[8:]], axis=0)


def _shift_up(cur, next8, k):
    rows = cur.shape[0]
    rolled = pltpu.roll(cur, rows - k, 0)
    row8 = lax.broadcasted_iota(jnp.int32, next8.shape, 0)
    bot = jnp.where(row8 < 8 - k, rolled[rows - 8:rows], pltpu.roll(next8, 8 - k, 0))
    return jnp.concatenate([rolled[:rows - 8], bot], axis=0)


def _causal_conv(raw, halo8, w, b):
    acc = raw * w[3:4, :] + b
    for k in range(1, CONV_WIDTH):
        acc = acc + _shift_down(raw, halo8, k) * w[3 - k:4 - k, :]
    return acc


def _conv_backward(dco, next8, raw, w):
    d_raw = dco * w[3:4, :]
    gw = [None] * CONV_WIDTH
    gw[3] = jnp.sum(dco * raw, axis=0, keepdims=True)
    for j in range(1, CONV_WIDTH):
        up = _shift_up(dco, next8, j)
        d_raw = d_raw + up * w[3 - j:4 - j, :]
        gw[3 - j] = jnp.sum(up * raw, axis=0, keepdims=True)
    gb = jnp.sum(dco, axis=0, keepdims=True)
    return d_raw, gw, gb


def _cumsum_rows(v):
    rows = v.shape[0]
    row = lax.broadcasted_iota(jnp.int32, v.shape, 0)
    s = 1
    while s < rows:
        v = v + jnp.where(row >= s, pltpu.roll(v, s, 0), 0.0)
        s *= 2
    return v


def _rev_cumsum_rows(v):
    rows = v.shape[0]
    row = lax.broadcasted_iota(jnp.int32, v.shape, 0)
    s = 1
    while s < rows:
        v = v + jnp.where(row < rows - s, pltpu.roll(v, rows - s, 0), 0.0)
        s *= 2
    return v


def _matmul(a, b, mode, name, tm=1024, tn=1024, tk=1024):
    if mode == "nn":
        (m, kk), n = a.shape, b.shape[1]
    elif mode == "nt":
        (m, kk), n = a.shape, b.shape[0]
    else:
        (kk, m), n = a.shape, b.shape[1]
    tm, tn, tk = min(tm, m), min(tn, n), min(tk, kk)
    assert m % tm == 0 and n % tn == 0 and kk % tk == 0, (name, a.shape, b.shape)
    nk = kk // tk
    dims = {"nn": NN, "nt": NT, "tn": TN}[mode]
    a_spec = pl.BlockSpec((tk, tm), lambda i, j, k: (k, i)) if mode == "tn" else pl.BlockSpec((tm, tk), lambda i, j, k: (i, k))
    b_spec = pl.BlockSpec((tn, tk), lambda i, j, k: (j, k)) if mode == "nt" else pl.BlockSpec((tk, tn), lambda i, j, k: (k, j))
    o_spec = pl.BlockSpec((tm, tn), lambda i, j, k: (i, j))

    def body_single(a_ref, b_ref, o_ref):
        o_ref[...] = _dot(a_ref[...], b_ref[...], dims)

    def body(a_ref, b_ref, o_ref, acc_ref):
        k = pl.program_id(2)

        @pl.when(k == 0)
        def _():
            acc_ref[...] = jnp.zeros_like(acc_ref)

        acc_ref[...] += _dot(a_ref[...], b_ref[...], dims)

        @pl.when(k == nk - 1)
        def _():
            o_ref[...] = acc_ref[...]

    return pl.pallas_call(
        body_single if nk == 1 else body, name=name, grid=(m // tm, n // tn, nk), in_specs=[a_spec, b_spec], out_specs=o_spec,
        out_shape=jax.ShapeDtypeStruct((m, n), F32),
        scratch_shapes=[] if nk == 1 else [pltpu.VMEM((tm, tn), F32)],
        compiler_params=_params(("parallel", "parallel", "arbitrary")),
    )(a, b)


def _rms_fwd(x, g, name, rows=512):
    t, d = x.shape
    rows = min(rows, t)

    def body(x_ref, g_ref, h_ref):
        xv = x_ref[...]
        r = lax.rsqrt(jnp.mean(xv * xv, axis=-1, keepdims=True) + EPS)
        h_ref[...] = ((xv * r) * g_ref[...]).astype(h_ref.dtype)

    return pl.pallas_call(
        body, name=name, grid=(t // rows,),
        in_specs=[pl.BlockSpec((rows, d), lambda i: (i, 0)), pl.BlockSpec((1, d), lambda i: (0, 0))],
        out_specs=pl.BlockSpec((rows, d), lambda i: (i, 0)),
        out_shape=jax.ShapeDtypeStruct((t, d), MXU_DTYPE),
        compiler_params=_params(("parallel",)),
    )(x, g)


def _rms_bwd(x, dh, dres, g, name, rows=512):
    t, d = x.shape
    rows = min(rows, t)
    has_res = dres is not None

    def body(*refs):
        if has_res:
            x_ref, dh_ref, dr_ref, g_ref, dx_ref, gg_ref = refs
        else:
            x_ref, dh_ref, g_ref, dx_ref, gg_ref = refs

        @pl.when(pl.program_id(0) == 0)
        def _():
            gg_ref[...] = jnp.zeros_like(gg_ref)

        xv = x_ref[...]
        dhv = dh_ref[...]
        r = lax.rsqrt(jnp.mean(xv * xv, axis=-1, keepdims=True) + EPS)
        n = xv * r
        dn = dhv * g_ref[...]
        dx = r * (dn - n * jnp.mean(dn * n, axis=-1, keepdims=True))
        if has_res:
            dx = dx + dr_ref[...]
        dx_ref[...] = dx
        gg_ref[...] += jnp.sum(dhv * n, axis=0, keepdims=True)

    row_spec = pl.BlockSpec((rows, d), lambda i: (i, 0))
    vec_spec = pl.BlockSpec((1, d), lambda i: (0, 0))
    args = (x, dh) + ((dres,) if has_res else ()) + (g,)
    return pl.pallas_call(
        body, name=name, grid=(t // rows,),
        in_specs=[row_spec, row_spec] + ([row_spec] if has_res else []) + [vec_spec],
        out_specs=[row_spec, vec_spec],
        out_shape=[jax.ShapeDtypeStruct((t, d), F32), jax.ShapeDtypeStruct((1, d), F32)],
        compiler_params=_params(("arbitrary",)),
    )(*args)


def _dh_norm_bwd(segs, x, dres, g, rows=1024, tk=512):
    t, d = x.shape
    rows = min(rows, t)
    plan = []
    step0 = 0
    for a, _ in segs:
        kb = min(tk, a.shape[1])
        assert a.shape[1] % kb == 0, a.shape
        plan.append((step0, a.shape[1] // kb, kb))
        step0 += a.shape[1] // kb
    n_steps = step0
    ns = len(segs)

    def body(*refs):
        a_refs, w_refs = refs[0:2 * ns:2], refs[1:2 * ns:2]
        x_ref, dr_ref, g_ref, dx_ref, gg_ref, acc_ref = refs[2 * ns:]
        i, k = pl.program_id(0), pl.program_id(1)

        @pl.when((i == 0) & (k == 0))
        def _():
            gg_ref[...] = jnp.zeros_like(gg_ref)

        @pl.when(k == 0)
        def _():
            acc_ref[...] = jnp.zeros_like(acc_ref)

        for s, (first, nblk, _) in enumerate(plan):
            @pl.when((k >= first) & (k < first + nblk))
            def _(s=s):
                acc_ref[...] += _dot(a_refs[s][...], w_refs[s][...], NN)

        @pl.when(k == n_steps - 1)
        def _():
            xv = x_ref[...]
            dhv = acc_ref[...]
            r = lax.rsqrt(jnp.mean(xv * xv, axis=-1, keepdims=True) + EPS)
            n = xv * r
            dn = dhv * g_ref[...]
            dx_ref[...] = r * (dn - n * jnp.mean(dn * n, axis=-1, keepdims=True)) + dr_ref[...]
            gg_ref[...] += jnp.sum(dhv * n, axis=0, keepdims=True)

    in_specs, args = [], []
    for (a, w), (first, nblk, kb) in zip(segs, plan):
        blk = lambda k, first=first, nblk=nblk: jnp.clip(k - first, 0, nblk - 1)
        in_specs.append(pl.BlockSpec((rows, kb), lambda i, k, blk=blk: (i, blk(k))))
        in_specs.append(pl.BlockSpec((kb, d), lambda i, k, blk=blk: (blk(k), 0)))
        args += [a, w]
    row_spec = pl.BlockSpec((rows, d), lambda i, k: (i, 0))
    vec_spec = pl.BlockSpec((1, d), lambda i, k: (0, 0))
    return pl.pallas_call(
        body, name="dh_norm_bwd", grid=(t // rows, n_steps),
        in_specs=in_specs + [row_spec, row_spec, vec_spec], out_specs=[row_spec, vec_spec],
        out_shape=[jax.ShapeDtypeStruct((t, d), F32), jax.ShapeDtypeStruct((1, d), F32)],
        scratch_shapes=[pltpu.VMEM((rows, d), F32)],
        compiler_params=_params(("arbitrary", "arbitrary")),
    )(*args, x, dres, g)


def _pair_select(lo, m, h0):
    return jnp.where(lo, m[:, h0:h0 + 1], m[:, h0 + 1:h0 + 2])


def _group_select(lo, m, heads):
    return jnp.concatenate([_pair_select(lo, m, h0) for h0 in heads], axis=1)


def _halves(lo, v):
    return (jnp.sum(jnp.where(lo, v, 0.0), axis=1, keepdims=True),
            jnp.sum(jnp.where(lo, 0.0, v), axis=1, keepdims=True))


def _ssd_common(dt_raw, dtb, alog):
    dt = _softplus(dt_raw + dtb)
    aneg = -jnp.exp(alog)
    a_cs = _cumsum_rows(dt * aneg)
    return dt, aneg, a_cs, a_cs.T


def _ssd_specs(nc, rev):
    cidx = (lambda c: nc - 1 - c) if rev else (lambda c: c)
    L = SSD_CHUNK
    return dict(
        z=pl.BlockSpec((L, SSD_WIDTH), lambda c: (cidx(c), 0)),
        xr=pl.BlockSpec((L, SSD_WIDTH), lambda c: (cidx(c), 1)),
        br=pl.BlockSpec((L, 512), lambda c: (cidx(c), 8)),
        cr=pl.BlockSpec((L, 512), lambda c: (cidx(c), 9)),
        dt=pl.BlockSpec((L, DT_PAD), lambda c: (cidx(c), 0)),
        cwx=pl.BlockSpec((CONV_WIDTH, SSD_WIDTH), lambda c: (0, 0)),
        cwb=pl.BlockSpec((CONV_WIDTH, 512), lambda c: (0, 4)),
        cwc=pl.BlockSpec((CONV_WIDTH, 512), lambda c: (0, 5)),
        cbx=pl.BlockSpec((1, SSD_WIDTH), lambda c: (0, 0)),
        cbb=pl.BlockSpec((1, 512), lambda c: (0, 4)),
        cbc=pl.BlockSpec((1, 512), lambda c: (0, 5)),
        vec128=pl.BlockSpec((1, LANES), lambda c: (0, 0)),
        vecw=pl.BlockSpec((1, SSD_WIDTH), lambda c: (0, 0)),
        wide=pl.BlockSpec((L, SSD_WIDTH), lambda c: (cidx(c), 0)),
        states=pl.BlockSpec((1, SSD_GROUPS, GROUP_COLS, SSD_STATE), lambda c: (cidx(c), 0, 0, 0)),
    )


def _ssd_fwd(proj_ssd, dt_p, conv_w, conv_b, dtb, alog, d_row, ng_row):
    t = proj_ssd.shape[0]
    nc = t // SSD_CHUNK
    L = SSD_CHUNK
    sp = _ssd_specs(nc, False)

    def body(z_ref, xr_ref, br_ref, cr_ref, dt_ref, cwx_ref, cwb_ref, cwc_ref, cbx_ref, cbb_ref, cbc_ref,
             dtb_ref, alog_ref, d_ref, ng_ref, yssd_ref, y_ref, st_ref, pre_ref,
             hx_ref, hb_ref, hc_ref, state_ref, yacc_ref):
        @pl.when(pl.program_id(0) == 0)
        def _():
            hx_ref[...] = jnp.zeros_like(hx_ref)
            hb_ref[...] = jnp.zeros_like(hb_ref)
            hc_ref[...] = jnp.zeros_like(hc_ref)
            state_ref[...] = jnp.zeros_like(state_ref)

        xr, br, cr = xr_ref[...], br_ref[...], cr_ref[...]
        px = _causal_conv(xr, hx_ref[...], cwx_ref[...], cbx_ref[...])
        pb = _causal_conv(br, hb_ref[...], cwb_ref[...], cbb_ref[...])
        pc = _causal_conv(cr, hc_ref[...], cwc_ref[...], cbc_ref[...])
        hx_ref[...] = xr[L - 8:L, :]
        hb_ref[...] = br[L - 8:L, :]
        hc_ref[...] = cr[L - 8:L, :]
        pre_ref[:, 0:SSD_WIDTH] = px
        pre_ref[:, SSD_WIDTH:SSD_WIDTH + 512] = pb
        pre_ref[:, SSD_WIDTH + 512:SSD_CONV_CH] = pc
        xs = px * _sigmoid(px)
        bm = pb * _sigmoid(pb)
        cm = pc * _sigmoid(pc)

        dt, _, a_cs, a_t = _ssd_common(dt_ref[...], dtb_ref[...], alog_ref[...])
        exp_a = jnp.exp(a_cs)
        a_last = a_cs[L - 1:L, :]
        dte = jnp.exp(a_last - a_cs)
        dec = jnp.exp(a_last)

        lane = lax.broadcasted_iota(jnp.int32, (L, LANES), 1)
        sub = lax.broadcasted_iota(jnp.int32, (L, LANES), 0)
        lo = lane < SSD_HEAD_DIM
        causal = sub >= lane
        top = sub < SSD_HEAD_DIM

        for g in range(SSD_GROUPS):
            b_g = bm[:, g * SSD_STATE:(g + 1) * SSD_STATE]
            c_g = cm[:, g * SSD_STATE:(g + 1) * SSD_STATE]
            cb = _dot(c_g, b_g, NT)
            heads = [2 * (g * PAIRS_PER_GROUP + jj) for jj in range(PAIRS_PER_GROUP)]
            gcols = slice(g * GROUP_COLS, (g + 1) * GROUP_COLS)
            xs_g = xs[:, gcols]
            xdt_g = xs_g * _group_select(lo, dt, heads)
            h_g = state_ref[g]
            st_ref[0, g] = h_g
            y_off_g = _dot(c_g, h_g, NT) * _group_select(lo, exp_a, heads)
            s_new_g = _dot(xdt_g * _group_select(lo, dte, heads), b_g, TN)
            for jj, h0 in enumerate(heads):
                blk = slice(jj * LANES, (jj + 1) * LANES)
                cols = slice(g * GROUP_COLS + jj * LANES, g * GROUP_COLS + (jj + 1) * LANES)
                xdt = xdt_g[:, blk]
                g0 = jnp.where(causal, jnp.exp(a_cs[:, h0:h0 + 1] - a_t[h0:h0 + 1, :]), 0.0) * cb
                g1 = jnp.where(causal, jnp.exp(a_cs[:, h0 + 1:h0 + 2] - a_t[h0 + 1:h0 + 2, :]), 0.0) * cb
                lhs = jnp.concatenate([g0, g1], axis=1)
                rhs = jnp.concatenate([jnp.where(lo, xdt, 0.0), jnp.where(lo, 0.0, xdt)], axis=0)
                y_diag = _dot(lhs, rhs, NN)
                dec_rows = jnp.where(top, dec[:, h0:h0 + 1], dec[:, h0 + 1:h0 + 2])
                state_ref[g, blk, :] = h_g[blk, :] * dec_rows + s_new_g[blk, :]
                yacc_ref[:, cols] = (y_diag + y_off_g[:, blk]) + xs_g[:, blk] * d_ref[:, cols]

        y = yacc_ref[...]
        y_ref[...] = y
        zz = z_ref[...]
        y2 = y * (zz * _sigmoid(zz))
        gw = SSD_WIDTH // SSD_GROUPS
        for g in range(SSD_GROUPS):
            seg = y2[:, g * gw:(g + 1) * gw]
            r = lax.rsqrt(jnp.mean(seg * seg, axis=-1, keepdims=True) + EPS)
            yssd_ref[:, g * gw:(g + 1) * gw] = ((seg * r) * ng_ref[:, g * gw:(g + 1) * gw]).astype(yssd_ref.dtype)

    return pl.pallas_call(
        body, name="ssd_fwd", grid=(nc,),
        in_specs=[sp["z"], sp["xr"], sp["br"], sp["cr"], sp["dt"], sp["cwx"], sp["cwb"], sp["cwc"],
                  sp["cbx"], sp["cbb"], sp["cbc"], sp["vec128"], sp["vec128"], sp["vecw"], sp["vecw"]],
        out_specs=[sp["wide"], sp["wide"], sp["states"], pl.BlockSpec((L, SSD_CONV_CH), lambda c: (c, 0))],
        out_shape=[jax.ShapeDtypeStruct((t, SSD_WIDTH), MXU_DTYPE), jax.ShapeDtypeStruct((t, SSD_WIDTH), F32),
                   jax.ShapeDtypeStruct((nc, SSD_GROUPS, GROUP_COLS, SSD_STATE), F32), jax.ShapeDtypeStruct((t, SSD_CONV_CH), F32)],
        scratch_shapes=[pltpu.VMEM((8, SSD_WIDTH), F32), pltpu.VMEM((8, 512), F32), pltpu.VMEM((8, 512), F32),
                        pltpu.VMEM((SSD_GROUPS, GROUP_COLS, SSD_STATE), F32), pltpu.VMEM((L, SSD_WIDTH), F32)],
        compiler_params=_params(("arbitrary",)),
    )(proj_ssd, proj_ssd, proj_ssd, proj_ssd, dt_p, conv_w, conv_w, conv_w, conv_b, conv_b, conv_b,
      dtb, alog, d_row, ng_row)


def _ssd_bwd(proj_ssd, pre, dt_p, y, states, dyssd, conv_w, dtb, alog, d_row, ng_row):
    t = proj_ssd.shape[0]
    nc = t // SSD_CHUNK
    L = SSD_CHUNK
    sp = _ssd_specs(nc, True)

    def pre_spec(width, col):
        return pl.BlockSpec((L, width), lambda c: (nc - 1 - c, col))

    def body(z_ref, xr_ref, br_ref, cr_ref, px_ref, pb_ref, pc_ref, dt_ref, y_ref, st_ref, dy_ref,
             cwx_ref, cwb_ref, cwc_ref, dtb_ref, alog_ref, d_ref, ng_ref,
             dssd_ref, ddt_ref, gcw_ref, gcb_ref, gdtb_ref, galog_ref, gd_ref, gng_ref,
             gn_ref, nx_ref, nb_ref, ncc_ref, dxs_ref):
        step = pl.program_id(0)

        @pl.when(step == 0)
        def _():
            gn_ref[...] = jnp.zeros_like(gn_ref)
            nx_ref[...] = jnp.zeros_like(nx_ref)
            nb_ref[...] = jnp.zeros_like(nb_ref)
            ncc_ref[...] = jnp.zeros_like(ncc_ref)
            for ref in (gcw_ref, gcb_ref, gdtb_ref, galog_ref, gd_ref, gng_ref):
                ref[...] = jnp.zeros_like(ref)

        xr, br, cr = xr_ref[...], br_ref[...], cr_ref[...]
        cwx, cwb, cwc = cwx_ref[...], cwb_ref[...], cwc_ref[...]
        px, pb, pc = px_ref[...], pb_ref[...], pc_ref[...]
        sx, sb, sc = _sigmoid(px), _sigmoid(pb), _sigmoid(pc)
        xs, bm, cm = px * sx, pb * sb, pc * sc

        dt_in = dt_ref[...] + dtb_ref[...]
        dt, aneg, a_cs, a_t = _ssd_common(dt_ref[...], dtb_ref[...], alog_ref[...])
        exp_a = jnp.exp(a_cs)
        a_last = a_cs[L - 1:L, :]
        dte = jnp.exp(a_last - a_cs)
        dec = jnp.exp(a_last)

        lane = lax.broadcasted_iota(jnp.int32, (L, LANES), 1)
        sub = lax.broadcasted_iota(jnp.int32, (L, LANES), 0)
        lo = lane < SSD_HEAD_DIM
        causal = sub >= lane
        top = sub < SSD_HEAD_DIM
        last_row = sub == L - 1

        yv = y_ref[...]
        zz = z_ref[...]
        sz = _sigmoid(zz)
        silz = zz * sz
        y2 = yv * silz
        dyv = dy_ref[...]
        gw = SSD_WIDTH // SSD_GROUPS
        d_y2_parts = []
        gng_parts = []
        for g in range(SSD_GROUPS):
            seg = y2[:, g * gw:(g + 1) * gw]
            dseg = dyv[:, g * gw:(g + 1) * gw]
            r = lax.rsqrt(jnp.mean(seg * seg, axis=-1, keepdims=True) + EPS)
            n = seg * r
            dn = dseg * ng_ref[:, g * gw:(g + 1) * gw]
            gng_parts.append(jnp.sum(dseg * n, axis=0, keepdims=True))
            d_y2_parts.append(r * (dn - n * jnp.mean(dn * n, axis=-1, keepdims=True)))
        d_y2 = jnp.concatenate(d_y2_parts, axis=1)
        gng_ref[...] += jnp.concatenate(gng_parts, axis=1)
        d_y = d_y2 * silz
        dssd_ref[:, 0:SSD_WIDTH] = (d_y2 * yv * (sz * (1.0 + zz * (1.0 - sz)))).astype(dssd_ref.dtype)
        gd_ref[...] += jnp.sum(d_y * xs, axis=0, keepdims=True)
        dxs_ref[...] = d_y * d_ref[...]

        d_a = jnp.zeros((L, LANES), F32)
        d_at = jnp.zeros((LANES, L), F32)
        ddt = jnp.zeros((L, LANES), F32)
        d_b_parts, d_c_parts = [], []
        for g in range(SSD_GROUPS):
            b_g = bm[:, g * SSD_STATE:(g + 1) * SSD_STATE]
            c_g = cm[:, g * SSD_STATE:(g + 1) * SSD_STATE]
            cb = _dot(c_g, b_g, NT)
            d_cb = jnp.zeros((L, L), F32)
            heads = [2 * (g * PAIRS_PER_GROUP + jj) for jj in range(PAIRS_PER_GROUP)]
            gcols = slice(g * GROUP_COLS, (g + 1) * GROUP_COLS)
            dy_g, xs_g = d_y[:, gcols], xs[:, gcols]
            dt_g = _group_select(lo, dt, heads)
            expa_g = _group_select(lo, exp_a, heads)
            dte_g = _group_select(lo, dte, heads)
            xdt_g = xs_g * dt_g
            h_g = st_ref[0, g]
            gn_g = gn_ref[g]
            dys_g = dy_g * expa_g
            d_cg = _dot(dys_g, h_g, NN)
            d_h_g = _dot(dys_g, c_g, TN)
            t1_g = dy_g * _dot(c_g, h_g, NT) * expa_g
            d_bg = _dot(xdt_g * dte_g, gn_g, NN)
            dxdt_g = _dot(b_g, gn_g, NT) * dte_g
            t2_g = dxdt_g * xdt_g
            t12_g = t1_g - t2_g
            gh_g = jnp.sum(gn_g * h_g, axis=1, keepdims=True)
            for jj, h0 in enumerate(heads):
                blk = slice(jj * LANES, (jj + 1) * LANES)
                cols = slice(g * GROUP_COLS + jj * LANES, g * GROUP_COLS + (jj + 1) * LANES)
                dy_p, xs_p, xdt, dt_pp = dy_g[:, blk], xs_g[:, blk], xdt_g[:, blk], dt_g[:, blk]
                l0 = jnp.where(causal, jnp.exp(a_cs[:, h0:h0 + 1] - a_t[h0:h0 + 1, :]), 0.0)
                l1 = jnp.where(causal, jnp.exp(a_cs[:, h0 + 1:h0 + 2] - a_t[h0 + 1:h0 + 2, :]), 0.0)
                g0, g1 = l0 * cb, l1 * cb
                dcat = jnp.concatenate([jnp.where(lo, dy_p, 0.0), jnp.where(lo, 0.0, dy_p)], axis=0)
                d_xdt = dxdt_g[:, blk] + _dot(jnp.concatenate([g0, g1], axis=0), dcat, TN)
                dm = _dot(dcat, xdt, NT)
                dm0, dm1 = dm[0:L], dm[L:2 * L]
                d_cb = d_cb + (l0 * dm0 + l1 * dm1)
                e0, e1 = dm0 * g0, dm1 * g1
                a0, a1 = _halves(lo, t12_g[:, blk])
                a0 = a0 + jnp.sum(e0, axis=1, keepdims=True)
                a1 = a1 + jnp.sum(e1, axis=1, keepdims=True)
                s0, s1 = _halves(lo, t2_g[:, blk])
                gh = gh_g[blk, :]
                dd0 = jnp.sum(jnp.where(top[:, 0:1], gh, 0.0), axis=0, keepdims=True)
                dd1 = jnp.sum(jnp.where(top[:, 0:1], 0.0, gh), axis=0, keepdims=True)
                end0 = jnp.sum(s0, axis=0, keepdims=True) + dd0 * dec[:, h0:h0 + 1]
                end1 = jnp.sum(s1, axis=0, keepdims=True) + dd1 * dec[:, h0 + 1:h0 + 2]
                d_a = d_a + jnp.where(lane == h0, a0 + jnp.where(last_row, end0, 0.0), 0.0)
                d_a = d_a + jnp.where(lane == h0 + 1, a1 + jnp.where(last_row, end1, 0.0), 0.0)
                d_at = d_at - jnp.where(sub == h0, jnp.sum(e0, axis=0, keepdims=True), 0.0)
                d_at = d_at - jnp.where(sub == h0 + 1, jnp.sum(e1, axis=0, keepdims=True), 0.0)
                dec_rows = jnp.where(top, dec[:, h0:h0 + 1], dec[:, h0 + 1:h0 + 2])
                gn_ref[g, blk, :] = d_h_g[blk, :] + dec_rows * gn_g[blk, :]
                q0, q1 = _halves(lo, d_xdt * xs_p)
                ddt = ddt + jnp.where(lane == h0, q0, 0.0) + jnp.where(lane == h0 + 1, q1, 0.0)
                dxs_ref[:, cols] += d_xdt * dt_pp
            d_cg = d_cg + _dot(d_cb, b_g, NN)
            d_bg = d_bg + _dot(d_cb, c_g, TN)
            d_b_parts.append(d_bg)
            d_c_parts.append(d_cg)

        rc = _rev_cumsum_rows(d_a + d_at.T)
        d_dt = rc * aneg + ddt
        galog_ref[...] += jnp.sum(rc * dt, axis=0, keepdims=True) * aneg
        d_dtraw = d_dt * _sigmoid(dt_in)
        gdtb_ref[...] += jnp.sum(d_dtraw, axis=0, keepdims=True)
        ddt_ref[...] = d_dtraw.astype(ddt_ref.dtype)

        def dsilu(p, s):
            return s * (1.0 + p * (1.0 - s))

        dcx = dxs_ref[...] * dsilu(px, sx)
        dcb = jnp.concatenate(d_b_parts, axis=1) * dsilu(pb, sb)
        dcc = jnp.concatenate(d_c_parts, axis=1) * dsilu(pc, sc)
        drx, gwx, gbx = _conv_backward(dcx, nx_ref[...], xr, cwx)
        drb, gwb, gbb = _conv_backward(dcb, nb_ref[...], br, cwb)
        drc, gwc, gbc = _conv_backward(dcc, ncc_ref[...], cr, cwc)
        nx_ref[...] = dcx[0:8, :]
        nb_ref[...] = dcb[0:8, :]
        ncc_ref[...] = dcc[0:8, :]
        dssd_ref[:, SSD_WIDTH:2 * SSD_WIDTH] = drx.astype(dssd_ref.dtype)
        dssd_ref[:, 2 * SSD_WIDTH:2 * SSD_WIDTH + 512] = drb.astype(dssd_ref.dtype)
        dssd_ref[:, 2 * SSD_WIDTH + 512:SSD_SEG] = drc.astype(dssd_ref.dtype)
        for k in range(CONV_WIDTH):
            gcw_ref[k:k + 1, :] += jnp.concatenate([gwx[k], gwb[k], gwc[k]], axis=1)
        gcb_ref[...] += jnp.concatenate([gbx, gbb, gbc], axis=1)

    const = lambda shape: pl.BlockSpec(shape, lambda c: (0,) * len(shape))
    return pl.pallas_call(
        body, name="ssd_bwd", grid=(nc,),
        in_specs=[sp["z"], sp["xr"], sp["br"], sp["cr"], pre_spec(SSD_WIDTH, 0), pre_spec(512, 4), pre_spec(512, 5),
                  sp["dt"], sp["wide"], sp["states"], sp["wide"],
                  sp["cwx"], sp["cwb"], sp["cwc"], sp["vec128"], sp["vec128"], sp["vecw"], sp["vecw"]],
        out_specs=[pl.BlockSpec((L, SSD_SEG), lambda c: (nc - 1 - c, 0)), sp["dt"],
                   const((CONV_WIDTH, SSD_CONV_CH)), const((1, SSD_CONV_CH)), const((1, LANES)), const((1, LANES)),
                   const((1, SSD_WIDTH)), const((1, SSD_WIDTH))],
        out_shape=[jax.ShapeDtypeStruct((t, SSD_SEG), MXU_DTYPE), jax.ShapeDtypeStruct((t, DT_PAD), MXU_DTYPE),
                   jax.ShapeDtypeStruct((CONV_WIDTH, SSD_CONV_CH), F32), jax.ShapeDtypeStruct((1, SSD_CONV_CH), F32),
                   jax.ShapeDtypeStruct((1, LANES), F32), jax.ShapeDtypeStruct((1, LANES), F32),
                   jax.ShapeDtypeStruct((1, SSD_WIDTH), F32), jax.ShapeDtypeStruct((1, SSD_WIDTH), F32)],
        scratch_shapes=[pltpu.VMEM((SSD_GROUPS, GROUP_COLS, SSD_STATE), F32), pltpu.VMEM((8, SSD_WIDTH), F32),
                        pltpu.VMEM((8, 512), F32), pltpu.VMEM((8, 512), F32), pltpu.VMEM((L, SSD_WIDTH), F32)],
        compiler_params=_params(("arbitrary",)),
    )(proj_ssd, proj_ssd, proj_ssd, proj_ssd, pre, pre, pre, dt_p, y, states, dyssd,
      conv_w, conv_w, conv_w, dtb, alog, d_row, ng_row)


def _lru_gates(xl, wa_ref, wx_ref, ba, bx, lam):
    pre_a, pre_x = [], []
    for g in range(LRU_NGROUPS):
        xg = xl[:, g * LRU_GROUP:(g + 1) * LRU_GROUP]
        pre_a.append(_dot(xg, wa_ref[g], NN))
        pre_x.append(_dot(xg, wx_ref[g], NN))
    r = _sigmoid(jnp.concatenate(pre_a, axis=1) + ba)
    i = _sigmoid(jnp.concatenate(pre_x, axis=1) + bx)
    log_a = (-LRU_C * r) * _softplus(-lam)
    a = jnp.exp(log_a)
    mult = jnp.sqrt(-jnp.tanh(log_a) * (a * a + 1.0))
    return r, i, log_a, a, mult


def _scan_rows(p, u, carry, reverse):
    rows, w = p.shape
    groups = rows // 8
    p3, u3 = p.reshape(groups, 8, w), u.reshape(groups, 8, w)
    row = lax.broadcasted_iota(jnp.int32, (groups, 8, w), 1)
    for s in (1, 2, 4):
        ok = row < 8 - s if reverse else row >= s
        shift = 8 - s if reverse else s
        u3 = p3 * jnp.where(ok, pltpu.roll(u3, shift, 1), 0.0) + u3
        p3 = p3 * jnp.where(ok, pltpu.roll(p3, shift, 1), 1.0)
    out = [None] * groups
    for k in (range(groups - 1, -1, -1) if reverse else range(groups)):
        out[k] = p3[k] * carry + u3[k]
        carry = out[k][0:1, :] if reverse else out[k][7:8, :]
    return jnp.concatenate(out, axis=0), carry


def _lru_fwd(proj_lru, conv_w, conv_b, wa, wx, ba, bx, lam):
    t = proj_lru.shape[0]
    rows = min(LRU_ROWS, t)
    nb = t // rows
    W = LRU_WIDTH

    def body(lg_ref, lx_ref, cw_ref, cb_ref, wa_ref, wx_ref, ba_ref, bx_ref, lam_ref, ylru_ref, h_ref,
             halo_ref, carry_ref):
        @pl.when(pl.program_id(0) == 0)
        def _():
            halo_ref[...] = jnp.zeros_like(halo_ref)
            carry_ref[...] = jnp.zeros_like(carry_ref)

        lx = lx_ref[...]
        xl = _causal_conv(lx, halo_ref[...], cw_ref[...], cb_ref[...])
        halo_ref[...] = lx[rows - 8:rows, :]
        _, i, _, a, mult = _lru_gates(xl, wa_ref, wx_ref, ba_ref[...], bx_ref[...], lam_ref[...])
        u = mult * (i * xl)
        h, carry_ref[...] = _scan_rows(a, u, carry_ref[...], False)
        h_ref[...] = h
        lg = lg_ref[...]
        ylru_ref[...] = (h * (lg * _sigmoid(lg))).astype(ylru_ref.dtype)

    const = lambda shape: pl.BlockSpec(shape, lambda b: (0,) * len(shape))
    return pl.pallas_call(
        body, name="lru_fwd", grid=(nb,),
        in_specs=[pl.BlockSpec((rows, W), lambda b: (b, 0)), pl.BlockSpec((rows, W), lambda b: (b, 1)),
                  const((CONV_WIDTH, W)), const((1, W)), const((LRU_NGROUPS, LRU_GROUP, LRU_GROUP)),
                  const((LRU_NGROUPS, LRU_GROUP, LRU_GROUP)), const((1, W)), const((1, W)), const((1, W))],
        out_specs=[pl.BlockSpec((rows, W), lambda b: (b, 0)), pl.BlockSpec((rows, W), lambda b: (b, 0))],
        out_shape=[jax.ShapeDtypeStruct((t, W), MXU_DTYPE), jax.ShapeDtypeStruct((t, W), F32)],
        scratch_shapes=[pltpu.VMEM((8, W), F32), pltpu.VMEM((1, W), F32)],
        compiler_params=_params(("arbitrary",)),
    )(proj_lru, proj_lru, conv_w, conv_b, wa, wx, ba, bx, lam)


def _lru_bwd(proj_lru, h, dylru, conv_w, conv_b, wa, wx, ba, bx, lam):
    t = proj_lru.shape[0]
    rows = min(LRU_ROWS, t)
    nb = t // rows
    W = LRU_WIDTH
    groups8 = rows // 8

    def rev(b):
        return nb - 1 - b

    def halo_spec(col):
        return pl.BlockSpec((8, W), lambda b: (jnp.maximum(rev(b) * groups8 - 1, 0), col))

    def body(lg_ref, lx_ref, hlx_ref, h_ref, hh_ref, dy_ref, cw_ref, cb_ref, wa_ref, wx_ref, ba_ref, bx_ref, lam_ref,
             dlru_ref, gcw_ref, gcb_ref, gba_ref, gbx_ref, glam_ref, gwa_ref, gwx_ref,
             gcarry_ref, afirst_ref, nxt_ref):
        step = pl.program_id(0)

        @pl.when(step == 0)
        def _():
            gcarry_ref[...] = jnp.zeros_like(gcarry_ref)
            afirst_ref[...] = jnp.zeros_like(afirst_ref)
            nxt_ref[...] = jnp.zeros_like(nxt_ref)
            for ref in (gcw_ref, gcb_ref, gba_ref, gbx_ref, glam_ref, gwa_ref, gwx_ref):
                ref[...] = jnp.zeros_like(ref)

        keep = jnp.where(step == nb - 1, 0.0, 1.0)
        lx = lx_ref[...]
        hlx = hlx_ref[...] * keep
        cw = cw_ref[...]
        xl = _causal_conv(lx, hlx, cw, cb_ref[...])
        lam = lam_ref[...]
        r, i, log_a, a, mult = _lru_gates(xl, wa_ref, wx_ref, ba_ref[...], bx_ref[...], lam)
        hv = h_ref[...]
        h_prev = _shift_down(hv, hh_ref[...] * keep, 1)
        lg = lg_ref[...]
        sg = _sigmoid(lg)
        dyv = dy_ref[...]
        d_h = dyv * (lg * sg)
        dlru_ref[:, 0:W] = (dyv * hv * (sg * (1.0 + lg * (1.0 - sg)))).astype(dlru_ref.dtype)

        row = lax.broadcasted_iota(jnp.int32, (rows, W), 0)
        p = jnp.where(row < rows - 1, pltpu.roll(a, rows - 1, 0), afirst_ref[...])
        gsc, gcarry_ref[...] = _scan_rows(p, d_h, gcarry_ref[...], True)
        afirst_ref[...] = a[0:1, :]

        d_a = gsc * h_prev
        v = i * xl
        d_mult = gsc * v
        d_v = gsc * mult
        d_i = d_v * xl
        d_xl = d_v * i
        d_la = d_a * a - d_mult * (a * a) / mult
        sp_neg = _softplus(-lam)
        d_r = d_la * (-LRU_C * sp_neg)
        glam_ref[...] += jnp.sum(d_la * r, axis=0, keepdims=True) * (LRU_C * _sigmoid(-lam))
        d_pa = d_r * r * (1.0 - r)
        d_px = d_i * i * (1.0 - i)
        gba_ref[...] += jnp.sum(d_pa, axis=0, keepdims=True)
        gbx_ref[...] += jnp.sum(d_px, axis=0, keepdims=True)
        parts = []
        for g in range(LRU_NGROUPS):
            cols = slice(g * LRU_GROUP, (g + 1) * LRU_GROUP)
            xg, dpa_g, dpx_g = xl[:, cols], d_pa[:, cols], d_px[:, cols]
            parts.append(_dot(dpa_g, wa_ref[g], NT) + _dot(dpx_g, wx_ref[g], NT))
            gwa_ref[g] += _dot(xg, dpa_g, TN)
            gwx_ref[g] += _dot(xg, dpx_g, TN)
        d_xl = d_xl + jnp.concatenate(parts, axis=1)
        d_lx, gw, gb = _conv_backward(d_xl, nxt_ref[...], lx, cw)
        nxt_ref[...] = d_xl[0:8, :]
        dlru_ref[:, W:2 * W] = d_lx.astype(dlru_ref.dtype)
        for k in range(CONV_WIDTH):
            gcw_ref[k:k + 1, :] += gw[k]
        gcb_ref[...] += gb

    const = lambda shape: pl.BlockSpec(shape, lambda b: (0,) * len(shape))
    wspec = const((LRU_NGROUPS, LRU_GROUP, LRU_GROUP))
    blk = lambda col: pl.BlockSpec((rows, W), lambda b: (rev(b), col))
    return pl.pallas_call(
        body, name="lru_bwd", grid=(nb,),
        in_specs=[blk(0), blk(1), halo_spec(1), blk(0), halo_spec(0), blk(0),
                  const((CONV_WIDTH, W)), const((1, W)), wspec, wspec, const((1, W)), const((1, W)), const((1, W))],
        out_specs=[pl.BlockSpec((rows, 2 * W), lambda b: (rev(b), 0)), const((CONV_WIDTH, W)), const((1, W)),
                   const((1, W)), const((1, W)), const((1, W)), wspec, wspec],
        out_shape=[jax.ShapeDtypeStruct((t, 2 * W), MXU_DTYPE), jax.ShapeDtypeStruct((CONV_WIDTH, W), F32),
                   jax.ShapeDtypeStruct((1, W), F32), jax.ShapeDtypeStruct((1, W), F32), jax.ShapeDtypeStruct((1, W), F32),
                   jax.ShapeDtypeStruct((1, W), F32), jax.ShapeDtypeStruct((LRU_NGROUPS, LRU_GROUP, LRU_GROUP), F32),
                   jax.ShapeDtypeStruct((LRU_NGROUPS, LRU_GROUP, LRU_GROUP), F32)],
        scratch_shapes=[pltpu.VMEM((1, W), F32), pltpu.VMEM((1, W), F32), pltpu.VMEM((8, W), F32)],
        compiler_params=_params(("arbitrary",)),
    )(proj_lru, proj_lru, proj_lru, h, h, dylru, conv_w, conv_b, wa, wx, ba, bx, lam)


def _mem_scores(q_h, k_h):
    s = _dot(q_h, k_h, NT) * (MEM_HEAD_DIM ** -0.5)
    s = s - jnp.max(s, axis=-1, keepdims=True)
    e = jnp.exp(s)
    return e / jnp.sum(e, axis=-1, keepdims=True)


def _mem_fwd(q, kv, rows=512):
    t = q.shape[0]
    rows = min(rows, t)
    m = kv.shape[0]

    def body(q_ref, kv_ref, y_ref):
        for hd in range(MEM_HEADS):
            cols = slice(hd * MEM_HEAD_DIM, (hd + 1) * MEM_HEAD_DIM)
            vcols = slice(D_MODEL + hd * MEM_HEAD_DIM, D_MODEL + (hd + 1) * MEM_HEAD_DIM)
            p = _mem_scores(q_ref[:, cols], kv_ref[:, cols])
            y_ref[:, cols] = _dot(p, kv_ref[:, vcols], NN).astype(y_ref.dtype)

    return pl.pallas_call(
        body, name="mem_fwd", grid=(t // rows,),
        in_specs=[pl.BlockSpec((rows, D_MODEL), lambda i: (i, 0)), pl.BlockSpec((m, 2 * D_MODEL), lambda i: (0, 0))],
        out_specs=pl.BlockSpec((rows, D_MODEL), lambda i: (i, 0)),
        out_shape=jax.ShapeDtypeStruct((t, D_MODEL), MXU_DTYPE),
        compiler_params=_params(("parallel",)),
    )(q, kv)


def _mem_bwd(q, kv, dy, rows=512):
    t = q.shape[0]
    rows = min(rows, t)
    m = kv.shape[0]

    def body(q_ref, kv_ref, dy_ref, dq_ref, dkv_ref):
        @pl.when(pl.program_id(0) == 0)
        def _():
            dkv_ref[...] = jnp.zeros_like(dkv_ref)

        for hd in range(MEM_HEADS):
            cols = slice(hd * MEM_HEAD_DIM, (hd + 1) * MEM_HEAD_DIM)
            vcols = slice(D_MODEL + hd * MEM_HEAD_DIM, D_MODEL + (hd + 1) * MEM_HEAD_DIM)
            q_h, k_h, dy_h = q_ref[:, cols], kv_ref[:, cols], dy_ref[:, cols]
            p = _mem_scores(q_h, k_h)
            dp = _dot(dy_h, kv_ref[:, vcols], NT)
            dkv_ref[:, vcols] += _dot(p, dy_h, TN)
            ds = p * (dp - jnp.sum(dp * p, axis=-1, keepdims=True)) * (MEM_HEAD_DIM ** -0.5)
            dq_ref[:, cols] = _dot(ds, k_h, NN).astype(dq_ref.dtype)
            dkv_ref[:, cols] += _dot(ds, q_h, TN)

    return pl.pallas_call(
        body, name="mem_bwd", grid=(t // rows,),
        in_specs=[pl.BlockSpec((rows, D_MODEL), lambda i: (i, 0)), pl.BlockSpec((m, 2 * D_MODEL), lambda i: (0, 0)),
                  pl.BlockSpec((rows, D_MODEL), lambda i: (i, 0))],
        out_specs=[pl.BlockSpec((rows, D_MODEL), lambda i: (i, 0)), pl.BlockSpec((m, 2 * D_MODEL), lambda i: (0, 0))],
        out_shape=[jax.ShapeDtypeStruct((t, D_MODEL), MXU_DTYPE), jax.ShapeDtypeStruct((m, 2 * D_MODEL), F32)],
        compiler_params=_params(("arbitrary",)),
    )(q, kv, dy)


def _merge_fwd(x, yssd, ylru, ymem, gl, w_bs, w_bl, w_bm, w_out, fg, tgt, rows=256):
    t = x.shape[0]
    rows = min(rows, t)
    D = D_MODEL

    def body(x_ref, ys_ref, yl_ref, ym_ref, gl_ref, wbs_ref, wbl_ref, wbm_ref, wo_ref, fg_ref, tgt_ref,
             ps_ref, pl_ref, pm_ref, mg_ref, dx2_ref, loss_ref, gfg_ref):
        @pl.when(pl.program_id(0) == 0)
        def _():
            loss_ref[...] = jnp.zeros_like(loss_ref)
            gfg_ref[...] = jnp.zeros_like(gfg_ref)

        ps = _dot(ys_ref[...], wbs_ref[...], NN)
        pl_ = _dot(yl_ref[...], wbl_ref[...], NN)
        pm = _dot(ym_ref[...], wbm_ref[...], NN)
        ps_ref[...] = ps
        pl_ref[...] = pl_
        pm_ref[...] = pm
        merged = (_sigmoid(gl_ref[:, 0:D]) * ps + _sigmoid(gl_ref[:, D:2 * D]) * pl_) + _sigmoid(gl_ref[:, 2 * D:3 * D]) * pm
        mg_ref[...] = merged.astype(mg_ref.dtype)
        x2 = x_ref[...] + _dot(merged, wo_ref[...], NN)
        r2 = lax.rsqrt(jnp.mean(x2 * x2, axis=-1, keepdims=True) + EPS)
        xn = x2 * r2
        fg = fg_ref[...]
        diff = xn * fg - tgt_ref[...]
        tile_loss = 0.5 * jnp.sum(jnp.mean(diff * diff, axis=-1, keepdims=True), axis=0, keepdims=True)
        loss_ref[...] += jnp.broadcast_to(tile_loss, loss_ref.shape)
        d_out = diff * (1.0 / D)
        gfg_ref[...] += jnp.sum(d_out * xn, axis=0, keepdims=True)
        dxn = d_out * fg
        dx2_ref[...] = r2 * (dxn - xn * jnp.mean(dxn * xn, axis=-1, keepdims=True))

    row = lambda w: pl.BlockSpec((rows, w), lambda i: (i, 0))
    const = lambda shape: pl.BlockSpec(shape, lambda i: (0,) * len(shape))
    return pl.pallas_call(
        body, name="merge_fwd", grid=(t // rows,),
        in_specs=[row(D), row(SSD_WIDTH), row(LRU_WIDTH), row(D), row(3 * D), const((SSD_WIDTH, D)), const((LRU_WIDTH, D)),
                  const((D, D)), const((D, D)), const((1, D)), row(D)],
        out_specs=[row(D), row(D), row(D), row(D), row(D), const((1, LANES)), const((1, D))],
        out_shape=[jax.ShapeDtypeStruct((t, D), F32), jax.ShapeDtypeStruct((t, D), F32), jax.ShapeDtypeStruct((t, D), F32),
                   jax.ShapeDtypeStruct((t, D), MXU_DTYPE), jax.ShapeDtypeStruct((t, D), F32),
                   jax.ShapeDtypeStruct((1, LANES), F32), jax.ShapeDtypeStruct((1, D), F32)],
        compiler_params=_params(("arbitrary",)),
    )(x, yssd, ylru, ymem, gl, w_bs, w_bl, w_bm, w_out, fg, tgt)


def _merge_bwd(dx2, gl, ps, pl_in, pm, w_bs, w_bl, w_bm, w_out, rows=256):
    t = dx2.shape[0]
    rows = min(rows, t)
    D = D_MODEL

    def body(dx2_ref, gl_ref, ps_ref, pl_ref, pm_ref, wbs_ref, wbl_ref, wbm_ref, wo_ref,
             dg_ref, dps_ref, dpl_ref, dpm_ref, dys_ref, dyl_ref, dym_ref):
        dm = _dot(dx2_ref[...], wo_ref[...], NT)
        for idx, (p_ref, dp_ref, w_ref, dy_ref) in enumerate(
                ((ps_ref, dps_ref, wbs_ref, dys_ref), (pl_ref, dpl_ref, wbl_ref, dyl_ref), (pm_ref, dpm_ref, wbm_ref, dym_ref))):
            gate = _sigmoid(gl_ref[:, idx * D:(idx + 1) * D])
            dg_ref[:, idx * D:(idx + 1) * D] = ((dm * p_ref[...]) * gate * (1.0 - gate)).astype(dg_ref.dtype)
            dp = dm * gate
            dp_ref[...] = dp.astype(dp_ref.dtype)
            dy_ref[...] = _dot(dp, w_ref[...], NT)

    row = lambda w: pl.BlockSpec((rows, w), lambda i: (i, 0))
    const = lambda shape: pl.BlockSpec(shape, lambda i: (0,) * len(shape))
    return pl.pallas_call(
        body, name="merge_bwd", grid=(t // rows,),
        in_specs=[row(D), row(3 * D), row(D), row(D), row(D), const((SSD_WIDTH, D)), const((LRU_WIDTH, D)),
                  const((D, D)), const((D, D))],
        out_specs=[row(3 * D), row(D), row(D), row(D), row(SSD_WIDTH), row(LRU_WIDTH), row(D)],
        out_shape=[jax.ShapeDtypeStruct((t, 3 * D), MXU_DTYPE), jax.ShapeDtypeStruct((t, D), MXU_DTYPE),
                   jax.ShapeDtypeStruct((t, D), MXU_DTYPE), jax.ShapeDtypeStruct((t, D), MXU_DTYPE),
                   jax.ShapeDtypeStruct((t, SSD_WIDTH), F32), jax.ShapeDtypeStruct((t, LRU_WIDTH), F32),
                   jax.ShapeDtypeStruct((t, D), F32)],
        compiler_params=_params(("parallel",)),
    )(dx2, gl, ps, pl_in, pm, w_bs, w_bl, w_bm, w_out)


def _mesh_place():
    x, y, c = lax.axis_index("x"), lax.axis_index("y"), lax.axis_index("c")
    return x, y, c, 4 * x + 2 * y + c


def _other_chips(x, y):
    return [(1 - x, y), (x, 1 - y), (1 - x, 1 - y)]


def _all_gather(arrs, name):
    n = len(arrs)

    def body(*refs):
        ins, outs = refs[:n], refs[n:2 * n]
        send_sems, recv_sems, local_sems = refs[2 * n:]
        x, y, c, me = _mesh_place()
        sibling = (x, y, 1 - c)
        chips = _other_chips(x, y)

        def slot(px, py, pc):
            return 4 * px + 2 * py + pc

        def copy(a, k, block, to, src=None):
            return pltpu.make_async_remote_copy(
                src_ref=outs[a].at[block] if src is None else src, dst_ref=outs[a].at[block],
                send_sem=send_sems.at[a, k], recv_sem=recv_sems.at[a, k], device_id=to, device_id_type=pl.DeviceIdType.MESH)

        local = [pltpu.make_async_copy(ins[a], outs[a].at[me], local_sems.at[a]) for a in range(n)]
        for cp in local:
            cp.start()
        sends = []
        for a in range(n):
            sends.append(copy(a, 0, me, sibling, src=ins[a]))
            for j, chip in enumerate(chips):
                sends.append(copy(a, 1 + j, me, (*chip, c), src=ins[a]))
        for cp in sends:
            cp.start()
        for j, chip in enumerate(chips):
            for a in range(n):
                copy(a, 1 + j, slot(*chip, c), sibling).wait_recv()
                passed = copy(a, 4 + j, slot(*chip, c), sibling)
                passed.start()
                sends.append(passed)
        for a in range(n):
            copy(a, 0, slot(x, y, 1 - c), sibling).wait_recv()
        for j, chip in enumerate(chips):
            for a in range(n):
                copy(a, 4 + j, slot(*chip, 1 - c), sibling).wait_recv()
        for cp in sends:
            cp.wait_send()
        for cp in local:
            cp.wait()

    any_spec = pl.BlockSpec(memory_space=pl.ANY)
    return pl.pallas_call(
        body, name=name, in_specs=[any_spec] * n, out_specs=[any_spec] * n,
        out_shape=[jax.ShapeDtypeStruct((N_DEV,) + a.shape, a.dtype) for a in arrs],
        scratch_shapes=[pltpu.SemaphoreType.DMA((n, 7)), pltpu.SemaphoreType.DMA((n, 7)), pltpu.SemaphoreType.DMA((n,))],
    )(*arrs)


N_CHIPS = 4


def _pair_exchange(parts):
    n = len(parts)

    def body(*refs):
        ins, outs = refs[:n], refs[n:2 * n]
        send_sems, recv_sems = refs[2 * n:]
        x, y, c, _ = _mesh_place()
        sibling = (x, y, 1 - c)
        sends = []
        for a in range(n):
            for q in range(N_CHIPS):
                cp = pltpu.make_async_remote_copy(src_ref=ins[a].at[q, 1 - c], dst_ref=outs[a].at[q], send_sem=send_sems.at[a, q],
                                                  recv_sem=recv_sems.at[a, q], device_id=sibling, device_id_type=pl.DeviceIdType.MESH)
                cp.start()
                sends.append(cp)
        for cp in sends:
            cp.wait_recv()
        for cp in sends:
            cp.wait_send()

    any_spec = pl.BlockSpec(memory_space=pl.ANY)
    return pl.pallas_call(
        body, name="grad_pair_exchange", in_specs=[any_spec] * n, out_specs=[any_spec] * n,
        out_shape=[jax.ShapeDtypeStruct((N_CHIPS,) + a.shape[2:], a.dtype) for a in parts],
        scratch_shapes=[pltpu.SemaphoreType.DMA((n, N_CHIPS)), pltpu.SemaphoreType.DMA((n, N_CHIPS))],
    )(*parts)


def _col_tile(r, c, limit_bytes):
    assert c % LANES == 0, c
    best = LANES
    for cand in range(LANES, c + 1, LANES):
        if c % cand == 0 and r * cand * 4 <= limit_bytes:
            best = cand
    return best


def _chip_sum(part, recv, core, name):
    _, _, r, c = part.shape
    ct = _col_tile(r, c, 2 << 20)

    def body(core_ref, p_ref, r_ref, s_ref, t_ref):
        s = p_ref[...] + r_ref[...]
        s_ref[...] = s
        t_ref[...] = s.astype(t_ref.dtype)

    blk = pl.BlockSpec((None, r, ct), lambda q, i, core_ref: (q, 0, i))
    return pl.pallas_call(
        body, name=name,
        grid_spec=pltpu.PrefetchScalarGridSpec(
            num_scalar_prefetch=1, grid=(N_CHIPS, c // ct),
            in_specs=[pl.BlockSpec((None, None, r, ct), lambda q, i, core_ref: (q, core_ref[0], 0, i)), blk],
            out_specs=[blk, blk]),
        out_shape=[jax.ShapeDtypeStruct((N_CHIPS, r, c), F32), jax.ShapeDtypeStruct((N_CHIPS, r, c), GRAD_WIRE_DTYPE)],
        compiler_params=_params(("parallel", "parallel")),
    )(core, part, recv)


def _chip_exchange(sums):
    n = len(sums)

    def body(*refs):
        ins, outs = refs[:n], refs[n:2 * n]
        send_sems, recv_sems = refs[2 * n:]
        x, y, c, _ = _mesh_place()
        my_chip = 2 * x + y
        sends = []
        for a in range(n):
            for j, (px, py) in enumerate(_other_chips(x, y)):
                cp = pltpu.make_async_remote_copy(src_ref=ins[a].at[2 * px + py], dst_ref=outs[a].at[my_chip], send_sem=send_sems.at[a, j],
                                                  recv_sem=recv_sems.at[a, j], device_id=(px, py, c), device_id_type=pl.DeviceIdType.MESH)
                cp.start()
                sends.append(cp)
        for a in range(n):
            for j, (px, py) in enumerate(_other_chips(x, y)):
                pltpu.make_async_remote_copy(src_ref=ins[a].at[my_chip], dst_ref=outs[a].at[2 * px + py], send_sem=send_sems.at[a, j],
                                             recv_sem=recv_sems.at[a, j], device_id=(px, py, c),
                                             device_id_type=pl.DeviceIdType.MESH).wait_recv()
        for cp in sends:
            cp.wait_send()

    any_spec = pl.BlockSpec(memory_space=pl.ANY)
    return pl.pallas_call(
        body, name="grad_chip_exchange", in_specs=[any_spec] * n, out_specs=[any_spec] * n,
        out_shape=[jax.ShapeDtypeStruct(a.shape, a.dtype) for a in sums],
        scratch_shapes=[pltpu.SemaphoreType.DMA((n, 3)), pltpu.SemaphoreType.DMA((n, 3))],
    )(*sums)


def _row_tile(r, limit):
    if r <= limit:
        return r
    best = 8
    for cand in range(8, limit + 1, 8):
        if r % cand == 0:
            best = cand
    assert r % best == 0, r
    return best


def _adam_update(w, g, m, v):
    nm = ADAM_B1 * m + (1.0 - ADAM_B1) * g
    nv = ADAM_B2 * v + (1.0 - ADAM_B2) * (g * g)
    m_hat = nm / (1.0 - ADAM_B1 ** ADAM_STEP)
    v_hat = nv / (1.0 - ADAM_B2 ** ADAM_STEP)
    return -ADAM_LR * (m_hat / (jnp.sqrt(v_hat) + ADAM_EPS) + ADAM_WD * w), nm, nv


def _sum_adamw(own, recv, chip, w, m, v, name):
    _, r, c = own.shape
    ct = _col_tile(r, c, 1 << 20)

    def body(chip_ref, o_ref, r1_ref, r2_ref, r3_ref, w_ref, m_ref, v_ref, g_ref, d_ref, nm_ref, nv_ref):
        g = ((o_ref[...] + r1_ref[...].astype(F32)) + r2_ref[...].astype(F32)) + r3_ref[...].astype(F32)
        g_ref[...] = g
        d_ref[...], nm_ref[...], nv_ref[...] = _adam_update(w_ref[...], g, m_ref[...], v_ref[...])

    def slot(k):
        return pl.BlockSpec((None, r, ct), lambda i, chip_ref: ((chip_ref[0] + k) % N_CHIPS, 0, i))

    spec = pl.BlockSpec((r, ct), lambda i, chip_ref: (0, i))
    shape = jax.ShapeDtypeStruct((r, c), F32)
    return pl.pallas_call(
        body, name=name,
        grid_spec=pltpu.PrefetchScalarGridSpec(
            num_scalar_prefetch=1, grid=(c // ct,),
            in_specs=[slot(0), slot(1), slot(2), slot(3), spec, spec, spec], out_specs=[spec] * 4),
        out_shape=[shape] * 4,
        compiler_params=_params(("parallel",)),
    )(chip, own, recv, recv, recv, w, m, v)


def _small_adamw(parts, ws, ms, vs):
    n = len(parts)

    def body(*refs):
        p_refs, w_refs, m_refs, v_refs = refs[:n], refs[n:2 * n], refs[2 * n:3 * n], refs[3 * n:4 * n]
        outs = refs[4 * n:]
        for i in range(n):
            g = p_refs[i][0]
            for k in range(1, N_DEV):
                g = g + p_refs[i][k]
            outs[i][...] = g
            outs[n + i][...], outs[2 * n + i][...], outs[3 * n + i][...] = _adam_update(
                w_refs[i][...], g, m_refs[i][...], v_refs[i][...])

    vmem = pl.BlockSpec(memory_space=pltpu.VMEM)
    shapes = [jax.ShapeDtypeStruct(w.shape, F32) for w in ws]
    res = pl.pallas_call(
        body, name="adamw_small", in_specs=[vmem] * (4 * n), out_specs=[vmem] * (4 * n), out_shape=shapes * 4,
        compiler_params=pltpu.CompilerParams(vmem_limit_bytes=VMEM_LIMIT),
    )(*parts, *ws, *ms, *vs)
    return res[:n], res[n:2 * n], res[2 * n:3 * n], res[3 * n:]


def _pack(arrs, dtype, row_multiple):
    flat = jnp.concatenate([a.reshape(-1).astype(dtype) for a in arrs])
    unit = LANES * row_multiple
    padded = -(-flat.shape[0] // unit) * unit
    return jnp.pad(flat, (0, padded - flat.shape[0])).reshape(-1, LANES)


def _unpack(packed, shapes, lead=()):
    flat = packed.reshape(lead + (-1,))
    out, off = [], 0
    for shp in shapes:
        n = math.prod(shp)
        out.append(flat[..., off:off + n].reshape(lead + tuple(shp)))
        off += n
    return out


def _gather_cols(g, lo, hi):
    width = g.shape[2]
    pieces = []
    for s in range(N_DEV):
        a, e = max(lo, s * width), min(hi, (s + 1) * width)
        if a < e:
            pieces.append(g[s, :, a - s * width:e - s * width])
    return pieces[0] if len(pieces) == 1 else jnp.concatenate(pieces, axis=1)


def _scatter_cols(segs, width):
    slots = []
    for k in range(N_DEV):
        lo, hi = k * width, (k + 1) * width
        pieces = []
        for arr, s_lo, s_hi in segs:
            a, e = max(lo, s_lo), min(hi, s_hi)
            if a < e:
                pieces.append(arr[:, a - s_lo:e - s_lo])
        slots.append(pieces[0] if len(pieces) == 1 else jnp.concatenate(pieces, axis=1))
    return jnp.stack(slots)


def _block_diag_groups(w):
    w4 = w.reshape(LRU_NGROUPS, 4, LRU_BLOCK, LRU_BLOCK)
    eye = jnp.eye(4, dtype=w.dtype)
    return jnp.einsum("gaij,ab->gaibj", w4, eye).reshape(LRU_NGROUPS, LRU_GROUP, LRU_GROUP)


def _block_diag_extract(wg):
    w5 = wg.reshape(LRU_NGROUPS, 4, LRU_BLOCK, 4, LRU_BLOCK)
    idx = jnp.arange(4)
    return w5[:, idx, :, idx, :].transpose(1, 0, 2, 3).reshape(LRU_BLOCKS, LRU_BLOCK, LRU_BLOCK)


BIG = ("w_in", "w_kv", "w_br_ssd", "w_br_lru", "w_br_mem", "w_out")
SMALL_SHARDED = ("ssd_conv_w", "ssd_norm_g", "lru_conv_w")
REPLICATED = ("norm_g", "ssd_conv_b", "ssd_dt_bias", "ssd_a_log", "ssd_d", "lru_conv_b", "lru_w_a", "lru_b_a",
              "lru_w_x", "lru_b_x", "lru_lambda", "mem_norm_g", "final_g")
WEIGHTS = ("norm_g", "w_in", "ssd_conv_w", "ssd_conv_b", "ssd_dt_bias", "ssd_a_log", "ssd_d", "ssd_norm_g", "lru_conv_w",
           "lru_conv_b", "lru_w_a", "lru_b_a", "lru_w_x", "lru_b_x", "lru_lambda", "mem_norm_g", "w_kv", "w_br_ssd",
           "w_br_lru", "w_br_mem", "w_out", "final_g")


def kernel(x, mem, norm_g, w_in, ssd_conv_w, ssd_conv_b, ssd_dt_bias, ssd_a_log, ssd_d, ssd_norm_g, lru_conv_w, lru_conv_b, lru_w_a, lru_b_a, lru_w_x, lru_b_x, lru_lambda, mem_norm_g, w_kv, w_br_ssd, w_br_lru, w_br_mem, w_out, final_g, loss_target, m_norm_g, m_w_in, m_ssd_conv_w, m_ssd_conv_b, m_ssd_dt_bias, m_ssd_a_log, m_ssd_d, m_ssd_norm_g, m_lru_conv_w, m_lru_conv_b, m_lru_w_a, m_lru_b_a, m_lru_w_x, m_lru_b_x, m_lru_lambda, m_mem_norm_g, m_w_kv, m_w_br_ssd, m_w_br_lru, m_w_br_mem, m_w_out, m_final_g, v_norm_g, v_w_in, v_ssd_conv_w, v_ssd_conv_b, v_ssd_dt_bias, v_ssd_a_log, v_ssd_d, v_ssd_norm_g, v_lru_conv_w, v_lru_conv_b, v_lru_w_a, v_lru_b_a, v_lru_w_x, v_lru_b_x, v_lru_lambda, v_mem_norm_g, v_w_kv, v_w_br_ssd, v_w_br_lru, v_w_br_mem, v_w_out, v_final_g):
    env = dict(locals())
    W = {n: env[n] for n in WEIGHTS}
    M = {n: env["m_" + n] for n in WEIGHTS}
    V = {n: env["v_" + n] for n in WEIGHTS}
    me = 4 * lax.axis_index("x") + 2 * lax.axis_index("y") + lax.axis_index("c")
    t = x.shape[1]
    xt = x[0]
    memt = mem[0]
    tgt = loss_target[0]

    small_shapes = [W[n].shape for n in SMALL_SHARDED]
    as2d = lambda d, n: jnp.transpose(d[n][0]) if n == "w_in" else d[n][0]
    gathered = _all_gather([as2d(W, n).astype(MXU_DTYPE) for n in BIG] + [_pack([W[n] for n in SMALL_SHARDED], F32, 8)],
                           "weights_all_gather")
    g_in, g_kv, g_bs, g_bl, g_bm, g_out, gs = gathered
    g_cw, g_ng, g_lcw = _unpack(gs, small_shapes, (N_DEV,))
    cols = lambda a: jnp.moveaxis(a[:, 0], 0, -2).reshape(a.shape[2:-1] + (-1,))
    rows_ = lambda a: a.reshape((-1,) + a.shape[2:])
    w_bs_f, w_bl_f, w_bm_f, w_out_f = rows_(g_bs), rows_(g_bl), rows_(g_bm), rows_(g_out)
    conv_w_f, ssd_ng_f, lru_cw_f = cols(g_cw), cols(g_ng), cols(g_lcw)
    b = SEG_BOUNDS
    w_kv_f = _gather_cols(g_kv, 0, 2 * D_MODEL)
    w_in_t = g_in.reshape(IN_WIDTH, D_MODEL)
    w_ssd, w_lru, w_q, w_g = w_in_t[b[0]:b[1]], w_in_t[b[2]:b[3]], w_in_t[b[3]:b[4]], w_in_t[b[4]:b[5]]
    w_dt = jnp.pad(w_in_t[b[1]:b[2]], ((0, DT_PAD - SSD_HEADS), (0, 0)))

    pad_heads = lambda a: jnp.pad(a, ((0, 0), (0, LANES - SSD_HEADS)))
    dtb, alog = pad_heads(ssd_dt_bias), pad_heads(ssd_a_log)
    d_row = jnp.repeat(ssd_d, SSD_HEAD_DIM, axis=1)
    ng_row = ssd_ng_f.reshape(1, SSD_WIDTH)
    wa_g, wx_g = _block_diag_groups(lru_w_a[0]), _block_diag_groups(lru_w_x[0])
    ba, bx = lru_b_a.reshape(1, LRU_WIDTH), lru_b_x.reshape(1, LRU_WIDTH)
    fg = final_g.reshape(1, D_MODEL)

    h = _rms_fwd(xt, norm_g, "norm_fwd")
    proj_ssd = _matmul(h, w_ssd, "nt", "proj_ssd", tm=4096, tn=512)
    proj_lru = _matmul(h, w_lru, "nt", "proj_lru", tm=4096, tn=512)
    proj_q = _matmul(h, w_q, "nt", "proj_q", tm=4096, tn=512)
    proj_g = _matmul(h, w_g, "nt", "proj_g", tm=4096, tn=512)
    proj_dt = _matmul(h, w_dt, "nt", "proj_dt", tm=4096)
    mem_n = _rms_fwd(memt, mem_norm_g, "mem_norm_fwd")
    kv = _matmul(mem_n, w_kv_f, "nn", "mem_kv")
    yssd, y_scan, states, ssd_pre = _ssd_fwd(proj_ssd, proj_dt, conv_w_f, ssd_conv_b, dtb, alog, d_row, ng_row)
    ylru, h_lru = _lru_fwd(proj_lru, lru_cw_f, lru_conv_b, wa_g, wx_g, ba, bx, lru_lambda)
    ymem = _mem_fwd(proj_q, kv)
    ps, pl_, pm, merged, dx2, loss_vec, g_fg = _merge_fwd(xt, yssd, ylru, ymem, proj_g, w_bs_f, w_bl_f, w_bm_f, w_out_f, fg, tgt)

    d_g, dps, dpl, dpm, dyssd, dylru, dymem = _merge_bwd(dx2, proj_g, ps, pl_, pm, w_bs_f, w_bl_f, w_bm_f, w_out_f)
    gw_out = _matmul(merged, dx2, "tn", "grad_w_out", tk=2048)
    gw_bs = _matmul(yssd, dps, "tn", "grad_w_br_ssd", tm=2048)
    gw_bl = _matmul(ylru, dpl, "tn", "grad_w_br_lru", tm=LRU_WIDTH, tk=2048)
    gw_bm = _matmul(ymem, dpm, "tn", "grad_w_br_mem", tk=2048)
    d_q, d_kv = _mem_bwd(proj_q, kv, dymem)
    gw_kv = _matmul(mem_n, d_kv, "tn", "grad_w_kv")
    d_memn = _matmul(d_kv, w_kv_f, "nt", "d_mem_n")
    _, g_memng = _rms_bwd(memt, d_memn, None, mem_norm_g, "mem_norm_bwd")
    d_lru, gl_cw, gl_cb, g_ba, g_bx, g_lam, gwa_g, gwx_g = _lru_bwd(proj_lru, h_lru, dylru, lru_cw_f, lru_conv_b, wa_g, wx_g, ba, bx, lru_lambda)
    d_ssd, d_dt, gs_cw, gs_cb, g_dtb, g_alog, g_dch, g_ngrow = _ssd_bwd(proj_ssd, ssd_pre, proj_dt, y_scan, states, dyssd, conv_w_f, dtb, alog, d_row, ng_row)
    grad_x, g_normg = _dh_norm_bwd([(d_ssd, w_ssd), (d_lru, w_lru), (d_q, w_q), (d_g, w_g), (d_dt, w_dt)], xt, dx2, norm_g)
    gw_ssd = _matmul(d_ssd, h, "tn", "grad_w_in_ssd", tm=2560)
    gw_lru = _matmul(d_lru, h, "tn", "grad_w_in_lru", tm=1536, tk=2048)
    gw_q = _matmul(d_q, h, "tn", "grad_w_in_q", tk=2048)
    gw_g = _matmul(d_g, h, "tn", "grad_w_in_g", tm=1536, tk=2048)
    gw_dt = _matmul(d_dt, h, "tn", "grad_w_in_dt", tk=2048)

    split_rows = lambda a: a.reshape((N_DEV, -1) + a.shape[1:])
    gw_in_t = jnp.concatenate([gw_ssd, gw_dt[:SSD_HEADS], gw_lru, gw_q, gw_g], axis=0)
    big_send = [split_rows(gw_in_t), _scatter_cols([(gw_kv, 0, 2 * D_MODEL)], 2 * D_MODEL // N_DEV),
                split_rows(gw_bs), split_rows(gw_bl), split_rows(gw_bm), split_rows(gw_out)]

    small_grads = {
        "norm_g": g_normg, "ssd_conv_w": gs_cw, "ssd_conv_b": gs_cb, "ssd_dt_bias": g_dtb[:, :SSD_HEADS],
        "ssd_a_log": g_alog[:, :SSD_HEADS], "ssd_d": jnp.sum(g_dch.reshape(SSD_HEADS, SSD_HEAD_DIM), axis=1).reshape(1, SSD_HEADS),
        "ssd_norm_g": g_ngrow.reshape(SSD_GROUPS, -1), "lru_conv_w": gl_cw, "lru_conv_b": gl_cb,
        "lru_w_a": _block_diag_extract(gwa_g), "lru_b_a": g_ba, "lru_w_x": _block_diag_extract(gwx_g), "lru_b_x": g_bx,
        "lru_lambda": g_lam, "mem_norm_g": g_memng, "final_g": g_fg,
    }
    small_all = REPLICATED + SMALL_SHARDED

    def small_shape(n, shards):
        shp = W[n].shape[1:] if W[n].ndim > 2 else (1, W[n].shape[-1])
        return shp[:-1] + (shp[-1] * shards,)

    small_recv = _all_gather([small_grads[n].reshape(small_shape(n, N_DEV if n in SMALL_SHARDED else 1)) for n in small_all],
                             "small_grads_all_gather")

    core = lax.axis_index("c").astype(jnp.int32).reshape(1)
    chip = (2 * lax.axis_index("x") + lax.axis_index("y")).astype(jnp.int32).reshape(1)
    by_chip = [a.reshape((N_CHIPS, 2) + a.shape[1:]) for a in big_send]
    from_sibling = _pair_exchange(by_chip)
    chip_sums = [_chip_sum(p, r, core, "chip_sum_" + n) for n, p, r in zip(BIG, by_chip, from_sibling)]
    from_chips = _chip_exchange([s16 for _, s16 in chip_sums])

    grads, delta, new_m, new_v = {}, {}, {}, {}
    for n, (s32, _), recv in zip(BIG, chip_sums, from_chips):
        res = _sum_adamw(s32, recv, chip, as2d(W, n), as2d(M, n), as2d(V, n), "adamw_" + n)
        for dst, a in zip((grads, delta, new_m, new_v), res):
            dst[n] = (jnp.transpose(a) if n == "w_in" else a)[None]

    parts = []
    for n, a in zip(small_all, small_recv):
        if n in SMALL_SHARDED:
            width = W[n].shape[-1]
            a = lax.dynamic_slice_in_dim(a, me * width, width, axis=a.ndim - 1)
        parts.append(a)
    canon = lambda d: [d[n].reshape(small_shape(n, 1)) for n in small_all]
    for dst, res in zip((grads, delta, new_m, new_v), _small_adamw(parts, canon(W), canon(M), canon(V))):
        for n, a in zip(small_all, res):
            dst[n] = a.reshape(W[n].shape)

    loss = lax.psum(loss_vec[0, 0], ("x", "y", "c"))
    return (loss, grad_x[None], *[grads[n] for n in WEIGHTS], *[delta[n] for n in WEIGHTS],
            *[new_m[n] for n in WEIGHTS], *[new_v[n] for n in WEIGHTS])
```

```python
import functools
import math

import jax
import jax.numpy as jnp
from jax import lax
from jax.experimental import pallas as pl
from jax.experimental.pallas import tpu as pltpu

F32 = jnp.float32
MXU_DTYPE = jnp.bfloat16
GRAD_WIRE_DTYPE = jnp.bfloat16

D_MODEL = 1024
EPS = 1e-6
CONV_WIDTH = 4
SSD_WIDTH = 2048
SSD_HEAD_DIM = 64
SSD_HEADS = 32
SSD_GROUPS = 4
SSD_STATE = 128
SSD_CHUNK = 128
SSD_CONV_CH = SSD_WIDTH + 2 * SSD_GROUPS * SSD_STATE
SSD_PAIRS = SSD_HEADS // 2
PAIRS_PER_GROUP = SSD_PAIRS // SSD_GROUPS
GROUP_COLS = SSD_WIDTH // SSD_GROUPS
LRU_WIDTH = 1536
LRU_BLOCKS = 16
LRU_BLOCK = 96
LRU_GROUP = 4 * LRU_BLOCK
LRU_NGROUPS = LRU_WIDTH // LRU_GROUP
LRU_C = 8.0
LRU_ROWS = 256
MEM_HEADS = 4
MEM_HEAD_DIM = 256
IN_WIDTH = 12320
N_DEV = 8
LANES = 128
SSD_SEG = SSD_WIDTH + SSD_CONV_CH
DT_PAD = LANES
SEG_BOUNDS = (0, 5120, 5152, 8224, 9248, 12320)

ADAM_LR = 0.001
ADAM_B1 = 0.9
ADAM_B2 = 0.999
ADAM_EPS = 1e-08
ADAM_WD = 0.01
ADAM_STEP = 10

VMEM_LIMIT = 56 * 1024 * 1024

NN = (((1,), (0,)), ((), ()))
NT = (((1,), (1,)), ((), ()))
TN = (((0,), (0,)), ((), ()))


def _dot(a, b, dims):
    return lax.dot_general(a.astype(MXU_DTYPE), b.astype(MXU_DTYPE), dims, preferred_element_type=F32)


def _sigmoid(x):
    return 0.5 * jnp.tanh(0.5 * x) + 0.5


def _log1p(e):
    u = 1.0 + e
    return jnp.where(u == 1.0, e, jnp.log(u) * (e / jnp.where(u == 1.0, 1.0, u - 1.0)))


def _softplus(x):
    return jnp.maximum(x, 0.0) + _log1p(jnp.exp(-jnp.abs(x)))


def _params(semantics):
    return pltpu.CompilerParams(dimension_semantics=semantics, vmem_limit_bytes=VMEM_LIMIT)


def _shift_down(cur, halo8, k):
    rolled = pltpu.roll(cur, k, 0)
    row8 = lax.broadcasted_iota(jnp.int32, halo8.shape, 0)
    top = jnp.where(row8 >= k, rolled[0:8], pltpu.roll(halo8, k, 0))
    return jnp.concatenate([top, rolled[8:]], axis=0)


def _shift_up(cur, next8, k):
    rows = cur.shape[0]
    rolled = pltpu.roll(cur, rows - k, 0)
    row8 = lax.broadcasted_iota(jnp.int32, next8.shape, 0)
    bot = jnp.where(row8 < 8 - k, rolled[rows - 8:rows], pltpu.roll(next8, 8 - k, 0))
    return jnp.concatenate([rolled[:rows - 8], bot], axis=0)


def _causal_conv(raw, halo8, w, b):
    acc = raw * w[3:4, :] + b
    for k in range(1, CONV_WIDTH):
        acc = acc + _shift_down(raw, halo8, k) * w[3 - k:4 - k, :]
    return acc


def _conv_backward(dco, next8, raw, w):
    d_raw = dco * w[3:4, :]
    gw = [None] * CONV_WIDTH
    gw[3] = jnp.sum(dco * raw, axis=0, keepdims=True)
    for j in range(1, CONV_WIDTH):
        up = _shift_up(dco, next8, j)
        d_raw = d_raw + up * w[3 - j:4 - j, :]
        gw[3 - j] = jnp.sum(up * raw, axis=0, keepdims=True)
    gb = jnp.sum(dco, axis=0, keepdims=True)
    return d_raw, gw, gb


def _cumsum_rows(v):
    rows = v.shape[0]
    row = lax.broadcasted_iota(jnp.int32, v.shape, 0)
    s = 1
    while s < rows:
        v = v + jnp.where(row >= s, pltpu.roll(v, s, 0), 0.0)
        s *= 2
    return v


def _rev_cumsum_rows(v):
    rows = v.shape[0]
    row = lax.broadcasted_iota(jnp.int32, v.shape, 0)
    s = 1
    while s < rows:
        v = v + jnp.where(row < rows - s, pltpu.roll(v, rows - s, 0), 0.0)
        s *= 2
    return v


def _matmul(a, b, mode, name, tm=1024, tn=1024, tk=1024, exchange=None):
    if mode == "nn":
        (m, kk), n = a.shape, b.shape[1]
    elif mode == "nt":
        (m, kk), n = a.shape, b.shape[0]
    else:
        (kk, m), n = a.shape, b.shape[1]
    tm, tn, tk = min(tm, m), min(tn, n), min(tk, kk)
    assert m % tm == 0 and n % tn == 0 and kk % tk == 0, (name, a.shape, b.shape)
    nk = kk // tk
    dims = {"nn": NN, "nt": NT, "tn": TN}[mode]
    a_spec = pl.BlockSpec((tk, tm), lambda i, j, k: (k, i)) if mode == "tn" else pl.BlockSpec((tm, tk), lambda i, j, k: (i, k))
    b_spec = pl.BlockSpec((tn, tk), lambda i, j, k: (j, k)) if mode == "nt" else pl.BlockSpec((tk, tn), lambda i, j, k: (k, j))
    o_spec = pl.BlockSpec((tm, tn), lambda i, j, k: (i, j))

    def body_single(a_ref, b_ref, o_ref):
        o_ref[...] = _dot(a_ref[...], b_ref[...], dims)

    def body(a_ref, b_ref, o_ref, acc_ref):
        k = pl.program_id(2)

        @pl.when(k == 0)
        def _():
            acc_ref[...] = jnp.zeros_like(acc_ref)

        acc_ref[...] += _dot(a_ref[...], b_ref[...], dims)

        @pl.when(k == nk - 1)
        def _():
            o_ref[...] = acc_ref[...]

    grid = (m // tm, n // tn, nk)
    if exchange is None:
        return pl.pallas_call(
            body_single if nk == 1 else body, name=name, grid=grid, in_specs=[a_spec, b_spec], out_specs=o_spec,
            out_shape=jax.ShapeDtypeStruct((m, n), F32),
            scratch_shapes=[] if nk == 1 else [pltpu.VMEM((tm, tn), F32)],
            compiler_params=_params(("parallel", "parallel", "arbitrary")),
        )(a, b)
    at = lambda ids: functools.reduce(lambda u, v: u & v, [pl.program_id(d) == ids[d] for d in range(3)])
    riding, ex_in, ex_out, ex_shape, ex_sems = _riding(
        exchange, body_single if nk == 1 else body, 2, 1, lambda: at((0, 0, 0)), lambda: at(tuple(g - 1 for g in grid)))
    res = pl.pallas_call(
        riding, name=name, grid=grid, in_specs=[a_spec, b_spec] + ex_in, out_specs=[o_spec] + ex_out,
        out_shape=[jax.ShapeDtypeStruct((m, n), F32)] + ex_shape,
        scratch_shapes=([] if nk == 1 else [pltpu.VMEM((tm, tn), F32)]) + ex_sems,
        compiler_params=_params(("arbitrary", "arbitrary", "arbitrary")),
    )(a, b, *exchange["arrays"])
    return res[0], res[1:]


def _rms_fwd(x, g, name, rows=512):
    t, d = x.shape
    rows = min(rows, t)

    def body(x_ref, g_ref, h_ref):
        xv = x_ref[...]
        r = lax.rsqrt(jnp.mean(xv * xv, axis=-1, keepdims=True) + EPS)
        h_ref[...] = ((xv * r) * g_ref[...]).astype(h_ref.dtype)

    return pl.pallas_call(
        body, name=name, grid=(t // rows,),
        in_specs=[pl.BlockSpec((rows, d), lambda i: (i, 0)), pl.BlockSpec((1, d), lambda i: (0, 0))],
        out_specs=pl.BlockSpec((rows, d), lambda i: (i, 0)),
        out_shape=jax.ShapeDtypeStruct((t, d), MXU_DTYPE),
        compiler_params=_params(("parallel",)),
    )(x, g)


def _rms_bwd(x, dh, dres, g, name, rows=512):
    t, d = x.shape
    rows = min(rows, t)
    has_res = dres is not None

    def body(*refs):
        if has_res:
            x_ref, dh_ref, dr_ref, g_ref, dx_ref, gg_ref = refs
        else:
            x_ref, dh_ref, g_ref, dx_ref, gg_ref = refs

        @pl.when(pl.program_id(0) == 0)
        def _():
            gg_ref[...] = jnp.zeros_like(gg_ref)

        xv = x_ref[...]
        dhv = dh_ref[...]
        r = lax.rsqrt(jnp.mean(xv * xv, axis=-1, keepdims=True) + EPS)
        n = xv * r
        dn = dhv * g_ref[...]
        dx = r * (dn - n * jnp.mean(dn * n, axis=-1, keepdims=True))
        if has_res:
            dx = dx + dr_ref[...]
        dx_ref[...] = dx
        gg_ref[...] += jnp.sum(dhv * n, axis=0, keepdims=True)

    row_spec = pl.BlockSpec((rows, d), lambda i: (i, 0))
    vec_spec = pl.BlockSpec((1, d), lambda i: (0, 0))
    args = (x, dh) + ((dres,) if has_res else ()) + (g,)
    return pl.pallas_call(
        body, name=name, grid=(t // rows,),
        in_specs=[row_spec, row_spec] + ([row_spec] if has_res else []) + [vec_spec],
        out_specs=[row_spec, vec_spec],
        out_shape=[jax.ShapeDtypeStruct((t, d), F32), jax.ShapeDtypeStruct((1, d), F32)],
        compiler_params=_params(("arbitrary",)),
    )(*args)


def _dh_norm_bwd(segs, x, dres, g, rows=1024, tk=512, exchange=None):
    t, d = x.shape
    rows = min(rows, t)
    plan = []
    step0 = 0
    for a, _ in segs:
        kb = min(tk, a.shape[1])
        assert a.shape[1] % kb == 0, a.shape
        plan.append((step0, a.shape[1] // kb, kb))
        step0 += a.shape[1] // kb
    n_steps = step0
    ns = len(segs)

    def body(*refs):
        a_refs, w_refs = refs[0:2 * ns:2], refs[1:2 * ns:2]
        x_ref, dr_ref, g_ref, dx_ref, gg_ref, acc_ref = refs[2 * ns:]
        i, k = pl.program_id(0), pl.program_id(1)

        @pl.when((i == 0) & (k == 0))
        def _():
            gg_ref[...] = jnp.zeros_like(gg_ref)

        @pl.when(k == 0)
        def _():
            acc_ref[...] = jnp.zeros_like(acc_ref)

        for s, (first, nblk, _) in enumerate(plan):
            @pl.when((k >= first) & (k < first + nblk))
            def _(s=s):
                acc_ref[...] += _dot(a_refs[s][...], w_refs[s][...], NN)

        @pl.when(k == n_steps - 1)
        def _():
            xv = x_ref[...]
            dhv = acc_ref[...]
            r = lax.rsqrt(jnp.mean(xv * xv, axis=-1, keepdims=True) + EPS)
            n = xv * r
            dn = dhv * g_ref[...]
            dx_ref[...] = r * (dn - n * jnp.mean(dn * n, axis=-1, keepdims=True)) + dr_ref[...]
            gg_ref[...] += jnp.sum(dhv * n, axis=0, keepdims=True)

    in_specs, args = [], []
    for (a, w), (first, nblk, kb) in zip(segs, plan):
        blk = lambda k, first=first, nblk=nblk: jnp.clip(k - first, 0, nblk - 1)
        in_specs.append(pl.BlockSpec((rows, kb), lambda i, k, blk=blk: (i, blk(k))))
        in_specs.append(pl.BlockSpec((kb, d), lambda i, k, blk=blk: (blk(k), 0)))
        args += [a, w]
    row_spec = pl.BlockSpec((rows, d), lambda i, k: (i, 0))
    vec_spec = pl.BlockSpec((1, d), lambda i, k: (0, 0))
    n_tiles = t // rows
    body, ex_in, ex_out, ex_shape, ex_sems = _riding(
        exchange, body, 2 * ns + 3, 2,
        lambda: (pl.program_id(0) == 0) & (pl.program_id(1) == 0),
        lambda: (pl.program_id(0) == n_tiles - 1) & (pl.program_id(1) == n_steps - 1))
    res = pl.pallas_call(
        body, name="dh_norm_bwd", grid=(n_tiles, n_steps),
        in_specs=in_specs + [row_spec, row_spec, vec_spec] + ex_in, out_specs=[row_spec, vec_spec] + ex_out,
        out_shape=[jax.ShapeDtypeStruct((t, d), F32), jax.ShapeDtypeStruct((1, d), F32)] + ex_shape,
        scratch_shapes=[pltpu.VMEM((rows, d), F32)] + ex_sems,
        compiler_params=_params(("arbitrary", "arbitrary")),
    )(*args, x, dres, g, *(exchange["arrays"] if exchange else []))
    return res[0], res[1], res[2:]


def _pair_select(lo, m, h0):
    return jnp.where(lo, m[:, h0:h0 + 1], m[:, h0 + 1:h0 + 2])


def _group_select(lo, m, heads):
    return jnp.concatenate([_pair_select(lo, m, h0) for h0 in heads], axis=1)


def _halves(lo, v):
    return (jnp.sum(jnp.where(lo, v, 0.0), axis=1, keepdims=True),
            jnp.sum(jnp.where(lo, 0.0, v), axis=1, keepdims=True))


def _ssd_common(dt_raw, dtb, alog):
    dt = _softplus(dt_raw + dtb)
    aneg = -jnp.exp(alog)
    a_cs = _cumsum_rows(dt * aneg)
    return dt, aneg, a_cs, a_cs.T


def _ssd_specs(nc, rev):
    cidx = (lambda c: nc - 1 - c) if rev else (lambda c: c)
    L = SSD_CHUNK
    return dict(
        z=pl.BlockSpec((L, SSD_WIDTH), lambda c: (cidx(c), 0)),
        xr=pl.BlockSpec((L, SSD_WIDTH), lambda c: (cidx(c), 1)),
        br=pl.BlockSpec((L, 512), lambda c: (cidx(c), 8)),
        cr=pl.BlockSpec((L, 512), lambda c: (cidx(c), 9)),
        dt=pl.BlockSpec((L, DT_PAD), lambda c: (cidx(c), 0)),
        cwx=pl.BlockSpec((CONV_WIDTH, SSD_WIDTH), lambda c: (0, 0)),
        cwb=pl.BlockSpec((CONV_WIDTH, 512), lambda c: (0, 4)),
        cwc=pl.BlockSpec((CONV_WIDTH, 512), lambda c: (0, 5)),
        cbx=pl.BlockSpec((1, SSD_WIDTH), lambda c: (0, 0)),
        cbb=pl.BlockSpec((1, 512), lambda c: (0, 4)),
        cbc=pl.BlockSpec((1, 512), lambda c: (0, 5)),
        vec128=pl.BlockSpec((1, LANES), lambda c: (0, 0)),
        vecw=pl.BlockSpec((1, SSD_WIDTH), lambda c: (0, 0)),
        wide=pl.BlockSpec((L, SSD_WIDTH), lambda c: (cidx(c), 0)),
        states=pl.BlockSpec((1, SSD_GROUPS, GROUP_COLS, SSD_STATE), lambda c: (cidx(c), 0, 0, 0)),
    )


def _ssd_fwd(proj_ssd, dt_p, conv_w, conv_b, dtb, alog, d_row, ng_row):
    t = proj_ssd.shape[0]
    nc = t // SSD_CHUNK
    L = SSD_CHUNK
    sp = _ssd_specs(nc, False)

    def body(z_ref, xr_ref, br_ref, cr_ref, dt_ref, cwx_ref, cwb_ref, cwc_ref, cbx_ref, cbb_ref, cbc_ref,
             dtb_ref, alog_ref, d_ref, ng_ref, yssd_ref, y_ref, st_ref, pre_ref,
             hx_ref, hb_ref, hc_ref, state_ref, yacc_ref):
        @pl.when(pl.program_id(0) == 0)
        def _():
            hx_ref[...] = jnp.zeros_like(hx_ref)
            hb_ref[...] = jnp.zeros_like(hb_ref)
            hc_ref[...] = jnp.zeros_like(hc_ref)
            state_ref[...] = jnp.zeros_like(state_ref)

        xr, br, cr = xr_ref[...], br_ref[...], cr_ref[...]
        px = _causal_conv(xr, hx_ref[...], cwx_ref[...], cbx_ref[...])
        pb = _causal_conv(br, hb_ref[...], cwb_ref[...], cbb_ref[...])
        pc = _causal_conv(cr, hc_ref[...], cwc_ref[...], cbc_ref[...])
        hx_ref[...] = xr[L - 8:L, :]
        hb_ref[...] = br[L - 8:L, :]
        hc_ref[...] = cr[L - 8:L, :]
        pre_ref[:, 0:SSD_WIDTH] = px
        pre_ref[:, SSD_WIDTH:SSD_WIDTH + 512] = pb
        pre_ref[:, SSD_WIDTH + 512:SSD_CONV_CH] = pc
        xs = px * _sigmoid(px)
        bm = pb * _sigmoid(pb)
        cm = pc * _sigmoid(pc)

        dt, _, a_cs, a_t = _ssd_common(dt_ref[...], dtb_ref[...], alog_ref[...])
        exp_a = jnp.exp(a_cs)
        a_last = a_cs[L - 1:L, :]
        dte = jnp.exp(a_last - a_cs)
        dec = jnp.exp(a_last)

        lane = lax.broadcasted_iota(jnp.int32, (L, LANES), 1)
        sub = lax.broadcasted_iota(jnp.int32, (L, LANES), 0)
        lo = lane < SSD_HEAD_DIM
        causal = sub >= lane
        top = sub < SSD_HEAD_DIM

        for g in range(SSD_GROUPS):
            b_g = bm[:, g * SSD_STATE:(g + 1) * SSD_STATE]
            c_g = cm[:, g * SSD_STATE:(g + 1) * SSD_STATE]
            cb = _dot(c_g, b_g, NT)
            heads = [2 * (g * PAIRS_PER_GROUP + jj) for jj in range(PAIRS_PER_GROUP)]
            gcols = slice(g * GROUP_COLS, (g + 1) * GROUP_COLS)
            xs_g = xs[:, gcols]
            xdt_g = xs_g * _group_select(lo, dt, heads)
            h_g = state_ref[g]
            st_ref[0, g] = h_g
            y_off_g = _dot(c_g, h_g, NT) * _group_select(lo, exp_a, heads)
            s_new_g = _dot(xdt_g * _group_select(lo, dte, heads), b_g, TN)
            for jj, h0 in enumerate(heads):
                blk = slice(jj * LANES, (jj + 1) * LANES)
                cols = slice(g * GROUP_COLS + jj * LANES, g * GROUP_COLS + (jj + 1) * LANES)
                xdt = xdt_g[:, blk]
                g0 = jnp.where(causal, jnp.exp(a_cs[:, h0:h0 + 1] - a_t[h0:h0 + 1, :]), 0.0) * cb
                g1 = jnp.where(causal, jnp.exp(a_cs[:, h0 + 1:h0 + 2] - a_t[h0 + 1:h0 + 2, :]), 0.0) * cb
                lhs = jnp.concatenate([g0, g1], axis=1)
                rhs = jnp.concatenate([jnp.where(lo, xdt, 0.0), jnp.where(lo, 0.0, xdt)], axis=0)
                y_diag = _dot(lhs, rhs, NN)
                dec_rows = jnp.where(top, dec[:, h0:h0 + 1], dec[:, h0 + 1:h0 + 2])
                state_ref[g, blk, :] = h_g[blk, :] * dec_rows + s_new_g[blk, :]
                yacc_ref[:, cols] = (y_diag + y_off_g[:, blk]) + xs_g[:, blk] * d_ref[:, cols]

        y = yacc_ref[...]
        y_ref[...] = y
        zz = z_ref[...]
        y2 = y * (zz * _sigmoid(zz))
        gw = SSD_WIDTH // SSD_GROUPS
        for g in range(SSD_GROUPS):
            seg = y2[:, g * gw:(g + 1) * gw]
            r = lax.rsqrt(jnp.mean(seg * seg, axis=-1, keepdims=True) + EPS)
            yssd_ref[:, g * gw:(g + 1) * gw] = ((seg * r) * ng_ref[:, g * gw:(g + 1) * gw]).astype(yssd_ref.dtype)

    return pl.pallas_call(
        body, name="ssd_fwd", grid=(nc,),
        in_specs=[sp["z"], sp["xr"], sp["br"], sp["cr"], sp["dt"], sp["cwx"], sp["cwb"], sp["cwc"],
                  sp["cbx"], sp["cbb"], sp["cbc"], sp["vec128"], sp["vec128"], sp["vecw"], sp["vecw"]],
        out_specs=[sp["wide"], sp["wide"], sp["states"], pl.BlockSpec((L, SSD_CONV_CH), lambda c: (c, 0))],
        out_shape=[jax.ShapeDtypeStruct((t, SSD_WIDTH), MXU_DTYPE), jax.ShapeDtypeStruct((t, SSD_WIDTH), F32),
                   jax.ShapeDtypeStruct((nc, SSD_GROUPS, GROUP_COLS, SSD_STATE), F32), jax.ShapeDtypeStruct((t, SSD_CONV_CH), F32)],
        scratch_shapes=[pltpu.VMEM((8, SSD_WIDTH), F32), pltpu.VMEM((8, 512), F32), pltpu.VMEM((8, 512), F32),
                        pltpu.VMEM((SSD_GROUPS, GROUP_COLS, SSD_STATE), F32), pltpu.VMEM((L, SSD_WIDTH), F32)],
        compiler_params=_params(("arbitrary",)),
    )(proj_ssd, proj_ssd, proj_ssd, proj_ssd, dt_p, conv_w, conv_w, conv_w, conv_b, conv_b, conv_b,
      dtb, alog, d_row, ng_row)


def _ssd_bwd(proj_ssd, pre, dt_p, y, states, dyssd, conv_w, dtb, alog, d_row, ng_row, exchange=None):
    t = proj_ssd.shape[0]
    nc = t // SSD_CHUNK
    L = SSD_CHUNK
    sp = _ssd_specs(nc, True)

    def pre_spec(width, col):
        return pl.BlockSpec((L, width), lambda c: (nc - 1 - c, col))

    def body(z_ref, xr_ref, br_ref, cr_ref, px_ref, pb_ref, pc_ref, dt_ref, y_ref, st_ref, dy_ref,
             cwx_ref, cwb_ref, cwc_ref, dtb_ref, alog_ref, d_ref, ng_ref,
             dssd_ref, ddt_ref, gcw_ref, gcb_ref, gdtb_ref, galog_ref, gd_ref, gng_ref,
             gn_ref, nx_ref, nb_ref, ncc_ref, dxs_ref):
        step = pl.program_id(0)

        @pl.when(step == 0)
        def _():
            gn_ref[...] = jnp.zeros_like(gn_ref)
            nx_ref[...] = jnp.zeros_like(nx_ref)
            nb_ref[...] = jnp.zeros_like(nb_ref)
            ncc_ref[...] = jnp.zeros_like(ncc_ref)
            for ref in (gcw_ref, gcb_ref, gdtb_ref, galog_ref, gd_ref, gng_ref):
                ref[...] = jnp.zeros_like(ref)

        xr, br, cr = xr_ref[...], br_ref[...], cr_ref[...]
        cwx, cwb, cwc = cwx_ref[...], cwb_ref[...], cwc_ref[...]
        px, pb, pc = px_ref[...], pb_ref[...], pc_ref[...]
        sx, sb, sc = _sigmoid(px), _sigmoid(pb), _sigmoid(pc)
        xs, bm, cm = px * sx, pb * sb, pc * sc

        dt_in = dt_ref[...] + dtb_ref[...]
        dt, aneg, a_cs, a_t = _ssd_common(dt_ref[...], dtb_ref[...], alog_ref[...])
        exp_a = jnp.exp(a_cs)
        a_last = a_cs[L - 1:L, :]
        dte = jnp.exp(a_last - a_cs)
        dec = jnp.exp(a_last)

        lane = lax.broadcasted_iota(jnp.int32, (L, LANES), 1)
        sub = lax.broadcasted_iota(jnp.int32, (L, LANES), 0)
        lo = lane < SSD_HEAD_DIM
        causal = sub >= lane
        top = sub < SSD_HEAD_DIM
        last_row = sub == L - 1

        yv = y_ref[...]
        zz = z_ref[...]
        sz = _sigmoid(zz)
        silz = zz * sz
        y2 = yv * silz
        dyv = dy_ref[...]
        gw = SSD_WIDTH // SSD_GROUPS
        d_y2_parts = []
        gng_parts = []
        for g in range(SSD_GROUPS):
            seg = y2[:, g * gw:(g + 1) * gw]
            dseg = dyv[:, g * gw:(g + 1) * gw]
            r = lax.rsqrt(jnp.mean(seg * seg, axis=-1, keepdims=True) + EPS)
            n = seg * r
            dn = dseg * ng_ref[:, g * gw:(g + 1) * gw]
            gng_parts.append(jnp.sum(dseg * n, axis=0, keepdims=True))
            d_y2_parts.append(r * (dn - n * jnp.mean(dn * n, axis=-1, keepdims=True)))
        d_y2 = jnp.concatenate(d_y2_parts, axis=1)
        gng_ref[...] += jnp.concatenate(gng_parts, axis=1)
        d_y = d_y2 * silz
        dssd_ref[:, 0:SSD_WIDTH] = (d_y2 * yv * (sz * (1.0 + zz * (1.0 - sz)))).astype(dssd_ref.dtype)
        gd_ref[...] += jnp.sum(d_y * xs, axis=0, keepdims=True)
        dxs_ref[...] = d_y * d_ref[...]

        d_a = jnp.zeros((L, LANES), F32)
        d_at = jnp.zeros((LANES, L), F32)
        ddt = jnp.zeros((L, LANES), F32)
        d_b_parts, d_c_parts = [], []
        for g in range(SSD_GROUPS):
            b_g = bm[:, g * SSD_STATE:(g + 1) * SSD_STATE]
            c_g = cm[:, g * SSD_STATE:(g + 1) * SSD_STATE]
            cb = _dot(c_g, b_g, NT)
            d_cb = jnp.zeros((L, L), F32)
            heads = [2 * (g * PAIRS_PER_GROUP + jj) for jj in range(PAIRS_PER_GROUP)]
            gcols = slice(g * GROUP_COLS, (g + 1) * GROUP_COLS)
            dy_g, xs_g = d_y[:, gcols], xs[:, gcols]
            dt_g = _group_select(lo, dt, heads)
            expa_g = _group_select(lo, exp_a, heads)
            dte_g = _group_select(lo, dte, heads)
            xdt_g = xs_g * dt_g
            h_g = st_ref[0, g]
            gn_g = gn_ref[g]
            dys_g = dy_g * expa_g
            d_cg = _dot(dys_g, h_g, NN)
            d_h_g = _dot(dys_g, c_g, TN)
            t1_g = dy_g * _dot(c_g, h_g, NT) * expa_g
            d_bg = _dot(xdt_g * dte_g, gn_g, NN)
            dxdt_g = _dot(b_g, gn_g, NT) * dte_g
            t2_g = dxdt_g * xdt_g
            t12_g = t1_g - t2_g
            gh_g = jnp.sum(gn_g * h_g, axis=1, keepdims=True)
            for jj, h0 in enumerate(heads):
                blk = slice(jj * LANES, (jj + 1) * LANES)
                cols = slice(g * GROUP_COLS + jj * LANES, g * GROUP_COLS + (jj + 1) * LANES)
                dy_p, xs_p, xdt, dt_pp = dy_g[:, blk], xs_g[:, blk], xdt_g[:, blk], dt_g[:, blk]
                l0 = jnp.where(causal, jnp.exp(a_cs[:, h0:h0 + 1] - a_t[h0:h0 + 1, :]), 0.0)
                l1 = jnp.where(causal, jnp.exp(a_cs[:, h0 + 1:h0 + 2] - a_t[h0 + 1:h0 + 2, :]), 0.0)
                g0, g1 = l0 * cb, l1 * cb
                dcat = jnp.concatenate([jnp.where(lo, dy_p, 0.0), jnp.where(lo, 0.0, dy_p)], axis=0)
                d_xdt = dxdt_g[:, blk] + _dot(jnp.concatenate([g0, g1], axis=0), dcat, TN)
                dm = _dot(dcat, xdt, NT)
                dm0, dm1 = dm[0:L], dm[L:2 * L]
                d_cb = d_cb + (l0 * dm0 + l1 * dm1)
                e0, e1 = dm0 * g0, dm1 * g1
                a0, a1 = _halves(lo, t12_g[:, blk])
                a0 = a0 + jnp.sum(e0, axis=1, keepdims=True)
                a1 = a1 + jnp.sum(e1, axis=1, keepdims=True)
                s0, s1 = _halves(lo, t2_g[:, blk])
                gh = gh_g[blk, :]
                dd0 = jnp.sum(jnp.where(top[:, 0:1], gh, 0.0), axis=0, keepdims=True)
                dd1 = jnp.sum(jnp.where(top[:, 0:1], 0.0, gh), axis=0, keepdims=True)
                end0 = jnp.sum(s0, axis=0, keepdims=True) + dd0 * dec[:, h0:h0 + 1]
                end1 = jnp.sum(s1, axis=0, keepdims=True) + dd1 * dec[:, h0 + 1:h0 + 2]
                d_a = d_a + jnp.where(lane == h0, a0 + jnp.where(last_row, end0, 0.0), 0.0)
                d_a = d_a + jnp.where(lane == h0 + 1, a1 + jnp.where(last_row, end1, 0.0), 0.0)
                d_at = d_at - jnp.where(sub == h0, jnp.sum(e0, axis=0, keepdims=True), 0.0)
                d_at = d_at - jnp.where(sub == h0 + 1, jnp.sum(e1, axis=0, keepdims=True), 0.0)
                dec_rows = jnp.where(top, dec[:, h0:h0 + 1], dec[:, h0 + 1:h0 + 2])
                gn_ref[g, blk, :] = d_h_g[blk, :] + dec_rows * gn_g[blk, :]
                q0, q1 = _halves(lo, d_xdt * xs_p)
                ddt = ddt + jnp.where(lane == h0, q0, 0.0) + jnp.where(lane == h0 + 1, q1, 0.0)
                dxs_ref[:, cols] += d_xdt * dt_pp
            d_cg = d_cg + _dot(d_cb, b_g, NN)
            d_bg = d_bg + _dot(d_cb, c_g, TN)
            d_b_parts.append(d_bg)
            d_c_parts.append(d_cg)

        rc = _rev_cumsum_rows(d_a + d_at.T)
        d_dt = rc * aneg + ddt
        galog_ref[...] += jnp.sum(rc * dt, axis=0, keepdims=True) * aneg
        d_dtraw = d_dt * _sigmoid(dt_in)
        gdtb_ref[...] += jnp.sum(d_dtraw, axis=0, keepdims=True)
        ddt_ref[...] = d_dtraw.astype(ddt_ref.dtype)

        def dsilu(p, s):
            return s * (1.0 + p * (1.0 - s))

        dcx = dxs_ref[...] * dsilu(px, sx)
        dcb = jnp.concatenate(d_b_parts, axis=1) * dsilu(pb, sb)
        dcc = jnp.concatenate(d_c_parts, axis=1) * dsilu(pc, sc)
        drx, gwx, gbx = _conv_backward(dcx, nx_ref[...], xr, cwx)
        drb, gwb, gbb = _conv_backward(dcb, nb_ref[...], br, cwb)
        drc, gwc, gbc = _conv_backward(dcc, ncc_ref[...], cr, cwc)
        nx_ref[...] = dcx[0:8, :]
        nb_ref[...] = dcb[0:8, :]
        ncc_ref[...] = dcc[0:8, :]
        dssd_ref[:, SSD_WIDTH:2 * SSD_WIDTH] = drx.astype(dssd_ref.dtype)
        dssd_ref[:, 2 * SSD_WIDTH:2 * SSD_WIDTH + 512] = drb.astype(dssd_ref.dtype)
        dssd_ref[:, 2 * SSD_WIDTH + 512:SSD_SEG] = drc.astype(dssd_ref.dtype)
        for k in range(CONV_WIDTH):
            gcw_ref[k:k + 1, :] += jnp.concatenate([gwx[k], gwb[k], gwc[k]], axis=1)
        gcb_ref[...] += jnp.concatenate([gbx, gbb, gbc], axis=1)

    const = lambda shape: pl.BlockSpec(shape, lambda c: (0,) * len(shape))
    body, ex_in, ex_out, ex_shape, ex_sems = _riding(
        exchange, body, 18, 8, lambda: pl.program_id(0) == 0, lambda: pl.program_id(0) == nc - 1)
    res = pl.pallas_call(
        body, name="ssd_bwd", grid=(nc,),
        in_specs=[sp["z"], sp["xr"], sp["br"], sp["cr"], pre_spec(SSD_WIDTH, 0), pre_spec(512, 4), pre_spec(512, 5),
                  sp["dt"], sp["wide"], sp["states"], sp["wide"],
                  sp["cwx"], sp["cwb"], sp["cwc"], sp["vec128"], sp["vec128"], sp["vecw"], sp["vecw"]] + ex_in,
        out_specs=[pl.BlockSpec((L, SSD_SEG), lambda c: (nc - 1 - c, 0)), sp["dt"],
                   const((CONV_WIDTH, SSD_CONV_CH)), const((1, SSD_CONV_CH)), const((1, LANES)), const((1, LANES)),
                   const((1, SSD_WIDTH)), const((1, SSD_WIDTH))] + ex_out,
        out_shape=[jax.ShapeDtypeStruct((t, SSD_SEG), MXU_DTYPE), jax.ShapeDtypeStruct((t, DT_PAD), MXU_DTYPE),
                   jax.ShapeDtypeStruct((CONV_WIDTH, SSD_CONV_CH), F32), jax.ShapeDtypeStruct((1, SSD_CONV_CH), F32),
                   jax.ShapeDtypeStruct((1, LANES), F32), jax.ShapeDtypeStruct((1, LANES), F32),
                   jax.ShapeDtypeStruct((1, SSD_WIDTH), F32), jax.ShapeDtypeStruct((1, SSD_WIDTH), F32)] + ex_shape,
        scratch_shapes=[pltpu.VMEM((SSD_GROUPS, GROUP_COLS, SSD_STATE), F32), pltpu.VMEM((8, SSD_WIDTH), F32),
                        pltpu.VMEM((8, 512), F32), pltpu.VMEM((8, 512), F32), pltpu.VMEM((L, SSD_WIDTH), F32)] + ex_sems,
        compiler_params=_params(("arbitrary",)),
    )(proj_ssd, proj_ssd, proj_ssd, proj_ssd, pre, pre, pre, dt_p, y, states, dyssd,
      conv_w, conv_w, conv_w, dtb, alog, d_row, ng_row, *(exchange["arrays"] if exchange else []))
    return res[:8], res[8:]


def _lru_gates(xl, wa_ref, wx_ref, ba, bx, lam):
    pre_a, pre_x = [], []
    for g in range(LRU_NGROUPS):
        xg = xl[:, g * LRU_GROUP:(g + 1) * LRU_GROUP]
        pre_a.append(_dot(xg, wa_ref[g], NN))
        pre_x.append(_dot(xg, wx_ref[g], NN))
    r = _sigmoid(jnp.concatenate(pre_a, axis=1) + ba)
    i = _sigmoid(jnp.concatenate(pre_x, axis=1) + bx)
    log_a = (-LRU_C * r) * _softplus(-lam)
    a = jnp.exp(log_a)
    mult = jnp.sqrt(-jnp.tanh(log_a) * (a * a + 1.0))
    return r, i, log_a, a, mult


def _scan_rows(p, u, carry, reverse):
    rows, w = p.shape
    groups = rows // 8
    p3, u3 = p.reshape(groups, 8, w), u.reshape(groups, 8, w)
    row = lax.broadcasted_iota(jnp.int32, (groups, 8, w), 1)
    for s in (1, 2, 4):
        ok = row < 8 - s if reverse else row >= s
        shift = 8 - s if reverse else s
        u3 = p3 * jnp.where(ok, pltpu.roll(u3, shift, 1), 0.0) + u3
        p3 = p3 * jnp.where(ok, pltpu.roll(p3, shift, 1), 1.0)
    out = [None] * groups
    for k in (range(groups - 1, -1, -1) if reverse else range(groups)):
        out[k] = p3[k] * carry + u3[k]
        carry = out[k][0:1, :] if reverse else out[k][7:8, :]
    return jnp.concatenate(out, axis=0), carry


def _lru_fwd(proj_lru, conv_w, conv_b, wa, wx, ba, bx, lam):
    t = proj_lru.shape[0]
    rows = min(LRU_ROWS, t)
    nb = t // rows
    W = LRU_WIDTH

    def body(lg_ref, lx_ref, cw_ref, cb_ref, wa_ref, wx_ref, ba_ref, bx_ref, lam_ref, ylru_ref, h_ref,
             halo_ref, carry_ref):
        @pl.when(pl.program_id(0) == 0)
        def _():
            halo_ref[...] = jnp.zeros_like(halo_ref)
            carry_ref[...] = jnp.zeros_like(carry_ref)

        lx = lx_ref[...]
        xl = _causal_conv(lx, halo_ref[...], cw_ref[...], cb_ref[...])
        halo_ref[...] = lx[rows - 8:rows, :]
        _, i, _, a, mult = _lru_gates(xl, wa_ref, wx_ref, ba_ref[...], bx_ref[...], lam_ref[...])
        u = mult * (i * xl)
        h, carry_ref[...] = _scan_rows(a, u, carry_ref[...], False)
        h_ref[...] = h
        lg = lg_ref[...]
        ylru_ref[...] = (h * (lg * _sigmoid(lg))).astype(ylru_ref.dtype)

    const = lambda shape: pl.BlockSpec(shape, lambda b: (0,) * len(shape))
    return pl.pallas_call(
        body, name="lru_fwd", grid=(nb,),
        in_specs=[pl.BlockSpec((rows, W), lambda b: (b, 0)), pl.BlockSpec((rows, W), lambda b: (b, 1)),
                  const((CONV_WIDTH, W)), const((1, W)), const((LRU_NGROUPS, LRU_GROUP, LRU_GROUP)),
                  const((LRU_NGROUPS, LRU_GROUP, LRU_GROUP)), const((1, W)), const((1, W)), const((1, W))],
        out_specs=[pl.BlockSpec((rows, W), lambda b: (b, 0)), pl.BlockSpec((rows, W), lambda b: (b, 0))],
        out_shape=[jax.ShapeDtypeStruct((t, W), MXU_DTYPE), jax.ShapeDtypeStruct((t, W), F32)],
        scratch_shapes=[pltpu.VMEM((8, W), F32), pltpu.VMEM((1, W), F32)],
        compiler_params=_params(("arbitrary",)),
    )(proj_lru, proj_lru, conv_w, conv_b, wa, wx, ba, bx, lam)


def _lru_bwd(proj_lru, h, dylru, conv_w, conv_b, wa, wx, ba, bx, lam, exchange=None):
    t = proj_lru.shape[0]
    rows = min(LRU_ROWS, t)
    nb = t // rows
    W = LRU_WIDTH
    groups8 = rows // 8

    def rev(b):
        return nb - 1 - b

    def halo_spec(col):
        return pl.BlockSpec((8, W), lambda b: (jnp.maximum(rev(b) * groups8 - 1, 0), col))

    def body(lg_ref, lx_ref, hlx_ref, h_ref, hh_ref, dy_ref, cw_ref, cb_ref, wa_ref, wx_ref, ba_ref, bx_ref, lam_ref,
             dlru_ref, gcw_ref, gcb_ref, gba_ref, gbx_ref, glam_ref, gwa_ref, gwx_ref,
             gcarry_ref, afirst_ref, nxt_ref):
        step = pl.program_id(0)

        @pl.when(step == 0)
        def _():
            gcarry_ref[...] = jnp.zeros_like(gcarry_ref)
            afirst_ref[...] = jnp.zeros_like(afirst_ref)
            nxt_ref[...] = jnp.zeros_like(nxt_ref)
            for ref in (gcw_ref, gcb_ref, gba_ref, gbx_ref, glam_ref, gwa_ref, gwx_ref):
                ref[...] = jnp.zeros_like(ref)

        keep = jnp.where(step == nb - 1, 0.0, 1.0)
        lx = lx_ref[...]
        hlx = hlx_ref[...] * keep
        cw = cw_ref[...]
        xl = _causal_conv(lx, hlx, cw, cb_ref[...])
        lam = lam_ref[...]
        r, i, log_a, a, mult = _lru_gates(xl, wa_ref, wx_ref, ba_ref[...], bx_ref[...], lam)
        hv = h_ref[...]
        h_prev = _shift_down(hv, hh_ref[...] * keep, 1)
        lg = lg_ref[...]
        sg = _sigmoid(lg)
        dyv = dy_ref[...]
        d_h = dyv * (lg * sg)
        dlru_ref[:, 0:W] = (dyv * hv * (sg * (1.0 + lg * (1.0 - sg)))).astype(dlru_ref.dtype)

        row = lax.broadcasted_iota(jnp.int32, (rows, W), 0)
        p = jnp.where(row < rows - 1, pltpu.roll(a, rows - 1, 0), afirst_ref[...])
        gsc, gcarry_ref[...] = _scan_rows(p, d_h, gcarry_ref[...], True)
        afirst_ref[...] = a[0:1, :]

        d_a = gsc * h_prev
        v = i * xl
        d_mult = gsc * v
        d_v = gsc * mult
        d_i = d_v * xl
        d_xl = d_v * i
        d_la = d_a * a - d_mult * (a * a) / mult
        sp_neg = _softplus(-lam)
        d_r = d_la * (-LRU_C * sp_neg)
        glam_ref[...] += jnp.sum(d_la * r, axis=0, keepdims=True) * (LRU_C * _sigmoid(-lam))
        d_pa = d_r * r * (1.0 - r)
        d_px = d_i * i * (1.0 - i)
        gba_ref[...] += jnp.sum(d_pa, axis=0, keepdims=True)
        gbx_ref[...] += jnp.sum(d_px, axis=0, keepdims=True)
        parts = []
        for g in range(LRU_NGROUPS):
            cols = slice(g * LRU_GROUP, (g + 1) * LRU_GROUP)
            xg, dpa_g, dpx_g = xl[:, cols], d_pa[:, cols], d_px[:, cols]
            parts.append(_dot(dpa_g, wa_ref[g], NT) + _dot(dpx_g, wx_ref[g], NT))
            gwa_ref[g] += _dot(xg, dpa_g, TN)
            gwx_ref[g] += _dot(xg, dpx_g, TN)
        d_xl = d_xl + jnp.concatenate(parts, axis=1)
        d_lx, gw, gb = _conv_backward(d_xl, nxt_ref[...], lx, cw)
        nxt_ref[...] = d_xl[0:8, :]
        dlru_ref[:, W:2 * W] = d_lx.astype(dlru_ref.dtype)
        for k in range(CONV_WIDTH):
            gcw_ref[k:k + 1, :] += gw[k]
        gcb_ref[...] += gb

    const = lambda shape: pl.BlockSpec(shape, lambda b: (0,) * len(shape))
    wspec = const((LRU_NGROUPS, LRU_GROUP, LRU_GROUP))
    blk = lambda col: pl.BlockSpec((rows, W), lambda b: (rev(b), col))
    body, ex_in, ex_out, ex_shape, ex_sems = _riding(
        exchange, body, 13, 8, lambda: pl.program_id(0) == 0, lambda: pl.program_id(0) == nb - 1)
    res = pl.pallas_call(
        body, name="lru_bwd", grid=(nb,),
        in_specs=[blk(0), blk(1), halo_spec(1), blk(0), halo_spec(0), blk(0),
                  const((CONV_WIDTH, W)), const((1, W)), wspec, wspec, const((1, W)), const((1, W)), const((1, W))] + ex_in,
        out_specs=[pl.BlockSpec((rows, 2 * W), lambda b: (rev(b), 0)), const((CONV_WIDTH, W)), const((1, W)),
                   const((1, W)), const((1, W)), const((1, W)), wspec, wspec] + ex_out,
        out_shape=[jax.ShapeDtypeStruct((t, 2 * W), MXU_DTYPE), jax.ShapeDtypeStruct((CONV_WIDTH, W), F32),
                   jax.ShapeDtypeStruct((1, W), F32), jax.ShapeDtypeStruct((1, W), F32), jax.ShapeDtypeStruct((1, W), F32),
                   jax.ShapeDtypeStruct((1, W), F32), jax.ShapeDtypeStruct((LRU_NGROUPS, LRU_GROUP, LRU_GROUP), F32),
                   jax.ShapeDtypeStruct((LRU_NGROUPS, LRU_GROUP, LRU_GROUP), F32)] + ex_shape,
        scratch_shapes=[pltpu.VMEM((1, W), F32), pltpu.VMEM((1, W), F32), pltpu.VMEM((8, W), F32)] + ex_sems,
        compiler_params=_params(("arbitrary",)),
    )(proj_lru, proj_lru, proj_lru, h, h, dylru, conv_w, conv_b, wa, wx, ba, bx, lam, *(exchange["arrays"] if exchange else []))
    return res[:8], res[8:]


def _mem_scores(q_h, k_h):
    s = _dot(q_h, k_h, NT) * (MEM_HEAD_DIM ** -0.5)
    s = s - jnp.max(s, axis=-1, keepdims=True)
    e = jnp.exp(s)
    return e / jnp.sum(e, axis=-1, keepdims=True)


def _mem_fwd(q, kv, rows=512):
    t = q.shape[0]
    rows = min(rows, t)
    m = kv.shape[0]

    def body(q_ref, kv_ref, y_ref):
        for hd in range(MEM_HEADS):
            cols = slice(hd * MEM_HEAD_DIM, (hd + 1) * MEM_HEAD_DIM)
            vcols = slice(D_MODEL + hd * MEM_HEAD_DIM, D_MODEL + (hd + 1) * MEM_HEAD_DIM)
            p = _mem_scores(q_ref[:, cols], kv_ref[:, cols])
            y_ref[:, cols] = _dot(p, kv_ref[:, vcols], NN).astype(y_ref.dtype)

    return pl.pallas_call(
        body, name="mem_fwd", grid=(t // rows,),
        in_specs=[pl.BlockSpec((rows, D_MODEL), lambda i: (i, 0)), pl.BlockSpec((m, 2 * D_MODEL), lambda i: (0, 0))],
        out_specs=pl.BlockSpec((rows, D_MODEL), lambda i: (i, 0)),
        out_shape=jax.ShapeDtypeStruct((t, D_MODEL), MXU_DTYPE),
        compiler_params=_params(("parallel",)),
    )(q, kv)


def _mem_bwd(q, kv, dy, rows=512):
    t = q.shape[0]
    rows = min(rows, t)
    m = kv.shape[0]

    def body(q_ref, kv_ref, dy_ref, dq_ref, dkv_ref):
        @pl.when(pl.program_id(0) == 0)
        def _():
            dkv_ref[...] = jnp.zeros_like(dkv_ref)

        for hd in range(MEM_HEADS):
            cols = slice(hd * MEM_HEAD_DIM, (hd + 1) * MEM_HEAD_DIM)
            vcols = slice(D_MODEL + hd * MEM_HEAD_DIM, D_MODEL + (hd + 1) * MEM_HEAD_DIM)
            q_h, k_h, dy_h = q_ref[:, cols], kv_ref[:, cols], dy_ref[:, cols]
            p = _mem_scores(q_h, k_h)
            dp = _dot(dy_h, kv_ref[:, vcols], NT)
            dkv_ref[:, vcols] += _dot(p, dy_h, TN)
            ds = p * (dp - jnp.sum(dp * p, axis=-1, keepdims=True)) * (MEM_HEAD_DIM ** -0.5)
            dq_ref[:, cols] = _dot(ds, k_h, NN).astype(dq_ref.dtype)
            dkv_ref[:, cols] += _dot(ds, q_h, TN)

    return pl.pallas_call(
        body, name="mem_bwd", grid=(t // rows,),
        in_specs=[pl.BlockSpec((rows, D_MODEL), lambda i: (i, 0)), pl.BlockSpec((m, 2 * D_MODEL), lambda i: (0, 0)),
                  pl.BlockSpec((rows, D_MODEL), lambda i: (i, 0))],
        out_specs=[pl.BlockSpec((rows, D_MODEL), lambda i: (i, 0)), pl.BlockSpec((m, 2 * D_MODEL), lambda i: (0, 0))],
        out_shape=[jax.ShapeDtypeStruct((t, D_MODEL), MXU_DTYPE), jax.ShapeDtypeStruct((m, 2 * D_MODEL), F32)],
        compiler_params=_params(("arbitrary",)),
    )(q, kv, dy)


def _merge_fwd(x, yssd, ylru, ymem, gl, w_bs, w_bl, w_bm, w_out, fg, tgt, rows=256):
    t = x.shape[0]
    rows = min(rows, t)
    D = D_MODEL

    def body(x_ref, ys_ref, yl_ref, ym_ref, gl_ref, wbs_ref, wbl_ref, wbm_ref, wo_ref, fg_ref, tgt_ref,
             ps_ref, pl_ref, pm_ref, mg_ref, dx2_ref, loss_ref, gfg_ref):
        @pl.when(pl.program_id(0) == 0)
        def _():
            loss_ref[...] = jnp.zeros_like(loss_ref)
            gfg_ref[...] = jnp.zeros_like(gfg_ref)

        ps = _dot(ys_ref[...], wbs_ref[...], NN)
        pl_ = _dot(yl_ref[...], wbl_ref[...], NN)
        pm = _dot(ym_ref[...], wbm_ref[...], NN)
        ps_ref[...] = ps
        pl_ref[...] = pl_
        pm_ref[...] = pm
        merged = (_sigmoid(gl_ref[:, 0:D]) * ps + _sigmoid(gl_ref[:, D:2 * D]) * pl_) + _sigmoid(gl_ref[:, 2 * D:3 * D]) * pm
        mg_ref[...] = merged.astype(mg_ref.dtype)
        x2 = x_ref[...] + _dot(merged, wo_ref[...], NN)
        r2 = lax.rsqrt(jnp.mean(x2 * x2, axis=-1, keepdims=True) + EPS)
        xn = x2 * r2
        fg = fg_ref[...]
        diff = xn * fg - tgt_ref[...]
        tile_loss = 0.5 * jnp.sum(jnp.mean(diff * diff, axis=-1, keepdims=True), axis=0, keepdims=True)
        loss_ref[...] += jnp.broadcast_to(tile_loss, loss_ref.shape)
        d_out = diff * (1.0 / D)
        gfg_ref[...] += jnp.sum(d_out * xn, axis=0, keepdims=True)
        dxn = d_out * fg
        dx2_ref[...] = r2 * (dxn - xn * jnp.mean(dxn * xn, axis=-1, keepdims=True))

    row = lambda w: pl.BlockSpec((rows, w), lambda i: (i, 0))
    const = lambda shape: pl.BlockSpec(shape, lambda i: (0,) * len(shape))
    return pl.pallas_call(
        body, name="merge_fwd", grid=(t // rows,),
        in_specs=[row(D), row(SSD_WIDTH), row(LRU_WIDTH), row(D), row(3 * D), const((SSD_WIDTH, D)), const((LRU_WIDTH, D)),
                  const((D, D)), const((D, D)), const((1, D)), row(D)],
        out_specs=[row(D), row(D), row(D), row(D), row(D), const((1, LANES)), const((1, D))],
        out_shape=[jax.ShapeDtypeStruct((t, D), F32), jax.ShapeDtypeStruct((t, D), F32), jax.ShapeDtypeStruct((t, D), F32),
                   jax.ShapeDtypeStruct((t, D), MXU_DTYPE), jax.ShapeDtypeStruct((t, D), F32),
                   jax.ShapeDtypeStruct((1, LANES), F32), jax.ShapeDtypeStruct((1, D), F32)],
        compiler_params=_params(("arbitrary",)),
    )(x, yssd, ylru, ymem, gl, w_bs, w_bl, w_bm, w_out, fg, tgt)


def _merge_bwd(dx2, gl, ps, pl_in, pm, w_bs, w_bl, w_bm, w_out, rows=256):
    t = dx2.shape[0]
    rows = min(rows, t)
    D = D_MODEL

    def body(dx2_ref, gl_ref, ps_ref, pl_ref, pm_ref, wbs_ref, wbl_ref, wbm_ref, wo_ref,
             dg_ref, dps_ref, dpl_ref, dpm_ref, dys_ref, dyl_ref, dym_ref):
        dm = _dot(dx2_ref[...], wo_ref[...], NT)
        for idx, (p_ref, dp_ref, w_ref, dy_ref) in enumerate(
                ((ps_ref, dps_ref, wbs_ref, dys_ref), (pl_ref, dpl_ref, wbl_ref, dyl_ref), (pm_ref, dpm_ref, wbm_ref, dym_ref))):
            gate = _sigmoid(gl_ref[:, idx * D:(idx + 1) * D])
            dg_ref[:, idx * D:(idx + 1) * D] = ((dm * p_ref[...]) * gate * (1.0 - gate)).astype(dg_ref.dtype)
            dp = dm * gate
            dp_ref[...] = dp.astype(dp_ref.dtype)
            dy_ref[...] = _dot(dp, w_ref[...], NT)

    row = lambda w: pl.BlockSpec((rows, w), lambda i: (i, 0))
    const = lambda shape: pl.BlockSpec(shape, lambda i: (0,) * len(shape))
    return pl.pallas_call(
        body, name="merge_bwd", grid=(t // rows,),
        in_specs=[row(D), row(3 * D), row(D), row(D), row(D), const((SSD_WIDTH, D)), const((LRU_WIDTH, D)),
                  const((D, D)), const((D, D))],
        out_specs=[row(3 * D), row(D), row(D), row(D), row(SSD_WIDTH), row(LRU_WIDTH), row(D)],
        out_shape=[jax.ShapeDtypeStruct((t, 3 * D), MXU_DTYPE), jax.ShapeDtypeStruct((t, D), MXU_DTYPE),
                   jax.ShapeDtypeStruct((t, D), MXU_DTYPE), jax.ShapeDtypeStruct((t, D), MXU_DTYPE),
                   jax.ShapeDtypeStruct((t, SSD_WIDTH), F32), jax.ShapeDtypeStruct((t, LRU_WIDTH), F32),
                   jax.ShapeDtypeStruct((t, D), F32)],
        compiler_params=_params(("parallel",)),
    )(dx2, gl, ps, pl_in, pm, w_bs, w_bl, w_bm, w_out)


def _mesh_place():
    x, y, c = lax.axis_index("x"), lax.axis_index("y"), lax.axis_index("c")
    return x, y, c, 4 * x + 2 * y + c


def _other_chips(x, y):
    return [(1 - x, y), (x, 1 - y), (1 - x, 1 - y)]


def _all_gather_plan(arrs):
    n = len(arrs)

    def parts(ins, outs, send_sems, recv_sems, local_sems):
        x, y, c, me = _mesh_place()
        sibling = (x, y, 1 - c)
        chips = _other_chips(x, y)

        def slot(px, py, pc):
            return 4 * px + 2 * py + pc

        def copy(a, k, block, to, src=None):
            return pltpu.make_async_remote_copy(
                src_ref=outs[a].at[block] if src is None else src, dst_ref=outs[a].at[block],
                send_sem=send_sems.at[a, k], recv_sem=recv_sems.at[a, k], device_id=to, device_id_type=pl.DeviceIdType.MESH)

        def local():
            return [pltpu.make_async_copy(ins[a], outs[a].at[me], local_sems.at[a]) for a in range(n)]

        def first():
            return [copy(a, k, me, to, src=ins[a]) for a in range(n)
                    for k, to in enumerate([sibling] + [(*chip, c) for chip in chips])]

        return x, y, c, sibling, chips, slot, copy, local, first

    def start(ins, outs, *sems):
        *_, local, first = parts(ins, outs, *sems)
        for cp in local() + first():
            cp.start()

    def wait(ins, outs, *sems):
        x, y, c, sibling, chips, slot, copy, local, first = parts(ins, outs, *sems)
        sends = first()
        for j, chip in enumerate(chips):
            for a in range(n):
                copy(a, 1 + j, slot(*chip, c), sibling).wait_recv()
                passed = copy(a, 4 + j, slot(*chip, c), sibling)
                passed.start()
                sends.append(passed)
        for a in range(n):
            copy(a, 0, slot(x, y, 1 - c), sibling).wait_recv()
        for j, chip in enumerate(chips):
            for a in range(n):
                copy(a, 4 + j, slot(*chip, 1 - c), sibling).wait_recv()
        for cp in sends:
            cp.wait_send()
        for cp in local():
            cp.wait()

    return dict(arrays=list(arrs), out_shape=[jax.ShapeDtypeStruct((N_DEV,) + a.shape, a.dtype) for a in arrs],
                sems=[(n, 7), (n, 7), (n,)], start=start, wait=wait)


def _all_gather(arrs, name):
    return _run_exchange(_all_gather_plan(arrs), name)


N_CHIPS = 4


def _pair_plan(parts):
    n = len(parts)

    def copies(ins, outs, send_sems, recv_sems):
        x, y, c, _ = _mesh_place()
        return [pltpu.make_async_remote_copy(src_ref=ins[a].at[q, 1 - c], dst_ref=outs[a].at[q], send_sem=send_sems.at[a, q],
                                             recv_sem=recv_sems.at[a, q], device_id=(x, y, 1 - c), device_id_type=pl.DeviceIdType.MESH)
                for a in range(n) for q in range(N_CHIPS)]

    def start(ins, outs, send_sems, recv_sems):
        for cp in copies(ins, outs, send_sems, recv_sems):
            cp.start()

    def wait(ins, outs, send_sems, recv_sems):
        cps = copies(ins, outs, send_sems, recv_sems)
        for cp in cps:
            cp.wait_recv()
        for cp in cps:
            cp.wait_send()

    return dict(arrays=list(parts), out_shape=[jax.ShapeDtypeStruct((N_CHIPS,) + a.shape[2:], a.dtype) for a in parts],
                sems=[(n, N_CHIPS), (n, N_CHIPS)], start=start, wait=wait)


def _chip_plan(sums):
    n = len(sums)

    def copies(ins, outs, send_sems, recv_sems, arriving):
        x, y, c, _ = _mesh_place()
        my_chip = 2 * x + y
        cps = []
        for a in range(n):
            for j, (px, py) in enumerate(_other_chips(x, y)):
                src, dst = (my_chip, 2 * px + py) if arriving else (2 * px + py, my_chip)
                cps.append(pltpu.make_async_remote_copy(
                    src_ref=ins[a].at[src], dst_ref=outs[a].at[dst], send_sem=send_sems.at[a, j], recv_sem=recv_sems.at[a, j],
                    device_id=(px, py, c), device_id_type=pl.DeviceIdType.MESH))
        return cps

    def start(ins, outs, send_sems, recv_sems):
        for cp in copies(ins, outs, send_sems, recv_sems, False):
            cp.start()

    def wait(ins, outs, send_sems, recv_sems):
        for cp in copies(ins, outs, send_sems, recv_sems, True):
            cp.wait_recv()
        for cp in copies(ins, outs, send_sems, recv_sems, False):
            cp.wait_send()

    return dict(arrays=list(sums), out_shape=[jax.ShapeDtypeStruct(a.shape, a.dtype) for a in sums],
                sems=[(n, 3), (n, 3)], start=start, wait=wait)


def _run_exchange(plan, name):
    n = len(plan["arrays"])

    def body(*refs):
        ins, outs, sems = refs[:n], refs[n:2 * n], refs[2 * n:]
        plan["start"](ins, outs, *sems)
        plan["wait"](ins, outs, *sems)

    any_spec = pl.BlockSpec(memory_space=pl.ANY)
    return pl.pallas_call(
        body, name=name, in_specs=[any_spec] * n, out_specs=[any_spec] * n, out_shape=plan["out_shape"],
        scratch_shapes=[pltpu.SemaphoreType.DMA(shape) for shape in plan["sems"]],
    )(*plan["arrays"])


def _riding(plan, body, n_in, n_out, first, last):
    if plan is None:
        return body, [], [], [], []
    ne = len(plan["arrays"])

    def wrapped(*refs):
        ins, ex_in = refs[:n_in], refs[n_in:n_in + ne]
        outs = refs[n_in + ne:n_in + ne + n_out]
        ex_out = refs[n_in + ne + n_out:n_in + 2 * ne + n_out]
        n_sems = len(plan["sems"])
        scratch, sems = refs[n_in + 2 * ne + n_out:-n_sems], refs[-n_sems:]

        @pl.when(first())
        def _():
            plan["start"](ex_in, ex_out, *sems)

        body(*ins, *outs, *scratch)

        @pl.when(last())
        def _():
            plan["wait"](ex_in, ex_out, *sems)

    any_spec = pl.BlockSpec(memory_space=pl.ANY)
    sems = [pltpu.SemaphoreType.DMA(shape) for shape in plan["sems"]]
    return wrapped, [any_spec] * ne, [any_spec] * ne, plan["out_shape"], sems


def _col_tile(r, c, limit_bytes):
    assert c % LANES == 0, c
    best = LANES
    for cand in range(LANES, c + 1, LANES):
        if c % cand == 0 and r * cand * 4 <= limit_bytes:
            best = cand
    return best


def _chip_sum(part, recv, core, name):
    _, _, r, c = part.shape
    ct = _col_tile(r, c, 2 << 20)

    def body(core_ref, p_ref, r_ref, s_ref, t_ref):
        s = p_ref[...] + r_ref[...]
        s_ref[...] = s
        t_ref[...] = s.astype(t_ref.dtype)

    blk = pl.BlockSpec((None, r, ct), lambda q, i, core_ref: (q, 0, i))
    return pl.pallas_call(
        body, name=name,
        grid_spec=pltpu.PrefetchScalarGridSpec(
            num_scalar_prefetch=1, grid=(N_CHIPS, c // ct),
            in_specs=[pl.BlockSpec((None, None, r, ct), lambda q, i, core_ref: (q, core_ref[0], 0, i)), blk],
            out_specs=[blk, blk]),
        out_shape=[jax.ShapeDtypeStruct((N_CHIPS, r, c), F32), jax.ShapeDtypeStruct((N_CHIPS, r, c), GRAD_WIRE_DTYPE)],
        compiler_params=_params(("parallel", "parallel")),
    )(core, part, recv)


def _row_tile(r, limit):
    if r <= limit:
        return r
    best = 8
    for cand in range(8, limit + 1, 8):
        if r % cand == 0:
            best = cand
    assert r % best == 0, r
    return best


def _adam_update(w, g, m, v):
    nm = ADAM_B1 * m + (1.0 - ADAM_B1) * g
    nv = ADAM_B2 * v + (1.0 - ADAM_B2) * (g * g)
    m_hat = nm / (1.0 - ADAM_B1 ** ADAM_STEP)
    v_hat = nv / (1.0 - ADAM_B2 ** ADAM_STEP)
    return -ADAM_LR * (m_hat / (jnp.sqrt(v_hat) + ADAM_EPS) + ADAM_WD * w), nm, nv


def _sum_adamw(own, recv, chip, w, m, v, name):
    _, r, c = own.shape
    ct = _col_tile(r, c, 1 << 20)

    def body(chip_ref, o_ref, r1_ref, r2_ref, r3_ref, w_ref, m_ref, v_ref, g_ref, d_ref, nm_ref, nv_ref):
        g = ((o_ref[...] + r1_ref[...].astype(F32)) + r2_ref[...].astype(F32)) + r3_ref[...].astype(F32)
        g_ref[...] = g
        d_ref[...], nm_ref[...], nv_ref[...] = _adam_update(w_ref[...], g, m_ref[...], v_ref[...])

    def slot(k):
        return pl.BlockSpec((None, r, ct), lambda i, chip_ref: ((chip_ref[0] + k) % N_CHIPS, 0, i))

    spec = pl.BlockSpec((r, ct), lambda i, chip_ref: (0, i))
    shape = jax.ShapeDtypeStruct((r, c), F32)
    return pl.pallas_call(
        body, name=name,
        grid_spec=pltpu.PrefetchScalarGridSpec(
            num_scalar_prefetch=1, grid=(c // ct,),
            in_specs=[slot(0), slot(1), slot(2), slot(3), spec, spec, spec], out_specs=[spec] * 4),
        out_shape=[shape] * 4,
        compiler_params=_params(("parallel",)),
    )(chip, own, recv, recv, recv, w, m, v)


def _small_adamw(parts, ws, ms, vs):
    n = len(parts)

    def body(*refs):
        p_refs, w_refs, m_refs, v_refs = refs[:n], refs[n:2 * n], refs[2 * n:3 * n], refs[3 * n:4 * n]
        outs = refs[4 * n:]
        for i in range(n):
            g = p_refs[i][0]
            for k in range(1, N_DEV):
                g = g + p_refs[i][k]
            outs[i][...] = g
            outs[n + i][...], outs[2 * n + i][...], outs[3 * n + i][...] = _adam_update(
                w_refs[i][...], g, m_refs[i][...], v_refs[i][...])

    vmem = pl.BlockSpec(memory_space=pltpu.VMEM)
    shapes = [jax.ShapeDtypeStruct(w.shape, F32) for w in ws]
    res = pl.pallas_call(
        body, name="adamw_small", in_specs=[vmem] * (4 * n), out_specs=[vmem] * (4 * n), out_shape=shapes * 4,
        compiler_params=pltpu.CompilerParams(vmem_limit_bytes=VMEM_LIMIT),
    )(*parts, *ws, *ms, *vs)
    return res[:n], res[n:2 * n], res[2 * n:3 * n], res[3 * n:]


def _pack(arrs, dtype, row_multiple):
    flat = jnp.concatenate([a.reshape(-1).astype(dtype) for a in arrs])
    unit = LANES * row_multiple
    padded = -(-flat.shape[0] // unit) * unit
    return jnp.pad(flat, (0, padded - flat.shape[0])).reshape(-1, LANES)


def _unpack(packed, shapes, lead=()):
    flat = packed.reshape(lead + (-1,))
    out, off = [], 0
    for shp in shapes:
        n = math.prod(shp)
        out.append(flat[..., off:off + n].reshape(lead + tuple(shp)))
        off += n
    return out


def _gather_cols(g, lo, hi):
    width = g.shape[2]
    pieces = []
    for s in range(N_DEV):
        a, e = max(lo, s * width), min(hi, (s + 1) * width)
        if a < e:
            pieces.append(g[s, :, a - s * width:e - s * width])
    return pieces[0] if len(pieces) == 1 else jnp.concatenate(pieces, axis=1)


def _scatter_cols(segs, width):
    slots = []
    for k in range(N_DEV):
        lo, hi = k * width, (k + 1) * width
        pieces = []
        for arr, s_lo, s_hi in segs:
            a, e = max(lo, s_lo), min(hi, s_hi)
            if a < e:
                pieces.append(arr[:, a - s_lo:e - s_lo])
        slots.append(pieces[0] if len(pieces) == 1 else jnp.concatenate(pieces, axis=1))
    return jnp.stack(slots)


def _block_diag_groups(w):
    w4 = w.reshape(LRU_NGROUPS, 4, LRU_BLOCK, LRU_BLOCK)
    eye = jnp.eye(4, dtype=w.dtype)
    return jnp.einsum("gaij,ab->gaibj", w4, eye).reshape(LRU_NGROUPS, LRU_GROUP, LRU_GROUP)


def _block_diag_extract(wg):
    w5 = wg.reshape(LRU_NGROUPS, 4, LRU_BLOCK, 4, LRU_BLOCK)
    idx = jnp.arange(4)
    return w5[:, idx, :, idx, :].transpose(1, 0, 2, 3).reshape(LRU_BLOCKS, LRU_BLOCK, LRU_BLOCK)


BIG = ("w_in", "w_kv", "w_br_ssd", "w_br_lru", "w_br_mem", "w_out")
SMALL_SHARDED = ("ssd_conv_w", "ssd_norm_g", "lru_conv_w")
REPLICATED = ("norm_g", "ssd_conv_b", "ssd_dt_bias", "ssd_a_log", "ssd_d", "lru_conv_b", "lru_w_a", "lru_b_a",
              "lru_w_x", "lru_b_x", "lru_lambda", "mem_norm_g", "final_g")
WEIGHTS = ("norm_g", "w_in", "ssd_conv_w", "ssd_conv_b", "ssd_dt_bias", "ssd_a_log", "ssd_d", "ssd_norm_g", "lru_conv_w",
           "lru_conv_b", "lru_w_a", "lru_b_a", "lru_w_x", "lru_b_x", "lru_lambda", "mem_norm_g", "w_kv", "w_br_ssd",
           "w_br_lru", "w_br_mem", "w_out", "final_g")


def kernel(x, mem, norm_g, w_in, ssd_conv_w, ssd_conv_b, ssd_dt_bias, ssd_a_log, ssd_d, ssd_norm_g, lru_conv_w, lru_conv_b, lru_w_a, lru_b_a, lru_w_x, lru_b_x, lru_lambda, mem_norm_g, w_kv, w_br_ssd, w_br_lru, w_br_mem, w_out, final_g, loss_target, m_norm_g, m_w_in, m_ssd_conv_w, m_ssd_conv_b, m_ssd_dt_bias, m_ssd_a_log, m_ssd_d, m_ssd_norm_g, m_lru_conv_w, m_lru_conv_b, m_lru_w_a, m_lru_b_a, m_lru_w_x, m_lru_b_x, m_lru_lambda, m_mem_norm_g, m_w_kv, m_w_br_ssd, m_w_br_lru, m_w_br_mem, m_w_out, m_final_g, v_norm_g, v_w_in, v_ssd_conv_w, v_ssd_conv_b, v_ssd_dt_bias, v_ssd_a_log, v_ssd_d, v_ssd_norm_g, v_lru_conv_w, v_lru_conv_b, v_lru_w_a, v_lru_b_a, v_lru_w_x, v_lru_b_x, v_lru_lambda, v_mem_norm_g, v_w_kv, v_w_br_ssd, v_w_br_lru, v_w_br_mem, v_w_out, v_final_g):
    env = dict(locals())
    W = {n: env[n] for n in WEIGHTS}
    M = {n: env["m_" + n] for n in WEIGHTS}
    V = {n: env["v_" + n] for n in WEIGHTS}
    me = 4 * lax.axis_index("x") + 2 * lax.axis_index("y") + lax.axis_index("c")
    t = x.shape[1]
    xt = x[0]
    memt = mem[0]
    tgt = loss_target[0]

    small_shapes = [W[n].shape for n in SMALL_SHARDED]
    as2d = lambda d, n: jnp.transpose(d[n][0]) if n == "w_in" else d[n][0]
    (g_in,) = _all_gather([as2d(W, "w_in").astype(MXU_DTYPE)], "w_in_all_gather")
    b = SEG_BOUNDS
    w_in_t = g_in.reshape(IN_WIDTH, D_MODEL)
    w_ssd, w_lru, w_q, w_g = w_in_t[b[0]:b[1]], w_in_t[b[2]:b[3]], w_in_t[b[3]:b[4]], w_in_t[b[4]:b[5]]
    w_dt = jnp.pad(w_in_t[b[1]:b[2]], ((0, DT_PAD - SSD_HEADS), (0, 0)))

    h = _rms_fwd(xt, norm_g, "norm_fwd")
    later = _all_gather_plan([as2d(W, n).astype(MXU_DTYPE) for n in BIG[1:]] + [_pack([W[n] for n in SMALL_SHARDED], F32, 8)])
    proj_ssd, (g_kv, g_bs, g_bl, g_bm, g_out, gs) = _matmul(h, w_ssd, "nt", "proj_ssd", tm=4096, tn=512, exchange=later)
    g_cw, g_ng, g_lcw = _unpack(gs, small_shapes, (N_DEV,))
    cols = lambda a: jnp.moveaxis(a[:, 0], 0, -2).reshape(a.shape[2:-1] + (-1,))
    rows_ = lambda a: a.reshape((-1,) + a.shape[2:])
    w_bs_f, w_bl_f, w_bm_f, w_out_f = rows_(g_bs), rows_(g_bl), rows_(g_bm), rows_(g_out)
    conv_w_f, ssd_ng_f, lru_cw_f = cols(g_cw), cols(g_ng), cols(g_lcw)
    w_kv_f = _gather_cols(g_kv, 0, 2 * D_MODEL)

    pad_heads = lambda a: jnp.pad(a, ((0, 0), (0, LANES - SSD_HEADS)))
    dtb, alog = pad_heads(ssd_dt_bias), pad_heads(ssd_a_log)
    d_row = jnp.repeat(ssd_d, SSD_HEAD_DIM, axis=1)
    ng_row = ssd_ng_f.reshape(1, SSD_WIDTH)
    wa_g, wx_g = _block_diag_groups(lru_w_a[0]), _block_diag_groups(lru_w_x[0])
    ba, bx = lru_b_a.reshape(1, LRU_WIDTH), lru_b_x.reshape(1, LRU_WIDTH)
    fg = final_g.reshape(1, D_MODEL)

    proj_lru = _matmul(h, w_lru, "nt", "proj_lru", tm=4096, tn=512)
    proj_q = _matmul(h, w_q, "nt", "proj_q", tm=4096, tn=512)
    proj_g = _matmul(h, w_g, "nt", "proj_g", tm=4096, tn=512)
    proj_dt = _matmul(h, w_dt, "nt", "proj_dt", tm=4096)
    mem_n = _rms_fwd(memt, mem_norm_g, "mem_norm_fwd")
    kv = _matmul(mem_n, w_kv_f, "nn", "mem_kv")
    yssd, y_scan, states, ssd_pre = _ssd_fwd(proj_ssd, proj_dt, conv_w_f, ssd_conv_b, dtb, alog, d_row, ng_row)
    ylru, h_lru = _lru_fwd(proj_lru, lru_cw_f, lru_conv_b, wa_g, wx_g, ba, bx, lru_lambda)
    ymem = _mem_fwd(proj_q, kv)
    ps, pl_, pm, merged, dx2, loss_vec, g_fg = _merge_fwd(xt, yssd, ylru, ymem, proj_g, w_bs_f, w_bl_f, w_bm_f, w_out_f, fg, tgt)

    d_g, dps, dpl, dpm, dyssd, dylru, dymem = _merge_bwd(dx2, proj_g, ps, pl_, pm, w_bs_f, w_bl_f, w_bm_f, w_out_f)
    gw_out = _matmul(merged, dx2, "tn", "grad_w_out", tk=2048)
    gw_bs = _matmul(yssd, dps, "tn", "grad_w_br_ssd", tm=2048)
    gw_bl = _matmul(ylru, dpl, "tn", "grad_w_br_lru", tm=LRU_WIDTH, tk=2048)
    gw_bm = _matmul(ymem, dpm, "tn", "grad_w_br_mem", tk=2048)
    d_q, d_kv = _mem_bwd(proj_q, kv, dymem)
    gw_kv = _matmul(mem_n, d_kv, "tn", "grad_w_kv")
    d_memn = _matmul(d_kv, w_kv_f, "nt", "d_mem_n")
    _, g_memng = _rms_bwd(memt, d_memn, None, mem_norm_g, "mem_norm_bwd")

    core = lax.axis_index("c").astype(jnp.int32).reshape(1)
    chip = (2 * lax.axis_index("x") + lax.axis_index("y")).astype(jnp.int32).reshape(1)
    by_chip = lambda a: a.reshape((N_CHIPS, 2, -1) + a.shape[1:])
    early = ("w_kv", "w_br_ssd", "w_br_lru", "w_br_mem", "w_out")
    early_parts = [by_chip(_scatter_cols([(gw_kv, 0, 2 * D_MODEL)], 2 * D_MODEL // N_DEV).reshape(-1, 2 * D_MODEL // N_DEV)),
                   by_chip(gw_bs), by_chip(gw_bl), by_chip(gw_bm), by_chip(gw_out)]
    (d_lru, gl_cw, gl_cb, g_ba, g_bx, g_lam, gwa_g, gwx_g), early_sib = _lru_bwd(
        proj_lru, h_lru, dylru, lru_cw_f, lru_conv_b, wa_g, wx_g, ba, bx, lru_lambda, exchange=_pair_plan(early_parts))
    early_sums = [_chip_sum(p, r, core, "chip_sum_" + n) for n, p, r in zip(early, early_parts, early_sib)]
    (d_ssd, d_dt, gs_cw, gs_cb, g_dtb, g_alog, g_dch, g_ngrow), early_recv = _ssd_bwd(
        proj_ssd, ssd_pre, proj_dt, y_scan, states, dyssd, conv_w_f, dtb, alog, d_row, ng_row,
        exchange=_chip_plan([s16 for _, s16 in early_sums]))
    gw_ssd = _matmul(d_ssd, h, "tn", "grad_w_in_ssd", tm=2560)
    gw_lru = _matmul(d_lru, h, "tn", "grad_w_in_lru", tm=1536, tk=2048)
    gw_q = _matmul(d_q, h, "tn", "grad_w_in_q", tk=2048)
    gw_g = _matmul(d_g, h, "tn", "grad_w_in_g", tm=1536, tk=2048)
    gw_dt = _matmul(d_dt, h, "tn", "grad_w_in_dt", tk=2048)
    in_part = by_chip(jnp.concatenate([gw_ssd, gw_dt[:SSD_HEADS], gw_lru, gw_q, gw_g], axis=0))
    (in_sib,) = _run_exchange(_pair_plan([in_part]), "grad_pair_exchange")
    in_sum = _chip_sum(in_part, in_sib, core, "chip_sum_w_in")
    grad_x, g_normg, (in_recv,) = _dh_norm_bwd([(d_ssd, w_ssd), (d_lru, w_lru), (d_q, w_q), (d_g, w_g), (d_dt, w_dt)], xt, dx2,
                                               norm_g, exchange=_chip_plan([in_sum[1]]))
    reduced = {"w_in": (in_sum[0], in_recv), **{n: (s[0], r) for n, s, r in zip(early, early_sums, early_recv)}}

    small_grads = {
        "norm_g": g_normg, "ssd_conv_w": gs_cw, "ssd_conv_b": gs_cb, "ssd_dt_bias": g_dtb[:, :SSD_HEADS],
        "ssd_a_log": g_alog[:, :SSD_HEADS], "ssd_d": jnp.sum(g_dch.reshape(SSD_HEADS, SSD_HEAD_DIM), axis=1).reshape(1, SSD_HEADS),
        "ssd_norm_g": g_ngrow.reshape(SSD_GROUPS, -1), "lru_conv_w": gl_cw, "lru_conv_b": gl_cb,
        "lru_w_a": _block_diag_extract(gwa_g), "lru_b_a": g_ba, "lru_w_x": _block_diag_extract(gwx_g), "lru_b_x": g_bx,
        "lru_lambda": g_lam, "mem_norm_g": g_memng, "final_g": g_fg,
    }
    small_all = REPLICATED + SMALL_SHARDED

    def small_shape(n, shards):
        shp = W[n].shape[1:] if W[n].ndim > 2 else (1, W[n].shape[-1])
        return shp[:-1] + (shp[-1] * shards,)

    small_recv = _all_gather([small_grads[n].reshape(small_shape(n, N_DEV if n in SMALL_SHARDED else 1)) for n in small_all],
                             "small_grads_all_gather")

    grads, delta, new_m, new_v = {}, {}, {}, {}
    for n in BIG:
        s32, recv = reduced[n]
        res = _sum_adamw(s32, recv, chip, as2d(W, n), as2d(M, n), as2d(V, n), "adamw_" + n)
        for dst, a in zip((grads, delta, new_m, new_v), res):
            dst[n] = (jnp.transpose(a) if n == "w_in" else a)[None]

    parts = []
    for n, a in zip(small_all, small_recv):
        if n in SMALL_SHARDED:
            width = W[n].shape[-1]
            a = lax.dynamic_slice_in_dim(a, me * width, width, axis=a.ndim - 1)
        parts.append(a)
    canon = lambda d: [d[n].reshape(small_shape(n, 1)) for n in small_all]
    for dst, res in zip((grads, delta, new_m, new_v), _small_adamw(parts, canon(W), canon(M), canon(V))):
        for n, a in zip(small_all, res):
            dst[n] = a.reshape(W[n].shape)

    loss = lax.psum(loss_vec[0, 0], ("x", "y", "c"))
    return (loss, grad_x[None], *[grads[n] for n in WEIGHTS], *[delta[n] for n in WEIGHTS],
            *[new_m[n] for n in WEIGHTS], *[new_v[n] for n in WEIGHTS])
```

```python
import functools
import math

import jax
import jax.numpy as jnp
from jax import lax
from jax.experimental import pallas as pl
from jax.experimental.pallas import tpu as pltpu

F32 = jnp.float32
MXU_DTYPE = jnp.bfloat16
GRAD_WIRE_DTYPE = jnp.bfloat16

D_MODEL = 1024
EPS = 1e-6
CONV_WIDTH = 4
SSD_WIDTH = 2048
SSD_HEAD_DIM = 64
SSD_HEADS = 32
SSD_GROUPS = 4
SSD_STATE = 128
SSD_CHUNK = 128
SSD_CONV_CH = SSD_WIDTH + 2 * SSD_GROUPS * SSD_STATE
SSD_PAIRS = SSD_HEADS // 2
PAIRS_PER_GROUP = SSD_PAIRS // SSD_GROUPS
GROUP_COLS = SSD_WIDTH // SSD_GROUPS
LRU_WIDTH = 1536
LRU_BLOCKS = 16
LRU_BLOCK = 96
LRU_GROUP = 4 * LRU_BLOCK
LRU_NGROUPS = LRU_WIDTH // LRU_GROUP
LRU_C = 8.0
LRU_ROWS = 256
MEM_HEADS = 4
MEM_HEAD_DIM = 256
IN_WIDTH = 12320
N_DEV = 8
LANES = 128
SSD_SEG = SSD_WIDTH + SSD_CONV_CH
DT_PAD = LANES
SEG_BOUNDS = (0, 5120, 5152, 8224, 9248, 12320)

ADAM_LR = 0.001
ADAM_B1 = 0.9
ADAM_B2 = 0.999
ADAM_EPS = 1e-08
ADAM_WD = 0.01
ADAM_STEP = 10

VMEM_LIMIT = 56 * 1024 * 1024

NN = (((1,), (0,)), ((), ()))
NT = (((1,), (1,)), ((), ()))
TN = (((0,), (0,)), ((), ()))


def _dot(a, b, dims):
    return lax.dot_general(a.astype(MXU_DTYPE), b.astype(MXU_DTYPE), dims, preferred_element_type=F32)


def _sigmoid(x):
    return 0.5 * jnp.tanh(0.5 * x) + 0.5


def _log1p(e):
    u = 1.0 + e
    return jnp.where(u == 1.0, e, jnp.log(u) * (e / jnp.where(u == 1.0, 1.0, u - 1.0)))


def _softplus(x):
    return jnp.maximum(x, 0.0) + _log1p(jnp.exp(-jnp.abs(x)))


def _params(semantics):
    return pltpu.CompilerParams(dimension_semantics=semantics, vmem_limit_bytes=VMEM_LIMIT)


def _shift_down(cur, halo8, k):
    rolled = pltpu.roll(cur, k, 0)
    row8 = lax.broadcasted_iota(jnp.int32, halo8.shape, 0)
    top = jnp.where(row8 >= k, rolled[0:8], pltpu.roll(halo8, k, 0))
    return jnp.concatenate([top, rolled[8:]], axis=0)


def _shift_up(cur, next8, k):
    rows = cur.shape[0]
    rolled = pltpu.roll(cur, rows - k, 0)
    row8 = lax.broadcasted_iota(jnp.int32, next8.shape, 0)
    bot = jnp.where(row8 < 8 - k, rolled[rows - 8:rows], pltpu.roll(next8, 8 - k, 0))
    return jnp.concatenate([rolled[:rows - 8], bot], axis=0)


def _causal_conv(raw, halo8, w, b):
    acc = raw * w[3:4, :] + b
    for k in range(1, CONV_WIDTH):
        acc = acc + _shift_down(raw, halo8, k) * w[3 - k:4 - k, :]
    return acc


def _conv_backward(dco, next8, raw, w):
    d_raw = dco * w[3:4, :]
    gw = [None] * CONV_WIDTH
    gw[3] = jnp.sum(dco * raw, axis=0, keepdims=True)
    for j in range(1, CONV_WIDTH):
        up = _shift_up(dco, next8, j)
        d_raw = d_raw + up * w[3 - j:4 - j, :]
        gw[3 - j] = jnp.sum(up * raw, axis=0, keepdims=True)
    gb = jnp.sum(dco, axis=0, keepdims=True)
    return d_raw, gw, gb


def _cumsum_rows(v):
    rows = v.shape[0]
    row = lax.broadcasted_iota(jnp.int32, v.shape, 0)
    s = 1
    while s < rows:
        v = v + jnp.where(row >= s, pltpu.roll(v, s, 0), 0.0)
        s *= 2
    return v


def _rev_cumsum_rows(v):
    rows = v.shape[0]
    row = lax.broadcasted_iota(jnp.int32, v.shape, 0)
    s = 1
    while s < rows:
        v = v + jnp.where(row < rows - s, pltpu.roll(v, rows - s, 0), 0.0)
        s *= 2
    return v


def _matmul(a, b, mode, name, tm=1024, tn=1024, tk=1024, exchange=None):
    if mode == "nn":
        (m, kk), n = a.shape, b.shape[1]
    elif mode == "nt":
        (m, kk), n = a.shape, b.shape[0]
    else:
        (kk, m), n = a.shape, b.shape[1]
    tm, tn, tk = min(tm, m), min(tn, n), min(tk, kk)
    assert m % tm == 0 and n % tn == 0 and kk % tk == 0, (name, a.shape, b.shape)
    nk = kk // tk
    dims = {"nn": NN, "nt": NT, "tn": TN}[mode]
    a_spec = pl.BlockSpec((tk, tm), lambda i, j, k: (k, i)) if mode == "tn" else pl.BlockSpec((tm, tk), lambda i, j, k: (i, k))
    b_spec = pl.BlockSpec((tn, tk), lambda i, j, k: (j, k)) if mode == "nt" else pl.BlockSpec((tk, tn), lambda i, j, k: (k, j))
    o_spec = pl.BlockSpec((tm, tn), lambda i, j, k: (i, j))

    def body_single(a_ref, b_ref, o_ref):
        o_ref[...] = _dot(a_ref[...], b_ref[...], dims)

    def body(a_ref, b_ref, o_ref, acc_ref):
        k = pl.program_id(2)

        @pl.when(k == 0)
        def _():
            acc_ref[...] = jnp.zeros_like(acc_ref)

        acc_ref[...] += _dot(a_ref[...], b_ref[...], dims)

        @pl.when(k == nk - 1)
        def _():
            o_ref[...] = acc_ref[...]

    grid = (m // tm, n // tn, nk)
    if exchange is None:
        return pl.pallas_call(
            body_single if nk == 1 else body, name=name, grid=grid, in_specs=[a_spec, b_spec], out_specs=o_spec,
            out_shape=jax.ShapeDtypeStruct((m, n), F32),
            scratch_shapes=[] if nk == 1 else [pltpu.VMEM((tm, tn), F32)],
            compiler_params=_params(("parallel", "parallel", "arbitrary")),
        )(a, b)
    at = lambda ids: functools.reduce(lambda u, v: u & v, [pl.program_id(d) == ids[d] for d in range(3)])
    riding, ex_in, ex_out, ex_shape, ex_sems = _riding(
        exchange, body_single if nk == 1 else body, 2, 1, lambda: at((0, 0, 0)), lambda: at(tuple(g - 1 for g in grid)))
    res = pl.pallas_call(
        riding, name=name, grid=grid, in_specs=[a_spec, b_spec] + ex_in, out_specs=[o_spec] + ex_out,
        out_shape=[jax.ShapeDtypeStruct((m, n), F32)] + ex_shape,
        scratch_shapes=([] if nk == 1 else [pltpu.VMEM((tm, tn), F32)]) + ex_sems,
        compiler_params=_params(("arbitrary", "arbitrary", "arbitrary")),
    )(a, b, *exchange["arrays"])
    return res[0], res[1:]


def _rms_fwd(x, g, name, rows=512):
    t, d = x.shape
    rows = min(rows, t)

    def body(x_ref, g_ref, h_ref):
        xv = x_ref[...]
        r = lax.rsqrt(jnp.mean(xv * xv, axis=-1, keepdims=True) + EPS)
        h_ref[...] = ((xv * r) * g_ref[...]).astype(h_ref.dtype)

    return pl.pallas_call(
        body, name=name, grid=(t // rows,),
        in_specs=[pl.BlockSpec((rows, d), lambda i: (i, 0)), pl.BlockSpec((1, d), lambda i: (0, 0))],
        out_specs=pl.BlockSpec((rows, d), lambda i: (i, 0)),
        out_shape=jax.ShapeDtypeStruct((t, d), MXU_DTYPE),
        compiler_params=_params(("parallel",)),
    )(x, g)


def _rms_bwd(x, dh, dres, g, name, rows=512):
    t, d = x.shape
    rows = min(rows, t)
    has_res = dres is not None

    def body(*refs):
        if has_res:
            x_ref, dh_ref, dr_ref, g_ref, dx_ref, gg_ref = refs
        else:
            x_ref, dh_ref, g_ref, dx_ref, gg_ref = refs

        @pl.when(pl.program_id(0) == 0)
        def _():
            gg_ref[...] = jnp.zeros_like(gg_ref)

        xv = x_ref[...]
        dhv = dh_ref[...]
        r = lax.rsqrt(jnp.mean(xv * xv, axis=-1, keepdims=True) + EPS)
        n = xv * r
        dn = dhv * g_ref[...]
        dx = r * (dn - n * jnp.mean(dn * n, axis=-1, keepdims=True))
        if has_res:
            dx = dx + dr_ref[...]
        dx_ref[...] = dx
        gg_ref[...] += jnp.sum(dhv * n, axis=0, keepdims=True)

    row_spec = pl.BlockSpec((rows, d), lambda i: (i, 0))
    vec_spec = pl.BlockSpec((1, d), lambda i: (0, 0))
    args = (x, dh) + ((dres,) if has_res else ()) + (g,)
    return pl.pallas_call(
        body, name=name, grid=(t // rows,),
        in_specs=[row_spec, row_spec] + ([row_spec] if has_res else []) + [vec_spec],
        out_specs=[row_spec, vec_spec],
        out_shape=[jax.ShapeDtypeStruct((t, d), F32), jax.ShapeDtypeStruct((1, d), F32)],
        compiler_params=_params(("arbitrary",)),
    )(*args)


def _dh_norm_bwd(segs, x, dres, g, rows=1024, tk=512, exchange=None):
    t, d = x.shape
    rows = min(rows, t)
    plan = []
    step0 = 0
    for a, _ in segs:
        kb = min(tk, a.shape[1])
        assert a.shape[1] % kb == 0, a.shape
        plan.append((step0, a.shape[1] // kb, kb))
        step0 += a.shape[1] // kb
    n_steps = step0
    ns = len(segs)

    def body(*refs):
        a_refs, w_refs = refs[0:2 * ns:2], refs[1:2 * ns:2]
        x_ref, dr_ref, g_ref, dx_ref, gg_ref, acc_ref = refs[2 * ns:]
        i, k = pl.program_id(0), pl.program_id(1)

        @pl.when((i == 0) & (k == 0))
        def _():
            gg_ref[...] = jnp.zeros_like(gg_ref)

        @pl.when(k == 0)
        def _():
            acc_ref[...] = jnp.zeros_like(acc_ref)

        for s, (first, nblk, _) in enumerate(plan):
            @pl.when((k >= first) & (k < first + nblk))
            def _(s=s):
                acc_ref[...] += _dot(a_refs[s][...], w_refs[s][...], NN)

        @pl.when(k == n_steps - 1)
        def _():
            xv = x_ref[...]
            dhv = acc_ref[...]
            r = lax.rsqrt(jnp.mean(xv * xv, axis=-1, keepdims=True) + EPS)
            n = xv * r
            dn = dhv * g_ref[...]
            dx_ref[...] = r * (dn - n * jnp.mean(dn * n, axis=-1, keepdims=True)) + dr_ref[...]
            gg_ref[...] += jnp.sum(dhv * n, axis=0, keepdims=True)

    in_specs, args = [], []
    for (a, w), (first, nblk, kb) in zip(segs, plan):
        blk = lambda k, first=first, nblk=nblk: jnp.clip(k - first, 0, nblk - 1)
        in_specs.append(pl.BlockSpec((rows, kb), lambda i, k, blk=blk: (i, blk(k))))
        in_specs.append(pl.BlockSpec((kb, d), lambda i, k, blk=blk: (blk(k), 0)))
        args += [a, w]
    row_spec = pl.BlockSpec((rows, d), lambda i, k: (i, 0))
    vec_spec = pl.BlockSpec((1, d), lambda i, k: (0, 0))
    n_tiles = t // rows
    body, ex_in, ex_out, ex_shape, ex_sems = _riding(
        exchange, body, 2 * ns + 3, 2,
        lambda: (pl.program_id(0) == 0) & (pl.program_id(1) == 0),
        lambda: (pl.program_id(0) == n_tiles - 1) & (pl.program_id(1) == n_steps - 1))
    res = pl.pallas_call(
        body, name="dh_norm_bwd", grid=(n_tiles, n_steps),
        in_specs=in_specs + [row_spec, row_spec, vec_spec] + ex_in, out_specs=[row_spec, vec_spec] + ex_out,
        out_shape=[jax.ShapeDtypeStruct((t, d), F32), jax.ShapeDtypeStruct((1, d), F32)] + ex_shape,
        scratch_shapes=[pltpu.VMEM((rows, d), F32)] + ex_sems,
        compiler_params=_params(("arbitrary", "arbitrary")),
    )(*args, x, dres, g, *(exchange["arrays"] if exchange else []))
    return res[0], res[1], res[2:]


def _pair_select(lo, m, h0):
    return jnp.where(lo, m[:, h0:h0 + 1], m[:, h0 + 1:h0 + 2])


def _group_select(lo, m, heads):
    return jnp.concatenate([_pair_select(lo, m, h0) for h0 in heads], axis=1)


def _halves(lo, v):
    return (jnp.sum(jnp.where(lo, v, 0.0), axis=1, keepdims=True),
            jnp.sum(jnp.where(lo, 0.0, v), axis=1, keepdims=True))


def _ssd_common(dt_raw, dtb, alog):
    dt = _softplus(dt_raw + dtb)
    aneg = -jnp.exp(alog)
    a_cs = _cumsum_rows(dt * aneg)
    return dt, aneg, a_cs, a_cs.T


def _ssd_specs(nc, rev):
    cidx = (lambda c: nc - 1 - c) if rev else (lambda c: c)
    L = SSD_CHUNK
    return dict(
        z=pl.BlockSpec((L, SSD_WIDTH), lambda c: (cidx(c), 0)),
        xr=pl.BlockSpec((L, SSD_WIDTH), lambda c: (cidx(c), 1)),
        br=pl.BlockSpec((L, 512), lambda c: (cidx(c), 8)),
        cr=pl.BlockSpec((L, 512), lambda c: (cidx(c), 9)),
        dt=pl.BlockSpec((L, DT_PAD), lambda c: (cidx(c), 0)),
        cwx=pl.BlockSpec((CONV_WIDTH, SSD_WIDTH), lambda c: (0, 0)),
        cwb=pl.BlockSpec((CONV_WIDTH, 512), lambda c: (0, 4)),
        cwc=pl.BlockSpec((CONV_WIDTH, 512), lambda c: (0, 5)),
        cbx=pl.BlockSpec((1, SSD_WIDTH), lambda c: (0, 0)),
        cbb=pl.BlockSpec((1, 512), lambda c: (0, 4)),
        cbc=pl.BlockSpec((1, 512), lambda c: (0, 5)),
        vec128=pl.BlockSpec((1, LANES), lambda c: (0, 0)),
        vecw=pl.BlockSpec((1, SSD_WIDTH), lambda c: (0, 0)),
        wide=pl.BlockSpec((L, SSD_WIDTH), lambda c: (cidx(c), 0)),
        states=pl.BlockSpec((1, SSD_GROUPS, GROUP_COLS, SSD_STATE), lambda c: (cidx(c), 0, 0, 0)),
    )


def _ssd_fwd(proj_ssd, dt_p, conv_w, conv_b, dtb, alog, d_row, ng_row):
    t = proj_ssd.shape[0]
    nc = t // SSD_CHUNK
    L = SSD_CHUNK
    sp = _ssd_specs(nc, False)

    def body(z_ref, xr_ref, br_ref, cr_ref, dt_ref, cwx_ref, cwb_ref, cwc_ref, cbx_ref, cbb_ref, cbc_ref,
             dtb_ref, alog_ref, d_ref, ng_ref, yssd_ref, y_ref, st_ref, pre_ref,
             hx_ref, hb_ref, hc_ref, state_ref, yacc_ref):
        @pl.when(pl.program_id(0) == 0)
        def _():
            hx_ref[...] = jnp.zeros_like(hx_ref)
            hb_ref[...] = jnp.zeros_like(hb_ref)
            hc_ref[...] = jnp.zeros_like(hc_ref)
            state_ref[...] = jnp.zeros_like(state_ref)

        xr, br, cr = xr_ref[...], br_ref[...], cr_ref[...]
        px = _causal_conv(xr, hx_ref[...], cwx_ref[...], cbx_ref[...])
        pb = _causal_conv(br, hb_ref[...], cwb_ref[...], cbb_ref[...])
        pc = _causal_conv(cr, hc_ref[...], cwc_ref[...], cbc_ref[...])
        hx_ref[...] = xr[L - 8:L, :]
        hb_ref[...] = br[L - 8:L, :]
        hc_ref[...] = cr[L - 8:L, :]
        pre_ref[:, 0:SSD_WIDTH] = px
        pre_ref[:, SSD_WIDTH:SSD_WIDTH + 512] = pb
        pre_ref[:, SSD_WIDTH + 512:SSD_CONV_CH] = pc
        xs = px * _sigmoid(px)
        bm = pb * _sigmoid(pb)
        cm = pc * _sigmoid(pc)

        dt, _, a_cs, a_t = _ssd_common(dt_ref[...], dtb_ref[...], alog_ref[...])
        exp_a = jnp.exp(a_cs)
        a_last = a_cs[L - 1:L, :]
        dte = jnp.exp(a_last - a_cs)
        dec = jnp.exp(a_last)

        lane = lax.broadcasted_iota(jnp.int32, (L, LANES), 1)
        sub = lax.broadcasted_iota(jnp.int32, (L, LANES), 0)
        lo = lane < SSD_HEAD_DIM
        causal = sub >= lane
        top = sub < SSD_HEAD_DIM

        for g in range(SSD_GROUPS):
            b_g = bm[:, g * SSD_STATE:(g + 1) * SSD_STATE]
            c_g = cm[:, g * SSD_STATE:(g + 1) * SSD_STATE]
            cb = _dot(c_g, b_g, NT)
            heads = [2 * (g * PAIRS_PER_GROUP + jj) for jj in range(PAIRS_PER_GROUP)]
            gcols = slice(g * GROUP_COLS, (g + 1) * GROUP_COLS)
            xs_g = xs[:, gcols]
            xdt_g = xs_g * _group_select(lo, dt, heads)
            h_g = state_ref[g]
            st_ref[0, g] = h_g
            y_off_g = _dot(c_g, h_g, NT) * _group_select(lo, exp_a, heads)
            s_new_g = _dot(xdt_g * _group_select(lo, dte, heads), b_g, TN)
            for jj, h0 in enumerate(heads):
                blk = slice(jj * LANES, (jj + 1) * LANES)
                cols = slice(g * GROUP_COLS + jj * LANES, g * GROUP_COLS + (jj + 1) * LANES)
                xdt = xdt_g[:, blk]
                g0 = jnp.where(causal, jnp.exp(a_cs[:, h0:h0 + 1] - a_t[h0:h0 + 1, :]), 0.0) * cb
                g1 = jnp.where(causal, jnp.exp(a_cs[:, h0 + 1:h0 + 2] - a_t[h0 + 1:h0 + 2, :]), 0.0) * cb
                lhs = jnp.concatenate([g0, g1], axis=1)
                rhs = jnp.concatenate([jnp.where(lo, xdt, 0.0), jnp.where(lo, 0.0, xdt)], axis=0)
                y_diag = _dot(lhs, rhs, NN)
                dec_rows = jnp.where(top, dec[:, h0:h0 + 1], dec[:, h0 + 1:h0 + 2])
                state_ref[g, blk, :] = h_g[blk, :] * dec_rows + s_new_g[blk, :]
                yacc_ref[:, cols] = (y_diag + y_off_g[:, blk]) + xs_g[:, blk] * d_ref[:, cols]

        y = yacc_ref[...]
        y_ref[...] = y
        zz = z_ref[...]
        y2 = y * (zz * _sigmoid(zz))
        gw = SSD_WIDTH // SSD_GROUPS
        for g in range(SSD_GROUPS):
            seg = y2[:, g * gw:(g + 1) * gw]
            r = lax.rsqrt(jnp.mean(seg * seg, axis=-1, keepdims=True) + EPS)
            yssd_ref[:, g * gw:(g + 1) * gw] = ((seg * r) * ng_ref[:, g * gw:(g + 1) * gw]).astype(yssd_ref.dtype)

    return pl.pallas_call(
        body, name="ssd_fwd", grid=(nc,),
        in_specs=[sp["z"], sp["xr"], sp["br"], sp["cr"], sp["dt"], sp["cwx"], sp["cwb"], sp["cwc"],
                  sp["cbx"], sp["cbb"], sp["cbc"], sp["vec128"], sp["vec128"], sp["vecw"], sp["vecw"]],
        out_specs=[sp["wide"], sp["wide"], sp["states"], pl.BlockSpec((L, SSD_CONV_CH), lambda c: (c, 0))],
        out_shape=[jax.ShapeDtypeStruct((t, SSD_WIDTH), MXU_DTYPE), jax.ShapeDtypeStruct((t, SSD_WIDTH), F32),
                   jax.ShapeDtypeStruct((nc, SSD_GROUPS, GROUP_COLS, SSD_STATE), F32), jax.ShapeDtypeStruct((t, SSD_CONV_CH), F32)],
        scratch_shapes=[pltpu.VMEM((8, SSD_WIDTH), F32), pltpu.VMEM((8, 512), F32), pltpu.VMEM((8, 512), F32),
                        pltpu.VMEM((SSD_GROUPS, GROUP_COLS, SSD_STATE), F32), pltpu.VMEM((L, SSD_WIDTH), F32)],
        compiler_params=_params(("arbitrary",)),
    )(proj_ssd, proj_ssd, proj_ssd, proj_ssd, dt_p, conv_w, conv_w, conv_w, conv_b, conv_b, conv_b,
      dtb, alog, d_row, ng_row)


def _ssd_bwd(proj_ssd, pre, dt_p, y, states, dyssd, conv_w, dtb, alog, d_row, ng_row, exchange=None):
    t = proj_ssd.shape[0]
    nc = t // SSD_CHUNK
    L = SSD_CHUNK
    sp = _ssd_specs(nc, True)

    def pre_spec(width, col):
        return pl.BlockSpec((L, width), lambda c: (nc - 1 - c, col))

    def body(z_ref, xr_ref, br_ref, cr_ref, px_ref, pb_ref, pc_ref, dt_ref, y_ref, st_ref, dy_ref,
             cwx_ref, cwb_ref, cwc_ref, dtb_ref, alog_ref, d_ref, ng_ref,
             dssd_ref, ddt_ref, gcw_ref, gcb_ref, gdtb_ref, galog_ref, gd_ref, gng_ref,
             gn_ref, nx_ref, nb_ref, ncc_ref, dxs_ref):
        step = pl.program_id(0)

        @pl.when(step == 0)
        def _():
            gn_ref[...] = jnp.zeros_like(gn_ref)
            nx_ref[...] = jnp.zeros_like(nx_ref)
            nb_ref[...] = jnp.zeros_like(nb_ref)
            ncc_ref[...] = jnp.zeros_like(ncc_ref)
            for ref in (gcw_ref, gcb_ref, gdtb_ref, galog_ref, gd_ref, gng_ref):
                ref[...] = jnp.zeros_like(ref)

        xr, br, cr = xr_ref[...], br_ref[...], cr_ref[...]
        cwx, cwb, cwc = cwx_ref[...], cwb_ref[...], cwc_ref[...]
        px, pb, pc = px_ref[...], pb_ref[...], pc_ref[...]
        sx, sb, sc = _sigmoid(px), _sigmoid(pb), _sigmoid(pc)
        xs, bm, cm = px * sx, pb * sb, pc * sc

        dt_in = dt_ref[...] + dtb_ref[...]
        dt, aneg, a_cs, a_t = _ssd_common(dt_ref[...], dtb_ref[...], alog_ref[...])
        exp_a = jnp.exp(a_cs)
        a_last = a_cs[L - 1:L, :]
        dte = jnp.exp(a_last - a_cs)
        dec = jnp.exp(a_last)

        lane = lax.broadcasted_iota(jnp.int32, (L, LANES), 1)
        sub = lax.broadcasted_iota(jnp.int32, (L, LANES), 0)
        lo = lane < SSD_HEAD_DIM
        causal = sub >= lane
        top = sub < SSD_HEAD_DIM
        last_row = sub == L - 1

        yv = y_ref[...]
        zz = z_ref[...]
        sz = _sigmoid(zz)
        silz = zz * sz
        y2 = yv * silz
        dyv = dy_ref[...]
        gw = SSD_WIDTH // SSD_GROUPS
        d_y2_parts = []
        gng_parts = []
        for g in range(SSD_GROUPS):
            seg = y2[:, g * gw:(g + 1) * gw]
            dseg = dyv[:, g * gw:(g + 1) * gw]
            r = lax.rsqrt(jnp.mean(seg * seg, axis=-1, keepdims=True) + EPS)
            n = seg * r
            dn = dseg * ng_ref[:, g * gw:(g + 1) * gw]
            gng_parts.append(jnp.sum(dseg * n, axis=0, keepdims=True))
            d_y2_parts.append(r * (dn - n * jnp.mean(dn * n, axis=-1, keepdims=True)))
        d_y2 = jnp.concatenate(d_y2_parts, axis=1)
        gng_ref[...] += jnp.concatenate(gng_parts, axis=1)
        d_y = d_y2 * silz
        dssd_ref[:, 0:SSD_WIDTH] = (d_y2 * yv * (sz * (1.0 + zz * (1.0 - sz)))).astype(dssd_ref.dtype)
        gd_ref[...] += jnp.sum(d_y * xs, axis=0, keepdims=True)
        dxs_ref[...] = d_y * d_ref[...]

        d_a = jnp.zeros((L, LANES), F32)
        d_at = jnp.zeros((LANES, L), F32)
        ddt = jnp.zeros((L, LANES), F32)
        d_b_parts, d_c_parts = [], []
        for g in range(SSD_GROUPS):
            b_g = bm[:, g * SSD_STATE:(g + 1) * SSD_STATE]
            c_g = cm[:, g * SSD_STATE:(g + 1) * SSD_STATE]
            cb = _dot(c_g, b_g, NT)
            d_cb = jnp.zeros((L, L), F32)
            heads = [2 * (g * PAIRS_PER_GROUP + jj) for jj in range(PAIRS_PER_GROUP)]
            gcols = slice(g * GROUP_COLS, (g + 1) * GROUP_COLS)
            dy_g, xs_g = d_y[:, gcols], xs[:, gcols]
            dt_g = _group_select(lo, dt, heads)
            expa_g = _group_select(lo, exp_a, heads)
            dte_g = _group_select(lo, dte, heads)
            xdt_g = xs_g * dt_g
            h_g = st_ref[0, g]
            gn_g = gn_ref[g]
            dys_g = dy_g * expa_g
            d_cg = _dot(dys_g, h_g, NN)
            d_h_g = _dot(dys_g, c_g, TN)
            t1_g = dy_g * _dot(c_g, h_g, NT) * expa_g
            d_bg = _dot(xdt_g * dte_g, gn_g, NN)
            dxdt_g = _dot(b_g, gn_g, NT) * dte_g
            t2_g = dxdt_g * xdt_g
            t12_g = t1_g - t2_g
            gh_g = jnp.sum(gn_g * h_g, axis=1, keepdims=True)
            for jj, h0 in enumerate(heads):
                blk = slice(jj * LANES, (jj + 1) * LANES)
                cols = slice(g * GROUP_COLS + jj * LANES, g * GROUP_COLS + (jj + 1) * LANES)
                dy_p, xs_p, xdt, dt_pp = dy_g[:, blk], xs_g[:, blk], xdt_g[:, blk], dt_g[:, blk]
                l0 = jnp.where(causal, jnp.exp(a_cs[:, h0:h0 + 1] - a_t[h0:h0 + 1, :]), 0.0)
                l1 = jnp.where(causal, jnp.exp(a_cs[:, h0 + 1:h0 + 2] - a_t[h0 + 1:h0 + 2, :]), 0.0)
                g0, g1 = l0 * cb, l1 * cb
                dcat = jnp.concatenate([jnp.where(lo, dy_p, 0.0), jnp.where(lo, 0.0, dy_p)], axis=0)
                d_xdt = dxdt_g[:, blk] + _dot(jnp.concatenate([g0, g1], axis=0), dcat, TN)
                dm = _dot(dcat, xdt, NT)
                dm0, dm1 = dm[0:L], dm[L:2 * L]
                d_cb = d_cb + (l0 * dm0 + l1 * dm1)
                e0, e1 = dm0 * g0, dm1 * g1
                a0, a1 = _halves(lo, t12_g[:, blk])
                a0 = a0 + jnp.sum(e0, axis=1, keepdims=True)
                a1 = a1 + jnp.sum(e1, axis=1, keepdims=True)
                s0, s1 = _halves(lo, t2_g[:, blk])
                gh = gh_g[blk, :]
                dd0 = jnp.sum(jnp.where(top[:, 0:1], gh, 0.0), axis=0, keepdims=True)
                dd1 = jnp.sum(jnp.where(top[:, 0:1], 0.0, gh), axis=0, keepdims=True)
                end0 = jnp.sum(s0, axis=0, keepdims=True) + dd0 * dec[:, h0:h0 + 1]
                end1 = jnp.sum(s1, axis=0, keepdims=True) + dd1 * dec[:, h0 + 1:h0 + 2]
                d_a = d_a + jnp.where(lane == h0, a0 + jnp.where(last_row, end0, 0.0), 0.0)
                d_a = d_a + jnp.where(lane == h0 + 1, a1 + jnp.where(last_row, end1, 0.0), 0.0)
                d_at = d_at - jnp.where(sub == h0, jnp.sum(e0, axis=0, keepdims=True), 0.0)
                d_at = d_at - jnp.where(sub == h0 + 1, jnp.sum(e1, axis=0, keepdims=True), 0.0)
                dec_rows = jnp.where(top, dec[:, h0:h0 + 1], dec[:, h0 + 1:h0 + 2])
                gn_ref[g, blk, :] = d_h_g[blk, :] + dec_rows * gn_g[blk, :]
                q0, q1 = _halves(lo, d_xdt * xs_p)
                ddt = ddt + jnp.where(lane == h0, q0, 0.0) + jnp.where(lane == h0 + 1, q1, 0.0)
                dxs_ref[:, cols] += d_xdt * dt_pp
            d_cg = d_cg + _dot(d_cb, b_g, NN)
            d_bg = d_bg + _dot(d_cb, c_g, TN)
            d_b_parts.append(d_bg)
            d_c_parts.append(d_cg)

        rc = _rev_cumsum_rows(d_a + d_at.T)
        d_dt = rc * aneg + ddt
        galog_ref[...] += jnp.sum(rc * dt, axis=0, keepdims=True) * aneg
        d_dtraw = d_dt * _sigmoid(dt_in)
        gdtb_ref[...] += jnp.sum(d_dtraw, axis=0, keepdims=True)
        ddt_ref[...] = d_dtraw.astype(ddt_ref.dtype)

        def dsilu(p, s):
            return s * (1.0 + p * (1.0 - s))

        dcx = dxs_ref[...] * dsilu(px, sx)
        dcb = jnp.concatenate(d_b_parts, axis=1) * dsilu(pb, sb)
        dcc = jnp.concatenate(d_c_parts, axis=1) * dsilu(pc, sc)
        drx, gwx, gbx = _conv_backward(dcx, nx_ref[...], xr, cwx)
        drb, gwb, gbb = _conv_backward(dcb, nb_ref[...], br, cwb)
        drc, gwc, gbc = _conv_backward(dcc, ncc_ref[...], cr, cwc)
        nx_ref[...] = dcx[0:8, :]
        nb_ref[...] = dcb[0:8, :]
        ncc_ref[...] = dcc[0:8, :]
        dssd_ref[:, SSD_WIDTH:2 * SSD_WIDTH] = drx.astype(dssd_ref.dtype)
        dssd_ref[:, 2 * SSD_WIDTH:2 * SSD_WIDTH + 512] = drb.astype(dssd_ref.dtype)
        dssd_ref[:, 2 * SSD_WIDTH + 512:SSD_SEG] = drc.astype(dssd_ref.dtype)
        for k in range(CONV_WIDTH):
            gcw_ref[k:k + 1, :] += jnp.concatenate([gwx[k], gwb[k], gwc[k]], axis=1)
        gcb_ref[...] += jnp.concatenate([gbx, gbb, gbc], axis=1)

    const = lambda shape: pl.BlockSpec(shape, lambda c: (0,) * len(shape))
    body, ex_in, ex_out, ex_shape, ex_sems = _riding(
        exchange, body, 18, 8, lambda: pl.program_id(0) == 0, lambda: pl.program_id(0) == nc - 1)
    res = pl.pallas_call(
        body, name="ssd_bwd", grid=(nc,),
        in_specs=[sp["z"], sp["xr"], sp["br"], sp["cr"], pre_spec(SSD_WIDTH, 0), pre_spec(512, 4), pre_spec(512, 5),
                  sp["dt"], sp["wide"], sp["states"], sp["wide"],
                  sp["cwx"], sp["cwb"], sp["cwc"], sp["vec128"], sp["vec128"], sp["vecw"], sp["vecw"]] + ex_in,
        out_specs=[pl.BlockSpec((L, SSD_SEG), lambda c: (nc - 1 - c, 0)), sp["dt"],
                   const((CONV_WIDTH, SSD_CONV_CH)), const((1, SSD_CONV_CH)), const((1, LANES)), const((1, LANES)),
                   const((1, SSD_WIDTH)), const((1, SSD_WIDTH))] + ex_out,
        out_shape=[jax.ShapeDtypeStruct((t, SSD_SEG), MXU_DTYPE), jax.ShapeDtypeStruct((t, DT_PAD), MXU_DTYPE),
                   jax.ShapeDtypeStruct((CONV_WIDTH, SSD_CONV_CH), F32), jax.ShapeDtypeStruct((1, SSD_CONV_CH), F32),
                   jax.ShapeDtypeStruct((1, LANES), F32), jax.ShapeDtypeStruct((1, LANES), F32),
                   jax.ShapeDtypeStruct((1, SSD_WIDTH), F32), jax.ShapeDtypeStruct((1, SSD_WIDTH), F32)] + ex_shape,
        scratch_shapes=[pltpu.VMEM((SSD_GROUPS, GROUP_COLS, SSD_STATE), F32), pltpu.VMEM((8, SSD_WIDTH), F32),
                        pltpu.VMEM((8, 512), F32), pltpu.VMEM((8, 512), F32), pltpu.VMEM((L, SSD_WIDTH), F32)] + ex_sems,
        compiler_params=_params(("arbitrary",)),
    )(proj_ssd, proj_ssd, proj_ssd, proj_ssd, pre, pre, pre, dt_p, y, states, dyssd,
      conv_w, conv_w, conv_w, dtb, alog, d_row, ng_row, *(exchange["arrays"] if exchange else []))
    return res[:8], res[8:]


def _lru_gates(xl, wa_ref, wx_ref, ba, bx, lam):
    pre_a, pre_x = [], []
    for g in range(LRU_NGROUPS):
        xg = xl[:, g * LRU_GROUP:(g + 1) * LRU_GROUP]
        pre_a.append(_dot(xg, wa_ref[g], NN))
        pre_x.append(_dot(xg, wx_ref[g], NN))
    r = _sigmoid(jnp.concatenate(pre_a, axis=1) + ba)
    i = _sigmoid(jnp.concatenate(pre_x, axis=1) + bx)
    log_a = (-LRU_C * r) * _softplus(-lam)
    a = jnp.exp(log_a)
    mult = jnp.sqrt(-jnp.tanh(log_a) * (a * a + 1.0))
    return r, i, log_a, a, mult


def _scan_rows(p, u, carry, reverse):
    rows, w = p.shape
    groups = rows // 8
    p3, u3 = p.reshape(groups, 8, w), u.reshape(groups, 8, w)
    row = lax.broadcasted_iota(jnp.int32, (groups, 8, w), 1)
    for s in (1, 2, 4):
        ok = row < 8 - s if reverse else row >= s
        shift = 8 - s if reverse else s
        u3 = p3 * jnp.where(ok, pltpu.roll(u3, shift, 1), 0.0) + u3
        p3 = p3 * jnp.where(ok, pltpu.roll(p3, shift, 1), 1.0)
    out = [None] * groups
    for k in (range(groups - 1, -1, -1) if reverse else range(groups)):
        out[k] = p3[k] * carry + u3[k]
        carry = out[k][0:1, :] if reverse else out[k][7:8, :]
    return jnp.concatenate(out, axis=0), carry


def _lru_fwd(proj_lru, conv_w, conv_b, wa, wx, ba, bx, lam):
    t = proj_lru.shape[0]
    rows = min(LRU_ROWS, t)
    nb = t // rows
    W = LRU_WIDTH

    def body(lg_ref, lx_ref, cw_ref, cb_ref, wa_ref, wx_ref, ba_ref, bx_ref, lam_ref, ylru_ref, h_ref,
             halo_ref, carry_ref):
        @pl.when(pl.program_id(0) == 0)
        def _():
            halo_ref[...] = jnp.zeros_like(halo_ref)
            carry_ref[...] = jnp.zeros_like(carry_ref)

        lx = lx_ref[...]
        xl = _causal_conv(lx, halo_ref[...], cw_ref[...], cb_ref[...])
        halo_ref[...] = lx[rows - 8:rows, :]
        _, i, _, a, mult = _lru_gates(xl, wa_ref, wx_ref, ba_ref[...], bx_ref[...], lam_ref[...])
        u = mult * (i * xl)
        h, carry_ref[...] = _scan_rows(a, u, carry_ref[...], False)
        h_ref[...] = h
        lg = lg_ref[...]
        ylru_ref[...] = (h * (lg * _sigmoid(lg))).astype(ylru_ref.dtype)

    const = lambda shape: pl.BlockSpec(shape, lambda b: (0,) * len(shape))
    return pl.pallas_call(
        body, name="lru_fwd", grid=(nb,),
        in_specs=[pl.BlockSpec((rows, W), lambda b: (b, 0)), pl.BlockSpec((rows, W), lambda b: (b, 1)),
                  const((CONV_WIDTH, W)), const((1, W)), const((LRU_NGROUPS, LRU_GROUP, LRU_GROUP)),
                  const((LRU_NGROUPS, LRU_GROUP, LRU_GROUP)), const((1, W)), const((1, W)), const((1, W))],
        out_specs=[pl.BlockSpec((rows, W), lambda b: (b, 0)), pl.BlockSpec((rows, W), lambda b: (b, 0))],
        out_shape=[jax.ShapeDtypeStruct((t, W), MXU_DTYPE), jax.ShapeDtypeStruct((t, W), F32)],
        scratch_shapes=[pltpu.VMEM((8, W), F32), pltpu.VMEM((1, W), F32)],
        compiler_params=_params(("arbitrary",)),
    )(proj_lru, proj_lru, conv_w, conv_b, wa, wx, ba, bx, lam)


def _lru_bwd(proj_lru, h, dylru, conv_w, conv_b, wa, wx, ba, bx, lam, exchange=None):
    t = proj_lru.shape[0]
    rows = min(LRU_ROWS, t)
    nb = t // rows
    W = LRU_WIDTH
    groups8 = rows // 8

    def rev(b):
        return nb - 1 - b

    def halo_spec(col):
        return pl.BlockSpec((8, W), lambda b: (jnp.maximum(rev(b) * groups8 - 1, 0), col))

    def body(lg_ref, lx_ref, hlx_ref, h_ref, hh_ref, dy_ref, cw_ref, cb_ref, wa_ref, wx_ref, ba_ref, bx_ref, lam_ref,
             dlru_ref, gcw_ref, gcb_ref, gba_ref, gbx_ref, glam_ref, gwa_ref, gwx_ref,
             gcarry_ref, afirst_ref, nxt_ref):
        step = pl.program_id(0)

        @pl.when(step == 0)
        def _():
            gcarry_ref[...] = jnp.zeros_like(gcarry_ref)
            afirst_ref[...] = jnp.zeros_like(afirst_ref)
            nxt_ref[...] = jnp.zeros_like(nxt_ref)
            for ref in (gcw_ref, gcb_ref, gba_ref, gbx_ref, glam_ref, gwa_ref, gwx_ref):
                ref[...] = jnp.zeros_like(ref)

        keep = jnp.where(step == nb - 1, 0.0, 1.0)
        lx = lx_ref[...]
        hlx = hlx_ref[...] * keep
        cw = cw_ref[...]
        xl = _causal_conv(lx, hlx, cw, cb_ref[...])
        lam = lam_ref[...]
        r, i, log_a, a, mult = _lru_gates(xl, wa_ref, wx_ref, ba_ref[...], bx_ref[...], lam)
        hv = h_ref[...]
        h_prev = _shift_down(hv, hh_ref[...] * keep, 1)
        lg = lg_ref[...]
        sg = _sigmoid(lg)
        dyv = dy_ref[...]
        d_h = dyv * (lg * sg)
        dlru_ref[:, 0:W] = (dyv * hv * (sg * (1.0 + lg * (1.0 - sg)))).astype(dlru_ref.dtype)

        row = lax.broadcasted_iota(jnp.int32, (rows, W), 0)
        p = jnp.where(row < rows - 1, pltpu.roll(a, rows - 1, 0), afirst_ref[...])
        gsc, gcarry_ref[...] = _scan_rows(p, d_h, gcarry_ref[...], True)
        afirst_ref[...] = a[0:1, :]

        d_a = gsc * h_prev
        v = i * xl
        d_mult = gsc * v
        d_v = gsc * mult
        d_i = d_v * xl
        d_xl = d_v * i
        d_la = d_a * a - d_mult * (a * a) / mult
        sp_neg = _softplus(-lam)
        d_r = d_la * (-LRU_C * sp_neg)
        glam_ref[...] += jnp.sum(d_la * r, axis=0, keepdims=True) * (LRU_C * _sigmoid(-lam))
        d_pa = d_r * r * (1.0 - r)
        d_px = d_i * i * (1.0 - i)
        gba_ref[...] += jnp.sum(d_pa, axis=0, keepdims=True)
        gbx_ref[...] += jnp.sum(d_px, axis=0, keepdims=True)
        parts = []
        for g in range(LRU_NGROUPS):
            cols = slice(g * LRU_GROUP, (g + 1) * LRU_GROUP)
            xg, dpa_g, dpx_g = xl[:, cols], d_pa[:, cols], d_px[:, cols]
            parts.append(_dot(dpa_g, wa_ref[g], NT) + _dot(dpx_g, wx_ref[g], NT))
            gwa_ref[g] += _dot(xg, dpa_g, TN)
            gwx_ref[g] += _dot(xg, dpx_g, TN)
        d_xl = d_xl + jnp.concatenate(parts, axis=1)
        d_lx, gw, gb = _conv_backward(d_xl, nxt_ref[...], lx, cw)
        nxt_ref[...] = d_xl[0:8, :]
        dlru_ref[:, W:2 * W] = d_lx.astype(dlru_ref.dtype)
        for k in range(CONV_WIDTH):
            gcw_ref[k:k + 1, :] += gw[k]
        gcb_ref[...] += gb

    const = lambda shape: pl.BlockSpec(shape, lambda b: (0,) * len(shape))
    wspec = const((LRU_NGROUPS, LRU_GROUP, LRU_GROUP))
    blk = lambda col: pl.BlockSpec((rows, W), lambda b: (rev(b), col))
    body, ex_in, ex_out, ex_shape, ex_sems = _riding(
        exchange, body, 13, 8, lambda: pl.program_id(0) == 0, lambda: pl.program_id(0) == nb - 1)
    res = pl.pallas_call(
        body, name="lru_bwd", grid=(nb,),
        in_specs=[blk(0), blk(1), halo_spec(1), blk(0), halo_spec(0), blk(0),
                  const((CONV_WIDTH, W)), const((1, W)), wspec, wspec, const((1, W)), const((1, W)), const((1, W))] + ex_in,
        out_specs=[pl.BlockSpec((rows, 2 * W), lambda b: (rev(b), 0)), const((CONV_WIDTH, W)), const((1, W)),
                   const((1, W)), const((1, W)), const((1, W)), wspec, wspec] + ex_out,
        out_shape=[jax.ShapeDtypeStruct((t, 2 * W), MXU_DTYPE), jax.ShapeDtypeStruct((CONV_WIDTH, W), F32),
                   jax.ShapeDtypeStruct((1, W), F32), jax.ShapeDtypeStruct((1, W), F32), jax.ShapeDtypeStruct((1, W), F32),
                   jax.ShapeDtypeStruct((1, W), F32), jax.ShapeDtypeStruct((LRU_NGROUPS, LRU_GROUP, LRU_GROUP), F32),
                   jax.ShapeDtypeStruct((LRU_NGROUPS, LRU_GROUP, LRU_GROUP), F32)] + ex_shape,
        scratch_shapes=[pltpu.VMEM((1, W), F32), pltpu.VMEM((1, W), F32), pltpu.VMEM((8, W), F32)] + ex_sems,
        compiler_params=_params(("arbitrary",)),
    )(proj_lru, proj_lru, proj_lru, h, h, dylru, conv_w, conv_b, wa, wx, ba, bx, lam, *(exchange["arrays"] if exchange else []))
    return res[:8], res[8:]


def _mem_scores(q_h, k_h):
    s = _dot(q_h, k_h, NT) * (MEM_HEAD_DIM ** -0.5)
    s = s - jnp.max(s, axis=-1, keepdims=True)
    e = jnp.exp(s)
    return e / jnp.sum(e, axis=-1, keepdims=True)


def _mem_fwd(q, kv, rows=512):
    t = q.shape[0]
    rows = min(rows, t)
    m = kv.shape[0]

    def body(q_ref, kv_ref, y_ref):
        for hd in range(MEM_HEADS):
            cols = slice(hd * MEM_HEAD_DIM, (hd + 1) * MEM_HEAD_DIM)
            vcols = slice(D_MODEL + hd * MEM_HEAD_DIM, D_MODEL + (hd + 1) * MEM_HEAD_DIM)
            p = _mem_scores(q_ref[:, cols], kv_ref[:, cols])
            y_ref[:, cols] = _dot(p, kv_ref[:, vcols], NN).astype(y_ref.dtype)

    return pl.pallas_call(
        body, name="mem_fwd", grid=(t // rows,),
        in_specs=[pl.BlockSpec((rows, D_MODEL), lambda i: (i, 0)), pl.BlockSpec((m, 2 * D_MODEL), lambda i: (0, 0))],
        out_specs=pl.BlockSpec((rows, D_MODEL), lambda i: (i, 0)),
        out_shape=jax.ShapeDtypeStruct((t, D_MODEL), MXU_DTYPE),
        compiler_params=_params(("parallel",)),
    )(q, kv)


def _mem_bwd(q, kv, dy, rows=512):
    t = q.shape[0]
    rows = min(rows, t)
    m = kv.shape[0]

    def body(q_ref, kv_ref, dy_ref, dq_ref, dkv_ref):
        @pl.when(pl.program_id(0) == 0)
        def _():
            dkv_ref[...] = jnp.zeros_like(dkv_ref)

        for hd in range(MEM_HEADS):
            cols = slice(hd * MEM_HEAD_DIM, (hd + 1) * MEM_HEAD_DIM)
            vcols = slice(D_MODEL + hd * MEM_HEAD_DIM, D_MODEL + (hd + 1) * MEM_HEAD_DIM)
            q_h, k_h, dy_h = q_ref[:, cols], kv_ref[:, cols], dy_ref[:, cols]
            p = _mem_scores(q_h, k_h)
            dp = _dot(dy_h, kv_ref[:, vcols], NT)
            dkv_ref[:, vcols] += _dot(p, dy_h, TN)
            ds = p * (dp - jnp.sum(dp * p, axis=-1, keepdims=True)) * (MEM_HEAD_DIM ** -0.5)
            dq_ref[:, cols] = _dot(ds, k_h, NN).astype(dq_ref.dtype)
            dkv_ref[:, cols] += _dot(ds, q_h, TN)

    return pl.pallas_call(
        body, name="mem_bwd", grid=(t // rows,),
        in_specs=[pl.BlockSpec((rows, D_MODEL), lambda i: (i, 0)), pl.BlockSpec((m, 2 * D_MODEL), lambda i: (0, 0)),
                  pl.BlockSpec((rows, D_MODEL), lambda i: (i, 0))],
        out_specs=[pl.BlockSpec((rows, D_MODEL), lambda i: (i, 0)), pl.BlockSpec((m, 2 * D_MODEL), lambda i: (0, 0))],
        out_shape=[jax.ShapeDtypeStruct((t, D_MODEL), MXU_DTYPE), jax.ShapeDtypeStruct((m, 2 * D_MODEL), F32)],
        compiler_params=_params(("arbitrary",)),
    )(q, kv, dy)


def _merge_fwd(x, yssd, ylru, ymem, gl, w_bs, w_bl, w_bm, w_out, fg, tgt, rows=256):
    t = x.shape[0]
    rows = min(rows, t)
    D = D_MODEL

    def body(x_ref, ys_ref, yl_ref, ym_ref, gl_ref, wbs_ref, wbl_ref, wbm_ref, wo_ref, fg_ref, tgt_ref,
             ps_ref, pl_ref, pm_ref, mg_ref, dx2_ref, loss_ref, gfg_ref):
        @pl.when(pl.program_id(0) == 0)
        def _():
            loss_ref[...] = jnp.zeros_like(loss_ref)
            gfg_ref[...] = jnp.zeros_like(gfg_ref)

        ps = _dot(ys_ref[...], wbs_ref[...], NN)
        pl_ = _dot(yl_ref[...], wbl_ref[...], NN)
        pm = _dot(ym_ref[...], wbm_ref[...], NN)
        ps_ref[...] = ps
        pl_ref[...] = pl_
        pm_ref[...] = pm
        merged = (_sigmoid(gl_ref[:, 0:D]) * ps + _sigmoid(gl_ref[:, D:2 * D]) * pl_) + _sigmoid(gl_ref[:, 2 * D:3 * D]) * pm
        mg_ref[...] = merged.astype(mg_ref.dtype)
        x2 = x_ref[...] + _dot(merged, wo_ref[...], NN)
        r2 = lax.rsqrt(jnp.mean(x2 * x2, axis=-1, keepdims=True) + EPS)
        xn = x2 * r2
        fg = fg_ref[...]
        diff = xn * fg - tgt_ref[...]
        tile_loss = 0.5 * jnp.sum(jnp.mean(diff * diff, axis=-1, keepdims=True), axis=0, keepdims=True)
        loss_ref[...] += jnp.broadcast_to(tile_loss, loss_ref.shape)
        d_out = diff * (1.0 / D)
        gfg_ref[...] += jnp.sum(d_out * xn, axis=0, keepdims=True)
        dxn = d_out * fg
        dx2_ref[...] = r2 * (dxn - xn * jnp.mean(dxn * xn, axis=-1, keepdims=True))

    row = lambda w: pl.BlockSpec((rows, w), lambda i: (i, 0))
    const = lambda shape: pl.BlockSpec(shape, lambda i: (0,) * len(shape))
    return pl.pallas_call(
        body, name="merge_fwd", grid=(t // rows,),
        in_specs=[row(D), row(SSD_WIDTH), row(LRU_WIDTH), row(D), row(3 * D), const((SSD_WIDTH, D)), const((LRU_WIDTH, D)),
                  const((D, D)), const((D, D)), const((1, D)), row(D)],
        out_specs=[row(D), row(D), row(D), row(D), row(D), const((1, LANES)), const((1, D))],
        out_shape=[jax.ShapeDtypeStruct((t, D), F32), jax.ShapeDtypeStruct((t, D), F32), jax.ShapeDtypeStruct((t, D), F32),
                   jax.ShapeDtypeStruct((t, D), MXU_DTYPE), jax.ShapeDtypeStruct((t, D), F32),
                   jax.ShapeDtypeStruct((1, LANES), F32), jax.ShapeDtypeStruct((1, D), F32)],
        compiler_params=_params(("arbitrary",)),
    )(x, yssd, ylru, ymem, gl, w_bs, w_bl, w_bm, w_out, fg, tgt)


def _merge_bwd(dx2, gl, ps, pl_in, pm, w_bs, w_bl, w_bm, w_out, rows=256):
    t = dx2.shape[0]
    rows = min(rows, t)
    D = D_MODEL

    def body(dx2_ref, gl_ref, ps_ref, pl_ref, pm_ref, wbs_ref, wbl_ref, wbm_ref, wo_ref,
             dg_ref, dps_ref, dpl_ref, dpm_ref, dys_ref, dyl_ref, dym_ref):
        dm = _dot(dx2_ref[...], wo_ref[...], NT)
        for idx, (p_ref, dp_ref, w_ref, dy_ref) in enumerate(
                ((ps_ref, dps_ref, wbs_ref, dys_ref), (pl_ref, dpl_ref, wbl_ref, dyl_ref), (pm_ref, dpm_ref, wbm_ref, dym_ref))):
            gate = _sigmoid(gl_ref[:, idx * D:(idx + 1) * D])
            dg_ref[:, idx * D:(idx + 1) * D] = ((dm * p_ref[...]) * gate * (1.0 - gate)).astype(dg_ref.dtype)
            dp = dm * gate
            dp_ref[...] = dp.astype(dp_ref.dtype)
            dy_ref[...] = _dot(dp, w_ref[...], NT)

    row = lambda w: pl.BlockSpec((rows, w), lambda i: (i, 0))
    const = lambda shape: pl.BlockSpec(shape, lambda i: (0,) * len(shape))
    return pl.pallas_call(
        body, name="merge_bwd", grid=(t // rows,),
        in_specs=[row(D), row(3 * D), row(D), row(D), row(D), const((SSD_WIDTH, D)), const((LRU_WIDTH, D)),
                  const((D, D)), const((D, D))],
        out_specs=[row(3 * D), row(D), row(D), row(D), row(SSD_WIDTH), row(LRU_WIDTH), row(D)],
        out_shape=[jax.ShapeDtypeStruct((t, 3 * D), MXU_DTYPE), jax.ShapeDtypeStruct((t, D), MXU_DTYPE),
                   jax.ShapeDtypeStruct((t, D), MXU_DTYPE), jax.ShapeDtypeStruct((t, D), MXU_DTYPE),
                   jax.ShapeDtypeStruct((t, SSD_WIDTH), F32), jax.ShapeDtypeStruct((t, LRU_WIDTH), F32),
                   jax.ShapeDtypeStruct((t, D), F32)],
        compiler_params=_params(("parallel",)),
    )(dx2, gl, ps, pl_in, pm, w_bs, w_bl, w_bm, w_out)


def _mesh_place():
    x, y, c = lax.axis_index("x"), lax.axis_index("y"), lax.axis_index("c")
    return x, y, c, 4 * x + 2 * y + c


def _other_chips(x, y):
    return [(1 - x, y), (x, 1 - y), (1 - x, 1 - y)]


def _all_gather_plan(arrs):
    n = len(arrs)

    def parts(ins, outs, send_sems, recv_sems, local_sems):
        x, y, c, me = _mesh_place()
        sibling = (x, y, 1 - c)
        chips = _other_chips(x, y)

        def slot(px, py, pc):
            return 4 * px + 2 * py + pc

        def copy(a, k, block, to, src=None):
            return pltpu.make_async_remote_copy(
                src_ref=outs[a].at[block] if src is None else src, dst_ref=outs[a].at[block],
                send_sem=send_sems.at[a, k], recv_sem=recv_sems.at[a, k], device_id=to, device_id_type=pl.DeviceIdType.MESH)

        def local():
            return [pltpu.make_async_copy(ins[a], outs[a].at[me], local_sems.at[a]) for a in range(n)]

        def first():
            return [copy(a, k, me, to, src=ins[a]) for a in range(n)
                    for k, to in enumerate([sibling] + [(*chip, c) for chip in chips])]

        return x, y, c, sibling, chips, slot, copy, local, first

    def start(ins, outs, *sems):
        *_, local, first = parts(ins, outs, *sems)
        for cp in local() + first():
            cp.start()

    def wait(ins, outs, *sems):
        x, y, c, sibling, chips, slot, copy, local, first = parts(ins, outs, *sems)
        sends = first()
        for j, chip in enumerate(chips):
            for a in range(n):
                copy(a, 1 + j, slot(*chip, c), sibling).wait_recv()
                passed = copy(a, 4 + j, slot(*chip, c), sibling)
                passed.start()
                sends.append(passed)
        for a in range(n):
            copy(a, 0, slot(x, y, 1 - c), sibling).wait_recv()
        for j, chip in enumerate(chips):
            for a in range(n):
                copy(a, 4 + j, slot(*chip, 1 - c), sibling).wait_recv()
        for cp in sends:
            cp.wait_send()
        for cp in local():
            cp.wait()

    return dict(arrays=list(arrs), out_shape=[jax.ShapeDtypeStruct((N_DEV,) + a.shape, a.dtype) for a in arrs],
                sems=[(n, 7), (n, 7), (n,)], start=start, wait=wait)


def _all_gather(arrs, name):
    return _run_exchange(_all_gather_plan(arrs), name)


N_CHIPS = 4


def _pair_plan(parts):
    n = len(parts)

    def copies(ins, outs, send_sems, recv_sems):
        x, y, c, _ = _mesh_place()
        return [pltpu.make_async_remote_copy(src_ref=ins[a].at[q, 1 - c], dst_ref=outs[a].at[q], send_sem=send_sems.at[a, q],
                                             recv_sem=recv_sems.at[a, q], device_id=(x, y, 1 - c), device_id_type=pl.DeviceIdType.MESH)
                for a in range(n) for q in range(N_CHIPS)]

    def start(ins, outs, send_sems, recv_sems):
        for cp in copies(ins, outs, send_sems, recv_sems):
            cp.start()

    def wait(ins, outs, send_sems, recv_sems):
        cps = copies(ins, outs, send_sems, recv_sems)
        for cp in cps:
            cp.wait_recv()
        for cp in cps:
            cp.wait_send()

    return dict(arrays=list(parts), out_shape=[jax.ShapeDtypeStruct((N_CHIPS,) + a.shape[2:], a.dtype) for a in parts],
                sems=[(n, N_CHIPS), (n, N_CHIPS)], start=start, wait=wait)


def _chip_plan(sums):
    n = len(sums)

    def copies(ins, outs, send_sems, recv_sems, arriving):
        x, y, c, _ = _mesh_place()
        my_chip = 2 * x + y
        cps = []
        for a in range(n):
            for j, (px, py) in enumerate(_other_chips(x, y)):
                src, dst = (my_chip, 2 * px + py) if arriving else (2 * px + py, my_chip)
                cps.append(pltpu.make_async_remote_copy(
                    src_ref=ins[a].at[src], dst_ref=outs[a].at[dst], send_sem=send_sems.at[a, j], recv_sem=recv_sems.at[a, j],
                    device_id=(px, py, c), device_id_type=pl.DeviceIdType.MESH))
        return cps

    def start(ins, outs, send_sems, recv_sems):
        for cp in copies(ins, outs, send_sems, recv_sems, False):
            cp.start()

    def wait(ins, outs, send_sems, recv_sems):
        for cp in copies(ins, outs, send_sems, recv_sems, True):
            cp.wait_recv()
        for cp in copies(ins, outs, send_sems, recv_sems, False):
            cp.wait_send()

    return dict(arrays=list(sums), out_shape=[jax.ShapeDtypeStruct(a.shape, a.dtype) for a in sums],
                sems=[(n, 3), (n, 3)], start=start, wait=wait)


def _both(p1, p2):
    n1, s1 = len(p1["arrays"]), len(p1["sems"])

    def each(method):
        def run(ins, outs, *sems):
            p1[method](ins[:n1], outs[:n1], *sems[:s1])
            p2[method](ins[n1:], outs[n1:], *sems[s1:])
        return run

    return dict(arrays=p1["arrays"] + p2["arrays"], out_shape=p1["out_shape"] + p2["out_shape"],
                sems=p1["sems"] + p2["sems"], start=each("start"), wait=each("wait"))


def _run_exchange(plan, name):
    n = len(plan["arrays"])

    def body(*refs):
        ins, outs, sems = refs[:n], refs[n:2 * n], refs[2 * n:]
        plan["start"](ins, outs, *sems)
        plan["wait"](ins, outs, *sems)

    any_spec = pl.BlockSpec(memory_space=pl.ANY)
    return pl.pallas_call(
        body, name=name, in_specs=[any_spec] * n, out_specs=[any_spec] * n, out_shape=plan["out_shape"],
        scratch_shapes=[pltpu.SemaphoreType.DMA(shape) for shape in plan["sems"]],
    )(*plan["arrays"])


def _riding(plan, body, n_in, n_out, first, last):
    if plan is None:
        return body, [], [], [], []
    ne = len(plan["arrays"])

    def wrapped(*refs):
        ins, ex_in = refs[:n_in], refs[n_in:n_in + ne]
        outs = refs[n_in + ne:n_in + ne + n_out]
        ex_out = refs[n_in + ne + n_out:n_in + 2 * ne + n_out]
        n_sems = len(plan["sems"])
        scratch, sems = refs[n_in + 2 * ne + n_out:-n_sems], refs[-n_sems:]

        @pl.when(first())
        def _():
            plan["start"](ex_in, ex_out, *sems)

        body(*ins, *outs, *scratch)

        @pl.when(last())
        def _():
            plan["wait"](ex_in, ex_out, *sems)

    any_spec = pl.BlockSpec(memory_space=pl.ANY)
    sems = [pltpu.SemaphoreType.DMA(shape) for shape in plan["sems"]]
    return wrapped, [any_spec] * ne, [any_spec] * ne, plan["out_shape"], sems


def _col_tile(r, c, limit_bytes):
    assert c % LANES == 0, c
    best = LANES
    for cand in range(LANES, c + 1, LANES):
        if c % cand == 0 and r * cand * 4 <= limit_bytes:
            best = cand
    return best


def _chip_sum(part, recv, core, name):
    _, _, r, c = part.shape
    ct = _col_tile(r, c, 2 << 20)

    def body(core_ref, p_ref, r_ref, s_ref, t_ref):
        s = p_ref[...] + r_ref[...]
        s_ref[...] = s
        t_ref[...] = s.astype(t_ref.dtype)

    blk = pl.BlockSpec((None, r, ct), lambda q, i, core_ref: (q, 0, i))
    return pl.pallas_call(
        body, name=name,
        grid_spec=pltpu.PrefetchScalarGridSpec(
            num_scalar_prefetch=1, grid=(N_CHIPS, c // ct),
            in_specs=[pl.BlockSpec((None, None, r, ct), lambda q, i, core_ref: (q, core_ref[0], 0, i)), blk],
            out_specs=[blk, blk]),
        out_shape=[jax.ShapeDtypeStruct((N_CHIPS, r, c), F32), jax.ShapeDtypeStruct((N_CHIPS, r, c), GRAD_WIRE_DTYPE)],
        compiler_params=_params(("parallel", "parallel")),
    )(core, part, recv)


def _row_tile(r, limit):
    if r <= limit:
        return r
    best = 8
    for cand in range(8, limit + 1, 8):
        if r % cand == 0:
            best = cand
    assert r % best == 0, r
    return best


def _adam_update(w, g, m, v):
    nm = ADAM_B1 * m + (1.0 - ADAM_B1) * g
    nv = ADAM_B2 * v + (1.0 - ADAM_B2) * (g * g)
    m_hat = nm / (1.0 - ADAM_B1 ** ADAM_STEP)
    v_hat = nv / (1.0 - ADAM_B2 ** ADAM_STEP)
    return -ADAM_LR * (m_hat / (jnp.sqrt(v_hat) + ADAM_EPS) + ADAM_WD * w), nm, nv


def _sum_adamw(own, recv, chip, w, m, v, name, exchange=None):
    _, r, c = own.shape
    ct = _col_tile(r, c, 1 << 20)

    def body(chip_ref, o_ref, r1_ref, r2_ref, r3_ref, w_ref, m_ref, v_ref, g_ref, d_ref, nm_ref, nv_ref):
        g = ((o_ref[...] + r1_ref[...].astype(F32)) + r2_ref[...].astype(F32)) + r3_ref[...].astype(F32)
        g_ref[...] = g
        d_ref[...], nm_ref[...], nv_ref[...] = _adam_update(w_ref[...], g, m_ref[...], v_ref[...])

    def slot(k):
        return pl.BlockSpec((None, r, ct), lambda i, chip_ref: ((chip_ref[0] + k) % N_CHIPS, 0, i))

    spec = pl.BlockSpec((r, ct), lambda i, chip_ref: (0, i))
    shape = jax.ShapeDtypeStruct((r, c), F32)
    n_tiles = c // ct
    body, ex_in, ex_out, ex_shape, ex_sems = _riding(
        exchange, body, 8, 4, lambda: pl.program_id(0) == 0, lambda: pl.program_id(0) == n_tiles - 1)
    res = pl.pallas_call(
        body, name=name,
        grid_spec=pltpu.PrefetchScalarGridSpec(
            num_scalar_prefetch=1, grid=(n_tiles,),
            in_specs=[slot(0), slot(1), slot(2), slot(3), spec, spec, spec] + ex_in, out_specs=[spec] * 4 + ex_out,
            scratch_shapes=ex_sems),
        out_shape=[shape] * 4 + ex_shape,
        compiler_params=_params(("arbitrary",)),
    )(chip, own, recv, recv, recv, w, m, v, *(exchange["arrays"] if exchange else []))
    return res[:4], res[4:]


def _small_adamw(parts, ws, ms, vs):
    n = len(parts)

    def body(*refs):
        p_refs, w_refs, m_refs, v_refs = refs[:n], refs[n:2 * n], refs[2 * n:3 * n], refs[3 * n:4 * n]
        outs = refs[4 * n:]
        for i in range(n):
            g = p_refs[i][0]
            for k in range(1, N_DEV):
                g = g + p_refs[i][k]
            outs[i][...] = g
            outs[n + i][...], outs[2 * n + i][...], outs[3 * n + i][...] = _adam_update(
                w_refs[i][...], g, m_refs[i][...], v_refs[i][...])

    vmem = pl.BlockSpec(memory_space=pltpu.VMEM)
    shapes = [jax.ShapeDtypeStruct(w.shape, F32) for w in ws]
    res = pl.pallas_call(
        body, name="adamw_small", in_specs=[vmem] * (4 * n), out_specs=[vmem] * (4 * n), out_shape=shapes * 4,
        compiler_params=pltpu.CompilerParams(vmem_limit_bytes=VMEM_LIMIT),
    )(*parts, *ws, *ms, *vs)
    return res[:n], res[n:2 * n], res[2 * n:3 * n], res[3 * n:]


def _pack(arrs, dtype, row_multiple):
    flat = jnp.concatenate([a.reshape(-1).astype(dtype) for a in arrs])
    unit = LANES * row_multiple
    padded = -(-flat.shape[0] // unit) * unit
    return jnp.pad(flat, (0, padded - flat.shape[0])).reshape(-1, LANES)


def _unpack(packed, shapes, lead=()):
    flat = packed.reshape(lead + (-1,))
    out, off = [], 0
    for shp in shapes:
        n = math.prod(shp)
        out.append(flat[..., off:off + n].reshape(lead + tuple(shp)))
        off += n
    return out


def _gather_cols(g, lo, hi):
    width = g.shape[2]
    pieces = []
    for s in range(N_DEV):
        a, e = max(lo, s * width), min(hi, (s + 1) * width)
        if a < e:
            pieces.append(g[s, :, a - s * width:e - s * width])
    return pieces[0] if len(pieces) == 1 else jnp.concatenate(pieces, axis=1)


def _scatter_cols(segs, width):
    slots = []
    for k in range(N_DEV):
        lo, hi = k * width, (k + 1) * width
        pieces = []
        for arr, s_lo, s_hi in segs:
            a, e = max(lo, s_lo), min(hi, s_hi)
            if a < e:
                pieces.append(arr[:, a - s_lo:e - s_lo])
        slots.append(pieces[0] if len(pieces) == 1 else jnp.concatenate(pieces, axis=1))
    return jnp.stack(slots)


def _block_diag_groups(w):
    w4 = w.reshape(LRU_NGROUPS, 4, LRU_BLOCK, LRU_BLOCK)
    eye = jnp.eye(4, dtype=w.dtype)
    return jnp.einsum("gaij,ab->gaibj", w4, eye).reshape(LRU_NGROUPS, LRU_GROUP, LRU_GROUP)


def _block_diag_extract(wg):
    w5 = wg.reshape(LRU_NGROUPS, 4, LRU_BLOCK, 4, LRU_BLOCK)
    idx = jnp.arange(4)
    return w5[:, idx, :, idx, :].transpose(1, 0, 2, 3).reshape(LRU_BLOCKS, LRU_BLOCK, LRU_BLOCK)


BIG = ("w_in", "w_kv", "w_br_ssd", "w_br_lru", "w_br_mem", "w_out")
SMALL_SHARDED = ("ssd_conv_w", "ssd_norm_g", "lru_conv_w")
REPLICATED = ("norm_g", "ssd_conv_b", "ssd_dt_bias", "ssd_a_log", "ssd_d", "lru_conv_b", "lru_w_a", "lru_b_a",
              "lru_w_x", "lru_b_x", "lru_lambda", "mem_norm_g", "final_g")
WEIGHTS = ("norm_g", "w_in", "ssd_conv_w", "ssd_conv_b", "ssd_dt_bias", "ssd_a_log", "ssd_d", "ssd_norm_g", "lru_conv_w",
           "lru_conv_b", "lru_w_a", "lru_b_a", "lru_w_x", "lru_b_x", "lru_lambda", "mem_norm_g", "w_kv", "w_br_ssd",
           "w_br_lru", "w_br_mem", "w_out", "final_g")


def kernel(x, mem, norm_g, w_in, ssd_conv_w, ssd_conv_b, ssd_dt_bias, ssd_a_log, ssd_d, ssd_norm_g, lru_conv_w, lru_conv_b, lru_w_a, lru_b_a, lru_w_x, lru_b_x, lru_lambda, mem_norm_g, w_kv, w_br_ssd, w_br_lru, w_br_mem, w_out, final_g, loss_target, m_norm_g, m_w_in, m_ssd_conv_w, m_ssd_conv_b, m_ssd_dt_bias, m_ssd_a_log, m_ssd_d, m_ssd_norm_g, m_lru_conv_w, m_lru_conv_b, m_lru_w_a, m_lru_b_a, m_lru_w_x, m_lru_b_x, m_lru_lambda, m_mem_norm_g, m_w_kv, m_w_br_ssd, m_w_br_lru, m_w_br_mem, m_w_out, m_final_g, v_norm_g, v_w_in, v_ssd_conv_w, v_ssd_conv_b, v_ssd_dt_bias, v_ssd_a_log, v_ssd_d, v_ssd_norm_g, v_lru_conv_w, v_lru_conv_b, v_lru_w_a, v_lru_b_a, v_lru_w_x, v_lru_b_x, v_lru_lambda, v_mem_norm_g, v_w_kv, v_w_br_ssd, v_w_br_lru, v_w_br_mem, v_w_out, v_final_g):
    env = dict(locals())
    W = {n: env[n] for n in WEIGHTS}
    M = {n: env["m_" + n] for n in WEIGHTS}
    V = {n: env["v_" + n] for n in WEIGHTS}
    me = 4 * lax.axis_index("x") + 2 * lax.axis_index("y") + lax.axis_index("c")
    t = x.shape[1]
    xt = x[0]
    memt = mem[0]
    tgt = loss_target[0]

    small_shapes = [W[n].shape for n in SMALL_SHARDED]
    as2d = lambda d, n: jnp.transpose(d[n][0]) if n == "w_in" else d[n][0]
    (g_in,) = _all_gather([as2d(W, "w_in").astype(MXU_DTYPE)], "w_in_all_gather")
    b = SEG_BOUNDS
    w_in_t = g_in.reshape(IN_WIDTH, D_MODEL)
    w_ssd, w_lru, w_q, w_g = w_in_t[b[0]:b[1]], w_in_t[b[2]:b[3]], w_in_t[b[3]:b[4]], w_in_t[b[4]:b[5]]
    w_dt = jnp.pad(w_in_t[b[1]:b[2]], ((0, DT_PAD - SSD_HEADS), (0, 0)))

    h = _rms_fwd(xt, norm_g, "norm_fwd")
    later = _all_gather_plan([as2d(W, n).astype(MXU_DTYPE) for n in BIG[1:]] + [_pack([W[n] for n in SMALL_SHARDED], F32, 8)])
    proj_ssd, (g_kv, g_bs, g_bl, g_bm, g_out, gs) = _matmul(h, w_ssd, "nt", "proj_ssd", tm=4096, tn=512, exchange=later)
    g_cw, g_ng, g_lcw = _unpack(gs, small_shapes, (N_DEV,))
    cols = lambda a: jnp.moveaxis(a[:, 0], 0, -2).reshape(a.shape[2:-1] + (-1,))
    rows_ = lambda a: a.reshape((-1,) + a.shape[2:])
    w_bs_f, w_bl_f, w_bm_f, w_out_f = rows_(g_bs), rows_(g_bl), rows_(g_bm), rows_(g_out)
    conv_w_f, ssd_ng_f, lru_cw_f = cols(g_cw), cols(g_ng), cols(g_lcw)
    w_kv_f = _gather_cols(g_kv, 0, 2 * D_MODEL)

    pad_heads = lambda a: jnp.pad(a, ((0, 0), (0, LANES - SSD_HEADS)))
    dtb, alog = pad_heads(ssd_dt_bias), pad_heads(ssd_a_log)
    d_row = jnp.repeat(ssd_d, SSD_HEAD_DIM, axis=1)
    ng_row = ssd_ng_f.reshape(1, SSD_WIDTH)
    wa_g, wx_g = _block_diag_groups(lru_w_a[0]), _block_diag_groups(lru_w_x[0])
    ba, bx = lru_b_a.reshape(1, LRU_WIDTH), lru_b_x.reshape(1, LRU_WIDTH)
    fg = final_g.reshape(1, D_MODEL)

    proj_lru = _matmul(h, w_lru, "nt", "proj_lru", tm=4096, tn=512)
    proj_q = _matmul(h, w_q, "nt", "proj_q", tm=4096, tn=512)
    proj_g = _matmul(h, w_g, "nt", "proj_g", tm=4096, tn=512)
    proj_dt = _matmul(h, w_dt, "nt", "proj_dt", tm=4096)
    mem_n = _rms_fwd(memt, mem_norm_g, "mem_norm_fwd")
    kv = _matmul(mem_n, w_kv_f, "nn", "mem_kv")
    yssd, y_scan, states, ssd_pre = _ssd_fwd(proj_ssd, proj_dt, conv_w_f, ssd_conv_b, dtb, alog, d_row, ng_row)
    ylru, h_lru = _lru_fwd(proj_lru, lru_cw_f, lru_conv_b, wa_g, wx_g, ba, bx, lru_lambda)
    ymem = _mem_fwd(proj_q, kv)
    ps, pl_, pm, merged, dx2, loss_vec, g_fg = _merge_fwd(xt, yssd, ylru, ymem, proj_g, w_bs_f, w_bl_f, w_bm_f, w_out_f, fg, tgt)

    d_g, dps, dpl, dpm, dyssd, dylru, dymem = _merge_bwd(dx2, proj_g, ps, pl_, pm, w_bs_f, w_bl_f, w_bm_f, w_out_f)
    gw_out = _matmul(merged, dx2, "tn", "grad_w_out", tk=2048)
    gw_bs = _matmul(yssd, dps, "tn", "grad_w_br_ssd", tm=2048)
    gw_bl = _matmul(ylru, dpl, "tn", "grad_w_br_lru", tm=LRU_WIDTH, tk=2048)
    gw_bm = _matmul(ymem, dpm, "tn", "grad_w_br_mem", tk=2048)
    d_q, d_kv = _mem_bwd(proj_q, kv, dymem)
    gw_kv = _matmul(mem_n, d_kv, "tn", "grad_w_kv")
    d_memn = _matmul(d_kv, w_kv_f, "nt", "d_mem_n")
    _, g_memng = _rms_bwd(memt, d_memn, None, mem_norm_g, "mem_norm_bwd")

    core = lax.axis_index("c").astype(jnp.int32).reshape(1)
    chip = (2 * lax.axis_index("x") + lax.axis_index("y")).astype(jnp.int32).reshape(1)
    by_chip = lambda a: a.reshape((N_CHIPS, 2, -1) + a.shape[1:])
    early = ("w_kv", "w_br_ssd", "w_br_lru", "w_br_mem", "w_out")
    early_parts = [by_chip(_scatter_cols([(gw_kv, 0, 2 * D_MODEL)], 2 * D_MODEL // N_DEV).reshape(-1, 2 * D_MODEL // N_DEV)),
                   by_chip(gw_bs), by_chip(gw_bl), by_chip(gw_bm), by_chip(gw_out)]
    (d_lru, gl_cw, gl_cb, g_ba, g_bx, g_lam, gwa_g, gwx_g), early_sib = _lru_bwd(
        proj_lru, h_lru, dylru, lru_cw_f, lru_conv_b, wa_g, wx_g, ba, bx, lru_lambda, exchange=_pair_plan(early_parts))
    early_sums = [_chip_sum(p, r, core, "chip_sum_" + n) for n, p, r in zip(early, early_parts, early_sib)]
    (d_ssd, d_dt, gs_cw, gs_cb, g_dtb, g_alog, g_dch, g_ngrow), early_recv = _ssd_bwd(
        proj_ssd, ssd_pre, proj_dt, y_scan, states, dyssd, conv_w_f, dtb, alog, d_row, ng_row,
        exchange=_chip_plan([s16 for _, s16 in early_sums]))
    gw_ssd = _matmul(d_ssd, h, "tn", "grad_w_in_ssd", tm=2560)
    gw_lru = _matmul(d_lru, h, "tn", "grad_w_in_lru", tm=1536, tk=2048)
    gw_q = _matmul(d_q, h, "tn", "grad_w_in_q", tk=2048)
    gw_g = _matmul(d_g, h, "tn", "grad_w_in_g", tm=1536, tk=2048)
    gw_dt = _matmul(d_dt, h, "tn", "grad_w_in_dt", tk=2048)
    in_part = by_chip(jnp.concatenate([gw_ssd, gw_dt[:SSD_HEADS], gw_lru, gw_q, gw_g], axis=0))
    (in_sib,) = _run_exchange(_pair_plan([in_part]), "grad_pair_exchange")
    in_sum = _chip_sum(in_part, in_sib, core, "chip_sum_w_in")

    small_grads = {
        "ssd_conv_w": gs_cw, "ssd_conv_b": gs_cb, "ssd_dt_bias": g_dtb[:, :SSD_HEADS],
        "ssd_a_log": g_alog[:, :SSD_HEADS], "ssd_d": jnp.sum(g_dch.reshape(SSD_HEADS, SSD_HEAD_DIM), axis=1).reshape(1, SSD_HEADS),
        "ssd_norm_g": g_ngrow.reshape(SSD_GROUPS, -1), "lru_conv_w": gl_cw, "lru_conv_b": gl_cb,
        "lru_w_a": _block_diag_extract(gwa_g), "lru_b_a": g_ba, "lru_w_x": _block_diag_extract(gwx_g), "lru_b_x": g_bx,
        "lru_lambda": g_lam, "mem_norm_g": g_memng, "final_g": g_fg,
    }
    small_all = REPLICATED + SMALL_SHARDED

    def small_shape(n, shards):
        shp = W[n].shape[1:] if W[n].ndim > 2 else (1, W[n].shape[-1])
        return shp[:-1] + (shp[-1] * shards,)

    riders = tuple(small_grads)
    small_plan = _all_gather_plan([small_grads[n].reshape(small_shape(n, N_DEV if n in SMALL_SHARDED else 1)) for n in riders])
    grad_x, g_normg, landed = _dh_norm_bwd([(d_ssd, w_ssd), (d_lru, w_lru), (d_q, w_q), (d_g, w_g), (d_dt, w_dt)], xt, dx2,
                                           norm_g, exchange=_both(_chip_plan([in_sum[1]]), small_plan))
    small_recv = dict(zip(riders, landed[1:]))
    reduced = {"w_in": (in_sum[0], landed[0]), **{n: (s[0], r) for n, s, r in zip(early, early_sums, early_recv)}}

    grads, delta, new_m, new_v = {}, {}, {}, {}
    for n in BIG:
        s32, recv = reduced[n]
        res, landed = _sum_adamw(s32, recv, chip, as2d(W, n), as2d(M, n), as2d(V, n), "adamw_" + n,
                                 exchange=_all_gather_plan([g_normg]) if n == "w_in" else None)
        if n == "w_in":
            small_recv["norm_g"] = landed[0]
        for dst, a in zip((grads, delta, new_m, new_v), res):
            dst[n] = (jnp.transpose(a) if n == "w_in" else a)[None]

    parts = []
    for n in small_all:
        a = small_recv[n]
        if n in SMALL_SHARDED:
            width = W[n].shape[-1]
            a = lax.dynamic_slice_in_dim(a, me * width, width, axis=a.ndim - 1)
        parts.append(a)
    canon = lambda d: [d[n].reshape(small_shape(n, 1)) for n in small_all]
    for dst, res in zip((grads, delta, new_m, new_v), _small_adamw(parts, canon(W), canon(M), canon(V))):
        for n, a in zip(small_all, res):
            dst[n] = a.reshape(W[n].shape)

    loss = lax.psum(loss_vec[0, 0], ("x", "y", "c"))
    return (loss, grad_x[None], *[grads[n] for n in WEIGHTS], *[delta[n] for n in WEIGHTS],
            *[new_m[n] for n in WEIGHTS], *[new_v[n] for n in WEIGHTS])
```

```python
import functools
import math

import jax
import jax.numpy as jnp
from jax import lax
from jax.experimental import pallas as pl
from jax.experimental.pallas import tpu as pltpu

F32 = jnp.float32
MXU_DTYPE = jnp.bfloat16
GRAD_WIRE_DTYPE = jnp.bfloat16

D_MODEL = 1024
EPS = 1e-6
CONV_WIDTH = 4
SSD_WIDTH = 2048
SSD_HEAD_DIM = 64
SSD_HEADS = 32
SSD_GROUPS = 4
SSD_STATE = 128
SSD_CHUNK = 128
SSD_BC = SSD_GROUPS * SSD_STATE
SSD_CONV_CH = SSD_WIDTH + 2 * SSD_BC
SSD_PAIRS = SSD_HEADS // 2
PAIRS_PER_GROUP = SSD_PAIRS // SSD_GROUPS
GROUP_COLS = SSD_WIDTH // SSD_GROUPS
LRU_WIDTH = 1536
LRU_BLOCKS = 16
LRU_BLOCK = 96
LRU_GROUP = 4 * LRU_BLOCK
LRU_NGROUPS = LRU_WIDTH // LRU_GROUP
LRU_C = 8.0
LRU_ROWS = 256
MEM_HEADS = 4
MEM_HEAD_DIM = 256
IN_WIDTH = 12320
N_DEV = 8
LANES = 128
SSD_SEG = SSD_WIDTH + SSD_CONV_CH
DT_PAD = LANES
SEG_BOUNDS = (0, 5120, 5152, 8224, 9248, 12320)

ADAM_LR = 0.001
ADAM_B1 = 0.9
ADAM_B2 = 0.999
ADAM_EPS = 1e-08
ADAM_WD = 0.01
ADAM_STEP = 10

VMEM_LIMIT = 56 * 1024 * 1024

NN = (((1,), (0,)), ((), ()))
NT = (((1,), (1,)), ((), ()))
TN = (((0,), (0,)), ((), ()))


def _dot(a, b, dims):
    return lax.dot_general(a.astype(MXU_DTYPE), b.astype(MXU_DTYPE), dims, preferred_element_type=F32)


def _sigmoid(x):
    return 0.5 * jnp.tanh(0.5 * x) + 0.5


def _log1p(e):
    u = 1.0 + e
    return jnp.where(u == 1.0, e, jnp.log(u) * (e / jnp.where(u == 1.0, 1.0, u - 1.0)))


def _softplus(x):
    return jnp.maximum(x, 0.0) + _log1p(jnp.exp(-jnp.abs(x)))


def _params(semantics):
    return pltpu.CompilerParams(dimension_semantics=semantics, vmem_limit_bytes=VMEM_LIMIT)


def _shift_down(cur, halo8, k):
    rolled = pltpu.roll(cur, k, 0)
    row8 = lax.broadcasted_iota(jnp.int32, halo8.shape, 0)
    top = jnp.where(row8 >= k, rolled[0:8], pltpu.roll(halo8, k, 0))
    return jnp.concatenate([top, rolled[8:]], axis=0)


def _shift_up(cur, next8, k):
    rows = cur.shape[0]
    rolled = pltpu.roll(cur, rows - k, 0)
    row8 = lax.broadcasted_iota(jnp.int32, next8.shape, 0)
    bot = jnp.where(row8 < 8 - k, rolled[rows - 8:rows], pltpu.roll(next8, 8 - k, 0))
    return jnp.concatenate([rolled[:rows - 8], bot], axis=0)


def _causal_conv(raw, halo8, w, b):
    acc = raw * w[3:4, :] + b
    for k in range(1, CONV_WIDTH):
        acc = acc + _shift_down(raw, halo8, k) * w[3 - k:4 - k, :]
    return acc


def _conv_backward(dco, next8, raw, w):
    d_raw = dco * w[3:4, :]
    gw = [None] * CONV_WIDTH
    gw[3] = jnp.sum(dco * raw, axis=0, keepdims=True)
    for j in range(1, CONV_WIDTH):
        up = _shift_up(dco, next8, j)
        d_raw = d_raw + up * w[3 - j:4 - j, :]
        gw[3 - j] = jnp.sum(up * raw, axis=0, keepdims=True)
    gb = jnp.sum(dco, axis=0, keepdims=True)
    return d_raw, gw, gb


def _cumsum_rows(v):
    rows = v.shape[0]
    row = lax.broadcasted_iota(jnp.int32, v.shape, 0)
    s = 1
    while s < rows:
        v = v + jnp.where(row >= s, pltpu.roll(v, s, 0), 0.0)
        s *= 2
    return v


def _rev_cumsum_rows(v):
    rows = v.shape[0]
    row = lax.broadcasted_iota(jnp.int32, v.shape, 0)
    s = 1
    while s < rows:
        v = v + jnp.where(row < rows - s, pltpu.roll(v, rows - s, 0), 0.0)
        s *= 2
    return v


def _matmul(a, b, mode, name, tm=1024, tn=1024, tk=1024, exchange=None):
    if mode == "nn":
        (m, kk), n = a.shape, b.shape[1]
    elif mode == "nt":
        (m, kk), n = a.shape, b.shape[0]
    else:
        (kk, m), n = a.shape, b.shape[1]
    tm, tn, tk = min(tm, m), min(tn, n), min(tk, kk)
    assert m % tm == 0 and n % tn == 0 and kk % tk == 0, (name, a.shape, b.shape)
    nk = kk // tk
    dims = {"nn": NN, "nt": NT, "tn": TN}[mode]
    a_spec = pl.BlockSpec((tk, tm), lambda i, j, k: (k, i)) if mode == "tn" else pl.BlockSpec((tm, tk), lambda i, j, k: (i, k))
    b_spec = pl.BlockSpec((tn, tk), lambda i, j, k: (j, k)) if mode == "nt" else pl.BlockSpec((tk, tn), lambda i, j, k: (k, j))
    o_spec = pl.BlockSpec((tm, tn), lambda i, j, k: (i, j))

    def body_single(a_ref, b_ref, o_ref):
        o_ref[...] = _dot(a_ref[...], b_ref[...], dims)

    def body(a_ref, b_ref, o_ref, acc_ref):
        k = pl.program_id(2)

        @pl.when(k == 0)
        def _():
            acc_ref[...] = jnp.zeros_like(acc_ref)

        acc_ref[...] += _dot(a_ref[...], b_ref[...], dims)

        @pl.when(k == nk - 1)
        def _():
            o_ref[...] = acc_ref[...]

    grid = (m // tm, n // tn, nk)
    if exchange is None:
        return pl.pallas_call(
            body_single if nk == 1 else body, name=name, grid=grid, in_specs=[a_spec, b_spec], out_specs=o_spec,
            out_shape=jax.ShapeDtypeStruct((m, n), F32),
            scratch_shapes=[] if nk == 1 else [pltpu.VMEM((tm, tn), F32)],
            compiler_params=_params(("parallel", "parallel", "arbitrary")),
        )(a, b)
    at = lambda ids: functools.reduce(lambda u, v: u & v, [pl.program_id(d) == ids[d] for d in range(3)])
    riding, ex_in, ex_out, ex_shape, ex_sems = _riding(
        exchange, body_single if nk == 1 else body, 2, 1, lambda: at((0, 0, 0)), lambda: at(tuple(g - 1 for g in grid)))
    res = pl.pallas_call(
        riding, name=name, grid=grid, in_specs=[a_spec, b_spec] + ex_in, out_specs=[o_spec] + ex_out,
        out_shape=[jax.ShapeDtypeStruct((m, n), F32)] + ex_shape,
        scratch_shapes=([] if nk == 1 else [pltpu.VMEM((tm, tn), F32)]) + ex_sems,
        compiler_params=_params(("arbitrary", "arbitrary", "arbitrary")),
    )(a, b, *exchange["arrays"])
    return res[0], res[1:]


def _rms_fwd(x, g, name, rows=512):
    t, d = x.shape
    rows = min(rows, t)

    def body(x_ref, g_ref, h_ref):
        xv = x_ref[...]
        r = lax.rsqrt(jnp.mean(xv * xv, axis=-1, keepdims=True) + EPS)
        h_ref[...] = ((xv * r) * g_ref[...]).astype(h_ref.dtype)

    return pl.pallas_call(
        body, name=name, grid=(t // rows,),
        in_specs=[pl.BlockSpec((rows, d), lambda i: (i, 0)), pl.BlockSpec((1, d), lambda i: (0, 0))],
        out_specs=pl.BlockSpec((rows, d), lambda i: (i, 0)),
        out_shape=jax.ShapeDtypeStruct((t, d), MXU_DTYPE),
        compiler_params=_params(("parallel",)),
    )(x, g)


def _rms_bwd(x, dh, dres, g, name, rows=512):
    t, d = x.shape
    rows = min(rows, t)
    has_res = dres is not None

    def body(*refs):
        if has_res:
            x_ref, dh_ref, dr_ref, g_ref, dx_ref, gg_ref = refs
        else:
            x_ref, dh_ref, g_ref, dx_ref, gg_ref = refs

        @pl.when(pl.program_id(0) == 0)
        def _():
            gg_ref[...] = jnp.zeros_like(gg_ref)

        xv = x_ref[...]
        dhv = dh_ref[...]
        r = lax.rsqrt(jnp.mean(xv * xv, axis=-1, keepdims=True) + EPS)
        n = xv * r
        dn = dhv * g_ref[...]
        dx = r * (dn - n * jnp.mean(dn * n, axis=-1, keepdims=True))
        if has_res:
            dx = dx + dr_ref[...]
        dx_ref[...] = dx
        gg_ref[...] += jnp.sum(dhv * n, axis=0, keepdims=True)

    row_spec = pl.BlockSpec((rows, d), lambda i: (i, 0))
    vec_spec = pl.BlockSpec((1, d), lambda i: (0, 0))
    args = (x, dh) + ((dres,) if has_res else ()) + (g,)
    return pl.pallas_call(
        body, name=name, grid=(t // rows,),
        in_specs=[row_spec, row_spec] + ([row_spec] if has_res else []) + [vec_spec],
        out_specs=[row_spec, vec_spec],
        out_shape=[jax.ShapeDtypeStruct((t, d), F32), jax.ShapeDtypeStruct((1, d), F32)],
        compiler_params=_params(("arbitrary",)),
    )(*args)


def _dh(segs, rows=1024, tk=1024, exchange=None):
    t, d = segs[0][0].shape[0], segs[0][1].shape[1]
    rows = min(rows, t)
    steps = []
    step0 = 0
    for a, _ in segs:
        kb = min(tk, a.shape[1])
        assert a.shape[1] % kb == 0, a.shape
        steps.append((step0, a.shape[1] // kb, kb))
        step0 += a.shape[1] // kb
    n_steps = step0
    ns = len(segs)

    def body(*refs):
        a_refs, w_refs = refs[0:2 * ns:2], refs[1:2 * ns:2]
        dh_ref, acc_ref = refs[2 * ns:]
        k = pl.program_id(1)

        @pl.when(k == 0)
        def _():
            acc_ref[...] = jnp.zeros_like(acc_ref)

        for s, (first, nblk, _) in enumerate(steps):
            @pl.when((k >= first) & (k < first + nblk))
            def _(s=s):
                acc_ref[...] += _dot(a_refs[s][...], w_refs[s][...], NN)

        @pl.when(k == n_steps - 1)
        def _():
            dh_ref[...] = acc_ref[...]

    in_specs, args = [], []
    for (a, w), (first, nblk, kb) in zip(segs, steps):
        blk = lambda k, first=first, nblk=nblk: jnp.clip(k - first, 0, nblk - 1)
        in_specs.append(pl.BlockSpec((rows, kb), lambda i, k, blk=blk: (i, blk(k))))
        in_specs.append(pl.BlockSpec((kb, d), lambda i, k, blk=blk: (blk(k), 0)))
        args += [a, w]
    row_spec = pl.BlockSpec((rows, d), lambda i, k: (i, 0))
    n_tiles = t // rows
    body, ex_in, ex_out, ex_shape, ex_sems = _riding(
        exchange, body, 2 * ns, 1,
        lambda: (pl.program_id(0) == 0) & (pl.program_id(1) == 0),
        lambda: (pl.program_id(0) == n_tiles - 1) & (pl.program_id(1) == n_steps - 1))
    res = pl.pallas_call(
        body, name="dh", grid=(n_tiles, n_steps),
        in_specs=in_specs + ex_in, out_specs=[row_spec] + ex_out,
        out_shape=[jax.ShapeDtypeStruct((t, d), F32)] + ex_shape,
        scratch_shapes=[pltpu.VMEM((rows, d), F32)] + ex_sems,
        compiler_params=_params(("arbitrary", "arbitrary")),
    )(*args, *(exchange["arrays"] if exchange else []))
    return res[0], res[1:]


def _pair_select(lo, m, h0):
    return jnp.where(lo, m[:, h0:h0 + 1], m[:, h0 + 1:h0 + 2])


def _group_select(lo, m, heads):
    return jnp.concatenate([_pair_select(lo, m, h0) for h0 in heads], axis=1)


def _halves(lo, v):
    return (jnp.sum(jnp.where(lo, v, 0.0), axis=1, keepdims=True),
            jnp.sum(jnp.where(lo, 0.0, v), axis=1, keepdims=True))


def _ssd_common(dt_raw, dtb, alog):
    dt = _softplus(dt_raw + dtb)
    aneg = -jnp.exp(alog)
    a_cs = _cumsum_rows(dt * aneg)
    return dt, aneg, a_cs, a_cs.T


def _ssd_specs(nc, rev):
    cidx = (lambda c: nc - 1 - c) if rev else (lambda c: c)
    L = SSD_CHUNK
    b_proj = 2 * SSD_WIDTH // SSD_BC
    b_conv = SSD_WIDTH // SSD_BC
    return dict(
        z=pl.BlockSpec((L, SSD_WIDTH), lambda c: (cidx(c), 0)),
        xr=pl.BlockSpec((L, SSD_WIDTH), lambda c: (cidx(c), 1)),
        br=pl.BlockSpec((L, SSD_BC), lambda c: (cidx(c), b_proj)),
        cr=pl.BlockSpec((L, SSD_BC), lambda c: (cidx(c), b_proj + 1)),
        dt=pl.BlockSpec((L, DT_PAD), lambda c: (cidx(c), 0)),
        cwx=pl.BlockSpec((CONV_WIDTH, SSD_WIDTH), lambda c: (0, 0)),
        cwb=pl.BlockSpec((CONV_WIDTH, SSD_BC), lambda c: (0, b_conv)),
        cwc=pl.BlockSpec((CONV_WIDTH, SSD_BC), lambda c: (0, b_conv + 1)),
        cbx=pl.BlockSpec((1, SSD_WIDTH), lambda c: (0, 0)),
        cbb=pl.BlockSpec((1, SSD_BC), lambda c: (0, b_conv)),
        cbc=pl.BlockSpec((1, SSD_BC), lambda c: (0, b_conv + 1)),
        vec128=pl.BlockSpec((1, LANES), lambda c: (0, 0)),
        vecw=pl.BlockSpec((1, SSD_WIDTH), lambda c: (0, 0)),
        wide=pl.BlockSpec((L, SSD_WIDTH), lambda c: (cidx(c), 0)),
        states=pl.BlockSpec((1, SSD_GROUPS, GROUP_COLS, SSD_STATE), lambda c: (cidx(c), 0, 0, 0)),
    )


def _ssd_fwd(proj_ssd, dt_p, conv_w, conv_b, dtb, alog, d_row, ng_row):
    t = proj_ssd.shape[0]
    nc = t // SSD_CHUNK
    L = SSD_CHUNK
    sp = _ssd_specs(nc, False)

    def body(z_ref, xr_ref, br_ref, cr_ref, dt_ref, cwx_ref, cwb_ref, cwc_ref, cbx_ref, cbb_ref, cbc_ref,
             dtb_ref, alog_ref, d_ref, ng_ref, yssd_ref, y_ref, st_ref, pre_ref,
             hx_ref, hb_ref, hc_ref, state_ref, yacc_ref):
        @pl.when(pl.program_id(0) == 0)
        def _():
            hx_ref[...] = jnp.zeros_like(hx_ref)
            hb_ref[...] = jnp.zeros_like(hb_ref)
            hc_ref[...] = jnp.zeros_like(hc_ref)
            state_ref[...] = jnp.zeros_like(state_ref)

        xr, br, cr = xr_ref[...], br_ref[...], cr_ref[...]
        px = _causal_conv(xr, hx_ref[...], cwx_ref[...], cbx_ref[...])
        pb = _causal_conv(br, hb_ref[...], cwb_ref[...], cbb_ref[...])
        pc = _causal_conv(cr, hc_ref[...], cwc_ref[...], cbc_ref[...])
        hx_ref[...] = xr[L - 8:L, :]
        hb_ref[...] = br[L - 8:L, :]
        hc_ref[...] = cr[L - 8:L, :]
        pre_ref[:, 0:SSD_WIDTH] = px
        pre_ref[:, SSD_WIDTH:SSD_WIDTH + SSD_BC] = pb
        pre_ref[:, SSD_WIDTH + SSD_BC:SSD_CONV_CH] = pc
        xs = px * _sigmoid(px)
        bm = pb * _sigmoid(pb)
        cm = pc * _sigmoid(pc)

        dt, _, a_cs, a_t = _ssd_common(dt_ref[...], dtb_ref[...], alog_ref[...])
        exp_a = jnp.exp(a_cs)
        a_last = a_cs[L - 1:L, :]
        dte = jnp.exp(a_last - a_cs)
        dec = jnp.exp(a_last)

        lane = lax.broadcasted_iota(jnp.int32, (L, LANES), 1)
        sub = lax.broadcasted_iota(jnp.int32, (L, LANES), 0)
        lo = lane < SSD_HEAD_DIM
        causal = sub >= lane
        top = sub < SSD_HEAD_DIM

        for g in range(SSD_GROUPS):
            b_g = bm[:, g * SSD_STATE:(g + 1) * SSD_STATE]
            c_g = cm[:, g * SSD_STATE:(g + 1) * SSD_STATE]
            cb = _dot(c_g, b_g, NT)
            heads = [2 * (g * PAIRS_PER_GROUP + jj) for jj in range(PAIRS_PER_GROUP)]
            gcols = slice(g * GROUP_COLS, (g + 1) * GROUP_COLS)
            xs_g = xs[:, gcols]
            xdt_g = xs_g * _group_select(lo, dt, heads)
            h_g = state_ref[g]
            st_ref[0, g] = h_g
            y_off_g = _dot(c_g, h_g, NT) * _group_select(lo, exp_a, heads)
            s_new_g = _dot(xdt_g * _group_select(lo, dte, heads), b_g, TN)
            for jj, h0 in enumerate(heads):
                blk = slice(jj * LANES, (jj + 1) * LANES)
                cols = slice(g * GROUP_COLS + jj * LANES, g * GROUP_COLS + (jj + 1) * LANES)
                xdt = xdt_g[:, blk]
                g0 = jnp.where(causal, jnp.exp(a_cs[:, h0:h0 + 1] - a_t[h0:h0 + 1, :]), 0.0) * cb
                g1 = jnp.where(causal, jnp.exp(a_cs[:, h0 + 1:h0 + 2] - a_t[h0 + 1:h0 + 2, :]), 0.0) * cb
                lhs = jnp.concatenate([g0, g1], axis=1)
                rhs = jnp.concatenate([jnp.where(lo, xdt, 0.0), jnp.where(lo, 0.0, xdt)], axis=0)
                y_diag = _dot(lhs, rhs, NN)
                dec_rows = jnp.where(top, dec[:, h0:h0 + 1], dec[:, h0 + 1:h0 + 2])
                state_ref[g, blk, :] = h_g[blk, :] * dec_rows + s_new_g[blk, :]
                yacc_ref[:, cols] = (y_diag + y_off_g[:, blk]) + xs_g[:, blk] * d_ref[:, cols]

        y = yacc_ref[...]
        y_ref[...] = y
        zz = z_ref[...]
        y2 = y * (zz * _sigmoid(zz))
        gw = SSD_WIDTH // SSD_GROUPS
        for g in range(SSD_GROUPS):
            seg = y2[:, g * gw:(g + 1) * gw]
            r = lax.rsqrt(jnp.mean(seg * seg, axis=-1, keepdims=True) + EPS)
            yssd_ref[:, g * gw:(g + 1) * gw] = ((seg * r) * ng_ref[:, g * gw:(g + 1) * gw]).astype(yssd_ref.dtype)

    return pl.pallas_call(
        body, name="ssd_fwd", grid=(nc,),
        in_specs=[sp["z"], sp["xr"], sp["br"], sp["cr"], sp["dt"], sp["cwx"], sp["cwb"], sp["cwc"],
                  sp["cbx"], sp["cbb"], sp["cbc"], sp["vec128"], sp["vec128"], sp["vecw"], sp["vecw"]],
        out_specs=[sp["wide"], sp["wide"], sp["states"], pl.BlockSpec((L, SSD_CONV_CH), lambda c: (c, 0))],
        out_shape=[jax.ShapeDtypeStruct((t, SSD_WIDTH), MXU_DTYPE), jax.ShapeDtypeStruct((t, SSD_WIDTH), F32),
                   jax.ShapeDtypeStruct((nc, SSD_GROUPS, GROUP_COLS, SSD_STATE), F32), jax.ShapeDtypeStruct((t, SSD_CONV_CH), F32)],
        scratch_shapes=[pltpu.VMEM((8, SSD_WIDTH), F32), pltpu.VMEM((8, SSD_BC), F32), pltpu.VMEM((8, SSD_BC), F32),
                        pltpu.VMEM((SSD_GROUPS, GROUP_COLS, SSD_STATE), F32), pltpu.VMEM((L, SSD_WIDTH), F32)],
        compiler_params=_params(("arbitrary",)),
    )(proj_ssd, proj_ssd, proj_ssd, proj_ssd, dt_p, conv_w, conv_w, conv_w, conv_b, conv_b, conv_b,
      dtb, alog, d_row, ng_row)


def _ssd_bwd(proj_ssd, pre, dt_p, y, states, dyssd, conv_w, dtb, alog, d_row, ng_row, exchange=None):
    t = proj_ssd.shape[0]
    nc = t // SSD_CHUNK
    L = SSD_CHUNK
    sp = _ssd_specs(nc, True)

    def pre_spec(width, col):
        return pl.BlockSpec((L, width), lambda c: (nc - 1 - c, col))

    def body(z_ref, xr_ref, br_ref, cr_ref, px_ref, pb_ref, pc_ref, dt_ref, y_ref, st_ref, dy_ref,
             cwx_ref, cwb_ref, cwc_ref, dtb_ref, alog_ref, d_ref, ng_ref,
             dssd_ref, ddt_ref, gcw_ref, gcb_ref, gdtb_ref, galog_ref, gd_ref, gng_ref,
             gn_ref, nx_ref, nb_ref, ncc_ref, dxs_ref):
        step = pl.program_id(0)

        @pl.when(step == 0)
        def _():
            gn_ref[...] = jnp.zeros_like(gn_ref)
            nx_ref[...] = jnp.zeros_like(nx_ref)
            nb_ref[...] = jnp.zeros_like(nb_ref)
            ncc_ref[...] = jnp.zeros_like(ncc_ref)
            for ref in (gcw_ref, gcb_ref, gdtb_ref, galog_ref, gd_ref, gng_ref):
                ref[...] = jnp.zeros_like(ref)

        xr, br, cr = xr_ref[...], br_ref[...], cr_ref[...]
        cwx, cwb, cwc = cwx_ref[...], cwb_ref[...], cwc_ref[...]
        px, pb, pc = px_ref[...], pb_ref[...], pc_ref[...]
        sx, sb, sc = _sigmoid(px), _sigmoid(pb), _sigmoid(pc)
        xs, bm, cm = px * sx, pb * sb, pc * sc

        dt_in = dt_ref[...] + dtb_ref[...]
        dt, aneg, a_cs, a_t = _ssd_common(dt_ref[...], dtb_ref[...], alog_ref[...])
        exp_a = jnp.exp(a_cs)
        a_last = a_cs[L - 1:L, :]
        dte = jnp.exp(a_last - a_cs)
        dec = jnp.exp(a_last)

        lane = lax.broadcasted_iota(jnp.int32, (L, LANES), 1)
        sub = lax.broadcasted_iota(jnp.int32, (L, LANES), 0)
        lo = lane < SSD_HEAD_DIM
        causal = sub >= lane
        top = sub < SSD_HEAD_DIM
        last_row = sub == L - 1

        yv = y_ref[...]
        zz = z_ref[...]
        sz = _sigmoid(zz)
        silz = zz * sz
        y2 = yv * silz
        dyv = dy_ref[...]
        gw = SSD_WIDTH // SSD_GROUPS
        d_y2_parts = []
        gng_parts = []
        for g in range(SSD_GROUPS):
            seg = y2[:, g * gw:(g + 1) * gw]
            dseg = dyv[:, g * gw:(g + 1) * gw]
            r = lax.rsqrt(jnp.mean(seg * seg, axis=-1, keepdims=True) + EPS)
            n = seg * r
            dn = dseg * ng_ref[:, g * gw:(g + 1) * gw]
            gng_parts.append(jnp.sum(dseg * n, axis=0, keepdims=True))
            d_y2_parts.append(r * (dn - n * jnp.mean(dn * n, axis=-1, keepdims=True)))
        d_y2 = jnp.concatenate(d_y2_parts, axis=1)
        gng_ref[...] += jnp.concatenate(gng_parts, axis=1)
        d_y = d_y2 * silz
        dssd_ref[:, 0:SSD_WIDTH] = (d_y2 * yv * (sz * (1.0 + zz * (1.0 - sz)))).astype(dssd_ref.dtype)
        gd_ref[...] += jnp.sum(d_y * xs, axis=0, keepdims=True)
        dxs_ref[...] = d_y * d_ref[...]

        d_a = jnp.zeros((L, LANES), F32)
        d_at = jnp.zeros((LANES, L), F32)
        ddt = jnp.zeros((L, LANES), F32)
        d_b_parts, d_c_parts = [], []
        for g in range(SSD_GROUPS):
            b_g = bm[:, g * SSD_STATE:(g + 1) * SSD_STATE]
            c_g = cm[:, g * SSD_STATE:(g + 1) * SSD_STATE]
            cb = _dot(c_g, b_g, NT)
            d_cb = jnp.zeros((L, L), F32)
            heads = [2 * (g * PAIRS_PER_GROUP + jj) for jj in range(PAIRS_PER_GROUP)]
            gcols = slice(g * GROUP_COLS, (g + 1) * GROUP_COLS)
            dy_g, xs_g = d_y[:, gcols], xs[:, gcols]
            dt_g = _group_select(lo, dt, heads)
            expa_g = _group_select(lo, exp_a, heads)
            dte_g = _group_select(lo, dte, heads)
            xdt_g = xs_g * dt_g
            h_g = st_ref[0, g]
            gn_g = gn_ref[g]
            dys_g = dy_g * expa_g
            d_cg = _dot(dys_g, h_g, NN)
            d_h_g = _dot(dys_g, c_g, TN)
            t1_g = dy_g * _dot(c_g, h_g, NT) * expa_g
            d_bg = _dot(xdt_g * dte_g, gn_g, NN)
            dxdt_g = _dot(b_g, gn_g, NT) * dte_g
            t2_g = dxdt_g * xdt_g
            t12_g = t1_g - t2_g
            gh_g = jnp.sum(gn_g * h_g, axis=1, keepdims=True)
            for jj, h0 in enumerate(heads):
                blk = slice(jj * LANES, (jj + 1) * LANES)
                cols = slice(g * GROUP_COLS + jj * LANES, g * GROUP_COLS + (jj + 1) * LANES)
                dy_p, xs_p, xdt, dt_pp = dy_g[:, blk], xs_g[:, blk], xdt_g[:, blk], dt_g[:, blk]
                l0 = jnp.where(causal, jnp.exp(a_cs[:, h0:h0 + 1] - a_t[h0:h0 + 1, :]), 0.0)
                l1 = jnp.where(causal, jnp.exp(a_cs[:, h0 + 1:h0 + 2] - a_t[h0 + 1:h0 + 2, :]), 0.0)
                g0, g1 = l0 * cb, l1 * cb
                dcat = jnp.concatenate([jnp.where(lo, dy_p, 0.0), jnp.where(lo, 0.0, dy_p)], axis=0)
                d_xdt = dxdt_g[:, blk] + _dot(jnp.concatenate([g0, g1], axis=0), dcat, TN)
                dm = _dot(dcat, xdt, NT)
                dm0, dm1 = dm[0:L], dm[L:2 * L]
                d_cb = d_cb + (l0 * dm0 + l1 * dm1)
                e0, e1 = dm0 * g0, dm1 * g1
                a0, a1 = _halves(lo, t12_g[:, blk])
                a0 = a0 + jnp.sum(e0, axis=1, keepdims=True)
                a1 = a1 + jnp.sum(e1, axis=1, keepdims=True)
                s0, s1 = _halves(lo, t2_g[:, blk])
                gh = gh_g[blk, :]
                dd0 = jnp.sum(jnp.where(top[:, 0:1], gh, 0.0), axis=0, keepdims=True)
                dd1 = jnp.sum(jnp.where(top[:, 0:1], 0.0, gh), axis=0, keepdims=True)
                end0 = jnp.sum(s0, axis=0, keepdims=True) + dd0 * dec[:, h0:h0 + 1]
                end1 = jnp.sum(s1, axis=0, keepdims=True) + dd1 * dec[:, h0 + 1:h0 + 2]
                d_a = d_a + jnp.where(lane == h0, a0 + jnp.where(last_row, end0, 0.0), 0.0)
                d_a = d_a + jnp.where(lane == h0 + 1, a1 + jnp.where(last_row, end1, 0.0), 0.0)
                d_at = d_at - jnp.where(sub == h0, jnp.sum(e0, axis=0, keepdims=True), 0.0)
                d_at = d_at - jnp.where(sub == h0 + 1, jnp.sum(e1, axis=0, keepdims=True), 0.0)
                dec_rows = jnp.where(top, dec[:, h0:h0 + 1], dec[:, h0 + 1:h0 + 2])
                gn_ref[g, blk, :] = d_h_g[blk, :] + dec_rows * gn_g[blk, :]
                q0, q1 = _halves(lo, d_xdt * xs_p)
                ddt = ddt + jnp.where(lane == h0, q0, 0.0) + jnp.where(lane == h0 + 1, q1, 0.0)
                dxs_ref[:, cols] += d_xdt * dt_pp
            d_cg = d_cg + _dot(d_cb, b_g, NN)
            d_bg = d_bg + _dot(d_cb, c_g, TN)
            d_b_parts.append(d_bg)
            d_c_parts.append(d_cg)

        rc = _rev_cumsum_rows(d_a + d_at.T)
        d_dt = rc * aneg + ddt
        galog_ref[...] += jnp.sum(rc * dt, axis=0, keepdims=True) * aneg
        d_dtraw = d_dt * _sigmoid(dt_in)
        gdtb_ref[...] += jnp.sum(d_dtraw, axis=0, keepdims=True)
        ddt_ref[...] = d_dtraw.astype(ddt_ref.dtype)

        def dsilu(p, s):
            return s * (1.0 + p * (1.0 - s))

        dcx = dxs_ref[...] * dsilu(px, sx)
        dcb = jnp.concatenate(d_b_parts, axis=1) * dsilu(pb, sb)
        dcc = jnp.concatenate(d_c_parts, axis=1) * dsilu(pc, sc)
        drx, gwx, gbx = _conv_backward(dcx, nx_ref[...], xr, cwx)
        drb, gwb, gbb = _conv_backward(dcb, nb_ref[...], br, cwb)
        drc, gwc, gbc = _conv_backward(dcc, ncc_ref[...], cr, cwc)
        nx_ref[...] = dcx[0:8, :]
        nb_ref[...] = dcb[0:8, :]
        ncc_ref[...] = dcc[0:8, :]
        dssd_ref[:, SSD_WIDTH:2 * SSD_WIDTH] = drx.astype(dssd_ref.dtype)
        dssd_ref[:, 2 * SSD_WIDTH:2 * SSD_WIDTH + SSD_BC] = drb.astype(dssd_ref.dtype)
        dssd_ref[:, 2 * SSD_WIDTH + SSD_BC:SSD_SEG] = drc.astype(dssd_ref.dtype)
        for k in range(CONV_WIDTH):
            gcw_ref[k:k + 1, :] += jnp.concatenate([gwx[k], gwb[k], gwc[k]], axis=1)
        gcb_ref[...] += jnp.concatenate([gbx, gbb, gbc], axis=1)

    const = lambda shape: pl.BlockSpec(shape, lambda c: (0,) * len(shape))
    body, ex_in, ex_out, ex_shape, ex_sems = _riding(
        exchange, body, 18, 8, lambda: pl.program_id(0) == 0, lambda: pl.program_id(0) == nc - 1)
    res = pl.pallas_call(
        body, name="ssd_bwd", grid=(nc,),
        in_specs=[sp["z"], sp["xr"], sp["br"], sp["cr"], pre_spec(SSD_WIDTH, 0), pre_spec(SSD_BC, SSD_WIDTH // SSD_BC), pre_spec(SSD_BC, SSD_WIDTH // SSD_BC + 1),
                  sp["dt"], sp["wide"], sp["states"], sp["wide"],
                  sp["cwx"], sp["cwb"], sp["cwc"], sp["vec128"], sp["vec128"], sp["vecw"], sp["vecw"]] + ex_in,
        out_specs=[pl.BlockSpec((L, SSD_SEG), lambda c: (nc - 1 - c, 0)), sp["dt"],
                   const((CONV_WIDTH, SSD_CONV_CH)), const((1, SSD_CONV_CH)), const((1, LANES)), const((1, LANES)),
                   const((1, SSD_WIDTH)), const((1, SSD_WIDTH))] + ex_out,
        out_shape=[jax.ShapeDtypeStruct((t, SSD_SEG), MXU_DTYPE), jax.ShapeDtypeStruct((t, DT_PAD), MXU_DTYPE),
                   jax.ShapeDtypeStruct((CONV_WIDTH, SSD_CONV_CH), F32), jax.ShapeDtypeStruct((1, SSD_CONV_CH), F32),
                   jax.ShapeDtypeStruct((1, LANES), F32), jax.ShapeDtypeStruct((1, LANES), F32),
                   jax.ShapeDtypeStruct((1, SSD_WIDTH), F32), jax.ShapeDtypeStruct((1, SSD_WIDTH), F32)] + ex_shape,
        scratch_shapes=[pltpu.VMEM((SSD_GROUPS, GROUP_COLS, SSD_STATE), F32), pltpu.VMEM((8, SSD_WIDTH), F32),
                        pltpu.VMEM((8, SSD_BC), F32), pltpu.VMEM((8, SSD_BC), F32), pltpu.VMEM((L, SSD_WIDTH), F32)] + ex_sems,
        compiler_params=_params(("arbitrary",)),
    )(proj_ssd, proj_ssd, proj_ssd, proj_ssd, pre, pre, pre, dt_p, y, states, dyssd,
      conv_w, conv_w, conv_w, dtb, alog, d_row, ng_row, *(exchange["arrays"] if exchange else []))
    return res[:8], res[8:]


def _lru_gates(xl, wa_ref, wx_ref, ba, bx, lam):
    pre_a, pre_x = [], []
    for g in range(LRU_NGROUPS):
        xg = xl[:, g * LRU_GROUP:(g + 1) * LRU_GROUP]
        pre_a.append(_dot(xg, wa_ref[g], NN))
        pre_x.append(_dot(xg, wx_ref[g], NN))
    r = _sigmoid(jnp.concatenate(pre_a, axis=1) + ba)
    i = _sigmoid(jnp.concatenate(pre_x, axis=1) + bx)
    log_a = (-LRU_C * r) * _softplus(-lam)
    a = jnp.exp(log_a)
    mult = jnp.sqrt(-jnp.tanh(log_a) * (a * a + 1.0))
    return r, i, log_a, a, mult


def _scan_rows(p, u, carry, reverse):
    rows, w = p.shape
    groups = rows // 8
    p3, u3 = p.reshape(groups, 8, w), u.reshape(groups, 8, w)
    row = lax.broadcasted_iota(jnp.int32, (groups, 8, w), 1)
    for s in (1, 2, 4):
        ok = row < 8 - s if reverse else row >= s
        shift = 8 - s if reverse else s
        u3 = p3 * jnp.where(ok, pltpu.roll(u3, shift, 1), 0.0) + u3
        p3 = p3 * jnp.where(ok, pltpu.roll(p3, shift, 1), 1.0)
    out = [None] * groups
    for k in (range(groups - 1, -1, -1) if reverse else range(groups)):
        out[k] = p3[k] * carry + u3[k]
        carry = out[k][0:1, :] if reverse else out[k][7:8, :]
    return jnp.concatenate(out, axis=0), carry


def _lru_fwd(proj_lru, conv_w, conv_b, wa, wx, ba, bx, lam):
    t = proj_lru.shape[0]
    rows = min(LRU_ROWS, t)
    nb = t // rows
    W = LRU_WIDTH

    def body(lg_ref, lx_ref, cw_ref, cb_ref, wa_ref, wx_ref, ba_ref, bx_ref, lam_ref, ylru_ref, h_ref, xl_ref,
             halo_ref, carry_ref):
        @pl.when(pl.program_id(0) == 0)
        def _():
            halo_ref[...] = jnp.zeros_like(halo_ref)
            carry_ref[...] = jnp.zeros_like(carry_ref)

        lx = lx_ref[...]
        xl = _causal_conv(lx, halo_ref[...], cw_ref[...], cb_ref[...])
        halo_ref[...] = lx[rows - 8:rows, :]
        xl_ref[...] = xl
        _, i, _, a, mult = _lru_gates(xl, wa_ref, wx_ref, ba_ref[...], bx_ref[...], lam_ref[...])
        u = mult * (i * xl)
        h, carry_ref[...] = _scan_rows(a, u, carry_ref[...], False)
        h_ref[...] = h
        lg = lg_ref[...]
        ylru_ref[...] = (h * (lg * _sigmoid(lg))).astype(ylru_ref.dtype)

    const = lambda shape: pl.BlockSpec(shape, lambda b: (0,) * len(shape))
    return pl.pallas_call(
        body, name="lru_fwd", grid=(nb,),
        in_specs=[pl.BlockSpec((rows, W), lambda b: (b, 0)), pl.BlockSpec((rows, W), lambda b: (b, 1)),
                  const((CONV_WIDTH, W)), const((1, W)), const((LRU_NGROUPS, LRU_GROUP, LRU_GROUP)),
                  const((LRU_NGROUPS, LRU_GROUP, LRU_GROUP)), const((1, W)), const((1, W)), const((1, W))],
        out_specs=[pl.BlockSpec((rows, W), lambda b: (b, 0))] * 3,
        out_shape=[jax.ShapeDtypeStruct((t, W), MXU_DTYPE), jax.ShapeDtypeStruct((t, W), F32), jax.ShapeDtypeStruct((t, W), F32)],
        scratch_shapes=[pltpu.VMEM((8, W), F32), pltpu.VMEM((1, W), F32)],
        compiler_params=_params(("arbitrary",)),
    )(proj_lru, proj_lru, conv_w, conv_b, wa, wx, ba, bx, lam)


def _lru_bwd(proj_lru, xl, h, dylru, conv_w, wa, wx, ba, bx, lam, exchange=None):
    t = proj_lru.shape[0]
    rows = min(LRU_ROWS, t)
    nb = t // rows
    W = LRU_WIDTH
    groups8 = rows // 8

    def rev(b):
        return nb - 1 - b

    def halo_spec(col):
        return pl.BlockSpec((8, W), lambda b: (jnp.maximum(rev(b) * groups8 - 1, 0), col))

    def body(lg_ref, lx_ref, xl_ref, h_ref, hh_ref, dy_ref, cw_ref, wa_ref, wx_ref, ba_ref, bx_ref, lam_ref,
             dlru_ref, gcw_ref, gcb_ref, gba_ref, gbx_ref, glam_ref, gwa_ref, gwx_ref,
             gcarry_ref, afirst_ref, nxt_ref):
        step = pl.program_id(0)

        @pl.when(step == 0)
        def _():
            gcarry_ref[...] = jnp.zeros_like(gcarry_ref)
            afirst_ref[...] = jnp.zeros_like(afirst_ref)
            nxt_ref[...] = jnp.zeros_like(nxt_ref)
            for ref in (gcw_ref, gcb_ref, gba_ref, gbx_ref, glam_ref, gwa_ref, gwx_ref):
                ref[...] = jnp.zeros_like(ref)

        keep = jnp.where(step == nb - 1, 0.0, 1.0)
        lx = lx_ref[...]
        cw = cw_ref[...]
        xl = xl_ref[...]
        lam = lam_ref[...]
        r, i, log_a, a, mult = _lru_gates(xl, wa_ref, wx_ref, ba_ref[...], bx_ref[...], lam)
        hv = h_ref[...]
        h_prev = _shift_down(hv, hh_ref[...] * keep, 1)
        lg = lg_ref[...]
        sg = _sigmoid(lg)
        dyv = dy_ref[...]
        d_h = dyv * (lg * sg)
        dlru_ref[:, 0:W] = (dyv * hv * (sg * (1.0 + lg * (1.0 - sg)))).astype(dlru_ref.dtype)

        row = lax.broadcasted_iota(jnp.int32, (rows, W), 0)
        p = jnp.where(row < rows - 1, pltpu.roll(a, rows - 1, 0), afirst_ref[...])
        gsc, gcarry_ref[...] = _scan_rows(p, d_h, gcarry_ref[...], True)
        afirst_ref[...] = a[0:1, :]

        d_a = gsc * h_prev
        v = i * xl
        d_mult = gsc * v
        d_v = gsc * mult
        d_i = d_v * xl
        d_xl = d_v * i
        d_la = d_a * a - d_mult * (a * a) / mult
        sp_neg = _softplus(-lam)
        d_r = d_la * (-LRU_C * sp_neg)
        glam_ref[...] += jnp.sum(d_la * r, axis=0, keepdims=True) * (LRU_C * _sigmoid(-lam))
        d_pa = d_r * r * (1.0 - r)
        d_px = d_i * i * (1.0 - i)
        gba_ref[...] += jnp.sum(d_pa, axis=0, keepdims=True)
        gbx_ref[...] += jnp.sum(d_px, axis=0, keepdims=True)
        parts = []
        for g in range(LRU_NGROUPS):
            cols = slice(g * LRU_GROUP, (g + 1) * LRU_GROUP)
            xg, dpa_g, dpx_g = xl[:, cols], d_pa[:, cols], d_px[:, cols]
            parts.append(_dot(dpa_g, wa_ref[g], NT) + _dot(dpx_g, wx_ref[g], NT))
            gwa_ref[g] += _dot(xg, dpa_g, TN)
            gwx_ref[g] += _dot(xg, dpx_g, TN)
        d_xl = d_xl + jnp.concatenate(parts, axis=1)
        d_lx, gw, gb = _conv_backward(d_xl, nxt_ref[...], lx, cw)
        nxt_ref[...] = d_xl[0:8, :]
        dlru_ref[:, W:2 * W] = d_lx.astype(dlru_ref.dtype)
        for k in range(CONV_WIDTH):
            gcw_ref[k:k + 1, :] += gw[k]
        gcb_ref[...] += gb

    const = lambda shape: pl.BlockSpec(shape, lambda b: (0,) * len(shape))
    wspec = const((LRU_NGROUPS, LRU_GROUP, LRU_GROUP))
    blk = lambda col: pl.BlockSpec((rows, W), lambda b: (rev(b), col))
    body, ex_in, ex_out, ex_shape, ex_sems = _riding(
        exchange, body, 12, 8, lambda: pl.program_id(0) == 0, lambda: pl.program_id(0) == nb - 1)
    res = pl.pallas_call(
        body, name="lru_bwd", grid=(nb,),
        in_specs=[blk(0), blk(1), blk(0), blk(0), halo_spec(0), blk(0),
                  const((CONV_WIDTH, W)), wspec, wspec, const((1, W)), const((1, W)), const((1, W))] + ex_in,
        out_specs=[pl.BlockSpec((rows, 2 * W), lambda b: (rev(b), 0)), const((CONV_WIDTH, W)), const((1, W)),
                   const((1, W)), const((1, W)), const((1, W)), wspec, wspec] + ex_out,
        out_shape=[jax.ShapeDtypeStruct((t, 2 * W), MXU_DTYPE), jax.ShapeDtypeStruct((CONV_WIDTH, W), F32),
                   jax.ShapeDtypeStruct((1, W), F32), jax.ShapeDtypeStruct((1, W), F32), jax.ShapeDtypeStruct((1, W), F32),
                   jax.ShapeDtypeStruct((1, W), F32), jax.ShapeDtypeStruct((LRU_NGROUPS, LRU_GROUP, LRU_GROUP), F32),
                   jax.ShapeDtypeStruct((LRU_NGROUPS, LRU_GROUP, LRU_GROUP), F32)] + ex_shape,
        scratch_shapes=[pltpu.VMEM((1, W), F32), pltpu.VMEM((1, W), F32), pltpu.VMEM((8, W), F32)] + ex_sems,
        compiler_params=_params(("arbitrary",)),
    )(proj_lru, proj_lru, xl, h, h, dylru, conv_w, wa, wx, ba, bx, lam, *(exchange["arrays"] if exchange else []))
    return res[:8], res[8:]


def _mem_scores(q_h, k_h):
    s = _dot(q_h, k_h, NT) * (MEM_HEAD_DIM ** -0.5)
    s = s - jnp.max(s, axis=-1, keepdims=True)
    e = jnp.exp(s)
    return e / jnp.sum(e, axis=-1, keepdims=True)


def _mem_fwd(q, kv, rows=512):
    t = q.shape[0]
    rows = min(rows, t)
    m = kv.shape[0]

    def body(q_ref, kv_ref, y_ref):
        for hd in range(MEM_HEADS):
            cols = slice(hd * MEM_HEAD_DIM, (hd + 1) * MEM_HEAD_DIM)
            vcols = slice(D_MODEL + hd * MEM_HEAD_DIM, D_MODEL + (hd + 1) * MEM_HEAD_DIM)
            p = _mem_scores(q_ref[:, cols], kv_ref[:, cols])
            y_ref[:, cols] = _dot(p, kv_ref[:, vcols], NN).astype(y_ref.dtype)

    return pl.pallas_call(
        body, name="mem_fwd", grid=(t // rows,),
        in_specs=[pl.BlockSpec((rows, D_MODEL), lambda i: (i, 0)), pl.BlockSpec((m, 2 * D_MODEL), lambda i: (0, 0))],
        out_specs=pl.BlockSpec((rows, D_MODEL), lambda i: (i, 0)),
        out_shape=jax.ShapeDtypeStruct((t, D_MODEL), MXU_DTYPE),
        compiler_params=_params(("parallel",)),
    )(q, kv)


def _mem_bwd(q, kv, dy, rows=512):
    t = q.shape[0]
    rows = min(rows, t)
    m = kv.shape[0]

    def body(q_ref, kv_ref, dy_ref, dq_ref, dkv_ref):
        @pl.when(pl.program_id(0) == 0)
        def _():
            dkv_ref[...] = jnp.zeros_like(dkv_ref)

        for hd in range(MEM_HEADS):
            cols = slice(hd * MEM_HEAD_DIM, (hd + 1) * MEM_HEAD_DIM)
            vcols = slice(D_MODEL + hd * MEM_HEAD_DIM, D_MODEL + (hd + 1) * MEM_HEAD_DIM)
            q_h, k_h, dy_h = q_ref[:, cols], kv_ref[:, cols], dy_ref[:, cols]
            p = _mem_scores(q_h, k_h)
            dp = _dot(dy_h, kv_ref[:, vcols], NT)
            dkv_ref[:, vcols] += _dot(p, dy_h, TN)
            ds = p * (dp - jnp.sum(dp * p, axis=-1, keepdims=True)) * (MEM_HEAD_DIM ** -0.5)
            dq_ref[:, cols] = _dot(ds, k_h, NN).astype(dq_ref.dtype)
            dkv_ref[:, cols] += _dot(ds, q_h, TN)

    return pl.pallas_call(
        body, name="mem_bwd", grid=(t // rows,),
        in_specs=[pl.BlockSpec((rows, D_MODEL), lambda i: (i, 0)), pl.BlockSpec((m, 2 * D_MODEL), lambda i: (0, 0)),
                  pl.BlockSpec((rows, D_MODEL), lambda i: (i, 0))],
        out_specs=[pl.BlockSpec((rows, D_MODEL), lambda i: (i, 0)), pl.BlockSpec((m, 2 * D_MODEL), lambda i: (0, 0))],
        out_shape=[jax.ShapeDtypeStruct((t, D_MODEL), MXU_DTYPE), jax.ShapeDtypeStruct((m, 2 * D_MODEL), F32)],
        compiler_params=_params(("arbitrary",)),
    )(q, kv, dy)


def _merge_fwd(x, yssd, ylru, ymem, gl, w_bs, w_bl, w_bm, w_out, fg, tgt, rows=256):
    t = x.shape[0]
    rows = min(rows, t)
    D = D_MODEL

    def body(x_ref, ys_ref, yl_ref, ym_ref, gl_ref, wbs_ref, wbl_ref, wbm_ref, wo_ref, fg_ref, tgt_ref,
             ps_ref, pl_ref, pm_ref, mg_ref, dx2_ref, loss_ref, gfg_ref):
        @pl.when(pl.program_id(0) == 0)
        def _():
            loss_ref[...] = jnp.zeros_like(loss_ref)
            gfg_ref[...] = jnp.zeros_like(gfg_ref)

        ps = _dot(ys_ref[...], wbs_ref[...], NN)
        pl_ = _dot(yl_ref[...], wbl_ref[...], NN)
        pm = _dot(ym_ref[...], wbm_ref[...], NN)
        ps_ref[...] = ps
        pl_ref[...] = pl_
        pm_ref[...] = pm
        merged = (_sigmoid(gl_ref[:, 0:D]) * ps + _sigmoid(gl_ref[:, D:2 * D]) * pl_) + _sigmoid(gl_ref[:, 2 * D:3 * D]) * pm
        mg_ref[...] = merged.astype(mg_ref.dtype)
        x2 = x_ref[...] + _dot(merged, wo_ref[...], NN)
        r2 = lax.rsqrt(jnp.mean(x2 * x2, axis=-1, keepdims=True) + EPS)
        xn = x2 * r2
        fg = fg_ref[...]
        diff = xn * fg - tgt_ref[...]
        tile_loss = 0.5 * jnp.sum(jnp.mean(diff * diff, axis=-1, keepdims=True), axis=0, keepdims=True)
        loss_ref[...] += jnp.broadcast_to(tile_loss, loss_ref.shape)
        d_out = diff * (1.0 / D)
        gfg_ref[...] += jnp.sum(d_out * xn, axis=0, keepdims=True)
        dxn = d_out * fg
        dx2_ref[...] = r2 * (dxn - xn * jnp.mean(dxn * xn, axis=-1, keepdims=True))

    row = lambda w: pl.BlockSpec((rows, w), lambda i: (i, 0))
    const = lambda shape: pl.BlockSpec(shape, lambda i: (0,) * len(shape))
    return pl.pallas_call(
        body, name="merge_fwd", grid=(t // rows,),
        in_specs=[row(D), row(SSD_WIDTH), row(LRU_WIDTH), row(D), row(3 * D), const((SSD_WIDTH, D)), const((LRU_WIDTH, D)),
                  const((D, D)), const((D, D)), const((1, D)), row(D)],
        out_specs=[row(D), row(D), row(D), row(D), row(D), const((1, LANES)), const((1, D))],
        out_shape=[jax.ShapeDtypeStruct((t, D), F32), jax.ShapeDtypeStruct((t, D), F32), jax.ShapeDtypeStruct((t, D), F32),
                   jax.ShapeDtypeStruct((t, D), MXU_DTYPE), jax.ShapeDtypeStruct((t, D), F32),
                   jax.ShapeDtypeStruct((1, LANES), F32), jax.ShapeDtypeStruct((1, D), F32)],
        compiler_params=_params(("arbitrary",)),
    )(x, yssd, ylru, ymem, gl, w_bs, w_bl, w_bm, w_out, fg, tgt)


def _merge_bwd(dx2, gl, ps, pl_in, pm, w_bs, w_bl, w_bm, w_out, rows=256):
    t = dx2.shape[0]
    rows = min(rows, t)
    D = D_MODEL

    def body(dx2_ref, gl_ref, ps_ref, pl_ref, pm_ref, wbs_ref, wbl_ref, wbm_ref, wo_ref,
             dg_ref, dps_ref, dpl_ref, dpm_ref, dys_ref, dyl_ref, dym_ref):
        dm = _dot(dx2_ref[...], wo_ref[...], NT)
        for idx, (p_ref, dp_ref, w_ref, dy_ref) in enumerate(
                ((ps_ref, dps_ref, wbs_ref, dys_ref), (pl_ref, dpl_ref, wbl_ref, dyl_ref), (pm_ref, dpm_ref, wbm_ref, dym_ref))):
            gate = _sigmoid(gl_ref[:, idx * D:(idx + 1) * D])
            dg_ref[:, idx * D:(idx + 1) * D] = ((dm * p_ref[...]) * gate * (1.0 - gate)).astype(dg_ref.dtype)
            dp = dm * gate
            dp_ref[...] = dp.astype(dp_ref.dtype)
            dy_ref[...] = _dot(dp, w_ref[...], NT)

    row = lambda w: pl.BlockSpec((rows, w), lambda i: (i, 0))
    const = lambda shape: pl.BlockSpec(shape, lambda i: (0,) * len(shape))
    return pl.pallas_call(
        body, name="merge_bwd", grid=(t // rows,),
        in_specs=[row(D), row(3 * D), row(D), row(D), row(D), const((SSD_WIDTH, D)), const((LRU_WIDTH, D)),
                  const((D, D)), const((D, D))],
        out_specs=[row(3 * D), row(D), row(D), row(D), row(SSD_WIDTH), row(LRU_WIDTH), row(D)],
        out_shape=[jax.ShapeDtypeStruct((t, 3 * D), MXU_DTYPE), jax.ShapeDtypeStruct((t, D), MXU_DTYPE),
                   jax.ShapeDtypeStruct((t, D), MXU_DTYPE), jax.ShapeDtypeStruct((t, D), MXU_DTYPE),
                   jax.ShapeDtypeStruct((t, SSD_WIDTH), F32), jax.ShapeDtypeStruct((t, LRU_WIDTH), F32),
                   jax.ShapeDtypeStruct((t, D), F32)],
        compiler_params=_params(("parallel",)),
    )(dx2, gl, ps, pl_in, pm, w_bs, w_bl, w_bm, w_out)


def _mesh_place():
    x, y, c = lax.axis_index("x"), lax.axis_index("y"), lax.axis_index("c")
    return x, y, c, 4 * x + 2 * y + c


def _other_chips(x, y):
    return [(1 - x, y), (x, 1 - y), (1 - x, 1 - y)]


def _all_gather_plan(arrs):
    n = len(arrs)

    def parts(ins, outs, send_sems, recv_sems, local_sems):
        x, y, c, me = _mesh_place()
        sibling = (x, y, 1 - c)
        chips = _other_chips(x, y)

        def slot(px, py, pc):
            return 4 * px + 2 * py + pc

        def copy(a, k, block, to, src=None):
            return pltpu.make_async_remote_copy(
                src_ref=outs[a].at[block] if src is None else src, dst_ref=outs[a].at[block],
                send_sem=send_sems.at[a, k], recv_sem=recv_sems.at[a, k], device_id=to, device_id_type=pl.DeviceIdType.MESH)

        def local():
            return [pltpu.make_async_copy(ins[a], outs[a].at[me], local_sems.at[a]) for a in range(n)]

        def first():
            return [copy(a, k, me, to, src=ins[a]) for a in range(n)
                    for k, to in enumerate([sibling] + [(*chip, c) for chip in chips])]

        return x, y, c, sibling, chips, slot, copy, local, first

    def start(ins, outs, *sems):
        *_, local, first = parts(ins, outs, *sems)
        for cp in local() + first():
            cp.start()

    def wait(ins, outs, *sems):
        x, y, c, sibling, chips, slot, copy, local, first = parts(ins, outs, *sems)
        sends = first()
        for j, chip in enumerate(chips):
            for a in range(n):
                copy(a, 1 + j, slot(*chip, c), sibling).wait_recv()
                passed = copy(a, 4 + j, slot(*chip, c), sibling)
                passed.start()
                sends.append(passed)
        for a in range(n):
            copy(a, 0, slot(x, y, 1 - c), sibling).wait_recv()
        for j, chip in enumerate(chips):
            for a in range(n):
                copy(a, 4 + j, slot(*chip, 1 - c), sibling).wait_recv()
        for cp in sends:
            cp.wait_send()
        for cp in local():
            cp.wait()

    return dict(arrays=list(arrs), out_shape=[jax.ShapeDtypeStruct((N_DEV,) + a.shape, a.dtype) for a in arrs],
                sems=[(n, 7), (n, 7), (n,)], start=start, wait=wait)


def _all_gather(arrs, name):
    return _run_exchange(_all_gather_plan(arrs), name)


N_CHIPS = 4


def _pair_plan(parts):
    n = len(parts)

    def copies(ins, outs, send_sems, recv_sems):
        x, y, c, _ = _mesh_place()
        return [pltpu.make_async_remote_copy(src_ref=ins[a].at[q, 1 - c], dst_ref=outs[a].at[q], send_sem=send_sems.at[a, q],
                                             recv_sem=recv_sems.at[a, q], device_id=(x, y, 1 - c), device_id_type=pl.DeviceIdType.MESH)
                for a in range(n) for q in range(N_CHIPS)]

    def start(ins, outs, send_sems, recv_sems):
        for cp in copies(ins, outs, send_sems, recv_sems):
            cp.start()

    def wait(ins, outs, send_sems, recv_sems):
        cps = copies(ins, outs, send_sems, recv_sems)
        for cp in cps:
            cp.wait_recv()
        for cp in cps:
            cp.wait_send()

    return dict(arrays=list(parts), out_shape=[jax.ShapeDtypeStruct((N_CHIPS,) + a.shape[2:], a.dtype) for a in parts],
                sems=[(n, N_CHIPS), (n, N_CHIPS)], start=start, wait=wait)


def _chip_plan(sums):
    n = len(sums)

    def copies(ins, outs, send_sems, recv_sems, arriving):
        x, y, c, _ = _mesh_place()
        my_chip = 2 * x + y
        cps = []
        for a in range(n):
            for j, (px, py) in enumerate(_other_chips(x, y)):
                src, dst = (my_chip, 2 * px + py) if arriving else (2 * px + py, my_chip)
                cps.append(pltpu.make_async_remote_copy(
                    src_ref=ins[a].at[src], dst_ref=outs[a].at[dst], send_sem=send_sems.at[a, j], recv_sem=recv_sems.at[a, j],
                    device_id=(px, py, c), device_id_type=pl.DeviceIdType.MESH))
        return cps

    def start(ins, outs, send_sems, recv_sems):
        for cp in copies(ins, outs, send_sems, recv_sems, False):
            cp.start()

    def wait(ins, outs, send_sems, recv_sems):
        for cp in copies(ins, outs, send_sems, recv_sems, True):
            cp.wait_recv()
        for cp in copies(ins, outs, send_sems, recv_sems, False):
            cp.wait_send()

    return dict(arrays=list(sums), out_shape=[jax.ShapeDtypeStruct(a.shape, a.dtype) for a in sums],
                sems=[(n, 3), (n, 3)], start=start, wait=wait)


def _both(p1, p2):
    n1, s1 = len(p1["arrays"]), len(p1["sems"])

    def each(method):
        def run(ins, outs, *sems):
            p1[method](ins[:n1], outs[:n1], *sems[:s1])
            p2[method](ins[n1:], outs[n1:], *sems[s1:])
        return run

    return dict(arrays=p1["arrays"] + p2["arrays"], out_shape=p1["out_shape"] + p2["out_shape"],
                sems=p1["sems"] + p2["sems"], start=each("start"), wait=each("wait"))


def _run_exchange(plan, name):
    n = len(plan["arrays"])

    def body(*refs):
        ins, outs, sems = refs[:n], refs[n:2 * n], refs[2 * n:]
        plan["start"](ins, outs, *sems)
        plan["wait"](ins, outs, *sems)

    any_spec = pl.BlockSpec(memory_space=pl.ANY)
    return pl.pallas_call(
        body, name=name, in_specs=[any_spec] * n, out_specs=[any_spec] * n, out_shape=plan["out_shape"],
        scratch_shapes=[pltpu.SemaphoreType.DMA(shape) for shape in plan["sems"]],
    )(*plan["arrays"])


def _riding(plan, body, n_in, n_out, first, last):
    if plan is None:
        return body, [], [], [], []
    ne = len(plan["arrays"])

    def wrapped(*refs):
        ins, ex_in = refs[:n_in], refs[n_in:n_in + ne]
        outs = refs[n_in + ne:n_in + ne + n_out]
        ex_out = refs[n_in + ne + n_out:n_in + 2 * ne + n_out]
        n_sems = len(plan["sems"])
        scratch, sems = refs[n_in + 2 * ne + n_out:-n_sems], refs[-n_sems:]

        @pl.when(first())
        def _():
            plan["start"](ex_in, ex_out, *sems)

        body(*ins, *outs, *scratch)

        @pl.when(last())
        def _():
            plan["wait"](ex_in, ex_out, *sems)

    any_spec = pl.BlockSpec(memory_space=pl.ANY)
    sems = [pltpu.SemaphoreType.DMA(shape) for shape in plan["sems"]]
    return wrapped, [any_spec] * ne, [any_spec] * ne, plan["out_shape"], sems


def _col_tile(r, c, limit_bytes):
    assert c % LANES == 0, c
    best = LANES
    for cand in range(LANES, c + 1, LANES):
        if c % cand == 0 and r * cand * 4 <= limit_bytes:
            best = cand
    return best


def _chip_sum(part, recv, core, name):
    _, _, r, c = part.shape
    ct = _col_tile(r, c, 2 << 20)

    def body(core_ref, p_ref, r_ref, s_ref, t_ref):
        s = p_ref[...] + r_ref[...]
        s_ref[...] = s
        t_ref[...] = s.astype(t_ref.dtype)

    blk = pl.BlockSpec((None, r, ct), lambda q, i, core_ref: (q, 0, i))
    return pl.pallas_call(
        body, name=name,
        grid_spec=pltpu.PrefetchScalarGridSpec(
            num_scalar_prefetch=1, grid=(N_CHIPS, c // ct),
            in_specs=[pl.BlockSpec((None, None, r, ct), lambda q, i, core_ref: (q, core_ref[0], 0, i)), blk],
            out_specs=[blk, blk]),
        out_shape=[jax.ShapeDtypeStruct((N_CHIPS, r, c), F32), jax.ShapeDtypeStruct((N_CHIPS, r, c), GRAD_WIRE_DTYPE)],
        compiler_params=_params(("parallel", "parallel")),
    )(core, part, recv)


def _adam_update(w, g, m, v):
    nm = ADAM_B1 * m + (1.0 - ADAM_B1) * g
    nv = ADAM_B2 * v + (1.0 - ADAM_B2) * (g * g)
    m_hat = nm / (1.0 - ADAM_B1 ** ADAM_STEP)
    v_hat = nv / (1.0 - ADAM_B2 ** ADAM_STEP)
    return -ADAM_LR * (m_hat / (jnp.sqrt(v_hat) + ADAM_EPS) + ADAM_WD * w), nm, nv


def _sum_adamw(own, recv, chip, w, m, v, name, exchange=None):
    _, r, c = own.shape
    ct = _col_tile(r, c, 1 << 20)

    def body(chip_ref, o_ref, r1_ref, r2_ref, r3_ref, w_ref, m_ref, v_ref, g_ref, d_ref, nm_ref, nv_ref):
        g = ((o_ref[...] + r1_ref[...].astype(F32)) + r2_ref[...].astype(F32)) + r3_ref[...].astype(F32)
        g_ref[...] = g
        d_ref[...], nm_ref[...], nv_ref[...] = _adam_update(w_ref[...], g, m_ref[...], v_ref[...])

    def slot(k):
        return pl.BlockSpec((None, r, ct), lambda i, chip_ref: ((chip_ref[0] + k) % N_CHIPS, 0, i))

    spec = pl.BlockSpec((r, ct), lambda i, chip_ref: (0, i))
    shape = jax.ShapeDtypeStruct((r, c), F32)
    n_tiles = c // ct
    body, ex_in, ex_out, ex_shape, ex_sems = _riding(
        exchange, body, 8, 4, lambda: pl.program_id(0) == 0, lambda: pl.program_id(0) == n_tiles - 1)
    res = pl.pallas_call(
        body, name=name,
        grid_spec=pltpu.PrefetchScalarGridSpec(
            num_scalar_prefetch=1, grid=(n_tiles,),
            in_specs=[slot(0), slot(1), slot(2), slot(3), spec, spec, spec] + ex_in, out_specs=[spec] * 4 + ex_out,
            scratch_shapes=ex_sems),
        out_shape=[shape] * 4 + ex_shape,
        compiler_params=_params(("arbitrary",)),
    )(chip, own, recv, recv, recv, w, m, v, *(exchange["arrays"] if exchange else []))
    return res[:4], res[4:]


def _small_adamw(parts, ws, ms, vs):
    n = len(parts)

    def body(*refs):
        p_refs, w_refs, m_refs, v_refs = refs[:n], refs[n:2 * n], refs[2 * n:3 * n], refs[3 * n:4 * n]
        outs = refs[4 * n:]
        for i in range(n):
            g = p_refs[i][0]
            for k in range(1, N_DEV):
                g = g + p_refs[i][k]
            outs[i][...] = g
            outs[n + i][...], outs[2 * n + i][...], outs[3 * n + i][...] = _adam_update(
                w_refs[i][...], g, m_refs[i][...], v_refs[i][...])

    vmem = pl.BlockSpec(memory_space=pltpu.VMEM)
    shapes = [jax.ShapeDtypeStruct(w.shape, F32) for w in ws]
    res = pl.pallas_call(
        body, name="adamw_small", in_specs=[vmem] * (4 * n), out_specs=[vmem] * (4 * n), out_shape=shapes * 4,
        compiler_params=pltpu.CompilerParams(vmem_limit_bytes=VMEM_LIMIT),
    )(*parts, *ws, *ms, *vs)
    return res[:n], res[n:2 * n], res[2 * n:3 * n], res[3 * n:]


def _pack(arrs, dtype, row_multiple):
    flat = jnp.concatenate([a.reshape(-1).astype(dtype) for a in arrs])
    unit = LANES * row_multiple
    padded = -(-flat.shape[0] // unit) * unit
    return jnp.pad(flat, (0, padded - flat.shape[0])).reshape(-1, LANES)


def _unpack(packed, shapes, lead=()):
    flat = packed.reshape(lead + (-1,))
    out, off = [], 0
    for shp in shapes:
        n = math.prod(shp)
        out.append(flat[..., off:off + n].reshape(lead + tuple(shp)))
        off += n
    return out


def _gather_cols(g, lo, hi):
    width = g.shape[2]
    pieces = []
    for s in range(N_DEV):
        a, e = max(lo, s * width), min(hi, (s + 1) * width)
        if a < e:
            pieces.append(g[s, :, a - s * width:e - s * width])
    return pieces[0] if len(pieces) == 1 else jnp.concatenate(pieces, axis=1)


def _scatter_cols(segs, width):
    slots = []
    for k in range(N_DEV):
        lo, hi = k * width, (k + 1) * width
        pieces = []
        for arr, s_lo, s_hi in segs:
            a, e = max(lo, s_lo), min(hi, s_hi)
            if a < e:
                pieces.append(arr[:, a - s_lo:e - s_lo])
        slots.append(pieces[0] if len(pieces) == 1 else jnp.concatenate(pieces, axis=1))
    return jnp.stack(slots)


def _block_diag_groups(w):
    w4 = w.reshape(LRU_NGROUPS, 4, LRU_BLOCK, LRU_BLOCK)
    eye = jnp.eye(4, dtype=w.dtype)
    return jnp.einsum("gaij,ab->gaibj", w4, eye).reshape(LRU_NGROUPS, LRU_GROUP, LRU_GROUP)


def _block_diag_extract(wg):
    w5 = wg.reshape(LRU_NGROUPS, 4, LRU_BLOCK, 4, LRU_BLOCK)
    idx = jnp.arange(4)
    return w5[:, idx, :, idx, :].transpose(1, 0, 2, 3).reshape(LRU_BLOCKS, LRU_BLOCK, LRU_BLOCK)


BIG = ("w_in", "w_kv", "w_br_ssd", "w_br_lru", "w_br_mem", "w_out")
SMALL_SHARDED = ("ssd_conv_w", "ssd_norm_g", "lru_conv_w")
REPLICATED = ("norm_g", "ssd_conv_b", "ssd_dt_bias", "ssd_a_log", "ssd_d", "lru_conv_b", "lru_w_a", "lru_b_a",
              "lru_w_x", "lru_b_x", "lru_lambda", "mem_norm_g", "final_g")
WEIGHTS = ("norm_g", "w_in", "ssd_conv_w", "ssd_conv_b", "ssd_dt_bias", "ssd_a_log", "ssd_d", "ssd_norm_g", "lru_conv_w",
           "lru_conv_b", "lru_w_a", "lru_b_a", "lru_w_x", "lru_b_x", "lru_lambda", "mem_norm_g", "w_kv", "w_br_ssd",
           "w_br_lru", "w_br_mem", "w_out", "final_g")


def kernel(x, mem, norm_g, w_in, ssd_conv_w, ssd_conv_b, ssd_dt_bias, ssd_a_log, ssd_d, ssd_norm_g, lru_conv_w, lru_conv_b, lru_w_a, lru_b_a, lru_w_x, lru_b_x, lru_lambda, mem_norm_g, w_kv, w_br_ssd, w_br_lru, w_br_mem, w_out, final_g, loss_target, m_norm_g, m_w_in, m_ssd_conv_w, m_ssd_conv_b, m_ssd_dt_bias, m_ssd_a_log, m_ssd_d, m_ssd_norm_g, m_lru_conv_w, m_lru_conv_b, m_lru_w_a, m_lru_b_a, m_lru_w_x, m_lru_b_x, m_lru_lambda, m_mem_norm_g, m_w_kv, m_w_br_ssd, m_w_br_lru, m_w_br_mem, m_w_out, m_final_g, v_norm_g, v_w_in, v_ssd_conv_w, v_ssd_conv_b, v_ssd_dt_bias, v_ssd_a_log, v_ssd_d, v_ssd_norm_g, v_lru_conv_w, v_lru_conv_b, v_lru_w_a, v_lru_b_a, v_lru_w_x, v_lru_b_x, v_lru_lambda, v_mem_norm_g, v_w_kv, v_w_br_ssd, v_w_br_lru, v_w_br_mem, v_w_out, v_final_g):
    env = dict(locals())
    W = {n: env[n] for n in WEIGHTS}
    M = {n: env["m_" + n] for n in WEIGHTS}
    V = {n: env["v_" + n] for n in WEIGHTS}
    me = 4 * lax.axis_index("x") + 2 * lax.axis_index("y") + lax.axis_index("c")
    t = x.shape[1]
    xt = x[0]
    memt = mem[0]
    tgt = loss_target[0]

    small_shapes = [W[n].shape for n in SMALL_SHARDED]
    as2d = lambda d, n: jnp.transpose(d[n][0]) if n == "w_in" else d[n][0]
    (g_in,) = _all_gather([as2d(W, "w_in").astype(MXU_DTYPE)], "w_in_all_gather")
    b = SEG_BOUNDS
    w_in_t = g_in.reshape(IN_WIDTH, D_MODEL)
    w_ssd, w_lru, w_q, w_g = w_in_t[b[0]:b[1]], w_in_t[b[2]:b[3]], w_in_t[b[3]:b[4]], w_in_t[b[4]:b[5]]
    w_dt = jnp.pad(w_in_t[b[1]:b[2]], ((0, DT_PAD - SSD_HEADS), (0, 0)))

    h = _rms_fwd(xt, norm_g, "norm_fwd")
    later = _all_gather_plan([as2d(W, n).astype(MXU_DTYPE) for n in BIG[1:]] + [_pack([W[n] for n in SMALL_SHARDED], F32, 8)])
    proj_ssd, (g_kv, g_bs, g_bl, g_bm, g_out, gs) = _matmul(h, w_ssd, "nt", "proj_ssd", tm=4096, tn=512, exchange=later)
    g_cw, g_ng, g_lcw = _unpack(gs, small_shapes, (N_DEV,))
    cols = lambda a: jnp.moveaxis(a[:, 0], 0, -2).reshape(a.shape[2:-1] + (-1,))
    rows_ = lambda a: a.reshape((-1,) + a.shape[2:])
    w_bs_f, w_bl_f, w_bm_f, w_out_f = rows_(g_bs), rows_(g_bl), rows_(g_bm), rows_(g_out)
    conv_w_f, ssd_ng_f, lru_cw_f = cols(g_cw), cols(g_ng), cols(g_lcw)
    w_kv_f = _gather_cols(g_kv, 0, 2 * D_MODEL)

    pad_heads = lambda a: jnp.pad(a, ((0, 0), (0, LANES - SSD_HEADS)))
    dtb, alog = pad_heads(ssd_dt_bias), pad_heads(ssd_a_log)
    d_row = jnp.repeat(ssd_d, SSD_HEAD_DIM, axis=1)
    ng_row = ssd_ng_f.reshape(1, SSD_WIDTH)
    wa_g, wx_g = _block_diag_groups(lru_w_a[0]), _block_diag_groups(lru_w_x[0])
    ba, bx = lru_b_a.reshape(1, LRU_WIDTH), lru_b_x.reshape(1, LRU_WIDTH)
    fg = final_g.reshape(1, D_MODEL)

    proj_lru = _matmul(h, w_lru, "nt", "proj_lru", tm=4096, tn=512)
    proj_q = _matmul(h, w_q, "nt", "proj_q", tm=4096, tn=512)
    proj_g = _matmul(h, w_g, "nt", "proj_g", tm=4096, tn=512)
    proj_dt = _matmul(h, w_dt, "nt", "proj_dt", tm=4096)
    mem_n = _rms_fwd(memt, mem_norm_g, "mem_norm_fwd")
    kv = _matmul(mem_n, w_kv_f, "nn", "mem_kv")
    yssd, y_scan, states, ssd_pre = _ssd_fwd(proj_ssd, proj_dt, conv_w_f, ssd_conv_b, dtb, alog, d_row, ng_row)
    ylru, h_lru, xl_lru = _lru_fwd(proj_lru, lru_cw_f, lru_conv_b, wa_g, wx_g, ba, bx, lru_lambda)
    ymem = _mem_fwd(proj_q, kv)
    ps, pl_, pm, merged, dx2, loss_vec, g_fg = _merge_fwd(xt, yssd, ylru, ymem, proj_g, w_bs_f, w_bl_f, w_bm_f, w_out_f, fg, tgt)

    d_g, dps, dpl, dpm, dyssd, dylru, dymem = _merge_bwd(dx2, proj_g, ps, pl_, pm, w_bs_f, w_bl_f, w_bm_f, w_out_f)
    gw_out = _matmul(merged, dx2, "tn", "grad_w_out", tk=2048)
    gw_bs = _matmul(yssd, dps, "tn", "grad_w_br_ssd", tm=2048)
    gw_bl = _matmul(ylru, dpl, "tn", "grad_w_br_lru", tm=LRU_WIDTH, tk=2048)
    gw_bm = _matmul(ymem, dpm, "tn", "grad_w_br_mem", tk=2048)
    d_q, d_kv = _mem_bwd(proj_q, kv, dymem)
    gw_kv = _matmul(mem_n, d_kv, "tn", "grad_w_kv")
    d_memn = _matmul(d_kv, w_kv_f, "nt", "d_mem_n")
    _, g_memng = _rms_bwd(memt, d_memn, None, mem_norm_g, "mem_norm_bwd")

    core = lax.axis_index("c").astype(jnp.int32).reshape(1)
    chip = (2 * lax.axis_index("x") + lax.axis_index("y")).astype(jnp.int32).reshape(1)
    by_chip = lambda a: a.reshape((N_CHIPS, 2, -1) + a.shape[1:])
    early = ("w_kv", "w_br_ssd", "w_br_lru", "w_br_mem", "w_out")
    early_parts = [by_chip(_scatter_cols([(gw_kv, 0, 2 * D_MODEL)], 2 * D_MODEL // N_DEV).reshape(-1, 2 * D_MODEL // N_DEV)),
                   by_chip(gw_bs), by_chip(gw_bl), by_chip(gw_bm), by_chip(gw_out)]
    (d_lru, gl_cw, gl_cb, g_ba, g_bx, g_lam, gwa_g, gwx_g), early_sib = _lru_bwd(
        proj_lru, xl_lru, h_lru, dylru, lru_cw_f, wa_g, wx_g, ba, bx, lru_lambda, exchange=_pair_plan(early_parts))
    early_sums = [_chip_sum(p, r, core, "chip_sum_" + n) for n, p, r in zip(early, early_parts, early_sib)]
    (d_ssd, d_dt, gs_cw, gs_cb, g_dtb, g_alog, g_dch, g_ngrow), early_recv = _ssd_bwd(
        proj_ssd, ssd_pre, proj_dt, y_scan, states, dyssd, conv_w_f, dtb, alog, d_row, ng_row,
        exchange=_chip_plan([s16 for _, s16 in early_sums]))
    gw_ssd = _matmul(d_ssd, h, "tn", "grad_w_in_ssd", tm=2560)
    gw_lru = _matmul(d_lru, h, "tn", "grad_w_in_lru", tm=1536, tk=2048)
    gw_q = _matmul(d_q, h, "tn", "grad_w_in_q", tk=2048)
    gw_g = _matmul(d_g, h, "tn", "grad_w_in_g", tm=1536, tk=2048)
    gw_dt = _matmul(d_dt, h, "tn", "grad_w_in_dt", tk=2048)
    in_part = by_chip(jnp.concatenate([gw_ssd, gw_dt[:SSD_HEADS], gw_lru, gw_q, gw_g], axis=0))
    (in_sib,) = _run_exchange(_pair_plan([in_part]), "grad_pair_exchange")
    in_sum = _chip_sum(in_part, in_sib, core, "chip_sum_w_in")

    small_grads = {
        "ssd_conv_w": gs_cw, "ssd_conv_b": gs_cb, "ssd_dt_bias": g_dtb[:, :SSD_HEADS],
        "ssd_a_log": g_alog[:, :SSD_HEADS], "ssd_d": jnp.sum(g_dch.reshape(SSD_HEADS, SSD_HEAD_DIM), axis=1).reshape(1, SSD_HEADS),
        "ssd_norm_g": g_ngrow.reshape(SSD_GROUPS, -1), "lru_conv_w": gl_cw, "lru_conv_b": gl_cb,
        "lru_w_a": _block_diag_extract(gwa_g), "lru_b_a": g_ba, "lru_w_x": _block_diag_extract(gwx_g), "lru_b_x": g_bx,
        "lru_lambda": g_lam, "mem_norm_g": g_memng, "final_g": g_fg,
    }
    small_all = REPLICATED + SMALL_SHARDED

    def small_shape(n, shards):
        shp = W[n].shape[1:] if W[n].ndim > 2 else (1, W[n].shape[-1])
        return shp[:-1] + (shp[-1] * shards,)

    riders = tuple(small_grads)
    small_plan = _all_gather_plan([small_grads[n].reshape(small_shape(n, N_DEV if n in SMALL_SHARDED else 1)) for n in riders])
    dh, landed = _dh([(d_ssd, w_ssd), (d_lru, w_lru), (d_q, w_q), (d_g, w_g), (d_dt, w_dt)],
                     exchange=_both(_chip_plan([in_sum[1]]), small_plan))
    grad_x, g_normg = _rms_bwd(xt, dh, dx2, norm_g, "norm_bwd")
    small_recv = dict(zip(riders, landed[1:]))
    reduced = {"w_in": (in_sum[0], landed[0]), **{n: (s[0], r) for n, s, r in zip(early, early_sums, early_recv)}}

    grads, delta, new_m, new_v = {}, {}, {}, {}
    for n in BIG:
        s32, recv = reduced[n]
        res, landed = _sum_adamw(s32, recv, chip, as2d(W, n), as2d(M, n), as2d(V, n), "adamw_" + n,
                                 exchange=_all_gather_plan([g_normg]) if n == "w_in" else None)
        if n == "w_in":
            small_recv["norm_g"] = landed[0]
        for dst, a in zip((grads, delta, new_m, new_v), res):
            dst[n] = (jnp.transpose(a) if n == "w_in" else a)[None]

    parts = []
    for n in small_all:
        a = small_recv[n]
        if n in SMALL_SHARDED:
            width = W[n].shape[-1]
            a = lax.dynamic_slice_in_dim(a, me * width, width, axis=a.ndim - 1)
        parts.append(a)
    canon = lambda d: [d[n].reshape(small_shape(n, 1)) for n in small_all]
    for dst, res in zip((grads, delta, new_m, new_v), _small_adamw(parts, canon(W), canon(M), canon(V))):
        for n, a in zip(small_all, res):
            dst[n] = a.reshape(W[n].shape)

    loss = lax.psum(loss_vec[0, 0], ("x", "y", "c"))
    return (loss, grad_x[None], *[grads[n] for n in WEIGHTS], *[delta[n] for n in WEIGHTS],
            *[new_m[n] for n in WEIGHTS], *[new_v[n] for n in WEIGHTS])
```

```python
import functools
import math

import jax
import jax.numpy as jnp
from jax import lax
from jax.experimental import pallas as pl
from jax.experimental.pallas import tpu as pltpu

F32 = jnp.float32
MXU_DTYPE = jnp.bfloat16
GRAD_WIRE_DTYPE = jnp.bfloat16

D_MODEL = 1024
EPS = 1e-6
CONV_WIDTH = 4
SSD_WIDTH = 2048
SSD_HEAD_DIM = 64
SSD_HEADS = 32
SSD_GROUPS = 4
SSD_STATE = 128
SSD_CHUNK = 128
SSD_BC = SSD_GROUPS * SSD_STATE
SSD_CONV_CH = SSD_WIDTH + 2 * SSD_BC
SSD_PAIRS = SSD_HEADS // 2
PAIRS_PER_GROUP = SSD_PAIRS // SSD_GROUPS
GROUP_COLS = SSD_WIDTH // SSD_GROUPS
LRU_WIDTH = 1536
LRU_BLOCKS = 16
LRU_BLOCK = 96
LRU_GROUP = 4 * LRU_BLOCK
LRU_NGROUPS = LRU_WIDTH // LRU_GROUP
LRU_C = 8.0
LRU_ROWS = 256
MEM_HEADS = 4
MEM_HEAD_DIM = 256
IN_WIDTH = 12320
N_DEV = 8
LANES = 128
SSD_SEG = SSD_WIDTH + SSD_CONV_CH
DT_PAD = LANES
SEG_BOUNDS = (0, 5120, 5152, 8224, 9248, 12320)

ADAM_LR = 0.001
ADAM_B1 = 0.9
ADAM_B2 = 0.999
ADAM_EPS = 1e-08
ADAM_WD = 0.01
ADAM_STEP = 10

VMEM_LIMIT = 56 * 1024 * 1024

NN = (((1,), (0,)), ((), ()))
NT = (((1,), (1,)), ((), ()))
TN = (((0,), (0,)), ((), ()))


def _dot(a, b, dims):
    return lax.dot_general(a.astype(MXU_DTYPE), b.astype(MXU_DTYPE), dims, preferred_element_type=F32)


def _sigmoid(x):
    return 0.5 * jnp.tanh(0.5 * x) + 0.5


def _log1p(e):
    u = 1.0 + e
    return jnp.where(u == 1.0, e, jnp.log(u) * (e / jnp.where(u == 1.0, 1.0, u - 1.0)))


def _softplus(x):
    return jnp.maximum(x, 0.0) + _log1p(jnp.exp(-jnp.abs(x)))


def _params(semantics):
    return pltpu.CompilerParams(dimension_semantics=semantics, vmem_limit_bytes=VMEM_LIMIT)


def _shift_down(cur, halo8, k):
    rolled = pltpu.roll(cur, k, 0)
    row8 = lax.broadcasted_iota(jnp.int32, halo8.shape, 0)
    top = jnp.where(row8 >= k, rolled[0:8], pltpu.roll(halo8, k, 0))
    return jnp.concatenate([top, rolled[8:]], axis=0)


def _shift_up(cur, next8, k):
    rows = cur.shape[0]
    rolled = pltpu.roll(cur, rows - k, 0)
    row8 = lax.broadcasted_iota(jnp.int32, next8.shape, 0)
    bot = jnp.where(row8 < 8 - k, rolled[rows - 8:rows], pltpu.roll(next8, 8 - k, 0))
    return jnp.concatenate([rolled[:rows - 8], bot], axis=0)


def _causal_conv(raw, halo8, w, b):
    acc = raw * w[3:4, :] + b
    for k in range(1, CONV_WIDTH):
        acc = acc + _shift_down(raw, halo8, k) * w[3 - k:4 - k, :]
    return acc


def _conv_backward(dco, next8, raw, w):
    d_raw = dco * w[3:4, :]
    gw = [None] * CONV_WIDTH
    gw[3] = jnp.sum(dco * raw, axis=0, keepdims=True)
    for j in range(1, CONV_WIDTH):
        up = _shift_up(dco, next8, j)
        d_raw = d_raw + up * w[3 - j:4 - j, :]
        gw[3 - j] = jnp.sum(up * raw, axis=0, keepdims=True)
    gb = jnp.sum(dco, axis=0, keepdims=True)
    return d_raw, gw, gb


def _cumsum_rows(v):
    rows = v.shape[0]
    row = lax.broadcasted_iota(jnp.int32, v.shape, 0)
    s = 1
    while s < rows:
        v = v + jnp.where(row >= s, pltpu.roll(v, s, 0), 0.0)
        s *= 2
    return v


def _rev_cumsum_rows(v):
    rows = v.shape[0]
    row = lax.broadcasted_iota(jnp.int32, v.shape, 0)
    s = 1
    while s < rows:
        v = v + jnp.where(row < rows - s, pltpu.roll(v, rows - s, 0), 0.0)
        s *= 2
    return v


def _matmul(a, b, mode, name, tm=1024, tn=1024, tk=1024, exchange=None):
    if mode == "nn":
        (m, kk), n = a.shape, b.shape[1]
    elif mode == "nt":
        (m, kk), n = a.shape, b.shape[0]
    else:
        (kk, m), n = a.shape, b.shape[1]
    tm, tn, tk = min(tm, m), min(tn, n), min(tk, kk)
    assert m % tm == 0 and n % tn == 0 and kk % tk == 0, (name, a.shape, b.shape)
    nk = kk // tk
    dims = {"nn": NN, "nt": NT, "tn": TN}[mode]
    a_spec = pl.BlockSpec((tk, tm), lambda i, j, k: (k, i)) if mode == "tn" else pl.BlockSpec((tm, tk), lambda i, j, k: (i, k))
    b_spec = pl.BlockSpec((tn, tk), lambda i, j, k: (j, k)) if mode == "nt" else pl.BlockSpec((tk, tn), lambda i, j, k: (k, j))
    o_spec = pl.BlockSpec((tm, tn), lambda i, j, k: (i, j))

    def body_single(a_ref, b_ref, o_ref):
        o_ref[...] = _dot(a_ref[...], b_ref[...], dims)

    def body(a_ref, b_ref, o_ref, acc_ref):
        k = pl.program_id(2)

        @pl.when(k == 0)
        def _():
            acc_ref[...] = jnp.zeros_like(acc_ref)

        acc_ref[...] += _dot(a_ref[...], b_ref[...], dims)

        @pl.when(k == nk - 1)
        def _():
            o_ref[...] = acc_ref[...]

    grid = (m // tm, n // tn, nk)
    if exchange is None:
        return pl.pallas_call(
            body_single if nk == 1 else body, name=name, grid=grid, in_specs=[a_spec, b_spec], out_specs=o_spec,
            out_shape=jax.ShapeDtypeStruct((m, n), F32),
            scratch_shapes=[] if nk == 1 else [pltpu.VMEM((tm, tn), F32)],
            compiler_params=_params(("parallel", "parallel", "arbitrary")),
        )(a, b)
    at = lambda ids: functools.reduce(lambda u, v: u & v, [pl.program_id(d) == ids[d] for d in range(3)])
    riding, ex_in, ex_out, ex_shape, ex_sems = _riding(
        exchange, body_single if nk == 1 else body, 2, 1, lambda: at((0, 0, 0)), lambda: at(tuple(g - 1 for g in grid)))
    res = pl.pallas_call(
        riding, name=name, grid=grid, in_specs=[a_spec, b_spec] + ex_in, out_specs=[o_spec] + ex_out,
        out_shape=[jax.ShapeDtypeStruct((m, n), F32)] + ex_shape,
        scratch_shapes=([] if nk == 1 else [pltpu.VMEM((tm, tn), F32)]) + ex_sems,
        compiler_params=_params(("arbitrary", "arbitrary", "arbitrary")),
    )(a, b, *exchange["arrays"])
    return res[0], res[1:]


def _rms_fwd(x, g, name, rows=512, exchange=None):
    t, d = x.shape
    rows = min(rows, t)
    n_tiles = t // rows

    def body(x_ref, g_ref, h_ref):
        xv = x_ref[...]
        r = lax.rsqrt(jnp.mean(xv * xv, axis=-1, keepdims=True) + EPS)
        h_ref[...] = ((xv * r) * g_ref[...]).astype(h_ref.dtype)

    body, ex_in, ex_out, ex_shape, ex_sems = _riding(
        exchange, body, 2, 1, lambda: pl.program_id(0) == 0, lambda: pl.program_id(0) == n_tiles - 1)
    res = pl.pallas_call(
        body, name=name, grid=(n_tiles,),
        in_specs=[pl.BlockSpec((rows, d), lambda i: (i, 0)), pl.BlockSpec((1, d), lambda i: (0, 0))] + ex_in,
        out_specs=[pl.BlockSpec((rows, d), lambda i: (i, 0))] + ex_out,
        out_shape=[jax.ShapeDtypeStruct((t, d), MXU_DTYPE)] + ex_shape,
        scratch_shapes=ex_sems,
        compiler_params=_params(("arbitrary",) if exchange else ("parallel",)),
    )(x, g, *(exchange["arrays"] if exchange else []))
    return (res[0], res[1:]) if exchange else res[0]


def _rms_bwd(x, dh, dres, g, name, rows=512):
    t, d = x.shape
    rows = min(rows, t)
    has_res = dres is not None

    def body(*refs):
        if has_res:
            x_ref, dh_ref, dr_ref, g_ref, dx_ref, gg_ref = refs
        else:
            x_ref, dh_ref, g_ref, dx_ref, gg_ref = refs

        @pl.when(pl.program_id(0) == 0)
        def _():
            gg_ref[...] = jnp.zeros_like(gg_ref)

        xv = x_ref[...]
        dhv = dh_ref[...]
        r = lax.rsqrt(jnp.mean(xv * xv, axis=-1, keepdims=True) + EPS)
        n = xv * r
        dn = dhv * g_ref[...]
        dx = r * (dn - n * jnp.mean(dn * n, axis=-1, keepdims=True))
        if has_res:
            dx = dx + dr_ref[...]
        dx_ref[...] = dx
        gg_ref[...] += jnp.sum(dhv * n, axis=0, keepdims=True)

    row_spec = pl.BlockSpec((rows, d), lambda i: (i, 0))
    vec_spec = pl.BlockSpec((1, d), lambda i: (0, 0))
    args = (x, dh) + ((dres,) if has_res else ()) + (g,)
    return pl.pallas_call(
        body, name=name, grid=(t // rows,),
        in_specs=[row_spec, row_spec] + ([row_spec] if has_res else []) + [vec_spec],
        out_specs=[row_spec, vec_spec],
        out_shape=[jax.ShapeDtypeStruct((t, d), F32), jax.ShapeDtypeStruct((1, d), F32)],
        compiler_params=_params(("arbitrary",)),
    )(*args)


def _dh(segs, rows=1024, tk=1024, exchange=None):
    t, d = segs[0][0].shape[0], segs[0][1].shape[1]
    rows = min(rows, t)
    steps = []
    step0 = 0
    for a, _ in segs:
        kb = min(tk, a.shape[1])
        assert a.shape[1] % kb == 0, a.shape
        steps.append((step0, a.shape[1] // kb, kb))
        step0 += a.shape[1] // kb
    n_steps = step0
    ns = len(segs)

    def body(*refs):
        a_refs, w_refs = refs[0:2 * ns:2], refs[1:2 * ns:2]
        dh_ref, acc_ref = refs[2 * ns:]
        k = pl.program_id(1)

        @pl.when(k == 0)
        def _():
            acc_ref[...] = jnp.zeros_like(acc_ref)

        for s, (first, nblk, _) in enumerate(steps):
            @pl.when((k >= first) & (k < first + nblk))
            def _(s=s):
                acc_ref[...] += _dot(a_refs[s][...], w_refs[s][...], NN)

        @pl.when(k == n_steps - 1)
        def _():
            dh_ref[...] = acc_ref[...]

    in_specs, args = [], []
    for (a, w), (first, nblk, kb) in zip(segs, steps):
        blk = lambda k, first=first, nblk=nblk: jnp.clip(k - first, 0, nblk - 1)
        in_specs.append(pl.BlockSpec((rows, kb), lambda i, k, blk=blk: (i, blk(k))))
        in_specs.append(pl.BlockSpec((kb, d), lambda i, k, blk=blk: (blk(k), 0)))
        args += [a, w]
    row_spec = pl.BlockSpec((rows, d), lambda i, k: (i, 0))
    n_tiles = t // rows
    body, ex_in, ex_out, ex_shape, ex_sems = _riding(
        exchange, body, 2 * ns, 1,
        lambda: (pl.program_id(0) == 0) & (pl.program_id(1) == 0),
        lambda: (pl.program_id(0) == n_tiles - 1) & (pl.program_id(1) == n_steps - 1))
    res = pl.pallas_call(
        body, name="dh", grid=(n_tiles, n_steps),
        in_specs=in_specs + ex_in, out_specs=[row_spec] + ex_out,
        out_shape=[jax.ShapeDtypeStruct((t, d), F32)] + ex_shape,
        scratch_shapes=[pltpu.VMEM((rows, d), F32)] + ex_sems,
        compiler_params=_params(("arbitrary", "arbitrary")),
    )(*args, *(exchange["arrays"] if exchange else []))
    return res[0], res[1:]


def _pair_select(lo, m, h0):
    return jnp.where(lo, m[:, h0:h0 + 1], m[:, h0 + 1:h0 + 2])


def _group_select(lo, m, heads):
    return jnp.concatenate([_pair_select(lo, m, h0) for h0 in heads], axis=1)


def _halves(lo, v):
    return (jnp.sum(jnp.where(lo, v, 0.0), axis=1, keepdims=True),
            jnp.sum(jnp.where(lo, 0.0, v), axis=1, keepdims=True))


def _ssd_common(dt_raw, dtb, alog):
    dt = _softplus(dt_raw + dtb)
    aneg = -jnp.exp(alog)
    a_cs = _cumsum_rows(dt * aneg)
    return dt, aneg, a_cs, a_cs.T


def _ssd_specs(nc, rev):
    cidx = (lambda c: nc - 1 - c) if rev else (lambda c: c)
    L = SSD_CHUNK
    b_proj = 2 * SSD_WIDTH // SSD_BC
    b_conv = SSD_WIDTH // SSD_BC
    return dict(
        z=pl.BlockSpec((L, SSD_WIDTH), lambda c: (cidx(c), 0)),
        xr=pl.BlockSpec((L, SSD_WIDTH), lambda c: (cidx(c), 1)),
        br=pl.BlockSpec((L, SSD_BC), lambda c: (cidx(c), b_proj)),
        cr=pl.BlockSpec((L, SSD_BC), lambda c: (cidx(c), b_proj + 1)),
        dt=pl.BlockSpec((L, DT_PAD), lambda c: (cidx(c), 0)),
        cwx=pl.BlockSpec((CONV_WIDTH, SSD_WIDTH), lambda c: (0, 0)),
        cwb=pl.BlockSpec((CONV_WIDTH, SSD_BC), lambda c: (0, b_conv)),
        cwc=pl.BlockSpec((CONV_WIDTH, SSD_BC), lambda c: (0, b_conv + 1)),
        cbx=pl.BlockSpec((1, SSD_WIDTH), lambda c: (0, 0)),
        cbb=pl.BlockSpec((1, SSD_BC), lambda c: (0, b_conv)),
        cbc=pl.BlockSpec((1, SSD_BC), lambda c: (0, b_conv + 1)),
        vec128=pl.BlockSpec((1, LANES), lambda c: (0, 0)),
        vecw=pl.BlockSpec((1, SSD_WIDTH), lambda c: (0, 0)),
        wide=pl.BlockSpec((L, SSD_WIDTH), lambda c: (cidx(c), 0)),
        states=pl.BlockSpec((1, SSD_GROUPS, GROUP_COLS, SSD_STATE), lambda c: (cidx(c), 0, 0, 0)),
    )


def _ssd_fwd(proj_ssd, dt_p, conv_w, conv_b, dtb, alog, d_row, ng_row):
    t = proj_ssd.shape[0]
    nc = t // SSD_CHUNK
    L = SSD_CHUNK
    sp = _ssd_specs(nc, False)

    def body(z_ref, xr_ref, br_ref, cr_ref, dt_ref, cwx_ref, cwb_ref, cwc_ref, cbx_ref, cbb_ref, cbc_ref,
             dtb_ref, alog_ref, d_ref, ng_ref, yssd_ref, y_ref, st_ref, pre_ref,
             hx_ref, hb_ref, hc_ref, state_ref, yacc_ref):
        @pl.when(pl.program_id(0) == 0)
        def _():
            hx_ref[...] = jnp.zeros_like(hx_ref)
            hb_ref[...] = jnp.zeros_like(hb_ref)
            hc_ref[...] = jnp.zeros_like(hc_ref)
            state_ref[...] = jnp.zeros_like(state_ref)

        xr, br, cr = xr_ref[...], br_ref[...], cr_ref[...]
        px = _causal_conv(xr, hx_ref[...], cwx_ref[...], cbx_ref[...])
        pb = _causal_conv(br, hb_ref[...], cwb_ref[...], cbb_ref[...])
        pc = _causal_conv(cr, hc_ref[...], cwc_ref[...], cbc_ref[...])
        hx_ref[...] = xr[L - 8:L, :]
        hb_ref[...] = br[L - 8:L, :]
        hc_ref[...] = cr[L - 8:L, :]
        pre_ref[:, 0:SSD_WIDTH] = px
        pre_ref[:, SSD_WIDTH:SSD_WIDTH + SSD_BC] = pb
        pre_ref[:, SSD_WIDTH + SSD_BC:SSD_CONV_CH] = pc
        xs = px * _sigmoid(px)
        bm = pb * _sigmoid(pb)
        cm = pc * _sigmoid(pc)

        dt, _, a_cs, a_t = _ssd_common(dt_ref[...], dtb_ref[...], alog_ref[...])
        exp_a = jnp.exp(a_cs)
        a_last = a_cs[L - 1:L, :]
        dte = jnp.exp(a_last - a_cs)
        dec = jnp.exp(a_last)

        lane = lax.broadcasted_iota(jnp.int32, (L, LANES), 1)
        sub = lax.broadcasted_iota(jnp.int32, (L, LANES), 0)
        lo = lane < SSD_HEAD_DIM
        causal = sub >= lane
        top = sub < SSD_HEAD_DIM

        for g in range(SSD_GROUPS):
            b_g = bm[:, g * SSD_STATE:(g + 1) * SSD_STATE]
            c_g = cm[:, g * SSD_STATE:(g + 1) * SSD_STATE]
            cb = _dot(c_g, b_g, NT)
            heads = [2 * (g * PAIRS_PER_GROUP + jj) for jj in range(PAIRS_PER_GROUP)]
            gcols = slice(g * GROUP_COLS, (g + 1) * GROUP_COLS)
            xs_g = xs[:, gcols]
            xdt_g = xs_g * _group_select(lo, dt, heads)
            h_g = state_ref[g]
            st_ref[0, g] = h_g
            y_off_g = _dot(c_g, h_g, NT) * _group_select(lo, exp_a, heads)
            s_new_g = _dot(xdt_g * _group_select(lo, dte, heads), b_g, TN)
            for jj, h0 in enumerate(heads):
                blk = slice(jj * LANES, (jj + 1) * LANES)
                cols = slice(g * GROUP_COLS + jj * LANES, g * GROUP_COLS + (jj + 1) * LANES)
                xdt = xdt_g[:, blk]
                g0 = jnp.where(causal, jnp.exp(a_cs[:, h0:h0 + 1] - a_t[h0:h0 + 1, :]), 0.0) * cb
                g1 = jnp.where(causal, jnp.exp(a_cs[:, h0 + 1:h0 + 2] - a_t[h0 + 1:h0 + 2, :]), 0.0) * cb
                lhs = jnp.concatenate([g0, g1], axis=1)
                rhs = jnp.concatenate([jnp.where(lo, xdt, 0.0), jnp.where(lo, 0.0, xdt)], axis=0)
                y_diag = _dot(lhs, rhs, NN)
                dec_rows = jnp.where(top, dec[:, h0:h0 + 1], dec[:, h0 + 1:h0 + 2])
                state_ref[g, blk, :] = h_g[blk, :] * dec_rows + s_new_g[blk, :]
                yacc_ref[:, cols] = (y_diag + y_off_g[:, blk]) + xs_g[:, blk] * d_ref[:, cols]

        y = yacc_ref[...]
        y_ref[...] = y
        zz = z_ref[...]
        y2 = y * (zz * _sigmoid(zz))
        gw = SSD_WIDTH // SSD_GROUPS
        for g in range(SSD_GROUPS):
            seg = y2[:, g * gw:(g + 1) * gw]
            r = lax.rsqrt(jnp.mean(seg * seg, axis=-1, keepdims=True) + EPS)
            yssd_ref[:, g * gw:(g + 1) * gw] = ((seg * r) * ng_ref[:, g * gw:(g + 1) * gw]).astype(yssd_ref.dtype)

    return pl.pallas_call(
        body, name="ssd_fwd", grid=(nc,),
        in_specs=[sp["z"], sp["xr"], sp["br"], sp["cr"], sp["dt"], sp["cwx"], sp["cwb"], sp["cwc"],
                  sp["cbx"], sp["cbb"], sp["cbc"], sp["vec128"], sp["vec128"], sp["vecw"], sp["vecw"]],
        out_specs=[sp["wide"], sp["wide"], sp["states"], pl.BlockSpec((L, SSD_CONV_CH), lambda c: (c, 0))],
        out_shape=[jax.ShapeDtypeStruct((t, SSD_WIDTH), MXU_DTYPE), jax.ShapeDtypeStruct((t, SSD_WIDTH), F32),
                   jax.ShapeDtypeStruct((nc, SSD_GROUPS, GROUP_COLS, SSD_STATE), F32), jax.ShapeDtypeStruct((t, SSD_CONV_CH), F32)],
        scratch_shapes=[pltpu.VMEM((8, SSD_WIDTH), F32), pltpu.VMEM((8, SSD_BC), F32), pltpu.VMEM((8, SSD_BC), F32),
                        pltpu.VMEM((SSD_GROUPS, GROUP_COLS, SSD_STATE), F32), pltpu.VMEM((L, SSD_WIDTH), F32)],
        compiler_params=_params(("arbitrary",)),
    )(proj_ssd, proj_ssd, proj_ssd, proj_ssd, dt_p, conv_w, conv_w, conv_w, conv_b, conv_b, conv_b,
      dtb, alog, d_row, ng_row)


def _ssd_bwd(proj_ssd, pre, dt_p, y, states, dyssd, conv_w, dtb, alog, d_row, ng_row, exchange=None):
    t = proj_ssd.shape[0]
    nc = t // SSD_CHUNK
    L = SSD_CHUNK
    sp = _ssd_specs(nc, True)

    def pre_spec(width, col):
        return pl.BlockSpec((L, width), lambda c: (nc - 1 - c, col))

    def body(z_ref, xr_ref, br_ref, cr_ref, px_ref, pb_ref, pc_ref, dt_ref, y_ref, st_ref, dy_ref,
             cwx_ref, cwb_ref, cwc_ref, dtb_ref, alog_ref, d_ref, ng_ref,
             dssd_ref, ddt_ref, gcw_ref, gcb_ref, gdtb_ref, galog_ref, gd_ref, gng_ref,
             gn_ref, nx_ref, nb_ref, ncc_ref, dxs_ref):
        step = pl.program_id(0)

        @pl.when(step == 0)
        def _():
            gn_ref[...] = jnp.zeros_like(gn_ref)
            nx_ref[...] = jnp.zeros_like(nx_ref)
            nb_ref[...] = jnp.zeros_like(nb_ref)
            ncc_ref[...] = jnp.zeros_like(ncc_ref)
            for ref in (gcw_ref, gcb_ref, gdtb_ref, galog_ref, gd_ref, gng_ref):
                ref[...] = jnp.zeros_like(ref)

        xr, br, cr = xr_ref[...], br_ref[...], cr_ref[...]
        cwx, cwb, cwc = cwx_ref[...], cwb_ref[...], cwc_ref[...]
        px, pb, pc = px_ref[...], pb_ref[...], pc_ref[...]
        sx, sb, sc = _sigmoid(px), _sigmoid(pb), _sigmoid(pc)
        xs, bm, cm = px * sx, pb * sb, pc * sc

        dt_in = dt_ref[...] + dtb_ref[...]
        dt, aneg, a_cs, a_t = _ssd_common(dt_ref[...], dtb_ref[...], alog_ref[...])
        exp_a = jnp.exp(a_cs)
        a_last = a_cs[L - 1:L, :]
        dte = jnp.exp(a_last - a_cs)
        dec = jnp.exp(a_last)

        lane = lax.broadcasted_iota(jnp.int32, (L, LANES), 1)
        sub = lax.broadcasted_iota(jnp.int32, (L, LANES), 0)
        lo = lane < SSD_HEAD_DIM
        causal = sub >= lane
        top = sub < SSD_HEAD_DIM
        last_row = sub == L - 1

        yv = y_ref[...]
        zz = z_ref[...]
        sz = _sigmoid(zz)
        silz = zz * sz
        y2 = yv * silz
        dyv = dy_ref[...]
        gw = SSD_WIDTH // SSD_GROUPS
        d_y2_parts = []
        gng_parts = []
        for g in range(SSD_GROUPS):
            seg = y2[:, g * gw:(g + 1) * gw]
            dseg = dyv[:, g * gw:(g + 1) * gw]
            r = lax.rsqrt(jnp.mean(seg * seg, axis=-1, keepdims=True) + EPS)
            n = seg * r
            dn = dseg * ng_ref[:, g * gw:(g + 1) * gw]
            gng_parts.append(jnp.sum(dseg * n, axis=0, keepdims=True))
            d_y2_parts.append(r * (dn - n * jnp.mean(dn * n, axis=-1, keepdims=True)))
        d_y2 = jnp.concatenate(d_y2_parts, axis=1)
        gng_ref[...] += jnp.concatenate(gng_parts, axis=1)
        d_y = d_y2 * silz
        dssd_ref[:, 0:SSD_WIDTH] = (d_y2 * yv * (sz * (1.0 + zz * (1.0 - sz)))).astype(dssd_ref.dtype)
        gd_ref[...] += jnp.sum(d_y * xs, axis=0, keepdims=True)
        dxs_ref[...] = d_y * d_ref[...]

        d_a = jnp.zeros((L, LANES), F32)
        d_at = jnp.zeros((LANES, L), F32)
        ddt = jnp.zeros((L, LANES), F32)
        d_b_parts, d_c_parts = [], []
        for g in range(SSD_GROUPS):
            b_g = bm[:, g * SSD_STATE:(g + 1) * SSD_STATE]
            c_g = cm[:, g * SSD_STATE:(g + 1) * SSD_STATE]
            cb = _dot(c_g, b_g, NT)
            d_cb = jnp.zeros((L, L), F32)
            heads = [2 * (g * PAIRS_PER_GROUP + jj) for jj in range(PAIRS_PER_GROUP)]
            gcols = slice(g * GROUP_COLS, (g + 1) * GROUP_COLS)
            dy_g, xs_g = d_y[:, gcols], xs[:, gcols]
            dt_g = _group_select(lo, dt, heads)
            expa_g = _group_select(lo, exp_a, heads)
            dte_g = _group_select(lo, dte, heads)
            xdt_g = xs_g * dt_g
            h_g = st_ref[0, g]
            gn_g = gn_ref[g]
            dys_g = dy_g * expa_g
            d_cg = _dot(dys_g, h_g, NN)
            d_h_g = _dot(dys_g, c_g, TN)
            t1_g = dy_g * _dot(c_g, h_g, NT) * expa_g
            d_bg = _dot(xdt_g * dte_g, gn_g, NN)
            dxdt_g = _dot(b_g, gn_g, NT) * dte_g
            t2_g = dxdt_g * xdt_g
            t12_g = t1_g - t2_g
            gh_g = jnp.sum(gn_g * h_g, axis=1, keepdims=True)
            for jj, h0 in enumerate(heads):
                blk = slice(jj * LANES, (jj + 1) * LANES)
                cols = slice(g * GROUP_COLS + jj * LANES, g * GROUP_COLS + (jj + 1) * LANES)
                dy_p, xs_p, xdt, dt_pp = dy_g[:, blk], xs_g[:, blk], xdt_g[:, blk], dt_g[:, blk]
                l0 = jnp.where(causal, jnp.exp(a_cs[:, h0:h0 + 1] - a_t[h0:h0 + 1, :]), 0.0)
                l1 = jnp.where(causal, jnp.exp(a_cs[:, h0 + 1:h0 + 2] - a_t[h0 + 1:h0 + 2, :]), 0.0)
                g0, g1 = l0 * cb, l1 * cb
                dcat = jnp.concatenate([jnp.where(lo, dy_p, 0.0), jnp.where(lo, 0.0, dy_p)], axis=0)
                d_xdt = dxdt_g[:, blk] + _dot(jnp.concatenate([g0, g1], axis=0), dcat, TN)
                dm = _dot(dcat, xdt, NT)
                dm0, dm1 = dm[0:L], dm[L:2 * L]
                d_cb = d_cb + (l0 * dm0 + l1 * dm1)
                e0, e1 = dm0 * g0, dm1 * g1
                a0, a1 = _halves(lo, t12_g[:, blk])
                a0 = a0 + jnp.sum(e0, axis=1, keepdims=True)
                a1 = a1 + jnp.sum(e1, axis=1, keepdims=True)
                s0, s1 = _halves(lo, t2_g[:, blk])
                gh = gh_g[blk, :]
                dd0 = jnp.sum(jnp.where(top[:, 0:1], gh, 0.0), axis=0, keepdims=True)
                dd1 = jnp.sum(jnp.where(top[:, 0:1], 0.0, gh), axis=0, keepdims=True)
                end0 = jnp.sum(s0, axis=0, keepdims=True) + dd0 * dec[:, h0:h0 + 1]
                end1 = jnp.sum(s1, axis=0, keepdims=True) + dd1 * dec[:, h0 + 1:h0 + 2]
                d_a = d_a + jnp.where(lane == h0, a0 + jnp.where(last_row, end0, 0.0), 0.0)
                d_a = d_a + jnp.where(lane == h0 + 1, a1 + jnp.where(last_row, end1, 0.0), 0.0)
                d_at = d_at - jnp.where(sub == h0, jnp.sum(e0, axis=0, keepdims=True), 0.0)
                d_at = d_at - jnp.where(sub == h0 + 1, jnp.sum(e1, axis=0, keepdims=True), 0.0)
                dec_rows = jnp.where(top, dec[:, h0:h0 + 1], dec[:, h0 + 1:h0 + 2])
                gn_ref[g, blk, :] = d_h_g[blk, :] + dec_rows * gn_g[blk, :]
                q0, q1 = _halves(lo, d_xdt * xs_p)
                ddt = ddt + jnp.where(lane == h0, q0, 0.0) + jnp.where(lane == h0 + 1, q1, 0.0)
                dxs_ref[:, cols] += d_xdt * dt_pp
            d_cg = d_cg + _dot(d_cb, b_g, NN)
            d_bg = d_bg + _dot(d_cb, c_g, TN)
            d_b_parts.append(d_bg)
            d_c_parts.append(d_cg)

        rc = _rev_cumsum_rows(d_a + d_at.T)
        d_dt = rc * aneg + ddt
        galog_ref[...] += jnp.sum(rc * dt, axis=0, keepdims=True) * aneg
        d_dtraw = d_dt * _sigmoid(dt_in)
        gdtb_ref[...] += jnp.sum(d_dtraw, axis=0, keepdims=True)
        ddt_ref[...] = d_dtraw.astype(ddt_ref.dtype)

        def dsilu(p, s):
            return s * (1.0 + p * (1.0 - s))

        dcx = dxs_ref[...] * dsilu(px, sx)
        dcb = jnp.concatenate(d_b_parts, axis=1) * dsilu(pb, sb)
        dcc = jnp.concatenate(d_c_parts, axis=1) * dsilu(pc, sc)
        drx, gwx, gbx = _conv_backward(dcx, nx_ref[...], xr, cwx)
        drb, gwb, gbb = _conv_backward(dcb, nb_ref[...], br, cwb)
        drc, gwc, gbc = _conv_backward(dcc, ncc_ref[...], cr, cwc)
        nx_ref[...] = dcx[0:8, :]
        nb_ref[...] = dcb[0:8, :]
        ncc_ref[...] = dcc[0:8, :]
        dssd_ref[:, SSD_WIDTH:2 * SSD_WIDTH] = drx.astype(dssd_ref.dtype)
        dssd_ref[:, 2 * SSD_WIDTH:2 * SSD_WIDTH + SSD_BC] = drb.astype(dssd_ref.dtype)
        dssd_ref[:, 2 * SSD_WIDTH + SSD_BC:SSD_SEG] = drc.astype(dssd_ref.dtype)
        for k in range(CONV_WIDTH):
            gcw_ref[k:k + 1, :] += jnp.concatenate([gwx[k], gwb[k], gwc[k]], axis=1)
        gcb_ref[...] += jnp.concatenate([gbx, gbb, gbc], axis=1)

    const = lambda shape: pl.BlockSpec(shape, lambda c: (0,) * len(shape))
    body, ex_in, ex_out, ex_shape, ex_sems = _riding(
        exchange, body, 18, 8, lambda: pl.program_id(0) == 0, lambda: pl.program_id(0) == nc - 1)
    res = pl.pallas_call(
        body, name="ssd_bwd", grid=(nc,),
        in_specs=[sp["z"], sp["xr"], sp["br"], sp["cr"], pre_spec(SSD_WIDTH, 0), pre_spec(SSD_BC, SSD_WIDTH // SSD_BC), pre_spec(SSD_BC, SSD_WIDTH // SSD_BC + 1),
                  sp["dt"], sp["wide"], sp["states"], sp["wide"],
                  sp["cwx"], sp["cwb"], sp["cwc"], sp["vec128"], sp["vec128"], sp["vecw"], sp["vecw"]] + ex_in,
        out_specs=[pl.BlockSpec((L, SSD_SEG), lambda c: (nc - 1 - c, 0)), sp["dt"],
                   const((CONV_WIDTH, SSD_CONV_CH)), const((1, SSD_CONV_CH)), const((1, LANES)), const((1, LANES)),
                   const((1, SSD_WIDTH)), const((1, SSD_WIDTH))] + ex_out,
        out_shape=[jax.ShapeDtypeStruct((t, SSD_SEG), MXU_DTYPE), jax.ShapeDtypeStruct((t, DT_PAD), MXU_DTYPE),
                   jax.ShapeDtypeStruct((CONV_WIDTH, SSD_CONV_CH), F32), jax.ShapeDtypeStruct((1, SSD_CONV_CH), F32),
                   jax.ShapeDtypeStruct((1, LANES), F32), jax.ShapeDtypeStruct((1, LANES), F32),
                   jax.ShapeDtypeStruct((1, SSD_WIDTH), F32), jax.ShapeDtypeStruct((1, SSD_WIDTH), F32)] + ex_shape,
        scratch_shapes=[pltpu.VMEM((SSD_GROUPS, GROUP_COLS, SSD_STATE), F32), pltpu.VMEM((8, SSD_WIDTH), F32),
                        pltpu.VMEM((8, SSD_BC), F32), pltpu.VMEM((8, SSD_BC), F32), pltpu.VMEM((L, SSD_WIDTH), F32)] + ex_sems,
        compiler_params=_params(("arbitrary",)),
    )(proj_ssd, proj_ssd, proj_ssd, proj_ssd, pre, pre, pre, dt_p, y, states, dyssd,
      conv_w, conv_w, conv_w, dtb, alog, d_row, ng_row, *(exchange["arrays"] if exchange else []))
    return res[:8], res[8:]


def _lru_gates(xl, wa_ref, wx_ref, ba, bx, lam):
    pre_a, pre_x = [], []
    for g in range(LRU_NGROUPS):
        xg = xl[:, g * LRU_GROUP:(g + 1) * LRU_GROUP]
        pre_a.append(_dot(xg, wa_ref[g], NN))
        pre_x.append(_dot(xg, wx_ref[g], NN))
    r = _sigmoid(jnp.concatenate(pre_a, axis=1) + ba)
    i = _sigmoid(jnp.concatenate(pre_x, axis=1) + bx)
    log_a = (-LRU_C * r) * _softplus(-lam)
    a = jnp.exp(log_a)
    mult_sq = -jnp.tanh(log_a) * (a * a + 1.0)
    return r, i, mult_sq, a, jnp.sqrt(mult_sq)


def _scan_rows(p, u, carry, reverse):
    rows, w = p.shape
    groups = rows // 8
    p3, u3 = p.reshape(groups, 8, w), u.reshape(groups, 8, w)
    row = lax.broadcasted_iota(jnp.int32, (groups, 8, w), 1)
    for s in (1, 2, 4):
        ok = row < 8 - s if reverse else row >= s
        shift = 8 - s if reverse else s
        u3 = p3 * jnp.where(ok, pltpu.roll(u3, shift, 1), 0.0) + u3
        p3 = p3 * jnp.where(ok, pltpu.roll(p3, shift, 1), 1.0)
    out = [None] * groups
    for k in (range(groups - 1, -1, -1) if reverse else range(groups)):
        out[k] = p3[k] * carry + u3[k]
        carry = out[k][0:1, :] if reverse else out[k][7:8, :]
    return jnp.concatenate(out, axis=0), carry


def _lru_fwd(proj_lru, conv_w, conv_b, wa, wx, ba, bx, lam):
    t = proj_lru.shape[0]
    rows = min(LRU_ROWS, t)
    nb = t // rows
    W = LRU_WIDTH

    def body(lg_ref, lx_ref, cw_ref, cb_ref, wa_ref, wx_ref, ba_ref, bx_ref, lam_ref, ylru_ref, h_ref, xl_ref,
             halo_ref, carry_ref):
        @pl.when(pl.program_id(0) == 0)
        def _():
            halo_ref[...] = jnp.zeros_like(halo_ref)
            carry_ref[...] = jnp.zeros_like(carry_ref)

        lx = lx_ref[...]
        xl = _causal_conv(lx, halo_ref[...], cw_ref[...], cb_ref[...])
        halo_ref[...] = lx[rows - 8:rows, :]
        xl_ref[...] = xl
        _, i, _, a, mult = _lru_gates(xl, wa_ref, wx_ref, ba_ref[...], bx_ref[...], lam_ref[...])
        u = mult * (i * xl)
        h, carry_ref[...] = _scan_rows(a, u, carry_ref[...], False)
        h_ref[...] = h
        lg = lg_ref[...]
        ylru_ref[...] = (h * (lg * _sigmoid(lg))).astype(ylru_ref.dtype)

    const = lambda shape: pl.BlockSpec(shape, lambda b: (0,) * len(shape))
    return pl.pallas_call(
        body, name="lru_fwd", grid=(nb,),
        in_specs=[pl.BlockSpec((rows, W), lambda b: (b, 0)), pl.BlockSpec((rows, W), lambda b: (b, 1)),
                  const((CONV_WIDTH, W)), const((1, W)), const((LRU_NGROUPS, LRU_GROUP, LRU_GROUP)),
                  const((LRU_NGROUPS, LRU_GROUP, LRU_GROUP)), const((1, W)), const((1, W)), const((1, W))],
        out_specs=[pl.BlockSpec((rows, W), lambda b: (b, 0))] * 3,
        out_shape=[jax.ShapeDtypeStruct((t, W), MXU_DTYPE), jax.ShapeDtypeStruct((t, W), F32), jax.ShapeDtypeStruct((t, W), F32)],
        scratch_shapes=[pltpu.VMEM((8, W), F32), pltpu.VMEM((1, W), F32)],
        compiler_params=_params(("arbitrary",)),
    )(proj_lru, proj_lru, conv_w, conv_b, wa, wx, ba, bx, lam)


def _lru_bwd(proj_lru, xl, h, dylru, conv_w, wa, wx, ba, bx, lam, exchange=None):
    t = proj_lru.shape[0]
    rows = min(LRU_ROWS, t)
    nb = t // rows
    W = LRU_WIDTH
    groups8 = rows // 8

    def rev(b):
        return nb - 1 - b

    def halo_spec(col):
        return pl.BlockSpec((8, W), lambda b: (jnp.maximum(rev(b) * groups8 - 1, 0), col))

    def body(lg_ref, lx_ref, xl_ref, h_ref, hh_ref, dy_ref, cw_ref, wa_ref, wx_ref, ba_ref, bx_ref, lam_ref,
             dlru_ref, gcw_ref, gcb_ref, gba_ref, gbx_ref, glam_ref, gwa_ref, gwx_ref,
             gcarry_ref, afirst_ref, nxt_ref):
        step = pl.program_id(0)

        @pl.when(step == 0)
        def _():
            gcarry_ref[...] = jnp.zeros_like(gcarry_ref)
            afirst_ref[...] = jnp.zeros_like(afirst_ref)
            nxt_ref[...] = jnp.zeros_like(nxt_ref)
            for ref in (gcw_ref, gcb_ref, gba_ref, gbx_ref, glam_ref, gwa_ref, gwx_ref):
                ref[...] = jnp.zeros_like(ref)

        keep = jnp.where(step == nb - 1, 0.0, 1.0)
        lx = lx_ref[...]
        cw = cw_ref[...]
        xl = xl_ref[...]
        lam = lam_ref[...]
        r, i, mult_sq, a, mult = _lru_gates(xl, wa_ref, wx_ref, ba_ref[...], bx_ref[...], lam)
        hv = h_ref[...]
        h_prev = _shift_down(hv, hh_ref[...] * keep, 1)
        lg = lg_ref[...]
        sg = _sigmoid(lg)
        dyv = dy_ref[...]
        d_h = dyv * (lg * sg)
        dlru_ref[:, 0:W] = (dyv * hv * (sg * (1.0 + lg * (1.0 - sg)))).astype(dlru_ref.dtype)

        row = lax.broadcasted_iota(jnp.int32, (rows, W), 0)
        p = jnp.where(row < rows - 1, pltpu.roll(a, rows - 1, 0), afirst_ref[...])
        gsc, gcarry_ref[...] = _scan_rows(p, d_h, gcarry_ref[...], True)
        afirst_ref[...] = a[0:1, :]

        d_a = gsc * h_prev
        v = i * xl
        d_mult = gsc * v
        d_v = gsc * mult
        d_i = d_v * xl
        d_xl = d_v * i
        d_la = d_a * a - d_mult * (a * a) * lax.rsqrt(mult_sq)
        sp_neg = _softplus(-lam)
        d_r = d_la * (-LRU_C * sp_neg)
        glam_ref[...] += jnp.sum(d_la * r, axis=0, keepdims=True) * (LRU_C * _sigmoid(-lam))
        d_pa = d_r * r * (1.0 - r)
        d_px = d_i * i * (1.0 - i)
        gba_ref[...] += jnp.sum(d_pa, axis=0, keepdims=True)
        gbx_ref[...] += jnp.sum(d_px, axis=0, keepdims=True)
        parts = []
        for g in range(LRU_NGROUPS):
            cols = slice(g * LRU_GROUP, (g + 1) * LRU_GROUP)
            xg, dpa_g, dpx_g = xl[:, cols], d_pa[:, cols], d_px[:, cols]
            parts.append(_dot(dpa_g, wa_ref[g], NT) + _dot(dpx_g, wx_ref[g], NT))
            gwa_ref[g] += _dot(xg, dpa_g, TN)
            gwx_ref[g] += _dot(xg, dpx_g, TN)
        d_xl = d_xl + jnp.concatenate(parts, axis=1)
        d_lx, gw, gb = _conv_backward(d_xl, nxt_ref[...], lx, cw)
        nxt_ref[...] = d_xl[0:8, :]
        dlru_ref[:, W:2 * W] = d_lx.astype(dlru_ref.dtype)
        for k in range(CONV_WIDTH):
            gcw_ref[k:k + 1, :] += gw[k]
        gcb_ref[...] += gb

    const = lambda shape: pl.BlockSpec(shape, lambda b: (0,) * len(shape))
    wspec = const((LRU_NGROUPS, LRU_GROUP, LRU_GROUP))
    blk = lambda col: pl.BlockSpec((rows, W), lambda b: (rev(b), col))
    body, ex_in, ex_out, ex_shape, ex_sems = _riding(
        exchange, body, 12, 8, lambda: pl.program_id(0) == 0, lambda: pl.program_id(0) == nb - 1)
    res = pl.pallas_call(
        body, name="lru_bwd", grid=(nb,),
        in_specs=[blk(0), blk(1), blk(0), blk(0), halo_spec(0), blk(0),
                  const((CONV_WIDTH, W)), wspec, wspec, const((1, W)), const((1, W)), const((1, W))] + ex_in,
        out_specs=[pl.BlockSpec((rows, 2 * W), lambda b: (rev(b), 0)), const((CONV_WIDTH, W)), const((1, W)),
                   const((1, W)), const((1, W)), const((1, W)), wspec, wspec] + ex_out,
        out_shape=[jax.ShapeDtypeStruct((t, 2 * W), MXU_DTYPE), jax.ShapeDtypeStruct((CONV_WIDTH, W), F32),
                   jax.ShapeDtypeStruct((1, W), F32), jax.ShapeDtypeStruct((1, W), F32), jax.ShapeDtypeStruct((1, W), F32),
                   jax.ShapeDtypeStruct((1, W), F32), jax.ShapeDtypeStruct((LRU_NGROUPS, LRU_GROUP, LRU_GROUP), F32),
                   jax.ShapeDtypeStruct((LRU_NGROUPS, LRU_GROUP, LRU_GROUP), F32)] + ex_shape,
        scratch_shapes=[pltpu.VMEM((1, W), F32), pltpu.VMEM((1, W), F32), pltpu.VMEM((8, W), F32)] + ex_sems,
        compiler_params=_params(("arbitrary",)),
    )(proj_lru, proj_lru, xl, h, h, dylru, conv_w, wa, wx, ba, bx, lam, *(exchange["arrays"] if exchange else []))
    return res[:8], res[8:]


def _mem_scores(q_h, k_h):
    s = _dot(q_h, k_h, NT) * (MEM_HEAD_DIM ** -0.5)
    s = s - jnp.max(s, axis=-1, keepdims=True)
    e = jnp.exp(s)
    return e / jnp.sum(e, axis=-1, keepdims=True)


def _mem_fwd(q, kv, rows=512):
    t = q.shape[0]
    rows = min(rows, t)
    m = kv.shape[0]

    def body(q_ref, kv_ref, y_ref):
        for hd in range(MEM_HEADS):
            cols = slice(hd * MEM_HEAD_DIM, (hd + 1) * MEM_HEAD_DIM)
            vcols = slice(D_MODEL + hd * MEM_HEAD_DIM, D_MODEL + (hd + 1) * MEM_HEAD_DIM)
            p = _mem_scores(q_ref[:, cols], kv_ref[:, cols])
            y_ref[:, cols] = _dot(p, kv_ref[:, vcols], NN).astype(y_ref.dtype)

    return pl.pallas_call(
        body, name="mem_fwd", grid=(t // rows,),
        in_specs=[pl.BlockSpec((rows, D_MODEL), lambda i: (i, 0)), pl.BlockSpec((m, 2 * D_MODEL), lambda i: (0, 0))],
        out_specs=pl.BlockSpec((rows, D_MODEL), lambda i: (i, 0)),
        out_shape=jax.ShapeDtypeStruct((t, D_MODEL), MXU_DTYPE),
        compiler_params=_params(("parallel",)),
    )(q, kv)


def _mem_bwd(q, kv, dy, rows=512):
    t = q.shape[0]
    rows = min(rows, t)
    m = kv.shape[0]

    def body(q_ref, kv_ref, dy_ref, dq_ref, dkv_ref):
        @pl.when(pl.program_id(0) == 0)
        def _():
            dkv_ref[...] = jnp.zeros_like(dkv_ref)

        for hd in range(MEM_HEADS):
            cols = slice(hd * MEM_HEAD_DIM, (hd + 1) * MEM_HEAD_DIM)
            vcols = slice(D_MODEL + hd * MEM_HEAD_DIM, D_MODEL + (hd + 1) * MEM_HEAD_DIM)
            q_h, k_h, dy_h = q_ref[:, cols], kv_ref[:, cols], dy_ref[:, cols]
            p = _mem_scores(q_h, k_h)
            dp = _dot(dy_h, kv_ref[:, vcols], NT)
            dkv_ref[:, vcols] += _dot(p, dy_h, TN)
            ds = p * (dp - jnp.sum(dp * p, axis=-1, keepdims=True)) * (MEM_HEAD_DIM ** -0.5)
            dq_ref[:, cols] = _dot(ds, k_h, NN).astype(dq_ref.dtype)
            dkv_ref[:, cols] += _dot(ds, q_h, TN)

    return pl.pallas_call(
        body, name="mem_bwd", grid=(t // rows,),
        in_specs=[pl.BlockSpec((rows, D_MODEL), lambda i: (i, 0)), pl.BlockSpec((m, 2 * D_MODEL), lambda i: (0, 0)),
                  pl.BlockSpec((rows, D_MODEL), lambda i: (i, 0))],
        out_specs=[pl.BlockSpec((rows, D_MODEL), lambda i: (i, 0)), pl.BlockSpec((m, 2 * D_MODEL), lambda i: (0, 0))],
        out_shape=[jax.ShapeDtypeStruct((t, D_MODEL), MXU_DTYPE), jax.ShapeDtypeStruct((m, 2 * D_MODEL), F32)],
        compiler_params=_params(("arbitrary",)),
    )(q, kv, dy)


def _merge_fwd(x, yssd, ylru, ymem, gl, w_bs, w_bl, w_bm, w_out, fg, tgt, rows=256):
    t = x.shape[0]
    rows = min(rows, t)
    D = D_MODEL

    def body(x_ref, ys_ref, yl_ref, ym_ref, gl_ref, wbs_ref, wbl_ref, wbm_ref, wo_ref, fg_ref, tgt_ref,
             ps_ref, pl_ref, pm_ref, mg_ref, dx2_ref, loss_ref, gfg_ref):
        @pl.when(pl.program_id(0) == 0)
        def _():
            loss_ref[...] = jnp.zeros_like(loss_ref)
            gfg_ref[...] = jnp.zeros_like(gfg_ref)

        ps = _dot(ys_ref[...], wbs_ref[...], NN)
        pl_ = _dot(yl_ref[...], wbl_ref[...], NN)
        pm = _dot(ym_ref[...], wbm_ref[...], NN)
        ps_ref[...] = ps
        pl_ref[...] = pl_
        pm_ref[...] = pm
        merged = (_sigmoid(gl_ref[:, 0:D]) * ps + _sigmoid(gl_ref[:, D:2 * D]) * pl_) + _sigmoid(gl_ref[:, 2 * D:3 * D]) * pm
        mg_ref[...] = merged.astype(mg_ref.dtype)
        x2 = x_ref[...] + _dot(merged, wo_ref[...], NN)
        r2 = lax.rsqrt(jnp.mean(x2 * x2, axis=-1, keepdims=True) + EPS)
        xn = x2 * r2
        fg = fg_ref[...]
        diff = xn * fg - tgt_ref[...]
        tile_loss = 0.5 * jnp.sum(jnp.mean(diff * diff, axis=-1, keepdims=True), axis=0, keepdims=True)
        loss_ref[...] += jnp.broadcast_to(tile_loss, loss_ref.shape)
        d_out = diff * (1.0 / D)
        gfg_ref[...] += jnp.sum(d_out * xn, axis=0, keepdims=True)
        dxn = d_out * fg
        dx2_ref[...] = r2 * (dxn - xn * jnp.mean(dxn * xn, axis=-1, keepdims=True))

    row = lambda w: pl.BlockSpec((rows, w), lambda i: (i, 0))
    const = lambda shape: pl.BlockSpec(shape, lambda i: (0,) * len(shape))
    return pl.pallas_call(
        body, name="merge_fwd", grid=(t // rows,),
        in_specs=[row(D), row(SSD_WIDTH), row(LRU_WIDTH), row(D), row(3 * D), const((SSD_WIDTH, D)), const((LRU_WIDTH, D)),
                  const((D, D)), const((D, D)), const((1, D)), row(D)],
        out_specs=[row(D), row(D), row(D), row(D), row(D), const((1, LANES)), const((1, D))],
        out_shape=[jax.ShapeDtypeStruct((t, D), F32), jax.ShapeDtypeStruct((t, D), F32), jax.ShapeDtypeStruct((t, D), F32),
                   jax.ShapeDtypeStruct((t, D), MXU_DTYPE), jax.ShapeDtypeStruct((t, D), F32),
                   jax.ShapeDtypeStruct((1, LANES), F32), jax.ShapeDtypeStruct((1, D), F32)],
        compiler_params=_params(("arbitrary",)),
    )(x, yssd, ylru, ymem, gl, w_bs, w_bl, w_bm, w_out, fg, tgt)


def _merge_bwd(dx2, gl, ps, pl_in, pm, w_bs, w_bl, w_bm, w_out, rows=256):
    t = dx2.shape[0]
    rows = min(rows, t)
    D = D_MODEL

    def body(dx2_ref, gl_ref, ps_ref, pl_ref, pm_ref, wbs_ref, wbl_ref, wbm_ref, wo_ref,
             dg_ref, dps_ref, dpl_ref, dpm_ref, dys_ref, dyl_ref, dym_ref):
        dm = _dot(dx2_ref[...], wo_ref[...], NT)
        for idx, (p_ref, dp_ref, w_ref, dy_ref) in enumerate(
                ((ps_ref, dps_ref, wbs_ref, dys_ref), (pl_ref, dpl_ref, wbl_ref, dyl_ref), (pm_ref, dpm_ref, wbm_ref, dym_ref))):
            gate = _sigmoid(gl_ref[:, idx * D:(idx + 1) * D])
            dg_ref[:, idx * D:(idx + 1) * D] = ((dm * p_ref[...]) * gate * (1.0 - gate)).astype(dg_ref.dtype)
            dp = dm * gate
            dp_ref[...] = dp.astype(dp_ref.dtype)
            dy_ref[...] = _dot(dp, w_ref[...], NT)

    row = lambda w: pl.BlockSpec((rows, w), lambda i: (i, 0))
    const = lambda shape: pl.BlockSpec(shape, lambda i: (0,) * len(shape))
    return pl.pallas_call(
        body, name="merge_bwd", grid=(t // rows,),
        in_specs=[row(D), row(3 * D), row(D), row(D), row(D), const((SSD_WIDTH, D)), const((LRU_WIDTH, D)),
                  const((D, D)), const((D, D))],
        out_specs=[row(3 * D), row(D), row(D), row(D), row(SSD_WIDTH), row(LRU_WIDTH), row(D)],
        out_shape=[jax.ShapeDtypeStruct((t, 3 * D), MXU_DTYPE), jax.ShapeDtypeStruct((t, D), MXU_DTYPE),
                   jax.ShapeDtypeStruct((t, D), MXU_DTYPE), jax.ShapeDtypeStruct((t, D), MXU_DTYPE),
                   jax.ShapeDtypeStruct((t, SSD_WIDTH), F32), jax.ShapeDtypeStruct((t, LRU_WIDTH), F32),
                   jax.ShapeDtypeStruct((t, D), F32)],
        compiler_params=_params(("parallel",)),
    )(dx2, gl, ps, pl_in, pm, w_bs, w_bl, w_bm, w_out)


def _mesh_place():
    x, y, c = lax.axis_index("x"), lax.axis_index("y"), lax.axis_index("c")
    return x, y, c, 4 * x + 2 * y + c


def _other_chips(x, y):
    return [(1 - x, y), (x, 1 - y), (1 - x, 1 - y)]


def _all_gather_plan(arrs):
    n = len(arrs)

    def parts(ins, outs, send_sems, recv_sems, local_sems):
        x, y, c, me = _mesh_place()
        sibling = (x, y, 1 - c)
        chips = _other_chips(x, y)

        def slot(px, py, pc):
            return 4 * px + 2 * py + pc

        def copy(a, k, block, to, src=None):
            return pltpu.make_async_remote_copy(
                src_ref=outs[a].at[block] if src is None else src, dst_ref=outs[a].at[block],
                send_sem=send_sems.at[a, k], recv_sem=recv_sems.at[a, k], device_id=to, device_id_type=pl.DeviceIdType.MESH)

        def local():
            return [pltpu.make_async_copy(ins[a], outs[a].at[me], local_sems.at[a]) for a in range(n)]

        def first():
            return [copy(a, k, me, to, src=ins[a]) for a in range(n)
                    for k, to in enumerate([sibling] + [(*chip, c) for chip in chips])]

        return x, y, c, sibling, chips, slot, copy, local, first

    def start(ins, outs, *sems):
        *_, local, first = parts(ins, outs, *sems)
        for cp in local() + first():
            cp.start()

    def wait(ins, outs, *sems):
        x, y, c, sibling, chips, slot, copy, local, first = parts(ins, outs, *sems)
        sends = first()
        for j, chip in enumerate(chips):
            for a in range(n):
                copy(a, 1 + j, slot(*chip, c), sibling).wait_recv()
                passed = copy(a, 4 + j, slot(*chip, c), sibling)
                passed.start()
                sends.append(passed)
        for a in range(n):
            copy(a, 0, slot(x, y, 1 - c), sibling).wait_recv()
        for j, chip in enumerate(chips):
            for a in range(n):
                copy(a, 4 + j, slot(*chip, 1 - c), sibling).wait_recv()
        for cp in sends:
            cp.wait_send()
        for cp in local():
            cp.wait()

    return dict(arrays=list(arrs), out_shape=[jax.ShapeDtypeStruct((N_DEV,) + a.shape, a.dtype) for a in arrs],
                sems=[(n, 7), (n, 7), (n,)], start=start, wait=wait)


N_CHIPS = 4


def _pair_plan(parts):
    n = len(parts)

    def copies(ins, outs, send_sems, recv_sems):
        x, y, c, _ = _mesh_place()
        return [pltpu.make_async_remote_copy(src_ref=ins[a].at[q, 1 - c], dst_ref=outs[a].at[q], send_sem=send_sems.at[a, q],
                                             recv_sem=recv_sems.at[a, q], device_id=(x, y, 1 - c), device_id_type=pl.DeviceIdType.MESH)
                for a in range(n) for q in range(N_CHIPS)]

    def start(ins, outs, send_sems, recv_sems):
        for cp in copies(ins, outs, send_sems, recv_sems):
            cp.start()

    def wait(ins, outs, send_sems, recv_sems):
        cps = copies(ins, outs, send_sems, recv_sems)
        for cp in cps:
            cp.wait_recv()
        for cp in cps:
            cp.wait_send()

    return dict(arrays=list(parts), out_shape=[jax.ShapeDtypeStruct((N_CHIPS,) + a.shape[2:], a.dtype) for a in parts],
                sems=[(n, N_CHIPS), (n, N_CHIPS)], start=start, wait=wait)


def _chip_plan(sums):
    n = len(sums)

    def copies(ins, outs, send_sems, recv_sems, arriving):
        x, y, c, _ = _mesh_place()
        my_chip = 2 * x + y
        cps = []
        for a in range(n):
            for j, (px, py) in enumerate(_other_chips(x, y)):
                src, dst = (my_chip, 2 * px + py) if arriving else (2 * px + py, my_chip)
                cps.append(pltpu.make_async_remote_copy(
                    src_ref=ins[a].at[src], dst_ref=outs[a].at[dst], send_sem=send_sems.at[a, j], recv_sem=recv_sems.at[a, j],
                    device_id=(px, py, c), device_id_type=pl.DeviceIdType.MESH))
        return cps

    def start(ins, outs, send_sems, recv_sems):
        for cp in copies(ins, outs, send_sems, recv_sems, False):
            cp.start()

    def wait(ins, outs, send_sems, recv_sems):
        for cp in copies(ins, outs, send_sems, recv_sems, True):
            cp.wait_recv()
        for cp in copies(ins, outs, send_sems, recv_sems, False):
            cp.wait_send()

    return dict(arrays=list(sums), out_shape=[jax.ShapeDtypeStruct(a.shape, a.dtype) for a in sums],
                sems=[(n, 3), (n, 3)], start=start, wait=wait)


def _both(p1, p2):
    n1, s1 = len(p1["arrays"]), len(p1["sems"])

    def each(method):
        def run(ins, outs, *sems):
            p1[method](ins[:n1], outs[:n1], *sems[:s1])
            p2[method](ins[n1:], outs[n1:], *sems[s1:])
        return run

    return dict(arrays=p1["arrays"] + p2["arrays"], out_shape=p1["out_shape"] + p2["out_shape"],
                sems=p1["sems"] + p2["sems"], start=each("start"), wait=each("wait"))


def _run_exchange(plan, name):
    n = len(plan["arrays"])

    def body(*refs):
        ins, outs, sems = refs[:n], refs[n:2 * n], refs[2 * n:]
        plan["start"](ins, outs, *sems)
        plan["wait"](ins, outs, *sems)

    any_spec = pl.BlockSpec(memory_space=pl.ANY)
    return pl.pallas_call(
        body, name=name, in_specs=[any_spec] * n, out_specs=[any_spec] * n, out_shape=plan["out_shape"],
        scratch_shapes=[pltpu.SemaphoreType.DMA(shape) for shape in plan["sems"]],
    )(*plan["arrays"])


def _riding(plan, body, n_in, n_out, first, last):
    if plan is None:
        return body, [], [], [], []
    ne = len(plan["arrays"])

    def wrapped(*refs):
        ins, ex_in = refs[:n_in], refs[n_in:n_in + ne]
        outs = refs[n_in + ne:n_in + ne + n_out]
        ex_out = refs[n_in + ne + n_out:n_in + 2 * ne + n_out]
        n_sems = len(plan["sems"])
        scratch, sems = refs[n_in + 2 * ne + n_out:-n_sems], refs[-n_sems:]

        @pl.when(first())
        def _():
            plan["start"](ex_in, ex_out, *sems)

        body(*ins, *outs, *scratch)

        @pl.when(last())
        def _():
            plan["wait"](ex_in, ex_out, *sems)

    any_spec = pl.BlockSpec(memory_space=pl.ANY)
    sems = [pltpu.SemaphoreType.DMA(shape) for shape in plan["sems"]]
    return wrapped, [any_spec] * ne, [any_spec] * ne, plan["out_shape"], sems


def _col_tile(r, c, limit_bytes):
    assert c % LANES == 0, c
    best = LANES
    for cand in range(LANES, c + 1, LANES):
        if c % cand == 0 and r * cand * 4 <= limit_bytes:
            best = cand
    return best


def _chip_sum(part, recv, core, name):
    _, _, r, c = part.shape
    ct = _col_tile(r, c, 2 << 20)

    def body(core_ref, p_ref, r_ref, s_ref, t_ref):
        s = p_ref[...] + r_ref[...]
        s_ref[...] = s
        t_ref[...] = s.astype(t_ref.dtype)

    blk = pl.BlockSpec((None, r, ct), lambda q, i, core_ref: (q, 0, i))
    return pl.pallas_call(
        body, name=name,
        grid_spec=pltpu.PrefetchScalarGridSpec(
            num_scalar_prefetch=1, grid=(N_CHIPS, c // ct),
            in_specs=[pl.BlockSpec((None, None, r, ct), lambda q, i, core_ref: (q, core_ref[0], 0, i)), blk],
            out_specs=[blk, blk]),
        out_shape=[jax.ShapeDtypeStruct((N_CHIPS, r, c), F32), jax.ShapeDtypeStruct((N_CHIPS, r, c), GRAD_WIRE_DTYPE)],
        compiler_params=_params(("parallel", "parallel")),
    )(core, part, recv)


def _adam_update(w, g, m, v):
    nm = ADAM_B1 * m + (1.0 - ADAM_B1) * g
    nv = ADAM_B2 * v + (1.0 - ADAM_B2) * (g * g)
    m_hat = nm / (1.0 - ADAM_B1 ** ADAM_STEP)
    v_hat = nv / (1.0 - ADAM_B2 ** ADAM_STEP)
    return -ADAM_LR * (m_hat / (jnp.sqrt(v_hat) + ADAM_EPS) + ADAM_WD * w), nm, nv


def _sum_adamw(own, recv, chip, w, m, v, name, exchange=None):
    _, r, c = own.shape
    ct = _col_tile(r, c, 1 << 20)

    def body(chip_ref, o_ref, r1_ref, r2_ref, r3_ref, w_ref, m_ref, v_ref, g_ref, d_ref, nm_ref, nv_ref):
        g = ((o_ref[...] + r1_ref[...].astype(F32)) + r2_ref[...].astype(F32)) + r3_ref[...].astype(F32)
        g_ref[...] = g
        d_ref[...], nm_ref[...], nv_ref[...] = _adam_update(w_ref[...], g, m_ref[...], v_ref[...])

    def slot(k):
        return pl.BlockSpec((None, r, ct), lambda i, chip_ref: ((chip_ref[0] + k) % N_CHIPS, 0, i))

    spec = pl.BlockSpec((r, ct), lambda i, chip_ref: (0, i))
    shape = jax.ShapeDtypeStruct((r, c), F32)
    n_tiles = c // ct
    body, ex_in, ex_out, ex_shape, ex_sems = _riding(
        exchange, body, 8, 4, lambda: pl.program_id(0) == 0, lambda: pl.program_id(0) == n_tiles - 1)
    res = pl.pallas_call(
        body, name=name,
        grid_spec=pltpu.PrefetchScalarGridSpec(
            num_scalar_prefetch=1, grid=(n_tiles,),
            in_specs=[slot(0), slot(1), slot(2), slot(3), spec, spec, spec] + ex_in, out_specs=[spec] * 4 + ex_out,
            scratch_shapes=ex_sems),
        out_shape=[shape] * 4 + ex_shape,
        compiler_params=_params(("arbitrary",)),
    )(chip, own, recv, recv, recv, w, m, v, *(exchange["arrays"] if exchange else []))
    return res[:4], res[4:]


def _small_adamw(parts, ws, ms, vs):
    n = len(parts)

    def body(*refs):
        p_refs, w_refs, m_refs, v_refs = refs[:n], refs[n:2 * n], refs[2 * n:3 * n], refs[3 * n:4 * n]
        outs = refs[4 * n:]
        for i in range(n):
            g = p_refs[i][0]
            for k in range(1, N_DEV):
                g = g + p_refs[i][k]
            outs[i][...] = g
            outs[n + i][...], outs[2 * n + i][...], outs[3 * n + i][...] = _adam_update(
                w_refs[i][...], g, m_refs[i][...], v_refs[i][...])

    vmem = pl.BlockSpec(memory_space=pltpu.VMEM)
    shapes = [jax.ShapeDtypeStruct(w.shape, F32) for w in ws]
    res = pl.pallas_call(
        body, name="adamw_small", in_specs=[vmem] * (4 * n), out_specs=[vmem] * (4 * n), out_shape=shapes * 4,
        compiler_params=pltpu.CompilerParams(vmem_limit_bytes=VMEM_LIMIT),
    )(*parts, *ws, *ms, *vs)
    return res[:n], res[n:2 * n], res[2 * n:3 * n], res[3 * n:]


def _pack(arrs, dtype, row_multiple):
    flat = jnp.concatenate([a.reshape(-1).astype(dtype) for a in arrs])
    unit = LANES * row_multiple
    padded = -(-flat.shape[0] // unit) * unit
    return jnp.pad(flat, (0, padded - flat.shape[0])).reshape(-1, LANES)


def _unpack(packed, shapes, lead=()):
    flat = packed.reshape(lead + (-1,))
    out, off = [], 0
    for shp in shapes:
        n = math.prod(shp)
        out.append(flat[..., off:off + n].reshape(lead + tuple(shp)))
        off += n
    return out


def _gather_cols(g, lo, hi):
    width = g.shape[2]
    pieces = []
    for s in range(N_DEV):
        a, e = max(lo, s * width), min(hi, (s + 1) * width)
        if a < e:
            pieces.append(g[s, :, a - s * width:e - s * width])
    return pieces[0] if len(pieces) == 1 else jnp.concatenate(pieces, axis=1)


def _scatter_cols(segs, width):
    slots = []
    for k in range(N_DEV):
        lo, hi = k * width, (k + 1) * width
        pieces = []
        for arr, s_lo, s_hi in segs:
            a, e = max(lo, s_lo), min(hi, s_hi)
            if a < e:
                pieces.append(arr[:, a - s_lo:e - s_lo])
        slots.append(pieces[0] if len(pieces) == 1 else jnp.concatenate(pieces, axis=1))
    return jnp.stack(slots)


def _block_diag_groups(w):
    w4 = w.reshape(LRU_NGROUPS, 4, LRU_BLOCK, LRU_BLOCK)
    eye = jnp.eye(4, dtype=w.dtype)
    return jnp.einsum("gaij,ab->gaibj", w4, eye).reshape(LRU_NGROUPS, LRU_GROUP, LRU_GROUP)


def _block_diag_extract(wg):
    w5 = wg.reshape(LRU_NGROUPS, 4, LRU_BLOCK, 4, LRU_BLOCK)
    idx = jnp.arange(4)
    return w5[:, idx, :, idx, :].transpose(1, 0, 2, 3).reshape(LRU_BLOCKS, LRU_BLOCK, LRU_BLOCK)


BIG = ("w_in", "w_kv", "w_br_ssd", "w_br_lru", "w_br_mem", "w_out")
SMALL_SHARDED = ("ssd_conv_w", "ssd_norm_g", "lru_conv_w")
REPLICATED = ("norm_g", "ssd_conv_b", "ssd_dt_bias", "ssd_a_log", "ssd_d", "lru_conv_b", "lru_w_a", "lru_b_a",
              "lru_w_x", "lru_b_x", "lru_lambda", "mem_norm_g", "final_g")
WEIGHTS = ("norm_g", "w_in", "ssd_conv_w", "ssd_conv_b", "ssd_dt_bias", "ssd_a_log", "ssd_d", "ssd_norm_g", "lru_conv_w",
           "lru_conv_b", "lru_w_a", "lru_b_a", "lru_w_x", "lru_b_x", "lru_lambda", "mem_norm_g", "w_kv", "w_br_ssd",
           "w_br_lru", "w_br_mem", "w_out", "final_g")


def kernel(x, mem, norm_g, w_in, ssd_conv_w, ssd_conv_b, ssd_dt_bias, ssd_a_log, ssd_d, ssd_norm_g, lru_conv_w, lru_conv_b, lru_w_a, lru_b_a, lru_w_x, lru_b_x, lru_lambda, mem_norm_g, w_kv, w_br_ssd, w_br_lru, w_br_mem, w_out, final_g, loss_target, m_norm_g, m_w_in, m_ssd_conv_w, m_ssd_conv_b, m_ssd_dt_bias, m_ssd_a_log, m_ssd_d, m_ssd_norm_g, m_lru_conv_w, m_lru_conv_b, m_lru_w_a, m_lru_b_a, m_lru_w_x, m_lru_b_x, m_lru_lambda, m_mem_norm_g, m_w_kv, m_w_br_ssd, m_w_br_lru, m_w_br_mem, m_w_out, m_final_g, v_norm_g, v_w_in, v_ssd_conv_w, v_ssd_conv_b, v_ssd_dt_bias, v_ssd_a_log, v_ssd_d, v_ssd_norm_g, v_lru_conv_w, v_lru_conv_b, v_lru_w_a, v_lru_b_a, v_lru_w_x, v_lru_b_x, v_lru_lambda, v_mem_norm_g, v_w_kv, v_w_br_ssd, v_w_br_lru, v_w_br_mem, v_w_out, v_final_g):
    env = dict(locals())
    W = {n: env[n] for n in WEIGHTS}
    M = {n: env["m_" + n] for n in WEIGHTS}
    V = {n: env["v_" + n] for n in WEIGHTS}
    me = 4 * lax.axis_index("x") + 2 * lax.axis_index("y") + lax.axis_index("c")
    t = x.shape[1]
    xt = x[0]
    memt = mem[0]
    tgt = loss_target[0]

    small_shapes = [W[n].shape for n in SMALL_SHARDED]
    as2d = lambda d, n: jnp.transpose(d[n][0]) if n == "w_in" else d[n][0]
    h, (g_in,) = _rms_fwd(xt, norm_g, "norm_fwd", exchange=_all_gather_plan([as2d(W, "w_in").astype(MXU_DTYPE)]))
    b = SEG_BOUNDS
    w_in_t = g_in.reshape(IN_WIDTH, D_MODEL)
    w_ssd, w_lru, w_q, w_g = w_in_t[b[0]:b[1]], w_in_t[b[2]:b[3]], w_in_t[b[3]:b[4]], w_in_t[b[4]:b[5]]
    w_dt = jnp.pad(w_in_t[b[1]:b[2]], ((0, DT_PAD - SSD_HEADS), (0, 0)))

    later = _all_gather_plan([as2d(W, n).astype(MXU_DTYPE) for n in BIG[1:]] + [_pack([W[n] for n in SMALL_SHARDED], F32, 8)])
    proj_ssd, (g_kv, g_bs, g_bl, g_bm, g_out, gs) = _matmul(h, w_ssd, "nt", "proj_ssd", tm=4096, tn=512, exchange=later)
    g_cw, g_ng, g_lcw = _unpack(gs, small_shapes, (N_DEV,))
    cols = lambda a: jnp.moveaxis(a[:, 0], 0, -2).reshape(a.shape[2:-1] + (-1,))
    rows_ = lambda a: a.reshape((-1,) + a.shape[2:])
    w_bs_f, w_bl_f, w_bm_f, w_out_f = rows_(g_bs), rows_(g_bl), rows_(g_bm), rows_(g_out)
    conv_w_f, ssd_ng_f, lru_cw_f = cols(g_cw), cols(g_ng), cols(g_lcw)
    w_kv_f = _gather_cols(g_kv, 0, 2 * D_MODEL)

    pad_heads = lambda a: jnp.pad(a, ((0, 0), (0, LANES - SSD_HEADS)))
    dtb, alog = pad_heads(ssd_dt_bias), pad_heads(ssd_a_log)
    d_row = jnp.repeat(ssd_d, SSD_HEAD_DIM, axis=1)
    ng_row = ssd_ng_f.reshape(1, SSD_WIDTH)
    wa_g, wx_g = _block_diag_groups(lru_w_a[0]), _block_diag_groups(lru_w_x[0])
    ba, bx = lru_b_a.reshape(1, LRU_WIDTH), lru_b_x.reshape(1, LRU_WIDTH)
    fg = final_g.reshape(1, D_MODEL)

    proj_lru = _matmul(h, w_lru, "nt", "proj_lru", tm=4096, tn=512)
    proj_q = _matmul(h, w_q, "nt", "proj_q", tm=4096, tn=512)
    proj_g = _matmul(h, w_g, "nt", "proj_g", tm=4096, tn=512)
    proj_dt = _matmul(h, w_dt, "nt", "proj_dt", tm=4096)
    mem_n = _rms_fwd(memt, mem_norm_g, "mem_norm_fwd")
    kv = _matmul(mem_n, w_kv_f, "nn", "mem_kv")
    yssd, y_scan, states, ssd_pre = _ssd_fwd(proj_ssd, proj_dt, conv_w_f, ssd_conv_b, dtb, alog, d_row, ng_row)
    ylru, h_lru, xl_lru = _lru_fwd(proj_lru, lru_cw_f, lru_conv_b, wa_g, wx_g, ba, bx, lru_lambda)
    ymem = _mem_fwd(proj_q, kv)
    ps, pl_, pm, merged, dx2, loss_vec, g_fg = _merge_fwd(xt, yssd, ylru, ymem, proj_g, w_bs_f, w_bl_f, w_bm_f, w_out_f, fg, tgt)

    d_g, dps, dpl, dpm, dyssd, dylru, dymem = _merge_bwd(dx2, proj_g, ps, pl_, pm, w_bs_f, w_bl_f, w_bm_f, w_out_f)
    gw_out = _matmul(merged, dx2, "tn", "grad_w_out", tk=2048)
    gw_bs = _matmul(yssd, dps, "tn", "grad_w_br_ssd", tm=2048)
    gw_bl = _matmul(ylru, dpl, "tn", "grad_w_br_lru", tm=LRU_WIDTH, tk=2048)
    gw_bm = _matmul(ymem, dpm, "tn", "grad_w_br_mem", tk=2048)
    d_q, d_kv = _mem_bwd(proj_q, kv, dymem)
    gw_kv = _matmul(mem_n, d_kv, "tn", "grad_w_kv")
    d_memn = _matmul(d_kv, w_kv_f, "nt", "d_mem_n")
    _, g_memng = _rms_bwd(memt, d_memn, None, mem_norm_g, "mem_norm_bwd")

    core = lax.axis_index("c").astype(jnp.int32).reshape(1)
    chip = (2 * lax.axis_index("x") + lax.axis_index("y")).astype(jnp.int32).reshape(1)
    by_chip = lambda a: a.reshape((N_CHIPS, 2, -1) + a.shape[1:])
    early = ("w_kv", "w_br_ssd", "w_br_lru", "w_br_mem", "w_out")
    early_parts = [by_chip(_scatter_cols([(gw_kv, 0, 2 * D_MODEL)], 2 * D_MODEL // N_DEV).reshape(-1, 2 * D_MODEL // N_DEV)),
                   by_chip(gw_bs), by_chip(gw_bl), by_chip(gw_bm), by_chip(gw_out)]
    (d_lru, gl_cw, gl_cb, g_ba, g_bx, g_lam, gwa_g, gwx_g), early_sib = _lru_bwd(
        proj_lru, xl_lru, h_lru, dylru, lru_cw_f, wa_g, wx_g, ba, bx, lru_lambda, exchange=_pair_plan(early_parts))
    early_sums = [_chip_sum(p, r, core, "chip_sum_" + n) for n, p, r in zip(early, early_parts, early_sib)]
    (d_ssd, d_dt, gs_cw, gs_cb, g_dtb, g_alog, g_dch, g_ngrow), early_recv = _ssd_bwd(
        proj_ssd, ssd_pre, proj_dt, y_scan, states, dyssd, conv_w_f, dtb, alog, d_row, ng_row,
        exchange=_chip_plan([s16 for _, s16 in early_sums]))
    gw_ssd = _matmul(d_ssd, h, "tn", "grad_w_in_ssd", tm=2560)
    gw_lru = _matmul(d_lru, h, "tn", "grad_w_in_lru", tm=1536, tk=2048)
    gw_q = _matmul(d_q, h, "tn", "grad_w_in_q", tk=2048)
    gw_g = _matmul(d_g, h, "tn", "grad_w_in_g", tm=1536, tk=2048)
    gw_dt = _matmul(d_dt, h, "tn", "grad_w_in_dt", tk=2048)
    in_part = by_chip(jnp.concatenate([gw_ssd, gw_dt[:SSD_HEADS], gw_lru, gw_q, gw_g], axis=0))
    (in_sib,) = _run_exchange(_pair_plan([in_part]), "grad_pair_exchange")
    in_sum = _chip_sum(in_part, in_sib, core, "chip_sum_w_in")

    small_grads = {
        "ssd_conv_w": gs_cw, "ssd_conv_b": gs_cb, "ssd_dt_bias": g_dtb[:, :SSD_HEADS],
        "ssd_a_log": g_alog[:, :SSD_HEADS], "ssd_d": jnp.sum(g_dch.reshape(SSD_HEADS, SSD_HEAD_DIM), axis=1).reshape(1, SSD_HEADS),
        "ssd_norm_g": g_ngrow.reshape(SSD_GROUPS, -1), "lru_conv_w": gl_cw, "lru_conv_b": gl_cb,
        "lru_w_a": _block_diag_extract(gwa_g), "lru_b_a": g_ba, "lru_w_x": _block_diag_extract(gwx_g), "lru_b_x": g_bx,
        "lru_lambda": g_lam, "mem_norm_g": g_memng, "final_g": g_fg,
    }
    small_all = REPLICATED + SMALL_SHARDED

    def small_shape(n, shards):
        shp = W[n].shape[1:] if W[n].ndim > 2 else (1, W[n].shape[-1])
        return shp[:-1] + (shp[-1] * shards,)

    riders = tuple(small_grads)
    small_plan = _all_gather_plan([small_grads[n].reshape(small_shape(n, N_DEV if n in SMALL_SHARDED else 1)) for n in riders])
    dh, landed = _dh([(d_ssd, w_ssd), (d_lru, w_lru), (d_q, w_q), (d_g, w_g), (d_dt, w_dt)],
                     exchange=_both(_chip_plan([in_sum[1]]), small_plan))
    grad_x, g_normg = _rms_bwd(xt, dh, dx2, norm_g, "norm_bwd")
    small_recv = dict(zip(riders, landed[1:]))
    reduced = {"w_in": (in_sum[0], landed[0]), **{n: (s[0], r) for n, s, r in zip(early, early_sums, early_recv)}}

    grads, delta, new_m, new_v = {}, {}, {}, {}
    for n in BIG:
        s32, recv = reduced[n]
        res, landed = _sum_adamw(s32, recv, chip, as2d(W, n), as2d(M, n), as2d(V, n), "adamw_" + n,
                                 exchange=_all_gather_plan([g_normg]) if n == "w_in" else None)
        if n == "w_in":
            small_recv["norm_g"] = landed[0]
        for dst, a in zip((grads, delta, new_m, new_v), res):
            dst[n] = (jnp.transpose(a) if n == "w_in" else a)[None]

    parts = []
    for n in small_all:
        a = small_recv[n]
        if n in SMALL_SHARDED:
            width = W[n].shape[-1]
            a = lax.dynamic_slice_in_dim(a, me * width, width, axis=a.ndim - 1)
        parts.append(a)
    canon = lambda d: [d[n].reshape(small_shape(n, 1)) for n in small_all]
    for dst, res in zip((grads, delta, new_m, new_v), _small_adamw(parts, canon(W), canon(M), canon(V))):
        for n, a in zip(small_all, res):
            dst[n] = a.reshape(W[n].shape)

    loss = lax.psum(loss_vec[0, 0], ("x", "y", "c"))
    return (loss, grad_x[None], *[grads[n] for n in WEIGHTS], *[delta[n] for n in WEIGHTS],
            *[new_m[n] for n in WEIGHTS], *[new_v[n] for n in WEIGHTS])
```

```python
import functools
import math

import jax
import jax.numpy as jnp
from jax import lax
from jax.experimental import pallas as pl
from jax.experimental.pallas import tpu as pltpu

F32 = jnp.float32
MXU_DTYPE = jnp.bfloat16
GRAD_WIRE_DTYPE = jnp.bfloat16

D_MODEL = 1024
EPS = 1e-6
CONV_WIDTH = 4
SSD_WIDTH = 2048
SSD_HEAD_DIM = 64
SSD_HEADS = 32
SSD_GROUPS = 4
SSD_STATE = 128
SSD_CHUNK = 128
SSD_BC = SSD_GROUPS * SSD_STATE
SSD_CONV_CH = SSD_WIDTH + 2 * SSD_BC
SSD_PAIRS = SSD_HEADS // 2
PAIRS_PER_GROUP = SSD_PAIRS // SSD_GROUPS
GROUP_COLS = SSD_WIDTH // SSD_GROUPS
LRU_WIDTH = 1536
LRU_BLOCKS = 16
LRU_BLOCK = 96
LRU_GROUP = 4 * LRU_BLOCK
LRU_NGROUPS = LRU_WIDTH // LRU_GROUP
LRU_C = 8.0
LRU_ROWS = 256
MEM_HEADS = 4
MEM_HEAD_DIM = 256
IN_WIDTH = 12320
N_DEV = 8
LANES = 128
SSD_SEG = SSD_WIDTH + SSD_CONV_CH
DT_PAD = LANES
SEG_BOUNDS = (0, 5120, 5152, 8224, 9248, 12320)

ADAM_LR = 0.001
ADAM_B1 = 0.9
ADAM_B2 = 0.999
ADAM_EPS = 1e-08
ADAM_WD = 0.01
ADAM_STEP = 10

VMEM_LIMIT = 56 * 1024 * 1024

NN = (((1,), (0,)), ((), ()))
NT = (((1,), (1,)), ((), ()))
TN = (((0,), (0,)), ((), ()))


def _dot(a, b, dims):
    return lax.dot_general(a.astype(MXU_DTYPE), b.astype(MXU_DTYPE), dims, preferred_element_type=F32)


def _sigmoid(x):
    return 0.5 * jnp.tanh(0.5 * x) + 0.5


def _log1p(e):
    u = 1.0 + e
    return jnp.where(u == 1.0, e, jnp.log(u) * (e / jnp.where(u == 1.0, 1.0, u - 1.0)))


def _softplus(x):
    return jnp.maximum(x, 0.0) + _log1p(jnp.exp(-jnp.abs(x)))


def _params(semantics):
    return pltpu.CompilerParams(dimension_semantics=semantics, vmem_limit_bytes=VMEM_LIMIT)


def _shift_down(cur, halo8, k):
    rolled = pltpu.roll(cur, k, 0)
    row8 = lax.broadcasted_iota(jnp.int32, halo8.shape, 0)
    top = jnp.where(row8 >= k, rolled[0:8], pltpu.roll(halo8, k, 0))
    return jnp.concatenate([top, rolled[8:]], axis=0)


def _shift_up(cur, next8, k):
    rows = cur.shape[0]
    rolled = pltpu.roll(cur, rows - k, 0)
    row8 = lax.broadcasted_iota(jnp.int32, next8.shape, 0)
    bot = jnp.where(row8 < 8 - k, rolled[rows - 8:rows], pltpu.roll(next8, 8 - k, 0))
    return jnp.concatenate([rolled[:rows - 8], bot], axis=0)


def _causal_conv(raw, halo8, w, b):
    acc = raw * w[3:4, :] + b
    for k in range(1, CONV_WIDTH):
        acc = acc + _shift_down(raw, halo8, k) * w[3 - k:4 - k, :]
    return acc


def _conv_backward(dco, next8, raw, w):
    d_raw = dco * w[3:4, :]
    gw = [None] * CONV_WIDTH
    gw[3] = jnp.sum(dco * raw, axis=0, keepdims=True)
    for j in range(1, CONV_WIDTH):
        up = _shift_up(dco, next8, j)
        d_raw = d_raw + up * w[3 - j:4 - j, :]
        gw[3 - j] = jnp.sum(up * raw, axis=0, keepdims=True)
    gb = jnp.sum(dco, axis=0, keepdims=True)
    return d_raw, gw, gb


def _cumsum_rows(v):
    rows = v.shape[0]
    row = lax.broadcasted_iota(jnp.int32, v.shape, 0)
    s = 1
    while s < rows:
        v = v + jnp.where(row >= s, pltpu.roll(v, s, 0), 0.0)
        s *= 2
    return v


def _rev_cumsum_rows(v):
    rows = v.shape[0]
    row = lax.broadcasted_iota(jnp.int32, v.shape, 0)
    s = 1
    while s < rows:
        v = v + jnp.where(row < rows - s, pltpu.roll(v, rows - s, 0), 0.0)
        s *= 2
    return v


def _matmul(a, b, mode, name, tm=1024, tn=1024, tk=1024, exchange=None):
    if mode == "nn":
        (m, kk), n = a.shape, b.shape[1]
    elif mode == "nt":
        (m, kk), n = a.shape, b.shape[0]
    else:
        (kk, m), n = a.shape, b.shape[1]
    tm, tn, tk = min(tm, m), min(tn, n), min(tk, kk)
    assert m % tm == 0 and n % tn == 0 and kk % tk == 0, (name, a.shape, b.shape)
    nk = kk // tk
    dims = {"nn": NN, "nt": NT, "tn": TN}[mode]
    a_spec = pl.BlockSpec((tk, tm), lambda i, j, k: (k, i)) if mode == "tn" else pl.BlockSpec((tm, tk), lambda i, j, k: (i, k))
    b_spec = pl.BlockSpec((tn, tk), lambda i, j, k: (j, k)) if mode == "nt" else pl.BlockSpec((tk, tn), lambda i, j, k: (k, j))
    o_spec = pl.BlockSpec((tm, tn), lambda i, j, k: (i, j))

    def body_single(a_ref, b_ref, o_ref):
        o_ref[...] = _dot(a_ref[...], b_ref[...], dims)

    def body(a_ref, b_ref, o_ref, acc_ref):
        k = pl.program_id(2)

        @pl.when(k == 0)
        def _():
            acc_ref[...] = jnp.zeros_like(acc_ref)

        acc_ref[...] += _dot(a_ref[...], b_ref[...], dims)

        @pl.when(k == nk - 1)
        def _():
            o_ref[...] = acc_ref[...]

    grid = (m // tm, n // tn, nk)
    if exchange is None:
        return pl.pallas_call(
            body_single if nk == 1 else body, name=name, grid=grid, in_specs=[a_spec, b_spec], out_specs=o_spec,
            out_shape=jax.ShapeDtypeStruct((m, n), F32),
            scratch_shapes=[] if nk == 1 else [pltpu.VMEM((tm, tn), F32)],
            compiler_params=_params(("parallel", "parallel", "arbitrary")),
        )(a, b)
    at = lambda ids: functools.reduce(lambda u, v: u & v, [pl.program_id(d) == ids[d] for d in range(3)])
    riding, ex_in, ex_out, ex_shape, ex_sems = _riding(
        exchange, body_single if nk == 1 else body, 2, 1, lambda: at((0, 0, 0)), lambda: at(tuple(g - 1 for g in grid)))
    res = pl.pallas_call(
        riding, name=name, grid=grid, in_specs=[a_spec, b_spec] + ex_in, out_specs=[o_spec] + ex_out,
        out_shape=[jax.ShapeDtypeStruct((m, n), F32)] + ex_shape,
        scratch_shapes=([] if nk == 1 else [pltpu.VMEM((tm, tn), F32)]) + ex_sems,
        compiler_params=_params(("arbitrary", "arbitrary", "arbitrary")),
    )(a, b, *exchange["arrays"])
    return res[0], res[1:]


def _rms_fwd(x, g, name, rows=512, exchange=None):
    t, d = x.shape
    rows = min(rows, t)
    n_tiles = t // rows

    def body(x_ref, g_ref, h_ref):
        xv = x_ref[...]
        r = lax.rsqrt(jnp.mean(xv * xv, axis=-1, keepdims=True) + EPS)
        h_ref[...] = ((xv * r) * g_ref[...]).astype(h_ref.dtype)

    body, ex_in, ex_out, ex_shape, ex_sems = _riding(
        exchange, body, 2, 1, lambda: pl.program_id(0) == 0, lambda: pl.program_id(0) == n_tiles - 1)
    res = pl.pallas_call(
        body, name=name, grid=(n_tiles,),
        in_specs=[pl.BlockSpec((rows, d), lambda i: (i, 0)), pl.BlockSpec((1, d), lambda i: (0, 0))] + ex_in,
        out_specs=[pl.BlockSpec((rows, d), lambda i: (i, 0))] + ex_out,
        out_shape=[jax.ShapeDtypeStruct((t, d), MXU_DTYPE)] + ex_shape,
        scratch_shapes=ex_sems,
        compiler_params=_params(("arbitrary",) if exchange else ("parallel",)),
    )(x, g, *(exchange["arrays"] if exchange else []))
    return (res[0], res[1:]) if exchange else res[0]


def _rms_bwd(x, dh, dres, g, name, rows=512):
    t, d = x.shape
    rows = min(rows, t)
    has_res = dres is not None

    def body(*refs):
        if has_res:
            x_ref, dh_ref, dr_ref, g_ref, dx_ref, gg_ref = refs
        else:
            x_ref, dh_ref, g_ref, dx_ref, gg_ref = refs

        @pl.when(pl.program_id(0) == 0)
        def _():
            gg_ref[...] = jnp.zeros_like(gg_ref)

        xv = x_ref[...]
        dhv = dh_ref[...]
        r = lax.rsqrt(jnp.mean(xv * xv, axis=-1, keepdims=True) + EPS)
        n = xv * r
        dn = dhv * g_ref[...]
        dx = r * (dn - n * jnp.mean(dn * n, axis=-1, keepdims=True))
        if has_res:
            dx = dx + dr_ref[...]
        dx_ref[...] = dx
        gg_ref[...] += jnp.sum(dhv * n, axis=0, keepdims=True)

    row_spec = pl.BlockSpec((rows, d), lambda i: (i, 0))
    vec_spec = pl.BlockSpec((1, d), lambda i: (0, 0))
    args = (x, dh) + ((dres,) if has_res else ()) + (g,)
    return pl.pallas_call(
        body, name=name, grid=(t // rows,),
        in_specs=[row_spec, row_spec] + ([row_spec] if has_res else []) + [vec_spec],
        out_specs=[row_spec, vec_spec],
        out_shape=[jax.ShapeDtypeStruct((t, d), F32), jax.ShapeDtypeStruct((1, d), F32)],
        compiler_params=_params(("arbitrary",)),
    )(*args)


def _dh(segs, rows=1024, tk=1024, exchange=None):
    t, d = segs[0][0].shape[0], segs[0][1].shape[1]
    rows = min(rows, t)
    steps = []
    step0 = 0
    for a, _ in segs:
        kb = min(tk, a.shape[1])
        assert a.shape[1] % kb == 0, a.shape
        steps.append((step0, a.shape[1] // kb, kb))
        step0 += a.shape[1] // kb
    n_steps = step0
    ns = len(segs)

    def body(*refs):
        a_refs, w_refs = refs[0:2 * ns:2], refs[1:2 * ns:2]
        dh_ref, acc_ref = refs[2 * ns:]
        k = pl.program_id(1)

        @pl.when(k == 0)
        def _():
            acc_ref[...] = jnp.zeros_like(acc_ref)

        for s, (first, nblk, _) in enumerate(steps):
            @pl.when((k >= first) & (k < first + nblk))
            def _(s=s):
                acc_ref[...] += _dot(a_refs[s][...], w_refs[s][...], NN)

        @pl.when(k == n_steps - 1)
        def _():
            dh_ref[...] = acc_ref[...]

    in_specs, args = [], []
    for (a, w), (first, nblk, kb) in zip(segs, steps):
        blk = lambda k, first=first, nblk=nblk: jnp.clip(k - first, 0, nblk - 1)
        in_specs.append(pl.BlockSpec((rows, kb), lambda i, k, blk=blk: (i, blk(k))))
        in_specs.append(pl.BlockSpec((kb, d), lambda i, k, blk=blk: (blk(k), 0)))
        args += [a, w]
    row_spec = pl.BlockSpec((rows, d), lambda i, k: (i, 0))
    n_tiles = t // rows
    body, ex_in, ex_out, ex_shape, ex_sems = _riding(
        exchange, body, 2 * ns, 1,
        lambda: (pl.program_id(0) == 0) & (pl.program_id(1) == 0),
        lambda: (pl.program_id(0) == n_tiles - 1) & (pl.program_id(1) == n_steps - 1))
    res = pl.pallas_call(
        body, name="dh", grid=(n_tiles, n_steps),
        in_specs=in_specs + ex_in, out_specs=[row_spec] + ex_out,
        out_shape=[jax.ShapeDtypeStruct((t, d), F32)] + ex_shape,
        scratch_shapes=[pltpu.VMEM((rows, d), F32)] + ex_sems,
        compiler_params=_params(("arbitrary", "arbitrary")),
    )(*args, *(exchange["arrays"] if exchange else []))
    return res[0], res[1:]


def _pair_select(lo, m, h0):
    return jnp.where(lo, m[:, h0:h0 + 1], m[:, h0 + 1:h0 + 2])


def _group_select(lo, m, heads):
    return jnp.concatenate([_pair_select(lo, m, h0) for h0 in heads], axis=1)


def _halves(lo, v):
    return (jnp.sum(jnp.where(lo, v, 0.0), axis=1, keepdims=True),
            jnp.sum(jnp.where(lo, 0.0, v), axis=1, keepdims=True))


def _ssd_common(dt_raw, dtb, alog):
    dt = _softplus(dt_raw + dtb)
    aneg = -jnp.exp(alog)
    a_cs = _cumsum_rows(dt * aneg)
    return dt, aneg, a_cs, a_cs.T


def _ssd_specs(nc, rev):
    cidx = (lambda c: nc - 1 - c) if rev else (lambda c: c)
    L = SSD_CHUNK
    b_proj = 2 * SSD_WIDTH // SSD_BC
    b_conv = SSD_WIDTH // SSD_BC
    return dict(
        z=pl.BlockSpec((L, SSD_WIDTH), lambda c: (cidx(c), 0)),
        xr=pl.BlockSpec((L, SSD_WIDTH), lambda c: (cidx(c), 1)),
        br=pl.BlockSpec((L, SSD_BC), lambda c: (cidx(c), b_proj)),
        cr=pl.BlockSpec((L, SSD_BC), lambda c: (cidx(c), b_proj + 1)),
        dt=pl.BlockSpec((L, DT_PAD), lambda c: (cidx(c), 0)),
        cwx=pl.BlockSpec((CONV_WIDTH, SSD_WIDTH), lambda c: (0, 0)),
        cwb=pl.BlockSpec((CONV_WIDTH, SSD_BC), lambda c: (0, b_conv)),
        cwc=pl.BlockSpec((CONV_WIDTH, SSD_BC), lambda c: (0, b_conv + 1)),
        cbx=pl.BlockSpec((1, SSD_WIDTH), lambda c: (0, 0)),
        cbb=pl.BlockSpec((1, SSD_BC), lambda c: (0, b_conv)),
        cbc=pl.BlockSpec((1, SSD_BC), lambda c: (0, b_conv + 1)),
        vec128=pl.BlockSpec((1, LANES), lambda c: (0, 0)),
        vecw=pl.BlockSpec((1, SSD_WIDTH), lambda c: (0, 0)),
        wide=pl.BlockSpec((L, SSD_WIDTH), lambda c: (cidx(c), 0)),
        states=pl.BlockSpec((1, SSD_GROUPS, GROUP_COLS, SSD_STATE), lambda c: (cidx(c), 0, 0, 0)),
    )


def _ssd_fwd(proj_ssd, dt_p, conv_w, conv_b, dtb, alog, d_row, ng_row):
    t = proj_ssd.shape[0]
    nc = t // SSD_CHUNK
    L = SSD_CHUNK
    sp = _ssd_specs(nc, False)

    def body(z_ref, xr_ref, br_ref, cr_ref, dt_ref, cwx_ref, cwb_ref, cwc_ref, cbx_ref, cbb_ref, cbc_ref,
             dtb_ref, alog_ref, d_ref, ng_ref, yssd_ref, y_ref, st_ref, pre_ref,
             hx_ref, hb_ref, hc_ref, state_ref, yacc_ref):
        @pl.when(pl.program_id(0) == 0)
        def _():
            hx_ref[...] = jnp.zeros_like(hx_ref)
            hb_ref[...] = jnp.zeros_like(hb_ref)
            hc_ref[...] = jnp.zeros_like(hc_ref)
            state_ref[...] = jnp.zeros_like(state_ref)

        xr, br, cr = xr_ref[...], br_ref[...], cr_ref[...]
        px = _causal_conv(xr, hx_ref[...], cwx_ref[...], cbx_ref[...])
        pb = _causal_conv(br, hb_ref[...], cwb_ref[...], cbb_ref[...])
        pc = _causal_conv(cr, hc_ref[...], cwc_ref[...], cbc_ref[...])
        hx_ref[...] = xr[L - 8:L, :]
        hb_ref[...] = br[L - 8:L, :]
        hc_ref[...] = cr[L - 8:L, :]
        pre_ref[:, 0:SSD_WIDTH] = px
        pre_ref[:, SSD_WIDTH:SSD_WIDTH + SSD_BC] = pb
        pre_ref[:, SSD_WIDTH + SSD_BC:SSD_CONV_CH] = pc
        xs = px * _sigmoid(px)
        bm = pb * _sigmoid(pb)
        cm = pc * _sigmoid(pc)

        dt, _, a_cs, a_t = _ssd_common(dt_ref[...], dtb_ref[...], alog_ref[...])
        exp_a = jnp.exp(a_cs)
        a_last = a_cs[L - 1:L, :]
        dte = jnp.exp(a_last - a_cs)
        dec = jnp.exp(a_last)

        lane = lax.broadcasted_iota(jnp.int32, (L, LANES), 1)
        sub = lax.broadcasted_iota(jnp.int32, (L, LANES), 0)
        lo = lane < SSD_HEAD_DIM
        causal = sub >= lane
        top = sub < SSD_HEAD_DIM

        for g in range(SSD_GROUPS):
            b_g = bm[:, g * SSD_STATE:(g + 1) * SSD_STATE]
            c_g = cm[:, g * SSD_STATE:(g + 1) * SSD_STATE]
            cb = _dot(c_g, b_g, NT)
            heads = [2 * (g * PAIRS_PER_GROUP + jj) for jj in range(PAIRS_PER_GROUP)]
            gcols = slice(g * GROUP_COLS, (g + 1) * GROUP_COLS)
            xs_g = xs[:, gcols]
            xdt_g = xs_g * _group_select(lo, dt, heads)
            h_g = state_ref[g]
            st_ref[0, g] = h_g
            y_off_g = _dot(c_g, h_g, NT) * _group_select(lo, exp_a, heads)
            s_new_g = _dot(xdt_g * _group_select(lo, dte, heads), b_g, TN)
            for jj, h0 in enumerate(heads):
                blk = slice(jj * LANES, (jj + 1) * LANES)
                cols = slice(g * GROUP_COLS + jj * LANES, g * GROUP_COLS + (jj + 1) * LANES)
                xdt = xdt_g[:, blk]
                g0 = jnp.where(causal, jnp.exp(a_cs[:, h0:h0 + 1] - a_t[h0:h0 + 1, :]), 0.0) * cb
                g1 = jnp.where(causal, jnp.exp(a_cs[:, h0 + 1:h0 + 2] - a_t[h0 + 1:h0 + 2, :]), 0.0) * cb
                lhs = jnp.concatenate([g0, g1], axis=1)
                rhs = jnp.concatenate([jnp.where(lo, xdt, 0.0), jnp.where(lo, 0.0, xdt)], axis=0)
                y_diag = _dot(lhs, rhs, NN)
                dec_rows = jnp.where(top, dec[:, h0:h0 + 1], dec[:, h0 + 1:h0 + 2])
                state_ref[g, blk, :] = h_g[blk, :] * dec_rows + s_new_g[blk, :]
                yacc_ref[:, cols] = (y_diag + y_off_g[:, blk]) + xs_g[:, blk] * d_ref[:, cols]

        y = yacc_ref[...]
        y_ref[...] = y
        zz = z_ref[...]
        y2 = y * (zz * _sigmoid(zz))
        gw = SSD_WIDTH // SSD_GROUPS
        for g in range(SSD_GROUPS):
            seg = y2[:, g * gw:(g + 1) * gw]
            r = lax.rsqrt(jnp.mean(seg * seg, axis=-1, keepdims=True) + EPS)
            yssd_ref[:, g * gw:(g + 1) * gw] = ((seg * r) * ng_ref[:, g * gw:(g + 1) * gw]).astype(yssd_ref.dtype)

    return pl.pallas_call(
        body, name="ssd_fwd", grid=(nc,),
        in_specs=[sp["z"], sp["xr"], sp["br"], sp["cr"], sp["dt"], sp["cwx"], sp["cwb"], sp["cwc"],
                  sp["cbx"], sp["cbb"], sp["cbc"], sp["vec128"], sp["vec128"], sp["vecw"], sp["vecw"]],
        out_specs=[sp["wide"], sp["wide"], sp["states"], pl.BlockSpec((L, SSD_CONV_CH), lambda c: (c, 0))],
        out_shape=[jax.ShapeDtypeStruct((t, SSD_WIDTH), MXU_DTYPE), jax.ShapeDtypeStruct((t, SSD_WIDTH), F32),
                   jax.ShapeDtypeStruct((nc, SSD_GROUPS, GROUP_COLS, SSD_STATE), F32), jax.ShapeDtypeStruct((t, SSD_CONV_CH), F32)],
        scratch_shapes=[pltpu.VMEM((8, SSD_WIDTH), F32), pltpu.VMEM((8, SSD_BC), F32), pltpu.VMEM((8, SSD_BC), F32),
                        pltpu.VMEM((SSD_GROUPS, GROUP_COLS, SSD_STATE), F32), pltpu.VMEM((L, SSD_WIDTH), F32)],
        compiler_params=_params(("arbitrary",)),
    )(proj_ssd, proj_ssd, proj_ssd, proj_ssd, dt_p, conv_w, conv_w, conv_w, conv_b, conv_b, conv_b,
      dtb, alog, d_row, ng_row)


def _ssd_bwd(proj_ssd, pre, dt_p, y, states, dyssd, conv_w, dtb, alog, d_row, ng_row, exchange=None):
    t = proj_ssd.shape[0]
    nc = t // SSD_CHUNK
    L = SSD_CHUNK
    sp = _ssd_specs(nc, True)

    def pre_spec(width, col):
        return pl.BlockSpec((L, width), lambda c: (nc - 1 - c, col))

    def body(z_ref, xr_ref, br_ref, cr_ref, px_ref, pb_ref, pc_ref, dt_ref, y_ref, st_ref, dy_ref,
             cwx_ref, cwb_ref, cwc_ref, dtb_ref, alog_ref, d_ref, ng_ref,
             dssd_ref, ddt_ref, gcw_ref, gcb_ref, gdtb_ref, galog_ref, gd_ref, gng_ref,
             gn_ref, nx_ref, nb_ref, ncc_ref, dxs_ref):
        step = pl.program_id(0)

        @pl.when(step == 0)
        def _():
            gn_ref[...] = jnp.zeros_like(gn_ref)
            nx_ref[...] = jnp.zeros_like(nx_ref)
            nb_ref[...] = jnp.zeros_like(nb_ref)
            ncc_ref[...] = jnp.zeros_like(ncc_ref)
            for ref in (gcw_ref, gcb_ref, gdtb_ref, galog_ref, gd_ref, gng_ref):
                ref[...] = jnp.zeros_like(ref)

        xr, br, cr = xr_ref[...], br_ref[...], cr_ref[...]
        cwx, cwb, cwc = cwx_ref[...], cwb_ref[...], cwc_ref[...]
        px, pb, pc = px_ref[...], pb_ref[...], pc_ref[...]
        sx, sb, sc = _sigmoid(px), _sigmoid(pb), _sigmoid(pc)
        xs, bm, cm = px * sx, pb * sb, pc * sc

        dt_in = dt_ref[...] + dtb_ref[...]
        dt, aneg, a_cs, a_t = _ssd_common(dt_ref[...], dtb_ref[...], alog_ref[...])
        exp_a = jnp.exp(a_cs)
        a_last = a_cs[L - 1:L, :]
        dte = jnp.exp(a_last - a_cs)
        dec = jnp.exp(a_last)

        lane = lax.broadcasted_iota(jnp.int32, (L, LANES), 1)
        sub = lax.broadcasted_iota(jnp.int32, (L, LANES), 0)
        lo = lane < SSD_HEAD_DIM
        causal = sub >= lane
        top = sub < SSD_HEAD_DIM
        last_row = sub == L - 1

        yv = y_ref[...]
        zz = z_ref[...]
        sz = _sigmoid(zz)
        silz = zz * sz
        y2 = yv * silz
        dyv = dy_ref[...]
        gw = SSD_WIDTH // SSD_GROUPS
        d_y2_parts = []
        gng_parts = []
        for g in range(SSD_GROUPS):
            seg = y2[:, g * gw:(g + 1) * gw]
            dseg = dyv[:, g * gw:(g + 1) * gw]
            r = lax.rsqrt(jnp.mean(seg * seg, axis=-1, keepdims=True) + EPS)
            n = seg * r
            dn = dseg * ng_ref[:, g * gw:(g + 1) * gw]
            gng_parts.append(jnp.sum(dseg * n, axis=0, keepdims=True))
            d_y2_parts.append(r * (dn - n * jnp.mean(dn * n, axis=-1, keepdims=True)))
        d_y2 = jnp.concatenate(d_y2_parts, axis=1)
        gng_ref[...] += jnp.concatenate(gng_parts, axis=1)
        d_y = d_y2 * silz
        dssd_ref[:, 0:SSD_WIDTH] = (d_y2 * yv * (sz * (1.0 + zz * (1.0 - sz)))).astype(dssd_ref.dtype)
        gd_ref[...] += jnp.sum(d_y * xs, axis=0, keepdims=True)
        dxs_ref[...] = d_y * d_ref[...]

        d_a = jnp.zeros((L, LANES), F32)
        d_at = jnp.zeros((LANES, L), F32)
        ddt = jnp.zeros((L, LANES), F32)
        d_b_parts, d_c_parts = [], []
        for g in range(SSD_GROUPS):
            b_g = bm[:, g * SSD_STATE:(g + 1) * SSD_STATE]
            c_g = cm[:, g * SSD_STATE:(g + 1) * SSD_STATE]
            cb = _dot(c_g, b_g, NT)
            d_cb = jnp.zeros((L, L), F32)
            heads = [2 * (g * PAIRS_PER_GROUP + jj) for jj in range(PAIRS_PER_GROUP)]
            gcols = slice(g * GROUP_COLS, (g + 1) * GROUP_COLS)
            dy_g, xs_g = d_y[:, gcols], xs[:, gcols]
            dt_g = _group_select(lo, dt, heads)
            expa_g = _group_select(lo, exp_a, heads)
            dte_g = _group_select(lo, dte, heads)
            xdt_g = xs_g * dt_g
            h_g = st_ref[0, g]
            gn_g = gn_ref[g]
            dys_g = dy_g * expa_g
            d_cg = _dot(dys_g, h_g, NN)
            d_h_g = _dot(dys_g, c_g, TN)
            t1_g = dy_g * _dot(c_g, h_g, NT) * expa_g
            d_bg = _dot(xdt_g * dte_g, gn_g, NN)
            dxdt_g = _dot(b_g, gn_g, NT) * dte_g
            t2_g = dxdt_g * xdt_g
            t12_g = t1_g - t2_g
            gh_g = jnp.sum(gn_g * h_g, axis=1, keepdims=True)
            for jj, h0 in enumerate(heads):
                blk = slice(jj * LANES, (jj + 1) * LANES)
                cols = slice(g * GROUP_COLS + jj * LANES, g * GROUP_COLS + (jj + 1) * LANES)
                dy_p, xs_p, xdt, dt_pp = dy_g[:, blk], xs_g[:, blk], xdt_g[:, blk], dt_g[:, blk]
                l0 = jnp.where(causal, jnp.exp(a_cs[:, h0:h0 + 1] - a_t[h0:h0 + 1, :]), 0.0)
                l1 = jnp.where(causal, jnp.exp(a_cs[:, h0 + 1:h0 + 2] - a_t[h0 + 1:h0 + 2, :]), 0.0)
                g0, g1 = l0 * cb, l1 * cb
                dcat = jnp.concatenate([jnp.where(lo, dy_p, 0.0), jnp.where(lo, 0.0, dy_p)], axis=0)
                d_xdt = dxdt_g[:, blk] + _dot(jnp.concatenate([g0, g1], axis=0), dcat, TN)
                dm = _dot(dcat, xdt, NT)
                dm0, dm1 = dm[0:L], dm[L:2 * L]
                d_cb = d_cb + (l0 * dm0 + l1 * dm1)
                e0, e1 = dm0 * g0, dm1 * g1
                a0, a1 = _halves(lo, t12_g[:, blk])
                a0 = a0 + jnp.sum(e0, axis=1, keepdims=True)
                a1 = a1 + jnp.sum(e1, axis=1, keepdims=True)
                s0, s1 = _halves(lo, t2_g[:, blk])
                gh = gh_g[blk, :]
                dd0 = jnp.sum(jnp.where(top[:, 0:1], gh, 0.0), axis=0, keepdims=True)
                dd1 = jnp.sum(jnp.where(top[:, 0:1], 0.0, gh), axis=0, keepdims=True)
                end0 = jnp.sum(s0, axis=0, keepdims=True) + dd0 * dec[:, h0:h0 + 1]
                end1 = jnp.sum(s1, axis=0, keepdims=True) + dd1 * dec[:, h0 + 1:h0 + 2]
                d_a = d_a + jnp.where(lane == h0, a0 + jnp.where(last_row, end0, 0.0), 0.0)
                d_a = d_a + jnp.where(lane == h0 + 1, a1 + jnp.where(last_row, end1, 0.0), 0.0)
                d_at = d_at - jnp.where(sub == h0, jnp.sum(e0, axis=0, keepdims=True), 0.0)
                d_at = d_at - jnp.where(sub == h0 + 1, jnp.sum(e1, axis=0, keepdims=True), 0.0)
                dec_rows = jnp.where(top, dec[:, h0:h0 + 1], dec[:, h0 + 1:h0 + 2])
                gn_ref[g, blk, :] = d_h_g[blk, :] + dec_rows * gn_g[blk, :]
                q0, q1 = _halves(lo, d_xdt * xs_p)
                ddt = ddt + jnp.where(lane == h0, q0, 0.0) + jnp.where(lane == h0 + 1, q1, 0.0)
                dxs_ref[:, cols] += d_xdt * dt_pp
            d_cg = d_cg + _dot(d_cb, b_g, NN)
            d_bg = d_bg + _dot(d_cb, c_g, TN)
            d_b_parts.append(d_bg)
            d_c_parts.append(d_cg)

        rc = _rev_cumsum_rows(d_a + d_at.T)
        d_dt = rc * aneg + ddt
        galog_ref[...] += jnp.sum(rc * dt, axis=0, keepdims=True) * aneg
        d_dtraw = d_dt * _sigmoid(dt_in)
        gdtb_ref[...] += jnp.sum(d_dtraw, axis=0, keepdims=True)
        ddt_ref[...] = d_dtraw.astype(ddt_ref.dtype)

        def dsilu(p, s):
            return s * (1.0 + p * (1.0 - s))

        dcx = dxs_ref[...] * dsilu(px, sx)
        dcb = jnp.concatenate(d_b_parts, axis=1) * dsilu(pb, sb)
        dcc = jnp.concatenate(d_c_parts, axis=1) * dsilu(pc, sc)
        drx, gwx, gbx = _conv_backward(dcx, nx_ref[...], xr, cwx)
        drb, gwb, gbb = _conv_backward(dcb, nb_ref[...], br, cwb)
        drc, gwc, gbc = _conv_backward(dcc, ncc_ref[...], cr, cwc)
        nx_ref[...] = dcx[0:8, :]
        nb_ref[...] = dcb[0:8, :]
        ncc_ref[...] = dcc[0:8, :]
        dssd_ref[:, SSD_WIDTH:2 * SSD_WIDTH] = drx.astype(dssd_ref.dtype)
        dssd_ref[:, 2 * SSD_WIDTH:2 * SSD_WIDTH + SSD_BC] = drb.astype(dssd_ref.dtype)
        dssd_ref[:, 2 * SSD_WIDTH + SSD_BC:SSD_SEG] = drc.astype(dssd_ref.dtype)
        for k in range(CONV_WIDTH):
            gcw_ref[k:k + 1, :] += jnp.concatenate([gwx[k], gwb[k], gwc[k]], axis=1)
        gcb_ref[...] += jnp.concatenate([gbx, gbb, gbc], axis=1)

    const = lambda shape: pl.BlockSpec(shape, lambda c: (0,) * len(shape))
    body, ex_in, ex_out, ex_shape, ex_sems = _riding(
        exchange, body, 18, 8, lambda: pl.program_id(0) == 0, lambda: pl.program_id(0) == nc - 1)
    res = pl.pallas_call(
        body, name="ssd_bwd", grid=(nc,),
        in_specs=[sp["z"], sp["xr"], sp["br"], sp["cr"], pre_spec(SSD_WIDTH, 0), pre_spec(SSD_BC, SSD_WIDTH // SSD_BC), pre_spec(SSD_BC, SSD_WIDTH // SSD_BC + 1),
                  sp["dt"], sp["wide"], sp["states"], sp["wide"],
                  sp["cwx"], sp["cwb"], sp["cwc"], sp["vec128"], sp["vec128"], sp["vecw"], sp["vecw"]] + ex_in,
        out_specs=[pl.BlockSpec((L, SSD_SEG), lambda c: (nc - 1 - c, 0)), sp["dt"],
                   const((CONV_WIDTH, SSD_CONV_CH)), const((1, SSD_CONV_CH)), const((1, LANES)), const((1, LANES)),
                   const((1, SSD_WIDTH)), const((1, SSD_WIDTH))] + ex_out,
        out_shape=[jax.ShapeDtypeStruct((t, SSD_SEG), MXU_DTYPE), jax.ShapeDtypeStruct((t, DT_PAD), MXU_DTYPE),
                   jax.ShapeDtypeStruct((CONV_WIDTH, SSD_CONV_CH), F32), jax.ShapeDtypeStruct((1, SSD_CONV_CH), F32),
                   jax.ShapeDtypeStruct((1, LANES), F32), jax.ShapeDtypeStruct((1, LANES), F32),
                   jax.ShapeDtypeStruct((1, SSD_WIDTH), F32), jax.ShapeDtypeStruct((1, SSD_WIDTH), F32)] + ex_shape,
        scratch_shapes=[pltpu.VMEM((SSD_GROUPS, GROUP_COLS, SSD_STATE), F32), pltpu.VMEM((8, SSD_WIDTH), F32),
                        pltpu.VMEM((8, SSD_BC), F32), pltpu.VMEM((8, SSD_BC), F32), pltpu.VMEM((L, SSD_WIDTH), F32)] + ex_sems,
        compiler_params=_params(("arbitrary",)),
    )(proj_ssd, proj_ssd, proj_ssd, proj_ssd, pre, pre, pre, dt_p, y, states, dyssd,
      conv_w, conv_w, conv_w, dtb, alog, d_row, ng_row, *(exchange["arrays"] if exchange else []))
    return res[:8], res[8:]


def _lru_gates(xl, wa_ref, wx_ref, ba, bx, lam):
    pre_a, pre_x = [], []
    for g in range(LRU_NGROUPS):
        xg = xl[:, g * LRU_GROUP:(g + 1) * LRU_GROUP]
        pre_a.append(_dot(xg, wa_ref[g], NN))
        pre_x.append(_dot(xg, wx_ref[g], NN))
    r = _sigmoid(jnp.concatenate(pre_a, axis=1) + ba)
    i = _sigmoid(jnp.concatenate(pre_x, axis=1) + bx)
    log_a = (-LRU_C * r) * _softplus(-lam)
    a = jnp.exp(log_a)
    mult_sq = -jnp.tanh(log_a) * (a * a + 1.0)
    return r, i, mult_sq, a, jnp.sqrt(mult_sq)


def _scan_rows(p, u, carry, reverse):
    rows, w = p.shape
    groups = rows // 8
    p3, u3 = p.reshape(groups, 8, w), u.reshape(groups, 8, w)
    row = lax.broadcasted_iota(jnp.int32, (groups, 8, w), 1)
    for s in (1, 2, 4):
        ok = row < 8 - s if reverse else row >= s
        shift = 8 - s if reverse else s
        u3 = p3 * jnp.where(ok, pltpu.roll(u3, shift, 1), 0.0) + u3
        p3 = p3 * jnp.where(ok, pltpu.roll(p3, shift, 1), 1.0)
    out = [None] * groups
    for k in (range(groups - 1, -1, -1) if reverse else range(groups)):
        out[k] = p3[k] * carry + u3[k]
        carry = out[k][0:1, :] if reverse else out[k][7:8, :]
    return jnp.concatenate(out, axis=0), carry


def _lru_fwd(proj_lru, conv_w, conv_b, wa, wx, ba, bx, lam):
    t = proj_lru.shape[0]
    rows = min(LRU_ROWS, t)
    nb = t // rows
    W = LRU_WIDTH

    def body(lg_ref, lx_ref, cw_ref, cb_ref, wa_ref, wx_ref, ba_ref, bx_ref, lam_ref, ylru_ref, h_ref, xl_ref,
             halo_ref, carry_ref):
        @pl.when(pl.program_id(0) == 0)
        def _():
            halo_ref[...] = jnp.zeros_like(halo_ref)
            carry_ref[...] = jnp.zeros_like(carry_ref)

        lx = lx_ref[...]
        xl = _causal_conv(lx, halo_ref[...], cw_ref[...], cb_ref[...])
        halo_ref[...] = lx[rows - 8:rows, :]
        xl_ref[...] = xl
        _, i, _, a, mult = _lru_gates(xl, wa_ref, wx_ref, ba_ref[...], bx_ref[...], lam_ref[...])
        u = mult * (i * xl)
        h, carry_ref[...] = _scan_rows(a, u, carry_ref[...], False)
        h_ref[...] = h
        lg = lg_ref[...]
        ylru_ref[...] = (h * (lg * _sigmoid(lg))).astype(ylru_ref.dtype)

    const = lambda shape: pl.BlockSpec(shape, lambda b: (0,) * len(shape))
    return pl.pallas_call(
        body, name="lru_fwd", grid=(nb,),
        in_specs=[pl.BlockSpec((rows, W), lambda b: (b, 0)), pl.BlockSpec((rows, W), lambda b: (b, 1)),
                  const((CONV_WIDTH, W)), const((1, W)), const((LRU_NGROUPS, LRU_GROUP, LRU_GROUP)),
                  const((LRU_NGROUPS, LRU_GROUP, LRU_GROUP)), const((1, W)), const((1, W)), const((1, W))],
        out_specs=[pl.BlockSpec((rows, W), lambda b: (b, 0))] * 3,
        out_shape=[jax.ShapeDtypeStruct((t, W), MXU_DTYPE), jax.ShapeDtypeStruct((t, W), F32), jax.ShapeDtypeStruct((t, W), F32)],
        scratch_shapes=[pltpu.VMEM((8, W), F32), pltpu.VMEM((1, W), F32)],
        compiler_params=_params(("arbitrary",)),
    )(proj_lru, proj_lru, conv_w, conv_b, wa, wx, ba, bx, lam)


def _lru_bwd(proj_lru, xl, h, dylru, conv_w, wa, wx, ba, bx, lam, exchange=None):
    t = proj_lru.shape[0]
    rows = min(LRU_ROWS, t)
    nb = t // rows
    W = LRU_WIDTH
    groups8 = rows // 8

    def rev(b):
        return nb - 1 - b

    def halo_spec(col):
        return pl.BlockSpec((8, W), lambda b: (jnp.maximum(rev(b) * groups8 - 1, 0), col))

    def body(lg_ref, lx_ref, xl_ref, h_ref, hh_ref, dy_ref, cw_ref, wa_ref, wx_ref, ba_ref, bx_ref, lam_ref,
             dlru_ref, gcw_ref, gcb_ref, gba_ref, gbx_ref, glam_ref, gwa_ref, gwx_ref,
             gcarry_ref, afirst_ref, nxt_ref):
        step = pl.program_id(0)

        @pl.when(step == 0)
        def _():
            gcarry_ref[...] = jnp.zeros_like(gcarry_ref)
            afirst_ref[...] = jnp.zeros_like(afirst_ref)
            nxt_ref[...] = jnp.zeros_like(nxt_ref)
            for ref in (gcw_ref, gcb_ref, gba_ref, gbx_ref, glam_ref, gwa_ref, gwx_ref):
                ref[...] = jnp.zeros_like(ref)

        keep = jnp.where(step == nb - 1, 0.0, 1.0)
        lx = lx_ref[...]
        cw = cw_ref[...]
        xl = xl_ref[...]
        lam = lam_ref[...]
        r, i, mult_sq, a, mult = _lru_gates(xl, wa_ref, wx_ref, ba_ref[...], bx_ref[...], lam)
        hv = h_ref[...]
        h_prev = _shift_down(hv, hh_ref[...] * keep, 1)
        lg = lg_ref[...]
        sg = _sigmoid(lg)
        dyv = dy_ref[...]
        d_h = dyv * (lg * sg)
        dlru_ref[:, 0:W] = (dyv * hv * (sg * (1.0 + lg * (1.0 - sg)))).astype(dlru_ref.dtype)

        row = lax.broadcasted_iota(jnp.int32, (rows, W), 0)
        p = jnp.where(row < rows - 1, pltpu.roll(a, rows - 1, 0), afirst_ref[...])
        gsc, gcarry_ref[...] = _scan_rows(p, d_h, gcarry_ref[...], True)
        afirst_ref[...] = a[0:1, :]

        d_a = gsc * h_prev
        v = i * xl
        d_mult = gsc * v
        d_v = gsc * mult
        d_i = d_v * xl
        d_xl = d_v * i
        d_la = d_a * a - d_mult * (a * a) * lax.rsqrt(mult_sq)
        sp_neg = _softplus(-lam)
        d_r = d_la * (-LRU_C * sp_neg)
        glam_ref[...] += jnp.sum(d_la * r, axis=0, keepdims=True) * (LRU_C * _sigmoid(-lam))
        d_pa = d_r * r * (1.0 - r)
        d_px = d_i * i * (1.0 - i)
        gba_ref[...] += jnp.sum(d_pa, axis=0, keepdims=True)
        gbx_ref[...] += jnp.sum(d_px, axis=0, keepdims=True)
        parts = []
        for g in range(LRU_NGROUPS):
            cols = slice(g * LRU_GROUP, (g + 1) * LRU_GROUP)
            xg, dpa_g, dpx_g = xl[:, cols], d_pa[:, cols], d_px[:, cols]
            parts.append(_dot(dpa_g, wa_ref[g], NT) + _dot(dpx_g, wx_ref[g], NT))
            gwa_ref[g] += _dot(xg, dpa_g, TN)
            gwx_ref[g] += _dot(xg, dpx_g, TN)
        d_xl = d_xl + jnp.concatenate(parts, axis=1)
        d_lx, gw, gb = _conv_backward(d_xl, nxt_ref[...], lx, cw)
        nxt_ref[...] = d_xl[0:8, :]
        dlru_ref[:, W:2 * W] = d_lx.astype(dlru_ref.dtype)
        for k in range(CONV_WIDTH):
            gcw_ref[k:k + 1, :] += gw[k]
        gcb_ref[...] += gb

    const = lambda shape: pl.BlockSpec(shape, lambda b: (0,) * len(shape))
    wspec = const((LRU_NGROUPS, LRU_GROUP, LRU_GROUP))
    blk = lambda col: pl.BlockSpec((rows, W), lambda b: (rev(b), col))
    body, ex_in, ex_out, ex_shape, ex_sems = _riding(
        exchange, body, 12, 8, lambda: pl.program_id(0) == 0, lambda: pl.program_id(0) == nb - 1)
    res = pl.pallas_call(
        body, name="lru_bwd", grid=(nb,),
        in_specs=[blk(0), blk(1), blk(0), blk(0), halo_spec(0), blk(0),
                  const((CONV_WIDTH, W)), wspec, wspec, const((1, W)), const((1, W)), const((1, W))] + ex_in,
        out_specs=[pl.BlockSpec((rows, 2 * W), lambda b: (rev(b), 0)), const((CONV_WIDTH, W)), const((1, W)),
                   const((1, W)), const((1, W)), const((1, W)), wspec, wspec] + ex_out,
        out_shape=[jax.ShapeDtypeStruct((t, 2 * W), MXU_DTYPE), jax.ShapeDtypeStruct((CONV_WIDTH, W), F32),
                   jax.ShapeDtypeStruct((1, W), F32), jax.ShapeDtypeStruct((1, W), F32), jax.ShapeDtypeStruct((1, W), F32),
                   jax.ShapeDtypeStruct((1, W), F32), jax.ShapeDtypeStruct((LRU_NGROUPS, LRU_GROUP, LRU_GROUP), F32),
                   jax.ShapeDtypeStruct((LRU_NGROUPS, LRU_GROUP, LRU_GROUP), F32)] + ex_shape,
        scratch_shapes=[pltpu.VMEM((1, W), F32), pltpu.VMEM((1, W), F32), pltpu.VMEM((8, W), F32)] + ex_sems,
        compiler_params=_params(("arbitrary",)),
    )(proj_lru, proj_lru, xl, h, h, dylru, conv_w, wa, wx, ba, bx, lam, *(exchange["arrays"] if exchange else []))
    return res[:8], res[8:]


def _mem_scores(q_h, k_h):
    s = _dot(q_h, k_h, NT) * (MEM_HEAD_DIM ** -0.5)
    s = s - jnp.max(s, axis=-1, keepdims=True)
    e = jnp.exp(s)
    return e / jnp.sum(e, axis=-1, keepdims=True)


def _mem_fwd(q, kv, rows=512):
    t = q.shape[0]
    rows = min(rows, t)
    m = kv.shape[0]

    def body(q_ref, kv_ref, y_ref):
        for hd in range(MEM_HEADS):
            cols = slice(hd * MEM_HEAD_DIM, (hd + 1) * MEM_HEAD_DIM)
            vcols = slice(D_MODEL + hd * MEM_HEAD_DIM, D_MODEL + (hd + 1) * MEM_HEAD_DIM)
            p = _mem_scores(q_ref[:, cols], kv_ref[:, cols])
            y_ref[:, cols] = _dot(p, kv_ref[:, vcols], NN).astype(y_ref.dtype)

    return pl.pallas_call(
        body, name="mem_fwd", grid=(t // rows,),
        in_specs=[pl.BlockSpec((rows, D_MODEL), lambda i: (i, 0)), pl.BlockSpec((m, 2 * D_MODEL), lambda i: (0, 0))],
        out_specs=pl.BlockSpec((rows, D_MODEL), lambda i: (i, 0)),
        out_shape=jax.ShapeDtypeStruct((t, D_MODEL), MXU_DTYPE),
        compiler_params=_params(("parallel",)),
    )(q, kv)


def _mem_bwd(q, kv, dy, rows=512):
    t = q.shape[0]
    rows = min(rows, t)
    m = kv.shape[0]

    def body(q_ref, kv_ref, dy_ref, dq_ref, dkv_ref):
        @pl.when(pl.program_id(0) == 0)
        def _():
            dkv_ref[...] = jnp.zeros_like(dkv_ref)

        for hd in range(MEM_HEADS):
            cols = slice(hd * MEM_HEAD_DIM, (hd + 1) * MEM_HEAD_DIM)
            vcols = slice(D_MODEL + hd * MEM_HEAD_DIM, D_MODEL + (hd + 1) * MEM_HEAD_DIM)
            q_h, k_h, dy_h = q_ref[:, cols], kv_ref[:, cols], dy_ref[:, cols]
            p = _mem_scores(q_h, k_h)
            dp = _dot(dy_h, kv_ref[:, vcols], NT)
            dkv_ref[:, vcols] += _dot(p, dy_h, TN)
            ds = p * (dp - jnp.sum(dp * p, axis=-1, keepdims=True)) * (MEM_HEAD_DIM ** -0.5)
            dq_ref[:, cols] = _dot(ds, k_h, NN).astype(dq_ref.dtype)
            dkv_ref[:, cols] += _dot(ds, q_h, TN)

    return pl.pallas_call(
        body, name="mem_bwd", grid=(t // rows,),
        in_specs=[pl.BlockSpec((rows, D_MODEL), lambda i: (i, 0)), pl.BlockSpec((m, 2 * D_MODEL), lambda i: (0, 0)),
                  pl.BlockSpec((rows, D_MODEL), lambda i: (i, 0))],
        out_specs=[pl.BlockSpec((rows, D_MODEL), lambda i: (i, 0)), pl.BlockSpec((m, 2 * D_MODEL), lambda i: (0, 0))],
        out_shape=[jax.ShapeDtypeStruct((t, D_MODEL), MXU_DTYPE), jax.ShapeDtypeStruct((m, 2 * D_MODEL), F32)],
        compiler_params=_params(("arbitrary",)),
    )(q, kv, dy)


def _merge_fwd(x, yssd, ylru, ymem, gl, w_bs, w_bl, w_bm, w_out, fg, tgt, rows=256):
    t = x.shape[0]
    rows = min(rows, t)
    D = D_MODEL

    def body(x_ref, ys_ref, yl_ref, ym_ref, gl_ref, wbs_ref, wbl_ref, wbm_ref, wo_ref, fg_ref, tgt_ref,
             ps_ref, pl_ref, pm_ref, mg_ref, dx2_ref, loss_ref, gfg_ref):
        @pl.when(pl.program_id(0) == 0)
        def _():
            loss_ref[...] = jnp.zeros_like(loss_ref)
            gfg_ref[...] = jnp.zeros_like(gfg_ref)

        ps = _dot(ys_ref[...], wbs_ref[...], NN)
        pl_ = _dot(yl_ref[...], wbl_ref[...], NN)
        pm = _dot(ym_ref[...], wbm_ref[...], NN)
        ps_ref[...] = ps
        pl_ref[...] = pl_
        pm_ref[...] = pm
        merged = (_sigmoid(gl_ref[:, 0:D]) * ps + _sigmoid(gl_ref[:, D:2 * D]) * pl_) + _sigmoid(gl_ref[:, 2 * D:3 * D]) * pm
        mg_ref[...] = merged.astype(mg_ref.dtype)
        x2 = x_ref[...] + _dot(merged, wo_ref[...], NN)
        r2 = lax.rsqrt(jnp.mean(x2 * x2, axis=-1, keepdims=True) + EPS)
        xn = x2 * r2
        fg = fg_ref[...]
        diff = xn * fg - tgt_ref[...]
        tile_loss = 0.5 * jnp.sum(jnp.mean(diff * diff, axis=-1, keepdims=True), axis=0, keepdims=True)
        loss_ref[...] += jnp.broadcast_to(tile_loss, loss_ref.shape)
        d_out = diff * (1.0 / D)
        gfg_ref[...] += jnp.sum(d_out * xn, axis=0, keepdims=True)
        dxn = d_out * fg
        dx2_ref[...] = r2 * (dxn - xn * jnp.mean(dxn * xn, axis=-1, keepdims=True))

    row = lambda w: pl.BlockSpec((rows, w), lambda i: (i, 0))
    const = lambda shape: pl.BlockSpec(shape, lambda i: (0,) * len(shape))
    return pl.pallas_call(
        body, name="merge_fwd", grid=(t // rows,),
        in_specs=[row(D), row(SSD_WIDTH), row(LRU_WIDTH), row(D), row(3 * D), const((SSD_WIDTH, D)), const((LRU_WIDTH, D)),
                  const((D, D)), const((D, D)), const((1, D)), row(D)],
        out_specs=[row(D), row(D), row(D), row(D), row(D), const((1, LANES)), const((1, D))],
        out_shape=[jax.ShapeDtypeStruct((t, D), F32), jax.ShapeDtypeStruct((t, D), F32), jax.ShapeDtypeStruct((t, D), F32),
                   jax.ShapeDtypeStruct((t, D), MXU_DTYPE), jax.ShapeDtypeStruct((t, D), F32),
                   jax.ShapeDtypeStruct((1, LANES), F32), jax.ShapeDtypeStruct((1, D), F32)],
        compiler_params=_params(("arbitrary",)),
    )(x, yssd, ylru, ymem, gl, w_bs, w_bl, w_bm, w_out, fg, tgt)


def _merge_bwd(dx2, gl, ps, pl_in, pm, w_bs, w_bl, w_bm, w_out, rows=256):
    t = dx2.shape[0]
    rows = min(rows, t)
    D = D_MODEL

    def body(dx2_ref, gl_ref, ps_ref, pl_ref, pm_ref, wbs_ref, wbl_ref, wbm_ref, wo_ref,
             dg_ref, dps_ref, dpl_ref, dpm_ref, dys_ref, dyl_ref, dym_ref):
        dm = _dot(dx2_ref[...], wo_ref[...], NT)
        for idx, (p_ref, dp_ref, w_ref, dy_ref) in enumerate(
                ((ps_ref, dps_ref, wbs_ref, dys_ref), (pl_ref, dpl_ref, wbl_ref, dyl_ref), (pm_ref, dpm_ref, wbm_ref, dym_ref))):
            gate = _sigmoid(gl_ref[:, idx * D:(idx + 1) * D])
            dg_ref[:, idx * D:(idx + 1) * D] = ((dm * p_ref[...]) * gate * (1.0 - gate)).astype(dg_ref.dtype)
            dp = dm * gate
            dp_ref[...] = dp.astype(dp_ref.dtype)
            dy_ref[...] = _dot(dp, w_ref[...], NT)

    row = lambda w: pl.BlockSpec((rows, w), lambda i: (i, 0))
    const = lambda shape: pl.BlockSpec(shape, lambda i: (0,) * len(shape))
    return pl.pallas_call(
        body, name="merge_bwd", grid=(t // rows,),
        in_specs=[row(D), row(3 * D), row(D), row(D), row(D), const((SSD_WIDTH, D)), const((LRU_WIDTH, D)),
                  const((D, D)), const((D, D))],
        out_specs=[row(3 * D), row(D), row(D), row(D), row(SSD_WIDTH), row(LRU_WIDTH), row(D)],
        out_shape=[jax.ShapeDtypeStruct((t, 3 * D), MXU_DTYPE), jax.ShapeDtypeStruct((t, D), MXU_DTYPE),
                   jax.ShapeDtypeStruct((t, D), MXU_DTYPE), jax.ShapeDtypeStruct((t, D), MXU_DTYPE),
                   jax.ShapeDtypeStruct((t, SSD_WIDTH), F32), jax.ShapeDtypeStruct((t, LRU_WIDTH), F32),
                   jax.ShapeDtypeStruct((t, D), F32)],
        compiler_params=_params(("parallel",)),
    )(dx2, gl, ps, pl_in, pm, w_bs, w_bl, w_bm, w_out)


def _mesh_place():
    x, y, c = lax.axis_index("x"), lax.axis_index("y"), lax.axis_index("c")
    return x, y, c, 4 * x + 2 * y + c


def _other_chips(x, y):
    return [(1 - x, y), (x, 1 - y), (1 - x, 1 - y)]


def _all_gather_plan(arrs):
    n = len(arrs)

    def parts(ins, outs, send_sems, recv_sems, local_sems):
        x, y, c, me = _mesh_place()
        sibling = (x, y, 1 - c)
        chips = _other_chips(x, y)

        def slot(px, py, pc):
            return 4 * px + 2 * py + pc

        def copy(a, k, block, to, src=None):
            return pltpu.make_async_remote_copy(
                src_ref=outs[a].at[block] if src is None else src, dst_ref=outs[a].at[block],
                send_sem=send_sems.at[a, k], recv_sem=recv_sems.at[a, k], device_id=to, device_id_type=pl.DeviceIdType.MESH)

        def local():
            return [pltpu.make_async_copy(ins[a], outs[a].at[me], local_sems.at[a]) for a in range(n)]

        def first():
            return [copy(a, k, me, to, src=ins[a]) for a in range(n)
                    for k, to in enumerate([sibling] + [(*chip, c) for chip in chips])]

        return x, y, c, sibling, chips, slot, copy, local, first

    def start(ins, outs, *sems):
        *_, local, first = parts(ins, outs, *sems)
        for cp in local() + first():
            cp.start()

    def wait(ins, outs, *sems):
        x, y, c, sibling, chips, slot, copy, local, first = parts(ins, outs, *sems)
        sends = first()
        for j, chip in enumerate(chips):
            for a in range(n):
                copy(a, 1 + j, slot(*chip, c), sibling).wait_recv()
                passed = copy(a, 4 + j, slot(*chip, c), sibling)
                passed.start()
                sends.append(passed)
        for a in range(n):
            copy(a, 0, slot(x, y, 1 - c), sibling).wait_recv()
        for j, chip in enumerate(chips):
            for a in range(n):
                copy(a, 4 + j, slot(*chip, 1 - c), sibling).wait_recv()
        for cp in sends:
            cp.wait_send()
        for cp in local():
            cp.wait()

    return dict(arrays=list(arrs), out_shape=[jax.ShapeDtypeStruct((N_DEV,) + a.shape, a.dtype) for a in arrs],
                sems=[(n, 7), (n, 7), (n,)], start=start, wait=wait)


N_CHIPS = 4


def _pair_plan(parts):
    n = len(parts)

    def copies(ins, outs, send_sems, recv_sems):
        x, y, c, _ = _mesh_place()
        return [pltpu.make_async_remote_copy(src_ref=ins[a].at[q, 1 - c], dst_ref=outs[a].at[q], send_sem=send_sems.at[a, q],
                                             recv_sem=recv_sems.at[a, q], device_id=(x, y, 1 - c), device_id_type=pl.DeviceIdType.MESH)
                for a in range(n) for q in range(N_CHIPS)]

    def start(ins, outs, send_sems, recv_sems):
        for cp in copies(ins, outs, send_sems, recv_sems):
            cp.start()

    def wait(ins, outs, send_sems, recv_sems):
        cps = copies(ins, outs, send_sems, recv_sems)
        for cp in cps:
            cp.wait_recv()
        for cp in cps:
            cp.wait_send()

    return dict(arrays=list(parts), out_shape=[jax.ShapeDtypeStruct((N_CHIPS,) + a.shape[2:], a.dtype) for a in parts],
                sems=[(n, N_CHIPS), (n, N_CHIPS)], start=start, wait=wait)


def _chip_plan(sums):
    n = len(sums)

    def copies(ins, outs, send_sems, recv_sems, arriving):
        x, y, c, _ = _mesh_place()
        my_chip = 2 * x + y
        cps = []
        for a in range(n):
            for j, (px, py) in enumerate(_other_chips(x, y)):
                src, dst = (my_chip, 2 * px + py) if arriving else (2 * px + py, my_chip)
                cps.append(pltpu.make_async_remote_copy(
                    src_ref=ins[a].at[src], dst_ref=outs[a].at[dst], send_sem=send_sems.at[a, j], recv_sem=recv_sems.at[a, j],
                    device_id=(px, py, c), device_id_type=pl.DeviceIdType.MESH))
        return cps

    def start(ins, outs, send_sems, recv_sems):
        for cp in copies(ins, outs, send_sems, recv_sems, False):
            cp.start()

    def wait(ins, outs, send_sems, recv_sems):
        for cp in copies(ins, outs, send_sems, recv_sems, True):
            cp.wait_recv()
        for cp in copies(ins, outs, send_sems, recv_sems, False):
            cp.wait_send()

    return dict(arrays=list(sums), out_shape=[jax.ShapeDtypeStruct(a.shape, a.dtype) for a in sums],
                sems=[(n, 3), (n, 3)], start=start, wait=wait)


def _both(p1, p2):
    n1, s1 = len(p1["arrays"]), len(p1["sems"])

    def each(method):
        def run(ins, outs, *sems):
            p1[method](ins[:n1], outs[:n1], *sems[:s1])
            p2[method](ins[n1:], outs[n1:], *sems[s1:])
        return run

    return dict(arrays=p1["arrays"] + p2["arrays"], out_shape=p1["out_shape"] + p2["out_shape"],
                sems=p1["sems"] + p2["sems"], start=each("start"), wait=each("wait"))


def _run_exchange(plan, name):
    n = len(plan["arrays"])

    def body(*refs):
        ins, outs, sems = refs[:n], refs[n:2 * n], refs[2 * n:]
        plan["start"](ins, outs, *sems)
        plan["wait"](ins, outs, *sems)

    any_spec = pl.BlockSpec(memory_space=pl.ANY)
    return pl.pallas_call(
        body, name=name, in_specs=[any_spec] * n, out_specs=[any_spec] * n, out_shape=plan["out_shape"],
        scratch_shapes=[pltpu.SemaphoreType.DMA(shape) for shape in plan["sems"]],
    )(*plan["arrays"])


def _riding(plan, body, n_in, n_out, first, last):
    if plan is None:
        return body, [], [], [], []
    ne = len(plan["arrays"])

    def wrapped(*refs):
        ins, ex_in = refs[:n_in], refs[n_in:n_in + ne]
        outs = refs[n_in + ne:n_in + ne + n_out]
        ex_out = refs[n_in + ne + n_out:n_in + 2 * ne + n_out]
        n_sems = len(plan["sems"])
        scratch, sems = refs[n_in + 2 * ne + n_out:-n_sems], refs[-n_sems:]

        @pl.when(first())
        def _():
            plan["start"](ex_in, ex_out, *sems)

        body(*ins, *outs, *scratch)

        @pl.when(last())
        def _():
            plan["wait"](ex_in, ex_out, *sems)

    any_spec = pl.BlockSpec(memory_space=pl.ANY)
    sems = [pltpu.SemaphoreType.DMA(shape) for shape in plan["sems"]]
    return wrapped, [any_spec] * ne, [any_spec] * ne, plan["out_shape"], sems


def _col_tile(r, c, limit_bytes):
    assert c % LANES == 0, c
    best = LANES
    for cand in range(LANES, c + 1, LANES):
        if c % cand == 0 and r * cand * 4 <= limit_bytes:
            best = cand
    return best


def _chip_sum(part, recv, core, name):
    _, _, r, c = part.shape
    ct = _col_tile(r, c, 2 << 20)

    def body(core_ref, p_ref, r_ref, s_ref, t_ref):
        s = p_ref[...] + r_ref[...]
        s_ref[...] = s
        t_ref[...] = s.astype(t_ref.dtype)

    blk = pl.BlockSpec((None, r, ct), lambda q, i, core_ref: (q, 0, i))
    return pl.pallas_call(
        body, name=name,
        grid_spec=pltpu.PrefetchScalarGridSpec(
            num_scalar_prefetch=1, grid=(N_CHIPS, c // ct),
            in_specs=[pl.BlockSpec((None, None, r, ct), lambda q, i, core_ref: (q, core_ref[0], 0, i)), blk],
            out_specs=[blk, blk]),
        out_shape=[jax.ShapeDtypeStruct((N_CHIPS, r, c), F32), jax.ShapeDtypeStruct((N_CHIPS, r, c), GRAD_WIRE_DTYPE)],
        compiler_params=_params(("parallel", "parallel")),
    )(core, part, recv)


def _adam_update(w, g, m, v):
    nm = ADAM_B1 * m + (1.0 - ADAM_B1) * g
    nv = ADAM_B2 * v + (1.0 - ADAM_B2) * (g * g)
    m_hat = nm / (1.0 - ADAM_B1 ** ADAM_STEP)
    v_hat = nv / (1.0 - ADAM_B2 ** ADAM_STEP)
    return -ADAM_LR * (m_hat / (jnp.sqrt(v_hat) + ADAM_EPS) + ADAM_WD * w), nm, nv


def _sum_adamw(own, recv, chip, w, m, v, name, exchange=None):
    _, r, c = own.shape
    ct = _col_tile(r, c, 1 << 20)

    def body(chip_ref, o_ref, r1_ref, r2_ref, r3_ref, w_ref, m_ref, v_ref, g_ref, d_ref, nm_ref, nv_ref):
        g = ((o_ref[...] + r1_ref[...].astype(F32)) + r2_ref[...].astype(F32)) + r3_ref[...].astype(F32)
        g_ref[...] = g
        d_ref[...], nm_ref[...], nv_ref[...] = _adam_update(w_ref[...], g, m_ref[...], v_ref[...])

    def slot(k):
        return pl.BlockSpec((None, r, ct), lambda i, chip_ref: ((chip_ref[0] + k) % N_CHIPS, 0, i))

    spec = pl.BlockSpec((r, ct), lambda i, chip_ref: (0, i))
    shape = jax.ShapeDtypeStruct((r, c), F32)
    n_tiles = c // ct
    body, ex_in, ex_out, ex_shape, ex_sems = _riding(
        exchange, body, 8, 4, lambda: pl.program_id(0) == 0, lambda: pl.program_id(0) == n_tiles - 1)
    res = pl.pallas_call(
        body, name=name,
        grid_spec=pltpu.PrefetchScalarGridSpec(
            num_scalar_prefetch=1, grid=(n_tiles,),
            in_specs=[slot(0), slot(1), slot(2), slot(3), spec, spec, spec] + ex_in, out_specs=[spec] * 4 + ex_out,
            scratch_shapes=ex_sems),
        out_shape=[shape] * 4 + ex_shape,
        compiler_params=_params(("arbitrary",)),
    )(chip, own, recv, recv, recv, w, m, v, *(exchange["arrays"] if exchange else []))
    return res[:4], res[4:]


def _small_adamw(parts, ws, ms, vs):
    n = len(parts)

    def body(*refs):
        p_refs, w_refs, m_refs, v_refs = refs[:n], refs[n:2 * n], refs[2 * n:3 * n], refs[3 * n:4 * n]
        outs = refs[4 * n:]
        for i in range(n):
            g = p_refs[i][0]
            for k in range(1, N_DEV):
                g = g + p_refs[i][k]
            outs[i][...] = g
            outs[n + i][...], outs[2 * n + i][...], outs[3 * n + i][...] = _adam_update(
                w_refs[i][...], g, m_refs[i][...], v_refs[i][...])

    vmem = pl.BlockSpec(memory_space=pltpu.VMEM)
    shapes = [jax.ShapeDtypeStruct(w.shape, F32) for w in ws]
    res = pl.pallas_call(
        body, name="adamw_small", in_specs=[vmem] * (4 * n), out_specs=[vmem] * (4 * n), out_shape=shapes * 4,
        compiler_params=pltpu.CompilerParams(vmem_limit_bytes=VMEM_LIMIT),
    )(*parts, *ws, *ms, *vs)
    return res[:n], res[n:2 * n], res[2 * n:3 * n], res[3 * n:]


def _pack(arrs, dtype, row_multiple):
    flat = jnp.concatenate([a.reshape(-1).astype(dtype) for a in arrs])
    unit = LANES * row_multiple
    padded = -(-flat.shape[0] // unit) * unit
    return jnp.pad(flat, (0, padded - flat.shape[0])).reshape(-1, LANES)


def _unpack(packed, shapes, lead=()):
    flat = packed.reshape(lead + (-1,))
    out, off = [], 0
    for shp in shapes:
        n = math.prod(shp)
        out.append(flat[..., off:off + n].reshape(lead + tuple(shp)))
        off += n
    return out


def _gather_cols(g, lo, hi):
    width = g.shape[2]
    pieces = []
    for s in range(N_DEV):
        a, e = max(lo, s * width), min(hi, (s + 1) * width)
        if a < e:
            pieces.append(g[s, :, a - s * width:e - s * width])
    return pieces[0] if len(pieces) == 1 else jnp.concatenate(pieces, axis=1)


def _scatter_cols(segs, width):
    slots = []
    for k in range(N_DEV):
        lo, hi = k * width, (k + 1) * width
        pieces = []
        for arr, s_lo, s_hi in segs:
            a, e = max(lo, s_lo), min(hi, s_hi)
            if a < e:
                pieces.append(arr[:, a - s_lo:e - s_lo])
        slots.append(pieces[0] if len(pieces) == 1 else jnp.concatenate(pieces, axis=1))
    return jnp.stack(slots)


def _gather_rows(g, lo, hi):
    height = g.shape[1]
    pieces = []
    for s in range(N_DEV):
        a, e = max(lo, s * height), min(hi, (s + 1) * height)
        if a < e:
            pieces.append(g[s, a - s * height:e - s * height])
    return pieces[0] if len(pieces) == 1 else jnp.concatenate(pieces, axis=0)


def _scatter_rows(segs, height):
    slots = []
    for k in range(N_DEV):
        lo, hi = k * height, (k + 1) * height
        pieces = []
        for arr, s_lo, s_hi in segs:
            a, e = max(lo, s_lo), min(hi, s_hi)
            if a < e:
                pieces.append(arr[a - s_lo:e - s_lo])
        slots.append(pieces[0] if len(pieces) == 1 else jnp.concatenate(pieces, axis=0))
    return jnp.stack(slots)


def _block_diag_groups(w):
    w4 = w.reshape(LRU_NGROUPS, 4, LRU_BLOCK, LRU_BLOCK)
    eye = jnp.eye(4, dtype=w.dtype)
    return jnp.einsum("gaij,ab->gaibj", w4, eye).reshape(LRU_NGROUPS, LRU_GROUP, LRU_GROUP)


def _block_diag_extract(wg):
    w5 = wg.reshape(LRU_NGROUPS, 4, LRU_BLOCK, 4, LRU_BLOCK)
    idx = jnp.arange(4)
    return w5[:, idx, :, idx, :].transpose(1, 0, 2, 3).reshape(LRU_BLOCKS, LRU_BLOCK, LRU_BLOCK)


BIG = ("w_in", "w_kv", "w_br_ssd", "w_br_lru", "w_br_mem", "w_out")
SMALL_SHARDED = ("ssd_conv_w", "ssd_norm_g", "lru_conv_w")
REPLICATED = ("norm_g", "ssd_conv_b", "ssd_dt_bias", "ssd_a_log", "ssd_d", "lru_conv_b", "lru_w_a", "lru_b_a",
              "lru_w_x", "lru_b_x", "lru_lambda", "mem_norm_g", "final_g")
WEIGHTS = ("norm_g", "w_in", "ssd_conv_w", "ssd_conv_b", "ssd_dt_bias", "ssd_a_log", "ssd_d", "ssd_norm_g", "lru_conv_w",
           "lru_conv_b", "lru_w_a", "lru_b_a", "lru_w_x", "lru_b_x", "lru_lambda", "mem_norm_g", "w_kv", "w_br_ssd",
           "w_br_lru", "w_br_mem", "w_out", "final_g")


def kernel(x, mem, norm_g, w_in, ssd_conv_w, ssd_conv_b, ssd_dt_bias, ssd_a_log, ssd_d, ssd_norm_g, lru_conv_w, lru_conv_b, lru_w_a, lru_b_a, lru_w_x, lru_b_x, lru_lambda, mem_norm_g, w_kv, w_br_ssd, w_br_lru, w_br_mem, w_out, final_g, loss_target, m_norm_g, m_w_in, m_ssd_conv_w, m_ssd_conv_b, m_ssd_dt_bias, m_ssd_a_log, m_ssd_d, m_ssd_norm_g, m_lru_conv_w, m_lru_conv_b, m_lru_w_a, m_lru_b_a, m_lru_w_x, m_lru_b_x, m_lru_lambda, m_mem_norm_g, m_w_kv, m_w_br_ssd, m_w_br_lru, m_w_br_mem, m_w_out, m_final_g, v_norm_g, v_w_in, v_ssd_conv_w, v_ssd_conv_b, v_ssd_dt_bias, v_ssd_a_log, v_ssd_d, v_ssd_norm_g, v_lru_conv_w, v_lru_conv_b, v_lru_w_a, v_lru_b_a, v_lru_w_x, v_lru_b_x, v_lru_lambda, v_mem_norm_g, v_w_kv, v_w_br_ssd, v_w_br_lru, v_w_br_mem, v_w_out, v_final_g):
    env = dict(locals())
    W = {n: env[n] for n in WEIGHTS}
    M = {n: env["m_" + n] for n in WEIGHTS}
    V = {n: env["v_" + n] for n in WEIGHTS}
    me = 4 * lax.axis_index("x") + 2 * lax.axis_index("y") + lax.axis_index("c")
    t = x.shape[1]
    xt = x[0]
    memt = mem[0]
    tgt = loss_target[0]

    small_shapes = [W[n].shape for n in SMALL_SHARDED]
    as2d = lambda d, n: jnp.transpose(d[n][0]) if n == "w_in" else d[n][0]
    h, (g_in,) = _rms_fwd(xt, norm_g, "norm_fwd", exchange=_all_gather_plan([as2d(W, "w_in").astype(MXU_DTYPE)]))
    b = SEG_BOUNDS
    w_ssd, w_lru, w_q, w_g = (_gather_rows(g_in, b[0], b[1]), _gather_rows(g_in, b[2], b[3]),
                              _gather_rows(g_in, b[3], b[4]), _gather_rows(g_in, b[4], b[5]))
    w_dt = jnp.pad(_gather_rows(g_in, b[1], b[2]), ((0, DT_PAD - SSD_HEADS), (0, 0)))

    later = _all_gather_plan([as2d(W, n).astype(MXU_DTYPE) for n in BIG[1:]] + [_pack([W[n] for n in SMALL_SHARDED], F32, 8)])
    proj_ssd, (g_kv, g_bs, g_bl, g_bm, g_out, gs) = _matmul(h, w_ssd, "nt", "proj_ssd", tm=4096, tn=512, exchange=later)
    g_cw, g_ng, g_lcw = _unpack(gs, small_shapes, (N_DEV,))
    cols = lambda a: jnp.moveaxis(a[:, 0], 0, -2).reshape(a.shape[2:-1] + (-1,))
    rows_ = lambda a: a.reshape((-1,) + a.shape[2:])
    w_bs_f, w_bl_f, w_bm_f, w_out_f = rows_(g_bs), rows_(g_bl), rows_(g_bm), rows_(g_out)
    conv_w_f, ssd_ng_f, lru_cw_f = cols(g_cw), cols(g_ng), cols(g_lcw)
    w_kv_f = _gather_cols(g_kv, 0, 2 * D_MODEL)

    pad_heads = lambda a: jnp.pad(a, ((0, 0), (0, LANES - SSD_HEADS)))
    dtb, alog = pad_heads(ssd_dt_bias), pad_heads(ssd_a_log)
    d_row = jnp.repeat(ssd_d, SSD_HEAD_DIM, axis=1)
    ng_row = ssd_ng_f.reshape(1, SSD_WIDTH)
    wa_g, wx_g = _block_diag_groups(lru_w_a[0]), _block_diag_groups(lru_w_x[0])
    ba, bx = lru_b_a.reshape(1, LRU_WIDTH), lru_b_x.reshape(1, LRU_WIDTH)
    fg = final_g.reshape(1, D_MODEL)

    proj_lru = _matmul(h, w_lru, "nt", "proj_lru", tm=4096, tn=512)
    proj_q = _matmul(h, w_q, "nt", "proj_q", tm=4096, tn=512)
    proj_g = _matmul(h, w_g, "nt", "proj_g", tm=4096, tn=512)
    proj_dt = _matmul(h, w_dt, "nt", "proj_dt", tm=4096)
    mem_n = _rms_fwd(memt, mem_norm_g, "mem_norm_fwd")
    kv = _matmul(mem_n, w_kv_f, "nn", "mem_kv")
    yssd, y_scan, states, ssd_pre = _ssd_fwd(proj_ssd, proj_dt, conv_w_f, ssd_conv_b, dtb, alog, d_row, ng_row)
    ylru, h_lru, xl_lru = _lru_fwd(proj_lru, lru_cw_f, lru_conv_b, wa_g, wx_g, ba, bx, lru_lambda)
    ymem = _mem_fwd(proj_q, kv)
    ps, pl_, pm, merged, dx2, loss_vec, g_fg = _merge_fwd(xt, yssd, ylru, ymem, proj_g, w_bs_f, w_bl_f, w_bm_f, w_out_f, fg, tgt)

    d_g, dps, dpl, dpm, dyssd, dylru, dymem = _merge_bwd(dx2, proj_g, ps, pl_, pm, w_bs_f, w_bl_f, w_bm_f, w_out_f)
    gw_out = _matmul(merged, dx2, "tn", "grad_w_out", tk=2048)
    gw_bs = _matmul(yssd, dps, "tn", "grad_w_br_ssd", tm=2048)
    gw_bl = _matmul(ylru, dpl, "tn", "grad_w_br_lru", tm=LRU_WIDTH, tk=2048)
    gw_bm = _matmul(ymem, dpm, "tn", "grad_w_br_mem", tk=2048)
    d_q, d_kv = _mem_bwd(proj_q, kv, dymem)
    gw_kv = _matmul(mem_n, d_kv, "tn", "grad_w_kv")
    d_memn = _matmul(d_kv, w_kv_f, "nt", "d_mem_n")
    _, g_memng = _rms_bwd(memt, d_memn, None, mem_norm_g, "mem_norm_bwd")

    core = lax.axis_index("c").astype(jnp.int32).reshape(1)
    chip = (2 * lax.axis_index("x") + lax.axis_index("y")).astype(jnp.int32).reshape(1)
    by_chip = lambda a: a.reshape((N_CHIPS, 2, -1) + a.shape[1:])
    early = ("w_kv", "w_br_ssd", "w_br_lru", "w_br_mem", "w_out")
    early_parts = [by_chip(_scatter_cols([(gw_kv, 0, 2 * D_MODEL)], 2 * D_MODEL // N_DEV).reshape(-1, 2 * D_MODEL // N_DEV)),
                   by_chip(gw_bs), by_chip(gw_bl), by_chip(gw_bm), by_chip(gw_out)]
    (d_lru, gl_cw, gl_cb, g_ba, g_bx, g_lam, gwa_g, gwx_g), early_sib = _lru_bwd(
        proj_lru, xl_lru, h_lru, dylru, lru_cw_f, wa_g, wx_g, ba, bx, lru_lambda, exchange=_pair_plan(early_parts))
    early_sums = [_chip_sum(p, r, core, "chip_sum_" + n) for n, p, r in zip(early, early_parts, early_sib)]
    (d_ssd, d_dt, gs_cw, gs_cb, g_dtb, g_alog, g_dch, g_ngrow), early_recv = _ssd_bwd(
        proj_ssd, ssd_pre, proj_dt, y_scan, states, dyssd, conv_w_f, dtb, alog, d_row, ng_row,
        exchange=_chip_plan([s16 for _, s16 in early_sums]))
    gw_ssd = _matmul(d_ssd, h, "tn", "grad_w_in_ssd", tm=2560)
    gw_lru = _matmul(d_lru, h, "tn", "grad_w_in_lru", tm=1536, tk=2048)
    gw_q = _matmul(d_q, h, "tn", "grad_w_in_q", tk=2048)
    gw_g = _matmul(d_g, h, "tn", "grad_w_in_g", tm=1536, tk=2048)
    gw_dt = _matmul(d_dt, h, "tn", "grad_w_in_dt", tk=2048)
    in_segs = [(gw_ssd, b[0], b[1]), (gw_dt, b[1], b[2]), (gw_lru, b[2], b[3]), (gw_q, b[3], b[4]), (gw_g, b[4], b[5])]
    in_part = _scatter_rows(in_segs, IN_WIDTH // N_DEV).reshape(N_CHIPS, 2, IN_WIDTH // N_DEV, D_MODEL)
    (in_sib,) = _run_exchange(_pair_plan([in_part]), "grad_pair_exchange")
    in_sum = _chip_sum(in_part, in_sib, core, "chip_sum_w_in")

    small_grads = {
        "ssd_conv_w": gs_cw, "ssd_conv_b": gs_cb, "ssd_dt_bias": g_dtb[:, :SSD_HEADS],
        "ssd_a_log": g_alog[:, :SSD_HEADS], "ssd_d": jnp.sum(g_dch.reshape(SSD_HEADS, SSD_HEAD_DIM), axis=1).reshape(1, SSD_HEADS),
        "ssd_norm_g": g_ngrow.reshape(SSD_GROUPS, -1), "lru_conv_w": gl_cw, "lru_conv_b": gl_cb,
        "lru_w_a": _block_diag_extract(gwa_g), "lru_b_a": g_ba, "lru_w_x": _block_diag_extract(gwx_g), "lru_b_x": g_bx,
        "lru_lambda": g_lam, "mem_norm_g": g_memng, "final_g": g_fg,
    }
    small_all = REPLICATED + SMALL_SHARDED

    def small_shape(n, shards):
        shp = W[n].shape[1:] if W[n].ndim > 2 else (1, W[n].shape[-1])
        return shp[:-1] + (shp[-1] * shards,)

    riders = tuple(small_grads)
    small_plan = _all_gather_plan([small_grads[n].reshape(small_shape(n, N_DEV if n in SMALL_SHARDED else 1)) for n in riders])
    dh, landed = _dh([(d_ssd, w_ssd), (d_lru, w_lru), (d_q, w_q), (d_g, w_g), (d_dt, w_dt)],
                     exchange=_both(_chip_plan([in_sum[1]]), small_plan))
    grad_x, g_normg = _rms_bwd(xt, dh, dx2, norm_g, "norm_bwd")
    small_recv = dict(zip(riders, landed[1:]))
    reduced = {"w_in": (in_sum[0], landed[0]), **{n: (s[0], r) for n, s, r in zip(early, early_sums, early_recv)}}

    grads, delta, new_m, new_v = {}, {}, {}, {}
    for n in BIG:
        s32, recv = reduced[n]
        res, landed = _sum_adamw(s32, recv, chip, as2d(W, n), as2d(M, n), as2d(V, n), "adamw_" + n,
                                 exchange=_all_gather_plan([g_normg]) if n == "w_in" else None)
        if n == "w_in":
            small_recv["norm_g"] = landed[0]
        for dst, a in zip((grads, delta, new_m, new_v), res):
            dst[n] = (jnp.transpose(a) if n == "w_in" else a)[None]

    parts = []
    for n in small_all:
        a = small_recv[n]
        if n in SMALL_SHARDED:
            width = W[n].shape[-1]
            a = lax.dynamic_slice_in_dim(a, me * width, width, axis=a.ndim - 1)
        parts.append(a)
    canon = lambda d: [d[n].reshape(small_shape(n, 1)) for n in small_all]
    for dst, res in zip((grads, delta, new_m, new_v), _small_adamw(parts, canon(W), canon(M), canon(V))):
        for n, a in zip(small_all, res):
            dst[n] = a.reshape(W[n].shape)

    loss = lax.psum(loss_vec[0, 0], ("x", "y", "c"))
    return (loss, grad_x[None], *[grads[n] for n in WEIGHTS], *[delta[n] for n in WEIGHTS],
            *[new_m[n] for n in WEIGHTS], *[new_v[n] for n in WEIGHTS])
```

```python
import functools
import math

import jax
import jax.numpy as jnp
from jax import lax
from jax.experimental import pallas as pl
from jax.experimental.pallas import tpu as pltpu

F32 = jnp.float32
MXU_DTYPE = jnp.bfloat16
GRAD_WIRE_DTYPE = jnp.bfloat16

D_MODEL = 1024
EPS = 1e-6
CONV_WIDTH = 4
SSD_WIDTH = 2048
SSD_HEAD_DIM = 64
SSD_HEADS = 32
SSD_GROUPS = 4
SSD_STATE = 128
SSD_CHUNK = 128
SSD_BC = SSD_GROUPS * SSD_STATE
SSD_CONV_CH = SSD_WIDTH + 2 * SSD_BC
SSD_PAIRS = SSD_HEADS // 2
PAIRS_PER_GROUP = SSD_PAIRS // SSD_GROUPS
GROUP_COLS = SSD_WIDTH // SSD_GROUPS
LRU_WIDTH = 1536
LRU_BLOCKS = 16
LRU_BLOCK = 96
LRU_GROUP = 4 * LRU_BLOCK
LRU_NGROUPS = LRU_WIDTH // LRU_GROUP
LRU_C = 8.0
LRU_ROWS = 256
LRU_GATE_TAIL = -8.0
MEM_HEADS = 4
MEM_HEAD_DIM = 256
IN_WIDTH = 12320
N_DEV = 8
LANES = 128
SSD_SEG = SSD_WIDTH + SSD_CONV_CH
DT_PAD = LANES
SEG_BOUNDS = (0, 5120, 5152, 8224, 9248, 12320)

ADAM_LR = 0.001
ADAM_B1 = 0.9
ADAM_B2 = 0.999
ADAM_EPS = 1e-08
ADAM_WD = 0.01
ADAM_STEP = 10

VMEM_LIMIT = 56 * 1024 * 1024

NN = (((1,), (0,)), ((), ()))
NT = (((1,), (1,)), ((), ()))
TN = (((0,), (0,)), ((), ()))


def _dot(a, b, dims):
    return lax.dot_general(a.astype(MXU_DTYPE), b.astype(MXU_DTYPE), dims, preferred_element_type=F32)


def _sigmoid(x):
    return 0.5 * jnp.tanh(0.5 * x) + 0.5


def _log1p(e):
    u = 1.0 + e
    return jnp.where(u == 1.0, e, jnp.log(u) * (e / jnp.where(u == 1.0, 1.0, u - 1.0)))


def _softplus(x):
    return jnp.maximum(x, 0.0) + _log1p(jnp.exp(-jnp.abs(x)))


def _params(semantics):
    return pltpu.CompilerParams(dimension_semantics=semantics, vmem_limit_bytes=VMEM_LIMIT)


def _shift_down(cur, halo8, k):
    rolled = pltpu.roll(cur, k, 0)
    row8 = lax.broadcasted_iota(jnp.int32, halo8.shape, 0)
    top = jnp.where(row8 >= k, rolled[0:8], pltpu.roll(halo8, k, 0))
    return jnp.concatenate([top, rolled[8:]], axis=0)


def _shift_up(cur, next8, k):
    rows = cur.shape[0]
    rolled = pltpu.roll(cur, rows - k, 0)
    row8 = lax.broadcasted_iota(jnp.int32, next8.shape, 0)
    bot = jnp.where(row8 < 8 - k, rolled[rows - 8:rows], pltpu.roll(next8, 8 - k, 0))
    return jnp.concatenate([rolled[:rows - 8], bot], axis=0)


def _causal_conv(raw, halo8, w, b):
    acc = raw * w[3:4, :] + b
    for k in range(1, CONV_WIDTH):
        acc = acc + _shift_down(raw, halo8, k) * w[3 - k:4 - k, :]
    return acc


def _conv_backward(dco, next8, raw, w):
    d_raw = dco * w[3:4, :]
    gw = [None] * CONV_WIDTH
    gw[3] = jnp.sum(dco * raw, axis=0, keepdims=True)
    for j in range(1, CONV_WIDTH):
        up = _shift_up(dco, next8, j)
        d_raw = d_raw + up * w[3 - j:4 - j, :]
        gw[3 - j] = jnp.sum(up * raw, axis=0, keepdims=True)
    gb = jnp.sum(dco, axis=0, keepdims=True)
    return d_raw, gw, gb


def _cumsum_rows(v):
    rows = v.shape[0]
    row = lax.broadcasted_iota(jnp.int32, v.shape, 0)
    s = 1
    while s < rows:
        v = v + jnp.where(row >= s, pltpu.roll(v, s, 0), 0.0)
        s *= 2
    return v


def _rev_cumsum_rows(v):
    rows = v.shape[0]
    row = lax.broadcasted_iota(jnp.int32, v.shape, 0)
    s = 1
    while s < rows:
        v = v + jnp.where(row < rows - s, pltpu.roll(v, rows - s, 0), 0.0)
        s *= 2
    return v


def _matmul(a, b, mode, name, tm=1024, tn=1024, tk=1024, exchange=None):
    if mode == "nn":
        (m, kk), n = a.shape, b.shape[1]
    elif mode == "nt":
        (m, kk), n = a.shape, b.shape[0]
    else:
        (kk, m), n = a.shape, b.shape[1]
    tm, tn, tk = min(tm, m), min(tn, n), min(tk, kk)
    assert m % tm == 0 and n % tn == 0 and kk % tk == 0, (name, a.shape, b.shape)
    nk = kk // tk
    dims = {"nn": NN, "nt": NT, "tn": TN}[mode]
    a_spec = pl.BlockSpec((tk, tm), lambda i, j, k: (k, i)) if mode == "tn" else pl.BlockSpec((tm, tk), lambda i, j, k: (i, k))
    b_spec = pl.BlockSpec((tn, tk), lambda i, j, k: (j, k)) if mode == "nt" else pl.BlockSpec((tk, tn), lambda i, j, k: (k, j))
    o_spec = pl.BlockSpec((tm, tn), lambda i, j, k: (i, j))

    def body_single(a_ref, b_ref, o_ref):
        o_ref[...] = _dot(a_ref[...], b_ref[...], dims)

    def body(a_ref, b_ref, o_ref, acc_ref):
        k = pl.program_id(2)

        @pl.when(k == 0)
        def _():
            acc_ref[...] = jnp.zeros_like(acc_ref)

        acc_ref[...] += _dot(a_ref[...], b_ref[...], dims)

        @pl.when(k == nk - 1)
        def _():
            o_ref[...] = acc_ref[...]

    grid = (m // tm, n // tn, nk)
    if exchange is None:
        return pl.pallas_call(
            body_single if nk == 1 else body, name=name, grid=grid, in_specs=[a_spec, b_spec], out_specs=o_spec,
            out_shape=jax.ShapeDtypeStruct((m, n), F32),
            scratch_shapes=[] if nk == 1 else [pltpu.VMEM((tm, tn), F32)],
            compiler_params=_params(("parallel", "parallel", "arbitrary")),
        )(a, b)
    at = lambda ids: functools.reduce(lambda u, v: u & v, [pl.program_id(d) == ids[d] for d in range(3)])
    riding, ex_in, ex_out, ex_shape, ex_sems = _riding(
        exchange, body_single if nk == 1 else body, 2, 1, lambda: at((0, 0, 0)), lambda: at(tuple(g - 1 for g in grid)))
    res = pl.pallas_call(
        riding, name=name, grid=grid, in_specs=[a_spec, b_spec] + ex_in, out_specs=[o_spec] + ex_out,
        out_shape=[jax.ShapeDtypeStruct((m, n), F32)] + ex_shape,
        scratch_shapes=([] if nk == 1 else [pltpu.VMEM((tm, tn), F32)]) + ex_sems,
        compiler_params=_params(("arbitrary", "arbitrary", "arbitrary")),
    )(a, b, *exchange["arrays"])
    return res[0], res[1:]


def _rms_fwd(x, g, name, rows=512, exchange=None):
    t, d = x.shape
    rows = min(rows, t)
    n_tiles = t // rows

    def body(x_ref, g_ref, h_ref):
        xv = x_ref[...]
        r = lax.rsqrt(jnp.mean(xv * xv, axis=-1, keepdims=True) + EPS)
        h_ref[...] = ((xv * r) * g_ref[...]).astype(h_ref.dtype)

    body, ex_in, ex_out, ex_shape, ex_sems = _riding(
        exchange, body, 2, 1, lambda: pl.program_id(0) == 0, lambda: pl.program_id(0) == n_tiles - 1)
    res = pl.pallas_call(
        body, name=name, grid=(n_tiles,),
        in_specs=[pl.BlockSpec((rows, d), lambda i: (i, 0)), pl.BlockSpec((1, d), lambda i: (0, 0))] + ex_in,
        out_specs=[pl.BlockSpec((rows, d), lambda i: (i, 0))] + ex_out,
        out_shape=[jax.ShapeDtypeStruct((t, d), MXU_DTYPE)] + ex_shape,
        scratch_shapes=ex_sems,
        compiler_params=_params(("arbitrary",) if exchange else ("parallel",)),
    )(x, g, *(exchange["arrays"] if exchange else []))
    return (res[0], res[1:]) if exchange else res[0]


def _rms_bwd(x, dh, dres, g, name, rows=512):
    t, d = x.shape
    rows = min(rows, t)
    has_res = dres is not None

    def body(*refs):
        if has_res:
            x_ref, dh_ref, dr_ref, g_ref, dx_ref, gg_ref = refs
        else:
            x_ref, dh_ref, g_ref, dx_ref, gg_ref = refs

        @pl.when(pl.program_id(0) == 0)
        def _():
            gg_ref[...] = jnp.zeros_like(gg_ref)

        xv = x_ref[...]
        dhv = dh_ref[...]
        r = lax.rsqrt(jnp.mean(xv * xv, axis=-1, keepdims=True) + EPS)
        n = xv * r
        dn = dhv * g_ref[...]
        dx = r * (dn - n * jnp.mean(dn * n, axis=-1, keepdims=True))
        if has_res:
            dx = dx + dr_ref[...]
        dx_ref[...] = dx
        gg_ref[...] += jnp.sum(dhv * n, axis=0, keepdims=True)

    row_spec = pl.BlockSpec((rows, d), lambda i: (i, 0))
    vec_spec = pl.BlockSpec((1, d), lambda i: (0, 0))
    args = (x, dh) + ((dres,) if has_res else ()) + (g,)
    return pl.pallas_call(
        body, name=name, grid=(t // rows,),
        in_specs=[row_spec, row_spec] + ([row_spec] if has_res else []) + [vec_spec],
        out_specs=[row_spec, vec_spec],
        out_shape=[jax.ShapeDtypeStruct((t, d), F32), jax.ShapeDtypeStruct((1, d), F32)],
        compiler_params=_params(("arbitrary",)),
    )(*args)


def _dh(segs, rows=1024, tk=1024, exchange=None):
    t, d = segs[0][0].shape[0], segs[0][1].shape[1]
    rows = min(rows, t)
    steps = []
    step0 = 0
    for a, _ in segs:
        kb = min(tk, a.shape[1])
        assert a.shape[1] % kb == 0, a.shape
        steps.append((step0, a.shape[1] // kb, kb))
        step0 += a.shape[1] // kb
    n_steps = step0
    ns = len(segs)

    def body(*refs):
        a_refs, w_refs = refs[0:2 * ns:2], refs[1:2 * ns:2]
        dh_ref, acc_ref = refs[2 * ns:]
        k = pl.program_id(1)

        @pl.when(k == 0)
        def _():
            acc_ref[...] = jnp.zeros_like(acc_ref)

        for s, (first, nblk, _) in enumerate(steps):
            @pl.when((k >= first) & (k < first + nblk))
            def _(s=s):
                acc_ref[...] += _dot(a_refs[s][...], w_refs[s][...], NN)

        @pl.when(k == n_steps - 1)
        def _():
            dh_ref[...] = acc_ref[...]

    in_specs, args = [], []
    for (a, w), (first, nblk, kb) in zip(segs, steps):
        blk = lambda k, first=first, nblk=nblk: jnp.clip(k - first, 0, nblk - 1)
        in_specs.append(pl.BlockSpec((rows, kb), lambda i, k, blk=blk: (i, blk(k))))
        in_specs.append(pl.BlockSpec((kb, d), lambda i, k, blk=blk: (blk(k), 0)))
        args += [a, w]
    row_spec = pl.BlockSpec((rows, d), lambda i, k: (i, 0))
    n_tiles = t // rows
    body, ex_in, ex_out, ex_shape, ex_sems = _riding(
        exchange, body, 2 * ns, 1,
        lambda: (pl.program_id(0) == 0) & (pl.program_id(1) == 0),
        lambda: (pl.program_id(0) == n_tiles - 1) & (pl.program_id(1) == n_steps - 1))
    res = pl.pallas_call(
        body, name="dh", grid=(n_tiles, n_steps),
        in_specs=in_specs + ex_in, out_specs=[row_spec] + ex_out,
        out_shape=[jax.ShapeDtypeStruct((t, d), F32)] + ex_shape,
        scratch_shapes=[pltpu.VMEM((rows, d), F32)] + ex_sems,
        compiler_params=_params(("arbitrary", "arbitrary")),
    )(*args, *(exchange["arrays"] if exchange else []))
    return res[0], res[1:]


def _pair_select(lo, m, h0):
    return jnp.where(lo, m[:, h0:h0 + 1], m[:, h0 + 1:h0 + 2])


def _group_select(lo, m, heads):
    return jnp.concatenate([_pair_select(lo, m, h0) for h0 in heads], axis=1)


def _halves(lo, v):
    return (jnp.sum(jnp.where(lo, v, 0.0), axis=1, keepdims=True),
            jnp.sum(jnp.where(lo, 0.0, v), axis=1, keepdims=True))


def _ssd_common(dt_raw, dtb, alog):
    dt = _softplus(dt_raw + dtb)
    aneg = -jnp.exp(alog)
    a_cs = _cumsum_rows(dt * aneg)
    return dt, aneg, a_cs, a_cs.T


def _ssd_specs(nc, rev):
    cidx = (lambda c: nc - 1 - c) if rev else (lambda c: c)
    L = SSD_CHUNK
    b_proj = 2 * SSD_WIDTH // SSD_BC
    b_conv = SSD_WIDTH // SSD_BC
    return dict(
        z=pl.BlockSpec((L, SSD_WIDTH), lambda c: (cidx(c), 0)),
        xr=pl.BlockSpec((L, SSD_WIDTH), lambda c: (cidx(c), 1)),
        br=pl.BlockSpec((L, SSD_BC), lambda c: (cidx(c), b_proj)),
        cr=pl.BlockSpec((L, SSD_BC), lambda c: (cidx(c), b_proj + 1)),
        dt=pl.BlockSpec((L, DT_PAD), lambda c: (cidx(c), 0)),
        cwx=pl.BlockSpec((CONV_WIDTH, SSD_WIDTH), lambda c: (0, 0)),
        cwb=pl.BlockSpec((CONV_WIDTH, SSD_BC), lambda c: (0, b_conv)),
        cwc=pl.BlockSpec((CONV_WIDTH, SSD_BC), lambda c: (0, b_conv + 1)),
        cbx=pl.BlockSpec((1, SSD_WIDTH), lambda c: (0, 0)),
        cbb=pl.BlockSpec((1, SSD_BC), lambda c: (0, b_conv)),
        cbc=pl.BlockSpec((1, SSD_BC), lambda c: (0, b_conv + 1)),
        vec128=pl.BlockSpec((1, LANES), lambda c: (0, 0)),
        vecw=pl.BlockSpec((1, SSD_WIDTH), lambda c: (0, 0)),
        wide=pl.BlockSpec((L, SSD_WIDTH), lambda c: (cidx(c), 0)),
        states=pl.BlockSpec((1, SSD_GROUPS, GROUP_COLS, SSD_STATE), lambda c: (cidx(c), 0, 0, 0)),
    )


def _ssd_fwd(proj_ssd, dt_p, conv_w, conv_b, dtb, alog, d_row, ng_row):
    t = proj_ssd.shape[0]
    nc = t // SSD_CHUNK
    L = SSD_CHUNK
    sp = _ssd_specs(nc, False)

    def body(z_ref, xr_ref, br_ref, cr_ref, dt_ref, cwx_ref, cwb_ref, cwc_ref, cbx_ref, cbb_ref, cbc_ref,
             dtb_ref, alog_ref, d_ref, ng_ref, yssd_ref, y_ref, st_ref, pre_ref,
             hx_ref, hb_ref, hc_ref, state_ref, yacc_ref):
        @pl.when(pl.program_id(0) == 0)
        def _():
            hx_ref[...] = jnp.zeros_like(hx_ref)
            hb_ref[...] = jnp.zeros_like(hb_ref)
            hc_ref[...] = jnp.zeros_like(hc_ref)
            state_ref[...] = jnp.zeros_like(state_ref)

        xr, br, cr = xr_ref[...], br_ref[...], cr_ref[...]
        px = _causal_conv(xr, hx_ref[...], cwx_ref[...], cbx_ref[...])
        pb = _causal_conv(br, hb_ref[...], cwb_ref[...], cbb_ref[...])
        pc = _causal_conv(cr, hc_ref[...], cwc_ref[...], cbc_ref[...])
        hx_ref[...] = xr[L - 8:L, :]
        hb_ref[...] = br[L - 8:L, :]
        hc_ref[...] = cr[L - 8:L, :]
        pre_ref[:, 0:SSD_WIDTH] = px
        pre_ref[:, SSD_WIDTH:SSD_WIDTH + SSD_BC] = pb
        pre_ref[:, SSD_WIDTH + SSD_BC:SSD_CONV_CH] = pc
        xs = px * _sigmoid(px)
        bm = pb * _sigmoid(pb)
        cm = pc * _sigmoid(pc)

        dt, _, a_cs, a_t = _ssd_common(dt_ref[...], dtb_ref[...], alog_ref[...])
        exp_a = jnp.exp(a_cs)
        a_last = a_cs[L - 1:L, :]
        dte = jnp.exp(a_last - a_cs)
        dec = jnp.exp(a_last)

        lane = lax.broadcasted_iota(jnp.int32, (L, LANES), 1)
        sub = lax.broadcasted_iota(jnp.int32, (L, LANES), 0)
        lo = lane < SSD_HEAD_DIM
        causal = sub >= lane
        top = sub < SSD_HEAD_DIM

        for g in range(SSD_GROUPS):
            b_g = bm[:, g * SSD_STATE:(g + 1) * SSD_STATE]
            c_g = cm[:, g * SSD_STATE:(g + 1) * SSD_STATE]
            cb = _dot(c_g, b_g, NT)
            heads = [2 * (g * PAIRS_PER_GROUP + jj) for jj in range(PAIRS_PER_GROUP)]
            gcols = slice(g * GROUP_COLS, (g + 1) * GROUP_COLS)
            xs_g = xs[:, gcols]
            xdt_g = xs_g * _group_select(lo, dt, heads)
            h_g = state_ref[g]
            st_ref[0, g] = h_g
            y_off_g = _dot(c_g, h_g, NT) * _group_select(lo, exp_a, heads)
            s_new_g = _dot(xdt_g * _group_select(lo, dte, heads), b_g, TN)
            for jj, h0 in enumerate(heads):
                blk = slice(jj * LANES, (jj + 1) * LANES)
                cols = slice(g * GROUP_COLS + jj * LANES, g * GROUP_COLS + (jj + 1) * LANES)
                xdt = xdt_g[:, blk]
                g0 = jnp.where(causal, jnp.exp(a_cs[:, h0:h0 + 1] - a_t[h0:h0 + 1, :]), 0.0) * cb
                g1 = jnp.where(causal, jnp.exp(a_cs[:, h0 + 1:h0 + 2] - a_t[h0 + 1:h0 + 2, :]), 0.0) * cb
                lhs = jnp.concatenate([g0, g1], axis=1)
                rhs = jnp.concatenate([jnp.where(lo, xdt, 0.0), jnp.where(lo, 0.0, xdt)], axis=0)
                y_diag = _dot(lhs, rhs, NN)
                dec_rows = jnp.where(top, dec[:, h0:h0 + 1], dec[:, h0 + 1:h0 + 2])
                state_ref[g, blk, :] = h_g[blk, :] * dec_rows + s_new_g[blk, :]
                yacc_ref[:, cols] = (y_diag + y_off_g[:, blk]) + xs_g[:, blk] * d_ref[:, cols]

        y = yacc_ref[...]
        y_ref[...] = y
        zz = z_ref[...]
        y2 = y * (zz * _sigmoid(zz))
        gw = SSD_WIDTH // SSD_GROUPS
        for g in range(SSD_GROUPS):
            seg = y2[:, g * gw:(g + 1) * gw]
            r = lax.rsqrt(jnp.mean(seg * seg, axis=-1, keepdims=True) + EPS)
            yssd_ref[:, g * gw:(g + 1) * gw] = ((seg * r) * ng_ref[:, g * gw:(g + 1) * gw]).astype(yssd_ref.dtype)

    return pl.pallas_call(
        body, name="ssd_fwd", grid=(nc,),
        in_specs=[sp["z"], sp["xr"], sp["br"], sp["cr"], sp["dt"], sp["cwx"], sp["cwb"], sp["cwc"],
                  sp["cbx"], sp["cbb"], sp["cbc"], sp["vec128"], sp["vec128"], sp["vecw"], sp["vecw"]],
        out_specs=[sp["wide"], sp["wide"], sp["states"], pl.BlockSpec((L, SSD_CONV_CH), lambda c: (c, 0))],
        out_shape=[jax.ShapeDtypeStruct((t, SSD_WIDTH), MXU_DTYPE), jax.ShapeDtypeStruct((t, SSD_WIDTH), F32),
                   jax.ShapeDtypeStruct((nc, SSD_GROUPS, GROUP_COLS, SSD_STATE), F32), jax.ShapeDtypeStruct((t, SSD_CONV_CH), F32)],
        scratch_shapes=[pltpu.VMEM((8, SSD_WIDTH), F32), pltpu.VMEM((8, SSD_BC), F32), pltpu.VMEM((8, SSD_BC), F32),
                        pltpu.VMEM((SSD_GROUPS, GROUP_COLS, SSD_STATE), F32), pltpu.VMEM((L, SSD_WIDTH), F32)],
        compiler_params=_params(("arbitrary",)),
    )(proj_ssd, proj_ssd, proj_ssd, proj_ssd, dt_p, conv_w, conv_w, conv_w, conv_b, conv_b, conv_b,
      dtb, alog, d_row, ng_row)


def _ssd_bwd(proj_ssd, pre, dt_p, y, states, dyssd, conv_w, dtb, alog, d_row, ng_row, exchange=None):
    t = proj_ssd.shape[0]
    nc = t // SSD_CHUNK
    L = SSD_CHUNK
    sp = _ssd_specs(nc, True)

    def pre_spec(width, col):
        return pl.BlockSpec((L, width), lambda c: (nc - 1 - c, col))

    def body(z_ref, xr_ref, br_ref, cr_ref, px_ref, pb_ref, pc_ref, dt_ref, y_ref, st_ref, dy_ref,
             cwx_ref, cwb_ref, cwc_ref, dtb_ref, alog_ref, d_ref, ng_ref,
             dssd_ref, ddt_ref, gcw_ref, gcb_ref, gdtb_ref, galog_ref, gd_ref, gng_ref,
             gn_ref, nx_ref, nb_ref, ncc_ref, dxs_ref):
        step = pl.program_id(0)

        @pl.when(step == 0)
        def _():
            gn_ref[...] = jnp.zeros_like(gn_ref)
            nx_ref[...] = jnp.zeros_like(nx_ref)
            nb_ref[...] = jnp.zeros_like(nb_ref)
            ncc_ref[...] = jnp.zeros_like(ncc_ref)
            for ref in (gcw_ref, gcb_ref, gdtb_ref, galog_ref, gd_ref, gng_ref):
                ref[...] = jnp.zeros_like(ref)

        xr, br, cr = xr_ref[...], br_ref[...], cr_ref[...]
        cwx, cwb, cwc = cwx_ref[...], cwb_ref[...], cwc_ref[...]
        px, pb, pc = px_ref[...], pb_ref[...], pc_ref[...]
        sx, sb, sc = _sigmoid(px), _sigmoid(pb), _sigmoid(pc)
        xs, bm, cm = px * sx, pb * sb, pc * sc

        dt_in = dt_ref[...] + dtb_ref[...]
        dt, aneg, a_cs, a_t = _ssd_common(dt_ref[...], dtb_ref[...], alog_ref[...])
        exp_a = jnp.exp(a_cs)
        a_last = a_cs[L - 1:L, :]
        dte = jnp.exp(a_last - a_cs)
        dec = jnp.exp(a_last)

        lane = lax.broadcasted_iota(jnp.int32, (L, LANES), 1)
        sub = lax.broadcasted_iota(jnp.int32, (L, LANES), 0)
        lo = lane < SSD_HEAD_DIM
        causal = sub >= lane
        top = sub < SSD_HEAD_DIM
        last_row = sub == L - 1

        yv = y_ref[...]
        zz = z_ref[...]
        sz = _sigmoid(zz)
        silz = zz * sz
        y2 = yv * silz
        dyv = dy_ref[...]
        gw = SSD_WIDTH // SSD_GROUPS
        d_y2_parts = []
        gng_parts = []
        for g in range(SSD_GROUPS):
            seg = y2[:, g * gw:(g + 1) * gw]
            dseg = dyv[:, g * gw:(g + 1) * gw]
            r = lax.rsqrt(jnp.mean(seg * seg, axis=-1, keepdims=True) + EPS)
            n = seg * r
            dn = dseg * ng_ref[:, g * gw:(g + 1) * gw]
            gng_parts.append(jnp.sum(dseg * n, axis=0, keepdims=True))
            d_y2_parts.append(r * (dn - n * jnp.mean(dn * n, axis=-1, keepdims=True)))
        d_y2 = jnp.concatenate(d_y2_parts, axis=1)
        gng_ref[...] += jnp.concatenate(gng_parts, axis=1)
        d_y = d_y2 * silz
        dssd_ref[:, 0:SSD_WIDTH] = (d_y2 * yv * (sz * (1.0 + zz * (1.0 - sz)))).astype(dssd_ref.dtype)
        gd_ref[...] += jnp.sum(d_y * xs, axis=0, keepdims=True)
        dxs_ref[...] = d_y * d_ref[...]

        d_a = jnp.zeros((L, LANES), F32)
        d_at = jnp.zeros((LANES, L), F32)
        ddt = jnp.zeros((L, LANES), F32)
        d_b_parts, d_c_parts = [], []
        for g in range(SSD_GROUPS):
            b_g = bm[:, g * SSD_STATE:(g + 1) * SSD_STATE]
            c_g = cm[:, g * SSD_STATE:(g + 1) * SSD_STATE]
            cb = _dot(c_g, b_g, NT)
            d_cb = jnp.zeros((L, L), F32)
            heads = [2 * (g * PAIRS_PER_GROUP + jj) for jj in range(PAIRS_PER_GROUP)]
            gcols = slice(g * GROUP_COLS, (g + 1) * GROUP_COLS)
            dy_g, xs_g = d_y[:, gcols], xs[:, gcols]
            dt_g = _group_select(lo, dt, heads)
            expa_g = _group_select(lo, exp_a, heads)
            dte_g = _group_select(lo, dte, heads)
            xdt_g = xs_g * dt_g
            h_g = st_ref[0, g]
            gn_g = gn_ref[g]
            dys_g = dy_g * expa_g
            d_cg = _dot(dys_g, h_g, NN)
            d_h_g = _dot(dys_g, c_g, TN)
            t1_g = dy_g * _dot(c_g, h_g, NT) * expa_g
            d_bg = _dot(xdt_g * dte_g, gn_g, NN)
            dxdt_g = _dot(b_g, gn_g, NT) * dte_g
            t2_g = dxdt_g * xdt_g
            t12_g = t1_g - t2_g
            gh_g = jnp.sum(gn_g * h_g, axis=1, keepdims=True)
            for jj, h0 in enumerate(heads):
                blk = slice(jj * LANES, (jj + 1) * LANES)
                cols = slice(g * GROUP_COLS + jj * LANES, g * GROUP_COLS + (jj + 1) * LANES)
                dy_p, xs_p, xdt, dt_pp = dy_g[:, blk], xs_g[:, blk], xdt_g[:, blk], dt_g[:, blk]
                l0 = jnp.where(causal, jnp.exp(a_cs[:, h0:h0 + 1] - a_t[h0:h0 + 1, :]), 0.0)
                l1 = jnp.where(causal, jnp.exp(a_cs[:, h0 + 1:h0 + 2] - a_t[h0 + 1:h0 + 2, :]), 0.0)
                g0, g1 = l0 * cb, l1 * cb
                dcat = jnp.concatenate([jnp.where(lo, dy_p, 0.0), jnp.where(lo, 0.0, dy_p)], axis=0)
                d_xdt = dxdt_g[:, blk] + _dot(jnp.concatenate([g0, g1], axis=0), dcat, TN)
                dm = _dot(dcat, xdt, NT)
                dm0, dm1 = dm[0:L], dm[L:2 * L]
                d_cb = d_cb + (l0 * dm0 + l1 * dm1)
                e0, e1 = dm0 * g0, dm1 * g1
                a0, a1 = _halves(lo, t12_g[:, blk])
                a0 = a0 + jnp.sum(e0, axis=1, keepdims=True)
                a1 = a1 + jnp.sum(e1, axis=1, keepdims=True)
                s0, s1 = _halves(lo, t2_g[:, blk])
                gh = gh_g[blk, :]
                dd0 = jnp.sum(jnp.where(top[:, 0:1], gh, 0.0), axis=0, keepdims=True)
                dd1 = jnp.sum(jnp.where(top[:, 0:1], 0.0, gh), axis=0, keepdims=True)
                end0 = jnp.sum(s0, axis=0, keepdims=True) + dd0 * dec[:, h0:h0 + 1]
                end1 = jnp.sum(s1, axis=0, keepdims=True) + dd1 * dec[:, h0 + 1:h0 + 2]
                d_a = d_a + jnp.where(lane == h0, a0 + jnp.where(last_row, end0, 0.0), 0.0)
                d_a = d_a + jnp.where(lane == h0 + 1, a1 + jnp.where(last_row, end1, 0.0), 0.0)
                d_at = d_at - jnp.where(sub == h0, jnp.sum(e0, axis=0, keepdims=True), 0.0)
                d_at = d_at - jnp.where(sub == h0 + 1, jnp.sum(e1, axis=0, keepdims=True), 0.0)
                dec_rows = jnp.where(top, dec[:, h0:h0 + 1], dec[:, h0 + 1:h0 + 2])
                gn_ref[g, blk, :] = d_h_g[blk, :] + dec_rows * gn_g[blk, :]
                q0, q1 = _halves(lo, d_xdt * xs_p)
                ddt = ddt + jnp.where(lane == h0, q0, 0.0) + jnp.where(lane == h0 + 1, q1, 0.0)
                dxs_ref[:, cols] += d_xdt * dt_pp
            d_cg = d_cg + _dot(d_cb, b_g, NN)
            d_bg = d_bg + _dot(d_cb, c_g, TN)
            d_b_parts.append(d_bg)
            d_c_parts.append(d_cg)

        rc = _rev_cumsum_rows(d_a + d_at.T)
        d_dt = rc * aneg + ddt
        galog_ref[...] += jnp.sum(rc * dt, axis=0, keepdims=True) * aneg
        d_dtraw = d_dt * _sigmoid(dt_in)
        gdtb_ref[...] += jnp.sum(d_dtraw, axis=0, keepdims=True)
        ddt_ref[...] = d_dtraw.astype(ddt_ref.dtype)

        def dsilu(p, s):
            return s * (1.0 + p * (1.0 - s))

        dcx = dxs_ref[...] * dsilu(px, sx)
        dcb = jnp.concatenate(d_b_parts, axis=1) * dsilu(pb, sb)
        dcc = jnp.concatenate(d_c_parts, axis=1) * dsilu(pc, sc)
        drx, gwx, gbx = _conv_backward(dcx, nx_ref[...], xr, cwx)
        drb, gwb, gbb = _conv_backward(dcb, nb_ref[...], br, cwb)
        drc, gwc, gbc = _conv_backward(dcc, ncc_ref[...], cr, cwc)
        nx_ref[...] = dcx[0:8, :]
        nb_ref[...] = dcb[0:8, :]
        ncc_ref[...] = dcc[0:8, :]
        dssd_ref[:, SSD_WIDTH:2 * SSD_WIDTH] = drx.astype(dssd_ref.dtype)
        dssd_ref[:, 2 * SSD_WIDTH:2 * SSD_WIDTH + SSD_BC] = drb.astype(dssd_ref.dtype)
        dssd_ref[:, 2 * SSD_WIDTH + SSD_BC:SSD_SEG] = drc.astype(dssd_ref.dtype)
        for k in range(CONV_WIDTH):
            gcw_ref[k:k + 1, :] += jnp.concatenate([gwx[k], gwb[k], gwc[k]], axis=1)
        gcb_ref[...] += jnp.concatenate([gbx, gbb, gbc], axis=1)

    const = lambda shape: pl.BlockSpec(shape, lambda c: (0,) * len(shape))
    body, ex_in, ex_out, ex_shape, ex_sems = _riding(
        exchange, body, 18, 8, lambda: pl.program_id(0) == 0, lambda: pl.program_id(0) == nc - 1)
    res = pl.pallas_call(
        body, name="ssd_bwd", grid=(nc,),
        in_specs=[sp["z"], sp["xr"], sp["br"], sp["cr"], pre_spec(SSD_WIDTH, 0), pre_spec(SSD_BC, SSD_WIDTH // SSD_BC), pre_spec(SSD_BC, SSD_WIDTH // SSD_BC + 1),
                  sp["dt"], sp["wide"], sp["states"], sp["wide"],
                  sp["cwx"], sp["cwb"], sp["cwc"], sp["vec128"], sp["vec128"], sp["vecw"], sp["vecw"]] + ex_in,
        out_specs=[pl.BlockSpec((L, SSD_SEG), lambda c: (nc - 1 - c, 0)), sp["dt"],
                   const((CONV_WIDTH, SSD_CONV_CH)), const((1, SSD_CONV_CH)), const((1, LANES)), const((1, LANES)),
                   const((1, SSD_WIDTH)), const((1, SSD_WIDTH))] + ex_out,
        out_shape=[jax.ShapeDtypeStruct((t, SSD_SEG), MXU_DTYPE), jax.ShapeDtypeStruct((t, DT_PAD), MXU_DTYPE),
                   jax.ShapeDtypeStruct((CONV_WIDTH, SSD_CONV_CH), F32), jax.ShapeDtypeStruct((1, SSD_CONV_CH), F32),
                   jax.ShapeDtypeStruct((1, LANES), F32), jax.ShapeDtypeStruct((1, LANES), F32),
                   jax.ShapeDtypeStruct((1, SSD_WIDTH), F32), jax.ShapeDtypeStruct((1, SSD_WIDTH), F32)] + ex_shape,
        scratch_shapes=[pltpu.VMEM((SSD_GROUPS, GROUP_COLS, SSD_STATE), F32), pltpu.VMEM((8, SSD_WIDTH), F32),
                        pltpu.VMEM((8, SSD_BC), F32), pltpu.VMEM((8, SSD_BC), F32), pltpu.VMEM((L, SSD_WIDTH), F32)] + ex_sems,
        compiler_params=_params(("arbitrary",)),
    )(proj_ssd, proj_ssd, proj_ssd, proj_ssd, pre, pre, pre, dt_p, y, states, dyssd,
      conv_w, conv_w, conv_w, dtb, alog, d_row, ng_row, *(exchange["arrays"] if exchange else []))
    return res[:8], res[8:]


def _lru_gates(xl, wa_ref, wx_ref, ba, bx, lam):
    pre_a, pre_x = [], []
    for g in range(LRU_NGROUPS):
        xg = xl[:, g * LRU_GROUP:(g + 1) * LRU_GROUP]
        pre_a.append(_dot(xg, wa_ref[g], NN))
        pre_x.append(_dot(xg, wx_ref[g], NN))
    pa = jnp.concatenate(pre_a, axis=1) + ba
    tail = jnp.exp(jnp.minimum(pa, LRU_GATE_TAIL))
    r = jnp.where(pa < LRU_GATE_TAIL, tail * (1.0 - tail), _sigmoid(pa))
    i = _sigmoid(jnp.concatenate(pre_x, axis=1) + bx)
    log_a = (-LRU_C * r) * _softplus(-lam)
    a = jnp.exp(log_a)
    mult_sq = -jnp.tanh(log_a) * (a * a + 1.0)
    return r, i, mult_sq, a, jnp.sqrt(mult_sq)


def _scan_rows(p, u, carry, reverse):
    rows, w = p.shape
    groups = rows // 8
    p3, u3 = p.reshape(groups, 8, w), u.reshape(groups, 8, w)
    row = lax.broadcasted_iota(jnp.int32, (groups, 8, w), 1)
    for s in (1, 2, 4):
        ok = row < 8 - s if reverse else row >= s
        shift = 8 - s if reverse else s
        u3 = p3 * jnp.where(ok, pltpu.roll(u3, shift, 1), 0.0) + u3
        p3 = p3 * jnp.where(ok, pltpu.roll(p3, shift, 1), 1.0)
    out = [None] * groups
    for k in (range(groups - 1, -1, -1) if reverse else range(groups)):
        out[k] = p3[k] * carry + u3[k]
        carry = out[k][0:1, :] if reverse else out[k][7:8, :]
    return jnp.concatenate(out, axis=0), carry


def _lru_fwd(proj_lru, conv_w, conv_b, wa, wx, ba, bx, lam):
    t = proj_lru.shape[0]
    rows = min(LRU_ROWS, t)
    nb = t // rows
    W = LRU_WIDTH

    def body(lg_ref, lx_ref, cw_ref, cb_ref, wa_ref, wx_ref, ba_ref, bx_ref, lam_ref, ylru_ref, h_ref, xl_ref,
             halo_ref, carry_ref):
        @pl.when(pl.program_id(0) == 0)
        def _():
            halo_ref[...] = jnp.zeros_like(halo_ref)
            carry_ref[...] = jnp.zeros_like(carry_ref)

        lx = lx_ref[...]
        xl = _causal_conv(lx, halo_ref[...], cw_ref[...], cb_ref[...])
        halo_ref[...] = lx[rows - 8:rows, :]
        xl_ref[...] = xl
        _, i, _, a, mult = _lru_gates(xl, wa_ref, wx_ref, ba_ref[...], bx_ref[...], lam_ref[...])
        u = mult * (i * xl)
        h, carry_ref[...] = _scan_rows(a, u, carry_ref[...], False)
        h_ref[...] = h
        lg = lg_ref[...]
        ylru_ref[...] = (h * (lg * _sigmoid(lg))).astype(ylru_ref.dtype)

    const = lambda shape: pl.BlockSpec(shape, lambda b: (0,) * len(shape))
    return pl.pallas_call(
        body, name="lru_fwd", grid=(nb,),
        in_specs=[pl.BlockSpec((rows, W), lambda b: (b, 0)), pl.BlockSpec((rows, W), lambda b: (b, 1)),
                  const((CONV_WIDTH, W)), const((1, W)), const((LRU_NGROUPS, LRU_GROUP, LRU_GROUP)),
                  const((LRU_NGROUPS, LRU_GROUP, LRU_GROUP)), const((1, W)), const((1, W)), const((1, W))],
        out_specs=[pl.BlockSpec((rows, W), lambda b: (b, 0))] * 3,
        out_shape=[jax.ShapeDtypeStruct((t, W), MXU_DTYPE), jax.ShapeDtypeStruct((t, W), F32), jax.ShapeDtypeStruct((t, W), F32)],
        scratch_shapes=[pltpu.VMEM((8, W), F32), pltpu.VMEM((1, W), F32)],
        compiler_params=_params(("arbitrary",)),
    )(proj_lru, proj_lru, conv_w, conv_b, wa, wx, ba, bx, lam)


def _lru_bwd(proj_lru, xl, h, dylru, conv_w, wa, wx, ba, bx, lam, exchange=None):
    t = proj_lru.shape[0]
    rows = min(LRU_ROWS, t)
    nb = t // rows
    W = LRU_WIDTH
    groups8 = rows // 8

    def rev(b):
        return nb - 1 - b

    def halo_spec(col):
        return pl.BlockSpec((8, W), lambda b: (jnp.maximum(rev(b) * groups8 - 1, 0), col))

    def body(lg_ref, lx_ref, xl_ref, h_ref, hh_ref, dy_ref, cw_ref, wa_ref, wx_ref, ba_ref, bx_ref, lam_ref,
             dlru_ref, gcw_ref, gcb_ref, gba_ref, gbx_ref, glam_ref, gwa_ref, gwx_ref,
             gcarry_ref, afirst_ref, nxt_ref):
        step = pl.program_id(0)

        @pl.when(step == 0)
        def _():
            gcarry_ref[...] = jnp.zeros_like(gcarry_ref)
            afirst_ref[...] = jnp.zeros_like(afirst_ref)
            nxt_ref[...] = jnp.zeros_like(nxt_ref)
            for ref in (gcw_ref, gcb_ref, gba_ref, gbx_ref, glam_ref, gwa_ref, gwx_ref):
                ref[...] = jnp.zeros_like(ref)

        keep = jnp.where(step == nb - 1, 0.0, 1.0)
        lx = lx_ref[...]
        cw = cw_ref[...]
        xl = xl_ref[...]
        lam = lam_ref[...]
        r, i, mult_sq, a, mult = _lru_gates(xl, wa_ref, wx_ref, ba_ref[...], bx_ref[...], lam)
        hv = h_ref[...]
        h_prev = _shift_down(hv, hh_ref[...] * keep, 1)
        lg = lg_ref[...]
        sg = _sigmoid(lg)
        dyv = dy_ref[...]
        d_h = dyv * (lg * sg)
        dlru_ref[:, 0:W] = (dyv * hv * (sg * (1.0 + lg * (1.0 - sg)))).astype(dlru_ref.dtype)

        row = lax.broadcasted_iota(jnp.int32, (rows, W), 0)
        p = jnp.where(row < rows - 1, pltpu.roll(a, rows - 1, 0), afirst_ref[...])
        gsc, gcarry_ref[...] = _scan_rows(p, d_h, gcarry_ref[...], True)
        afirst_ref[...] = a[0:1, :]

        d_a = gsc * h_prev
        v = i * xl
        d_mult = gsc * v
        d_v = gsc * mult
        d_i = d_v * xl
        d_xl = d_v * i
        d_la = d_a * a - d_mult * (a * a) * lax.rsqrt(mult_sq)
        sp_neg = _softplus(-lam)
        d_r = d_la * (-LRU_C * sp_neg)
        glam_ref[...] += jnp.sum(d_la * r, axis=0, keepdims=True) * (LRU_C * _sigmoid(-lam))
        d_pa = d_r * r * (1.0 - r)
        d_px = d_i * i * (1.0 - i)
        gba_ref[...] += jnp.sum(d_pa, axis=0, keepdims=True)
        gbx_ref[...] += jnp.sum(d_px, axis=0, keepdims=True)
        parts = []
        for g in range(LRU_NGROUPS):
            cols = slice(g * LRU_GROUP, (g + 1) * LRU_GROUP)
            xg, dpa_g, dpx_g = xl[:, cols], d_pa[:, cols], d_px[:, cols]
            parts.append(_dot(dpa_g, wa_ref[g], NT) + _dot(dpx_g, wx_ref[g], NT))
            gwa_ref[g] += _dot(xg, dpa_g, TN)
            gwx_ref[g] += _dot(xg, dpx_g, TN)
        d_xl = d_xl + jnp.concatenate(parts, axis=1)
        d_lx, gw, gb = _conv_backward(d_xl, nxt_ref[...], lx, cw)
        nxt_ref[...] = d_xl[0:8, :]
        dlru_ref[:, W:2 * W] = d_lx.astype(dlru_ref.dtype)
        for k in range(CONV_WIDTH):
            gcw_ref[k:k + 1, :] += gw[k]
        gcb_ref[...] += gb

    const = lambda shape: pl.BlockSpec(shape, lambda b: (0,) * len(shape))
    wspec = const((LRU_NGROUPS, LRU_GROUP, LRU_GROUP))
    blk = lambda col: pl.BlockSpec((rows, W), lambda b: (rev(b), col))
    body, ex_in, ex_out, ex_shape, ex_sems = _riding(
        exchange, body, 12, 8, lambda: pl.program_id(0) == 0, lambda: pl.program_id(0) == nb - 1)
    res = pl.pallas_call(
        body, name="lru_bwd", grid=(nb,),
        in_specs=[blk(0), blk(1), blk(0), blk(0), halo_spec(0), blk(0),
                  const((CONV_WIDTH, W)), wspec, wspec, const((1, W)), const((1, W)), const((1, W))] + ex_in,
        out_specs=[pl.BlockSpec((rows, 2 * W), lambda b: (rev(b), 0)), const((CONV_WIDTH, W)), const((1, W)),
                   const((1, W)), const((1, W)), const((1, W)), wspec, wspec] + ex_out,
        out_shape=[jax.ShapeDtypeStruct((t, 2 * W), MXU_DTYPE), jax.ShapeDtypeStruct((CONV_WIDTH, W), F32),
                   jax.ShapeDtypeStruct((1, W), F32), jax.ShapeDtypeStruct((1, W), F32), jax.ShapeDtypeStruct((1, W), F32),
                   jax.ShapeDtypeStruct((1, W), F32), jax.ShapeDtypeStruct((LRU_NGROUPS, LRU_GROUP, LRU_GROUP), F32),
                   jax.ShapeDtypeStruct((LRU_NGROUPS, LRU_GROUP, LRU_GROUP), F32)] + ex_shape,
        scratch_shapes=[pltpu.VMEM((1, W), F32), pltpu.VMEM((1, W), F32), pltpu.VMEM((8, W), F32)] + ex_sems,
        compiler_params=_params(("arbitrary",)),
    )(proj_lru, proj_lru, xl, h, h, dylru, conv_w, wa, wx, ba, bx, lam, *(exchange["arrays"] if exchange else []))
    return res[:8], res[8:]


def _mem_scores(q_h, k_h):
    s = _dot(q_h, k_h, NT) * (MEM_HEAD_DIM ** -0.5)
    s = s - jnp.max(s, axis=-1, keepdims=True)
    e = jnp.exp(s)
    return e / jnp.sum(e, axis=-1, keepdims=True)


def _mem_fwd(q, kv, rows=512):
    t = q.shape[0]
    rows = min(rows, t)
    m = kv.shape[0]

    def body(q_ref, kv_ref, y_ref):
        for hd in range(MEM_HEADS):
            cols = slice(hd * MEM_HEAD_DIM, (hd + 1) * MEM_HEAD_DIM)
            vcols = slice(D_MODEL + hd * MEM_HEAD_DIM, D_MODEL + (hd + 1) * MEM_HEAD_DIM)
            p = _mem_scores(q_ref[:, cols], kv_ref[:, cols])
            y_ref[:, cols] = _dot(p, kv_ref[:, vcols], NN).astype(y_ref.dtype)

    return pl.pallas_call(
        body, name="mem_fwd", grid=(t // rows,),
        in_specs=[pl.BlockSpec((rows, D_MODEL), lambda i: (i, 0)), pl.BlockSpec((m, 2 * D_MODEL), lambda i: (0, 0))],
        out_specs=pl.BlockSpec((rows, D_MODEL), lambda i: (i, 0)),
        out_shape=jax.ShapeDtypeStruct((t, D_MODEL), MXU_DTYPE),
        compiler_params=_params(("parallel",)),
    )(q, kv)


def _mem_bwd(q, kv, dy, rows=512):
    t = q.shape[0]
    rows = min(rows, t)
    m = kv.shape[0]

    def body(q_ref, kv_ref, dy_ref, dq_ref, dkv_ref):
        @pl.when(pl.program_id(0) == 0)
        def _():
            dkv_ref[...] = jnp.zeros_like(dkv_ref)

        for hd in range(MEM_HEADS):
            cols = slice(hd * MEM_HEAD_DIM, (hd + 1) * MEM_HEAD_DIM)
            vcols = slice(D_MODEL + hd * MEM_HEAD_DIM, D_MODEL + (hd + 1) * MEM_HEAD_DIM)
            q_h, k_h, dy_h = q_ref[:, cols], kv_ref[:, cols], dy_ref[:, cols]
            p = _mem_scores(q_h, k_h)
            dp = _dot(dy_h, kv_ref[:, vcols], NT)
            dkv_ref[:, vcols] += _dot(p, dy_h, TN)
            ds = p * (dp - jnp.sum(dp * p, axis=-1, keepdims=True)) * (MEM_HEAD_DIM ** -0.5)
            dq_ref[:, cols] = _dot(ds, k_h, NN).astype(dq_ref.dtype)
            dkv_ref[:, cols] += _dot(ds, q_h, TN)

    return pl.pallas_call(
        body, name="mem_bwd", grid=(t // rows,),
        in_specs=[pl.BlockSpec((rows, D_MODEL), lambda i: (i, 0)), pl.BlockSpec((m, 2 * D_MODEL), lambda i: (0, 0)),
                  pl.BlockSpec((rows, D_MODEL), lambda i: (i, 0))],
        out_specs=[pl.BlockSpec((rows, D_MODEL), lambda i: (i, 0)), pl.BlockSpec((m, 2 * D_MODEL), lambda i: (0, 0))],
        out_shape=[jax.ShapeDtypeStruct((t, D_MODEL), MXU_DTYPE), jax.ShapeDtypeStruct((m, 2 * D_MODEL), F32)],
        compiler_params=_params(("arbitrary",)),
    )(q, kv, dy)


def _merge_fwd(x, yssd, ylru, ymem, gl, w_bs, w_bl, w_bm, w_out, fg, tgt, rows=256):
    t = x.shape[0]
    rows = min(rows, t)
    D = D_MODEL

    def body(x_ref, ys_ref, yl_ref, ym_ref, gl_ref, wbs_ref, wbl_ref, wbm_ref, wo_ref, fg_ref, tgt_ref,
             ps_ref, pl_ref, pm_ref, mg_ref, dx2_ref, loss_ref, gfg_ref):
        @pl.when(pl.program_id(0) == 0)
        def _():
            loss_ref[...] = jnp.zeros_like(loss_ref)
            gfg_ref[...] = jnp.zeros_like(gfg_ref)

        ps = _dot(ys_ref[...], wbs_ref[...], NN)
        pl_ = _dot(yl_ref[...], wbl_ref[...], NN)
        pm = _dot(ym_ref[...], wbm_ref[...], NN)
        ps_ref[...] = ps
        pl_ref[...] = pl_
        pm_ref[...] = pm
        merged = (_sigmoid(gl_ref[:, 0:D]) * ps + _sigmoid(gl_ref[:, D:2 * D]) * pl_) + _sigmoid(gl_ref[:, 2 * D:3 * D]) * pm
        mg_ref[...] = merged.astype(mg_ref.dtype)
        x2 = x_ref[...] + _dot(merged, wo_ref[...], NN)
        r2 = lax.rsqrt(jnp.mean(x2 * x2, axis=-1, keepdims=True) + EPS)
        xn = x2 * r2
        fg = fg_ref[...]
        diff = xn * fg - tgt_ref[...]
        tile_loss = 0.5 * jnp.sum(jnp.mean(diff * diff, axis=-1, keepdims=True), axis=0, keepdims=True)
        loss_ref[...] += jnp.broadcast_to(tile_loss, loss_ref.shape)
        d_out = diff * (1.0 / D)
        gfg_ref[...] += jnp.sum(d_out * xn, axis=0, keepdims=True)
        dxn = d_out * fg
        dx2_ref[...] = r2 * (dxn - xn * jnp.mean(dxn * xn, axis=-1, keepdims=True))

    row = lambda w: pl.BlockSpec((rows, w), lambda i: (i, 0))
    const = lambda shape: pl.BlockSpec(shape, lambda i: (0,) * len(shape))
    return pl.pallas_call(
        body, name="merge_fwd", grid=(t // rows,),
        in_specs=[row(D), row(SSD_WIDTH), row(LRU_WIDTH), row(D), row(3 * D), const((SSD_WIDTH, D)), const((LRU_WIDTH, D)),
                  const((D, D)), const((D, D)), const((1, D)), row(D)],
        out_specs=[row(D), row(D), row(D), row(D), row(D), const((1, LANES)), const((1, D))],
        out_shape=[jax.ShapeDtypeStruct((t, D), F32), jax.ShapeDtypeStruct((t, D), F32), jax.ShapeDtypeStruct((t, D), F32),
                   jax.ShapeDtypeStruct((t, D), MXU_DTYPE), jax.ShapeDtypeStruct((t, D), F32),
                   jax.ShapeDtypeStruct((1, LANES), F32), jax.ShapeDtypeStruct((1, D), F32)],
        compiler_params=_params(("arbitrary",)),
    )(x, yssd, ylru, ymem, gl, w_bs, w_bl, w_bm, w_out, fg, tgt)


def _merge_bwd(dx2, gl, ps, pl_in, pm, w_bs, w_bl, w_bm, w_out, rows=256):
    t = dx2.shape[0]
    rows = min(rows, t)
    D = D_MODEL

    def body(dx2_ref, gl_ref, ps_ref, pl_ref, pm_ref, wbs_ref, wbl_ref, wbm_ref, wo_ref,
             dg_ref, dps_ref, dpl_ref, dpm_ref, dys_ref, dyl_ref, dym_ref):
        dm = _dot(dx2_ref[...], wo_ref[...], NT)
        for idx, (p_ref, dp_ref, w_ref, dy_ref) in enumerate(
                ((ps_ref, dps_ref, wbs_ref, dys_ref), (pl_ref, dpl_ref, wbl_ref, dyl_ref), (pm_ref, dpm_ref, wbm_ref, dym_ref))):
            gate = _sigmoid(gl_ref[:, idx * D:(idx + 1) * D])
            dg_ref[:, idx * D:(idx + 1) * D] = ((dm * p_ref[...]) * gate * (1.0 - gate)).astype(dg_ref.dtype)
            dp = dm * gate
            dp_ref[...] = dp.astype(dp_ref.dtype)
            dy_ref[...] = _dot(dp, w_ref[...], NT)

    row = lambda w: pl.BlockSpec((rows, w), lambda i: (i, 0))
    const = lambda shape: pl.BlockSpec(shape, lambda i: (0,) * len(shape))
    return pl.pallas_call(
        body, name="merge_bwd", grid=(t // rows,),
        in_specs=[row(D), row(3 * D), row(D), row(D), row(D), const((SSD_WIDTH, D)), const((LRU_WIDTH, D)),
                  const((D, D)), const((D, D))],
        out_specs=[row(3 * D), row(D), row(D), row(D), row(SSD_WIDTH), row(LRU_WIDTH), row(D)],
        out_shape=[jax.ShapeDtypeStruct((t, 3 * D), MXU_DTYPE), jax.ShapeDtypeStruct((t, D), MXU_DTYPE),
                   jax.ShapeDtypeStruct((t, D), MXU_DTYPE), jax.ShapeDtypeStruct((t, D), MXU_DTYPE),
                   jax.ShapeDtypeStruct((t, SSD_WIDTH), F32), jax.ShapeDtypeStruct((t, LRU_WIDTH), F32),
                   jax.ShapeDtypeStruct((t, D), F32)],
        compiler_params=_params(("parallel",)),
    )(dx2, gl, ps, pl_in, pm, w_bs, w_bl, w_bm, w_out)


def _mesh_place():
    x, y, c = lax.axis_index("x"), lax.axis_index("y"), lax.axis_index("c")
    return x, y, c, 4 * x + 2 * y + c


def _other_chips(x, y):
    return [(1 - x, y), (x, 1 - y), (1 - x, 1 - y)]


def _all_gather_plan(arrs):
    n = len(arrs)

    def parts(ins, outs, send_sems, recv_sems, local_sems):
        x, y, c, me = _mesh_place()
        sibling = (x, y, 1 - c)
        chips = _other_chips(x, y)

        def slot(px, py, pc):
            return 4 * px + 2 * py + pc

        def copy(a, k, block, to, src=None):
            return pltpu.make_async_remote_copy(
                src_ref=outs[a].at[block] if src is None else src, dst_ref=outs[a].at[block],
                send_sem=send_sems.at[a, k], recv_sem=recv_sems.at[a, k], device_id=to, device_id_type=pl.DeviceIdType.MESH)

        def local():
            return [pltpu.make_async_copy(ins[a], outs[a].at[me], local_sems.at[a]) for a in range(n)]

        def first():
            return [copy(a, k, me, to, src=ins[a]) for a in range(n)
                    for k, to in enumerate([sibling] + [(*chip, c) for chip in chips])]

        return x, y, c, sibling, chips, slot, copy, local, first

    def start(ins, outs, *sems):
        *_, local, first = parts(ins, outs, *sems)
        for cp in local() + first():
            cp.start()

    def wait(ins, outs, *sems):
        x, y, c, sibling, chips, slot, copy, local, first = parts(ins, outs, *sems)
        sends = first()
        for j, chip in enumerate(chips):
            for a in range(n):
                copy(a, 1 + j, slot(*chip, c), sibling).wait_recv()
                passed = copy(a, 4 + j, slot(*chip, c), sibling)
                passed.start()
                sends.append(passed)
        for a in range(n):
            copy(a, 0, slot(x, y, 1 - c), sibling).wait_recv()
        for j, chip in enumerate(chips):
            for a in range(n):
                copy(a, 4 + j, slot(*chip, 1 - c), sibling).wait_recv()
        for cp in sends:
            cp.wait_send()
        for cp in local():
            cp.wait()

    return dict(arrays=list(arrs), out_shape=[jax.ShapeDtypeStruct((N_DEV,) + a.shape, a.dtype) for a in arrs],
                sems=[(n, 7), (n, 7), (n,)], start=start, wait=wait)


N_CHIPS = 4


def _pair_plan(parts):
    n = len(parts)

    def copies(ins, outs, send_sems, recv_sems):
        x, y, c, _ = _mesh_place()
        return [pltpu.make_async_remote_copy(src_ref=ins[a].at[q, 1 - c], dst_ref=outs[a].at[q], send_sem=send_sems.at[a, q],
                                             recv_sem=recv_sems.at[a, q], device_id=(x, y, 1 - c), device_id_type=pl.DeviceIdType.MESH)
                for a in range(n) for q in range(N_CHIPS)]

    def start(ins, outs, send_sems, recv_sems):
        for cp in copies(ins, outs, send_sems, recv_sems):
            cp.start()

    def wait(ins, outs, send_sems, recv_sems):
        cps = copies(ins, outs, send_sems, recv_sems)
        for cp in cps:
            cp.wait_recv()
        for cp in cps:
            cp.wait_send()

    return dict(arrays=list(parts), out_shape=[jax.ShapeDtypeStruct((N_CHIPS,) + a.shape[2:], a.dtype) for a in parts],
                sems=[(n, N_CHIPS), (n, N_CHIPS)], start=start, wait=wait)


def _chip_plan(sums):
    n = len(sums)

    def copies(ins, outs, send_sems, recv_sems, arriving):
        x, y, c, _ = _mesh_place()
        my_chip = 2 * x + y
        cps = []
        for a in range(n):
            for j, (px, py) in enumerate(_other_chips(x, y)):
                src, dst = (my_chip, 2 * px + py) if arriving else (2 * px + py, my_chip)
                cps.append(pltpu.make_async_remote_copy(
                    src_ref=ins[a].at[src], dst_ref=outs[a].at[dst], send_sem=send_sems.at[a, j], recv_sem=recv_sems.at[a, j],
                    device_id=(px, py, c), device_id_type=pl.DeviceIdType.MESH))
        return cps

    def start(ins, outs, send_sems, recv_sems):
        for cp in copies(ins, outs, send_sems, recv_sems, False):
            cp.start()

    def wait(ins, outs, send_sems, recv_sems):
        for cp in copies(ins, outs, send_sems, recv_sems, True):
            cp.wait_recv()
        for cp in copies(ins, outs, send_sems, recv_sems, False):
            cp.wait_send()

    return dict(arrays=list(sums), out_shape=[jax.ShapeDtypeStruct(a.shape, a.dtype) for a in sums],
                sems=[(n, 3), (n, 3)], start=start, wait=wait)


def _both(p1, p2):
    n1, s1 = len(p1["arrays"]), len(p1["sems"])

    def each(method):
        def run(ins, outs, *sems):
            p1[method](ins[:n1], outs[:n1], *sems[:s1])
            p2[method](ins[n1:], outs[n1:], *sems[s1:])
        return run

    return dict(arrays=p1["arrays"] + p2["arrays"], out_shape=p1["out_shape"] + p2["out_shape"],
                sems=p1["sems"] + p2["sems"], start=each("start"), wait=each("wait"))


def _run_exchange(plan, name):
    n = len(plan["arrays"])

    def body(*refs):
        ins, outs, sems = refs[:n], refs[n:2 * n], refs[2 * n:]
        plan["start"](ins, outs, *sems)
        plan["wait"](ins, outs, *sems)

    any_spec = pl.BlockSpec(memory_space=pl.ANY)
    return pl.pallas_call(
        body, name=name, in_specs=[any_spec] * n, out_specs=[any_spec] * n, out_shape=plan["out_shape"],
        scratch_shapes=[pltpu.SemaphoreType.DMA(shape) for shape in plan["sems"]],
    )(*plan["arrays"])


def _riding(plan, body, n_in, n_out, first, last):
    if plan is None:
        return body, [], [], [], []
    ne = len(plan["arrays"])

    def wrapped(*refs):
        ins, ex_in = refs[:n_in], refs[n_in:n_in + ne]
        outs = refs[n_in + ne:n_in + ne + n_out]
        ex_out = refs[n_in + ne + n_out:n_in + 2 * ne + n_out]
        n_sems = len(plan["sems"])
        scratch, sems = refs[n_in + 2 * ne + n_out:-n_sems], refs[-n_sems:]

        @pl.when(first())
        def _():
            plan["start"](ex_in, ex_out, *sems)

        body(*ins, *outs, *scratch)

        @pl.when(last())
        def _():
            plan["wait"](ex_in, ex_out, *sems)

    any_spec = pl.BlockSpec(memory_space=pl.ANY)
    sems = [pltpu.SemaphoreType.DMA(shape) for shape in plan["sems"]]
    return wrapped, [any_spec] * ne, [any_spec] * ne, plan["out_shape"], sems


def _col_tile(r, c, limit_bytes):
    assert c % LANES == 0, c
    best = LANES
    for cand in range(LANES, c + 1, LANES):
        if c % cand == 0 and r * cand * 4 <= limit_bytes:
            best = cand
    return best


def _chip_sum(part, recv, core, name):
    _, _, r, c = part.shape
    ct = _col_tile(r, c, 2 << 20)

    def body(core_ref, p_ref, r_ref, s_ref, t_ref):
        s = p_ref[...] + r_ref[...]
        s_ref[...] = s
        t_ref[...] = s.astype(t_ref.dtype)

    blk = pl.BlockSpec((None, r, ct), lambda q, i, core_ref: (q, 0, i))
    return pl.pallas_call(
        body, name=name,
        grid_spec=pltpu.PrefetchScalarGridSpec(
            num_scalar_prefetch=1, grid=(N_CHIPS, c // ct),
            in_specs=[pl.BlockSpec((None, None, r, ct), lambda q, i, core_ref: (q, core_ref[0], 0, i)), blk],
            out_specs=[blk, blk]),
        out_shape=[jax.ShapeDtypeStruct((N_CHIPS, r, c), F32), jax.ShapeDtypeStruct((N_CHIPS, r, c), GRAD_WIRE_DTYPE)],
        compiler_params=_params(("parallel", "parallel")),
    )(core, part, recv)


def _adam_update(w, g, m, v):
    nm = ADAM_B1 * m + (1.0 - ADAM_B1) * g
    nv = ADAM_B2 * v + (1.0 - ADAM_B2) * (g * g)
    m_hat = nm / (1.0 - ADAM_B1 ** ADAM_STEP)
    v_hat = nv / (1.0 - ADAM_B2 ** ADAM_STEP)
    return -ADAM_LR * (m_hat / (jnp.sqrt(v_hat) + ADAM_EPS) + ADAM_WD * w), nm, nv


def _sum_adamw(own, recv, chip, w, m, v, name, exchange=None):
    _, r, c = own.shape
    ct = _col_tile(r, c, 1 << 20)

    def body(chip_ref, o_ref, r1_ref, r2_ref, r3_ref, w_ref, m_ref, v_ref, g_ref, d_ref, nm_ref, nv_ref):
        g = ((o_ref[...] + r1_ref[...].astype(F32)) + r2_ref[...].astype(F32)) + r3_ref[...].astype(F32)
        g_ref[...] = g
        d_ref[...], nm_ref[...], nv_ref[...] = _adam_update(w_ref[...], g, m_ref[...], v_ref[...])

    def slot(k):
        return pl.BlockSpec((None, r, ct), lambda i, chip_ref: ((chip_ref[0] + k) % N_CHIPS, 0, i))

    spec = pl.BlockSpec((r, ct), lambda i, chip_ref: (0, i))
    shape = jax.ShapeDtypeStruct((r, c), F32)
    n_tiles = c // ct
    body, ex_in, ex_out, ex_shape, ex_sems = _riding(
        exchange, body, 8, 4, lambda: pl.program_id(0) == 0, lambda: pl.program_id(0) == n_tiles - 1)
    res = pl.pallas_call(
        body, name=name,
        grid_spec=pltpu.PrefetchScalarGridSpec(
            num_scalar_prefetch=1, grid=(n_tiles,),
            in_specs=[slot(0), slot(1), slot(2), slot(3), spec, spec, spec] + ex_in, out_specs=[spec] * 4 + ex_out,
            scratch_shapes=ex_sems),
        out_shape=[shape] * 4 + ex_shape,
        compiler_params=_params(("arbitrary",)),
    )(chip, own, recv, recv, recv, w, m, v, *(exchange["arrays"] if exchange else []))
    return res[:4], res[4:]


def _small_adamw(parts, ws, ms, vs):
    n = len(parts)

    def body(*refs):
        p_refs, w_refs, m_refs, v_refs = refs[:n], refs[n:2 * n], refs[2 * n:3 * n], refs[3 * n:4 * n]
        outs = refs[4 * n:]
        for i in range(n):
            g = p_refs[i][0]
            for k in range(1, N_DEV):
                g = g + p_refs[i][k]
            outs[i][...] = g
            outs[n + i][...], outs[2 * n + i][...], outs[3 * n + i][...] = _adam_update(
                w_refs[i][...], g, m_refs[i][...], v_refs[i][...])

    vmem = pl.BlockSpec(memory_space=pltpu.VMEM)
    shapes = [jax.ShapeDtypeStruct(w.shape, F32) for w in ws]
    res = pl.pallas_call(
        body, name="adamw_small", in_specs=[vmem] * (4 * n), out_specs=[vmem] * (4 * n), out_shape=shapes * 4,
        compiler_params=pltpu.CompilerParams(vmem_limit_bytes=VMEM_LIMIT),
    )(*parts, *ws, *ms, *vs)
    return res[:n], res[n:2 * n], res[2 * n:3 * n], res[3 * n:]


def _pack(arrs, dtype, row_multiple):
    flat = jnp.concatenate([a.reshape(-1).astype(dtype) for a in arrs])
    unit = LANES * row_multiple
    padded = -(-flat.shape[0] // unit) * unit
    return jnp.pad(flat, (0, padded - flat.shape[0])).reshape(-1, LANES)


def _unpack(packed, shapes, lead=()):
    flat = packed.reshape(lead + (-1,))
    out, off = [], 0
    for shp in shapes:
        n = math.prod(shp)
        out.append(flat[..., off:off + n].reshape(lead + tuple(shp)))
        off += n
    return out


def _gather_cols(g, lo, hi):
    width = g.shape[2]
    pieces = []
    for s in range(N_DEV):
        a, e = max(lo, s * width), min(hi, (s + 1) * width)
        if a < e:
            pieces.append(g[s, :, a - s * width:e - s * width])
    return pieces[0] if len(pieces) == 1 else jnp.concatenate(pieces, axis=1)


def _scatter_cols(segs, width):
    slots = []
    for k in range(N_DEV):
        lo, hi = k * width, (k + 1) * width
        pieces = []
        for arr, s_lo, s_hi in segs:
            a, e = max(lo, s_lo), min(hi, s_hi)
            if a < e:
                pieces.append(arr[:, a - s_lo:e - s_lo])
        slots.append(pieces[0] if len(pieces) == 1 else jnp.concatenate(pieces, axis=1))
    return jnp.stack(slots)


def _block_diag_groups(w):
    w4 = w.reshape(LRU_NGROUPS, 4, LRU_BLOCK, LRU_BLOCK)
    eye = jnp.eye(4, dtype=w.dtype)
    return jnp.einsum("gaij,ab->gaibj", w4, eye).reshape(LRU_NGROUPS, LRU_GROUP, LRU_GROUP)


def _block_diag_extract(wg):
    w5 = wg.reshape(LRU_NGROUPS, 4, LRU_BLOCK, 4, LRU_BLOCK)
    idx = jnp.arange(4)
    return w5[:, idx, :, idx, :].transpose(1, 0, 2, 3).reshape(LRU_BLOCKS, LRU_BLOCK, LRU_BLOCK)


BIG = ("w_in", "w_kv", "w_br_ssd", "w_br_lru", "w_br_mem", "w_out")
SMALL_SHARDED = ("ssd_conv_w", "ssd_norm_g", "lru_conv_w")
REPLICATED = ("norm_g", "ssd_conv_b", "ssd_dt_bias", "ssd_a_log", "ssd_d", "lru_conv_b", "lru_w_a", "lru_b_a",
              "lru_w_x", "lru_b_x", "lru_lambda", "mem_norm_g", "final_g")
WEIGHTS = ("norm_g", "w_in", "ssd_conv_w", "ssd_conv_b", "ssd_dt_bias", "ssd_a_log", "ssd_d", "ssd_norm_g", "lru_conv_w",
           "lru_conv_b", "lru_w_a", "lru_b_a", "lru_w_x", "lru_b_x", "lru_lambda", "mem_norm_g", "w_kv", "w_br_ssd",
           "w_br_lru", "w_br_mem", "w_out", "final_g")


def kernel(x, mem, norm_g, w_in, ssd_conv_w, ssd_conv_b, ssd_dt_bias, ssd_a_log, ssd_d, ssd_norm_g, lru_conv_w, lru_conv_b, lru_w_a, lru_b_a, lru_w_x, lru_b_x, lru_lambda, mem_norm_g, w_kv, w_br_ssd, w_br_lru, w_br_mem, w_out, final_g, loss_target, m_norm_g, m_w_in, m_ssd_conv_w, m_ssd_conv_b, m_ssd_dt_bias, m_ssd_a_log, m_ssd_d, m_ssd_norm_g, m_lru_conv_w, m_lru_conv_b, m_lru_w_a, m_lru_b_a, m_lru_w_x, m_lru_b_x, m_lru_lambda, m_mem_norm_g, m_w_kv, m_w_br_ssd, m_w_br_lru, m_w_br_mem, m_w_out, m_final_g, v_norm_g, v_w_in, v_ssd_conv_w, v_ssd_conv_b, v_ssd_dt_bias, v_ssd_a_log, v_ssd_d, v_ssd_norm_g, v_lru_conv_w, v_lru_conv_b, v_lru_w_a, v_lru_b_a, v_lru_w_x, v_lru_b_x, v_lru_lambda, v_mem_norm_g, v_w_kv, v_w_br_ssd, v_w_br_lru, v_w_br_mem, v_w_out, v_final_g):
    env = dict(locals())
    W = {n: env[n] for n in WEIGHTS}
    M = {n: env["m_" + n] for n in WEIGHTS}
    V = {n: env["v_" + n] for n in WEIGHTS}
    me = 4 * lax.axis_index("x") + 2 * lax.axis_index("y") + lax.axis_index("c")
    t = x.shape[1]
    xt = x[0]
    memt = mem[0]
    tgt = loss_target[0]

    small_shapes = [W[n].shape for n in SMALL_SHARDED]
    as2d = lambda d, n: jnp.transpose(d[n][0]) if n == "w_in" else d[n][0]
    h, (g_in,) = _rms_fwd(xt, norm_g, "norm_fwd", exchange=_all_gather_plan([as2d(W, "w_in").astype(MXU_DTYPE)]))
    b = SEG_BOUNDS
    w_in_t = g_in.reshape(IN_WIDTH, D_MODEL)
    w_ssd, w_lru, w_q, w_g = w_in_t[b[0]:b[1]], w_in_t[b[2]:b[3]], w_in_t[b[3]:b[4]], w_in_t[b[4]:b[5]]
    w_dt = jnp.pad(w_in_t[b[1]:b[2]], ((0, DT_PAD - SSD_HEADS), (0, 0)))

    later = _all_gather_plan([as2d(W, n).astype(MXU_DTYPE) for n in BIG[1:]] + [_pack([W[n] for n in SMALL_SHARDED], F32, 8)])
    proj_ssd, (g_kv, g_bs, g_bl, g_bm, g_out, gs) = _matmul(h, w_ssd, "nt", "proj_ssd", tm=4096, tn=512, exchange=later)
    g_cw, g_ng, g_lcw = _unpack(gs, small_shapes, (N_DEV,))
    cols = lambda a: jnp.moveaxis(a[:, 0], 0, -2).reshape(a.shape[2:-1] + (-1,))
    rows_ = lambda a: a.reshape((-1,) + a.shape[2:])
    w_bs_f, w_bl_f, w_bm_f, w_out_f = rows_(g_bs), rows_(g_bl), rows_(g_bm), rows_(g_out)
    conv_w_f, ssd_ng_f, lru_cw_f = cols(g_cw), cols(g_ng), cols(g_lcw)
    w_kv_f = _gather_cols(g_kv, 0, 2 * D_MODEL)

    pad_heads = lambda a: jnp.pad(a, ((0, 0), (0, LANES - SSD_HEADS)))
    dtb, alog = pad_heads(ssd_dt_bias), pad_heads(ssd_a_log)
    d_row = jnp.repeat(ssd_d, SSD_HEAD_DIM, axis=1)
    ng_row = ssd_ng_f.reshape(1, SSD_WIDTH)
    wa_g, wx_g = _block_diag_groups(lru_w_a[0]), _block_diag_groups(lru_w_x[0])
    ba, bx = lru_b_a.reshape(1, LRU_WIDTH), lru_b_x.reshape(1, LRU_WIDTH)
    fg = final_g.reshape(1, D_MODEL)

    proj_lru = _matmul(h, w_lru, "nt", "proj_lru", tm=4096, tn=512)
    proj_q = _matmul(h, w_q, "nt", "proj_q", tm=4096, tn=512)
    proj_g = _matmul(h, w_g, "nt", "proj_g", tm=4096, tn=512)
    proj_dt = _matmul(h, w_dt, "nt", "proj_dt", tm=4096)
    mem_n = _rms_fwd(memt, mem_norm_g, "mem_norm_fwd")
    kv = _matmul(mem_n, w_kv_f, "nn", "mem_kv")
    yssd, y_scan, states, ssd_pre = _ssd_fwd(proj_ssd, proj_dt, conv_w_f, ssd_conv_b, dtb, alog, d_row, ng_row)
    ylru, h_lru, xl_lru = _lru_fwd(proj_lru, lru_cw_f, lru_conv_b, wa_g, wx_g, ba, bx, lru_lambda)
    ymem = _mem_fwd(proj_q, kv)
    ps, pl_, pm, merged, dx2, loss_vec, g_fg = _merge_fwd(xt, yssd, ylru, ymem, proj_g, w_bs_f, w_bl_f, w_bm_f, w_out_f, fg, tgt)

    d_g, dps, dpl, dpm, dyssd, dylru, dymem = _merge_bwd(dx2, proj_g, ps, pl_, pm, w_bs_f, w_bl_f, w_bm_f, w_out_f)
    gw_out = _matmul(merged, dx2, "tn", "grad_w_out", tk=2048)
    gw_bs = _matmul(yssd, dps, "tn", "grad_w_br_ssd", tm=2048)
    gw_bl = _matmul(ylru, dpl, "tn", "grad_w_br_lru", tm=LRU_WIDTH, tk=2048)
    gw_bm = _matmul(ymem, dpm, "tn", "grad_w_br_mem", tk=2048)
    d_q, d_kv = _mem_bwd(proj_q, kv, dymem)
    gw_kv = _matmul(mem_n, d_kv, "tn", "grad_w_kv")
    d_memn = _matmul(d_kv, w_kv_f, "nt", "d_mem_n")
    _, g_memng = _rms_bwd(memt, d_memn, None, mem_norm_g, "mem_norm_bwd")

    core = lax.axis_index("c").astype(jnp.int32).reshape(1)
    chip = (2 * lax.axis_index("x") + lax.axis_index("y")).astype(jnp.int32).reshape(1)
    by_chip = lambda a: a.reshape((N_CHIPS, 2, -1) + a.shape[1:])
    early = ("w_kv", "w_br_ssd", "w_br_lru", "w_br_mem", "w_out")
    early_parts = [by_chip(_scatter_cols([(gw_kv, 0, 2 * D_MODEL)], 2 * D_MODEL // N_DEV).reshape(-1, 2 * D_MODEL // N_DEV)),
                   by_chip(gw_bs), by_chip(gw_bl), by_chip(gw_bm), by_chip(gw_out)]
    (d_lru, gl_cw, gl_cb, g_ba, g_bx, g_lam, gwa_g, gwx_g), early_sib = _lru_bwd(
        proj_lru, xl_lru, h_lru, dylru, lru_cw_f, wa_g, wx_g, ba, bx, lru_lambda, exchange=_pair_plan(early_parts))
    early_sums = [_chip_sum(p, r, core, "chip_sum_" + n) for n, p, r in zip(early, early_parts, early_sib)]
    (d_ssd, d_dt, gs_cw, gs_cb, g_dtb, g_alog, g_dch, g_ngrow), early_recv = _ssd_bwd(
        proj_ssd, ssd_pre, proj_dt, y_scan, states, dyssd, conv_w_f, dtb, alog, d_row, ng_row,
        exchange=_chip_plan([s16 for _, s16 in early_sums]))
    gw_ssd = _matmul(d_ssd, h, "tn", "grad_w_in_ssd", tm=2560)
    gw_lru = _matmul(d_lru, h, "tn", "grad_w_in_lru", tm=1536, tk=2048)
    gw_q = _matmul(d_q, h, "tn", "grad_w_in_q", tk=2048)
    gw_g = _matmul(d_g, h, "tn", "grad_w_in_g", tm=1536, tk=2048)
    gw_dt = _matmul(d_dt, h, "tn", "grad_w_in_dt", tk=2048)
    in_part = by_chip(jnp.concatenate([gw_ssd, gw_dt[:SSD_HEADS], gw_lru, gw_q, gw_g], axis=0))
    (in_sib,) = _run_exchange(_pair_plan([in_part]), "grad_pair_exchange")
    in_sum = _chip_sum(in_part, in_sib, core, "chip_sum_w_in")

    small_grads = {
        "ssd_conv_w": gs_cw, "ssd_conv_b": gs_cb, "ssd_dt_bias": g_dtb[:, :SSD_HEADS],
        "ssd_a_log": g_alog[:, :SSD_HEADS], "ssd_d": jnp.sum(g_dch.reshape(SSD_HEADS, SSD_HEAD_DIM), axis=1).reshape(1, SSD_HEADS),
        "ssd_norm_g": g_ngrow.reshape(SSD_GROUPS, -1), "lru_conv_w": gl_cw, "lru_conv_b": gl_cb,
        "lru_w_a": _block_diag_extract(gwa_g), "lru_b_a": g_ba, "lru_w_x": _block_diag_extract(gwx_g), "lru_b_x": g_bx,
        "lru_lambda": g_lam, "mem_norm_g": g_memng, "final_g": g_fg,
    }
    small_all = REPLICATED + SMALL_SHARDED

    def small_shape(n, shards):
        shp = W[n].shape[1:] if W[n].ndim > 2 else (1, W[n].shape[-1])
        return shp[:-1] + (shp[-1] * shards,)

    riders = tuple(small_grads)
    small_plan = _all_gather_plan([small_grads[n].reshape(small_shape(n, N_DEV if n in SMALL_SHARDED else 1)) for n in riders])
    dh, landed = _dh([(d_ssd, w_ssd), (d_lru, w_lru), (d_q, w_q), (d_g, w_g), (d_dt, w_dt)],
                     exchange=_both(_chip_plan([in_sum[1]]), small_plan))
    grad_x, g_normg = _rms_bwd(xt, dh, dx2, norm_g, "norm_bwd")
    small_recv = dict(zip(riders, landed[1:]))
    reduced = {"w_in": (in_sum[0], landed[0]), **{n: (s[0], r) for n, s, r in zip(early, early_sums, early_recv)}}

    grads, delta, new_m, new_v = {}, {}, {}, {}
    for n in BIG:
        s32, recv = reduced[n]
        res, landed = _sum_adamw(s32, recv, chip, as2d(W, n), as2d(M, n), as2d(V, n), "adamw_" + n,
                                 exchange=_all_gather_plan([g_normg]) if n == "w_in" else None)
        if n == "w_in":
            small_recv["norm_g"] = landed[0]
        for dst, a in zip((grads, delta, new_m, new_v), res):
            dst[n] = (jnp.transpose(a) if n == "w_in" else a)[None]

    parts = []
    for n in small_all:
        a = small_recv[n]
        if n in SMALL_SHARDED:
            width = W[n].shape[-1]
            a = lax.dynamic_slice_in_dim(a, me * width, width, axis=a.ndim - 1)
        parts.append(a)
    canon = lambda d: [d[n].reshape(small_shape(n, 1)) for n in small_all]
    for dst, res in zip((grads, delta, new_m, new_v), _small_adamw(parts, canon(W), canon(M), canon(V))):
        for n, a in zip(small_all, res):
            dst[n] = a.reshape(W[n].shape)

    loss = lax.psum(loss_vec[0, 0], ("x", "y", "c"))
    return (loss, grad_x[None], *[grads[n] for n in WEIGHTS], *[delta[n] for n in WEIGHTS],
            *[new_m[n] for n in WEIGHTS], *[new_v[n] for n in WEIGHTS])
```

```python
import functools
import math

import jax
import jax.numpy as jnp
from jax import lax
from jax.experimental import pallas as pl
from jax.experimental.pallas import tpu as pltpu

F32 = jnp.float32
MXU_DTYPE = jnp.bfloat16
GRAD_WIRE_DTYPE = jnp.bfloat16

D_MODEL = 1024
EPS = 1e-6
CONV_WIDTH = 4
SSD_WIDTH = 2048
SSD_HEAD_DIM = 64
SSD_HEADS = 32
SSD_GROUPS = 4
SSD_STATE = 128
SSD_CHUNK = 128
SSD_BC = SSD_GROUPS * SSD_STATE
SSD_CONV_CH = SSD_WIDTH + 2 * SSD_BC
SSD_PAIRS = SSD_HEADS // 2
PAIRS_PER_GROUP = SSD_PAIRS // SSD_GROUPS
GROUP_COLS = SSD_WIDTH // SSD_GROUPS
LRU_WIDTH = 1536
LRU_BLOCKS = 16
LRU_BLOCK = 96
LRU_GROUP = 4 * LRU_BLOCK
LRU_NGROUPS = LRU_WIDTH // LRU_GROUP
LRU_C = 8.0
LRU_ROWS = 256
LRU_GATE_TAIL = -8.0
MEM_HEADS = 4
MEM_HEAD_DIM = 256
IN_WIDTH = 12320
N_DEV = 8
LANES = 128
SSD_SEG = SSD_WIDTH + SSD_CONV_CH
DT_PAD = LANES
SEG_BOUNDS = (0, 5120, 5152, 8224, 9248, 12320)

ADAM_LR = 0.001
ADAM_B1 = 0.9
ADAM_B2 = 0.999
ADAM_EPS = 1e-08
ADAM_WD = 0.01
ADAM_STEP = 10

VMEM_LIMIT = 56 * 1024 * 1024

NN = (((1,), (0,)), ((), ()))
NT = (((1,), (1,)), ((), ()))
TN = (((0,), (0,)), ((), ()))


def _dot(a, b, dims):
    return lax.dot_general(a.astype(MXU_DTYPE), b.astype(MXU_DTYPE), dims, preferred_element_type=F32)


def _sigmoid(x):
    return 0.5 * jnp.tanh(0.5 * x) + 0.5


def _log1p(e):
    u = 1.0 + e
    return jnp.where(u == 1.0, e, jnp.log(u) * (e / jnp.where(u == 1.0, 1.0, u - 1.0)))


def _softplus(x):
    return jnp.maximum(x, 0.0) + _log1p(jnp.exp(-jnp.abs(x)))


def _params(semantics):
    return pltpu.CompilerParams(dimension_semantics=semantics, vmem_limit_bytes=VMEM_LIMIT)


def _shift_down(cur, halo8, k):
    rolled = pltpu.roll(cur, k, 0)
    row8 = lax.broadcasted_iota(jnp.int32, halo8.shape, 0)
    top = jnp.where(row8 >= k, rolled[0:8], pltpu.roll(halo8, k, 0))
    return jnp.concatenate([top, rolled[8:]], axis=0)


def _shift_up(cur, next8, k):
    rows = cur.shape[0]
    rolled = pltpu.roll(cur, rows - k, 0)
    row8 = lax.broadcasted_iota(jnp.int32, next8.shape, 0)
    bot = jnp.where(row8 < 8 - k, rolled[rows - 8:rows], pltpu.roll(next8, 8 - k, 0))
    return jnp.concatenate([rolled[:rows - 8], bot], axis=0)


def _causal_conv(raw, halo8, w, b):
    acc = raw * w[3:4, :] + b
    for k in range(1, CONV_WIDTH):
        acc = acc + _shift_down(raw, halo8, k) * w[3 - k:4 - k, :]
    return acc


def _conv_backward(dco, next8, raw, w):
    d_raw = dco * w[3:4, :]
    gw = [None] * CONV_WIDTH
    gw[3] = jnp.sum(dco * raw, axis=0, keepdims=True)
    for j in range(1, CONV_WIDTH):
        up = _shift_up(dco, next8, j)
        d_raw = d_raw + up * w[3 - j:4 - j, :]
        gw[3 - j] = jnp.sum(up * raw, axis=0, keepdims=True)
    gb = jnp.sum(dco, axis=0, keepdims=True)
    return d_raw, gw, gb


def _cumsum_rows(v):
    rows = v.shape[0]
    row = lax.broadcasted_iota(jnp.int32, v.shape, 0)
    s = 1
    while s < rows:
        v = v + jnp.where(row >= s, pltpu.roll(v, s, 0), 0.0)
        s *= 2
    return v


def _rev_cumsum_rows(v):
    rows = v.shape[0]
    row = lax.broadcasted_iota(jnp.int32, v.shape, 0)
    s = 1
    while s < rows:
        v = v + jnp.where(row < rows - s, pltpu.roll(v, rows - s, 0), 0.0)
        s *= 2
    return v


def _matmul(a, b, mode, name, tm=1024, tn=1024, tk=1024, exchange=None):
    if mode == "nn":
        (m, kk), n = a.shape, b.shape[1]
    elif mode == "nt":
        (m, kk), n = a.shape, b.shape[0]
    else:
        (kk, m), n = a.shape, b.shape[1]
    tm, tn, tk = min(tm, m), min(tn, n), min(tk, kk)
    assert m % tm == 0 and n % tn == 0 and kk % tk == 0, (name, a.shape, b.shape)
    nk = kk // tk
    dims = {"nn": NN, "nt": NT, "tn": TN}[mode]
    a_spec = pl.BlockSpec((tk, tm), lambda i, j, k: (k, i)) if mode == "tn" else pl.BlockSpec((tm, tk), lambda i, j, k: (i, k))
    b_spec = pl.BlockSpec((tn, tk), lambda i, j, k: (j, k)) if mode == "nt" else pl.BlockSpec((tk, tn), lambda i, j, k: (k, j))
    o_spec = pl.BlockSpec((tm, tn), lambda i, j, k: (i, j))

    def body_single(a_ref, b_ref, o_ref):
        o_ref[...] = _dot(a_ref[...], b_ref[...], dims)

    def body(a_ref, b_ref, o_ref, acc_ref):
        k = pl.program_id(2)

        @pl.when(k == 0)
        def _():
            acc_ref[...] = jnp.zeros_like(acc_ref)

        acc_ref[...] += _dot(a_ref[...], b_ref[...], dims)

        @pl.when(k == nk - 1)
        def _():
            o_ref[...] = acc_ref[...]

    grid = (m // tm, n // tn, nk)
    if exchange is None:
        return pl.pallas_call(
            body_single if nk == 1 else body, name=name, grid=grid, in_specs=[a_spec, b_spec], out_specs=o_spec,
            out_shape=jax.ShapeDtypeStruct((m, n), F32),
            scratch_shapes=[] if nk == 1 else [pltpu.VMEM((tm, tn), F32)],
            compiler_params=_params(("parallel", "parallel", "arbitrary")),
        )(a, b)
    at = lambda ids: functools.reduce(lambda u, v: u & v, [pl.program_id(d) == ids[d] for d in range(3)])
    riding, ex_in, ex_out, ex_shape, ex_sems = _riding(
        exchange, body_single if nk == 1 else body, 2, 1, lambda: at((0, 0, 0)), lambda: at(tuple(g - 1 for g in grid)))
    res = pl.pallas_call(
        riding, name=name, grid=grid, in_specs=[a_spec, b_spec] + ex_in, out_specs=[o_spec] + ex_out,
        out_shape=[jax.ShapeDtypeStruct((m, n), F32)] + ex_shape,
        scratch_shapes=([] if nk == 1 else [pltpu.VMEM((tm, tn), F32)]) + ex_sems,
        compiler_params=_params(("arbitrary", "arbitrary", "arbitrary")),
    )(a, b, *exchange["arrays"])
    return res[0], res[1:]


def _rms_fwd(x, g, name, rows=512, exchange=None):
    t, d = x.shape
    rows = min(rows, t)
    n_tiles = t // rows

    def body(x_ref, g_ref, h_ref):
        xv = x_ref[...]
        r = lax.rsqrt(jnp.mean(xv * xv, axis=-1, keepdims=True) + EPS)
        h_ref[...] = ((xv * r) * g_ref[...]).astype(h_ref.dtype)

    body, ex_in, ex_out, ex_shape, ex_sems = _riding(
        exchange, body, 2, 1, lambda: pl.program_id(0) == 0, lambda: pl.program_id(0) == n_tiles - 1)
    res = pl.pallas_call(
        body, name=name, grid=(n_tiles,),
        in_specs=[pl.BlockSpec((rows, d), lambda i: (i, 0)), pl.BlockSpec((1, d), lambda i: (0, 0))] + ex_in,
        out_specs=[pl.BlockSpec((rows, d), lambda i: (i, 0))] + ex_out,
        out_shape=[jax.ShapeDtypeStruct((t, d), MXU_DTYPE)] + ex_shape,
        scratch_shapes=ex_sems,
        compiler_params=_params(("arbitrary",) if exchange else ("parallel",)),
    )(x, g, *(exchange["arrays"] if exchange else []))
    return (res[0], res[1:]) if exchange else res[0]


def _rms_bwd(x, dh, dres, g, name, rows=512):
    t, d = x.shape
    rows = min(rows, t)
    has_res = dres is not None

    def body(*refs):
        if has_res:
            x_ref, dh_ref, dr_ref, g_ref, dx_ref, gg_ref = refs
        else:
            x_ref, dh_ref, g_ref, dx_ref, gg_ref = refs

        @pl.when(pl.program_id(0) == 0)
        def _():
            gg_ref[...] = jnp.zeros_like(gg_ref)

        xv = x_ref[...]
        dhv = dh_ref[...]
        r = lax.rsqrt(jnp.mean(xv * xv, axis=-1, keepdims=True) + EPS)
        n = xv * r
        dn = dhv * g_ref[...]
        dx = r * (dn - n * jnp.mean(dn * n, axis=-1, keepdims=True))
        if has_res:
            dx = dx + dr_ref[...]
        dx_ref[...] = dx
        gg_ref[...] += jnp.sum(dhv * n, axis=0, keepdims=True)

    row_spec = pl.BlockSpec((rows, d), lambda i: (i, 0))
    vec_spec = pl.BlockSpec((1, d), lambda i: (0, 0))
    args = (x, dh) + ((dres,) if has_res else ()) + (g,)
    return pl.pallas_call(
        body, name=name, grid=(t // rows,),
        in_specs=[row_spec, row_spec] + ([row_spec] if has_res else []) + [vec_spec],
        out_specs=[row_spec, vec_spec],
        out_shape=[jax.ShapeDtypeStruct((t, d), F32), jax.ShapeDtypeStruct((1, d), F32)],
        compiler_params=_params(("arbitrary",)),
    )(*args)


def _dh(segs, half, rows=1024, tk=1024, exchange=None, into=None):
    t, d = segs[0][0].shape[0], segs[0][1].shape[1]
    rows = min(rows, t // 2)
    steps = []
    step0 = 0
    for a, _ in segs:
        kb = min(tk, a.shape[1])
        assert a.shape[1] % kb == 0, a.shape
        steps.append((step0, a.shape[1] // kb, kb))
        step0 += a.shape[1] // kb
    n_steps = step0
    ns = len(segs)
    n_tiles = t // rows // 2
    tile0 = half * n_tiles

    def body(*refs):
        a_refs, w_refs = refs[0:2 * ns:2], refs[1:2 * ns:2]
        dh_ref, acc_ref = refs[-2:]
        k = pl.program_id(1)

        @pl.when(k == 0)
        def _():
            acc_ref[...] = jnp.zeros_like(acc_ref)

        for s, (first, nblk, _) in enumerate(steps):
            @pl.when((k >= first) & (k < first + nblk))
            def _(s=s):
                acc_ref[...] += _dot(a_refs[s][...], w_refs[s][...], NN)

        @pl.when(k == n_steps - 1)
        def _():
            dh_ref[...] = acc_ref[...]

    in_specs, args = [], []
    for (a, w), (first, nblk, kb) in zip(segs, steps):
        blk = lambda k, first=first, nblk=nblk: jnp.clip(k - first, 0, nblk - 1)
        in_specs.append(pl.BlockSpec((rows, kb), lambda i, k, blk=blk: (i + tile0, blk(k))))
        in_specs.append(pl.BlockSpec((kb, d), lambda i, k, blk=blk: (blk(k), 0)))
        args += [a, w]
    row_spec = pl.BlockSpec((rows, d), lambda i, k: (i + tile0, 0))
    if into is not None:
        in_specs.append(pl.BlockSpec(memory_space=pl.ANY))
        args.append(into)
    body, ex_in, ex_out, ex_shape, ex_sems = _riding(
        exchange, body, len(args), 1,
        lambda: (pl.program_id(0) == 0) & (pl.program_id(1) == 0),
        lambda: (pl.program_id(0) == n_tiles - 1) & (pl.program_id(1) == n_steps - 1))
    res = pl.pallas_call(
        body, name=f"dh_{half}", grid=(n_tiles, n_steps),
        in_specs=in_specs + ex_in, out_specs=[row_spec] + ex_out,
        out_shape=[jax.ShapeDtypeStruct((t, d), F32)] + ex_shape,
        scratch_shapes=[pltpu.VMEM((rows, d), F32)] + ex_sems,
        input_output_aliases={} if into is None else {2 * ns: 0},
        compiler_params=_params(("arbitrary", "arbitrary")),
    )(*args, *(exchange["arrays"] if exchange else []))
    return res[0], res[1:]


def _pair_select(lo, m, h0):
    return jnp.where(lo, m[:, h0:h0 + 1], m[:, h0 + 1:h0 + 2])


def _group_select(lo, m, heads):
    return jnp.concatenate([_pair_select(lo, m, h0) for h0 in heads], axis=1)


def _halves(lo, v):
    return (jnp.sum(jnp.where(lo, v, 0.0), axis=1, keepdims=True),
            jnp.sum(jnp.where(lo, 0.0, v), axis=1, keepdims=True))


def _ssd_common(dt_raw, dtb, alog):
    dt = _softplus(dt_raw + dtb)
    aneg = -jnp.exp(alog)
    a_cs = _cumsum_rows(dt * aneg)
    return dt, aneg, a_cs, a_cs.T


def _ssd_specs(nc, rev):
    cidx = (lambda c: nc - 1 - c) if rev else (lambda c: c)
    L = SSD_CHUNK
    b_proj = 2 * SSD_WIDTH // SSD_BC
    b_conv = SSD_WIDTH // SSD_BC
    return dict(
        z=pl.BlockSpec((L, SSD_WIDTH), lambda c: (cidx(c), 0)),
        xr=pl.BlockSpec((L, SSD_WIDTH), lambda c: (cidx(c), 1)),
        br=pl.BlockSpec((L, SSD_BC), lambda c: (cidx(c), b_proj)),
        cr=pl.BlockSpec((L, SSD_BC), lambda c: (cidx(c), b_proj + 1)),
        dt=pl.BlockSpec((L, DT_PAD), lambda c: (cidx(c), 0)),
        cwx=pl.BlockSpec((CONV_WIDTH, SSD_WIDTH), lambda c: (0, 0)),
        cwb=pl.BlockSpec((CONV_WIDTH, SSD_BC), lambda c: (0, b_conv)),
        cwc=pl.BlockSpec((CONV_WIDTH, SSD_BC), lambda c: (0, b_conv + 1)),
        cbx=pl.BlockSpec((1, SSD_WIDTH), lambda c: (0, 0)),
        cbb=pl.BlockSpec((1, SSD_BC), lambda c: (0, b_conv)),
        cbc=pl.BlockSpec((1, SSD_BC), lambda c: (0, b_conv + 1)),
        vec128=pl.BlockSpec((1, LANES), lambda c: (0, 0)),
        vecw=pl.BlockSpec((1, SSD_WIDTH), lambda c: (0, 0)),
        wide=pl.BlockSpec((L, SSD_WIDTH), lambda c: (cidx(c), 0)),
        states=pl.BlockSpec((1, SSD_GROUPS, GROUP_COLS, SSD_STATE), lambda c: (cidx(c), 0, 0, 0)),
    )


def _ssd_fwd(proj_ssd, dt_p, conv_w, conv_b, dtb, alog, d_row, ng_row):
    t = proj_ssd.shape[0]
    nc = t // SSD_CHUNK
    L = SSD_CHUNK
    sp = _ssd_specs(nc, False)

    def body(z_ref, xr_ref, br_ref, cr_ref, dt_ref, cwx_ref, cwb_ref, cwc_ref, cbx_ref, cbb_ref, cbc_ref,
             dtb_ref, alog_ref, d_ref, ng_ref, yssd_ref, y_ref, st_ref, pre_ref,
             hx_ref, hb_ref, hc_ref, state_ref, yacc_ref):
        @pl.when(pl.program_id(0) == 0)
        def _():
            hx_ref[...] = jnp.zeros_like(hx_ref)
            hb_ref[...] = jnp.zeros_like(hb_ref)
            hc_ref[...] = jnp.zeros_like(hc_ref)
            state_ref[...] = jnp.zeros_like(state_ref)

        xr, br, cr = xr_ref[...], br_ref[...], cr_ref[...]
        px = _causal_conv(xr, hx_ref[...], cwx_ref[...], cbx_ref[...])
        pb = _causal_conv(br, hb_ref[...], cwb_ref[...], cbb_ref[...])
        pc = _causal_conv(cr, hc_ref[...], cwc_ref[...], cbc_ref[...])
        hx_ref[...] = xr[L - 8:L, :]
        hb_ref[...] = br[L - 8:L, :]
        hc_ref[...] = cr[L - 8:L, :]
        pre_ref[:, 0:SSD_WIDTH] = px
        pre_ref[:, SSD_WIDTH:SSD_WIDTH + SSD_BC] = pb
        pre_ref[:, SSD_WIDTH + SSD_BC:SSD_CONV_CH] = pc
        xs = px * _sigmoid(px)
        bm = pb * _sigmoid(pb)
        cm = pc * _sigmoid(pc)

        dt, _, a_cs, a_t = _ssd_common(dt_ref[...], dtb_ref[...], alog_ref[...])
        exp_a = jnp.exp(a_cs)
        a_last = a_cs[L - 1:L, :]
        dte = jnp.exp(a_last - a_cs)
        dec = jnp.exp(a_last)

        lane = lax.broadcasted_iota(jnp.int32, (L, LANES), 1)
        sub = lax.broadcasted_iota(jnp.int32, (L, LANES), 0)
        lo = lane < SSD_HEAD_DIM
        causal = sub >= lane
        top = sub < SSD_HEAD_DIM

        for g in range(SSD_GROUPS):
            b_g = bm[:, g * SSD_STATE:(g + 1) * SSD_STATE]
            c_g = cm[:, g * SSD_STATE:(g + 1) * SSD_STATE]
            cb = _dot(c_g, b_g, NT)
            heads = [2 * (g * PAIRS_PER_GROUP + jj) for jj in range(PAIRS_PER_GROUP)]
            gcols = slice(g * GROUP_COLS, (g + 1) * GROUP_COLS)
            xs_g = xs[:, gcols]
            xdt_g = xs_g * _group_select(lo, dt, heads)
            h_g = state_ref[g]
            st_ref[0, g] = h_g
            y_off_g = _dot(c_g, h_g, NT) * _group_select(lo, exp_a, heads)
            s_new_g = _dot(xdt_g * _group_select(lo, dte, heads), b_g, TN)
            for jj, h0 in enumerate(heads):
                blk = slice(jj * LANES, (jj + 1) * LANES)
                cols = slice(g * GROUP_COLS + jj * LANES, g * GROUP_COLS + (jj + 1) * LANES)
                xdt = xdt_g[:, blk]
                g0 = jnp.where(causal, jnp.exp(a_cs[:, h0:h0 + 1] - a_t[h0:h0 + 1, :]), 0.0) * cb
                g1 = jnp.where(causal, jnp.exp(a_cs[:, h0 + 1:h0 + 2] - a_t[h0 + 1:h0 + 2, :]), 0.0) * cb
                lhs = jnp.concatenate([g0, g1], axis=1)
                rhs = jnp.concatenate([jnp.where(lo, xdt, 0.0), jnp.where(lo, 0.0, xdt)], axis=0)
                y_diag = _dot(lhs, rhs, NN)
                dec_rows = jnp.where(top, dec[:, h0:h0 + 1], dec[:, h0 + 1:h0 + 2])
                state_ref[g, blk, :] = h_g[blk, :] * dec_rows + s_new_g[blk, :]
                yacc_ref[:, cols] = (y_diag + y_off_g[:, blk]) + xs_g[:, blk] * d_ref[:, cols]

        y = yacc_ref[...]
        y_ref[...] = y
        zz = z_ref[...]
        y2 = y * (zz * _sigmoid(zz))
        gw = SSD_WIDTH // SSD_GROUPS
        for g in range(SSD_GROUPS):
            seg = y2[:, g * gw:(g + 1) * gw]
            r = lax.rsqrt(jnp.mean(seg * seg, axis=-1, keepdims=True) + EPS)
            yssd_ref[:, g * gw:(g + 1) * gw] = ((seg * r) * ng_ref[:, g * gw:(g + 1) * gw]).astype(yssd_ref.dtype)

    return pl.pallas_call(
        body, name="ssd_fwd", grid=(nc,),
        in_specs=[sp["z"], sp["xr"], sp["br"], sp["cr"], sp["dt"], sp["cwx"], sp["cwb"], sp["cwc"],
                  sp["cbx"], sp["cbb"], sp["cbc"], sp["vec128"], sp["vec128"], sp["vecw"], sp["vecw"]],
        out_specs=[sp["wide"], sp["wide"], sp["states"], pl.BlockSpec((L, SSD_CONV_CH), lambda c: (c, 0))],
        out_shape=[jax.ShapeDtypeStruct((t, SSD_WIDTH), MXU_DTYPE), jax.ShapeDtypeStruct((t, SSD_WIDTH), F32),
                   jax.ShapeDtypeStruct((nc, SSD_GROUPS, GROUP_COLS, SSD_STATE), F32), jax.ShapeDtypeStruct((t, SSD_CONV_CH), F32)],
        scratch_shapes=[pltpu.VMEM((8, SSD_WIDTH), F32), pltpu.VMEM((8, SSD_BC), F32), pltpu.VMEM((8, SSD_BC), F32),
                        pltpu.VMEM((SSD_GROUPS, GROUP_COLS, SSD_STATE), F32), pltpu.VMEM((L, SSD_WIDTH), F32)],
        compiler_params=_params(("arbitrary",)),
    )(proj_ssd, proj_ssd, proj_ssd, proj_ssd, dt_p, conv_w, conv_w, conv_w, conv_b, conv_b, conv_b,
      dtb, alog, d_row, ng_row)


def _ssd_bwd(proj_ssd, pre, dt_p, y, states, dyssd, conv_w, dtb, alog, d_row, ng_row, exchange=None):
    t = proj_ssd.shape[0]
    nc = t // SSD_CHUNK
    L = SSD_CHUNK
    sp = _ssd_specs(nc, True)

    def pre_spec(width, col):
        return pl.BlockSpec((L, width), lambda c: (nc - 1 - c, col))

    def body(z_ref, xr_ref, br_ref, cr_ref, px_ref, pb_ref, pc_ref, dt_ref, y_ref, st_ref, dy_ref,
             cwx_ref, cwb_ref, cwc_ref, dtb_ref, alog_ref, d_ref, ng_ref,
             dssd_ref, ddt_ref, gcw_ref, gcb_ref, gdtb_ref, galog_ref, gd_ref, gng_ref,
             gn_ref, nx_ref, nb_ref, ncc_ref, dxs_ref):
        step = pl.program_id(0)

        @pl.when(step == 0)
        def _():
            gn_ref[...] = jnp.zeros_like(gn_ref)
            nx_ref[...] = jnp.zeros_like(nx_ref)
            nb_ref[...] = jnp.zeros_like(nb_ref)
            ncc_ref[...] = jnp.zeros_like(ncc_ref)
            for ref in (gcw_ref, gcb_ref, gdtb_ref, galog_ref, gd_ref, gng_ref):
                ref[...] = jnp.zeros_like(ref)

        xr, br, cr = xr_ref[...], br_ref[...], cr_ref[...]
        cwx, cwb, cwc = cwx_ref[...], cwb_ref[...], cwc_ref[...]
        px, pb, pc = px_ref[...], pb_ref[...], pc_ref[...]
        sx, sb, sc = _sigmoid(px), _sigmoid(pb), _sigmoid(pc)
        xs, bm, cm = px * sx, pb * sb, pc * sc

        dt_in = dt_ref[...] + dtb_ref[...]
        dt, aneg, a_cs, a_t = _ssd_common(dt_ref[...], dtb_ref[...], alog_ref[...])
        exp_a = jnp.exp(a_cs)
        a_last = a_cs[L - 1:L, :]
        dte = jnp.exp(a_last - a_cs)
        dec = jnp.exp(a_last)

        lane = lax.broadcasted_iota(jnp.int32, (L, LANES), 1)
        sub = lax.broadcasted_iota(jnp.int32, (L, LANES), 0)
        lo = lane < SSD_HEAD_DIM
        causal = sub >= lane
        top = sub < SSD_HEAD_DIM
        last_row = sub == L - 1

        yv = y_ref[...]
        zz = z_ref[...]
        sz = _sigmoid(zz)
        silz = zz * sz
        y2 = yv * silz
        dyv = dy_ref[...]
        gw = SSD_WIDTH // SSD_GROUPS
        d_y2_parts = []
        gng_parts = []
        for g in range(SSD_GROUPS):
            seg = y2[:, g * gw:(g + 1) * gw]
            dseg = dyv[:, g * gw:(g + 1) * gw]
            r = lax.rsqrt(jnp.mean(seg * seg, axis=-1, keepdims=True) + EPS)
            n = seg * r
            dn = dseg * ng_ref[:, g * gw:(g + 1) * gw]
            gng_parts.append(jnp.sum(dseg * n, axis=0, keepdims=True))
            d_y2_parts.append(r * (dn - n * jnp.mean(dn * n, axis=-1, keepdims=True)))
        d_y2 = jnp.concatenate(d_y2_parts, axis=1)
        gng_ref[...] += jnp.concatenate(gng_parts, axis=1)
        d_y = d_y2 * silz
        dssd_ref[:, 0:SSD_WIDTH] = (d_y2 * yv * (sz * (1.0 + zz * (1.0 - sz)))).astype(dssd_ref.dtype)
        gd_ref[...] += jnp.sum(d_y * xs, axis=0, keepdims=True)
        dxs_ref[...] = d_y * d_ref[...]

        d_a = jnp.zeros((L, LANES), F32)
        d_at = jnp.zeros((LANES, L), F32)
        ddt = jnp.zeros((L, LANES), F32)
        d_b_parts, d_c_parts = [], []
        for g in range(SSD_GROUPS):
            b_g = bm[:, g * SSD_STATE:(g + 1) * SSD_STATE]
            c_g = cm[:, g * SSD_STATE:(g + 1) * SSD_STATE]
            cb = _dot(c_g, b_g, NT)
            d_cb = jnp.zeros((L, L), F32)
            heads = [2 * (g * PAIRS_PER_GROUP + jj) for jj in range(PAIRS_PER_GROUP)]
            gcols = slice(g * GROUP_COLS, (g + 1) * GROUP_COLS)
            dy_g, xs_g = d_y[:, gcols], xs[:, gcols]
            dt_g = _group_select(lo, dt, heads)
            expa_g = _group_select(lo, exp_a, heads)
            dte_g = _group_select(lo, dte, heads)
            xdt_g = xs_g * dt_g
            h_g = st_ref[0, g]
            gn_g = gn_ref[g]
            dys_g = dy_g * expa_g
            d_cg = _dot(dys_g, h_g, NN)
            d_h_g = _dot(dys_g, c_g, TN)
            t1_g = dy_g * _dot(c_g, h_g, NT) * expa_g
            d_bg = _dot(xdt_g * dte_g, gn_g, NN)
            dxdt_g = _dot(b_g, gn_g, NT) * dte_g
            t2_g = dxdt_g * xdt_g
            t12_g = t1_g - t2_g
            gh_g = jnp.sum(gn_g * h_g, axis=1, keepdims=True)
            for jj, h0 in enumerate(heads):
                blk = slice(jj * LANES, (jj + 1) * LANES)
                cols = slice(g * GROUP_COLS + jj * LANES, g * GROUP_COLS + (jj + 1) * LANES)
                dy_p, xs_p, xdt, dt_pp = dy_g[:, blk], xs_g[:, blk], xdt_g[:, blk], dt_g[:, blk]
                l0 = jnp.where(causal, jnp.exp(a_cs[:, h0:h0 + 1] - a_t[h0:h0 + 1, :]), 0.0)
                l1 = jnp.where(causal, jnp.exp(a_cs[:, h0 + 1:h0 + 2] - a_t[h0 + 1:h0 + 2, :]), 0.0)
                g0, g1 = l0 * cb, l1 * cb
                dcat = jnp.concatenate([jnp.where(lo, dy_p, 0.0), jnp.where(lo, 0.0, dy_p)], axis=0)
                d_xdt = dxdt_g[:, blk] + _dot(jnp.concatenate([g0, g1], axis=0), dcat, TN)
                dm = _dot(dcat, xdt, NT)
                dm0, dm1 = dm[0:L], dm[L:2 * L]
                d_cb = d_cb + (l0 * dm0 + l1 * dm1)
                e0, e1 = dm0 * g0, dm1 * g1
                a0, a1 = _halves(lo, t12_g[:, blk])
                a0 = a0 + jnp.sum(e0, axis=1, keepdims=True)
                a1 = a1 + jnp.sum(e1, axis=1, keepdims=True)
                s0, s1 = _halves(lo, t2_g[:, blk])
                gh = gh_g[blk, :]
                dd0 = jnp.sum(jnp.where(top[:, 0:1], gh, 0.0), axis=0, keepdims=True)
                dd1 = jnp.sum(jnp.where(top[:, 0:1], 0.0, gh), axis=0, keepdims=True)
                end0 = jnp.sum(s0, axis=0, keepdims=True) + dd0 * dec[:, h0:h0 + 1]
                end1 = jnp.sum(s1, axis=0, keepdims=True) + dd1 * dec[:, h0 + 1:h0 + 2]
                d_a = d_a + jnp.where(lane == h0, a0 + jnp.where(last_row, end0, 0.0), 0.0)
                d_a = d_a + jnp.where(lane == h0 + 1, a1 + jnp.where(last_row, end1, 0.0), 0.0)
                d_at = d_at - jnp.where(sub == h0, jnp.sum(e0, axis=0, keepdims=True), 0.0)
                d_at = d_at - jnp.where(sub == h0 + 1, jnp.sum(e1, axis=0, keepdims=True), 0.0)
                dec_rows = jnp.where(top, dec[:, h0:h0 + 1], dec[:, h0 + 1:h0 + 2])
                gn_ref[g, blk, :] = d_h_g[blk, :] + dec_rows * gn_g[blk, :]
                q0, q1 = _halves(lo, d_xdt * xs_p)
                ddt = ddt + jnp.where(lane == h0, q0, 0.0) + jnp.where(lane == h0 + 1, q1, 0.0)
                dxs_ref[:, cols] += d_xdt * dt_pp
            d_cg = d_cg + _dot(d_cb, b_g, NN)
            d_bg = d_bg + _dot(d_cb, c_g, TN)
            d_b_parts.append(d_bg)
            d_c_parts.append(d_cg)

        rc = _rev_cumsum_rows(d_a + d_at.T)
        d_dt = rc * aneg + ddt
        galog_ref[...] += jnp.sum(rc * dt, axis=0, keepdims=True) * aneg
        d_dtraw = d_dt * _sigmoid(dt_in)
        gdtb_ref[...] += jnp.sum(d_dtraw, axis=0, keepdims=True)
        ddt_ref[...] = d_dtraw.astype(ddt_ref.dtype)

        def dsilu(p, s):
            return s * (1.0 + p * (1.0 - s))

        dcx = dxs_ref[...] * dsilu(px, sx)
        dcb = jnp.concatenate(d_b_parts, axis=1) * dsilu(pb, sb)
        dcc = jnp.concatenate(d_c_parts, axis=1) * dsilu(pc, sc)
        drx, gwx, gbx = _conv_backward(dcx, nx_ref[...], xr, cwx)
        drb, gwb, gbb = _conv_backward(dcb, nb_ref[...], br, cwb)
        drc, gwc, gbc = _conv_backward(dcc, ncc_ref[...], cr, cwc)
        nx_ref[...] = dcx[0:8, :]
        nb_ref[...] = dcb[0:8, :]
        ncc_ref[...] = dcc[0:8, :]
        dssd_ref[:, SSD_WIDTH:2 * SSD_WIDTH] = drx.astype(dssd_ref.dtype)
        dssd_ref[:, 2 * SSD_WIDTH:2 * SSD_WIDTH + SSD_BC] = drb.astype(dssd_ref.dtype)
        dssd_ref[:, 2 * SSD_WIDTH + SSD_BC:SSD_SEG] = drc.astype(dssd_ref.dtype)
        for k in range(CONV_WIDTH):
            gcw_ref[k:k + 1, :] += jnp.concatenate([gwx[k], gwb[k], gwc[k]], axis=1)
        gcb_ref[...] += jnp.concatenate([gbx, gbb, gbc], axis=1)

    const = lambda shape: pl.BlockSpec(shape, lambda c: (0,) * len(shape))
    body, ex_in, ex_out, ex_shape, ex_sems = _riding(
        exchange, body, 18, 8, lambda: pl.program_id(0) == 0, lambda: pl.program_id(0) == nc - 1)
    res = pl.pallas_call(
        body, name="ssd_bwd", grid=(nc,),
        in_specs=[sp["z"], sp["xr"], sp["br"], sp["cr"], pre_spec(SSD_WIDTH, 0), pre_spec(SSD_BC, SSD_WIDTH // SSD_BC), pre_spec(SSD_BC, SSD_WIDTH // SSD_BC + 1),
                  sp["dt"], sp["wide"], sp["states"], sp["wide"],
                  sp["cwx"], sp["cwb"], sp["cwc"], sp["vec128"], sp["vec128"], sp["vecw"], sp["vecw"]] + ex_in,
        out_specs=[pl.BlockSpec((L, SSD_SEG), lambda c: (nc - 1 - c, 0)), sp["dt"],
                   const((CONV_WIDTH, SSD_CONV_CH)), const((1, SSD_CONV_CH)), const((1, LANES)), const((1, LANES)),
                   const((1, SSD_WIDTH)), const((1, SSD_WIDTH))] + ex_out,
        out_shape=[jax.ShapeDtypeStruct((t, SSD_SEG), MXU_DTYPE), jax.ShapeDtypeStruct((t, DT_PAD), MXU_DTYPE),
                   jax.ShapeDtypeStruct((CONV_WIDTH, SSD_CONV_CH), F32), jax.ShapeDtypeStruct((1, SSD_CONV_CH), F32),
                   jax.ShapeDtypeStruct((1, LANES), F32), jax.ShapeDtypeStruct((1, LANES), F32),
                   jax.ShapeDtypeStruct((1, SSD_WIDTH), F32), jax.ShapeDtypeStruct((1, SSD_WIDTH), F32)] + ex_shape,
        scratch_shapes=[pltpu.VMEM((SSD_GROUPS, GROUP_COLS, SSD_STATE), F32), pltpu.VMEM((8, SSD_WIDTH), F32),
                        pltpu.VMEM((8, SSD_BC), F32), pltpu.VMEM((8, SSD_BC), F32), pltpu.VMEM((L, SSD_WIDTH), F32)] + ex_sems,
        compiler_params=_params(("arbitrary",)),
    )(proj_ssd, proj_ssd, proj_ssd, proj_ssd, pre, pre, pre, dt_p, y, states, dyssd,
      conv_w, conv_w, conv_w, dtb, alog, d_row, ng_row, *(exchange["arrays"] if exchange else []))
    return res[:8], res[8:]


def _lru_gates(xl, wa_ref, wx_ref, ba, bx, lam):
    pre_a, pre_x = [], []
    for g in range(LRU_NGROUPS):
        xg = xl[:, g * LRU_GROUP:(g + 1) * LRU_GROUP]
        pre_a.append(_dot(xg, wa_ref[g], NN))
        pre_x.append(_dot(xg, wx_ref[g], NN))
    pa = jnp.concatenate(pre_a, axis=1) + ba
    tail = jnp.exp(jnp.minimum(pa, LRU_GATE_TAIL))
    r = jnp.where(pa < LRU_GATE_TAIL, tail * (1.0 - tail), _sigmoid(pa))
    i = _sigmoid(jnp.concatenate(pre_x, axis=1) + bx)
    log_a = (-LRU_C * r) * _softplus(-lam)
    a = jnp.exp(log_a)
    mult_sq = -jnp.tanh(log_a) * (a * a + 1.0)
    return r, i, mult_sq, a, jnp.sqrt(mult_sq)


def _scan_rows(p, u, carry, reverse):
    rows, w = p.shape
    groups = rows // 8
    p3, u3 = p.reshape(groups, 8, w), u.reshape(groups, 8, w)
    row = lax.broadcasted_iota(jnp.int32, (groups, 8, w), 1)
    for s in (1, 2, 4):
        ok = row < 8 - s if reverse else row >= s
        shift = 8 - s if reverse else s
        u3 = p3 * jnp.where(ok, pltpu.roll(u3, shift, 1), 0.0) + u3
        p3 = p3 * jnp.where(ok, pltpu.roll(p3, shift, 1), 1.0)
    out = [None] * groups
    for k in (range(groups - 1, -1, -1) if reverse else range(groups)):
        out[k] = p3[k] * carry + u3[k]
        carry = out[k][0:1, :] if reverse else out[k][7:8, :]
    return jnp.concatenate(out, axis=0), carry


def _lru_fwd(proj_lru, conv_w, conv_b, wa, wx, ba, bx, lam):
    t = proj_lru.shape[0]
    rows = min(LRU_ROWS, t)
    nb = t // rows
    W = LRU_WIDTH

    def body(lg_ref, lx_ref, cw_ref, cb_ref, wa_ref, wx_ref, ba_ref, bx_ref, lam_ref, ylru_ref, h_ref, xl_ref,
             halo_ref, carry_ref):
        @pl.when(pl.program_id(0) == 0)
        def _():
            halo_ref[...] = jnp.zeros_like(halo_ref)
            carry_ref[...] = jnp.zeros_like(carry_ref)

        lx = lx_ref[...]
        xl = _causal_conv(lx, halo_ref[...], cw_ref[...], cb_ref[...])
        halo_ref[...] = lx[rows - 8:rows, :]
        xl_ref[...] = xl
        _, i, _, a, mult = _lru_gates(xl, wa_ref, wx_ref, ba_ref[...], bx_ref[...], lam_ref[...])
        u = mult * (i * xl)
        h, carry_ref[...] = _scan_rows(a, u, carry_ref[...], False)
        h_ref[...] = h
        lg = lg_ref[...]
        ylru_ref[...] = (h * (lg * _sigmoid(lg))).astype(ylru_ref.dtype)

    const = lambda shape: pl.BlockSpec(shape, lambda b: (0,) * len(shape))
    return pl.pallas_call(
        body, name="lru_fwd", grid=(nb,),
        in_specs=[pl.BlockSpec((rows, W), lambda b: (b, 0)), pl.BlockSpec((rows, W), lambda b: (b, 1)),
                  const((CONV_WIDTH, W)), const((1, W)), const((LRU_NGROUPS, LRU_GROUP, LRU_GROUP)),
                  const((LRU_NGROUPS, LRU_GROUP, LRU_GROUP)), const((1, W)), const((1, W)), const((1, W))],
        out_specs=[pl.BlockSpec((rows, W), lambda b: (b, 0))] * 3,
        out_shape=[jax.ShapeDtypeStruct((t, W), MXU_DTYPE), jax.ShapeDtypeStruct((t, W), F32), jax.ShapeDtypeStruct((t, W), F32)],
        scratch_shapes=[pltpu.VMEM((8, W), F32), pltpu.VMEM((1, W), F32)],
        compiler_params=_params(("arbitrary",)),
    )(proj_lru, proj_lru, conv_w, conv_b, wa, wx, ba, bx, lam)


def _lru_bwd(proj_lru, xl, h, dylru, conv_w, wa, wx, ba, bx, lam, exchange=None):
    t = proj_lru.shape[0]
    rows = min(LRU_ROWS, t)
    nb = t // rows
    W = LRU_WIDTH
    groups8 = rows // 8

    def rev(b):
        return nb - 1 - b

    def halo_spec(col):
        return pl.BlockSpec((8, W), lambda b: (jnp.maximum(rev(b) * groups8 - 1, 0), col))

    def body(lg_ref, lx_ref, xl_ref, h_ref, hh_ref, dy_ref, cw_ref, wa_ref, wx_ref, ba_ref, bx_ref, lam_ref,
             dlru_ref, gcw_ref, gcb_ref, gba_ref, gbx_ref, glam_ref, gwa_ref, gwx_ref,
             gcarry_ref, afirst_ref, nxt_ref):
        step = pl.program_id(0)

        @pl.when(step == 0)
        def _():
            gcarry_ref[...] = jnp.zeros_like(gcarry_ref)
            afirst_ref[...] = jnp.zeros_like(afirst_ref)
            nxt_ref[...] = jnp.zeros_like(nxt_ref)
            for ref in (gcw_ref, gcb_ref, gba_ref, gbx_ref, glam_ref, gwa_ref, gwx_ref):
                ref[...] = jnp.zeros_like(ref)

        keep = jnp.where(step == nb - 1, 0.0, 1.0)
        lx = lx_ref[...]
        cw = cw_ref[...]
        xl = xl_ref[...]
        lam = lam_ref[...]
        r, i, mult_sq, a, mult = _lru_gates(xl, wa_ref, wx_ref, ba_ref[...], bx_ref[...], lam)
        hv = h_ref[...]
        h_prev = _shift_down(hv, hh_ref[...] * keep, 1)
        lg = lg_ref[...]
        sg = _sigmoid(lg)
        dyv = dy_ref[...]
        d_h = dyv * (lg * sg)
        dlru_ref[:, 0:W] = (dyv * hv * (sg * (1.0 + lg * (1.0 - sg)))).astype(dlru_ref.dtype)

        row = lax.broadcasted_iota(jnp.int32, (rows, W), 0)
        p = jnp.where(row < rows - 1, pltpu.roll(a, rows - 1, 0), afirst_ref[...])
        gsc, gcarry_ref[...] = _scan_rows(p, d_h, gcarry_ref[...], True)
        afirst_ref[...] = a[0:1, :]

        d_a = gsc * h_prev
        v = i * xl
        d_mult = gsc * v
        d_v = gsc * mult
        d_i = d_v * xl
        d_xl = d_v * i
        d_la = d_a * a - d_mult * (a * a) * lax.rsqrt(mult_sq)
        sp_neg = _softplus(-lam)
        d_r = d_la * (-LRU_C * sp_neg)
        glam_ref[...] += jnp.sum(d_la * r, axis=0, keepdims=True) * (LRU_C * _sigmoid(-lam))
        d_pa = d_r * r * (1.0 - r)
        d_px = d_i * i * (1.0 - i)
        gba_ref[...] += jnp.sum(d_pa, axis=0, keepdims=True)
        gbx_ref[...] += jnp.sum(d_px, axis=0, keepdims=True)
        parts = []
        for g in range(LRU_NGROUPS):
            cols = slice(g * LRU_GROUP, (g + 1) * LRU_GROUP)
            xg, dpa_g, dpx_g = xl[:, cols], d_pa[:, cols], d_px[:, cols]
            parts.append(_dot(dpa_g, wa_ref[g], NT) + _dot(dpx_g, wx_ref[g], NT))
            gwa_ref[g] += _dot(xg, dpa_g, TN)
            gwx_ref[g] += _dot(xg, dpx_g, TN)
        d_xl = d_xl + jnp.concatenate(parts, axis=1)
        d_lx, gw, gb = _conv_backward(d_xl, nxt_ref[...], lx, cw)
        nxt_ref[...] = d_xl[0:8, :]
        dlru_ref[:, W:2 * W] = d_lx.astype(dlru_ref.dtype)
        for k in range(CONV_WIDTH):
            gcw_ref[k:k + 1, :] += gw[k]
        gcb_ref[...] += gb

    const = lambda shape: pl.BlockSpec(shape, lambda b: (0,) * len(shape))
    wspec = const((LRU_NGROUPS, LRU_GROUP, LRU_GROUP))
    blk = lambda col: pl.BlockSpec((rows, W), lambda b: (rev(b), col))
    body, ex_in, ex_out, ex_shape, ex_sems = _riding(
        exchange, body, 12, 8, lambda: pl.program_id(0) == 0, lambda: pl.program_id(0) == nb - 1)
    res = pl.pallas_call(
        body, name="lru_bwd", grid=(nb,),
        in_specs=[blk(0), blk(1), blk(0), blk(0), halo_spec(0), blk(0),
                  const((CONV_WIDTH, W)), wspec, wspec, const((1, W)), const((1, W)), const((1, W))] + ex_in,
        out_specs=[pl.BlockSpec((rows, 2 * W), lambda b: (rev(b), 0)), const((CONV_WIDTH, W)), const((1, W)),
                   const((1, W)), const((1, W)), const((1, W)), wspec, wspec] + ex_out,
        out_shape=[jax.ShapeDtypeStruct((t, 2 * W), MXU_DTYPE), jax.ShapeDtypeStruct((CONV_WIDTH, W), F32),
                   jax.ShapeDtypeStruct((1, W), F32), jax.ShapeDtypeStruct((1, W), F32), jax.ShapeDtypeStruct((1, W), F32),
                   jax.ShapeDtypeStruct((1, W), F32), jax.ShapeDtypeStruct((LRU_NGROUPS, LRU_GROUP, LRU_GROUP), F32),
                   jax.ShapeDtypeStruct((LRU_NGROUPS, LRU_GROUP, LRU_GROUP), F32)] + ex_shape,
        scratch_shapes=[pltpu.VMEM((1, W), F32), pltpu.VMEM((1, W), F32), pltpu.VMEM((8, W), F32)] + ex_sems,
        compiler_params=_params(("arbitrary",)),
    )(proj_lru, proj_lru, xl, h, h, dylru, conv_w, wa, wx, ba, bx, lam, *(exchange["arrays"] if exchange else []))
    return res[:8], res[8:]


def _mem_scores(q_h, k_h):
    s = _dot(q_h, k_h, NT) * (MEM_HEAD_DIM ** -0.5)
    s = s - jnp.max(s, axis=-1, keepdims=True)
    e = jnp.exp(s)
    return e / jnp.sum(e, axis=-1, keepdims=True)


def _mem_fwd(q, kv, rows=512):
    t = q.shape[0]
    rows = min(rows, t)
    m = kv.shape[0]

    def body(q_ref, kv_ref, y_ref):
        for hd in range(MEM_HEADS):
            cols = slice(hd * MEM_HEAD_DIM, (hd + 1) * MEM_HEAD_DIM)
            vcols = slice(D_MODEL + hd * MEM_HEAD_DIM, D_MODEL + (hd + 1) * MEM_HEAD_DIM)
            p = _mem_scores(q_ref[:, cols], kv_ref[:, cols])
            y_ref[:, cols] = _dot(p, kv_ref[:, vcols], NN).astype(y_ref.dtype)

    return pl.pallas_call(
        body, name="mem_fwd", grid=(t // rows,),
        in_specs=[pl.BlockSpec((rows, D_MODEL), lambda i: (i, 0)), pl.BlockSpec((m, 2 * D_MODEL), lambda i: (0, 0))],
        out_specs=pl.BlockSpec((rows, D_MODEL), lambda i: (i, 0)),
        out_shape=jax.ShapeDtypeStruct((t, D_MODEL), MXU_DTYPE),
        compiler_params=_params(("parallel",)),
    )(q, kv)


def _mem_bwd(q, kv, dy, rows=512):
    t = q.shape[0]
    rows = min(rows, t)
    m = kv.shape[0]

    def body(q_ref, kv_ref, dy_ref, dq_ref, dkv_ref):
        @pl.when(pl.program_id(0) == 0)
        def _():
            dkv_ref[...] = jnp.zeros_like(dkv_ref)

        for hd in range(MEM_HEADS):
            cols = slice(hd * MEM_HEAD_DIM, (hd + 1) * MEM_HEAD_DIM)
            vcols = slice(D_MODEL + hd * MEM_HEAD_DIM, D_MODEL + (hd + 1) * MEM_HEAD_DIM)
            q_h, k_h, dy_h = q_ref[:, cols], kv_ref[:, cols], dy_ref[:, cols]
            p = _mem_scores(q_h, k_h)
            dp = _dot(dy_h, kv_ref[:, vcols], NT)
            dkv_ref[:, vcols] += _dot(p, dy_h, TN)
            ds = p * (dp - jnp.sum(dp * p, axis=-1, keepdims=True)) * (MEM_HEAD_DIM ** -0.5)
            dq_ref[:, cols] = _dot(ds, k_h, NN).astype(dq_ref.dtype)
            dkv_ref[:, cols] += _dot(ds, q_h, TN)

    return pl.pallas_call(
        body, name="mem_bwd", grid=(t // rows,),
        in_specs=[pl.BlockSpec((rows, D_MODEL), lambda i: (i, 0)), pl.BlockSpec((m, 2 * D_MODEL), lambda i: (0, 0)),
                  pl.BlockSpec((rows, D_MODEL), lambda i: (i, 0))],
        out_specs=[pl.BlockSpec((rows, D_MODEL), lambda i: (i, 0)), pl.BlockSpec((m, 2 * D_MODEL), lambda i: (0, 0))],
        out_shape=[jax.ShapeDtypeStruct((t, D_MODEL), MXU_DTYPE), jax.ShapeDtypeStruct((m, 2 * D_MODEL), F32)],
        compiler_params=_params(("arbitrary",)),
    )(q, kv, dy)


def _merge_fwd(x, yssd, ylru, ymem, gl, w_bs, w_bl, w_bm, w_out, fg, tgt, rows=256):
    t = x.shape[0]
    rows = min(rows, t)
    D = D_MODEL

    def body(x_ref, ys_ref, yl_ref, ym_ref, gl_ref, wbs_ref, wbl_ref, wbm_ref, wo_ref, fg_ref, tgt_ref,
             ps_ref, pl_ref, pm_ref, mg_ref, dx2_ref, loss_ref, gfg_ref):
        @pl.when(pl.program_id(0) == 0)
        def _():
            loss_ref[...] = jnp.zeros_like(loss_ref)
            gfg_ref[...] = jnp.zeros_like(gfg_ref)

        ps = _dot(ys_ref[...], wbs_ref[...], NN)
        pl_ = _dot(yl_ref[...], wbl_ref[...], NN)
        pm = _dot(ym_ref[...], wbm_ref[...], NN)
        ps_ref[...] = ps
        pl_ref[...] = pl_
        pm_ref[...] = pm
        merged = (_sigmoid(gl_ref[:, 0:D]) * ps + _sigmoid(gl_ref[:, D:2 * D]) * pl_) + _sigmoid(gl_ref[:, 2 * D:3 * D]) * pm
        mg_ref[...] = merged.astype(mg_ref.dtype)
        x2 = x_ref[...] + _dot(merged, wo_ref[...], NN)
        r2 = lax.rsqrt(jnp.mean(x2 * x2, axis=-1, keepdims=True) + EPS)
        xn = x2 * r2
        fg = fg_ref[...]
        diff = xn * fg - tgt_ref[...]
        tile_loss = 0.5 * jnp.sum(jnp.mean(diff * diff, axis=-1, keepdims=True), axis=0, keepdims=True)
        loss_ref[...] += jnp.broadcast_to(tile_loss, loss_ref.shape)
        d_out = diff * (1.0 / D)
        gfg_ref[...] += jnp.sum(d_out * xn, axis=0, keepdims=True)
        dxn = d_out * fg
        dx2_ref[...] = r2 * (dxn - xn * jnp.mean(dxn * xn, axis=-1, keepdims=True))

    row = lambda w: pl.BlockSpec((rows, w), lambda i: (i, 0))
    const = lambda shape: pl.BlockSpec(shape, lambda i: (0,) * len(shape))
    return pl.pallas_call(
        body, name="merge_fwd", grid=(t // rows,),
        in_specs=[row(D), row(SSD_WIDTH), row(LRU_WIDTH), row(D), row(3 * D), const((SSD_WIDTH, D)), const((LRU_WIDTH, D)),
                  const((D, D)), const((D, D)), const((1, D)), row(D)],
        out_specs=[row(D), row(D), row(D), row(D), row(D), const((1, LANES)), const((1, D))],
        out_shape=[jax.ShapeDtypeStruct((t, D), F32), jax.ShapeDtypeStruct((t, D), F32), jax.ShapeDtypeStruct((t, D), F32),
                   jax.ShapeDtypeStruct((t, D), MXU_DTYPE), jax.ShapeDtypeStruct((t, D), F32),
                   jax.ShapeDtypeStruct((1, LANES), F32), jax.ShapeDtypeStruct((1, D), F32)],
        compiler_params=_params(("arbitrary",)),
    )(x, yssd, ylru, ymem, gl, w_bs, w_bl, w_bm, w_out, fg, tgt)


def _merge_bwd(dx2, gl, ps, pl_in, pm, w_bs, w_bl, w_bm, w_out, rows=256):
    t = dx2.shape[0]
    rows = min(rows, t)
    D = D_MODEL

    def body(dx2_ref, gl_ref, ps_ref, pl_ref, pm_ref, wbs_ref, wbl_ref, wbm_ref, wo_ref,
             dg_ref, dps_ref, dpl_ref, dpm_ref, dys_ref, dyl_ref, dym_ref):
        dm = _dot(dx2_ref[...], wo_ref[...], NT)
        for idx, (p_ref, dp_ref, w_ref, dy_ref) in enumerate(
                ((ps_ref, dps_ref, wbs_ref, dys_ref), (pl_ref, dpl_ref, wbl_ref, dyl_ref), (pm_ref, dpm_ref, wbm_ref, dym_ref))):
            gate = _sigmoid(gl_ref[:, idx * D:(idx + 1) * D])
            dg_ref[:, idx * D:(idx + 1) * D] = ((dm * p_ref[...]) * gate * (1.0 - gate)).astype(dg_ref.dtype)
            dp = dm * gate
            dp_ref[...] = dp.astype(dp_ref.dtype)
            dy_ref[...] = _dot(dp, w_ref[...], NT)

    row = lambda w: pl.BlockSpec((rows, w), lambda i: (i, 0))
    const = lambda shape: pl.BlockSpec(shape, lambda i: (0,) * len(shape))
    return pl.pallas_call(
        body, name="merge_bwd", grid=(t // rows,),
        in_specs=[row(D), row(3 * D), row(D), row(D), row(D), const((SSD_WIDTH, D)), const((LRU_WIDTH, D)),
                  const((D, D)), const((D, D))],
        out_specs=[row(3 * D), row(D), row(D), row(D), row(SSD_WIDTH), row(LRU_WIDTH), row(D)],
        out_shape=[jax.ShapeDtypeStruct((t, 3 * D), MXU_DTYPE), jax.ShapeDtypeStruct((t, D), MXU_DTYPE),
                   jax.ShapeDtypeStruct((t, D), MXU_DTYPE), jax.ShapeDtypeStruct((t, D), MXU_DTYPE),
                   jax.ShapeDtypeStruct((t, SSD_WIDTH), F32), jax.ShapeDtypeStruct((t, LRU_WIDTH), F32),
                   jax.ShapeDtypeStruct((t, D), F32)],
        compiler_params=_params(("parallel",)),
    )(dx2, gl, ps, pl_in, pm, w_bs, w_bl, w_bm, w_out)


def _mesh_place():
    x, y, c = lax.axis_index("x"), lax.axis_index("y"), lax.axis_index("c")
    return x, y, c, 4 * x + 2 * y + c


def _other_chips(x, y):
    return [(1 - x, y), (x, 1 - y), (1 - x, 1 - y)]


def _all_gather_plan(arrs):
    n = len(arrs)

    def parts(ins, outs, send_sems, recv_sems, local_sems):
        x, y, c, me = _mesh_place()
        sibling = (x, y, 1 - c)
        chips = _other_chips(x, y)

        def slot(px, py, pc):
            return 4 * px + 2 * py + pc

        def copy(a, k, block, to, src=None):
            return pltpu.make_async_remote_copy(
                src_ref=outs[a].at[block] if src is None else src, dst_ref=outs[a].at[block],
                send_sem=send_sems.at[a, k], recv_sem=recv_sems.at[a, k], device_id=to, device_id_type=pl.DeviceIdType.MESH)

        def local():
            return [pltpu.make_async_copy(ins[a], outs[a].at[me], local_sems.at[a]) for a in range(n)]

        def first():
            return [copy(a, k, me, to, src=ins[a]) for a in range(n)
                    for k, to in enumerate([sibling] + [(*chip, c) for chip in chips])]

        return x, y, c, sibling, chips, slot, copy, local, first

    def start(ins, outs, *sems):
        *_, local, first = parts(ins, outs, *sems)
        for cp in local() + first():
            cp.start()

    def wait(ins, outs, *sems):
        x, y, c, sibling, chips, slot, copy, local, first = parts(ins, outs, *sems)
        sends = first()
        for j, chip in enumerate(chips):
            for a in range(n):
                copy(a, 1 + j, slot(*chip, c), sibling).wait_recv()
                passed = copy(a, 4 + j, slot(*chip, c), sibling)
                passed.start()
                sends.append(passed)
        for a in range(n):
            copy(a, 0, slot(x, y, 1 - c), sibling).wait_recv()
        for j, chip in enumerate(chips):
            for a in range(n):
                copy(a, 4 + j, slot(*chip, 1 - c), sibling).wait_recv()
        for cp in sends:
            cp.wait_send()
        for cp in local():
            cp.wait()

    return dict(arrays=list(arrs), out_shape=[jax.ShapeDtypeStruct((N_DEV,) + a.shape, a.dtype) for a in arrs],
                sems=[(n, 7), (n, 7), (n,)], start=start, wait=wait)


N_CHIPS = 4


def _pair_plan(parts):
    n = len(parts)

    def copies(ins, outs, send_sems, recv_sems):
        x, y, c, _ = _mesh_place()
        return [pltpu.make_async_remote_copy(src_ref=ins[a].at[q, 1 - c], dst_ref=outs[a].at[q], send_sem=send_sems.at[a, q],
                                             recv_sem=recv_sems.at[a, q], device_id=(x, y, 1 - c), device_id_type=pl.DeviceIdType.MESH)
                for a in range(n) for q in range(N_CHIPS)]

    def start(ins, outs, send_sems, recv_sems):
        for cp in copies(ins, outs, send_sems, recv_sems):
            cp.start()

    def wait(ins, outs, send_sems, recv_sems):
        cps = copies(ins, outs, send_sems, recv_sems)
        for cp in cps:
            cp.wait_recv()
        for cp in cps:
            cp.wait_send()

    return dict(arrays=list(parts), out_shape=[jax.ShapeDtypeStruct((N_CHIPS,) + a.shape[2:], a.dtype) for a in parts],
                sems=[(n, N_CHIPS), (n, N_CHIPS)], start=start, wait=wait)


def _chip_plan(sums):
    n = len(sums)

    def copies(ins, outs, send_sems, recv_sems, arriving):
        x, y, c, _ = _mesh_place()
        my_chip = 2 * x + y
        cps = []
        for a in range(n):
            for j, (px, py) in enumerate(_other_chips(x, y)):
                src, dst = (my_chip, 2 * px + py) if arriving else (2 * px + py, my_chip)
                cps.append(pltpu.make_async_remote_copy(
                    src_ref=ins[a].at[src], dst_ref=outs[a].at[dst], send_sem=send_sems.at[a, j], recv_sem=recv_sems.at[a, j],
                    device_id=(px, py, c), device_id_type=pl.DeviceIdType.MESH))
        return cps

    def start(ins, outs, send_sems, recv_sems):
        for cp in copies(ins, outs, send_sems, recv_sems, False):
            cp.start()

    def wait(ins, outs, send_sems, recv_sems):
        for cp in copies(ins, outs, send_sems, recv_sems, True):
            cp.wait_recv()
        for cp in copies(ins, outs, send_sems, recv_sems, False):
            cp.wait_send()

    return dict(arrays=list(sums), out_shape=[jax.ShapeDtypeStruct(a.shape, a.dtype) for a in sums],
                sems=[(n, 3), (n, 3)], start=start, wait=wait)


def _both(p1, p2):
    n1, s1 = len(p1["arrays"]), len(p1["sems"])

    def each(method):
        def run(ins, outs, *sems):
            p1[method](ins[:n1], outs[:n1], *sems[:s1])
            p2[method](ins[n1:], outs[n1:], *sems[s1:])
        return run

    return dict(arrays=p1["arrays"] + p2["arrays"], out_shape=p1["out_shape"] + p2["out_shape"],
                sems=p1["sems"] + p2["sems"], start=each("start"), wait=each("wait"))


def _run_exchange(plan, name):
    n = len(plan["arrays"])

    def body(*refs):
        ins, outs, sems = refs[:n], refs[n:2 * n], refs[2 * n:]
        plan["start"](ins, outs, *sems)
        plan["wait"](ins, outs, *sems)

    any_spec = pl.BlockSpec(memory_space=pl.ANY)
    return pl.pallas_call(
        body, name=name, in_specs=[any_spec] * n, out_specs=[any_spec] * n, out_shape=plan["out_shape"],
        scratch_shapes=[pltpu.SemaphoreType.DMA(shape) for shape in plan["sems"]],
    )(*plan["arrays"])


def _riding(plan, body, n_in, n_out, first, last):
    if plan is None:
        return body, [], [], [], []
    ne = len(plan["arrays"])

    def wrapped(*refs):
        ins, ex_in = refs[:n_in], refs[n_in:n_in + ne]
        outs = refs[n_in + ne:n_in + ne + n_out]
        ex_out = refs[n_in + ne + n_out:n_in + 2 * ne + n_out]
        n_sems = len(plan["sems"])
        scratch, sems = refs[n_in + 2 * ne + n_out:-n_sems], refs[-n_sems:]

        @pl.when(first())
        def _():
            plan["start"](ex_in, ex_out, *sems)

        body(*ins, *outs, *scratch)

        @pl.when(last())
        def _():
            plan["wait"](ex_in, ex_out, *sems)

    any_spec = pl.BlockSpec(memory_space=pl.ANY)
    sems = [pltpu.SemaphoreType.DMA(shape) for shape in plan["sems"]]
    return wrapped, [any_spec] * ne, [any_spec] * ne, plan["out_shape"], sems


def _col_tile(r, c, limit_bytes):
    assert c % LANES == 0, c
    best = LANES
    for cand in range(LANES, c + 1, LANES):
        if c % cand == 0 and r * cand * 4 <= limit_bytes:
            best = cand
    return best


def _chip_sum(part, recv, core, name):
    _, _, r, c = part.shape
    ct = _col_tile(r, c, 2 << 20)

    def body(core_ref, p_ref, r_ref, s_ref, t_ref):
        s = p_ref[...] + r_ref[...]
        s_ref[...] = s
        t_ref[...] = s.astype(t_ref.dtype)

    blk = pl.BlockSpec((None, r, ct), lambda q, i, core_ref: (q, 0, i))
    return pl.pallas_call(
        body, name=name,
        grid_spec=pltpu.PrefetchScalarGridSpec(
            num_scalar_prefetch=1, grid=(N_CHIPS, c // ct),
            in_specs=[pl.BlockSpec((None, None, r, ct), lambda q, i, core_ref: (q, core_ref[0], 0, i)), blk],
            out_specs=[blk, blk]),
        out_shape=[jax.ShapeDtypeStruct((N_CHIPS, r, c), F32), jax.ShapeDtypeStruct((N_CHIPS, r, c), GRAD_WIRE_DTYPE)],
        compiler_params=_params(("parallel", "parallel")),
    )(core, part, recv)


def _adam_update(w, g, m, v):
    nm = ADAM_B1 * m + (1.0 - ADAM_B1) * g
    nv = ADAM_B2 * v + (1.0 - ADAM_B2) * (g * g)
    m_hat = nm / (1.0 - ADAM_B1 ** ADAM_STEP)
    v_hat = nv / (1.0 - ADAM_B2 ** ADAM_STEP)
    return -ADAM_LR * (m_hat / (jnp.sqrt(v_hat) + ADAM_EPS) + ADAM_WD * w), nm, nv


def _sum_adamw(own, recv, chip, w, m, v, name, exchange=None):
    _, r, c = own.shape
    ct = _col_tile(r, c, 1 << 20)

    def body(chip_ref, o_ref, r1_ref, r2_ref, r3_ref, w_ref, m_ref, v_ref, g_ref, d_ref, nm_ref, nv_ref):
        g = ((o_ref[...] + r1_ref[...].astype(F32)) + r2_ref[...].astype(F32)) + r3_ref[...].astype(F32)
        g_ref[...] = g
        d_ref[...], nm_ref[...], nv_ref[...] = _adam_update(w_ref[...], g, m_ref[...], v_ref[...])

    def slot(k):
        return pl.BlockSpec((None, r, ct), lambda i, chip_ref: ((chip_ref[0] + k) % N_CHIPS, 0, i))

    spec = pl.BlockSpec((r, ct), lambda i, chip_ref: (0, i))
    shape = jax.ShapeDtypeStruct((r, c), F32)
    n_tiles = c // ct
    body, ex_in, ex_out, ex_shape, ex_sems = _riding(
        exchange, body, 8, 4, lambda: pl.program_id(0) == 0, lambda: pl.program_id(0) == n_tiles - 1)
    res = pl.pallas_call(
        body, name=name,
        grid_spec=pltpu.PrefetchScalarGridSpec(
            num_scalar_prefetch=1, grid=(n_tiles,),
            in_specs=[slot(0), slot(1), slot(2), slot(3), spec, spec, spec] + ex_in, out_specs=[spec] * 4 + ex_out,
            scratch_shapes=ex_sems),
        out_shape=[shape] * 4 + ex_shape,
        compiler_params=_params(("arbitrary",)),
    )(chip, own, recv, recv, recv, w, m, v, *(exchange["arrays"] if exchange else []))
    return res[:4], res[4:]


def _small_adamw(parts, ws, ms, vs):
    n = len(parts)

    def body(*refs):
        p_refs, w_refs, m_refs, v_refs = refs[:n], refs[n:2 * n], refs[2 * n:3 * n], refs[3 * n:4 * n]
        outs = refs[4 * n:]
        for i in range(n):
            g = p_refs[i][0]
            for k in range(1, N_DEV):
                g = g + p_refs[i][k]
            outs[i][...] = g
            outs[n + i][...], outs[2 * n + i][...], outs[3 * n + i][...] = _adam_update(
                w_refs[i][...], g, m_refs[i][...], v_refs[i][...])

    vmem = pl.BlockSpec(memory_space=pltpu.VMEM)
    shapes = [jax.ShapeDtypeStruct(w.shape, F32) for w in ws]
    res = pl.pallas_call(
        body, name="adamw_small", in_specs=[vmem] * (4 * n), out_specs=[vmem] * (4 * n), out_shape=shapes * 4,
        compiler_params=pltpu.CompilerParams(vmem_limit_bytes=VMEM_LIMIT),
    )(*parts, *ws, *ms, *vs)
    return res[:n], res[n:2 * n], res[2 * n:3 * n], res[3 * n:]


def _pack(arrs, dtype, row_multiple):
    flat = jnp.concatenate([a.reshape(-1).astype(dtype) for a in arrs])
    unit = LANES * row_multiple
    padded = -(-flat.shape[0] // unit) * unit
    return jnp.pad(flat, (0, padded - flat.shape[0])).reshape(-1, LANES)


def _unpack(packed, shapes, lead=()):
    flat = packed.reshape(lead + (-1,))
    out, off = [], 0
    for shp in shapes:
        n = math.prod(shp)
        out.append(flat[..., off:off + n].reshape(lead + tuple(shp)))
        off += n
    return out


def _gather_cols(g, lo, hi):
    width = g.shape[2]
    pieces = []
    for s in range(N_DEV):
        a, e = max(lo, s * width), min(hi, (s + 1) * width)
        if a < e:
            pieces.append(g[s, :, a - s * width:e - s * width])
    return pieces[0] if len(pieces) == 1 else jnp.concatenate(pieces, axis=1)


def _scatter_cols(segs, width):
    slots = []
    for k in range(N_DEV):
        lo, hi = k * width, (k + 1) * width
        pieces = []
        for arr, s_lo, s_hi in segs:
            a, e = max(lo, s_lo), min(hi, s_hi)
            if a < e:
                pieces.append(arr[:, a - s_lo:e - s_lo])
        slots.append(pieces[0] if len(pieces) == 1 else jnp.concatenate(pieces, axis=1))
    return jnp.stack(slots)


def _block_diag_groups(w):
    w4 = w.reshape(LRU_NGROUPS, 4, LRU_BLOCK, LRU_BLOCK)
    eye = jnp.eye(4, dtype=w.dtype)
    return jnp.einsum("gaij,ab->gaibj", w4, eye).reshape(LRU_NGROUPS, LRU_GROUP, LRU_GROUP)


def _block_diag_extract(wg):
    w5 = wg.reshape(LRU_NGROUPS, 4, LRU_BLOCK, 4, LRU_BLOCK)
    idx = jnp.arange(4)
    return w5[:, idx, :, idx, :].transpose(1, 0, 2, 3).reshape(LRU_BLOCKS, LRU_BLOCK, LRU_BLOCK)


BIG = ("w_in", "w_kv", "w_br_ssd", "w_br_lru", "w_br_mem", "w_out")
SMALL_SHARDED = ("ssd_conv_w", "ssd_norm_g", "lru_conv_w")
REPLICATED = ("norm_g", "ssd_conv_b", "ssd_dt_bias", "ssd_a_log", "ssd_d", "lru_conv_b", "lru_w_a", "lru_b_a",
              "lru_w_x", "lru_b_x", "lru_lambda", "mem_norm_g", "final_g")
WEIGHTS = ("norm_g", "w_in", "ssd_conv_w", "ssd_conv_b", "ssd_dt_bias", "ssd_a_log", "ssd_d", "ssd_norm_g", "lru_conv_w",
           "lru_conv_b", "lru_w_a", "lru_b_a", "lru_w_x", "lru_b_x", "lru_lambda", "mem_norm_g", "w_kv", "w_br_ssd",
           "w_br_lru", "w_br_mem", "w_out", "final_g")


def kernel(x, mem, norm_g, w_in, ssd_conv_w, ssd_conv_b, ssd_dt_bias, ssd_a_log, ssd_d, ssd_norm_g, lru_conv_w, lru_conv_b, lru_w_a, lru_b_a, lru_w_x, lru_b_x, lru_lambda, mem_norm_g, w_kv, w_br_ssd, w_br_lru, w_br_mem, w_out, final_g, loss_target, m_norm_g, m_w_in, m_ssd_conv_w, m_ssd_conv_b, m_ssd_dt_bias, m_ssd_a_log, m_ssd_d, m_ssd_norm_g, m_lru_conv_w, m_lru_conv_b, m_lru_w_a, m_lru_b_a, m_lru_w_x, m_lru_b_x, m_lru_lambda, m_mem_norm_g, m_w_kv, m_w_br_ssd, m_w_br_lru, m_w_br_mem, m_w_out, m_final_g, v_norm_g, v_w_in, v_ssd_conv_w, v_ssd_conv_b, v_ssd_dt_bias, v_ssd_a_log, v_ssd_d, v_ssd_norm_g, v_lru_conv_w, v_lru_conv_b, v_lru_w_a, v_lru_b_a, v_lru_w_x, v_lru_b_x, v_lru_lambda, v_mem_norm_g, v_w_kv, v_w_br_ssd, v_w_br_lru, v_w_br_mem, v_w_out, v_final_g):
    env = dict(locals())
    W = {n: env[n] for n in WEIGHTS}
    M = {n: env["m_" + n] for n in WEIGHTS}
    V = {n: env["v_" + n] for n in WEIGHTS}
    me = 4 * lax.axis_index("x") + 2 * lax.axis_index("y") + lax.axis_index("c")
    t = x.shape[1]
    xt = x[0]
    memt = mem[0]
    tgt = loss_target[0]

    small_shapes = [W[n].shape for n in SMALL_SHARDED]
    as2d = lambda d, n: jnp.transpose(d[n][0]) if n == "w_in" else d[n][0]
    h, (g_in,) = _rms_fwd(xt, norm_g, "norm_fwd", exchange=_all_gather_plan([as2d(W, "w_in").astype(MXU_DTYPE)]))
    b = SEG_BOUNDS
    w_in_t = g_in.reshape(IN_WIDTH, D_MODEL)
    w_ssd, w_lru, w_q, w_g = w_in_t[b[0]:b[1]], w_in_t[b[2]:b[3]], w_in_t[b[3]:b[4]], w_in_t[b[4]:b[5]]
    w_dt = jnp.pad(w_in_t[b[1]:b[2]], ((0, DT_PAD - SSD_HEADS), (0, 0)))

    later = _all_gather_plan([as2d(W, n).astype(MXU_DTYPE) for n in BIG[1:]] + [_pack([W[n] for n in SMALL_SHARDED], F32, 8)])
    proj_ssd, (g_kv, g_bs, g_bl, g_bm, g_out, gs) = _matmul(h, w_ssd, "nt", "proj_ssd", tm=4096, tn=512, exchange=later)
    g_cw, g_ng, g_lcw = _unpack(gs, small_shapes, (N_DEV,))
    cols = lambda a: jnp.moveaxis(a[:, 0], 0, -2).reshape(a.shape[2:-1] + (-1,))
    rows_ = lambda a: a.reshape((-1,) + a.shape[2:])
    w_bs_f, w_bl_f, w_bm_f, w_out_f = rows_(g_bs), rows_(g_bl), rows_(g_bm), rows_(g_out)
    conv_w_f, ssd_ng_f, lru_cw_f = cols(g_cw), cols(g_ng), cols(g_lcw)
    w_kv_f = _gather_cols(g_kv, 0, 2 * D_MODEL)

    pad_heads = lambda a: jnp.pad(a, ((0, 0), (0, LANES - SSD_HEADS)))
    dtb, alog = pad_heads(ssd_dt_bias), pad_heads(ssd_a_log)
    d_row = jnp.repeat(ssd_d, SSD_HEAD_DIM, axis=1)
    ng_row = ssd_ng_f.reshape(1, SSD_WIDTH)
    wa_g, wx_g = _block_diag_groups(lru_w_a[0]), _block_diag_groups(lru_w_x[0])
    ba, bx = lru_b_a.reshape(1, LRU_WIDTH), lru_b_x.reshape(1, LRU_WIDTH)
    fg = final_g.reshape(1, D_MODEL)

    proj_lru = _matmul(h, w_lru, "nt", "proj_lru", tm=4096, tn=512)
    proj_q = _matmul(h, w_q, "nt", "proj_q", tm=4096, tn=512)
    proj_g = _matmul(h, w_g, "nt", "proj_g", tm=4096, tn=512)
    proj_dt = _matmul(h, w_dt, "nt", "proj_dt", tm=4096)
    mem_n = _rms_fwd(memt, mem_norm_g, "mem_norm_fwd")
    kv = _matmul(mem_n, w_kv_f, "nn", "mem_kv")
    yssd, y_scan, states, ssd_pre = _ssd_fwd(proj_ssd, proj_dt, conv_w_f, ssd_conv_b, dtb, alog, d_row, ng_row)
    ylru, h_lru, xl_lru = _lru_fwd(proj_lru, lru_cw_f, lru_conv_b, wa_g, wx_g, ba, bx, lru_lambda)
    ymem = _mem_fwd(proj_q, kv)
    ps, pl_, pm, merged, dx2, loss_vec, g_fg = _merge_fwd(xt, yssd, ylru, ymem, proj_g, w_bs_f, w_bl_f, w_bm_f, w_out_f, fg, tgt)

    d_g, dps, dpl, dpm, dyssd, dylru, dymem = _merge_bwd(dx2, proj_g, ps, pl_, pm, w_bs_f, w_bl_f, w_bm_f, w_out_f)
    gw_out = _matmul(merged, dx2, "tn", "grad_w_out", tk=2048)
    gw_bs = _matmul(yssd, dps, "tn", "grad_w_br_ssd", tm=2048)
    gw_bl = _matmul(ylru, dpl, "tn", "grad_w_br_lru", tm=LRU_WIDTH, tk=2048)
    gw_bm = _matmul(ymem, dpm, "tn", "grad_w_br_mem", tk=2048)
    d_q, d_kv = _mem_bwd(proj_q, kv, dymem)
    gw_kv = _matmul(mem_n, d_kv, "tn", "grad_w_kv")
    d_memn = _matmul(d_kv, w_kv_f, "nt", "d_mem_n")
    _, g_memng = _rms_bwd(memt, d_memn, None, mem_norm_g, "mem_norm_bwd")

    core = lax.axis_index("c").astype(jnp.int32).reshape(1)
    chip = (2 * lax.axis_index("x") + lax.axis_index("y")).astype(jnp.int32).reshape(1)
    by_chip = lambda a: a.reshape((N_CHIPS, 2, -1) + a.shape[1:])
    early = ("w_kv", "w_br_ssd", "w_br_lru", "w_br_mem", "w_out")
    early_parts = [by_chip(_scatter_cols([(gw_kv, 0, 2 * D_MODEL)], 2 * D_MODEL // N_DEV).reshape(-1, 2 * D_MODEL // N_DEV)),
                   by_chip(gw_bs), by_chip(gw_bl), by_chip(gw_bm), by_chip(gw_out)]
    (d_lru, gl_cw, gl_cb, g_ba, g_bx, g_lam, gwa_g, gwx_g), early_sib = _lru_bwd(
        proj_lru, xl_lru, h_lru, dylru, lru_cw_f, wa_g, wx_g, ba, bx, lru_lambda, exchange=_pair_plan(early_parts))
    early_sums = [_chip_sum(p, r, core, "chip_sum_" + n) for n, p, r in zip(early, early_parts, early_sib)]
    (d_ssd, d_dt, gs_cw, gs_cb, g_dtb, g_alog, g_dch, g_ngrow), early_recv = _ssd_bwd(
        proj_ssd, ssd_pre, proj_dt, y_scan, states, dyssd, conv_w_f, dtb, alog, d_row, ng_row,
        exchange=_chip_plan([s16 for _, s16 in early_sums]))
    gw_ssd = _matmul(d_ssd, h, "tn", "grad_w_in_ssd", tm=2560)
    gw_lru = _matmul(d_lru, h, "tn", "grad_w_in_lru", tm=1536, tk=2048)
    gw_q = _matmul(d_q, h, "tn", "grad_w_in_q", tk=2048)
    gw_g = _matmul(d_g, h, "tn", "grad_w_in_g", tm=1536, tk=2048)
    gw_dt = _matmul(d_dt, h, "tn", "grad_w_in_dt", tk=2048)
    in_part = by_chip(jnp.concatenate([gw_ssd, gw_dt[:SSD_HEADS], gw_lru, gw_q, gw_g], axis=0))

    small_grads = {
        "ssd_conv_w": gs_cw, "ssd_conv_b": gs_cb, "ssd_dt_bias": g_dtb[:, :SSD_HEADS],
        "ssd_a_log": g_alog[:, :SSD_HEADS], "ssd_d": jnp.sum(g_dch.reshape(SSD_HEADS, SSD_HEAD_DIM), axis=1).reshape(1, SSD_HEADS),
        "ssd_norm_g": g_ngrow.reshape(SSD_GROUPS, -1), "lru_conv_w": gl_cw, "lru_conv_b": gl_cb,
        "lru_w_a": _block_diag_extract(gwa_g), "lru_b_a": g_ba, "lru_w_x": _block_diag_extract(gwx_g), "lru_b_x": g_bx,
        "lru_lambda": g_lam, "mem_norm_g": g_memng, "final_g": g_fg,
    }
    small_all = REPLICATED + SMALL_SHARDED

    def small_shape(n, shards):
        shp = W[n].shape[1:] if W[n].ndim > 2 else (1, W[n].shape[-1])
        return shp[:-1] + (shp[-1] * shards,)

    riders = tuple(small_grads)
    small_plan = _all_gather_plan([small_grads[n].reshape(small_shape(n, N_DEV if n in SMALL_SHARDED else 1)) for n in riders])
    dh_segs = [(d_ssd, w_ssd), (d_lru, w_lru), (d_q, w_q), (d_g, w_g), (d_dt, w_dt)]
    dh, landed_a = _dh(dh_segs, 0, exchange=_both(_pair_plan([in_part]), small_plan))
    in_sum = _chip_sum(in_part, landed_a[0], core, "chip_sum_w_in")
    dh, landed_b = _dh(dh_segs, 1, exchange=_chip_plan([in_sum[1]]), into=dh)
    landed = list(landed_b) + list(landed_a[1:])
    grad_x, g_normg = _rms_bwd(xt, dh, dx2, norm_g, "norm_bwd")
    small_recv = dict(zip(riders, landed[1:]))
    reduced = {"w_in": (in_sum[0], landed[0]), **{n: (s[0], r) for n, s, r in zip(early, early_sums, early_recv)}}

    grads, delta, new_m, new_v = {}, {}, {}, {}
    for n in BIG:
        s32, recv = reduced[n]
        res, landed = _sum_adamw(s32, recv, chip, as2d(W, n), as2d(M, n), as2d(V, n), "adamw_" + n,
                                 exchange=_all_gather_plan([g_normg]) if n == "w_in" else None)
        if n == "w_in":
            small_recv["norm_g"] = landed[0]
        for dst, a in zip((grads, delta, new_m, new_v), res):
            dst[n] = (jnp.transpose(a) if n == "w_in" else a)[None]

    parts = []
    for n in small_all:
        a = small_recv[n]
        if n in SMALL_SHARDED:
            width = W[n].shape[-1]
            a = lax.dynamic_slice_in_dim(a, me * width, width, axis=a.ndim - 1)
        parts.append(a)
    canon = lambda d: [d[n].reshape(small_shape(n, 1)) for n in small_all]
    for dst, res in zip((grads, delta, new_m, new_v), _small_adamw(parts, canon(W), canon(M), canon(V))):
        for n, a in zip(small_all, res):
            dst[n] = a.reshape(W[n].shape)

    loss = lax.psum(loss_vec[0, 0], ("x", "y", "c"))
    return (loss, grad_x[None], *[grads[n] for n in WEIGHTS], *[delta[n] for n in WEIGHTS],
            *[new_m[n] for n in WEIGHTS], *[new_v[n] for n in WEIGHTS])
```

```python
import functools
import math

import jax
import jax.numpy as jnp
from jax import lax
from jax.experimental import pallas as pl
from jax.experimental.pallas import tpu as pltpu

F32 = jnp.float32
MXU_DTYPE = jnp.bfloat16
GRAD_WIRE_DTYPE = jnp.bfloat16

D_MODEL = 1024
EPS = 1e-6
CONV_WIDTH = 4
SSD_WIDTH = 2048
SSD_HEAD_DIM = 64
SSD_HEADS = 32
SSD_GROUPS = 4
SSD_STATE = 128
SSD_CHUNK = 128
SSD_BC = SSD_GROUPS * SSD_STATE
SSD_CONV_CH = SSD_WIDTH + 2 * SSD_BC
SSD_PAIRS = SSD_HEADS // 2
PAIRS_PER_GROUP = SSD_PAIRS // SSD_GROUPS
GROUP_COLS = SSD_WIDTH // SSD_GROUPS
LRU_WIDTH = 1536
LRU_BLOCKS = 16
LRU_BLOCK = 96
LRU_GROUP = 4 * LRU_BLOCK
LRU_NGROUPS = LRU_WIDTH // LRU_GROUP
LRU_C = 8.0
LRU_ROWS = 256
LRU_GATE_TAIL = -8.0
MEM_HEADS = 4
MEM_HEAD_DIM = 256
IN_WIDTH = 12320
N_DEV = 8
LANES = 128
SSD_SEG = SSD_WIDTH + SSD_CONV_CH
DT_PAD = LANES
SEG_BOUNDS = (0, 5120, 5152, 8224, 9248, 12320)

ADAM_LR = 0.001
ADAM_B1 = 0.9
ADAM_B2 = 0.999
ADAM_EPS = 1e-08
ADAM_WD = 0.01
ADAM_STEP = 10

VMEM_LIMIT = 56 * 1024 * 1024

NN = (((1,), (0,)), ((), ()))
NT = (((1,), (1,)), ((), ()))
TN = (((0,), (0,)), ((), ()))


def _dot(a, b, dims):
    return lax.dot_general(a.astype(MXU_DTYPE), b.astype(MXU_DTYPE), dims, preferred_element_type=F32)


def _sigmoid(x):
    return 0.5 * jnp.tanh(0.5 * x) + 0.5


def _log1p(e):
    u = 1.0 + e
    return jnp.where(u == 1.0, e, jnp.log(u) * (e / jnp.where(u == 1.0, 1.0, u - 1.0)))


def _softplus(x):
    return jnp.maximum(x, 0.0) + _log1p(jnp.exp(-jnp.abs(x)))


def _params(semantics):
    return pltpu.CompilerParams(dimension_semantics=semantics, vmem_limit_bytes=VMEM_LIMIT)


def _shift_down(cur, halo8, k):
    rolled = pltpu.roll(cur, k, 0)
    row8 = lax.broadcasted_iota(jnp.int32, halo8.shape, 0)
    top = jnp.where(row8 >= k, rolled[0:8], pltpu.roll(halo8, k, 0))
    return jnp.concatenate([top, rolled[8:]], axis=0)


def _shift_up(cur, next8, k):
    rows = cur.shape[0]
    rolled = pltpu.roll(cur, rows - k, 0)
    row8 = lax.broadcasted_iota(jnp.int32, next8.shape, 0)
    bot = jnp.where(row8 < 8 - k, rolled[rows - 8:rows], pltpu.roll(next8, 8 - k, 0))
    return jnp.concatenate([rolled[:rows - 8], bot], axis=0)


def _causal_conv(raw, halo8, w, b):
    acc = raw * w[3:4, :] + b
    for k in range(1, CONV_WIDTH):
        acc = acc + _shift_down(raw, halo8, k) * w[3 - k:4 - k, :]
    return acc


def _conv_backward(dco, next8, raw, w):
    d_raw = dco * w[3:4, :]
    gw = [None] * CONV_WIDTH
    gw[3] = jnp.sum(dco * raw, axis=0, keepdims=True)
    for j in range(1, CONV_WIDTH):
        up = _shift_up(dco, next8, j)
        d_raw = d_raw + up * w[3 - j:4 - j, :]
        gw[3 - j] = jnp.sum(up * raw, axis=0, keepdims=True)
    gb = jnp.sum(dco, axis=0, keepdims=True)
    return d_raw, gw, gb


def _cumsum_rows(v):
    rows = v.shape[0]
    row = lax.broadcasted_iota(jnp.int32, v.shape, 0)
    s = 1
    while s < rows:
        v = v + jnp.where(row >= s, pltpu.roll(v, s, 0), 0.0)
        s *= 2
    return v


def _rev_cumsum_rows(v):
    rows = v.shape[0]
    row = lax.broadcasted_iota(jnp.int32, v.shape, 0)
    s = 1
    while s < rows:
        v = v + jnp.where(row < rows - s, pltpu.roll(v, rows - s, 0), 0.0)
        s *= 2
    return v


def _matmul(a, b, mode, name, tm=1024, tn=1024, tk=1024, exchange=None):
    if mode == "nn":
        (m, kk), n = a.shape, b.shape[1]
    elif mode == "nt":
        (m, kk), n = a.shape, b.shape[0]
    else:
        (kk, m), n = a.shape, b.shape[1]
    tm, tn, tk = min(tm, m), min(tn, n), min(tk, kk)
    assert m % tm == 0 and n % tn == 0 and kk % tk == 0, (name, a.shape, b.shape)
    nk = kk // tk
    dims = {"nn": NN, "nt": NT, "tn": TN}[mode]
    a_spec = pl.BlockSpec((tk, tm), lambda i, j, k: (k, i)) if mode == "tn" else pl.BlockSpec((tm, tk), lambda i, j, k: (i, k))
    b_spec = pl.BlockSpec((tn, tk), lambda i, j, k: (j, k)) if mode == "nt" else pl.BlockSpec((tk, tn), lambda i, j, k: (k, j))
    o_spec = pl.BlockSpec((tm, tn), lambda i, j, k: (i, j))

    def body_single(a_ref, b_ref, o_ref):
        o_ref[...] = _dot(a_ref[...], b_ref[...], dims)

    def body(a_ref, b_ref, o_ref, acc_ref):
        k = pl.program_id(2)

        @pl.when(k == 0)
        def _():
            acc_ref[...] = jnp.zeros_like(acc_ref)

        acc_ref[...] += _dot(a_ref[...], b_ref[...], dims)

        @pl.when(k == nk - 1)
        def _():
            o_ref[...] = acc_ref[...]

    grid = (m // tm, n // tn, nk)
    if exchange is None:
        return pl.pallas_call(
            body_single if nk == 1 else body, name=name, grid=grid, in_specs=[a_spec, b_spec], out_specs=o_spec,
            out_shape=jax.ShapeDtypeStruct((m, n), F32),
            scratch_shapes=[] if nk == 1 else [pltpu.VMEM((tm, tn), F32)],
            compiler_params=_params(("parallel", "parallel", "arbitrary")),
        )(a, b)
    at = lambda ids: functools.reduce(lambda u, v: u & v, [pl.program_id(d) == ids[d] for d in range(3)])
    riding, ex_in, ex_out, ex_shape, ex_sems = _riding(
        exchange, body_single if nk == 1 else body, 2, 1, lambda: at((0, 0, 0)), lambda: at(tuple(g - 1 for g in grid)))
    res = pl.pallas_call(
        riding, name=name, grid=grid, in_specs=[a_spec, b_spec] + ex_in, out_specs=[o_spec] + ex_out,
        out_shape=[jax.ShapeDtypeStruct((m, n), F32)] + ex_shape,
        scratch_shapes=([] if nk == 1 else [pltpu.VMEM((tm, tn), F32)]) + ex_sems,
        compiler_params=_params(("arbitrary", "arbitrary", "arbitrary")),
    )(a, b, *exchange["arrays"])
    return res[0], res[1:]


def _rms_fwd(x, g, name, rows=512, exchange=None):
    t, d = x.shape
    rows = min(rows, t)
    n_tiles = t // rows

    def body(x_ref, g_ref, h_ref):
        xv = x_ref[...]
        r = lax.rsqrt(jnp.mean(xv * xv, axis=-1, keepdims=True) + EPS)
        h_ref[...] = ((xv * r) * g_ref[...]).astype(h_ref.dtype)

    body, ex_in, ex_out, ex_shape, ex_sems = _riding(
        exchange, body, 2, 1, lambda: pl.program_id(0) == 0, lambda: pl.program_id(0) == n_tiles - 1)
    res = pl.pallas_call(
        body, name=name, grid=(n_tiles,),
        in_specs=[pl.BlockSpec((rows, d), lambda i: (i, 0)), pl.BlockSpec((1, d), lambda i: (0, 0))] + ex_in,
        out_specs=[pl.BlockSpec((rows, d), lambda i: (i, 0))] + ex_out,
        out_shape=[jax.ShapeDtypeStruct((t, d), MXU_DTYPE)] + ex_shape,
        scratch_shapes=ex_sems,
        compiler_params=_params(("arbitrary",) if exchange else ("parallel",)),
    )(x, g, *(exchange["arrays"] if exchange else []))
    return (res[0], res[1:]) if exchange else res[0]


def _rms_bwd(x, dh, dres, g, name, rows=512):
    t, d = x.shape
    rows = min(rows, t)
    has_res = dres is not None

    def body(*refs):
        if has_res:
            x_ref, dh_ref, dr_ref, g_ref, dx_ref, gg_ref = refs
        else:
            x_ref, dh_ref, g_ref, dx_ref, gg_ref = refs

        @pl.when(pl.program_id(0) == 0)
        def _():
            gg_ref[...] = jnp.zeros_like(gg_ref)

        xv = x_ref[...]
        dhv = dh_ref[...]
        r = lax.rsqrt(jnp.mean(xv * xv, axis=-1, keepdims=True) + EPS)
        n = xv * r
        dn = dhv * g_ref[...]
        dx = r * (dn - n * jnp.mean(dn * n, axis=-1, keepdims=True))
        if has_res:
            dx = dx + dr_ref[...]
        dx_ref[...] = dx
        gg_ref[...] += jnp.sum(dhv * n, axis=0, keepdims=True)

    row_spec = pl.BlockSpec((rows, d), lambda i: (i, 0))
    vec_spec = pl.BlockSpec((1, d), lambda i: (0, 0))
    args = (x, dh) + ((dres,) if has_res else ()) + (g,)
    return pl.pallas_call(
        body, name=name, grid=(t // rows,),
        in_specs=[row_spec, row_spec] + ([row_spec] if has_res else []) + [vec_spec],
        out_specs=[row_spec, vec_spec],
        out_shape=[jax.ShapeDtypeStruct((t, d), F32), jax.ShapeDtypeStruct((1, d), F32)],
        compiler_params=_params(("arbitrary",)),
    )(*args)


def _dh(segs, half, rows=1024, tk=1024, exchange=None, into=None):
    t, d = segs[0][0].shape[0], segs[0][1].shape[1]
    rows = min(rows, t // 2)
    steps = []
    step0 = 0
    for a, _ in segs:
        kb = min(tk, a.shape[1])
        assert a.shape[1] % kb == 0, a.shape
        steps.append((step0, a.shape[1] // kb, kb))
        step0 += a.shape[1] // kb
    n_steps = step0
    ns = len(segs)
    n_tiles = t // rows // 2
    tile0 = half * n_tiles

    def body(*refs):
        a_refs, w_refs = refs[0:2 * ns:2], refs[1:2 * ns:2]
        dh_ref, acc_ref = refs[-2:]
        k = pl.program_id(1)

        @pl.when(k == 0)
        def _():
            acc_ref[...] = jnp.zeros_like(acc_ref)

        for s, (first, nblk, _) in enumerate(steps):
            @pl.when((k >= first) & (k < first + nblk))
            def _(s=s):
                acc_ref[...] += _dot(a_refs[s][...], w_refs[s][...], NN)

        @pl.when(k == n_steps - 1)
        def _():
            dh_ref[...] = acc_ref[...]

    in_specs, args = [], []
    for (a, w), (first, nblk, kb) in zip(segs, steps):
        blk = lambda k, first=first, nblk=nblk: jnp.clip(k - first, 0, nblk - 1)
        in_specs.append(pl.BlockSpec((rows, kb), lambda i, k, blk=blk: (i + tile0, blk(k))))
        in_specs.append(pl.BlockSpec((kb, d), lambda i, k, blk=blk: (blk(k), 0)))
        args += [a, w]
    row_spec = pl.BlockSpec((rows, d), lambda i, k: (i + tile0, 0))
    if into is not None:
        in_specs.append(pl.BlockSpec(memory_space=pl.ANY))
        args.append(into)
    body, ex_in, ex_out, ex_shape, ex_sems = _riding(
        exchange, body, len(args), 1,
        lambda: (pl.program_id(0) == 0) & (pl.program_id(1) == 0),
        lambda: (pl.program_id(0) == n_tiles - 1) & (pl.program_id(1) == n_steps - 1))
    res = pl.pallas_call(
        body, name=f"dh_{half}", grid=(n_tiles, n_steps),
        in_specs=in_specs + ex_in, out_specs=[row_spec] + ex_out,
        out_shape=[jax.ShapeDtypeStruct((t, d), F32)] + ex_shape,
        scratch_shapes=[pltpu.VMEM((rows, d), F32)] + ex_sems,
        input_output_aliases={} if into is None else {2 * ns: 0},
        compiler_params=_params(("arbitrary", "arbitrary")),
    )(*args, *(exchange["arrays"] if exchange else []))
    return res[0], res[1:]


def _pair_select(lo, m, h0):
    return jnp.where(lo, m[:, h0:h0 + 1], m[:, h0 + 1:h0 + 2])


def _group_select(lo, m, heads):
    return jnp.concatenate([_pair_select(lo, m, h0) for h0 in heads], axis=1)


def _ssd_common(dt_raw, dtb, alog):
    dt = _softplus(dt_raw + dtb)
    aneg = -jnp.exp(alog)
    a_cs = _cumsum_rows(dt * aneg)
    return dt, aneg, a_cs, a_cs.T


def _ssd_specs(nc, rev):
    cidx = (lambda c: nc - 1 - c) if rev else (lambda c: c)
    L = SSD_CHUNK
    b_proj = 2 * SSD_WIDTH // SSD_BC
    b_conv = SSD_WIDTH // SSD_BC
    return dict(
        z=pl.BlockSpec((L, SSD_WIDTH), lambda c: (cidx(c), 0)),
        xr=pl.BlockSpec((L, SSD_WIDTH), lambda c: (cidx(c), 1)),
        br=pl.BlockSpec((L, SSD_BC), lambda c: (cidx(c), b_proj)),
        cr=pl.BlockSpec((L, SSD_BC), lambda c: (cidx(c), b_proj + 1)),
        dt=pl.BlockSpec((L, DT_PAD), lambda c: (cidx(c), 0)),
        cwx=pl.BlockSpec((CONV_WIDTH, SSD_WIDTH), lambda c: (0, 0)),
        cwb=pl.BlockSpec((CONV_WIDTH, SSD_BC), lambda c: (0, b_conv)),
        cwc=pl.BlockSpec((CONV_WIDTH, SSD_BC), lambda c: (0, b_conv + 1)),
        cbx=pl.BlockSpec((1, SSD_WIDTH), lambda c: (0, 0)),
        cbb=pl.BlockSpec((1, SSD_BC), lambda c: (0, b_conv)),
        cbc=pl.BlockSpec((1, SSD_BC), lambda c: (0, b_conv + 1)),
        vec128=pl.BlockSpec((1, LANES), lambda c: (0, 0)),
        vecw=pl.BlockSpec((1, SSD_WIDTH), lambda c: (0, 0)),
        wide=pl.BlockSpec((L, SSD_WIDTH), lambda c: (cidx(c), 0)),
        states=pl.BlockSpec((1, SSD_GROUPS, GROUP_COLS, SSD_STATE), lambda c: (cidx(c), 0, 0, 0)),
    )


def _ssd_fwd(proj_ssd, dt_p, conv_w, conv_b, dtb, alog, d_row, ng_row):
    t = proj_ssd.shape[0]
    nc = t // SSD_CHUNK
    L = SSD_CHUNK
    sp = _ssd_specs(nc, False)

    def body(z_ref, xr_ref, br_ref, cr_ref, dt_ref, cwx_ref, cwb_ref, cwc_ref, cbx_ref, cbb_ref, cbc_ref,
             dtb_ref, alog_ref, d_ref, ng_ref, yssd_ref, y_ref, st_ref, pre_ref,
             hx_ref, hb_ref, hc_ref, state_ref, yacc_ref):
        @pl.when(pl.program_id(0) == 0)
        def _():
            hx_ref[...] = jnp.zeros_like(hx_ref)
            hb_ref[...] = jnp.zeros_like(hb_ref)
            hc_ref[...] = jnp.zeros_like(hc_ref)
            state_ref[...] = jnp.zeros_like(state_ref)

        xr, br, cr = xr_ref[...], br_ref[...], cr_ref[...]
        px = _causal_conv(xr, hx_ref[...], cwx_ref[...], cbx_ref[...])
        pb = _causal_conv(br, hb_ref[...], cwb_ref[...], cbb_ref[...])
        pc = _causal_conv(cr, hc_ref[...], cwc_ref[...], cbc_ref[...])
        hx_ref[...] = xr[L - 8:L, :]
        hb_ref[...] = br[L - 8:L, :]
        hc_ref[...] = cr[L - 8:L, :]
        pre_ref[:, 0:SSD_WIDTH] = px
        pre_ref[:, SSD_WIDTH:SSD_WIDTH + SSD_BC] = pb
        pre_ref[:, SSD_WIDTH + SSD_BC:SSD_CONV_CH] = pc
        xs = px * _sigmoid(px)
        bm = pb * _sigmoid(pb)
        cm = pc * _sigmoid(pc)

        dt, _, a_cs, a_t = _ssd_common(dt_ref[...], dtb_ref[...], alog_ref[...])
        exp_a = jnp.exp(a_cs)
        a_last = a_cs[L - 1:L, :]
        dte = jnp.exp(a_last - a_cs)
        dec = jnp.exp(a_last)

        lane = lax.broadcasted_iota(jnp.int32, (L, LANES), 1)
        sub = lax.broadcasted_iota(jnp.int32, (L, LANES), 0)
        lo = lane < SSD_HEAD_DIM
        causal = sub >= lane
        top = sub < SSD_HEAD_DIM

        for g in range(SSD_GROUPS):
            b_g = bm[:, g * SSD_STATE:(g + 1) * SSD_STATE]
            c_g = cm[:, g * SSD_STATE:(g + 1) * SSD_STATE]
            cb = _dot(c_g, b_g, NT)
            heads = [2 * (g * PAIRS_PER_GROUP + jj) for jj in range(PAIRS_PER_GROUP)]
            gcols = slice(g * GROUP_COLS, (g + 1) * GROUP_COLS)
            xs_g = xs[:, gcols]
            xdt_g = xs_g * _group_select(lo, dt, heads)
            h_g = state_ref[g]
            st_ref[0, g] = h_g
            y_off_g = _dot(c_g, h_g, NT) * _group_select(lo, exp_a, heads)
            s_new_g = _dot(xdt_g * _group_select(lo, dte, heads), b_g, TN)
            for jj, h0 in enumerate(heads):
                blk = slice(jj * LANES, (jj + 1) * LANES)
                cols = slice(g * GROUP_COLS + jj * LANES, g * GROUP_COLS + (jj + 1) * LANES)
                xdt = xdt_g[:, blk]
                g0 = jnp.where(causal, jnp.exp(a_cs[:, h0:h0 + 1] - a_t[h0:h0 + 1, :]), 0.0) * cb
                g1 = jnp.where(causal, jnp.exp(a_cs[:, h0 + 1:h0 + 2] - a_t[h0 + 1:h0 + 2, :]), 0.0) * cb
                lhs = jnp.concatenate([g0, g1], axis=1)
                rhs = jnp.concatenate([jnp.where(lo, xdt, 0.0), jnp.where(lo, 0.0, xdt)], axis=0)
                y_diag = _dot(lhs, rhs, NN)
                dec_rows = jnp.where(top, dec[:, h0:h0 + 1], dec[:, h0 + 1:h0 + 2])
                state_ref[g, blk, :] = h_g[blk, :] * dec_rows + s_new_g[blk, :]
                yacc_ref[:, cols] = (y_diag + y_off_g[:, blk]) + xs_g[:, blk] * d_ref[:, cols]

        y = yacc_ref[...]
        y_ref[...] = y
        zz = z_ref[...]
        y2 = y * (zz * _sigmoid(zz))
        gw = SSD_WIDTH // SSD_GROUPS
        for g in range(SSD_GROUPS):
            seg = y2[:, g * gw:(g + 1) * gw]
            r = lax.rsqrt(jnp.mean(seg * seg, axis=-1, keepdims=True) + EPS)
            yssd_ref[:, g * gw:(g + 1) * gw] = ((seg * r) * ng_ref[:, g * gw:(g + 1) * gw]).astype(yssd_ref.dtype)

    return pl.pallas_call(
        body, name="ssd_fwd", grid=(nc,),
        in_specs=[sp["z"], sp["xr"], sp["br"], sp["cr"], sp["dt"], sp["cwx"], sp["cwb"], sp["cwc"],
                  sp["cbx"], sp["cbb"], sp["cbc"], sp["vec128"], sp["vec128"], sp["vecw"], sp["vecw"]],
        out_specs=[sp["wide"], sp["wide"], sp["states"], pl.BlockSpec((L, SSD_CONV_CH), lambda c: (c, 0))],
        out_shape=[jax.ShapeDtypeStruct((t, SSD_WIDTH), MXU_DTYPE), jax.ShapeDtypeStruct((t, SSD_WIDTH), F32),
                   jax.ShapeDtypeStruct((nc, SSD_GROUPS, GROUP_COLS, SSD_STATE), F32), jax.ShapeDtypeStruct((t, SSD_CONV_CH), F32)],
        scratch_shapes=[pltpu.VMEM((8, SSD_WIDTH), F32), pltpu.VMEM((8, SSD_BC), F32), pltpu.VMEM((8, SSD_BC), F32),
                        pltpu.VMEM((SSD_GROUPS, GROUP_COLS, SSD_STATE), F32), pltpu.VMEM((L, SSD_WIDTH), F32)],
        compiler_params=_params(("arbitrary",)),
    )(proj_ssd, proj_ssd, proj_ssd, proj_ssd, dt_p, conv_w, conv_w, conv_w, conv_b, conv_b, conv_b,
      dtb, alog, d_row, ng_row)


def _ssd_bwd(proj_ssd, pre, dt_p, y, states, dyssd, conv_w, dtb, alog, d_row, ng_row, exchange=None):
    t = proj_ssd.shape[0]
    nc = t // SSD_CHUNK
    L = SSD_CHUNK
    sp = _ssd_specs(nc, True)

    def pre_spec(width, col):
        return pl.BlockSpec((L, width), lambda c: (nc - 1 - c, col))

    def body(z_ref, xr_ref, br_ref, cr_ref, px_ref, pb_ref, pc_ref, dt_ref, y_ref, st_ref, dy_ref,
             cwx_ref, cwb_ref, cwc_ref, dtb_ref, alog_ref, d_ref, ng_ref,
             dssd_ref, ddt_ref, gcw_ref, gcb_ref, gdtb_ref, galog_ref, gd_ref, gng_ref,
             gn_ref, nx_ref, nb_ref, ncc_ref, dxs_ref, r12_ref, r2_ref, rq_ref):
        step = pl.program_id(0)

        @pl.when(step == 0)
        def _():
            gn_ref[...] = jnp.zeros_like(gn_ref)
            nx_ref[...] = jnp.zeros_like(nx_ref)
            nb_ref[...] = jnp.zeros_like(nb_ref)
            ncc_ref[...] = jnp.zeros_like(ncc_ref)
            for ref in (gcw_ref, gcb_ref, gdtb_ref, galog_ref, gd_ref, gng_ref):
                ref[...] = jnp.zeros_like(ref)

        xr, br, cr = xr_ref[...], br_ref[...], cr_ref[...]
        cwx, cwb, cwc = cwx_ref[...], cwb_ref[...], cwc_ref[...]
        px, pb, pc = px_ref[...], pb_ref[...], pc_ref[...]
        sx, sb, sc = _sigmoid(px), _sigmoid(pb), _sigmoid(pc)
        xs, bm, cm = px * sx, pb * sb, pc * sc

        dt_in = dt_ref[...] + dtb_ref[...]
        dt, aneg, a_cs, a_t = _ssd_common(dt_ref[...], dtb_ref[...], alog_ref[...])
        exp_a = jnp.exp(a_cs)
        a_last = a_cs[L - 1:L, :]
        dte = jnp.exp(a_last - a_cs)
        dec = jnp.exp(a_last)

        lane = lax.broadcasted_iota(jnp.int32, (L, LANES), 1)
        sub = lax.broadcasted_iota(jnp.int32, (L, LANES), 0)
        lo = lane < SSD_HEAD_DIM
        causal = sub >= lane
        top = sub < SSD_HEAD_DIM
        last_row = sub == L - 1

        yv = y_ref[...]
        zz = z_ref[...]
        sz = _sigmoid(zz)
        silz = zz * sz
        y2 = yv * silz
        dyv = dy_ref[...]
        gw = SSD_WIDTH // SSD_GROUPS
        d_y2_parts = []
        gng_parts = []
        for g in range(SSD_GROUPS):
            seg = y2[:, g * gw:(g + 1) * gw]
            dseg = dyv[:, g * gw:(g + 1) * gw]
            r = lax.rsqrt(jnp.mean(seg * seg, axis=-1, keepdims=True) + EPS)
            n = seg * r
            dn = dseg * ng_ref[:, g * gw:(g + 1) * gw]
            gng_parts.append(jnp.sum(dseg * n, axis=0, keepdims=True))
            d_y2_parts.append(r * (dn - n * jnp.mean(dn * n, axis=-1, keepdims=True)))
        d_y2 = jnp.concatenate(d_y2_parts, axis=1)
        gng_ref[...] += jnp.concatenate(gng_parts, axis=1)
        d_y = d_y2 * silz
        dssd_ref[:, 0:SSD_WIDTH] = (d_y2 * yv * (sz * (1.0 + zz * (1.0 - sz)))).astype(dssd_ref.dtype)
        gd_ref[...] += jnp.sum(d_y * xs, axis=0, keepdims=True)
        dxs_ref[...] = d_y * d_ref[...]

        d_a = jnp.zeros((L, LANES), F32)
        d_at = jnp.zeros((LANES, L), F32)
        d_b_parts, d_c_parts = [], []
        for g in range(SSD_GROUPS):
            b_g = bm[:, g * SSD_STATE:(g + 1) * SSD_STATE]
            c_g = cm[:, g * SSD_STATE:(g + 1) * SSD_STATE]
            cb = _dot(c_g, b_g, NT)
            d_cb = jnp.zeros((L, L), F32)
            heads = [2 * (g * PAIRS_PER_GROUP + jj) for jj in range(PAIRS_PER_GROUP)]
            gcols = slice(g * GROUP_COLS, (g + 1) * GROUP_COLS)
            dy_g, xs_g = d_y[:, gcols], xs[:, gcols]
            dt_g = _group_select(lo, dt, heads)
            expa_g = _group_select(lo, exp_a, heads)
            dte_g = _group_select(lo, dte, heads)
            xdt_g = xs_g * dt_g
            h_g = st_ref[0, g]
            gn_g = gn_ref[g]
            dys_g = dy_g * expa_g
            d_cg = _dot(dys_g, h_g, NN)
            d_h_g = _dot(dys_g, c_g, TN)
            t1_g = dy_g * _dot(c_g, h_g, NT) * expa_g
            d_bg = _dot(xdt_g * dte_g, gn_g, NN)
            dxdt_g = _dot(b_g, gn_g, NT) * dte_g
            t2_g = dxdt_g * xdt_g
            r12_ref[:, gcols] = t1_g - t2_g
            r2_ref[:, gcols] = t2_g
            gh_g = jnp.sum(gn_g * h_g, axis=1, keepdims=True)
            for jj, h0 in enumerate(heads):
                blk = slice(jj * LANES, (jj + 1) * LANES)
                cols = slice(g * GROUP_COLS + jj * LANES, g * GROUP_COLS + (jj + 1) * LANES)
                dy_p, xs_p, xdt, dt_pp = dy_g[:, blk], xs_g[:, blk], xdt_g[:, blk], dt_g[:, blk]
                l0 = jnp.where(causal, jnp.exp(a_cs[:, h0:h0 + 1] - a_t[h0:h0 + 1, :]), 0.0)
                l1 = jnp.where(causal, jnp.exp(a_cs[:, h0 + 1:h0 + 2] - a_t[h0 + 1:h0 + 2, :]), 0.0)
                g0, g1 = l0 * cb, l1 * cb
                dcat = jnp.concatenate([jnp.where(lo, dy_p, 0.0), jnp.where(lo, 0.0, dy_p)], axis=0)
                d_xdt = dxdt_g[:, blk] + _dot(jnp.concatenate([g0, g1], axis=0), dcat, TN)
                dm = _dot(dcat, xdt, NT)
                dm0, dm1 = dm[0:L], dm[L:2 * L]
                d_cb = d_cb + (l0 * dm0 + l1 * dm1)
                e0, e1 = dm0 * g0, dm1 * g1
                a0 = jnp.sum(e0, axis=1, keepdims=True)
                a1 = jnp.sum(e1, axis=1, keepdims=True)
                gh = gh_g[blk, :]
                dd0 = jnp.sum(jnp.where(top[:, 0:1], gh, 0.0), axis=0, keepdims=True)
                dd1 = jnp.sum(jnp.where(top[:, 0:1], 0.0, gh), axis=0, keepdims=True)
                end0 = dd0 * dec[:, h0:h0 + 1]
                end1 = dd1 * dec[:, h0 + 1:h0 + 2]
                d_a = d_a + jnp.where(lane == h0, a0 + jnp.where(last_row, end0, 0.0), 0.0)
                d_a = d_a + jnp.where(lane == h0 + 1, a1 + jnp.where(last_row, end1, 0.0), 0.0)
                d_at = d_at - jnp.where(sub == h0, jnp.sum(e0, axis=0, keepdims=True), 0.0)
                d_at = d_at - jnp.where(sub == h0 + 1, jnp.sum(e1, axis=0, keepdims=True), 0.0)
                dec_rows = jnp.where(top, dec[:, h0:h0 + 1], dec[:, h0 + 1:h0 + 2])
                gn_ref[g, blk, :] = d_h_g[blk, :] + dec_rows * gn_g[blk, :]
                rq_ref[:, cols] = d_xdt * xs_p
                dxs_ref[:, cols] += d_xdt * dt_pp
            d_cg = d_cg + _dot(d_cb, b_g, NN)
            d_bg = d_bg + _dot(d_cb, c_g, TN)
            d_b_parts.append(d_bg)
            d_c_parts.append(d_cg)

        chan = lax.broadcasted_iota(jnp.int32, (SSD_WIDTH, LANES), 0)
        head = lax.broadcasted_iota(jnp.int32, (SSD_WIDTH, LANES), 1)
        one_hot = jnp.where(lax.shift_right_logical(chan, SSD_HEAD_DIM.bit_length() - 1) == head, 1.0, 0.0)

        def head_sums(v):
            hi = v.astype(MXU_DTYPE)
            return _dot(hi, one_hot, NN) + _dot(v - hi.astype(F32), one_hot, NN)

        s2 = head_sums(r2_ref[...])
        d_a = d_a + head_sums(r12_ref[...]) + jnp.where(last_row, jnp.sum(s2, axis=0, keepdims=True), 0.0)
        ddt = head_sums(rq_ref[...])
        rc = _rev_cumsum_rows(d_a + d_at.T)
        d_dt = rc * aneg + ddt
        galog_ref[...] += jnp.sum(rc * dt, axis=0, keepdims=True) * aneg
        d_dtraw = d_dt * _sigmoid(dt_in)
        gdtb_ref[...] += jnp.sum(d_dtraw, axis=0, keepdims=True)
        ddt_ref[...] = d_dtraw.astype(ddt_ref.dtype)

        def dsilu(p, s):
            return s * (1.0 + p * (1.0 - s))

        dcx = dxs_ref[...] * dsilu(px, sx)
        dcb = jnp.concatenate(d_b_parts, axis=1) * dsilu(pb, sb)
        dcc = jnp.concatenate(d_c_parts, axis=1) * dsilu(pc, sc)
        drx, gwx, gbx = _conv_backward(dcx, nx_ref[...], xr, cwx)
        drb, gwb, gbb = _conv_backward(dcb, nb_ref[...], br, cwb)
        drc, gwc, gbc = _conv_backward(dcc, ncc_ref[...], cr, cwc)
        nx_ref[...] = dcx[0:8, :]
        nb_ref[...] = dcb[0:8, :]
        ncc_ref[...] = dcc[0:8, :]
        dssd_ref[:, SSD_WIDTH:2 * SSD_WIDTH] = drx.astype(dssd_ref.dtype)
        dssd_ref[:, 2 * SSD_WIDTH:2 * SSD_WIDTH + SSD_BC] = drb.astype(dssd_ref.dtype)
        dssd_ref[:, 2 * SSD_WIDTH + SSD_BC:SSD_SEG] = drc.astype(dssd_ref.dtype)
        for k in range(CONV_WIDTH):
            gcw_ref[k:k + 1, :] += jnp.concatenate([gwx[k], gwb[k], gwc[k]], axis=1)
        gcb_ref[...] += jnp.concatenate([gbx, gbb, gbc], axis=1)

    const = lambda shape: pl.BlockSpec(shape, lambda c: (0,) * len(shape))
    body, ex_in, ex_out, ex_shape, ex_sems = _riding(
        exchange, body, 18, 8, lambda: pl.program_id(0) == 0, lambda: pl.program_id(0) == nc - 1)
    res = pl.pallas_call(
        body, name="ssd_bwd", grid=(nc,),
        in_specs=[sp["z"], sp["xr"], sp["br"], sp["cr"], pre_spec(SSD_WIDTH, 0), pre_spec(SSD_BC, SSD_WIDTH // SSD_BC), pre_spec(SSD_BC, SSD_WIDTH // SSD_BC + 1),
                  sp["dt"], sp["wide"], sp["states"], sp["wide"],
                  sp["cwx"], sp["cwb"], sp["cwc"], sp["vec128"], sp["vec128"], sp["vecw"], sp["vecw"]] + ex_in,
        out_specs=[pl.BlockSpec((L, SSD_SEG), lambda c: (nc - 1 - c, 0)), sp["dt"],
                   const((CONV_WIDTH, SSD_CONV_CH)), const((1, SSD_CONV_CH)), const((1, LANES)), const((1, LANES)),
                   const((1, SSD_WIDTH)), const((1, SSD_WIDTH))] + ex_out,
        out_shape=[jax.ShapeDtypeStruct((t, SSD_SEG), MXU_DTYPE), jax.ShapeDtypeStruct((t, DT_PAD), MXU_DTYPE),
                   jax.ShapeDtypeStruct((CONV_WIDTH, SSD_CONV_CH), F32), jax.ShapeDtypeStruct((1, SSD_CONV_CH), F32),
                   jax.ShapeDtypeStruct((1, LANES), F32), jax.ShapeDtypeStruct((1, LANES), F32),
                   jax.ShapeDtypeStruct((1, SSD_WIDTH), F32), jax.ShapeDtypeStruct((1, SSD_WIDTH), F32)] + ex_shape,
        scratch_shapes=[pltpu.VMEM((SSD_GROUPS, GROUP_COLS, SSD_STATE), F32), pltpu.VMEM((8, SSD_WIDTH), F32),
                        pltpu.VMEM((8, SSD_BC), F32), pltpu.VMEM((8, SSD_BC), F32)] + [pltpu.VMEM((L, SSD_WIDTH), F32)] * 4 + ex_sems,
        compiler_params=_params(("arbitrary",)),
    )(proj_ssd, proj_ssd, proj_ssd, proj_ssd, pre, pre, pre, dt_p, y, states, dyssd,
      conv_w, conv_w, conv_w, dtb, alog, d_row, ng_row, *(exchange["arrays"] if exchange else []))
    return res[:8], res[8:]


def _lru_gates(xl, wa_ref, wx_ref, ba, bx, lam):
    pre_a, pre_x = [], []
    for g in range(LRU_NGROUPS):
        xg = xl[:, g * LRU_GROUP:(g + 1) * LRU_GROUP]
        pre_a.append(_dot(xg, wa_ref[g], NN))
        pre_x.append(_dot(xg, wx_ref[g], NN))
    pa = jnp.concatenate(pre_a, axis=1) + ba
    tail = jnp.exp(jnp.minimum(pa, LRU_GATE_TAIL))
    r = jnp.where(pa < LRU_GATE_TAIL, tail * (1.0 - tail), _sigmoid(pa))
    i = _sigmoid(jnp.concatenate(pre_x, axis=1) + bx)
    log_a = (-LRU_C * r) * _softplus(-lam)
    a = jnp.exp(log_a)
    mult_sq = -jnp.tanh(log_a) * (a * a + 1.0)
    return r, i, mult_sq, a, jnp.sqrt(mult_sq)


def _scan_rows(p, u, carry, reverse):
    rows, w = p.shape
    groups = rows // 8
    p3, u3 = p.reshape(groups, 8, w), u.reshape(groups, 8, w)
    row = lax.broadcasted_iota(jnp.int32, (groups, 8, w), 1)
    for s in (1, 2, 4):
        ok = row < 8 - s if reverse else row >= s
        shift = 8 - s if reverse else s
        u3 = p3 * jnp.where(ok, pltpu.roll(u3, shift, 1), 0.0) + u3
        p3 = p3 * jnp.where(ok, pltpu.roll(p3, shift, 1), 1.0)
    out = [None] * groups
    for k in (range(groups - 1, -1, -1) if reverse else range(groups)):
        out[k] = p3[k] * carry + u3[k]
        carry = out[k][0:1, :] if reverse else out[k][7:8, :]
    return jnp.concatenate(out, axis=0), carry


def _lru_fwd(proj_lru, conv_w, conv_b, wa, wx, ba, bx, lam):
    t = proj_lru.shape[0]
    rows = min(LRU_ROWS, t)
    nb = t // rows
    W = LRU_WIDTH

    def body(lg_ref, lx_ref, cw_ref, cb_ref, wa_ref, wx_ref, ba_ref, bx_ref, lam_ref, ylru_ref, h_ref, xl_ref,
             halo_ref, carry_ref):
        @pl.when(pl.program_id(0) == 0)
        def _():
            halo_ref[...] = jnp.zeros_like(halo_ref)
            carry_ref[...] = jnp.zeros_like(carry_ref)

        lx = lx_ref[...]
        xl = _causal_conv(lx, halo_ref[...], cw_ref[...], cb_ref[...])
        halo_ref[...] = lx[rows - 8:rows, :]
        xl_ref[...] = xl
        _, i, _, a, mult = _lru_gates(xl, wa_ref, wx_ref, ba_ref[...], bx_ref[...], lam_ref[...])
        u = mult * (i * xl)
        h, carry_ref[...] = _scan_rows(a, u, carry_ref[...], False)
        h_ref[...] = h
        lg = lg_ref[...]
        ylru_ref[...] = (h * (lg * _sigmoid(lg))).astype(ylru_ref.dtype)

    const = lambda shape: pl.BlockSpec(shape, lambda b: (0,) * len(shape))
    return pl.pallas_call(
        body, name="lru_fwd", grid=(nb,),
        in_specs=[pl.BlockSpec((rows, W), lambda b: (b, 0)), pl.BlockSpec((rows, W), lambda b: (b, 1)),
                  const((CONV_WIDTH, W)), const((1, W)), const((LRU_NGROUPS, LRU_GROUP, LRU_GROUP)),
                  const((LRU_NGROUPS, LRU_GROUP, LRU_GROUP)), const((1, W)), const((1, W)), const((1, W))],
        out_specs=[pl.BlockSpec((rows, W), lambda b: (b, 0))] * 3,
        out_shape=[jax.ShapeDtypeStruct((t, W), MXU_DTYPE), jax.ShapeDtypeStruct((t, W), F32), jax.ShapeDtypeStruct((t, W), F32)],
        scratch_shapes=[pltpu.VMEM((8, W), F32), pltpu.VMEM((1, W), F32)],
        compiler_params=_params(("arbitrary",)),
    )(proj_lru, proj_lru, conv_w, conv_b, wa, wx, ba, bx, lam)


def _lru_bwd(proj_lru, xl, h, dylru, conv_w, wa, wx, ba, bx, lam, exchange=None):
    t = proj_lru.shape[0]
    rows = min(LRU_ROWS, t)
    nb = t // rows
    W = LRU_WIDTH
    groups8 = rows // 8

    def rev(b):
        return nb - 1 - b

    def halo_spec(col):
        return pl.BlockSpec((8, W), lambda b: (jnp.maximum(rev(b) * groups8 - 1, 0), col))

    def body(lg_ref, lx_ref, xl_ref, h_ref, hh_ref, dy_ref, cw_ref, wa_ref, wx_ref, ba_ref, bx_ref, lam_ref,
             dlru_ref, gcw_ref, gcb_ref, gba_ref, gbx_ref, glam_ref, gwa_ref, gwx_ref,
             gcarry_ref, afirst_ref, nxt_ref):
        step = pl.program_id(0)

        @pl.when(step == 0)
        def _():
            gcarry_ref[...] = jnp.zeros_like(gcarry_ref)
            afirst_ref[...] = jnp.zeros_like(afirst_ref)
            nxt_ref[...] = jnp.zeros_like(nxt_ref)
            for ref in (gcw_ref, gcb_ref, gba_ref, gbx_ref, glam_ref, gwa_ref, gwx_ref):
                ref[...] = jnp.zeros_like(ref)

        keep = jnp.where(step == nb - 1, 0.0, 1.0)
        lx = lx_ref[...]
        cw = cw_ref[...]
        xl = xl_ref[...]
        lam = lam_ref[...]
        r, i, mult_sq, a, mult = _lru_gates(xl, wa_ref, wx_ref, ba_ref[...], bx_ref[...], lam)
        hv = h_ref[...]
        h_prev = _shift_down(hv, hh_ref[...] * keep, 1)
        lg = lg_ref[...]
        sg = _sigmoid(lg)
        dyv = dy_ref[...]
        d_h = dyv * (lg * sg)
        dlru_ref[:, 0:W] = (dyv * hv * (sg * (1.0 + lg * (1.0 - sg)))).astype(dlru_ref.dtype)

        row = lax.broadcasted_iota(jnp.int32, (rows, W), 0)
        p = jnp.where(row < rows - 1, pltpu.roll(a, rows - 1, 0), afirst_ref[...])
        gsc, gcarry_ref[...] = _scan_rows(p, d_h, gcarry_ref[...], True)
        afirst_ref[...] = a[0:1, :]

        d_a = gsc * h_prev
        v = i * xl
        d_mult = gsc * v
        d_v = gsc * mult
        d_i = d_v * xl
        d_xl = d_v * i
        d_la = d_a * a - d_mult * (a * a) * lax.rsqrt(mult_sq)
        sp_neg = _softplus(-lam)
        d_r = d_la * (-LRU_C * sp_neg)
        glam_ref[...] += jnp.sum(d_la * r, axis=0, keepdims=True) * (LRU_C * _sigmoid(-lam))
        d_pa = d_r * r * (1.0 - r)
        d_px = d_i * i * (1.0 - i)
        gba_ref[...] += jnp.sum(d_pa, axis=0, keepdims=True)
        gbx_ref[...] += jnp.sum(d_px, axis=0, keepdims=True)
        parts = []
        for g in range(LRU_NGROUPS):
            cols = slice(g * LRU_GROUP, (g + 1) * LRU_GROUP)
            xg, dpa_g, dpx_g = xl[:, cols], d_pa[:, cols], d_px[:, cols]
            parts.append(_dot(dpa_g, wa_ref[g], NT) + _dot(dpx_g, wx_ref[g], NT))
            gwa_ref[g] += _dot(xg, dpa_g, TN)
            gwx_ref[g] += _dot(xg, dpx_g, TN)
        d_xl = d_xl + jnp.concatenate(parts, axis=1)
        d_lx, gw, gb = _conv_backward(d_xl, nxt_ref[...], lx, cw)
        nxt_ref[...] = d_xl[0:8, :]
        dlru_ref[:, W:2 * W] = d_lx.astype(dlru_ref.dtype)
        for k in range(CONV_WIDTH):
            gcw_ref[k:k + 1, :] += gw[k]
        gcb_ref[...] += gb

    const = lambda shape: pl.BlockSpec(shape, lambda b: (0,) * len(shape))
    wspec = const((LRU_NGROUPS, LRU_GROUP, LRU_GROUP))
    blk = lambda col: pl.BlockSpec((rows, W), lambda b: (rev(b), col))
    body, ex_in, ex_out, ex_shape, ex_sems = _riding(
        exchange, body, 12, 8, lambda: pl.program_id(0) == 0, lambda: pl.program_id(0) == nb - 1)
    res = pl.pallas_call(
        body, name="lru_bwd", grid=(nb,),
        in_specs=[blk(0), blk(1), blk(0), blk(0), halo_spec(0), blk(0),
                  const((CONV_WIDTH, W)), wspec, wspec, const((1, W)), const((1, W)), const((1, W))] + ex_in,
        out_specs=[pl.BlockSpec((rows, 2 * W), lambda b: (rev(b), 0)), const((CONV_WIDTH, W)), const((1, W)),
                   const((1, W)), const((1, W)), const((1, W)), wspec, wspec] + ex_out,
        out_shape=[jax.ShapeDtypeStruct((t, 2 * W), MXU_DTYPE), jax.ShapeDtypeStruct((CONV_WIDTH, W), F32),
                   jax.ShapeDtypeStruct((1, W), F32), jax.ShapeDtypeStruct((1, W), F32), jax.ShapeDtypeStruct((1, W), F32),
                   jax.ShapeDtypeStruct((1, W), F32), jax.ShapeDtypeStruct((LRU_NGROUPS, LRU_GROUP, LRU_GROUP), F32),
                   jax.ShapeDtypeStruct((LRU_NGROUPS, LRU_GROUP, LRU_GROUP), F32)] + ex_shape,
        scratch_shapes=[pltpu.VMEM((1, W), F32), pltpu.VMEM((1, W), F32), pltpu.VMEM((8, W), F32)] + ex_sems,
        compiler_params=_params(("arbitrary",)),
    )(proj_lru, proj_lru, xl, h, h, dylru, conv_w, wa, wx, ba, bx, lam, *(exchange["arrays"] if exchange else []))
    return res[:8], res[8:]


def _mem_scores(q_h, k_h):
    s = _dot(q_h, k_h, NT) * (MEM_HEAD_DIM ** -0.5)
    s = s - jnp.max(s, axis=-1, keepdims=True)
    e = jnp.exp(s)
    return e / jnp.sum(e, axis=-1, keepdims=True)


def _mem_fwd(q, kv, rows=512):
    t = q.shape[0]
    rows = min(rows, t)
    m = kv.shape[0]

    def body(q_ref, kv_ref, y_ref):
        for hd in range(MEM_HEADS):
            cols = slice(hd * MEM_HEAD_DIM, (hd + 1) * MEM_HEAD_DIM)
            vcols = slice(D_MODEL + hd * MEM_HEAD_DIM, D_MODEL + (hd + 1) * MEM_HEAD_DIM)
            p = _mem_scores(q_ref[:, cols], kv_ref[:, cols])
            y_ref[:, cols] = _dot(p, kv_ref[:, vcols], NN).astype(y_ref.dtype)

    return pl.pallas_call(
        body, name="mem_fwd", grid=(t // rows,),
        in_specs=[pl.BlockSpec((rows, D_MODEL), lambda i: (i, 0)), pl.BlockSpec((m, 2 * D_MODEL), lambda i: (0, 0))],
        out_specs=pl.BlockSpec((rows, D_MODEL), lambda i: (i, 0)),
        out_shape=jax.ShapeDtypeStruct((t, D_MODEL), MXU_DTYPE),
        compiler_params=_params(("parallel",)),
    )(q, kv)


def _mem_bwd(q, kv, dy, rows=512):
    t = q.shape[0]
    rows = min(rows, t)
    m = kv.shape[0]

    def body(q_ref, kv_ref, dy_ref, dq_ref, dkv_ref):
        @pl.when(pl.program_id(0) == 0)
        def _():
            dkv_ref[...] = jnp.zeros_like(dkv_ref)

        for hd in range(MEM_HEADS):
            cols = slice(hd * MEM_HEAD_DIM, (hd + 1) * MEM_HEAD_DIM)
            vcols = slice(D_MODEL + hd * MEM_HEAD_DIM, D_MODEL + (hd + 1) * MEM_HEAD_DIM)
            q_h, k_h, dy_h = q_ref[:, cols], kv_ref[:, cols], dy_ref[:, cols]
            p = _mem_scores(q_h, k_h)
            dp = _dot(dy_h, kv_ref[:, vcols], NT)
            dkv_ref[:, vcols] += _dot(p, dy_h, TN)
            ds = p * (dp - jnp.sum(dp * p, axis=-1, keepdims=True)) * (MEM_HEAD_DIM ** -0.5)
            dq_ref[:, cols] = _dot(ds, k_h, NN).astype(dq_ref.dtype)
            dkv_ref[:, cols] += _dot(ds, q_h, TN)

    return pl.pallas_call(
        body, name="mem_bwd", grid=(t // rows,),
        in_specs=[pl.BlockSpec((rows, D_MODEL), lambda i: (i, 0)), pl.BlockSpec((m, 2 * D_MODEL), lambda i: (0, 0)),
                  pl.BlockSpec((rows, D_MODEL), lambda i: (i, 0))],
        out_specs=[pl.BlockSpec((rows, D_MODEL), lambda i: (i, 0)), pl.BlockSpec((m, 2 * D_MODEL), lambda i: (0, 0))],
        out_shape=[jax.ShapeDtypeStruct((t, D_MODEL), MXU_DTYPE), jax.ShapeDtypeStruct((m, 2 * D_MODEL), F32)],
        compiler_params=_params(("arbitrary",)),
    )(q, kv, dy)


def _merge_fwd(x, yssd, ylru, ymem, gl, w_bs, w_bl, w_bm, w_out, fg, tgt, rows=256):
    t = x.shape[0]
    rows = min(rows, t)
    D = D_MODEL

    def body(x_ref, ys_ref, yl_ref, ym_ref, gl_ref, wbs_ref, wbl_ref, wbm_ref, wo_ref, fg_ref, tgt_ref,
             ps_ref, pl_ref, pm_ref, mg_ref, dx2_ref, loss_ref, gfg_ref):
        @pl.when(pl.program_id(0) == 0)
        def _():
            loss_ref[...] = jnp.zeros_like(loss_ref)
            gfg_ref[...] = jnp.zeros_like(gfg_ref)

        ps = _dot(ys_ref[...], wbs_ref[...], NN)
        pl_ = _dot(yl_ref[...], wbl_ref[...], NN)
        pm = _dot(ym_ref[...], wbm_ref[...], NN)
        ps_ref[...] = ps
        pl_ref[...] = pl_
        pm_ref[...] = pm
        merged = (_sigmoid(gl_ref[:, 0:D]) * ps + _sigmoid(gl_ref[:, D:2 * D]) * pl_) + _sigmoid(gl_ref[:, 2 * D:3 * D]) * pm
        mg_ref[...] = merged.astype(mg_ref.dtype)
        x2 = x_ref[...] + _dot(merged, wo_ref[...], NN)
        r2 = lax.rsqrt(jnp.mean(x2 * x2, axis=-1, keepdims=True) + EPS)
        xn = x2 * r2
        fg = fg_ref[...]
        diff = xn * fg - tgt_ref[...]
        tile_loss = 0.5 * jnp.sum(jnp.mean(diff * diff, axis=-1, keepdims=True), axis=0, keepdims=True)
        loss_ref[...] += jnp.broadcast_to(tile_loss, loss_ref.shape)
        d_out = diff * (1.0 / D)
        gfg_ref[...] += jnp.sum(d_out * xn, axis=0, keepdims=True)
        dxn = d_out * fg
        dx2_ref[...] = r2 * (dxn - xn * jnp.mean(dxn * xn, axis=-1, keepdims=True))

    row = lambda w: pl.BlockSpec((rows, w), lambda i: (i, 0))
    const = lambda shape: pl.BlockSpec(shape, lambda i: (0,) * len(shape))
    return pl.pallas_call(
        body, name="merge_fwd", grid=(t // rows,),
        in_specs=[row(D), row(SSD_WIDTH), row(LRU_WIDTH), row(D), row(3 * D), const((SSD_WIDTH, D)), const((LRU_WIDTH, D)),
                  const((D, D)), const((D, D)), const((1, D)), row(D)],
        out_specs=[row(D), row(D), row(D), row(D), row(D), const((1, LANES)), const((1, D))],
        out_shape=[jax.ShapeDtypeStruct((t, D), F32), jax.ShapeDtypeStruct((t, D), F32), jax.ShapeDtypeStruct((t, D), F32),
                   jax.ShapeDtypeStruct((t, D), MXU_DTYPE), jax.ShapeDtypeStruct((t, D), F32),
                   jax.ShapeDtypeStruct((1, LANES), F32), jax.ShapeDtypeStruct((1, D), F32)],
        compiler_params=_params(("arbitrary",)),
    )(x, yssd, ylru, ymem, gl, w_bs, w_bl, w_bm, w_out, fg, tgt)


def _merge_bwd(dx2, gl, ps, pl_in, pm, w_bs, w_bl, w_bm, w_out, rows=256):
    t = dx2.shape[0]
    rows = min(rows, t)
    D = D_MODEL

    def body(dx2_ref, gl_ref, ps_ref, pl_ref, pm_ref, wbs_ref, wbl_ref, wbm_ref, wo_ref,
             dg_ref, dps_ref, dpl_ref, dpm_ref, dys_ref, dyl_ref, dym_ref):
        dm = _dot(dx2_ref[...], wo_ref[...], NT)
        for idx, (p_ref, dp_ref, w_ref, dy_ref) in enumerate(
                ((ps_ref, dps_ref, wbs_ref, dys_ref), (pl_ref, dpl_ref, wbl_ref, dyl_ref), (pm_ref, dpm_ref, wbm_ref, dym_ref))):
            gate = _sigmoid(gl_ref[:, idx * D:(idx + 1) * D])
            dg_ref[:, idx * D:(idx + 1) * D] = ((dm * p_ref[...]) * gate * (1.0 - gate)).astype(dg_ref.dtype)
            dp = dm * gate
            dp_ref[...] = dp.astype(dp_ref.dtype)
            dy_ref[...] = _dot(dp, w_ref[...], NT)

    row = lambda w: pl.BlockSpec((rows, w), lambda i: (i, 0))
    const = lambda shape: pl.BlockSpec(shape, lambda i: (0,) * len(shape))
    return pl.pallas_call(
        body, name="merge_bwd", grid=(t // rows,),
        in_specs=[row(D), row(3 * D), row(D), row(D), row(D), const((SSD_WIDTH, D)), const((LRU_WIDTH, D)),
                  const((D, D)), const((D, D))],
        out_specs=[row(3 * D), row(D), row(D), row(D), row(SSD_WIDTH), row(LRU_WIDTH), row(D)],
        out_shape=[jax.ShapeDtypeStruct((t, 3 * D), MXU_DTYPE), jax.ShapeDtypeStruct((t, D), MXU_DTYPE),
                   jax.ShapeDtypeStruct((t, D), MXU_DTYPE), jax.ShapeDtypeStruct((t, D), MXU_DTYPE),
                   jax.ShapeDtypeStruct((t, SSD_WIDTH), F32), jax.ShapeDtypeStruct((t, LRU_WIDTH), F32),
                   jax.ShapeDtypeStruct((t, D), F32)],
        compiler_params=_params(("parallel",)),
    )(dx2, gl, ps, pl_in, pm, w_bs, w_bl, w_bm, w_out)


def _mesh_place():
    x, y, c = lax.axis_index("x"), lax.axis_index("y"), lax.axis_index("c")
    return x, y, c, 4 * x + 2 * y + c


def _other_chips(x, y):
    return [(1 - x, y), (x, 1 - y), (1 - x, 1 - y)]


def _all_gather_plan(arrs):
    n = len(arrs)

    def parts(ins, outs, send_sems, recv_sems, local_sems):
        x, y, c, me = _mesh_place()
        sibling = (x, y, 1 - c)
        chips = _other_chips(x, y)

        def slot(px, py, pc):
            return 4 * px + 2 * py + pc

        def copy(a, k, block, to, src=None):
            return pltpu.make_async_remote_copy(
                src_ref=outs[a].at[block] if src is None else src, dst_ref=outs[a].at[block],
                send_sem=send_sems.at[a, k], recv_sem=recv_sems.at[a, k], device_id=to, device_id_type=pl.DeviceIdType.MESH)

        def local():
            return [pltpu.make_async_copy(ins[a], outs[a].at[me], local_sems.at[a]) for a in range(n)]

        def first():
            return [copy(a, k, me, to, src=ins[a]) for a in range(n)
                    for k, to in enumerate([sibling] + [(*chip, c) for chip in chips])]

        return x, y, c, sibling, chips, slot, copy, local, first

    def start(ins, outs, *sems):
        *_, local, first = parts(ins, outs, *sems)
        for cp in local() + first():
            cp.start()

    def wait(ins, outs, *sems):
        x, y, c, sibling, chips, slot, copy, local, first = parts(ins, outs, *sems)
        sends = first()
        for j, chip in enumerate(chips):
            for a in range(n):
                copy(a, 1 + j, slot(*chip, c), sibling).wait_recv()
                passed = copy(a, 4 + j, slot(*chip, c), sibling)
                passed.start()
                sends.append(passed)
        for a in range(n):
            copy(a, 0, slot(x, y, 1 - c), sibling).wait_recv()
        for j, chip in enumerate(chips):
            for a in range(n):
                copy(a, 4 + j, slot(*chip, 1 - c), sibling).wait_recv()
        for cp in sends:
            cp.wait_send()
        for cp in local():
            cp.wait()

    return dict(arrays=list(arrs), out_shape=[jax.ShapeDtypeStruct((N_DEV,) + a.shape, a.dtype) for a in arrs],
                sems=[(n, 7), (n, 7), (n,)], start=start, wait=wait)


N_CHIPS = 4


def _pair_plan(parts):
    n = len(parts)

    def copies(ins, outs, send_sems, recv_sems):
        x, y, c, _ = _mesh_place()
        return [pltpu.make_async_remote_copy(src_ref=ins[a].at[q, 1 - c], dst_ref=outs[a].at[q], send_sem=send_sems.at[a, q],
                                             recv_sem=recv_sems.at[a, q], device_id=(x, y, 1 - c), device_id_type=pl.DeviceIdType.MESH)
                for a in range(n) for q in range(N_CHIPS)]

    def start(ins, outs, send_sems, recv_sems):
        for cp in copies(ins, outs, send_sems, recv_sems):
            cp.start()

    def wait(ins, outs, send_sems, recv_sems):
        cps = copies(ins, outs, send_sems, recv_sems)
        for cp in cps:
            cp.wait_recv()
        for cp in cps:
            cp.wait_send()

    return dict(arrays=list(parts), out_shape=[jax.ShapeDtypeStruct((N_CHIPS,) + a.shape[2:], a.dtype) for a in parts],
                sems=[(n, N_CHIPS), (n, N_CHIPS)], start=start, wait=wait)


def _chip_plan(sums):
    n = len(sums)

    def copies(ins, outs, send_sems, recv_sems, arriving):
        x, y, c, _ = _mesh_place()
        my_chip = 2 * x + y
        cps = []
        for a in range(n):
            for j, (px, py) in enumerate(_other_chips(x, y)):
                src, dst = (my_chip, 2 * px + py) if arriving else (2 * px + py, my_chip)
                cps.append(pltpu.make_async_remote_copy(
                    src_ref=ins[a].at[src], dst_ref=outs[a].at[dst], send_sem=send_sems.at[a, j], recv_sem=recv_sems.at[a, j],
                    device_id=(px, py, c), device_id_type=pl.DeviceIdType.MESH))
        return cps

    def start(ins, outs, send_sems, recv_sems):
        for cp in copies(ins, outs, send_sems, recv_sems, False):
            cp.start()

    def wait(ins, outs, send_sems, recv_sems):
        for cp in copies(ins, outs, send_sems, recv_sems, True):
            cp.wait_recv()
        for cp in copies(ins, outs, send_sems, recv_sems, False):
            cp.wait_send()

    return dict(arrays=list(sums), out_shape=[jax.ShapeDtypeStruct(a.shape, a.dtype) for a in sums],
                sems=[(n, 3), (n, 3)], start=start, wait=wait)


def _both(p1, p2):
    n1, s1 = len(p1["arrays"]), len(p1["sems"])

    def each(method):
        def run(ins, outs, *sems):
            p1[method](ins[:n1], outs[:n1], *sems[:s1])
            p2[method](ins[n1:], outs[n1:], *sems[s1:])
        return run

    return dict(arrays=p1["arrays"] + p2["arrays"], out_shape=p1["out_shape"] + p2["out_shape"],
                sems=p1["sems"] + p2["sems"], start=each("start"), wait=each("wait"))


def _riding(plan, body, n_in, n_out, first, last):
    if plan is None:
        return body, [], [], [], []
    ne = len(plan["arrays"])

    def wrapped(*refs):
        ins, ex_in = refs[:n_in], refs[n_in:n_in + ne]
        outs = refs[n_in + ne:n_in + ne + n_out]
        ex_out = refs[n_in + ne + n_out:n_in + 2 * ne + n_out]
        n_sems = len(plan["sems"])
        scratch, sems = refs[n_in + 2 * ne + n_out:-n_sems], refs[-n_sems:]

        @pl.when(first())
        def _():
            plan["start"](ex_in, ex_out, *sems)

        body(*ins, *outs, *scratch)

        @pl.when(last())
        def _():
            plan["wait"](ex_in, ex_out, *sems)

    any_spec = pl.BlockSpec(memory_space=pl.ANY)
    sems = [pltpu.SemaphoreType.DMA(shape) for shape in plan["sems"]]
    return wrapped, [any_spec] * ne, [any_spec] * ne, plan["out_shape"], sems


def _col_tile(r, c, limit_bytes):
    assert c % LANES == 0, c
    best = LANES
    for cand in range(LANES, c + 1, LANES):
        if c % cand == 0 and r * cand * 4 <= limit_bytes:
            best = cand
    return best


def _chip_sum(part, recv, core, name):
    _, _, r, c = part.shape
    ct = _col_tile(r, c, 2 << 20)

    def body(core_ref, p_ref, r_ref, s_ref, t_ref):
        s = p_ref[...] + r_ref[...]
        s_ref[...] = s
        t_ref[...] = s.astype(t_ref.dtype)

    blk = pl.BlockSpec((None, r, ct), lambda q, i, core_ref: (q, 0, i))
    return pl.pallas_call(
        body, name=name,
        grid_spec=pltpu.PrefetchScalarGridSpec(
            num_scalar_prefetch=1, grid=(N_CHIPS, c // ct),
            in_specs=[pl.BlockSpec((None, None, r, ct), lambda q, i, core_ref: (q, core_ref[0], 0, i)), blk],
            out_specs=[blk, blk]),
        out_shape=[jax.ShapeDtypeStruct((N_CHIPS, r, c), F32), jax.ShapeDtypeStruct((N_CHIPS, r, c), GRAD_WIRE_DTYPE)],
        compiler_params=_params(("parallel", "parallel")),
    )(core, part, recv)


def _adam_update(w, g, m, v):
    nm = ADAM_B1 * m + (1.0 - ADAM_B1) * g
    nv = ADAM_B2 * v + (1.0 - ADAM_B2) * (g * g)
    m_hat = nm / (1.0 - ADAM_B1 ** ADAM_STEP)
    v_hat = nv / (1.0 - ADAM_B2 ** ADAM_STEP)
    return -ADAM_LR * (m_hat / (jnp.sqrt(v_hat) + ADAM_EPS) + ADAM_WD * w), nm, nv


def _sum_adamw(own, recv, chip, w, m, v, name, exchange=None):
    _, r, c = own.shape
    ct = _col_tile(r, c, 1 << 20)

    def body(chip_ref, o_ref, r1_ref, r2_ref, r3_ref, w_ref, m_ref, v_ref, g_ref, d_ref, nm_ref, nv_ref):
        g = ((o_ref[...] + r1_ref[...].astype(F32)) + r2_ref[...].astype(F32)) + r3_ref[...].astype(F32)
        g_ref[...] = g
        d_ref[...], nm_ref[...], nv_ref[...] = _adam_update(w_ref[...], g, m_ref[...], v_ref[...])

    def slot(k):
        return pl.BlockSpec((None, r, ct), lambda i, chip_ref: ((chip_ref[0] + k) % N_CHIPS, 0, i))

    spec = pl.BlockSpec((r, ct), lambda i, chip_ref: (0, i))
    shape = jax.ShapeDtypeStruct((r, c), F32)
    n_tiles = c // ct
    body, ex_in, ex_out, ex_shape, ex_sems = _riding(
        exchange, body, 8, 4, lambda: pl.program_id(0) == 0, lambda: pl.program_id(0) == n_tiles - 1)
    res = pl.pallas_call(
        body, name=name,
        grid_spec=pltpu.PrefetchScalarGridSpec(
            num_scalar_prefetch=1, grid=(n_tiles,),
            in_specs=[slot(0), slot(1), slot(2), slot(3), spec, spec, spec] + ex_in, out_specs=[spec] * 4 + ex_out,
            scratch_shapes=ex_sems),
        out_shape=[shape] * 4 + ex_shape,
        compiler_params=_params(("arbitrary",)),
    )(chip, own, recv, recv, recv, w, m, v, *(exchange["arrays"] if exchange else []))
    return res[:4], res[4:]


def _small_adamw(parts, ws, ms, vs):
    n = len(parts)

    def body(*refs):
        p_refs, w_refs, m_refs, v_refs = refs[:n], refs[n:2 * n], refs[2 * n:3 * n], refs[3 * n:4 * n]
        outs = refs[4 * n:]
        for i in range(n):
            g = p_refs[i][0]
            for k in range(1, N_DEV):
                g = g + p_refs[i][k]
            outs[i][...] = g
            outs[n + i][...], outs[2 * n + i][...], outs[3 * n + i][...] = _adam_update(
                w_refs[i][...], g, m_refs[i][...], v_refs[i][...])

    vmem = pl.BlockSpec(memory_space=pltpu.VMEM)
    shapes = [jax.ShapeDtypeStruct(w.shape, F32) for w in ws]
    res = pl.pallas_call(
        body, name="adamw_small", in_specs=[vmem] * (4 * n), out_specs=[vmem] * (4 * n), out_shape=shapes * 4,
        compiler_params=pltpu.CompilerParams(vmem_limit_bytes=VMEM_LIMIT),
    )(*parts, *ws, *ms, *vs)
    return res[:n], res[n:2 * n], res[2 * n:3 * n], res[3 * n:]


def _pack(arrs, dtype, row_multiple):
    flat = jnp.concatenate([a.reshape(-1).astype(dtype) for a in arrs])
    unit = LANES * row_multiple
    padded = -(-flat.shape[0] // unit) * unit
    return jnp.pad(flat, (0, padded - flat.shape[0])).reshape(-1, LANES)


def _unpack(packed, shapes, lead=()):
    flat = packed.reshape(lead + (-1,))
    out, off = [], 0
    for shp in shapes:
        n = math.prod(shp)
        out.append(flat[..., off:off + n].reshape(lead + tuple(shp)))
        off += n
    return out


def _gather_cols(g, lo, hi):
    width = g.shape[2]
    pieces = []
    for s in range(N_DEV):
        a, e = max(lo, s * width), min(hi, (s + 1) * width)
        if a < e:
            pieces.append(g[s, :, a - s * width:e - s * width])
    return pieces[0] if len(pieces) == 1 else jnp.concatenate(pieces, axis=1)


def _scatter_cols(segs, width):
    slots = []
    for k in range(N_DEV):
        lo, hi = k * width, (k + 1) * width
        pieces = []
        for arr, s_lo, s_hi in segs:
            a, e = max(lo, s_lo), min(hi, s_hi)
            if a < e:
                pieces.append(arr[:, a - s_lo:e - s_lo])
        slots.append(pieces[0] if len(pieces) == 1 else jnp.concatenate(pieces, axis=1))
    return jnp.stack(slots)


def _block_diag_groups(w):
    w4 = w.reshape(LRU_NGROUPS, 4, LRU_BLOCK, LRU_BLOCK)
    eye = jnp.eye(4, dtype=w.dtype)
    return jnp.einsum("gaij,ab->gaibj", w4, eye).reshape(LRU_NGROUPS, LRU_GROUP, LRU_GROUP)


def _block_diag_extract(wg):
    w5 = wg.reshape(LRU_NGROUPS, 4, LRU_BLOCK, 4, LRU_BLOCK)
    idx = jnp.arange(4)
    return w5[:, idx, :, idx, :].transpose(1, 0, 2, 3).reshape(LRU_BLOCKS, LRU_BLOCK, LRU_BLOCK)


BIG = ("w_in", "w_kv", "w_br_ssd", "w_br_lru", "w_br_mem", "w_out")
SMALL_SHARDED = ("ssd_conv_w", "ssd_norm_g", "lru_conv_w")
REPLICATED = ("norm_g", "ssd_conv_b", "ssd_dt_bias", "ssd_a_log", "ssd_d", "lru_conv_b", "lru_w_a", "lru_b_a",
              "lru_w_x", "lru_b_x", "lru_lambda", "mem_norm_g", "final_g")
WEIGHTS = ("norm_g", "w_in", "ssd_conv_w", "ssd_conv_b", "ssd_dt_bias", "ssd_a_log", "ssd_d", "ssd_norm_g", "lru_conv_w",
           "lru_conv_b", "lru_w_a", "lru_b_a", "lru_w_x", "lru_b_x", "lru_lambda", "mem_norm_g", "w_kv", "w_br_ssd",
           "w_br_lru", "w_br_mem", "w_out", "final_g")


def kernel(x, mem, norm_g, w_in, ssd_conv_w, ssd_conv_b, ssd_dt_bias, ssd_a_log, ssd_d, ssd_norm_g, lru_conv_w, lru_conv_b, lru_w_a, lru_b_a, lru_w_x, lru_b_x, lru_lambda, mem_norm_g, w_kv, w_br_ssd, w_br_lru, w_br_mem, w_out, final_g, loss_target, m_norm_g, m_w_in, m_ssd_conv_w, m_ssd_conv_b, m_ssd_dt_bias, m_ssd_a_log, m_ssd_d, m_ssd_norm_g, m_lru_conv_w, m_lru_conv_b, m_lru_w_a, m_lru_b_a, m_lru_w_x, m_lru_b_x, m_lru_lambda, m_mem_norm_g, m_w_kv, m_w_br_ssd, m_w_br_lru, m_w_br_mem, m_w_out, m_final_g, v_norm_g, v_w_in, v_ssd_conv_w, v_ssd_conv_b, v_ssd_dt_bias, v_ssd_a_log, v_ssd_d, v_ssd_norm_g, v_lru_conv_w, v_lru_conv_b, v_lru_w_a, v_lru_b_a, v_lru_w_x, v_lru_b_x, v_lru_lambda, v_mem_norm_g, v_w_kv, v_w_br_ssd, v_w_br_lru, v_w_br_mem, v_w_out, v_final_g):
    env = dict(locals())
    W = {n: env[n] for n in WEIGHTS}
    M = {n: env["m_" + n] for n in WEIGHTS}
    V = {n: env["v_" + n] for n in WEIGHTS}
    me = 4 * lax.axis_index("x") + 2 * lax.axis_index("y") + lax.axis_index("c")
    t = x.shape[1]
    xt = x[0]
    memt = mem[0]
    tgt = loss_target[0]

    small_shapes = [W[n].shape for n in SMALL_SHARDED]
    as2d = lambda d, n: jnp.transpose(d[n][0]) if n == "w_in" else d[n][0]
    h, (g_in,) = _rms_fwd(xt, norm_g, "norm_fwd", exchange=_all_gather_plan([as2d(W, "w_in").astype(MXU_DTYPE)]))
    b = SEG_BOUNDS
    w_in_t = g_in.reshape(IN_WIDTH, D_MODEL)
    w_ssd, w_lru, w_q, w_g = w_in_t[b[0]:b[1]], w_in_t[b[2]:b[3]], w_in_t[b[3]:b[4]], w_in_t[b[4]:b[5]]
    w_dt = jnp.pad(w_in_t[b[1]:b[2]], ((0, DT_PAD - SSD_HEADS), (0, 0)))

    later = _all_gather_plan([as2d(W, n).astype(MXU_DTYPE) for n in BIG[1:]] + [_pack([W[n] for n in SMALL_SHARDED], F32, 8)])
    proj_ssd, (g_kv, g_bs, g_bl, g_bm, g_out, gs) = _matmul(h, w_ssd, "nt", "proj_ssd", tm=4096, tn=512, exchange=later)
    g_cw, g_ng, g_lcw = _unpack(gs, small_shapes, (N_DEV,))
    cols = lambda a: jnp.moveaxis(a[:, 0], 0, -2).reshape(a.shape[2:-1] + (-1,))
    rows_ = lambda a: a.reshape((-1,) + a.shape[2:])
    w_bs_f, w_bl_f, w_bm_f, w_out_f = rows_(g_bs), rows_(g_bl), rows_(g_bm), rows_(g_out)
    conv_w_f, ssd_ng_f, lru_cw_f = cols(g_cw), cols(g_ng), cols(g_lcw)
    w_kv_f = _gather_cols(g_kv, 0, 2 * D_MODEL)

    pad_heads = lambda a: jnp.pad(a, ((0, 0), (0, LANES - SSD_HEADS)))
    dtb, alog = pad_heads(ssd_dt_bias), pad_heads(ssd_a_log)
    d_row = jnp.repeat(ssd_d, SSD_HEAD_DIM, axis=1)
    ng_row = ssd_ng_f.reshape(1, SSD_WIDTH)
    wa_g, wx_g = _block_diag_groups(lru_w_a[0]), _block_diag_groups(lru_w_x[0])
    ba, bx = lru_b_a.reshape(1, LRU_WIDTH), lru_b_x.reshape(1, LRU_WIDTH)
    fg = final_g.reshape(1, D_MODEL)

    proj_lru = _matmul(h, w_lru, "nt", "proj_lru", tm=4096, tn=512)
    proj_q = _matmul(h, w_q, "nt", "proj_q", tm=4096, tn=512)
    proj_g = _matmul(h, w_g, "nt", "proj_g", tm=4096, tn=512)
    proj_dt = _matmul(h, w_dt, "nt", "proj_dt", tm=4096)
    mem_n = _rms_fwd(memt, mem_norm_g, "mem_norm_fwd")
    kv = _matmul(mem_n, w_kv_f, "nn", "mem_kv")
    yssd, y_scan, states, ssd_pre = _ssd_fwd(proj_ssd, proj_dt, conv_w_f, ssd_conv_b, dtb, alog, d_row, ng_row)
    ylru, h_lru, xl_lru = _lru_fwd(proj_lru, lru_cw_f, lru_conv_b, wa_g, wx_g, ba, bx, lru_lambda)
    ymem = _mem_fwd(proj_q, kv)
    ps, pl_, pm, merged, dx2, loss_vec, g_fg = _merge_fwd(xt, yssd, ylru, ymem, proj_g, w_bs_f, w_bl_f, w_bm_f, w_out_f, fg, tgt)

    d_g, dps, dpl, dpm, dyssd, dylru, dymem = _merge_bwd(dx2, proj_g, ps, pl_, pm, w_bs_f, w_bl_f, w_bm_f, w_out_f)
    gw_out = _matmul(merged, dx2, "tn", "grad_w_out", tk=2048)
    gw_bs = _matmul(yssd, dps, "tn", "grad_w_br_ssd", tm=2048)
    gw_bl = _matmul(ylru, dpl, "tn", "grad_w_br_lru", tm=LRU_WIDTH, tk=2048)
    gw_bm = _matmul(ymem, dpm, "tn", "grad_w_br_mem", tk=2048)
    d_q, d_kv = _mem_bwd(proj_q, kv, dymem)
    gw_kv = _matmul(mem_n, d_kv, "tn", "grad_w_kv")
    d_memn = _matmul(d_kv, w_kv_f, "nt", "d_mem_n")
    _, g_memng = _rms_bwd(memt, d_memn, None, mem_norm_g, "mem_norm_bwd")

    core = lax.axis_index("c").astype(jnp.int32).reshape(1)
    chip = (2 * lax.axis_index("x") + lax.axis_index("y")).astype(jnp.int32).reshape(1)
    by_chip = lambda a: a.reshape((N_CHIPS, 2, -1) + a.shape[1:])
    early = ("w_kv", "w_br_ssd", "w_br_lru", "w_br_mem", "w_out")
    early_parts = [by_chip(_scatter_cols([(gw_kv, 0, 2 * D_MODEL)], 2 * D_MODEL // N_DEV).reshape(-1, 2 * D_MODEL // N_DEV)),
                   by_chip(gw_bs), by_chip(gw_bl), by_chip(gw_bm), by_chip(gw_out)]
    (d_lru, gl_cw, gl_cb, g_ba, g_bx, g_lam, gwa_g, gwx_g), early_sib = _lru_bwd(
        proj_lru, xl_lru, h_lru, dylru, lru_cw_f, wa_g, wx_g, ba, bx, lru_lambda, exchange=_pair_plan(early_parts))
    early_sums = [_chip_sum(p, r, core, "chip_sum_" + n) for n, p, r in zip(early, early_parts, early_sib)]
    (d_ssd, d_dt, gs_cw, gs_cb, g_dtb, g_alog, g_dch, g_ngrow), early_recv = _ssd_bwd(
        proj_ssd, ssd_pre, proj_dt, y_scan, states, dyssd, conv_w_f, dtb, alog, d_row, ng_row,
        exchange=_chip_plan([s16 for _, s16 in early_sums]))
    gw_ssd = _matmul(d_ssd, h, "tn", "grad_w_in_ssd", tm=2560)
    gw_lru = _matmul(d_lru, h, "tn", "grad_w_in_lru", tm=1536, tk=2048)
    gw_q = _matmul(d_q, h, "tn", "grad_w_in_q", tk=2048)
    gw_g = _matmul(d_g, h, "tn", "grad_w_in_g", tm=1536, tk=2048)
    gw_dt = _matmul(d_dt, h, "tn", "grad_w_in_dt", tk=2048)
    in_part = by_chip(jnp.concatenate([gw_ssd, gw_dt[:SSD_HEADS], gw_lru, gw_q, gw_g], axis=0))

    small_grads = {
        "ssd_conv_w": gs_cw, "ssd_conv_b": gs_cb, "ssd_dt_bias": g_dtb[:, :SSD_HEADS],
        "ssd_a_log": g_alog[:, :SSD_HEADS], "ssd_d": jnp.sum(g_dch.reshape(SSD_HEADS, SSD_HEAD_DIM), axis=1).reshape(1, SSD_HEADS),
        "ssd_norm_g": g_ngrow.reshape(SSD_GROUPS, -1), "lru_conv_w": gl_cw, "lru_conv_b": gl_cb,
        "lru_w_a": _block_diag_extract(gwa_g), "lru_b_a": g_ba, "lru_w_x": _block_diag_extract(gwx_g), "lru_b_x": g_bx,
        "lru_lambda": g_lam, "mem_norm_g": g_memng, "final_g": g_fg,
    }
    small_all = REPLICATED + SMALL_SHARDED

    def small_shape(n, shards):
        shp = W[n].shape[1:] if W[n].ndim > 2 else (1, W[n].shape[-1])
        return shp[:-1] + (shp[-1] * shards,)

    riders = tuple(small_grads)
    small_plan = _all_gather_plan([small_grads[n].reshape(small_shape(n, N_DEV if n in SMALL_SHARDED else 1)) for n in riders])
    dh_segs = [(d_ssd, w_ssd), (d_lru, w_lru), (d_q, w_q), (d_g, w_g), (d_dt, w_dt)]
    dh, landed_a = _dh(dh_segs, 0, exchange=_both(_pair_plan([in_part]), small_plan))
    in_sum = _chip_sum(in_part, landed_a[0], core, "chip_sum_w_in")
    dh, landed_b = _dh(dh_segs, 1, exchange=_chip_plan([in_sum[1]]), into=dh)
    landed = list(landed_b) + list(landed_a[1:])
    grad_x, g_normg = _rms_bwd(xt, dh, dx2, norm_g, "norm_bwd")
    small_recv = dict(zip(riders, landed[1:]))
    reduced = {"w_in": (in_sum[0], landed[0]), **{n: (s[0], r) for n, s, r in zip(early, early_sums, early_recv)}}

    grads, delta, new_m, new_v = {}, {}, {}, {}
    for n in BIG:
        s32, recv = reduced[n]
        res, landed = _sum_adamw(s32, recv, chip, as2d(W, n), as2d(M, n), as2d(V, n), "adamw_" + n,
                                 exchange=_all_gather_plan([g_normg]) if n == "w_in" else None)
        if n == "w_in":
            small_recv["norm_g"] = landed[0]
        for dst, a in zip((grads, delta, new_m, new_v), res):
            dst[n] = (jnp.transpose(a) if n == "w_in" else a)[None]

    parts = []
    for n in small_all:
        a = small_recv[n]
        if n in SMALL_SHARDED:
            width = W[n].shape[-1]
            a = lax.dynamic_slice_in_dim(a, me * width, width, axis=a.ndim - 1)
        parts.append(a)
    canon = lambda d: [d[n].reshape(small_shape(n, 1)) for n in small_all]
    for dst, res in zip((grads, delta, new_m, new_v), _small_adamw(parts, canon(W), canon(M), canon(V))):
        for n, a in zip(small_all, res):
            dst[n] = a.reshape(W[n].shape)

    loss = lax.psum(loss_vec[0, 0], ("x", "y", "c"))
    return (loss, grad_x[None], *[grads[n] for n in WEIGHTS], *[delta[n] for n in WEIGHTS],
            *[new_m[n] for n in WEIGHTS], *[new_v[n] for n in WEIGHTS])
```

```python
import functools
import math

import jax
import jax.numpy as jnp
from jax import lax
from jax.experimental import pallas as pl
from jax.experimental.pallas import tpu as pltpu

F32 = jnp.float32
MXU_DTYPE = jnp.bfloat16
GRAD_WIRE_DTYPE = jnp.bfloat16

D_MODEL = 1024
EPS = 1e-6
CONV_WIDTH = 4
SSD_WIDTH = 2048
SSD_HEAD_DIM = 64
SSD_HEADS = 32
SSD_GROUPS = 4
SSD_STATE = 128
SSD_CHUNK = 128
SSD_BC = SSD_GROUPS * SSD_STATE
SSD_CONV_CH = SSD_WIDTH + 2 * SSD_BC
SSD_PAIRS = SSD_HEADS // 2
PAIRS_PER_GROUP = SSD_PAIRS // SSD_GROUPS
GROUP_COLS = SSD_WIDTH // SSD_GROUPS
LRU_WIDTH = 1536
LRU_BLOCKS = 16
LRU_BLOCK = 96
LRU_GROUP = 4 * LRU_BLOCK
LRU_NGROUPS = LRU_WIDTH // LRU_GROUP
LRU_C = 8.0
LRU_ROWS = 256
LRU_GATE_TAIL = -8.0
MEM_HEADS = 4
MEM_HEAD_DIM = 256
IN_WIDTH = 12320
N_DEV = 8
LANES = 128
SSD_SEG = SSD_WIDTH + SSD_CONV_CH
DT_PAD = LANES
SEG_BOUNDS = (0, 5120, 5152, 8224, 9248, 12320)

ADAM_LR = 0.001
ADAM_B1 = 0.9
ADAM_B2 = 0.999
ADAM_EPS = 1e-08
ADAM_WD = 0.01
ADAM_STEP = 10

VMEM_LIMIT = 56 * 1024 * 1024

NN = (((1,), (0,)), ((), ()))
NT = (((1,), (1,)), ((), ()))
TN = (((0,), (0,)), ((), ()))


def _dot(a, b, dims):
    return lax.dot_general(a.astype(MXU_DTYPE), b.astype(MXU_DTYPE), dims, preferred_element_type=F32)


def _sigmoid(x):
    return 0.5 * jnp.tanh(0.5 * x) + 0.5


def _log1p(e):
    u = 1.0 + e
    return jnp.where(u == 1.0, e, jnp.log(u) * (e / jnp.where(u == 1.0, 1.0, u - 1.0)))


def _softplus(x):
    return jnp.maximum(x, 0.0) + _log1p(jnp.exp(-jnp.abs(x)))


def _params(semantics):
    return pltpu.CompilerParams(dimension_semantics=semantics, vmem_limit_bytes=VMEM_LIMIT)


def _shift_down(cur, halo8, k):
    rolled = pltpu.roll(cur, k, 0)
    row8 = lax.broadcasted_iota(jnp.int32, halo8.shape, 0)
    top = jnp.where(row8 >= k, rolled[0:8], pltpu.roll(halo8, k, 0))
    return jnp.concatenate([top, rolled[8:]], axis=0)


def _shift_up(cur, next8, k):
    rows = cur.shape[0]
    rolled = pltpu.roll(cur, rows - k, 0)
    row8 = lax.broadcasted_iota(jnp.int32, next8.shape, 0)
    bot = jnp.where(row8 < 8 - k, rolled[rows - 8:rows], pltpu.roll(next8, 8 - k, 0))
    return jnp.concatenate([rolled[:rows - 8], bot], axis=0)


def _causal_conv(raw, halo8, w, b):
    acc = raw * w[3:4, :] + b
    for k in range(1, CONV_WIDTH):
        acc = acc + _shift_down(raw, halo8, k) * w[3 - k:4 - k, :]
    return acc


def _conv_backward(dco, next8, raw, w):
    d_raw = dco * w[3:4, :]
    gw = [None] * CONV_WIDTH
    gw[3] = jnp.sum(dco * raw, axis=0, keepdims=True)
    for j in range(1, CONV_WIDTH):
        up = _shift_up(dco, next8, j)
        d_raw = d_raw + up * w[3 - j:4 - j, :]
        gw[3 - j] = jnp.sum(up * raw, axis=0, keepdims=True)
    gb = jnp.sum(dco, axis=0, keepdims=True)
    return d_raw, gw, gb


def _cumsum_rows(v):
    rows = v.shape[0]
    row = lax.broadcasted_iota(jnp.int32, v.shape, 0)
    s = 1
    while s < rows:
        v = v + jnp.where(row >= s, pltpu.roll(v, s, 0), 0.0)
        s *= 2
    return v


def _rev_cumsum_rows(v):
    rows = v.shape[0]
    row = lax.broadcasted_iota(jnp.int32, v.shape, 0)
    s = 1
    while s < rows:
        v = v + jnp.where(row < rows - s, pltpu.roll(v, rows - s, 0), 0.0)
        s *= 2
    return v


def _matmul(a, b, mode, name, tm=1024, tn=1024, tk=1024, exchange=None):
    if mode == "nn":
        (m, kk), n = a.shape, b.shape[1]
    elif mode == "nt":
        (m, kk), n = a.shape, b.shape[0]
    else:
        (kk, m), n = a.shape, b.shape[1]
    tm, tn, tk = min(tm, m), min(tn, n), min(tk, kk)
    assert m % tm == 0 and n % tn == 0 and kk % tk == 0, (name, a.shape, b.shape)
    nk = kk // tk
    dims = {"nn": NN, "nt": NT, "tn": TN}[mode]
    a_spec = pl.BlockSpec((tk, tm), lambda i, j, k: (k, i)) if mode == "tn" else pl.BlockSpec((tm, tk), lambda i, j, k: (i, k))
    b_spec = pl.BlockSpec((tn, tk), lambda i, j, k: (j, k)) if mode == "nt" else pl.BlockSpec((tk, tn), lambda i, j, k: (k, j))
    o_spec = pl.BlockSpec((tm, tn), lambda i, j, k: (i, j))

    def body_single(a_ref, b_ref, o_ref):
        o_ref[...] = _dot(a_ref[...], b_ref[...], dims)

    def body(a_ref, b_ref, o_ref, acc_ref):
        k = pl.program_id(2)

        @pl.when(k == 0)
        def _():
            acc_ref[...] = jnp.zeros_like(acc_ref)

        acc_ref[...] += _dot(a_ref[...], b_ref[...], dims)

        @pl.when(k == nk - 1)
        def _():
            o_ref[...] = acc_ref[...]

    grid = (m // tm, n // tn, nk)
    if exchange is None:
        return pl.pallas_call(
            body_single if nk == 1 else body, name=name, grid=grid, in_specs=[a_spec, b_spec], out_specs=o_spec,
            out_shape=jax.ShapeDtypeStruct((m, n), F32),
            scratch_shapes=[] if nk == 1 else [pltpu.VMEM((tm, tn), F32)],
            compiler_params=_params(("parallel", "parallel", "arbitrary")),
        )(a, b)
    at = lambda ids: functools.reduce(lambda u, v: u & v, [pl.program_id(d) == ids[d] for d in range(3)])
    riding, ex_in, ex_out, ex_shape, ex_sems = _riding(
        exchange, body_single if nk == 1 else body, 2, 1, lambda: at((0, 0, 0)), lambda: at(tuple(g - 1 for g in grid)))
    res = pl.pallas_call(
        riding, name=name, grid=grid, in_specs=[a_spec, b_spec] + ex_in, out_specs=[o_spec] + ex_out,
        out_shape=[jax.ShapeDtypeStruct((m, n), F32)] + ex_shape,
        scratch_shapes=([] if nk == 1 else [pltpu.VMEM((tm, tn), F32)]) + ex_sems,
        compiler_params=_params(("arbitrary", "arbitrary", "arbitrary")),
    )(a, b, *exchange["arrays"])
    return res[0], res[1:]


def _rms_fwd(x, g, name, rows=512, exchange=None):
    t, d = x.shape
    rows = min(rows, t)
    n_tiles = t // rows

    def body(x_ref, g_ref, h_ref):
        xv = x_ref[...]
        r = lax.rsqrt(jnp.mean(xv * xv, axis=-1, keepdims=True) + EPS)
        h_ref[...] = ((xv * r) * g_ref[...]).astype(h_ref.dtype)

    body, ex_in, ex_out, ex_shape, ex_sems = _riding(
        exchange, body, 2, 1, lambda: pl.program_id(0) == 0, lambda: pl.program_id(0) == n_tiles - 1)
    res = pl.pallas_call(
        body, name=name, grid=(n_tiles,),
        in_specs=[pl.BlockSpec((rows, d), lambda i: (i, 0)), pl.BlockSpec((1, d), lambda i: (0, 0))] + ex_in,
        out_specs=[pl.BlockSpec((rows, d), lambda i: (i, 0))] + ex_out,
        out_shape=[jax.ShapeDtypeStruct((t, d), MXU_DTYPE)] + ex_shape,
        scratch_shapes=ex_sems,
        compiler_params=_params(("arbitrary",) if exchange else ("parallel",)),
    )(x, g, *(exchange["arrays"] if exchange else []))
    return (res[0], res[1:]) if exchange else res[0]


def _rms_bwd(x, dh, dres, g, name, rows=512):
    t, d = x.shape
    rows = min(rows, t)
    has_res = dres is not None

    def body(*refs):
        if has_res:
            x_ref, dh_ref, dr_ref, g_ref, dx_ref, gg_ref = refs
        else:
            x_ref, dh_ref, g_ref, dx_ref, gg_ref = refs

        @pl.when(pl.program_id(0) == 0)
        def _():
            gg_ref[...] = jnp.zeros_like(gg_ref)

        xv = x_ref[...]
        dhv = dh_ref[...]
        r = lax.rsqrt(jnp.mean(xv * xv, axis=-1, keepdims=True) + EPS)
        n = xv * r
        dn = dhv * g_ref[...]
        dx = r * (dn - n * jnp.mean(dn * n, axis=-1, keepdims=True))
        if has_res:
            dx = dx + dr_ref[...]
        dx_ref[...] = dx
        gg_ref[...] += jnp.sum(dhv * n, axis=0, keepdims=True)

    row_spec = pl.BlockSpec((rows, d), lambda i: (i, 0))
    vec_spec = pl.BlockSpec((1, d), lambda i: (0, 0))
    args = (x, dh) + ((dres,) if has_res else ()) + (g,)
    return pl.pallas_call(
        body, name=name, grid=(t // rows,),
        in_specs=[row_spec, row_spec] + ([row_spec] if has_res else []) + [vec_spec],
        out_specs=[row_spec, vec_spec],
        out_shape=[jax.ShapeDtypeStruct((t, d), F32), jax.ShapeDtypeStruct((1, d), F32)],
        compiler_params=_params(("arbitrary",)),
    )(*args)


def _dh(segs, half, rows=1024, tk=1024, exchange=None, into=None):
    t, d = segs[0][0].shape[0], segs[0][1].shape[1]
    rows = min(rows, t // 2)
    steps = []
    step0 = 0
    for a, _ in segs:
        kb = min(tk, a.shape[1])
        assert a.shape[1] % kb == 0, a.shape
        steps.append((step0, a.shape[1] // kb, kb))
        step0 += a.shape[1] // kb
    n_steps = step0
    ns = len(segs)
    n_tiles = t // rows // 2
    tile0 = half * n_tiles

    def body(*refs):
        a_refs, w_refs = refs[0:2 * ns:2], refs[1:2 * ns:2]
        dh_ref, acc_ref = refs[-2:]
        k = pl.program_id(1)

        @pl.when(k == 0)
        def _():
            acc_ref[...] = jnp.zeros_like(acc_ref)

        for s, (first, nblk, _) in enumerate(steps):
            @pl.when((k >= first) & (k < first + nblk))
            def _(s=s):
                acc_ref[...] += _dot(a_refs[s][...], w_refs[s][...], NN)

        @pl.when(k == n_steps - 1)
        def _():
            dh_ref[...] = acc_ref[...]

    in_specs, args = [], []
    for (a, w), (first, nblk, kb) in zip(segs, steps):
        blk = lambda k, first=first, nblk=nblk: jnp.clip(k - first, 0, nblk - 1)
        in_specs.append(pl.BlockSpec((rows, kb), lambda i, k, blk=blk: (i + tile0, blk(k))))
        in_specs.append(pl.BlockSpec((kb, d), lambda i, k, blk=blk: (blk(k), 0)))
        args += [a, w]
    row_spec = pl.BlockSpec((rows, d), lambda i, k: (i + tile0, 0))
    if into is not None:
        in_specs.append(pl.BlockSpec(memory_space=pl.ANY))
        args.append(into)
    body, ex_in, ex_out, ex_shape, ex_sems = _riding(
        exchange, body, len(args), 1,
        lambda: (pl.program_id(0) == 0) & (pl.program_id(1) == 0),
        lambda: (pl.program_id(0) == n_tiles - 1) & (pl.program_id(1) == n_steps - 1))
    res = pl.pallas_call(
        body, name=f"dh_{half}", grid=(n_tiles, n_steps),
        in_specs=in_specs + ex_in, out_specs=[row_spec] + ex_out,
        out_shape=[jax.ShapeDtypeStruct((t, d), F32)] + ex_shape,
        scratch_shapes=[pltpu.VMEM((rows, d), F32)] + ex_sems,
        input_output_aliases={} if into is None else {2 * ns: 0},
        compiler_params=_params(("arbitrary", "arbitrary")),
    )(*args, *(exchange["arrays"] if exchange else []))
    return res[0], res[1:]


def _pair_select(lo, m, h0):
    return jnp.where(lo, m[:, h0:h0 + 1], m[:, h0 + 1:h0 + 2])


def _group_select(lo, m, heads):
    return jnp.concatenate([_pair_select(lo, m, h0) for h0 in heads], axis=1)


def _ssd_common(dt_raw, dtb, alog):
    dt = _softplus(dt_raw + dtb)
    aneg = -jnp.exp(alog)
    a_cs = _cumsum_rows(dt * aneg)
    return dt, aneg, a_cs, a_cs.T


def _ssd_specs(nc, rev):
    cidx = (lambda c: nc - 1 - c) if rev else (lambda c: c)
    L = SSD_CHUNK
    b_proj = 2 * SSD_WIDTH // SSD_BC
    b_conv = SSD_WIDTH // SSD_BC
    return dict(
        z=pl.BlockSpec((L, SSD_WIDTH), lambda c: (cidx(c), 0)),
        xr=pl.BlockSpec((L, SSD_WIDTH), lambda c: (cidx(c), 1)),
        br=pl.BlockSpec((L, SSD_BC), lambda c: (cidx(c), b_proj)),
        cr=pl.BlockSpec((L, SSD_BC), lambda c: (cidx(c), b_proj + 1)),
        dt=pl.BlockSpec((L, DT_PAD), lambda c: (cidx(c), 0)),
        cwx=pl.BlockSpec((CONV_WIDTH, SSD_WIDTH), lambda c: (0, 0)),
        cwb=pl.BlockSpec((CONV_WIDTH, SSD_BC), lambda c: (0, b_conv)),
        cwc=pl.BlockSpec((CONV_WIDTH, SSD_BC), lambda c: (0, b_conv + 1)),
        cbx=pl.BlockSpec((1, SSD_WIDTH), lambda c: (0, 0)),
        cbb=pl.BlockSpec((1, SSD_BC), lambda c: (0, b_conv)),
        cbc=pl.BlockSpec((1, SSD_BC), lambda c: (0, b_conv + 1)),
        vec128=pl.BlockSpec((1, LANES), lambda c: (0, 0)),
        vecw=pl.BlockSpec((1, SSD_WIDTH), lambda c: (0, 0)),
        wide=pl.BlockSpec((L, SSD_WIDTH), lambda c: (cidx(c), 0)),
        states=pl.BlockSpec((1, SSD_GROUPS, GROUP_COLS, SSD_STATE), lambda c: (cidx(c), 0, 0, 0)),
    )


def _ssd_fwd(proj_ssd, dt_p, conv_w, conv_b, dtb, alog, d_row, ng_row):
    t = proj_ssd.shape[0]
    nc = t // SSD_CHUNK
    L = SSD_CHUNK
    sp = _ssd_specs(nc, False)

    def body(z_ref, xr_ref, br_ref, cr_ref, dt_ref, cwx_ref, cwb_ref, cwc_ref, cbx_ref, cbb_ref, cbc_ref,
             dtb_ref, alog_ref, d_ref, ng_ref, yssd_ref, y_ref, st_ref, pre_ref,
             hx_ref, hb_ref, hc_ref, state_ref, yacc_ref):
        @pl.when(pl.program_id(0) == 0)
        def _():
            hx_ref[...] = jnp.zeros_like(hx_ref)
            hb_ref[...] = jnp.zeros_like(hb_ref)
            hc_ref[...] = jnp.zeros_like(hc_ref)
            state_ref[...] = jnp.zeros_like(state_ref)

        xr, br, cr = xr_ref[...], br_ref[...], cr_ref[...]
        px = _causal_conv(xr, hx_ref[...], cwx_ref[...], cbx_ref[...])
        pb = _causal_conv(br, hb_ref[...], cwb_ref[...], cbb_ref[...])
        pc = _causal_conv(cr, hc_ref[...], cwc_ref[...], cbc_ref[...])
        hx_ref[...] = xr[L - 8:L, :]
        hb_ref[...] = br[L - 8:L, :]
        hc_ref[...] = cr[L - 8:L, :]
        pre_ref[:, 0:SSD_WIDTH] = px
        pre_ref[:, SSD_WIDTH:SSD_WIDTH + SSD_BC] = pb
        pre_ref[:, SSD_WIDTH + SSD_BC:SSD_CONV_CH] = pc
        xs = px * _sigmoid(px)
        bm = pb * _sigmoid(pb)
        cm = pc * _sigmoid(pc)

        dt, _, a_cs, a_t = _ssd_common(dt_ref[...], dtb_ref[...], alog_ref[...])
        exp_a = jnp.exp(a_cs)
        a_last = a_cs[L - 1:L, :]
        dte = jnp.exp(a_last - a_cs)
        dec = jnp.exp(a_last)

        lane = lax.broadcasted_iota(jnp.int32, (L, LANES), 1)
        sub = lax.broadcasted_iota(jnp.int32, (L, LANES), 0)
        lo = lane < SSD_HEAD_DIM
        causal = sub >= lane
        top = sub < SSD_HEAD_DIM

        for g in range(SSD_GROUPS):
            b_g = bm[:, g * SSD_STATE:(g + 1) * SSD_STATE]
            c_g = cm[:, g * SSD_STATE:(g + 1) * SSD_STATE]
            cb = _dot(c_g, b_g, NT)
            heads = [2 * (g * PAIRS_PER_GROUP + jj) for jj in range(PAIRS_PER_GROUP)]
            gcols = slice(g * GROUP_COLS, (g + 1) * GROUP_COLS)
            xs_g = xs[:, gcols]
            xdt_g = xs_g * _group_select(lo, dt, heads)
            h_g = state_ref[g]
            st_ref[0, g] = h_g
            y_off_g = _dot(c_g, h_g, NT) * _group_select(lo, exp_a, heads)
            s_new_g = _dot(xdt_g * _group_select(lo, dte, heads), b_g, TN)
            for jj, h0 in enumerate(heads):
                blk = slice(jj * LANES, (jj + 1) * LANES)
                cols = slice(g * GROUP_COLS + jj * LANES, g * GROUP_COLS + (jj + 1) * LANES)
                xdt = xdt_g[:, blk]
                g0 = jnp.where(causal, jnp.exp(a_cs[:, h0:h0 + 1] - a_t[h0:h0 + 1, :]), 0.0) * cb
                g1 = jnp.where(causal, jnp.exp(a_cs[:, h0 + 1:h0 + 2] - a_t[h0 + 1:h0 + 2, :]), 0.0) * cb
                lhs = jnp.concatenate([g0, g1], axis=1)
                rhs = jnp.concatenate([jnp.where(lo, xdt, 0.0), jnp.where(lo, 0.0, xdt)], axis=0)
                y_diag = _dot(lhs, rhs, NN)
                dec_rows = jnp.where(top, dec[:, h0:h0 + 1], dec[:, h0 + 1:h0 + 2])
                state_ref[g, blk, :] = h_g[blk, :] * dec_rows + s_new_g[blk, :]
                yacc_ref[:, cols] = (y_diag + y_off_g[:, blk]) + xs_g[:, blk] * d_ref[:, cols]

        y = yacc_ref[...]
        y_ref[...] = y
        zz = z_ref[...]
        y2 = y * (zz * _sigmoid(zz))
        gw = SSD_WIDTH // SSD_GROUPS
        for g in range(SSD_GROUPS):
            seg = y2[:, g * gw:(g + 1) * gw]
            r = lax.rsqrt(jnp.mean(seg * seg, axis=-1, keepdims=True) + EPS)
            yssd_ref[:, g * gw:(g + 1) * gw] = ((seg * r) * ng_ref[:, g * gw:(g + 1) * gw]).astype(yssd_ref.dtype)

    return pl.pallas_call(
        body, name="ssd_fwd", grid=(nc,),
        in_specs=[sp["z"], sp["xr"], sp["br"], sp["cr"], sp["dt"], sp["cwx"], sp["cwb"], sp["cwc"],
                  sp["cbx"], sp["cbb"], sp["cbc"], sp["vec128"], sp["vec128"], sp["vecw"], sp["vecw"]],
        out_specs=[sp["wide"], sp["wide"], sp["states"], pl.BlockSpec((L, SSD_CONV_CH), lambda c: (c, 0))],
        out_shape=[jax.ShapeDtypeStruct((t, SSD_WIDTH), MXU_DTYPE), jax.ShapeDtypeStruct((t, SSD_WIDTH), F32),
                   jax.ShapeDtypeStruct((nc, SSD_GROUPS, GROUP_COLS, SSD_STATE), F32), jax.ShapeDtypeStruct((t, SSD_CONV_CH), F32)],
        scratch_shapes=[pltpu.VMEM((8, SSD_WIDTH), F32), pltpu.VMEM((8, SSD_BC), F32), pltpu.VMEM((8, SSD_BC), F32),
                        pltpu.VMEM((SSD_GROUPS, GROUP_COLS, SSD_STATE), F32), pltpu.VMEM((L, SSD_WIDTH), F32)],
        compiler_params=_params(("arbitrary",)),
    )(proj_ssd, proj_ssd, proj_ssd, proj_ssd, dt_p, conv_w, conv_w, conv_w, conv_b, conv_b, conv_b,
      dtb, alog, d_row, ng_row)


def _ssd_bwd(proj_ssd, pre, dt_p, y, states, dyssd, conv_w, dtb, alog, d_row, ng_row, exchange=None):
    t = proj_ssd.shape[0]
    nc = t // SSD_CHUNK
    L = SSD_CHUNK
    sp = _ssd_specs(nc, True)

    def pre_spec(width, col):
        return pl.BlockSpec((L, width), lambda c: (nc - 1 - c, col))

    def body(z_ref, xr_ref, br_ref, cr_ref, px_ref, pb_ref, pc_ref, dt_ref, y_ref, st_ref, dy_ref,
             cwx_ref, cwb_ref, cwc_ref, dtb_ref, alog_ref, d_ref, ng_ref,
             dssd_ref, ddt_ref, gcw_ref, gcb_ref, gdtb_ref, galog_ref, gd_ref, gng_ref,
             gn_ref, nx_ref, nb_ref, ncc_ref, dxs_ref, r12_ref, r2_ref, rq_ref, re_ref):
        step = pl.program_id(0)

        @pl.when(step == 0)
        def _():
            gn_ref[...] = jnp.zeros_like(gn_ref)
            nx_ref[...] = jnp.zeros_like(nx_ref)
            nb_ref[...] = jnp.zeros_like(nb_ref)
            ncc_ref[...] = jnp.zeros_like(ncc_ref)
            for ref in (gcw_ref, gcb_ref, gdtb_ref, galog_ref, gd_ref, gng_ref):
                ref[...] = jnp.zeros_like(ref)

        xr, br, cr = xr_ref[...], br_ref[...], cr_ref[...]
        cwx, cwb, cwc = cwx_ref[...], cwb_ref[...], cwc_ref[...]
        px, pb, pc = px_ref[...], pb_ref[...], pc_ref[...]
        sx, sb, sc = _sigmoid(px), _sigmoid(pb), _sigmoid(pc)
        xs, bm, cm = px * sx, pb * sb, pc * sc

        dt_in = dt_ref[...] + dtb_ref[...]
        dt, aneg, a_cs, a_t = _ssd_common(dt_ref[...], dtb_ref[...], alog_ref[...])
        exp_a = jnp.exp(a_cs)
        a_last = a_cs[L - 1:L, :]
        dte = jnp.exp(a_last - a_cs)
        dec = jnp.exp(a_last)

        lane = lax.broadcasted_iota(jnp.int32, (L, LANES), 1)
        sub = lax.broadcasted_iota(jnp.int32, (L, LANES), 0)
        lo = lane < SSD_HEAD_DIM
        causal = sub >= lane
        top = sub < SSD_HEAD_DIM
        last_row = sub == L - 1

        yv = y_ref[...]
        zz = z_ref[...]
        sz = _sigmoid(zz)
        silz = zz * sz
        y2 = yv * silz
        dyv = dy_ref[...]
        gw = SSD_WIDTH // SSD_GROUPS
        d_y2_parts = []
        gng_parts = []
        for g in range(SSD_GROUPS):
            seg = y2[:, g * gw:(g + 1) * gw]
            dseg = dyv[:, g * gw:(g + 1) * gw]
            r = lax.rsqrt(jnp.mean(seg * seg, axis=-1, keepdims=True) + EPS)
            n = seg * r
            dn = dseg * ng_ref[:, g * gw:(g + 1) * gw]
            gng_parts.append(jnp.sum(dseg * n, axis=0, keepdims=True))
            d_y2_parts.append(r * (dn - n * jnp.mean(dn * n, axis=-1, keepdims=True)))
        d_y2 = jnp.concatenate(d_y2_parts, axis=1)
        gng_ref[...] += jnp.concatenate(gng_parts, axis=1)
        d_y = d_y2 * silz
        dssd_ref[:, 0:SSD_WIDTH] = (d_y2 * yv * (sz * (1.0 + zz * (1.0 - sz)))).astype(dssd_ref.dtype)
        gd_ref[...] += jnp.sum(d_y * xs, axis=0, keepdims=True)
        dxs_ref[...] = d_y * d_ref[...]

        d_a = jnp.zeros((L, LANES), F32)
        d_at = jnp.zeros((LANES, L), F32)
        d_b_parts, d_c_parts = [], []
        for g in range(SSD_GROUPS):
            b_g = bm[:, g * SSD_STATE:(g + 1) * SSD_STATE]
            c_g = cm[:, g * SSD_STATE:(g + 1) * SSD_STATE]
            cb = _dot(c_g, b_g, NT)
            d_cb = jnp.zeros((L, L), F32)
            heads = [2 * (g * PAIRS_PER_GROUP + jj) for jj in range(PAIRS_PER_GROUP)]
            gcols = slice(g * GROUP_COLS, (g + 1) * GROUP_COLS)
            dy_g, xs_g = d_y[:, gcols], xs[:, gcols]
            dt_g = _group_select(lo, dt, heads)
            expa_g = _group_select(lo, exp_a, heads)
            dte_g = _group_select(lo, dte, heads)
            xdt_g = xs_g * dt_g
            h_g = st_ref[0, g]
            gn_g = gn_ref[g]
            dys_g = dy_g * expa_g
            d_cg = _dot(dys_g, h_g, NN)
            d_h_g = _dot(dys_g, c_g, TN)
            t1_g = dy_g * _dot(c_g, h_g, NT) * expa_g
            d_bg = _dot(xdt_g * dte_g, gn_g, NN)
            dxdt_g = _dot(b_g, gn_g, NT) * dte_g
            t2_g = dxdt_g * xdt_g
            r12_ref[:, gcols] = t1_g - t2_g
            r2_ref[:, gcols] = t2_g
            gh_g = jnp.sum(gn_g * h_g, axis=1, keepdims=True)
            for jj, h0 in enumerate(heads):
                blk = slice(jj * LANES, (jj + 1) * LANES)
                cols = slice(g * GROUP_COLS + jj * LANES, g * GROUP_COLS + (jj + 1) * LANES)
                dy_p, xs_p, xdt, dt_pp = dy_g[:, blk], xs_g[:, blk], xdt_g[:, blk], dt_g[:, blk]
                l0 = jnp.where(causal, jnp.exp(a_cs[:, h0:h0 + 1] - a_t[h0:h0 + 1, :]), 0.0)
                l1 = jnp.where(causal, jnp.exp(a_cs[:, h0 + 1:h0 + 2] - a_t[h0 + 1:h0 + 2, :]), 0.0)
                g0, g1 = l0 * cb, l1 * cb
                dcat = jnp.concatenate([jnp.where(lo, dy_p, 0.0), jnp.where(lo, 0.0, dy_p)], axis=0)
                d_xdt = dxdt_g[:, blk] + _dot(jnp.concatenate([g0, g1], axis=0), dcat, TN)
                dm = _dot(dcat, xdt, NT)
                dm0, dm1 = dm[0:L], dm[L:2 * L]
                d_cb = d_cb + (l0 * dm0 + l1 * dm1)
                e0, e1 = dm0 * g0, dm1 * g1
                re_ref[:, h0 * LANES:(h0 + 1) * LANES] = e0
                re_ref[:, (h0 + 1) * LANES:(h0 + 2) * LANES] = e1
                gh = gh_g[blk, :]
                dd0 = jnp.sum(jnp.where(top[:, 0:1], gh, 0.0), axis=0, keepdims=True)
                dd1 = jnp.sum(jnp.where(top[:, 0:1], 0.0, gh), axis=0, keepdims=True)
                end0 = dd0 * dec[:, h0:h0 + 1]
                end1 = dd1 * dec[:, h0 + 1:h0 + 2]
                d_a = d_a + jnp.where((lane == h0) & last_row, end0, 0.0) + jnp.where((lane == h0 + 1) & last_row, end1, 0.0)
                d_at = d_at - jnp.where(sub == h0, jnp.sum(e0, axis=0, keepdims=True), 0.0)
                d_at = d_at - jnp.where(sub == h0 + 1, jnp.sum(e1, axis=0, keepdims=True), 0.0)
                dec_rows = jnp.where(top, dec[:, h0:h0 + 1], dec[:, h0 + 1:h0 + 2])
                gn_ref[g, blk, :] = d_h_g[blk, :] + dec_rows * gn_g[blk, :]
                rq_ref[:, cols] = d_xdt * xs_p
                dxs_ref[:, cols] += d_xdt * dt_pp
            d_cg = d_cg + _dot(d_cb, b_g, NN)
            d_bg = d_bg + _dot(d_cb, c_g, TN)
            d_b_parts.append(d_bg)
            d_c_parts.append(d_cg)

        chan = lax.broadcasted_iota(jnp.int32, (SSD_WIDTH, LANES), 0)
        head = lax.broadcasted_iota(jnp.int32, (SSD_WIDTH, LANES), 1)
        one_hot = jnp.where(lax.shift_right_logical(chan, SSD_HEAD_DIM.bit_length() - 1) == head, 1.0, 0.0)

        def head_sums(v):
            hi = v.astype(MXU_DTYPE)
            return _dot(hi, one_hot, NN) + _dot(v - hi.astype(F32), one_hot, NN)

        s2 = head_sums(r2_ref[...])
        d_a = d_a + head_sums(r12_ref[...]) + jnp.where(last_row, jnp.sum(s2, axis=0, keepdims=True), 0.0)
        ddt = head_sums(rq_ref[...])
        col = lax.broadcasted_iota(jnp.int32, (SSD_HEADS * L, LANES), 0)
        one_hot_e = jnp.where(lax.shift_right_logical(col, L.bit_length() - 1) == lax.broadcasted_iota(jnp.int32, (SSD_HEADS * L, LANES), 1), 1.0, 0.0)
        ev = re_ref[...]
        ehi = ev.astype(MXU_DTYPE)
        d_a = d_a + _dot(ehi, one_hot_e, NN) + _dot(ev - ehi.astype(F32), one_hot_e, NN)
        rc = _rev_cumsum_rows(d_a + d_at.T)
        d_dt = rc * aneg + ddt
        galog_ref[...] += jnp.sum(rc * dt, axis=0, keepdims=True) * aneg
        d_dtraw = d_dt * _sigmoid(dt_in)
        gdtb_ref[...] += jnp.sum(d_dtraw, axis=0, keepdims=True)
        ddt_ref[...] = d_dtraw.astype(ddt_ref.dtype)

        def dsilu(p, s):
            return s * (1.0 + p * (1.0 - s))

        dcx = dxs_ref[...] * dsilu(px, sx)
        dcb = jnp.concatenate(d_b_parts, axis=1) * dsilu(pb, sb)
        dcc = jnp.concatenate(d_c_parts, axis=1) * dsilu(pc, sc)
        drx, gwx, gbx = _conv_backward(dcx, nx_ref[...], xr, cwx)
        drb, gwb, gbb = _conv_backward(dcb, nb_ref[...], br, cwb)
        drc, gwc, gbc = _conv_backward(dcc, ncc_ref[...], cr, cwc)
        nx_ref[...] = dcx[0:8, :]
        nb_ref[...] = dcb[0:8, :]
        ncc_ref[...] = dcc[0:8, :]
        dssd_ref[:, SSD_WIDTH:2 * SSD_WIDTH] = drx.astype(dssd_ref.dtype)
        dssd_ref[:, 2 * SSD_WIDTH:2 * SSD_WIDTH + SSD_BC] = drb.astype(dssd_ref.dtype)
        dssd_ref[:, 2 * SSD_WIDTH + SSD_BC:SSD_SEG] = drc.astype(dssd_ref.dtype)
        for k in range(CONV_WIDTH):
            gcw_ref[k:k + 1, :] += jnp.concatenate([gwx[k], gwb[k], gwc[k]], axis=1)
        gcb_ref[...] += jnp.concatenate([gbx, gbb, gbc], axis=1)

    const = lambda shape: pl.BlockSpec(shape, lambda c: (0,) * len(shape))
    body, ex_in, ex_out, ex_shape, ex_sems = _riding(
        exchange, body, 18, 8, lambda: pl.program_id(0) == 0, lambda: pl.program_id(0) == nc - 1)
    res = pl.pallas_call(
        body, name="ssd_bwd", grid=(nc,),
        in_specs=[sp["z"], sp["xr"], sp["br"], sp["cr"], pre_spec(SSD_WIDTH, 0), pre_spec(SSD_BC, SSD_WIDTH // SSD_BC), pre_spec(SSD_BC, SSD_WIDTH // SSD_BC + 1),
                  sp["dt"], sp["wide"], sp["states"], sp["wide"],
                  sp["cwx"], sp["cwb"], sp["cwc"], sp["vec128"], sp["vec128"], sp["vecw"], sp["vecw"]] + ex_in,
        out_specs=[pl.BlockSpec((L, SSD_SEG), lambda c: (nc - 1 - c, 0)), sp["dt"],
                   const((CONV_WIDTH, SSD_CONV_CH)), const((1, SSD_CONV_CH)), const((1, LANES)), const((1, LANES)),
                   const((1, SSD_WIDTH)), const((1, SSD_WIDTH))] + ex_out,
        out_shape=[jax.ShapeDtypeStruct((t, SSD_SEG), MXU_DTYPE), jax.ShapeDtypeStruct((t, DT_PAD), MXU_DTYPE),
                   jax.ShapeDtypeStruct((CONV_WIDTH, SSD_CONV_CH), F32), jax.ShapeDtypeStruct((1, SSD_CONV_CH), F32),
                   jax.ShapeDtypeStruct((1, LANES), F32), jax.ShapeDtypeStruct((1, LANES), F32),
                   jax.ShapeDtypeStruct((1, SSD_WIDTH), F32), jax.ShapeDtypeStruct((1, SSD_WIDTH), F32)] + ex_shape,
        scratch_shapes=[pltpu.VMEM((SSD_GROUPS, GROUP_COLS, SSD_STATE), F32), pltpu.VMEM((8, SSD_WIDTH), F32),
                        pltpu.VMEM((8, SSD_BC), F32), pltpu.VMEM((8, SSD_BC), F32)] + [pltpu.VMEM((L, SSD_WIDTH), F32)] * 4 + [pltpu.VMEM((L, SSD_HEADS * L), F32)] + ex_sems,
        compiler_params=_params(("arbitrary",)),
    )(proj_ssd, proj_ssd, proj_ssd, proj_ssd, pre, pre, pre, dt_p, y, states, dyssd,
      conv_w, conv_w, conv_w, dtb, alog, d_row, ng_row, *(exchange["arrays"] if exchange else []))
    return res[:8], res[8:]


def _lru_gates(xl, wa_ref, wx_ref, ba, bx, lam):
    pre_a, pre_x = [], []
    for g in range(LRU_NGROUPS):
        xg = xl[:, g * LRU_GROUP:(g + 1) * LRU_GROUP]
        pre_a.append(_dot(xg, wa_ref[g], NN))
        pre_x.append(_dot(xg, wx_ref[g], NN))
    pa = jnp.concatenate(pre_a, axis=1) + ba
    tail = jnp.exp(jnp.minimum(pa, LRU_GATE_TAIL))
    r = jnp.where(pa < LRU_GATE_TAIL, tail * (1.0 - tail), _sigmoid(pa))
    i = _sigmoid(jnp.concatenate(pre_x, axis=1) + bx)
    log_a = (-LRU_C * r) * _softplus(-lam)
    a = jnp.exp(log_a)
    mult_sq = -jnp.tanh(log_a) * (a * a + 1.0)
    return r, i, mult_sq, a, jnp.sqrt(mult_sq)


def _scan_rows(p, u, carry, reverse):
    rows, w = p.shape
    groups = rows // 8
    p3, u3 = p.reshape(groups, 8, w), u.reshape(groups, 8, w)
    row = lax.broadcasted_iota(jnp.int32, (groups, 8, w), 1)
    for s in (1, 2, 4):
        ok = row < 8 - s if reverse else row >= s
        shift = 8 - s if reverse else s
        u3 = p3 * jnp.where(ok, pltpu.roll(u3, shift, 1), 0.0) + u3
        p3 = p3 * jnp.where(ok, pltpu.roll(p3, shift, 1), 1.0)
    out = [None] * groups
    for k in (range(groups - 1, -1, -1) if reverse else range(groups)):
        out[k] = p3[k] * carry + u3[k]
        carry = out[k][0:1, :] if reverse else out[k][7:8, :]
    return jnp.concatenate(out, axis=0), carry


def _lru_fwd(proj_lru, conv_w, conv_b, wa, wx, ba, bx, lam):
    t = proj_lru.shape[0]
    rows = min(LRU_ROWS, t)
    nb = t // rows
    W = LRU_WIDTH

    def body(lg_ref, lx_ref, cw_ref, cb_ref, wa_ref, wx_ref, ba_ref, bx_ref, lam_ref, ylru_ref, h_ref, xl_ref,
             halo_ref, carry_ref):
        @pl.when(pl.program_id(0) == 0)
        def _():
            halo_ref[...] = jnp.zeros_like(halo_ref)
            carry_ref[...] = jnp.zeros_like(carry_ref)

        lx = lx_ref[...]
        xl = _causal_conv(lx, halo_ref[...], cw_ref[...], cb_ref[...])
        halo_ref[...] = lx[rows - 8:rows, :]
        xl_ref[...] = xl
        _, i, _, a, mult = _lru_gates(xl, wa_ref, wx_ref, ba_ref[...], bx_ref[...], lam_ref[...])
        u = mult * (i * xl)
        h, carry_ref[...] = _scan_rows(a, u, carry_ref[...], False)
        h_ref[...] = h
        lg = lg_ref[...]
        ylru_ref[...] = (h * (lg * _sigmoid(lg))).astype(ylru_ref.dtype)

    const = lambda shape: pl.BlockSpec(shape, lambda b: (0,) * len(shape))
    return pl.pallas_call(
        body, name="lru_fwd", grid=(nb,),
        in_specs=[pl.BlockSpec((rows, W), lambda b: (b, 0)), pl.BlockSpec((rows, W), lambda b: (b, 1)),
                  const((CONV_WIDTH, W)), const((1, W)), const((LRU_NGROUPS, LRU_GROUP, LRU_GROUP)),
                  const((LRU_NGROUPS, LRU_GROUP, LRU_GROUP)), const((1, W)), const((1, W)), const((1, W))],
        out_specs=[pl.BlockSpec((rows, W), lambda b: (b, 0))] * 3,
        out_shape=[jax.ShapeDtypeStruct((t, W), MXU_DTYPE), jax.ShapeDtypeStruct((t, W), F32), jax.ShapeDtypeStruct((t, W), F32)],
        scratch_shapes=[pltpu.VMEM((8, W), F32), pltpu.VMEM((1, W), F32)],
        compiler_params=_params(("arbitrary",)),
    )(proj_lru, proj_lru, conv_w, conv_b, wa, wx, ba, bx, lam)


def _lru_bwd(proj_lru, xl, h, dylru, conv_w, wa, wx, ba, bx, lam, exchange=None):
    t = proj_lru.shape[0]
    rows = min(LRU_ROWS, t)
    nb = t // rows
    W = LRU_WIDTH
    groups8 = rows // 8

    def rev(b):
        return nb - 1 - b

    def halo_spec(col):
        return pl.BlockSpec((8, W), lambda b: (jnp.maximum(rev(b) * groups8 - 1, 0), col))

    def body(lg_ref, lx_ref, xl_ref, h_ref, hh_ref, dy_ref, cw_ref, wa_ref, wx_ref, ba_ref, bx_ref, lam_ref,
             dlru_ref, gcw_ref, gcb_ref, gba_ref, gbx_ref, glam_ref, gwa_ref, gwx_ref,
             gcarry_ref, afirst_ref, nxt_ref):
        step = pl.program_id(0)

        @pl.when(step == 0)
        def _():
            gcarry_ref[...] = jnp.zeros_like(gcarry_ref)
            afirst_ref[...] = jnp.zeros_like(afirst_ref)
            nxt_ref[...] = jnp.zeros_like(nxt_ref)
            for ref in (gcw_ref, gcb_ref, gba_ref, gbx_ref, glam_ref, gwa_ref, gwx_ref):
                ref[...] = jnp.zeros_like(ref)

        keep = jnp.where(step == nb - 1, 0.0, 1.0)
        lx = lx_ref[...]
        cw = cw_ref[...]
        xl = xl_ref[...]
        lam = lam_ref[...]
        r, i, mult_sq, a, mult = _lru_gates(xl, wa_ref, wx_ref, ba_ref[...], bx_ref[...], lam)
        hv = h_ref[...]
        h_prev = _shift_down(hv, hh_ref[...] * keep, 1)
        lg = lg_ref[...]
        sg = _sigmoid(lg)
        dyv = dy_ref[...]
        d_h = dyv * (lg * sg)
        dlru_ref[:, 0:W] = (dyv * hv * (sg * (1.0 + lg * (1.0 - sg)))).astype(dlru_ref.dtype)

        row = lax.broadcasted_iota(jnp.int32, (rows, W), 0)
        p = jnp.where(row < rows - 1, pltpu.roll(a, rows - 1, 0), afirst_ref[...])
        gsc, gcarry_ref[...] = _scan_rows(p, d_h, gcarry_ref[...], True)
        afirst_ref[...] = a[0:1, :]

        d_a = gsc * h_prev
        v = i * xl
        d_mult = gsc * v
        d_v = gsc * mult
        d_i = d_v * xl
        d_xl = d_v * i
        d_la = d_a * a - d_mult * (a * a) * lax.rsqrt(mult_sq)
        sp_neg = _softplus(-lam)
        d_r = d_la * (-LRU_C * sp_neg)
        glam_ref[...] += jnp.sum(d_la * r, axis=0, keepdims=True) * (LRU_C * _sigmoid(-lam))
        d_pa = d_r * r * (1.0 - r)
        d_px = d_i * i * (1.0 - i)
        gba_ref[...] += jnp.sum(d_pa, axis=0, keepdims=True)
        gbx_ref[...] += jnp.sum(d_px, axis=0, keepdims=True)
        parts = []
        for g in range(LRU_NGROUPS):
            cols = slice(g * LRU_GROUP, (g + 1) * LRU_GROUP)
            xg, dpa_g, dpx_g = xl[:, cols], d_pa[:, cols], d_px[:, cols]
            parts.append(_dot(dpa_g, wa_ref[g], NT) + _dot(dpx_g, wx_ref[g], NT))
            gwa_ref[g] += _dot(xg, dpa_g, TN)
            gwx_ref[g] += _dot(xg, dpx_g, TN)
        d_xl = d_xl + jnp.concatenate(parts, axis=1)
        d_lx, gw, gb = _conv_backward(d_xl, nxt_ref[...], lx, cw)
        nxt_ref[...] = d_xl[0:8, :]
        dlru_ref[:, W:2 * W] = d_lx.astype(dlru_ref.dtype)
        for k in range(CONV_WIDTH):
            gcw_ref[k:k + 1, :] += gw[k]
        gcb_ref[...] += gb

    const = lambda shape: pl.BlockSpec(shape, lambda b: (0,) * len(shape))
    wspec = const((LRU_NGROUPS, LRU_GROUP, LRU_GROUP))
    blk = lambda col: pl.BlockSpec((rows, W), lambda b: (rev(b), col))
    body, ex_in, ex_out, ex_shape, ex_sems = _riding(
        exchange, body, 12, 8, lambda: pl.program_id(0) == 0, lambda: pl.program_id(0) == nb - 1)
    res = pl.pallas_call(
        body, name="lru_bwd", grid=(nb,),
        in_specs=[blk(0), blk(1), blk(0), blk(0), halo_spec(0), blk(0),
                  const((CONV_WIDTH, W)), wspec, wspec, const((1, W)), const((1, W)), const((1, W))] + ex_in,
        out_specs=[pl.BlockSpec((rows, 2 * W), lambda b: (rev(b), 0)), const((CONV_WIDTH, W)), const((1, W)),
                   const((1, W)), const((1, W)), const((1, W)), wspec, wspec] + ex_out,
        out_shape=[jax.ShapeDtypeStruct((t, 2 * W), MXU_DTYPE), jax.ShapeDtypeStruct((CONV_WIDTH, W), F32),
                   jax.ShapeDtypeStruct((1, W), F32), jax.ShapeDtypeStruct((1, W), F32), jax.ShapeDtypeStruct((1, W), F32),
                   jax.ShapeDtypeStruct((1, W), F32), jax.ShapeDtypeStruct((LRU_NGROUPS, LRU_GROUP, LRU_GROUP), F32),
                   jax.ShapeDtypeStruct((LRU_NGROUPS, LRU_GROUP, LRU_GROUP), F32)] + ex_shape,
        scratch_shapes=[pltpu.VMEM((1, W), F32), pltpu.VMEM((1, W), F32), pltpu.VMEM((8, W), F32)] + ex_sems,
        compiler_params=_params(("arbitrary",)),
    )(proj_lru, proj_lru, xl, h, h, dylru, conv_w, wa, wx, ba, bx, lam, *(exchange["arrays"] if exchange else []))
    return res[:8], res[8:]


def _mem_scores(q_h, k_h):
    s = _dot(q_h, k_h, NT) * (MEM_HEAD_DIM ** -0.5)
    s = s - jnp.max(s, axis=-1, keepdims=True)
    e = jnp.exp(s)
    return e / jnp.sum(e, axis=-1, keepdims=True)


def _mem_fwd(q, kv, rows=512):
    t = q.shape[0]
    rows = min(rows, t)
    m = kv.shape[0]

    def body(q_ref, kv_ref, y_ref):
        for hd in range(MEM_HEADS):
            cols = slice(hd * MEM_HEAD_DIM, (hd + 1) * MEM_HEAD_DIM)
            vcols = slice(D_MODEL + hd * MEM_HEAD_DIM, D_MODEL + (hd + 1) * MEM_HEAD_DIM)
            p = _mem_scores(q_ref[:, cols], kv_ref[:, cols])
            y_ref[:, cols] = _dot(p, kv_ref[:, vcols], NN).astype(y_ref.dtype)

    return pl.pallas_call(
        body, name="mem_fwd", grid=(t // rows,),
        in_specs=[pl.BlockSpec((rows, D_MODEL), lambda i: (i, 0)), pl.BlockSpec((m, 2 * D_MODEL), lambda i: (0, 0))],
        out_specs=pl.BlockSpec((rows, D_MODEL), lambda i: (i, 0)),
        out_shape=jax.ShapeDtypeStruct((t, D_MODEL), MXU_DTYPE),
        compiler_params=_params(("parallel",)),
    )(q, kv)


def _mem_bwd(q, kv, dy, rows=512):
    t = q.shape[0]
    rows = min(rows, t)
    m = kv.shape[0]

    def body(q_ref, kv_ref, dy_ref, dq_ref, dkv_ref):
        @pl.when(pl.program_id(0) == 0)
        def _():
            dkv_ref[...] = jnp.zeros_like(dkv_ref)

        for hd in range(MEM_HEADS):
            cols = slice(hd * MEM_HEAD_DIM, (hd + 1) * MEM_HEAD_DIM)
            vcols = slice(D_MODEL + hd * MEM_HEAD_DIM, D_MODEL + (hd + 1) * MEM_HEAD_DIM)
            q_h, k_h, dy_h = q_ref[:, cols], kv_ref[:, cols], dy_ref[:, cols]
            p = _mem_scores(q_h, k_h)
            dp = _dot(dy_h, kv_ref[:, vcols], NT)
            dkv_ref[:, vcols] += _dot(p, dy_h, TN)
            ds = p * (dp - jnp.sum(dp * p, axis=-1, keepdims=True)) * (MEM_HEAD_DIM ** -0.5)
            dq_ref[:, cols] = _dot(ds, k_h, NN).astype(dq_ref.dtype)
            dkv_ref[:, cols] += _dot(ds, q_h, TN)

    return pl.pallas_call(
        body, name="mem_bwd", grid=(t // rows,),
        in_specs=[pl.BlockSpec((rows, D_MODEL), lambda i: (i, 0)), pl.BlockSpec((m, 2 * D_MODEL), lambda i: (0, 0)),
                  pl.BlockSpec((rows, D_MODEL), lambda i: (i, 0))],
        out_specs=[pl.BlockSpec((rows, D_MODEL), lambda i: (i, 0)), pl.BlockSpec((m, 2 * D_MODEL), lambda i: (0, 0))],
        out_shape=[jax.ShapeDtypeStruct((t, D_MODEL), MXU_DTYPE), jax.ShapeDtypeStruct((m, 2 * D_MODEL), F32)],
        compiler_params=_params(("arbitrary",)),
    )(q, kv, dy)


def _merge_fwd(x, yssd, ylru, ymem, gl, w_bs, w_bl, w_bm, w_out, fg, tgt, rows=256):
    t = x.shape[0]
    rows = min(rows, t)
    D = D_MODEL

    def body(x_ref, ys_ref, yl_ref, ym_ref, gl_ref, wbs_ref, wbl_ref, wbm_ref, wo_ref, fg_ref, tgt_ref,
             ps_ref, pl_ref, pm_ref, mg_ref, dx2_ref, loss_ref, gfg_ref):
        @pl.when(pl.program_id(0) == 0)
        def _():
            loss_ref[...] = jnp.zeros_like(loss_ref)
            gfg_ref[...] = jnp.zeros_like(gfg_ref)

        ps = _dot(ys_ref[...], wbs_ref[...], NN)
        pl_ = _dot(yl_ref[...], wbl_ref[...], NN)
        pm = _dot(ym_ref[...], wbm_ref[...], NN)
        ps_ref[...] = ps
        pl_ref[...] = pl_
        pm_ref[...] = pm
        merged = (_sigmoid(gl_ref[:, 0:D]) * ps + _sigmoid(gl_ref[:, D:2 * D]) * pl_) + _sigmoid(gl_ref[:, 2 * D:3 * D]) * pm
        mg_ref[...] = merged.astype(mg_ref.dtype)
        x2 = x_ref[...] + _dot(merged, wo_ref[...], NN)
        r2 = lax.rsqrt(jnp.mean(x2 * x2, axis=-1, keepdims=True) + EPS)
        xn = x2 * r2
        fg = fg_ref[...]
        diff = xn * fg - tgt_ref[...]
        tile_loss = 0.5 * jnp.sum(jnp.mean(diff * diff, axis=-1, keepdims=True), axis=0, keepdims=True)
        loss_ref[...] += jnp.broadcast_to(tile_loss, loss_ref.shape)
        d_out = diff * (1.0 / D)
        gfg_ref[...] += jnp.sum(d_out * xn, axis=0, keepdims=True)
        dxn = d_out * fg
        dx2_ref[...] = r2 * (dxn - xn * jnp.mean(dxn * xn, axis=-1, keepdims=True))

    row = lambda w: pl.BlockSpec((rows, w), lambda i: (i, 0))
    const = lambda shape: pl.BlockSpec(shape, lambda i: (0,) * len(shape))
    return pl.pallas_call(
        body, name="merge_fwd", grid=(t // rows,),
        in_specs=[row(D), row(SSD_WIDTH), row(LRU_WIDTH), row(D), row(3 * D), const((SSD_WIDTH, D)), const((LRU_WIDTH, D)),
                  const((D, D)), const((D, D)), const((1, D)), row(D)],
        out_specs=[row(D), row(D), row(D), row(D), row(D), const((1, LANES)), const((1, D))],
        out_shape=[jax.ShapeDtypeStruct((t, D), F32), jax.ShapeDtypeStruct((t, D), F32), jax.ShapeDtypeStruct((t, D), F32),
                   jax.ShapeDtypeStruct((t, D), MXU_DTYPE), jax.ShapeDtypeStruct((t, D), F32),
                   jax.ShapeDtypeStruct((1, LANES), F32), jax.ShapeDtypeStruct((1, D), F32)],
        compiler_params=_params(("arbitrary",)),
    )(x, yssd, ylru, ymem, gl, w_bs, w_bl, w_bm, w_out, fg, tgt)


def _merge_bwd(dx2, gl, ps, pl_in, pm, w_bs, w_bl, w_bm, w_out, rows=256):
    t = dx2.shape[0]
    rows = min(rows, t)
    D = D_MODEL

    def body(dx2_ref, gl_ref, ps_ref, pl_ref, pm_ref, wbs_ref, wbl_ref, wbm_ref, wo_ref,
             dg_ref, dps_ref, dpl_ref, dpm_ref, dys_ref, dyl_ref, dym_ref):
        dm = _dot(dx2_ref[...], wo_ref[...], NT)
        for idx, (p_ref, dp_ref, w_ref, dy_ref) in enumerate(
                ((ps_ref, dps_ref, wbs_ref, dys_ref), (pl_ref, dpl_ref, wbl_ref, dyl_ref), (pm_ref, dpm_ref, wbm_ref, dym_ref))):
            gate = _sigmoid(gl_ref[:, idx * D:(idx + 1) * D])
            dg_ref[:, idx * D:(idx + 1) * D] = ((dm * p_ref[...]) * gate * (1.0 - gate)).astype(dg_ref.dtype)
            dp = dm * gate
            dp_ref[...] = dp.astype(dp_ref.dtype)
            dy_ref[...] = _dot(dp, w_ref[...], NT)

    row = lambda w: pl.BlockSpec((rows, w), lambda i: (i, 0))
    const = lambda shape: pl.BlockSpec(shape, lambda i: (0,) * len(shape))
    return pl.pallas_call(
        body, name="merge_bwd", grid=(t // rows,),
        in_specs=[row(D), row(3 * D), row(D), row(D), row(D), const((SSD_WIDTH, D)), const((LRU_WIDTH, D)),
                  const((D, D)), const((D, D))],
        out_specs=[row(3 * D), row(D), row(D), row(D), row(SSD_WIDTH), row(LRU_WIDTH), row(D)],
        out_shape=[jax.ShapeDtypeStruct((t, 3 * D), MXU_DTYPE), jax.ShapeDtypeStruct((t, D), MXU_DTYPE),
                   jax.ShapeDtypeStruct((t, D), MXU_DTYPE), jax.ShapeDtypeStruct((t, D), MXU_DTYPE),
                   jax.ShapeDtypeStruct((t, SSD_WIDTH), F32), jax.ShapeDtypeStruct((t, LRU_WIDTH), F32),
                   jax.ShapeDtypeStruct((t, D), F32)],
        compiler_params=_params(("parallel",)),
    )(dx2, gl, ps, pl_in, pm, w_bs, w_bl, w_bm, w_out)


def _mesh_place():
    x, y, c = lax.axis_index("x"), lax.axis_index("y"), lax.axis_index("c")
    return x, y, c, 4 * x + 2 * y + c


def _other_chips(x, y):
    return [(1 - x, y), (x, 1 - y), (1 - x, 1 - y)]


def _all_gather_plan(arrs):
    n = len(arrs)

    def parts(ins, outs, send_sems, recv_sems, local_sems):
        x, y, c, me = _mesh_place()
        sibling = (x, y, 1 - c)
        chips = _other_chips(x, y)

        def slot(px, py, pc):
            return 4 * px + 2 * py + pc

        def copy(a, k, block, to, src=None):
            return pltpu.make_async_remote_copy(
                src_ref=outs[a].at[block] if src is None else src, dst_ref=outs[a].at[block],
                send_sem=send_sems.at[a, k], recv_sem=recv_sems.at[a, k], device_id=to, device_id_type=pl.DeviceIdType.MESH)

        def local():
            return [pltpu.make_async_copy(ins[a], outs[a].at[me], local_sems.at[a]) for a in range(n)]

        def first():
            return [copy(a, k, me, to, src=ins[a]) for a in range(n)
                    for k, to in enumerate([sibling] + [(*chip, c) for chip in chips])]

        return x, y, c, sibling, chips, slot, copy, local, first

    def start(ins, outs, *sems):
        *_, local, first = parts(ins, outs, *sems)
        for cp in local() + first():
            cp.start()

    def wait(ins, outs, *sems):
        x, y, c, sibling, chips, slot, copy, local, first = parts(ins, outs, *sems)
        sends = first()
        for j, chip in enumerate(chips):
            for a in range(n):
                copy(a, 1 + j, slot(*chip, c), sibling).wait_recv()
                passed = copy(a, 4 + j, slot(*chip, c), sibling)
                passed.start()
                sends.append(passed)
        for a in range(n):
            copy(a, 0, slot(x, y, 1 - c), sibling).wait_recv()
        for j, chip in enumerate(chips):
            for a in range(n):
                copy(a, 4 + j, slot(*chip, 1 - c), sibling).wait_recv()
        for cp in sends:
            cp.wait_send()
        for cp in local():
            cp.wait()

    return dict(arrays=list(arrs), out_shape=[jax.ShapeDtypeStruct((N_DEV,) + a.shape, a.dtype) for a in arrs],
                sems=[(n, 7), (n, 7), (n,)], start=start, wait=wait)


N_CHIPS = 4


def _pair_plan(parts):
    n = len(parts)

    def copies(ins, outs, send_sems, recv_sems):
        x, y, c, _ = _mesh_place()
        return [pltpu.make_async_remote_copy(src_ref=ins[a].at[q, 1 - c], dst_ref=outs[a].at[q], send_sem=send_sems.at[a, q],
                                             recv_sem=recv_sems.at[a, q], device_id=(x, y, 1 - c), device_id_type=pl.DeviceIdType.MESH)
                for a in range(n) for q in range(N_CHIPS)]

    def start(ins, outs, send_sems, recv_sems):
        for cp in copies(ins, outs, send_sems, recv_sems):
            cp.start()

    def wait(ins, outs, send_sems, recv_sems):
        cps = copies(ins, outs, send_sems, recv_sems)
        for cp in cps:
            cp.wait_recv()
        for cp in cps:
            cp.wait_send()

    return dict(arrays=list(parts), out_shape=[jax.ShapeDtypeStruct((N_CHIPS,) + a.shape[2:], a.dtype) for a in parts],
                sems=[(n, N_CHIPS), (n, N_CHIPS)], start=start, wait=wait)


def _chip_plan(sums):
    n = len(sums)

    def copies(ins, outs, send_sems, recv_sems, arriving):
        x, y, c, _ = _mesh_place()
        my_chip = 2 * x + y
        cps = []
        for a in range(n):
            for j, (px, py) in enumerate(_other_chips(x, y)):
                src, dst = (my_chip, 2 * px + py) if arriving else (2 * px + py, my_chip)
                cps.append(pltpu.make_async_remote_copy(
                    src_ref=ins[a].at[src], dst_ref=outs[a].at[dst], send_sem=send_sems.at[a, j], recv_sem=recv_sems.at[a, j],
                    device_id=(px, py, c), device_id_type=pl.DeviceIdType.MESH))
        return cps

    def start(ins, outs, send_sems, recv_sems):
        for cp in copies(ins, outs, send_sems, recv_sems, False):
            cp.start()

    def wait(ins, outs, send_sems, recv_sems):
        for cp in copies(ins, outs, send_sems, recv_sems, True):
            cp.wait_recv()
        for cp in copies(ins, outs, send_sems, recv_sems, False):
            cp.wait_send()

    return dict(arrays=list(sums), out_shape=[jax.ShapeDtypeStruct(a.shape, a.dtype) for a in sums],
                sems=[(n, 3), (n, 3)], start=start, wait=wait)


def _both(p1, p2):
    n1, s1 = len(p1["arrays"]), len(p1["sems"])

    def each(method):
        def run(ins, outs, *sems):
            p1[method](ins[:n1], outs[:n1], *sems[:s1])
            p2[method](ins[n1:], outs[n1:], *sems[s1:])
        return run

    return dict(arrays=p1["arrays"] + p2["arrays"], out_shape=p1["out_shape"] + p2["out_shape"],
                sems=p1["sems"] + p2["sems"], start=each("start"), wait=each("wait"))


def _riding(plan, body, n_in, n_out, first, last):
    if plan is None:
        return body, [], [], [], []
    ne = len(plan["arrays"])

    def wrapped(*refs):
        ins, ex_in = refs[:n_in], refs[n_in:n_in + ne]
        outs = refs[n_in + ne:n_in + ne + n_out]
        ex_out = refs[n_in + ne + n_out:n_in + 2 * ne + n_out]
        n_sems = len(plan["sems"])
        scratch, sems = refs[n_in + 2 * ne + n_out:-n_sems], refs[-n_sems:]

        @pl.when(first())
        def _():
            plan["start"](ex_in, ex_out, *sems)

        body(*ins, *outs, *scratch)

        @pl.when(last())
        def _():
            plan["wait"](ex_in, ex_out, *sems)

    any_spec = pl.BlockSpec(memory_space=pl.ANY)
    sems = [pltpu.SemaphoreType.DMA(shape) for shape in plan["sems"]]
    return wrapped, [any_spec] * ne, [any_spec] * ne, plan["out_shape"], sems


def _col_tile(r, c, limit_bytes):
    assert c % LANES == 0, c
    best = LANES
    for cand in range(LANES, c + 1, LANES):
        if c % cand == 0 and r * cand * 4 <= limit_bytes:
            best = cand
    return best


def _chip_sum(part, recv, core, name):
    _, _, r, c = part.shape
    ct = _col_tile(r, c, 2 << 20)

    def body(core_ref, p_ref, r_ref, s_ref, t_ref):
        s = p_ref[...] + r_ref[...]
        s_ref[...] = s
        t_ref[...] = s.astype(t_ref.dtype)

    blk = pl.BlockSpec((None, r, ct), lambda q, i, core_ref: (q, 0, i))
    return pl.pallas_call(
        body, name=name,
        grid_spec=pltpu.PrefetchScalarGridSpec(
            num_scalar_prefetch=1, grid=(N_CHIPS, c // ct),
            in_specs=[pl.BlockSpec((None, None, r, ct), lambda q, i, core_ref: (q, core_ref[0], 0, i)), blk],
            out_specs=[blk, blk]),
        out_shape=[jax.ShapeDtypeStruct((N_CHIPS, r, c), F32), jax.ShapeDtypeStruct((N_CHIPS, r, c), GRAD_WIRE_DTYPE)],
        compiler_params=_params(("parallel", "parallel")),
    )(core, part, recv)


def _adam_update(w, g, m, v):
    nm = ADAM_B1 * m + (1.0 - ADAM_B1) * g
    nv = ADAM_B2 * v + (1.0 - ADAM_B2) * (g * g)
    m_hat = nm / (1.0 - ADAM_B1 ** ADAM_STEP)
    v_hat = nv / (1.0 - ADAM_B2 ** ADAM_STEP)
    return -ADAM_LR * (m_hat / (jnp.sqrt(v_hat) + ADAM_EPS) + ADAM_WD * w), nm, nv


def _sum_adamw(own, recv, chip, w, m, v, name, exchange=None):
    _, r, c = own.shape
    ct = _col_tile(r, c, 1 << 20)

    def body(chip_ref, o_ref, r1_ref, r2_ref, r3_ref, w_ref, m_ref, v_ref, g_ref, d_ref, nm_ref, nv_ref):
        g = ((o_ref[...] + r1_ref[...].astype(F32)) + r2_ref[...].astype(F32)) + r3_ref[...].astype(F32)
        g_ref[...] = g
        d_ref[...], nm_ref[...], nv_ref[...] = _adam_update(w_ref[...], g, m_ref[...], v_ref[...])

    def slot(k):
        return pl.BlockSpec((None, r, ct), lambda i, chip_ref: ((chip_ref[0] + k) % N_CHIPS, 0, i))

    spec = pl.BlockSpec((r, ct), lambda i, chip_ref: (0, i))
    shape = jax.ShapeDtypeStruct((r, c), F32)
    n_tiles = c // ct
    body, ex_in, ex_out, ex_shape, ex_sems = _riding(
        exchange, body, 8, 4, lambda: pl.program_id(0) == 0, lambda: pl.program_id(0) == n_tiles - 1)
    res = pl.pallas_call(
        body, name=name,
        grid_spec=pltpu.PrefetchScalarGridSpec(
            num_scalar_prefetch=1, grid=(n_tiles,),
            in_specs=[slot(0), slot(1), slot(2), slot(3), spec, spec, spec] + ex_in, out_specs=[spec] * 4 + ex_out,
            scratch_shapes=ex_sems),
        out_shape=[shape] * 4 + ex_shape,
        compiler_params=_params(("arbitrary",)),
    )(chip, own, recv, recv, recv, w, m, v, *(exchange["arrays"] if exchange else []))
    return res[:4], res[4:]


def _small_adamw(parts, ws, ms, vs):
    n = len(parts)

    def body(*refs):
        p_refs, w_refs, m_refs, v_refs = refs[:n], refs[n:2 * n], refs[2 * n:3 * n], refs[3 * n:4 * n]
        outs = refs[4 * n:]
        for i in range(n):
            g = p_refs[i][0]
            for k in range(1, N_DEV):
                g = g + p_refs[i][k]
            outs[i][...] = g
            outs[n + i][...], outs[2 * n + i][...], outs[3 * n + i][...] = _adam_update(
                w_refs[i][...], g, m_refs[i][...], v_refs[i][...])

    vmem = pl.BlockSpec(memory_space=pltpu.VMEM)
    shapes = [jax.ShapeDtypeStruct(w.shape, F32) for w in ws]
    res = pl.pallas_call(
        body, name="adamw_small", in_specs=[vmem] * (4 * n), out_specs=[vmem] * (4 * n), out_shape=shapes * 4,
        compiler_params=pltpu.CompilerParams(vmem_limit_bytes=VMEM_LIMIT),
    )(*parts, *ws, *ms, *vs)
    return res[:n], res[n:2 * n], res[2 * n:3 * n], res[3 * n:]


def _pack(arrs, dtype, row_multiple):
    flat = jnp.concatenate([a.reshape(-1).astype(dtype) for a in arrs])
    unit = LANES * row_multiple
    padded = -(-flat.shape[0] // unit) * unit
    return jnp.pad(flat, (0, padded - flat.shape[0])).reshape(-1, LANES)


def _unpack(packed, shapes, lead=()):
    flat = packed.reshape(lead + (-1,))
    out, off = [], 0
    for shp in shapes:
        n = math.prod(shp)
        out.append(flat[..., off:off + n].reshape(lead + tuple(shp)))
        off += n
    return out


def _gather_cols(g, lo, hi):
    width = g.shape[2]
    pieces = []
    for s in range(N_DEV):
        a, e = max(lo, s * width), min(hi, (s + 1) * width)
        if a < e:
            pieces.append(g[s, :, a - s * width:e - s * width])
    return pieces[0] if len(pieces) == 1 else jnp.concatenate(pieces, axis=1)


def _scatter_cols(segs, width):
    slots = []
    for k in range(N_DEV):
        lo, hi = k * width, (k + 1) * width
        pieces = []
        for arr, s_lo, s_hi in segs:
            a, e = max(lo, s_lo), min(hi, s_hi)
            if a < e:
                pieces.append(arr[:, a - s_lo:e - s_lo])
        slots.append(pieces[0] if len(pieces) == 1 else jnp.concatenate(pieces, axis=1))
    return jnp.stack(slots)


def _block_diag_groups(w):
    w4 = w.reshape(LRU_NGROUPS, 4, LRU_BLOCK, LRU_BLOCK)
    eye = jnp.eye(4, dtype=w.dtype)
    return jnp.einsum("gaij,ab->gaibj", w4, eye).reshape(LRU_NGROUPS, LRU_GROUP, LRU_GROUP)


def _block_diag_extract(wg):
    w5 = wg.reshape(LRU_NGROUPS, 4, LRU_BLOCK, 4, LRU_BLOCK)
    idx = jnp.arange(4)
    return w5[:, idx, :, idx, :].transpose(1, 0, 2, 3).reshape(LRU_BLOCKS, LRU_BLOCK, LRU_BLOCK)


BIG = ("w_in", "w_kv", "w_br_ssd", "w_br_lru", "w_br_mem", "w_out")
SMALL_SHARDED = ("ssd_conv_w", "ssd_norm_g", "lru_conv_w")
REPLICATED = ("norm_g", "ssd_conv_b", "ssd_dt_bias", "ssd_a_log", "ssd_d", "lru_conv_b", "lru_w_a", "lru_b_a",
              "lru_w_x", "lru_b_x", "lru_lambda", "mem_norm_g", "final_g")
WEIGHTS = ("norm_g", "w_in", "ssd_conv_w", "ssd_conv_b", "ssd_dt_bias", "ssd_a_log", "ssd_d", "ssd_norm_g", "lru_conv_w",
           "lru_conv_b", "lru_w_a", "lru_b_a", "lru_w_x", "lru_b_x", "lru_lambda", "mem_norm_g", "w_kv", "w_br_ssd",
           "w_br_lru", "w_br_mem", "w_out", "final_g")


def kernel(x, mem, norm_g, w_in, ssd_conv_w, ssd_conv_b, ssd_dt_bias, ssd_a_log, ssd_d, ssd_norm_g, lru_conv_w, lru_conv_b, lru_w_a, lru_b_a, lru_w_x, lru_b_x, lru_lambda, mem_norm_g, w_kv, w_br_ssd, w_br_lru, w_br_mem, w_out, final_g, loss_target, m_norm_g, m_w_in, m_ssd_conv_w, m_ssd_conv_b, m_ssd_dt_bias, m_ssd_a_log, m_ssd_d, m_ssd_norm_g, m_lru_conv_w, m_lru_conv_b, m_lru_w_a, m_lru_b_a, m_lru_w_x, m_lru_b_x, m_lru_lambda, m_mem_norm_g, m_w_kv, m_w_br_ssd, m_w_br_lru, m_w_br_mem, m_w_out, m_final_g, v_norm_g, v_w_in, v_ssd_conv_w, v_ssd_conv_b, v_ssd_dt_bias, v_ssd_a_log, v_ssd_d, v_ssd_norm_g, v_lru_conv_w, v_lru_conv_b, v_lru_w_a, v_lru_b_a, v_lru_w_x, v_lru_b_x, v_lru_lambda, v_mem_norm_g, v_w_kv, v_w_br_ssd, v_w_br_lru, v_w_br_mem, v_w_out, v_final_g):
    env = dict(locals())
    W = {n: env[n] for n in WEIGHTS}
    M = {n: env["m_" + n] for n in WEIGHTS}
    V = {n: env["v_" + n] for n in WEIGHTS}
    me = 4 * lax.axis_index("x") + 2 * lax.axis_index("y") + lax.axis_index("c")
    t = x.shape[1]
    xt = x[0]
    memt = mem[0]
    tgt = loss_target[0]

    small_shapes = [W[n].shape for n in SMALL_SHARDED]
    as2d = lambda d, n: jnp.transpose(d[n][0]) if n == "w_in" else d[n][0]
    h, (g_in,) = _rms_fwd(xt, norm_g, "norm_fwd", exchange=_all_gather_plan([as2d(W, "w_in").astype(MXU_DTYPE)]))
    b = SEG_BOUNDS
    w_in_t = g_in.reshape(IN_WIDTH, D_MODEL)
    w_ssd, w_lru, w_q, w_g = w_in_t[b[0]:b[1]], w_in_t[b[2]:b[3]], w_in_t[b[3]:b[4]], w_in_t[b[4]:b[5]]
    w_dt = jnp.pad(w_in_t[b[1]:b[2]], ((0, DT_PAD - SSD_HEADS), (0, 0)))

    later = _all_gather_plan([as2d(W, n).astype(MXU_DTYPE) for n in BIG[1:]] + [_pack([W[n] for n in SMALL_SHARDED], F32, 8)])
    proj_ssd, (g_kv, g_bs, g_bl, g_bm, g_out, gs) = _matmul(h, w_ssd, "nt", "proj_ssd", tm=4096, tn=512, exchange=later)
    g_cw, g_ng, g_lcw = _unpack(gs, small_shapes, (N_DEV,))
    cols = lambda a: jnp.moveaxis(a[:, 0], 0, -2).reshape(a.shape[2:-1] + (-1,))
    rows_ = lambda a: a.reshape((-1,) + a.shape[2:])
    w_bs_f, w_bl_f, w_bm_f, w_out_f = rows_(g_bs), rows_(g_bl), rows_(g_bm), rows_(g_out)
    conv_w_f, ssd_ng_f, lru_cw_f = cols(g_cw), cols(g_ng), cols(g_lcw)
    w_kv_f = _gather_cols(g_kv, 0, 2 * D_MODEL)

    pad_heads = lambda a: jnp.pad(a, ((0, 0), (0, LANES - SSD_HEADS)))
    dtb, alog = pad_heads(ssd_dt_bias), pad_heads(ssd_a_log)
    d_row = jnp.repeat(ssd_d, SSD_HEAD_DIM, axis=1)
    ng_row = ssd_ng_f.reshape(1, SSD_WIDTH)
    wa_g, wx_g = _block_diag_groups(lru_w_a[0]), _block_diag_groups(lru_w_x[0])
    ba, bx = lru_b_a.reshape(1, LRU_WIDTH), lru_b_x.reshape(1, LRU_WIDTH)
    fg = final_g.reshape(1, D_MODEL)

    proj_lru = _matmul(h, w_lru, "nt", "proj_lru", tm=4096, tn=512)
    proj_q = _matmul(h, w_q, "nt", "proj_q", tm=4096, tn=512)
    proj_g = _matmul(h, w_g, "nt", "proj_g", tm=4096, tn=512)
    proj_dt = _matmul(h, w_dt, "nt", "proj_dt", tm=4096)
    mem_n = _rms_fwd(memt, mem_norm_g, "mem_norm_fwd")
    kv = _matmul(mem_n, w_kv_f, "nn", "mem_kv")
    yssd, y_scan, states, ssd_pre = _ssd_fwd(proj_ssd, proj_dt, conv_w_f, ssd_conv_b, dtb, alog, d_row, ng_row)
    ylru, h_lru, xl_lru = _lru_fwd(proj_lru, lru_cw_f, lru_conv_b, wa_g, wx_g, ba, bx, lru_lambda)
    ymem = _mem_fwd(proj_q, kv)
    ps, pl_, pm, merged, dx2, loss_vec, g_fg = _merge_fwd(xt, yssd, ylru, ymem, proj_g, w_bs_f, w_bl_f, w_bm_f, w_out_f, fg, tgt)

    d_g, dps, dpl, dpm, dyssd, dylru, dymem = _merge_bwd(dx2, proj_g, ps, pl_, pm, w_bs_f, w_bl_f, w_bm_f, w_out_f)
    gw_out = _matmul(merged, dx2, "tn", "grad_w_out", tk=2048)
    gw_bs = _matmul(yssd, dps, "tn", "grad_w_br_ssd", tm=2048)
    gw_bl = _matmul(ylru, dpl, "tn", "grad_w_br_lru", tm=LRU_WIDTH, tk=2048)
    gw_bm = _matmul(ymem, dpm, "tn", "grad_w_br_mem", tk=2048)
    d_q, d_kv = _mem_bwd(proj_q, kv, dymem)
    gw_kv = _matmul(mem_n, d_kv, "tn", "grad_w_kv")
    d_memn = _matmul(d_kv, w_kv_f, "nt", "d_mem_n")
    _, g_memng = _rms_bwd(memt, d_memn, None, mem_norm_g, "mem_norm_bwd")

    core = lax.axis_index("c").astype(jnp.int32).reshape(1)
    chip = (2 * lax.axis_index("x") + lax.axis_index("y")).astype(jnp.int32).reshape(1)
    by_chip = lambda a: a.reshape((N_CHIPS, 2, -1) + a.shape[1:])
    early = ("w_kv", "w_br_ssd", "w_br_lru", "w_br_mem", "w_out")
    early_parts = [by_chip(_scatter_cols([(gw_kv, 0, 2 * D_MODEL)], 2 * D_MODEL // N_DEV).reshape(-1, 2 * D_MODEL // N_DEV)),
                   by_chip(gw_bs), by_chip(gw_bl), by_chip(gw_bm), by_chip(gw_out)]
    (d_lru, gl_cw, gl_cb, g_ba, g_bx, g_lam, gwa_g, gwx_g), early_sib = _lru_bwd(
        proj_lru, xl_lru, h_lru, dylru, lru_cw_f, wa_g, wx_g, ba, bx, lru_lambda, exchange=_pair_plan(early_parts))
    early_sums = [_chip_sum(p, r, core, "chip_sum_" + n) for n, p, r in zip(early, early_parts, early_sib)]
    (d_ssd, d_dt, gs_cw, gs_cb, g_dtb, g_alog, g_dch, g_ngrow), early_recv = _ssd_bwd(
        proj_ssd, ssd_pre, proj_dt, y_scan, states, dyssd, conv_w_f, dtb, alog, d_row, ng_row,
        exchange=_chip_plan([s16 for _, s16 in early_sums]))
    gw_ssd = _matmul(d_ssd, h, "tn", "grad_w_in_ssd", tm=2560)
    gw_lru = _matmul(d_lru, h, "tn", "grad_w_in_lru", tm=1536, tk=2048)
    gw_q = _matmul(d_q, h, "tn", "grad_w_in_q", tk=2048)
    gw_g = _matmul(d_g, h, "tn", "grad_w_in_g", tm=1536, tk=2048)
    gw_dt = _matmul(d_dt, h, "tn", "grad_w_in_dt", tk=2048)
    in_part = by_chip(jnp.concatenate([gw_ssd, gw_dt[:SSD_HEADS], gw_lru, gw_q, gw_g], axis=0))

    small_grads = {
        "ssd_conv_w": gs_cw, "ssd_conv_b": gs_cb, "ssd_dt_bias": g_dtb[:, :SSD_HEADS],
        "ssd_a_log": g_alog[:, :SSD_HEADS], "ssd_d": jnp.sum(g_dch.reshape(SSD_HEADS, SSD_HEAD_DIM), axis=1).reshape(1, SSD_HEADS),
        "ssd_norm_g": g_ngrow.reshape(SSD_GROUPS, -1), "lru_conv_w": gl_cw, "lru_conv_b": gl_cb,
        "lru_w_a": _block_diag_extract(gwa_g), "lru_b_a": g_ba, "lru_w_x": _block_diag_extract(gwx_g), "lru_b_x": g_bx,
        "lru_lambda": g_lam, "mem_norm_g": g_memng, "final_g": g_fg,
    }
    small_all = REPLICATED + SMALL_SHARDED

    def small_shape(n, shards):
        shp = W[n].shape[1:] if W[n].ndim > 2 else (1, W[n].shape[-1])
        return shp[:-1] + (shp[-1] * shards,)

    riders = tuple(small_grads)
    small_plan = _all_gather_plan([small_grads[n].reshape(small_shape(n, N_DEV if n in SMALL_SHARDED else 1)) for n in riders])
    dh_segs = [(d_ssd, w_ssd), (d_lru, w_lru), (d_q, w_q), (d_g, w_g), (d_dt, w_dt)]
    dh, landed_a = _dh(dh_segs, 0, exchange=_both(_pair_plan([in_part]), small_plan))
    in_sum = _chip_sum(in_part, landed_a[0], core, "chip_sum_w_in")
    dh, landed_b = _dh(dh_segs, 1, exchange=_chip_plan([in_sum[1]]), into=dh)
    landed = list(landed_b) + list(landed_a[1:])
    grad_x, g_normg = _rms_bwd(xt, dh, dx2, norm_g, "norm_bwd")
    small_recv = dict(zip(riders, landed[1:]))
    reduced = {"w_in": (in_sum[0], landed[0]), **{n: (s[0], r) for n, s, r in zip(early, early_sums, early_recv)}}

    grads, delta, new_m, new_v = {}, {}, {}, {}
    for n in BIG:
        s32, recv = reduced[n]
        res, landed = _sum_adamw(s32, recv, chip, as2d(W, n), as2d(M, n), as2d(V, n), "adamw_" + n,
                                 exchange=_all_gather_plan([g_normg]) if n == "w_in" else None)
        if n == "w_in":
            small_recv["norm_g"] = landed[0]
        for dst, a in zip((grads, delta, new_m, new_v), res):
            dst[n] = (jnp.transpose(a) if n == "w_in" else a)[None]

    parts = []
    for n in small_all:
        a = small_recv[n]
        if n in SMALL_SHARDED:
            width = W[n].shape[-1]
            a = lax.dynamic_slice_in_dim(a, me * width, width, axis=a.ndim - 1)
        parts.append(a)
    canon = lambda d: [d[n].reshape(small_shape(n, 1)) for n in small_all]
    for dst, res in zip((grads, delta, new_m, new_v), _small_adamw(parts, canon(W), canon(M), canon(V))):
        for n, a in zip(small_all, res):
            dst[n] = a.reshape(W[n].shape)

    loss = lax.psum(loss_vec[0, 0], ("x", "y", "c"))
    return (loss, grad_x[None], *[grads[n] for n in WEIGHTS], *[delta[n] for n in WEIGHTS],
            *[new_m[n] for n in WEIGHTS], *[new_v[n] for n in WEIGHTS])
```
